```python
import math
import jax, jax.numpy as jnp
from jax import lax
import numpy as np

D_MODEL = 1024
BATCH = 8
SEQ = 8192
DEPTH = 2

MEM_LEN = 256
HEAD_DIM = 64
N_MIX_HEADS = D_MODEL // HEAD_DIM
N_MEM_HEADS = 4
N_TOK_HEADS = N_MIX_HEADS - N_MEM_HEADS
TOK_WIDTH = N_TOK_HEADS * HEAD_DIM
MEM_WIDTH = N_MEM_HEADS * HEAD_DIM
MIX_WIDTH = TOK_WIDTH + MEM_WIDTH
Q_LORA = 384
KV_LORA = 256
QK_NOPE = 64
QK_ROPE = 32
V_DIM = HEAD_DIM
QK_DIM = QK_NOPE + QK_ROPE
ROPE_THETA = 10000.0
Q_BLOCK = 128
CONV_W = 4
LRU_C = 8.0
N_LRU_BLOCKS = N_TOK_HEADS
LRU_BLOCK = TOK_WIDTH // N_LRU_BLOCKS
ALPHA = (2.0 * DEPTH) ** 0.25
BETA = (8.0 * DEPTH) ** -0.25
NORM_EPS = 1e-6
N_MLA = (DEPTH + 1) // 2
N_LRU = DEPTH // 2
MLA_IN = Q_LORA + KV_LORA + QK_ROPE + MIX_WIDTH + MEM_WIDTH
LRU_IN = TOK_WIDTH + MIX_WIDTH + MEM_WIDTH

kernel_name = "hybrid_mla_rglru_memory_deepnorm"


def _split(t, sizes):
    idx = np.cumsum(sizes)[:-1].tolist()
    return jnp.split(t, idx, axis=-1)


def rms_norm(t, g):
    t32 = t.astype(jnp.float32)
    t32 = t32 * lax.rsqrt(jnp.mean(t32 * t32, axis=-1, keepdims=True) + NORM_EPS)
    return (t32 * g.astype(jnp.float32)).astype(t.dtype)


def layer_norm(t, g, b):
    t32 = t.astype(jnp.float32)
    mu = jnp.mean(t32, axis=-1, keepdims=True)
    var = jnp.mean(jnp.square(t32 - mu), axis=-1, keepdims=True)
    y = (t32 - mu) * lax.rsqrt(var + NORM_EPS)
    return (y * g.astype(jnp.float32) + b.astype(jnp.float32)).astype(t.dtype)


def apply_rope(t, positions):
    half = t.shape[-1] // 2
    inv_freq = ROPE_THETA ** (-jnp.arange(half, dtype=jnp.float32) / half)
    ang = positions.astype(jnp.float32)[..., None] * inv_freq
    cos = jnp.cos(ang)[:, :, None, :].astype(t.dtype)
    sin = jnp.sin(ang)[:, :, None, :].astype(t.dtype)
    t1, t2 = t[..., :half], t[..., half:]
    return jnp.concatenate([t1 * cos - t2 * sin, t1 * sin + t2 * cos], axis=-1)


def causal_attention(q, k, v):
    b, s, h, d = q.shape
    nb = s // Q_BLOCK
    scale = 1.0 / math.sqrt(d)
    qb = q.reshape(b, nb, Q_BLOCK, h, d).transpose(1, 0, 2, 3, 4)
    k_pos = jnp.arange(s)

    def one_block(args):
        q_blk, blk = args
        sc = jnp.einsum('bqhd,bkhd->bhqk', q_blk, k,
                        preferred_element_type=jnp.float32) * scale
        q_pos = blk * Q_BLOCK + jnp.arange(Q_BLOCK)
        mask = k_pos[None, :] <= q_pos[:, None]
        sc = jnp.where(mask[None, None], sc, -jnp.inf)
        p = jax.nn.softmax(sc, axis=-1).astype(v.dtype)
        return jnp.einsum('bhqk,bkhd->bqhd', p, v)

    out = lax.map(one_block, (qb, jnp.arange(nb)))
    return out.transpose(1, 0, 2, 3, 4).reshape(b, s, h, v.shape[-1])


def memory_attention(q, mem_k, mem_v):
    sc = jnp.einsum('bshd,bmhd->bhsm', q, mem_k,
                    preferred_element_type=jnp.float32) / math.sqrt(HEAD_DIM)
    p = jax.nn.softmax(sc, axis=-1).astype(mem_v.dtype)
    return jnp.einsum('bhsm,bmhd->bshd', p, mem_v)


def _lin_rec_combine(left, right):
    a1, b1 = left
    a2, b2 = right
    return a1 * a2, a2 * b1 + b2


def rg_lru_branch(u, conv_w, conv_b, w_r, b_r, w_i, b_i, lam):
    b, s, w = u.shape
    u_pad = jnp.pad(u, ((0, 0), (CONV_W - 1, 0), (0, 0)))
    xc = conv_b + u_pad[:, 0:s] * conv_w[0]
    for tap in range(1, CONV_W):
        xc = xc + u_pad[:, tap:tap + s] * conv_w[tap]
    xb = xc.reshape(b, s, N_LRU_BLOCKS, LRU_BLOCK)
    r = jax.nn.sigmoid(jnp.einsum('bsgi,gij->bsgj', xb, w_r).reshape(b, s, w) + b_r)
    i = jax.nn.sigmoid(jnp.einsum('bsgi,gij->bsgj', xb, w_i).reshape(b, s, w) + b_i)
    log_a = (-LRU_C * jax.nn.softplus(-lam.astype(jnp.float32))) * r.astype(jnp.float32)
    a = jnp.exp(log_a)
    gated_x = jnp.sqrt(-jnp.expm1(2.0 * log_a)) * (i * xc).astype(jnp.float32)
    _, hs = lax.associative_scan(_lin_rec_combine, (a, gated_x), axis=1)
    return hs.astype(u.dtype)


def _fwd_setup_inputs(seed: int = 0) -> dict:
    key = jax.random.key(seed)
    ks = jax.random.split(key, 24)
    f32 = jnp.float32
    nrm = lambda k, shape, s: jax.random.normal(k, shape, f32) * s
    x = nrm(ks[0], (BATCH, SEQ, D_MODEL), 1.0)
    mem = nrm(ks[1], (BATCH, MEM_LEN, D_MODEL), 1.0)
    offset = jax.random.randint(ks[2], (BATCH, 1), 0, 4096, dtype=jnp.int32)
    positions = (offset + jnp.arange(SEQ, dtype=jnp.int32)[None, :]).astype(jnp.int32)
    mla_w_in = nrm(ks[3], (N_MLA, D_MODEL, MLA_IN), D_MODEL ** -0.5)
    mla_q_norm = 1.0 + nrm(ks[4], (N_MLA, Q_LORA), 0.01)
    mla_w_uq = nrm(ks[5], (N_MLA, Q_LORA, N_TOK_HEADS * QK_DIM), Q_LORA ** -0.5)
    mla_kv_norm = 1.0 + nrm(ks[6], (N_MLA, KV_LORA), 0.01)
    mla_w_ukv = nrm(ks[7], (N_MLA, KV_LORA, N_TOK_HEADS * (QK_NOPE + V_DIM)), KV_LORA ** -0.5)
    lru_w_in = nrm(ks[8], (N_LRU, D_MODEL, LRU_IN), D_MODEL ** -0.5)
    lru_conv_w = nrm(ks[9], (N_LRU, CONV_W, TOK_WIDTH), CONV_W ** -0.5)
    lru_conv_b = nrm(ks[10], (N_LRU, TOK_WIDTH), 0.01)
    lru_w_rgate = nrm(ks[11], (N_LRU, N_LRU_BLOCKS, LRU_BLOCK, LRU_BLOCK), LRU_BLOCK ** -0.5)
    lru_b_rgate = nrm(ks[12], (N_LRU, TOK_WIDTH), 0.01)
    lru_w_igate = nrm(ks[13], (N_LRU, N_LRU_BLOCKS, LRU_BLOCK, LRU_BLOCK), LRU_BLOCK ** -0.5)
    lru_b_igate = nrm(ks[14], (N_LRU, TOK_WIDTH), 0.01)
    a_c = jax.random.uniform(ks[15], (N_LRU, TOK_WIDTH), f32, 0.9, 0.999)
    a0 = a_c ** (1.0 / LRU_C)
    lru_lambda = jnp.log(a0) - jnp.log1p(-a0)
    w_mem_kv = nrm(ks[16], (DEPTH, D_MODEL, 2 * MEM_WIDTH), D_MODEL ** -0.5)
    w_out = nrm(ks[17], (DEPTH, MIX_WIDTH, D_MODEL), BETA * MIX_WIDTH ** -0.5)
    ln_g = 1.0 + nrm(ks[18], (DEPTH, D_MODEL), 0.01)
    ln_b = nrm(ks[19], (DEPTH, D_MODEL), 0.01)
    return {"x": x, "mem": mem, "positions": positions,
            "mla_w_in": mla_w_in, "mla_q_norm": mla_q_norm, "mla_w_uq": mla_w_uq,
            "mla_kv_norm": mla_kv_norm, "mla_w_ukv": mla_w_ukv,
            "lru_w_in": lru_w_in, "lru_conv_w": lru_conv_w, "lru_conv_b": lru_conv_b,
            "lru_w_rgate": lru_w_rgate, "lru_b_rgate": lru_b_rgate,
            "lru_w_igate": lru_w_igate, "lru_b_igate": lru_b_igate, "lru_lambda": lru_lambda,
            "w_mem_kv": w_mem_kv, "w_out": w_out, "ln_g": ln_g, "ln_b": ln_b}


def _fwd_reference(x, mem, positions, mla_w_in, mla_q_norm, mla_w_uq, mla_kv_norm, mla_w_ukv,
              lru_w_in, lru_conv_w, lru_conv_b, lru_w_rgate, lru_b_rgate,
              lru_w_igate, lru_b_igate, lru_lambda, w_mem_kv, w_out, ln_g, ln_b):
    b, s, _ = x.shape
    h = x
    for layer in range(DEPTH):
        j = layer // 2
        if layer % 2 == 0:
            z = h @ mla_w_in[j]
            c_q, c_kv, k_r, gate, q_mem = _split(
                z, [Q_LORA, KV_LORA, QK_ROPE, MIX_WIDTH, MEM_WIDTH])
            q = (rms_norm(c_q, mla_q_norm[j]) @ mla_w_uq[j]).reshape(b, s, N_TOK_HEADS, QK_DIM)
            q = jnp.concatenate([q[..., :QK_NOPE], apply_rope(q[..., QK_NOPE:], positions)], axis=-1)
            kv = (rms_norm(c_kv, mla_kv_norm[j]) @ mla_w_ukv[j]).reshape(
                b, s, N_TOK_HEADS, QK_NOPE + V_DIM)
            k_nope, v = kv[..., :QK_NOPE], kv[..., QK_NOPE:]
            k_rope = apply_rope(k_r[:, :, None, :], positions)
            k = jnp.concatenate(
                [k_nope, jnp.broadcast_to(k_rope, (b, s, N_TOK_HEADS, QK_ROPE))], axis=-1)
            tok = causal_attention(q, k, v).reshape(b, s, TOK_WIDTH)
        else:
            z = h @ lru_w_in[j]
            u, gate, q_mem = _split(z, [TOK_WIDTH, MIX_WIDTH, MEM_WIDTH])
            tok = rg_lru_branch(u, lru_conv_w[j], lru_conv_b[j], lru_w_rgate[j], lru_b_rgate[j],
                                lru_w_igate[j], lru_b_igate[j], lru_lambda[j])
        mem_kv = (mem @ w_mem_kv[layer]).reshape(b, MEM_LEN, 2, N_MEM_HEADS, HEAD_DIM)
        mem_out = memory_attention(q_mem.reshape(b, s, N_MEM_HEADS, HEAD_DIM),
                                   mem_kv[:, :, 0], mem_kv[:, :, 1]).reshape(b, s, MEM_WIDTH)
        y = jnp.concatenate([tok, mem_out], axis=-1) * jax.nn.silu(gate)
        h = layer_norm(ALPHA * h + y @ w_out[layer], ln_g[layer], ln_b[layer])
    return h


import jax as _jax
import jax.numpy as _jnp

TWIN_FORMAT = 'train_step'
FWD_PARAMS = ['x', 'mem', 'positions', 'mla_w_in', 'mla_q_norm', 'mla_w_uq', 'mla_kv_norm', 'mla_w_ukv', 'lru_w_in', 'lru_conv_w', 'lru_conv_b', 'lru_w_rgate', 'lru_b_rgate', 'lru_w_igate', 'lru_b_igate', 'lru_lambda', 'w_mem_kv', 'w_out', 'ln_g', 'ln_b']
TWIN_WEIGHTS = ['mla_w_in', 'mla_q_norm', 'mla_w_uq', 'mla_kv_norm', 'mla_w_ukv', 'lru_w_in', 'lru_conv_w', 'lru_conv_b', 'lru_w_rgate', 'lru_b_rgate', 'lru_w_igate', 'lru_b_igate', 'lru_lambda', 'w_mem_kv', 'w_out', 'ln_g', 'ln_b']
TWIN_DIFF_INPUT = 'x'
TWIN_INPUTS = ['x', 'mem', 'positions', 'mla_w_in', 'mla_q_norm', 'mla_w_uq', 'mla_kv_norm', 'mla_w_ukv', 'lru_w_in', 'lru_conv_w', 'lru_conv_b', 'lru_w_rgate', 'lru_b_rgate', 'lru_w_igate', 'lru_b_igate', 'lru_lambda', 'w_mem_kv', 'w_out', 'ln_g', 'ln_b', 'loss_target', 'm_mla_w_in', 'm_mla_q_norm', 'm_mla_w_uq', 'm_mla_kv_norm', 'm_mla_w_ukv', 'm_lru_w_in', 'm_lru_conv_w', 'm_lru_conv_b', 'm_lru_w_rgate', 'm_lru_b_rgate', 'm_lru_w_igate', 'm_lru_b_igate', 'm_lru_lambda', 'm_w_mem_kv', 'm_w_out', 'm_ln_g', 'm_ln_b', 'v_mla_w_in', 'v_mla_q_norm', 'v_mla_w_uq', 'v_mla_kv_norm', 'v_mla_w_ukv', 'v_lru_w_in', 'v_lru_conv_w', 'v_lru_conv_b', 'v_lru_w_rgate', 'v_lru_b_rgate', 'v_lru_w_igate', 'v_lru_b_igate', 'v_lru_lambda', 'v_w_mem_kv', 'v_w_out', 'v_ln_g', 'v_ln_b']
TWIN_OUTPUTS = ['loss', 'grad_x', 'grad_mla_w_in', 'grad_mla_q_norm', 'grad_mla_w_uq', 'grad_mla_kv_norm', 'grad_mla_w_ukv', 'grad_lru_w_in', 'grad_lru_conv_w', 'grad_lru_conv_b', 'grad_lru_w_rgate', 'grad_lru_b_rgate', 'grad_lru_w_igate', 'grad_lru_b_igate', 'grad_lru_lambda', 'grad_w_mem_kv', 'grad_w_out', 'grad_ln_g', 'grad_ln_b', 'delta_mla_w_in', 'delta_mla_q_norm', 'delta_mla_w_uq', 'delta_mla_kv_norm', 'delta_mla_w_ukv', 'delta_lru_w_in', 'delta_lru_conv_w', 'delta_lru_conv_b', 'delta_lru_w_rgate', 'delta_lru_b_rgate', 'delta_lru_w_igate', 'delta_lru_b_igate', 'delta_lru_lambda', 'delta_w_mem_kv', 'delta_w_out', 'delta_ln_g', 'delta_ln_b', 'new_m_mla_w_in', 'new_m_mla_q_norm', 'new_m_mla_w_uq', 'new_m_mla_kv_norm', 'new_m_mla_w_ukv', 'new_m_lru_w_in', 'new_m_lru_conv_w', 'new_m_lru_conv_b', 'new_m_lru_w_rgate', 'new_m_lru_b_rgate', 'new_m_lru_w_igate', 'new_m_lru_b_igate', 'new_m_lru_lambda', 'new_m_w_mem_kv', 'new_m_w_out', 'new_m_ln_g', 'new_m_ln_b', 'new_v_mla_w_in', 'new_v_mla_q_norm', 'new_v_mla_w_uq', 'new_v_mla_kv_norm', 'new_v_mla_w_ukv', 'new_v_lru_w_in', 'new_v_lru_conv_w', 'new_v_lru_conv_b', 'new_v_lru_w_rgate', 'new_v_lru_b_rgate', 'new_v_lru_w_igate', 'new_v_lru_b_igate', 'new_v_lru_lambda', 'new_v_w_mem_kv', 'new_v_w_out', 'new_v_ln_g', 'new_v_ln_b']
TWIN_LEAF_KINDS = {'loss': 'loss', 'grad_x': 'grad_x', 'grad_mla_w_in': 'grad_w', 'grad_mla_q_norm': 'grad_w', 'grad_mla_w_uq': 'grad_w', 'grad_mla_kv_norm': 'grad_w', 'grad_mla_w_ukv': 'grad_w', 'grad_lru_w_in': 'grad_w', 'grad_lru_conv_w': 'grad_w', 'grad_lru_conv_b': 'grad_w', 'grad_lru_w_rgate': 'grad_w', 'grad_lru_b_rgate': 'grad_w', 'grad_lru_w_igate': 'grad_w', 'grad_lru_b_igate': 'grad_w', 'grad_lru_lambda': 'grad_w', 'grad_w_mem_kv': 'grad_w', 'grad_w_out': 'grad_w', 'grad_ln_g': 'grad_w', 'grad_ln_b': 'grad_w', 'delta_mla_w_in': 'delta_w', 'delta_mla_q_norm': 'delta_w', 'delta_mla_w_uq': 'delta_w', 'delta_mla_kv_norm': 'delta_w', 'delta_mla_w_ukv': 'delta_w', 'delta_lru_w_in': 'delta_w', 'delta_lru_conv_w': 'delta_w', 'delta_lru_conv_b': 'delta_w', 'delta_lru_w_rgate': 'delta_w', 'delta_lru_b_rgate': 'delta_w', 'delta_lru_w_igate': 'delta_w', 'delta_lru_b_igate': 'delta_w', 'delta_lru_lambda': 'delta_w', 'delta_w_mem_kv': 'delta_w', 'delta_w_out': 'delta_w', 'delta_ln_g': 'delta_w', 'delta_ln_b': 'delta_w', 'new_m_mla_w_in': 'new_m', 'new_m_mla_q_norm': 'new_m', 'new_m_mla_w_uq': 'new_m', 'new_m_mla_kv_norm': 'new_m', 'new_m_mla_w_ukv': 'new_m', 'new_m_lru_w_in': 'new_m', 'new_m_lru_conv_w': 'new_m', 'new_m_lru_conv_b': 'new_m', 'new_m_lru_w_rgate': 'new_m', 'new_m_lru_b_rgate': 'new_m', 'new_m_lru_w_igate': 'new_m', 'new_m_lru_b_igate': 'new_m', 'new_m_lru_lambda': 'new_m', 'new_m_w_mem_kv': 'new_m', 'new_m_w_out': 'new_m', 'new_m_ln_g': 'new_m', 'new_m_ln_b': 'new_m', 'new_v_mla_w_in': 'new_v', 'new_v_mla_q_norm': 'new_v', 'new_v_mla_w_uq': 'new_v', 'new_v_mla_kv_norm': 'new_v', 'new_v_mla_w_ukv': 'new_v', 'new_v_lru_w_in': 'new_v', 'new_v_lru_conv_w': 'new_v', 'new_v_lru_conv_b': 'new_v', 'new_v_lru_w_rgate': 'new_v', 'new_v_lru_b_rgate': 'new_v', 'new_v_lru_w_igate': 'new_v', 'new_v_lru_b_igate': 'new_v', 'new_v_lru_lambda': 'new_v', 'new_v_w_mem_kv': 'new_v', 'new_v_w_out': 'new_v', 'new_v_ln_g': 'new_v', 'new_v_ln_b': 'new_v'}


def _forward(args):
    return _fwd_reference(*[args[k] for k in FWD_PARAMS])


def _output_shape():
    def fwd():
        inp = _fwd_setup_inputs(0)
        return _fwd_reference(*[inp[k] for k in FWD_PARAMS])
    out = _jax.eval_shape(fwd)
    return out.shape, out.dtype

N_MICROBATCH = 1
ADAM_LR = 0.001
ADAM_B1 = 0.9
ADAM_B2 = 0.999
ADAM_EPS = 1e-08
ADAM_WD = 0.01
ADAM_STEP = 10
PER_EXAMPLE_BATCH_AXIS = {'x': 0, 'mem': 0, 'positions': 0, 'loss_target': 0}
SHARED_INPUTS = []
_WEIGHT_DTYPES = {'mla_w_in': _jnp.float32, 'mla_q_norm': _jnp.float32, 'mla_w_uq': _jnp.float32, 'mla_kv_norm': _jnp.float32, 'mla_w_ukv': _jnp.float32, 'lru_w_in': _jnp.float32, 'lru_conv_w': _jnp.float32, 'lru_conv_b': _jnp.float32, 'lru_w_rgate': _jnp.float32, 'lru_b_rgate': _jnp.float32, 'lru_w_igate': _jnp.float32, 'lru_b_igate': _jnp.float32, 'lru_lambda': _jnp.float32, 'w_mem_kv': _jnp.float32, 'w_out': _jnp.float32, 'ln_g': _jnp.float32, 'ln_b': _jnp.float32}
MOMENT_SCALE = {'mla_w_in': 1.384859e-02, 'mla_q_norm': 1.341314e-02, 'mla_w_uq': 7.933538e-03, 'mla_kv_norm': 2.700204e-02, 'mla_w_ukv': 1.011279e-02, 'lru_w_in': 3.308488e-02, 'lru_conv_w': 4.385131e-02, 'lru_conv_b': 4.966094e-01, 'lru_w_rgate': 2.004225e-02, 'lru_b_rgate': 1.477820e-02, 'lru_w_igate': 3.730731e-02, 'lru_b_igate': 1.407300e-02, 'lru_lambda': 2.176057e-02, 'w_mem_kv': 5.672844e-03, 'w_out': 4.827222e-02, 'ln_g': 4.526265e+01, 'ln_b': 1.118259e+00}


def _to_microbatches(a, axis):
    t = _jnp.moveaxis(a, axis, 0)
    t = t.reshape((N_MICROBATCH, t.shape[0] // N_MICROBATCH) + t.shape[1:])
    return _jnp.moveaxis(t, 1, axis + 1)


def setup_inputs(seed: int = 0) -> dict:
    inp = _fwd_setup_inputs(seed)
    key = _jax.random.fold_in(_jax.random.key(seed), 7919)
    shape, _ = _output_shape()
    out = dict(inp)
    out["loss_target"] = _jax.random.normal(_jax.random.fold_in(key, 0), shape, _jnp.float32)
    for i, name in enumerate(TWIN_WEIGHTS):
        w = inp[name].astype(_jnp.float32)
        if MOMENT_SCALE is None:
            s = _jnp.sqrt(_jnp.mean(_jnp.square(w)) + 1e-30)
        else:
            s = MOMENT_SCALE[name]
        km, kv = _jax.random.split(_jax.random.fold_in(key, i + 1))
        out[name] = w
        out["m_" + name] = s * _jax.random.normal(km, w.shape, _jnp.float32)
        out["v_" + name] = (s * s) * _jax.random.uniform(kv, w.shape, _jnp.float32, 0.5, 1.5)
    if N_MICROBATCH > 1:
        for name, axis in PER_EXAMPLE_BATCH_AXIS.items():
            out[name] = _to_microbatches(out[name], axis)
    return {'x': out['x'], 'mem': out['mem'], 'positions': out['positions'], 'mla_w_in': out['mla_w_in'], 'mla_q_norm': out['mla_q_norm'], 'mla_w_uq': out['mla_w_uq'], 'mla_kv_norm': out['mla_kv_norm'], 'mla_w_ukv': out['mla_w_ukv'], 'lru_w_in': out['lru_w_in'], 'lru_conv_w': out['lru_conv_w'], 'lru_conv_b': out['lru_conv_b'], 'lru_w_rgate': out['lru_w_rgate'], 'lru_b_rgate': out['lru_b_rgate'], 'lru_w_igate': out['lru_w_igate'], 'lru_b_igate': out['lru_b_igate'], 'lru_lambda': out['lru_lambda'], 'w_mem_kv': out['w_mem_kv'], 'w_out': out['w_out'], 'ln_g': out['ln_g'], 'ln_b': out['ln_b'], 'loss_target': out['loss_target'], 'm_mla_w_in': out['m_mla_w_in'], 'm_mla_q_norm': out['m_mla_q_norm'], 'm_mla_w_uq': out['m_mla_w_uq'], 'm_mla_kv_norm': out['m_mla_kv_norm'], 'm_mla_w_ukv': out['m_mla_w_ukv'], 'm_lru_w_in': out['m_lru_w_in'], 'm_lru_conv_w': out['m_lru_conv_w'], 'm_lru_conv_b': out['m_lru_conv_b'], 'm_lru_w_rgate': out['m_lru_w_rgate'], 'm_lru_b_rgate': out['m_lru_b_rgate'], 'm_lru_w_igate': out['m_lru_w_igate'], 'm_lru_b_igate': out['m_lru_b_igate'], 'm_lru_lambda': out['m_lru_lambda'], 'm_w_mem_kv': out['m_w_mem_kv'], 'm_w_out': out['m_w_out'], 'm_ln_g': out['m_ln_g'], 'm_ln_b': out['m_ln_b'], 'v_mla_w_in': out['v_mla_w_in'], 'v_mla_q_norm': out['v_mla_q_norm'], 'v_mla_w_uq': out['v_mla_w_uq'], 'v_mla_kv_norm': out['v_mla_kv_norm'], 'v_mla_w_ukv': out['v_mla_w_ukv'], 'v_lru_w_in': out['v_lru_w_in'], 'v_lru_conv_w': out['v_lru_conv_w'], 'v_lru_conv_b': out['v_lru_conv_b'], 'v_lru_w_rgate': out['v_lru_w_rgate'], 'v_lru_b_rgate': out['v_lru_b_rgate'], 'v_lru_w_igate': out['v_lru_w_igate'], 'v_lru_b_igate': out['v_lru_b_igate'], 'v_lru_lambda': out['v_lru_lambda'], 'v_w_mem_kv': out['v_w_mem_kv'], 'v_w_out': out['v_w_out'], 'v_ln_g': out['v_ln_g'], 'v_ln_b': out['v_ln_b']}


def _loss(weights, diff, rest, loss_target):
    with _jax.named_scope("forward"):
        args = {**rest, TWIN_DIFF_INPUT: diff, **{k: w.astype(_WEIGHT_DTYPES[k]) for k, w in weights.items()}}
        y = _forward(args)
    with _jax.named_scope("loss_head"):
        err = _jnp.square(y.astype(_jnp.float32) - loss_target)
        return 0.5 * _jnp.sum(_jnp.mean(err, axis=-1)) if err.ndim else 0.5 * err


def _adamw(w, g, m, v):
    m = ADAM_B1 * m + (1.0 - ADAM_B1) * g
    v = ADAM_B2 * v + (1.0 - ADAM_B2) * _jnp.square(g)
    m_hat = m / (1.0 - ADAM_B1 ** ADAM_STEP)
    v_hat = v / (1.0 - ADAM_B2 ** ADAM_STEP)
    delta = -ADAM_LR * (m_hat / (_jnp.sqrt(v_hat) + ADAM_EPS) + ADAM_WD * w)
    return delta, m, v


def reference(x, mem, positions, mla_w_in, mla_q_norm, mla_w_uq, mla_kv_norm, mla_w_ukv, lru_w_in, lru_conv_w, lru_conv_b, lru_w_rgate, lru_b_rgate, lru_w_igate, lru_b_igate, lru_lambda, w_mem_kv, w_out, ln_g, ln_b, loss_target, m_mla_w_in, m_mla_q_norm, m_mla_w_uq, m_mla_kv_norm, m_mla_w_ukv, m_lru_w_in, m_lru_conv_w, m_lru_conv_b, m_lru_w_rgate, m_lru_b_rgate, m_lru_w_igate, m_lru_b_igate, m_lru_lambda, m_w_mem_kv, m_w_out, m_ln_g, m_ln_b, v_mla_w_in, v_mla_q_norm, v_mla_w_uq, v_mla_kv_norm, v_mla_w_ukv, v_lru_w_in, v_lru_conv_w, v_lru_conv_b, v_lru_w_rgate, v_lru_b_rgate, v_lru_w_igate, v_lru_b_igate, v_lru_lambda, v_w_mem_kv, v_w_out, v_ln_g, v_ln_b):
    given = dict(x=x, mem=mem, positions=positions, mla_w_in=mla_w_in, mla_q_norm=mla_q_norm, mla_w_uq=mla_w_uq, mla_kv_norm=mla_kv_norm, mla_w_ukv=mla_w_ukv, lru_w_in=lru_w_in, lru_conv_w=lru_conv_w, lru_conv_b=lru_conv_b, lru_w_rgate=lru_w_rgate, lru_b_rgate=lru_b_rgate, lru_w_igate=lru_w_igate, lru_b_igate=lru_b_igate, lru_lambda=lru_lambda, w_mem_kv=w_mem_kv, w_out=w_out, ln_g=ln_g, ln_b=ln_b, loss_target=loss_target, m_mla_w_in=m_mla_w_in, m_mla_q_norm=m_mla_q_norm, m_mla_w_uq=m_mla_w_uq, m_mla_kv_norm=m_mla_kv_norm, m_mla_w_ukv=m_mla_w_ukv, m_lru_w_in=m_lru_w_in, m_lru_conv_w=m_lru_conv_w, m_lru_conv_b=m_lru_conv_b, m_lru_w_rgate=m_lru_w_rgate, m_lru_b_rgate=m_lru_b_rgate, m_lru_w_igate=m_lru_w_igate, m_lru_b_igate=m_lru_b_igate, m_lru_lambda=m_lru_lambda, m_w_mem_kv=m_w_mem_kv, m_w_out=m_w_out, m_ln_g=m_ln_g, m_ln_b=m_ln_b, v_mla_w_in=v_mla_w_in, v_mla_q_norm=v_mla_q_norm, v_mla_w_uq=v_mla_w_uq, v_mla_kv_norm=v_mla_kv_norm, v_mla_w_ukv=v_mla_w_ukv, v_lru_w_in=v_lru_w_in, v_lru_conv_w=v_lru_conv_w, v_lru_conv_b=v_lru_conv_b, v_lru_w_rgate=v_lru_w_rgate, v_lru_b_rgate=v_lru_b_rgate, v_lru_w_igate=v_lru_w_igate, v_lru_b_igate=v_lru_b_igate, v_lru_lambda=v_lru_lambda, v_w_mem_kv=v_w_mem_kv, v_w_out=v_w_out, v_ln_g=v_ln_g, v_ln_b=v_ln_b)
    weights = {n: given[n] for n in TWIN_WEIGHTS}
    shared = {n: given[n] for n in SHARED_INPUTS}
    per_example = {n: given[n] for n in ['x', 'mem', 'positions']}
    grad_fn = _jax.value_and_grad(_loss, argnums=(0, 1))

    def one_microbatch(ex, loss_target):
        ex = dict(ex)
        diff = ex.pop(TWIN_DIFF_INPUT)
        return grad_fn(weights, diff, {**shared, **ex}, loss_target)

    if N_MICROBATCH == 1:
        loss, (grad_w, grad_x) = one_microbatch(per_example, given["loss_target"])
    else:
        def body(carry, xs):
            loss_sum, grad_sum = carry
            l_k, (gw_k, gx_k) = one_microbatch(xs[0], xs[1])
            with _jax.named_scope("update"):
                return (loss_sum + l_k, _jax.tree.map(_jnp.add, grad_sum, gw_k)), gx_k

        init = (_jnp.zeros((), _jnp.float32), _jax.tree.map(_jnp.zeros_like, weights))
        (loss, grad_w), grad_x = _jax.lax.scan(body, init, (per_example, given["loss_target"]))
    with _jax.named_scope("update"):
        delta_w, new_m, new_v = {}, {}, {}
        for n in TWIN_WEIGHTS:
            delta_w[n], new_m[n], new_v[n] = _adamw(weights[n], grad_w[n], given["m_" + n], given["v_" + n])
    return (loss, grad_x, *[grad_w[n] for n in TWIN_WEIGHTS], *[delta_w[n] for n in TWIN_WEIGHTS],
            *[new_m[n] for n in TWIN_WEIGHTS], *[new_v[n] for n in TWIN_WEIGHTS])
```

```python
import functools
import math

import jax
import jax.numpy as jnp
from jax import lax
from jax.experimental import pallas as pl
from jax.experimental.pallas import tpu as pltpu

F32 = jnp.float32
BF16 = jnp.bfloat16

D_MODEL = 1024
MEM_LEN = 256
HEAD_DIM = 64
N_TOK_HEADS = 12
N_MEM_HEADS = 4
TOK_WIDTH = 768
MEM_WIDTH = 256
MIX_WIDTH = 1024
Q_LORA = 384
KV_LORA = 256
QK_NOPE = 64
QK_ROPE = 32
QK_DIM = 96
ROPE_THETA = 10000.0
CONV_W = 4
LRU_C = 8.0
ALPHA = (2.0 * 2) ** 0.25
NORM_EPS = 1e-6
MLA_IN = 1952
LRU_IN = 2048
ADAM_LR = 0.001
ADAM_B1 = 0.9
ADAM_B2 = 0.999
ADAM_EPS = 1e-08
ADAM_WD = 0.01
ADAM_STEP = 10

N_DEV = 8
LANES = 128
SUBLANES = 8
HEAD_PAD = 128
QKV_PAD = N_TOK_HEADS * HEAD_PAD
ZP = 2048
Z0_CQ, Z0_CKV, Z0_GATE, Z0_QMEM, Z0_KR = 0, 384, 640, 1664, 1920
KR_LANE = 64
Z1_U, Z1_GATE, Z1_QMEM = 0, 768, 1792

ROW_BLOCK = 512
ATT_BLOCK = 512
VMEM_LIMIT = 56 * 1024 * 1024
NEG_BIG = -1e30


def _cp(n_axes):
    return pltpu.CompilerParams(dimension_semantics=("arbitrary",) * n_axes,
                                vmem_limit_bytes=VMEM_LIMIT)


def _dot(a, b):
    return jnp.dot(a, b, preferred_element_type=F32)


def _dot_nt(a, b):
    return lax.dot_general(a, b, (((1,), (1,)), ((), ())), preferred_element_type=F32)


def _dot_tn(a, b):
    return lax.dot_general(a, b, (((0,), (0,)), ((), ())), preferred_element_type=F32)


def _sigmoid(t):
    return 1.0 / (1.0 + jnp.exp(-t))


def _lane(shape):
    return lax.broadcasted_iota(jnp.int32, shape, len(shape) - 1)


def _full(shape):
    nd = len(shape)
    return pl.BlockSpec(shape, lambda *_: (0,) * nd)


def _rows(ts, width, col=0):
    return pl.BlockSpec((ts, width), lambda i: (i, col))


def _rowmm(x, w, name, ts):
    s, k = x.shape
    n = w.shape[1]

    def body(x_ref, w_ref, o_ref):
        o_ref[...] = _dot(x_ref[...].astype(BF16), w_ref[...])

    return pl.pallas_call(
        body, grid=(s // ts,),
        in_specs=[_rows(ts, k), _full((k, n))],
        out_specs=_rows(ts, n),
        out_shape=jax.ShapeDtypeStruct((s, n), F32),
        name=name, compiler_params=_cp(1))(x, w)


def _rms_parts(t):
    rs = lax.rsqrt(jnp.mean(t * t, axis=-1, keepdims=True) + NORM_EPS)
    return rs


def _rope(t, c, sa, sb):
    return t * c + pltpu.roll(t, LANES - 16, 1) * sa + pltpu.roll(t, 16, 1) * sb


def _rope_t(d, c, sa, sb):
    return d * c + pltpu.roll(d * sa, 16, 1) + pltpu.roll(d * sb, LANES - 16, 1)


def _mla_prep_fwd(z0, tabs, gq, gkv, wuq, wukv, ts):
    s = z0.shape[0]

    def body(z_ref, c_ref, sa_ref, sb_ref, gq_ref, gkv_ref, wuq_ref, wukv_ref,
             q_ref, k_ref, v_ref):
        cq = z_ref[:, Z0_CQ:Z0_CQ + Q_LORA]
        ckv = z_ref[:, Z0_CKV:Z0_CKV + KV_LORA]
        kr = z_ref[:, Z0_KR:Z0_KR + LANES]
        cqn = cq * _rms_parts(cq) * gq_ref[...]
        ckvn = ckv * _rms_parts(ckv) * gkv_ref[...]
        q = _dot(cqn.astype(BF16), wuq_ref[...])
        kv = _dot(ckvn.astype(BF16), wukv_ref[...])
        c, sa, sb = c_ref[...], sa_ref[...], sb_ref[...]
        krope = _rope(kr, c, sa, sb)
        for h in range(N_TOK_HEADS):
            sl = slice(h * HEAD_PAD, (h + 1) * HEAD_PAD)
            q_ref[:, sl] = _rope(q[:, sl], c, sa, sb).astype(BF16)
            k_ref[:, sl] = (kv[:, sl] + krope).astype(BF16)
        v_ref[...] = kv[:, QKV_PAD:].astype(BF16)

    out = jax.ShapeDtypeStruct((s, QKV_PAD), BF16)
    return pl.pallas_call(
        body, grid=(s // ts,),
        in_specs=[_rows(ts, ZP), _rows(ts, LANES), _rows(ts, LANES), _rows(ts, LANES),
                  _full((1, Q_LORA)), _full((1, KV_LORA)),
                  _full((Q_LORA, QKV_PAD)), _full((KV_LORA, 2 * QKV_PAD))],
        out_specs=[_rows(ts, QKV_PAD)] * 3,
        out_shape=[out, out, out],
        name="mla_prep_fwd", compiler_params=_cp(1))(z0, *tabs, gq, gkv, wuq, wukv)


def _mla_prep_bwd(z0, dq, dk, dv, tabs, gq, gkv, wuq_t, wukv_t, ts):
    s = z0.shape[0]

    def body(z_ref, dq_ref, dk_ref, dv_ref, c_ref, sa_ref, sb_ref, gq_ref, gkv_ref,
             wuqt_ref, wukvt_ref, dza_ref, dzk_ref, dwuq_ref, dwukv_ref, dg_ref):
        @pl.when(pl.program_id(0) == 0)
        def _():
            dwuq_ref[...] = jnp.zeros_like(dwuq_ref)
            dwukv_ref[...] = jnp.zeros_like(dwukv_ref)
            dg_ref[...] = jnp.zeros_like(dg_ref)

        cq = z_ref[:, Z0_CQ:Z0_CQ + Q_LORA]
        ckv = z_ref[:, Z0_CKV:Z0_CKV + KV_LORA]
        rq, rkv = _rms_parts(cq), _rms_parts(ckv)
        gq_, gkv_ = gq_ref[...], gkv_ref[...]
        cqn = (cq * rq * gq_).astype(BF16)
        ckvn = (ckv * rkv * gkv_).astype(BF16)
        c, sa, sb = c_ref[...], sa_ref[...], sb_ref[...]
        dqp, dksum = [], None
        for h in range(N_TOK_HEADS):
            sl = slice(h * HEAD_PAD, (h + 1) * HEAD_PAD)
            dqp.append(_rope_t(dq_ref[:, sl], c, sa, sb))
            dksum = dk_ref[:, sl] if dksum is None else dksum + dk_ref[:, sl]
        dqp = jnp.concatenate(dqp, axis=1).astype(BF16)
        lane = _lane(dksum.shape)
        dzk_ref[...] = jnp.where((lane >= KR_LANE) & (lane < KR_LANE + QK_ROPE),
                                 _rope_t(dksum, c, sa, sb), 0.0)
        dkv = jnp.concatenate([dk_ref[...], dv_ref[...]], axis=1).astype(BF16)
        dcqn = _dot(dqp, wuqt_ref[...])
        dckvn = _dot(dkv, wukvt_ref[...])
        dwuq_ref[...] += _dot_tn(cqn, dqp)
        dwukv_ref[...] += _dot_tn(ckvn, dkv)
        dg_ref[0:1, 0:Q_LORA] += jnp.sum(dcqn * cq * rq, axis=0, keepdims=True)
        dg_ref[0:1, Q_LORA:Q_LORA + KV_LORA] += jnp.sum(dckvn * ckv * rkv, axis=0, keepdims=True)
        wq = dcqn * gq_
        wkv = dckvn * gkv_
        dcq = rq * wq - cq * (rq * rq * rq) * jnp.mean(wq * cq, axis=-1, keepdims=True)
        dckv = rkv * wkv - ckv * (rkv * rkv * rkv) * jnp.mean(wkv * ckv, axis=-1, keepdims=True)
        dza_ref[:, 0:Q_LORA] = dcq
        dza_ref[:, Q_LORA:Q_LORA + KV_LORA] = dckv

    na = Q_LORA + KV_LORA
    return pl.pallas_call(
        body, grid=(s // ts,),
        in_specs=[_rows(ts, ZP), _rows(ts, QKV_PAD), _rows(ts, QKV_PAD), _rows(ts, QKV_PAD),
                  _rows(ts, LANES), _rows(ts, LANES), _rows(ts, LANES),
                  _full((1, Q_LORA)), _full((1, KV_LORA)),
                  _full((QKV_PAD, Q_LORA)), _full((2 * QKV_PAD, KV_LORA))],
        out_specs=[_rows(ts, na), _rows(ts, LANES), _full((Q_LORA, QKV_PAD)),
                   _full((KV_LORA, 2 * QKV_PAD)), _full((SUBLANES, na))],
        out_shape=[jax.ShapeDtypeStruct((s, na), F32), jax.ShapeDtypeStruct((s, LANES), F32),
                   jax.ShapeDtypeStruct((Q_LORA, QKV_PAD), F32),
                   jax.ShapeDtypeStruct((KV_LORA, 2 * QKV_PAD), F32),
                   jax.ShapeDtypeStruct((SUBLANES, na), F32)],
        name="mla_prep_bwd", compiler_params=_cp(1))(
            z0, dq, dk, dv, *tabs, gq, gkv, wuq_t, wukv_t)


def _causal_mask(t):
    r = lax.broadcasted_iota(jnp.int32, (t, t), 0)
    c = lax.broadcasted_iota(jnp.int32, (t, t), 1)
    return c <= r


def _flash_fwd(q, k, v, t):
    s = q.shape[0]
    nb = s // t
    scale = 1.0 / math.sqrt(QK_DIM)

    def body(q_ref, k_ref, v_ref, o_ref, lse_ref, m_scr, l_scr, acc_scr):
        i, j = pl.program_id(1), pl.program_id(2)

        @pl.when(j == 0)
        def _():
            m_scr[...] = jnp.full_like(m_scr, NEG_BIG)
            l_scr[...] = jnp.zeros_like(l_scr)
            acc_scr[...] = jnp.zeros_like(acc_scr)

        def step(masked):
            sc = _dot_nt(q_ref[...], k_ref[...]) * scale
            if masked:
                sc = jnp.where(_causal_mask(t), sc, NEG_BIG)
            m_prev = m_scr[...]
            m_next = jnp.maximum(m_prev, jnp.max(sc, axis=-1, keepdims=True))
            alpha = jnp.exp(m_prev - m_next)
            p = jnp.exp(sc - m_next[:, :1])
            l_scr[...] = alpha * l_scr[...] + jnp.sum(p, axis=-1, keepdims=True)
            acc_scr[...] = alpha * acc_scr[...] + _dot(p.astype(BF16), v_ref[...])
            m_scr[...] = m_next

        @pl.when(j < i)
        def _():
            step(False)

        @pl.when(j == i)
        def _():
            step(True)
            l = l_scr[...]
            o_ref[...] = acc_scr[...] / l
            lse_ref[...] = m_scr[...] + jnp.log(l)

    qspec = pl.BlockSpec((t, HEAD_PAD), lambda h, i, j: (i, h))
    kspec = pl.BlockSpec((t, HEAD_PAD), lambda h, i, j: (jnp.minimum(j, i), h))
    out = jax.ShapeDtypeStruct((s, QKV_PAD), F32)
    return pl.pallas_call(
        body, grid=(N_TOK_HEADS, nb, nb),
        in_specs=[qspec, kspec, kspec],
        out_specs=[qspec, qspec],
        out_shape=[out, out],
        scratch_shapes=[pltpu.VMEM((t, HEAD_PAD), F32)] * 3,
        name="flash_fwd", compiler_params=_cp(3))(q, k, v)


def _flash_bwd(q, k, v, o, lse, do, t):
    s = q.shape[0]
    nb = s // t
    scale = 1.0 / math.sqrt(QK_DIM)

    def body(q_ref, k_ref, v_ref, o_ref, lse_ref, do_ref, dq_ref, dk_ref, dv_ref,
             dk_scr, dv_scr):
        j, i = pl.program_id(1), pl.program_id(2)

        @pl.when(i == j)
        def _():
            dk_scr[...] = jnp.zeros_like(dk_scr)
            dv_scr[...] = jnp.zeros_like(dv_scr)

        def step(masked):
            qb, kb, vb = q_ref[...], k_ref[...], v_ref[...]
            do_f = do_ref[...]
            do_b = do_f.astype(BF16)
            sc = _dot_nt(qb, kb) * scale
            if masked:
                sc = jnp.where(_causal_mask(t), sc, NEG_BIG)
            p = jnp.exp(sc - lse_ref[:, :1])
            dp = _dot_nt(do_b, vb)
            delta = jnp.sum(do_f * o_ref[...], axis=-1, keepdims=True)
            ds = (p * (dp - delta) * scale).astype(BF16)
            dv_scr[...] += _dot_tn(p.astype(BF16), do_b)
            dk_scr[...] += _dot_tn(ds, qb)
            dq_blk = _dot(ds, kb)
            rows = pl.ds(pl.multiple_of(i * t, t), t)

            @pl.when(j == 0)
            def _():
                dq_ref[rows, :] = dq_blk

            @pl.when(j > 0)
            def _():
                dq_ref[rows, :] += dq_blk

        @pl.when(i > j)
        def _():
            step(False)

        @pl.when(i == j)
        def _():
            step(True)

        @pl.when(i == nb - 1)
        def _():
            dk_ref[...] = dk_scr[...]
            dv_ref[...] = dv_scr[...]

    qspec = pl.BlockSpec((t, HEAD_PAD), lambda h, j, i: (jnp.maximum(i, j), h))
    kspec = pl.BlockSpec((t, HEAD_PAD), lambda h, j, i: (j, h))
    dqspec = pl.BlockSpec((s, HEAD_PAD), lambda h, j, i: (0, h))
    out = jax.ShapeDtypeStruct((s, QKV_PAD), F32)
    return pl.pallas_call(
        body, grid=(N_TOK_HEADS, nb, nb),
        in_specs=[qspec, kspec, kspec, qspec, qspec, qspec],
        out_specs=[dqspec, kspec, kspec],
        out_shape=[out, out, out],
        scratch_shapes=[pltpu.VMEM((t, HEAD_PAD), F32)] * 2,
        name="flash_bwd", compiler_params=_cp(3))(q, k, v, o, lse, do)


def _mem_probs(qp, kp, hh):
    lane = _lane(qp.shape)
    keep = (lane < HEAD_DIM) if hh == 0 else (lane >= HEAD_DIM)
    qh = jnp.where(keep, qp, 0.0).astype(BF16)
    sc = _dot_nt(qh, kp) * (1.0 / math.sqrt(HEAD_DIM))
    e = jnp.exp(sc - jnp.max(sc, axis=-1, keepdims=True))
    return e / jnp.sum(e, axis=-1, keepdims=True), keep


def _gate_mem_fwd(tok, z, memkv, g0, q0, padded, name, ts):
    s = z.shape[0]
    zw = z.shape[1]
    tw = tok.shape[1]

    def body(tok_ref, z_ref, mkv_ref, cat_ref, y_ref):
        if padded:
            for p in range(N_TOK_HEADS // 2):
                a = tok_ref[:, (2 * p) * HEAD_PAD:(2 * p + 1) * HEAD_PAD]
                b = tok_ref[:, (2 * p + 1) * HEAD_PAD:(2 * p + 2) * HEAD_PAD]
                cat_ref[:, p * LANES:(p + 1) * LANES] = a + pltpu.roll(b, HEAD_DIM, 1)
        else:
            cat_ref[:, 0:TOK_WIDTH] = tok_ref[...]
        for pr in range(N_MEM_HEADS // 2):
            sl = slice(pr * LANES, (pr + 1) * LANES)
            qp = z_ref[:, q0 + pr * LANES:q0 + (pr + 1) * LANES]
            kp = mkv_ref[:, sl].astype(BF16)
            vp = mkv_ref[:, MEM_WIDTH + pr * LANES:MEM_WIDTH + (pr + 1) * LANES].astype(BF16)
            outs = []
            for hh in range(2):
                p, _ = _mem_probs(qp, kp, hh)
                outs.append(_dot(p.astype(BF16), vp))
            lane = _lane(outs[0].shape)
            cat_ref[:, TOK_WIDTH + pr * LANES:TOK_WIDTH + (pr + 1) * LANES] = jnp.where(
                lane < HEAD_DIM, outs[0], outs[1])
        gate = z_ref[:, g0:g0 + MIX_WIDTH]
        y_ref[...] = cat_ref[...] * (gate * _sigmoid(gate))

    out = jax.ShapeDtypeStruct((s, MIX_WIDTH), F32)
    return pl.pallas_call(
        body, grid=(s // ts,),
        in_specs=[_rows(ts, tw), _rows(ts, zw), _full((MEM_LEN, 2 * MEM_WIDTH))],
        out_specs=[_rows(ts, MIX_WIDTH)] * 2,
        out_shape=[out, out],
        name=name, compiler_params=_cp(1))(tok, z, memkv)


def _gate_mem_bwd(dy, cat, z, memkv, g0, q0, padded, name, ts):
    s = z.shape[0]
    zw = z.shape[1]
    tw = QKV_PAD if padded else TOK_WIDTH
    gq_w = MIX_WIDTH + MEM_WIDTH

    def body(dy_ref, cat_ref, z_ref, mkv_ref, dzg_ref, dtok_ref, dmkv_ref):
        @pl.when(pl.program_id(0) == 0)
        def _():
            dmkv_ref[...] = jnp.zeros_like(dmkv_ref)

        gate = z_ref[:, g0:g0 + MIX_WIDTH]
        sg = _sigmoid(gate)
        dy_ = dy_ref[...]
        dzg_ref[:, 0:MIX_WIDTH] = dy_ * cat_ref[...] * (sg * (1.0 + gate * (1.0 - sg)))
        dcat = dy_ * (gate * sg)
        if padded:
            for p in range(N_TOK_HEADS // 2):
                d = dcat[:, p * LANES:(p + 1) * LANES]
                lane = _lane(d.shape)
                dtok_ref[:, (2 * p) * HEAD_PAD:(2 * p + 1) * HEAD_PAD] = jnp.where(
                    lane < HEAD_DIM, d, 0.0)
                dtok_ref[:, (2 * p + 1) * HEAD_PAD:(2 * p + 2) * HEAD_PAD] = jnp.where(
                    lane < HEAD_DIM, pltpu.roll(d, HEAD_DIM, 1), 0.0)
        else:
            dtok_ref[...] = dcat[:, 0:TOK_WIDTH]
        for pr in range(N_MEM_HEADS // 2):
            sl = slice(pr * LANES, (pr + 1) * LANES)
            vsl = slice(MEM_WIDTH + pr * LANES, MEM_WIDTH + (pr + 1) * LANES)
            qp = z_ref[:, q0 + pr * LANES:q0 + (pr + 1) * LANES]
            qpb = qp.astype(BF16)
            kp = mkv_ref[:, sl].astype(BF16)
            vp = mkv_ref[:, vsl].astype(BF16)
            dmo = dcat[:, TOK_WIDTH + pr * LANES:TOK_WIDTH + (pr + 1) * LANES]
            dqp = None
            for hh in range(2):
                p, keep = _mem_probs(qp, kp, hh)
                do_h = jnp.where(keep, dmo, 0.0).astype(BF16)
                dmkv_ref[:, vsl] += _dot_tn(p.astype(BF16), do_h)
                dp = _dot_nt(do_h, vp)
                ds = (p * (dp - jnp.sum(dp * p, axis=-1, keepdims=True))
                      * (1.0 / math.sqrt(HEAD_DIM))).astype(BF16)
                dqh = jnp.where(keep, _dot(ds, kp), 0.0)
                dqp = dqh if dqp is None else dqp + dqh
                dkh = _dot_tn(ds, qpb)
                klane = _lane(dkh.shape)
                kkeep = (klane < HEAD_DIM) if hh == 0 else (klane >= HEAD_DIM)
                dmkv_ref[:, sl] += jnp.where(kkeep, dkh, 0.0)
            dzg_ref[:, MIX_WIDTH + pr * LANES:MIX_WIDTH + (pr + 1) * LANES] = dqp

    return pl.pallas_call(
        body, grid=(s // ts,),
        in_specs=[_rows(ts, MIX_WIDTH), _rows(ts, MIX_WIDTH), _rows(ts, zw),
                  _full((MEM_LEN, 2 * MEM_WIDTH))],
        out_specs=[_rows(ts, gq_w), _rows(ts, tw), _full((MEM_LEN, 2 * MEM_WIDTH))],
        out_shape=[jax.ShapeDtypeStruct((s, gq_w), F32), jax.ShapeDtypeStruct((s, tw), F32),
                   jax.ShapeDtypeStruct((MEM_LEN, 2 * MEM_WIDTH), F32)],
        name=name, compiler_params=_cp(1))(dy, cat, z, memkv)


def _ln_stats(pre):
    mu = jnp.mean(pre, axis=-1, keepdims=True)
    d = pre - mu
    rstd = lax.rsqrt(jnp.mean(d * d, axis=-1, keepdims=True) + NORM_EPS)
    return d * rstd, rstd


def _outproj_ln_fwd(y, w, h, g, b, tgt, name, ts):
    s = y.shape[0]
    with_loss = tgt is not None

    def body(*refs):
        if with_loss:
            y_ref, w_ref, h_ref, g_ref, b_ref, t_ref, pre_ref, out_ref, loss_ref = refs
        else:
            y_ref, w_ref, h_ref, g_ref, b_ref, pre_ref, out_ref = refs
        pre = ALPHA * h_ref[...] + _dot(y_ref[...].astype(BF16), w_ref[...])
        pre_ref[...] = pre
        xhat, _ = _ln_stats(pre)
        hout = xhat * g_ref[...] + b_ref[...]
        if with_loss:
            @pl.when(pl.program_id(0) == 0)
            def _():
                loss_ref[...] = jnp.zeros_like(loss_ref)
            err = hout - t_ref[...]
            out_ref[...] = err * (1.0 / D_MODEL)
            loss_ref[...] += 0.5 * jnp.sum(jnp.mean(err * err, axis=-1, keepdims=True))
        else:
            out_ref[...] = hout

    act = jax.ShapeDtypeStruct((s, D_MODEL), F32)
    in_specs = [_rows(ts, MIX_WIDTH), _full((MIX_WIDTH, D_MODEL)), _rows(ts, D_MODEL),
                _full((1, D_MODEL)), _full((1, D_MODEL))]
    out_specs = [_rows(ts, D_MODEL)] * 2
    out_shape = [act, act]
    args = [y, w, h, g, b]
    if with_loss:
        in_specs.append(_rows(ts, D_MODEL))
        out_specs.append(_full((SUBLANES, LANES)))
        out_shape.append(jax.ShapeDtypeStruct((SUBLANES, LANES), F32))
        args.append(tgt)
    return pl.pallas_call(
        body, grid=(s // ts,), in_specs=in_specs, out_specs=out_specs, out_shape=out_shape,
        name=name, compiler_params=_cp(1))(*args)


def _outproj_ln_bwd(dh, pre, g, y, w_t, name, ts):
    s = y.shape[0]

    def body(dh_ref, pre_ref, g_ref, y_ref, wt_ref, dpre_ref, dy_ref, dw_ref, dgb_ref):
        @pl.when(pl.program_id(0) == 0)
        def _():
            dw_ref[...] = jnp.zeros_like(dw_ref)
            dgb_ref[...] = jnp.zeros_like(dgb_ref)

        dh_ = dh_ref[...]
        xhat, rstd = _ln_stats(pre_ref[...])
        dxh = dh_ * g_ref[...]
        dpre = rstd * (dxh - jnp.mean(dxh, axis=-1, keepdims=True)
                       - xhat * jnp.mean(dxh * xhat, axis=-1, keepdims=True))
        dpre_ref[...] = dpre
        dgb_ref[0:1, :] += jnp.sum(dh_ * xhat, axis=0, keepdims=True)
        dgb_ref[1:2, :] += jnp.sum(dh_, axis=0, keepdims=True)
        dpb = dpre.astype(BF16)
        dy_ref[...] = _dot(dpb, wt_ref[...])
        dw_ref[...] += _dot_tn(y_ref[...].astype(BF16), dpb)

    act = jax.ShapeDtypeStruct((s, D_MODEL), F32)
    return pl.pallas_call(
        body, grid=(s // ts,),
        in_specs=[_rows(ts, D_MODEL), _rows(ts, D_MODEL), _full((1, D_MODEL)),
                  _rows(ts, MIX_WIDTH), _full((D_MODEL, MIX_WIDTH))],
        out_specs=[_rows(ts, D_MODEL), _rows(ts, MIX_WIDTH), _full((MIX_WIDTH, D_MODEL)),
                   _full((SUBLANES, D_MODEL))],
        out_shape=[act, act, jax.ShapeDtypeStruct((MIX_WIDTH, D_MODEL), F32),
                   jax.ShapeDtypeStruct((SUBLANES, D_MODEL), F32)],
        name=name, compiler_params=_cp(1))(dh, pre, g, y, w_t)


def _linear_bwd(x, dys, offs, w_t, resid, name, ts):
    s, kdim = x.shape
    n = w_t.shape[0]
    widths = [d.shape[1] for d in dys]
    npieces = len(dys)

    def body(*refs):
        x_ref = refs[0]
        dy_refs = refs[1:1 + npieces]
        wt_ref, r_ref, dx_ref, dw_ref = refs[1 + npieces:]

        @pl.when(pl.program_id(0) == 0)
        def _():
            dw_ref[...] = jnp.zeros_like(dw_ref)

        xb = x_ref[...].astype(BF16)
        dx = ALPHA * r_ref[...]
        for dy_ref, off, wd in zip(dy_refs, offs, widths):
            dyb = dy_ref[...].astype(BF16)
            dx = dx + _dot(dyb, wt_ref[off:off + wd, :])
            dw_ref[:, off:off + wd] += _dot_tn(xb, dyb)
        dx_ref[...] = dx

    return pl.pallas_call(
        body, grid=(s // ts,),
        in_specs=[_rows(ts, kdim)] + [_rows(ts, wd) for wd in widths]
                 + [_full((n, kdim)), _rows(ts, kdim)],
        out_specs=[_rows(ts, kdim), _full((kdim, n))],
        out_shape=[jax.ShapeDtypeStruct((s, kdim), F32), jax.ShapeDtypeStruct((kdim, n), F32)],
        name=name, compiler_params=_cp(1))(x, *dys, w_t, resid)


def _wgrad_small(x, dy, name):
    def body(x_ref, dy_ref, dw_ref):
        dw_ref[...] = _dot_tn(x_ref[...].astype(BF16), dy_ref[...].astype(BF16))

    return pl.pallas_call(
        body, out_shape=jax.ShapeDtypeStruct((x.shape[1], dy.shape[1]), F32),
        name=name, compiler_params=pltpu.CompilerParams(vmem_limit_bytes=VMEM_LIMIT))(x, dy)


def _shift_down(u, carry8, k):
    if k == 0:
        return u
    rolled = pltpu.roll(u, k, 0)
    row = lax.broadcasted_iota(jnp.int32, carry8.shape, 0)
    top = jnp.where(row < k, pltpu.roll(carry8, k, 0), rolled[0:SUBLANES])
    return jnp.concatenate([top, rolled[SUBLANES:]], axis=0)


def _shift_up(u, carry8, k):
    if k == 0:
        return u
    n = u.shape[0]
    rolled = pltpu.roll(u, n - k, 0)
    row = lax.broadcasted_iota(jnp.int32, carry8.shape, 0)
    bot = jnp.where(row >= SUBLANES - k, pltpu.roll(carry8, SUBLANES - k, 0),
                    rolled[n - SUBLANES:])
    return jnp.concatenate([rolled[:n - SUBLANES], bot], axis=0)


def _neg_expm1(t):
    e = jnp.exp(t)
    em1 = e - 1.0
    safe = jnp.where(e == 1.0, 1.0, jnp.log(e))
    return -jnp.where(e == 1.0, t, jnp.where(em1 == -1.0, -1.0, em1 * t / safe))


def _lru_gates(u, carry8, cw_ref, vec_ref, wr_ref, wi_ref):
    xc = vec_ref[0:1, :] + cw_ref[3:4, :] * u
    for k in range(1, CONV_W):
        xc = xc + cw_ref[3 - k:4 - k, :] * _shift_down(u, carry8, k)
    xb = xc.astype(BF16)
    r = _sigmoid(_dot(xb, wr_ref[...]) + vec_ref[1:2, :])
    ig = _sigmoid(_dot(xb, wi_ref[...]) + vec_ref[2:3, :])
    nlam = -vec_ref[3:4, :]
    softplus = jnp.maximum(nlam, 0.0) + jnp.log(1.0 + jnp.exp(-jnp.abs(nlam)))
    cneg = -LRU_C * softplus
    log_a = cneg * r
    a = jnp.exp(log_a)
    sq = jnp.sqrt(_neg_expm1(2.0 * log_a))
    return xc, r, ig, cneg, a, sq


def _lru_fwd(z1, cw8, vec8, wr, wi, ts):
    s = z1.shape[0]

    def body(u_ref, cw_ref, vec_ref, wr_ref, wi_ref, hs_ref, hp_ref,
             cu_scr, ch_scr, a_scr, gx_scr):
        @pl.when(pl.program_id(0) == 0)
        def _():
            cu_scr[...] = jnp.zeros_like(cu_scr)
            ch_scr[...] = jnp.zeros_like(ch_scr)

        u = u_ref[...]
        xc, _, ig, _, a, sq = _lru_gates(u, cu_scr[...], cw_ref, vec_ref, wr_ref, wi_ref)
        a_scr[...] = a
        gx_scr[...] = sq * (ig * xc)

        def step(t, h):
            hp_ref[pl.ds(t, 1), :] = h
            h = a_scr[pl.ds(t, 1), :] * h + gx_scr[pl.ds(t, 1), :]
            hs_ref[pl.ds(t, 1), :] = h
            return h

        h = lax.fori_loop(0, ts, step, ch_scr[0:1, :], unroll=8)
        ch_scr[0:1, :] = h
        cu_scr[...] = u[ts - SUBLANES:, :]

    w = TOK_WIDTH
    out = jax.ShapeDtypeStruct((s, w), F32)
    return pl.pallas_call(
        body, grid=(s // ts,),
        in_specs=[_rows(ts, w), _full((SUBLANES, w)), _full((SUBLANES, w)),
                  _full((w, w)), _full((w, w))],
        out_specs=[_rows(ts, w)] * 2,
        out_shape=[out, out],
        scratch_shapes=[pltpu.VMEM((SUBLANES, w), F32), pltpu.VMEM((SUBLANES, w), F32),
                        pltpu.VMEM((ts, w), F32), pltpu.VMEM((ts, w), F32)],
        name="lru_fwd", compiler_params=_cp(1))(z1, cw8, vec8, wr, wi)


def _lru_bwd(z1, dhs, hprev, cw8, vec8, wr, wi, wr_t, wi_t, ts):
    s = z1.shape[0]
    nb = s // ts
    w = TOK_WIDTH
    tiles = ts // SUBLANES

    def body(u_ref, up_ref, dhs_ref, hp_ref, cw_ref, vec_ref, wr_ref, wi_ref, wrt_ref, wit_ref,
             du_ref, dwr_ref, dwi_ref, dvec_ref, cc_scr, cd_scr, a_scr, dh_scr):
        i = pl.program_id(0)

        @pl.when(i == 0)
        def _():
            cc_scr[...] = jnp.zeros_like(cc_scr)
            cd_scr[...] = jnp.zeros_like(cd_scr)
            dwr_ref[...] = jnp.zeros_like(dwr_ref)
            dwi_ref[...] = jnp.zeros_like(dwi_ref)
            dvec_ref[...] = jnp.zeros_like(dvec_ref)

        u = u_ref[...]
        carry8 = jnp.where(i == nb - 1, 0.0, up_ref[...])
        xc, r, ig, cneg, a, sq = _lru_gates(u, carry8, cw_ref, vec_ref, wr_ref, wi_ref)
        a_scr[...] = a

        def step(n, c):
            t = ts - 1 - n
            dh = dhs_ref[pl.ds(t, 1), :] + c
            dh_scr[pl.ds(t, 1), :] = dh
            return a_scr[pl.ds(t, 1), :] * dh

        cc_scr[0:1, :] = lax.fori_loop(0, ts, step, cc_scr[0:1, :], unroll=8)
        dh = dh_scr[...]
        ix = ig * xc
        dix = dh * sq
        dlog_a = dh * hp_ref[...] * a - (dh * ix) * (a * a) / sq
        dpr = (dlog_a * cneg) * r * (1.0 - r)
        dpi = (dix * xc) * ig * (1.0 - ig)
        dprb, dpib = dpr.astype(BF16), dpi.astype(BF16)
        xb = xc.astype(BF16)
        dwr_ref[...] += _dot_tn(xb, dprb)
        dwi_ref[...] += _dot_tn(xb, dpib)
        dxc = dix * ig + _dot(dprb, wrt_ref[...]) + _dot(dpib, wit_ref[...])
        for k in range(CONV_W):
            dvec_ref[3 - k:4 - k, :] += jnp.sum(dxc * _shift_down(u, carry8, k),
                                                axis=0, keepdims=True)
        dvec_ref[4:5, :] += jnp.sum(dxc, axis=0, keepdims=True)
        dvec_ref[5:6, :] += jnp.sum(dpr, axis=0, keepdims=True)
        dvec_ref[6:7, :] += jnp.sum(dpi, axis=0, keepdims=True)
        dvec_ref[7:8, :] += (jnp.sum(dlog_a * r, axis=0, keepdims=True)
                             * (LRU_C * _sigmoid(-vec_ref[3:4, :])))
        nxt = cd_scr[...]
        du = cw_ref[3:4, :] * dxc
        for k in range(1, CONV_W):
            du = du + cw_ref[3 - k:4 - k, :] * _shift_up(dxc, nxt, k)
        du_ref[...] = du
        cd_scr[...] = dxc[0:SUBLANES, :]

    rev = lambda i: (nb - 1 - i, 0)
    prev8 = lambda i: (jnp.maximum((nb - 1 - i) * tiles - 1, 0), 0)
    blk = pl.BlockSpec((ts, w), rev)
    return pl.pallas_call(
        body, grid=(nb,),
        in_specs=[blk, pl.BlockSpec((SUBLANES, w), prev8), blk, blk,
                  _full((SUBLANES, w)), _full((SUBLANES, w)),
                  _full((w, w)), _full((w, w)), _full((w, w)), _full((w, w))],
        out_specs=[blk, _full((w, w)), _full((w, w)), _full((SUBLANES, w))],
        out_shape=[jax.ShapeDtypeStruct((s, w), F32), jax.ShapeDtypeStruct((w, w), F32),
                   jax.ShapeDtypeStruct((w, w), F32), jax.ShapeDtypeStruct((SUBLANES, w), F32)],
        scratch_shapes=[pltpu.VMEM((SUBLANES, w), F32), pltpu.VMEM((SUBLANES, w), F32),
                        pltpu.VMEM((ts, w), F32), pltpu.VMEM((ts, w), F32)],
        name="lru_bwd", compiler_params=_cp(1))(
            z1, z1, dhs, hprev, cw8, vec8, wr, wi, wr_t, wi_t)


def _row_block(rows, cap=1024):
    best = SUBLANES
    for cand in range(SUBLANES, min(rows, cap) + 1, SUBLANES):
        if rows % cand == 0:
            best = cand
    return best


def _adamw(parts, w, m, v, name):
    rows = w.shape[0]
    tr = _row_block(rows)

    def body(p_ref, w_ref, m_ref, v_ref, g_ref, d_ref, nm_ref, nv_ref):
        g = p_ref[0]
        for dev in range(1, N_DEV):
            g = g + p_ref[dev]
        g_ref[...] = g
        nm = ADAM_B1 * m_ref[...] + (1.0 - ADAM_B1) * g
        nv = ADAM_B2 * v_ref[...] + (1.0 - ADAM_B2) * (g * g)
        m_hat = nm / (1.0 - ADAM_B1 ** ADAM_STEP)
        v_hat = nv / (1.0 - ADAM_B2 ** ADAM_STEP)
        d_ref[...] = -ADAM_LR * (m_hat / (jnp.sqrt(v_hat) + ADAM_EPS) + ADAM_WD * w_ref[...])
        nm_ref[...] = nm
        nv_ref[...] = nv

    blk = pl.BlockSpec((tr, LANES), lambda i: (i, 0))
    out = jax.ShapeDtypeStruct((rows, LANES), F32)
    return pl.pallas_call(
        body, grid=(rows // tr,),
        in_specs=[pl.BlockSpec((N_DEV, tr, LANES), lambda i: (0, i, 0)), blk, blk, blk],
        out_specs=[blk] * 4, out_shape=[out] * 4,
        name=name, compiler_params=_cp(1))(parts, w, m, v)


ANY = pl.BlockSpec(memory_space=pl.ANY)
MESH = pl.DeviceIdType.MESH


def _slot(p):
    return 4 * p[0] + 2 * p[1] + p[2]


def _allgather(xs):
    n = len(xs)

    def body(*refs):
        x_refs, o_refs = refs[:n], refs[n:2 * n]
        send_sems, recv_sems, local_sems = refs[2 * n:]
        x, y, c = lax.axis_index("x"), lax.axis_index("y"), lax.axis_index("c")
        me, sibling = (x, y, c), (x, y, 1 - c)
        chips = [(1 - x, y), (x, 1 - y), (1 - x, 1 - y)]

        def copy(a, k, block, to, from_input=False):
            dst = o_refs[a].at[_slot(block)]
            return pltpu.make_async_remote_copy(
                src_ref=x_refs[a] if from_input else dst, dst_ref=dst,
                send_sem=send_sems.at[a, k], recv_sem=recv_sems.at[a, k],
                device_id=to, device_id_type=MESH)

        mine = [pltpu.make_async_copy(x_refs[a], o_refs[a].at[_slot(me)], local_sems.at[a])
                for a in range(n)]
        for cp in mine:
            cp.start()
        first = []
        for a in range(n):
            first.append(copy(a, 0, me, sibling, True))
            first += [copy(a, 1 + j, me, (*chip, c), True) for j, chip in enumerate(chips)]
        for cp in first:
            cp.start()
        passed = []
        for j, chip in enumerate(chips):
            for a in range(n):
                copy(a, 1 + j, (*chip, c), me).wait_recv()
                cp = copy(a, 4 + j, (*chip, c), sibling)
                cp.start()
                passed.append(cp)
        for a in range(n):
            copy(a, 0, sibling, me).wait_recv()
            for j, chip in enumerate(chips):
                copy(a, 4 + j, (*chip, 1 - c), me).wait_recv()
        for cp in first + passed:
            cp.wait_send()
        for cp in mine:
            cp.wait()

    return pl.pallas_call(
        body,
        out_shape=[jax.ShapeDtypeStruct((N_DEV,) + t.shape, t.dtype) for t in xs],
        in_specs=[ANY] * n, out_specs=[ANY] * n,
        scratch_shapes=[pltpu.SemaphoreType.DMA((n, 7)), pltpu.SemaphoreType.DMA((n, 7)),
                        pltpu.SemaphoreType.DMA((n,))],
        name="allgather_weights")(*xs)


def _exchange_grads(gs, gr):
    def body(gs_ref, gr_ref, os_ref, or_ref, send_sems, recv_sems, local_sems):
        x, y, c = lax.axis_index("x"), lax.axis_index("y"), lax.axis_index("c")
        me = _slot((x, y, c))
        own = [pltpu.make_async_copy(gs_ref.at[me], os_ref.at[me], local_sems.at[0]),
               pltpu.make_async_copy(gr_ref, or_ref.at[me], local_sems.at[1])]
        for cp in own:
            cp.start()
        copies = []
        for rel in range(1, N_DEV):
            peer = (x ^ (rel >> 2), y ^ ((rel >> 1) & 1), c ^ (rel & 1))
            copies.append(pltpu.make_async_remote_copy(
                src_ref=gs_ref.at[_slot(peer)], dst_ref=os_ref.at[me],
                send_sem=send_sems.at[0, rel - 1], recv_sem=recv_sems.at[0, rel - 1],
                device_id=peer, device_id_type=MESH))
            copies.append(pltpu.make_async_remote_copy(
                src_ref=gr_ref, dst_ref=or_ref.at[me],
                send_sem=send_sems.at[1, rel - 1], recv_sem=recv_sems.at[1, rel - 1],
                device_id=peer, device_id_type=MESH))
        for cp in copies:
            cp.start()
        for cp in copies:
            cp.wait()
        for cp in own:
            cp.wait()

    return pl.pallas_call(
        body,
        out_shape=[jax.ShapeDtypeStruct(gs.shape, gs.dtype),
                   jax.ShapeDtypeStruct((N_DEV,) + gr.shape, gr.dtype)],
        in_specs=[ANY, ANY], out_specs=[ANY, ANY],
        scratch_shapes=[pltpu.SemaphoreType.DMA((2, 7)), pltpu.SemaphoreType.DMA((2, 7)),
                        pltpu.SemaphoreType.DMA((2,))],
        name="exchange_grads")(gs, gr)


BIG = [("mla_w_in", (D_MODEL, MLA_IN), 1), ("mla_w_uq", (Q_LORA, N_TOK_HEADS * QK_DIM), 1),
       ("mla_w_ukv", (KV_LORA, N_TOK_HEADS * 2 * HEAD_DIM), 1), ("lru_w_in", (D_MODEL, LRU_IN), 1),
       ("w_mem_kv", (2, D_MODEL, 2 * MEM_WIDTH), 1), ("w_out", (2, MIX_WIDTH, D_MODEL), 1)]
SMALL = [("lru_conv_w", (CONV_W, TOK_WIDTH), 1), ("lru_conv_b", (TOK_WIDTH,), 0),
         ("lru_b_rgate", (TOK_WIDTH,), 0), ("lru_b_igate", (TOK_WIDTH,), 0),
         ("lru_lambda", (TOK_WIDTH,), 0)]
REPL = [("mla_q_norm", (Q_LORA,)), ("mla_kv_norm", (KV_LORA,)),
        ("lru_w_rgate", (N_TOK_HEADS, HEAD_DIM, HEAD_DIM)),
        ("lru_w_igate", (N_TOK_HEADS, HEAD_DIM, HEAD_DIM)),
        ("ln_g", (2, D_MODEL)), ("ln_b", (2, D_MODEL))]


def _shard_shape(shape, axis):
    return tuple(d // N_DEV if a == axis else d for a, d in enumerate(shape))


def _size(shape):
    return math.prod(shape)


BIG_ROWS = sum(_size(s) for _, s, _ in BIG) // N_DEV // LANES
SMALL_ROWS = SUBLANES
SHARD_ROWS = 7936
REPL_ROWS = 808


def _pack_rows(flat_parts, rows):
    flat = jnp.concatenate([p.reshape(-1) for p in flat_parts])
    return jnp.pad(flat, (0, rows * LANES - flat.shape[0])).reshape(rows, LANES)


def _to_chunks(full, axis):
    shape = full.shape
    split = shape[:axis] + (N_DEV, shape[axis] // N_DEV) + shape[axis + 1:]
    return jnp.moveaxis(full.reshape(split), axis, 0).reshape(N_DEV, -1)


def _from_chunks(chunks, shape, axis):
    sh = _shard_shape(shape, axis)
    t = chunks.reshape((N_DEV,) + sh)
    t = jnp.moveaxis(t, 0, axis)
    return t.reshape(shape)


def _split_flat(flat2d, table):
    out, off = [], 0
    for size in table:
        out.append(flat2d[:, off:off + size])
        off += size
    return out


def _win0_to_padded(w):
    z = lambda n: jnp.zeros((w.shape[0], n), w.dtype)
    return jnp.concatenate([w[:, 0:640], w[:, 672:1952], z(KR_LANE), w[:, 640:672],
                            z(LANES - KR_LANE - QK_ROPE)], axis=1)


def _win0_from_padded(wp):
    k0 = Z0_KR + KR_LANE
    return jnp.concatenate([wp[:, 0:640], wp[:, k0:k0 + QK_ROPE], wp[:, 640:1920]], axis=1)


def _pad_heads(w, per_head, lo, hi):
    t = w.reshape(w.shape[0], N_TOK_HEADS, per_head)[:, :, lo:hi]
    t = jnp.pad(t, ((0, 0), (0, 0), (0, HEAD_PAD - (hi - lo))))
    return t.reshape(w.shape[0], QKV_PAD)


def _unpad_heads(wp, width):
    return wp.reshape(wp.shape[0], N_TOK_HEADS, HEAD_PAD)[:, :, :width]


def _block_diag(w):
    eye = jnp.eye(N_TOK_HEADS, dtype=w.dtype)
    return (w[:, :, None, :] * eye[:, None, :, None]).reshape(TOK_WIDTH, TOK_WIDTH)


def _diag_blocks(d):
    t = d.reshape(N_TOK_HEADS, HEAD_DIM, N_TOK_HEADS, HEAD_DIM)
    return jnp.stack([t[g, :, g, :] for g in range(N_TOK_HEADS)])


def _rope_tables(positions):
    half = QK_ROPE // 2
    inv_freq = ROPE_THETA ** (-jnp.arange(half, dtype=F32) / half)
    ang = positions.astype(F32)[:, None] * inv_freq
    cos, sin = jnp.cos(ang), jnp.sin(ang)
    s = positions.shape[0]
    one, zero = jnp.ones((s, QK_NOPE), F32), jnp.zeros((s, half), F32)
    tail = jnp.zeros((s, HEAD_PAD - QK_DIM), F32)
    znope = jnp.zeros((s, QK_NOPE), F32)
    c = jnp.concatenate([one, cos, cos, tail], axis=1)
    sa = jnp.concatenate([znope, -sin, zero, tail], axis=1)
    sb = jnp.concatenate([znope, zero, sin, tail], axis=1)
    return c, sa, sb


def _local_step(x, mem, positions, tgt, wts, ts, tatt):
    bf = lambda t: t.astype(BF16)
    win0 = _win0_to_padded(wts["mla_w_in"])
    wuq = _pad_heads(wts["mla_w_uq"], QK_DIM, 0, QK_DIM)
    wukv = jnp.concatenate([_pad_heads(wts["mla_w_ukv"], 2 * HEAD_DIM, 0, QK_NOPE),
                            _pad_heads(wts["mla_w_ukv"], 2 * HEAD_DIM, QK_NOPE, 2 * HEAD_DIM)],
                           axis=1)
    win1 = wts["lru_w_in"]
    wmkv, wout = wts["w_mem_kv"], wts["w_out"]
    gq = wts["mla_q_norm"].reshape(1, Q_LORA)
    gkv = wts["mla_kv_norm"].reshape(1, KV_LORA)
    ln_g, ln_b = wts["ln_g"], wts["ln_b"]
    wr, wi = bf(_block_diag(wts["lru_w_rgate"])), bf(_block_diag(wts["lru_w_igate"]))
    cw8 = jnp.pad(wts["lru_conv_w"], ((0, SUBLANES - CONV_W), (0, 0)))
    vec8 = jnp.pad(jnp.stack([wts["lru_conv_b"], wts["lru_b_rgate"], wts["lru_b_igate"],
                              wts["lru_lambda"]]), ((0, SUBLANES - 4), (0, 0)))
    tabs = _rope_tables(positions)
    tmem = mem.shape[0]

    z0 = _rowmm(x, win0, "in_proj0", ts)
    q, k, v = _mla_prep_fwd(z0, tabs, gq, gkv, wuq, wukv, ts)
    o, lse = _flash_fwd(q, k, v, tatt)
    mkv0 = _rowmm(mem, wmkv[0], "mem_kv0", tmem)
    cat0, y0 = _gate_mem_fwd(o, z0, mkv0, Z0_GATE, Z0_QMEM, True, "gate_mem_fwd0", ts)
    pre0, h1 = _outproj_ln_fwd(y0, wout[0], x, ln_g[0:1], ln_b[0:1], None, "outproj_ln_fwd0", ts)
    z1 = _rowmm(h1, win1, "in_proj1", ts)
    hs, hprev = _lru_fwd(z1, cw8, vec8, wr, wi, ts)
    mkv1 = _rowmm(mem, wmkv[1], "mem_kv1", tmem)
    cat1, y1 = _gate_mem_fwd(hs, z1, mkv1, Z1_GATE, Z1_QMEM, False, "gate_mem_fwd1", ts)
    pre1, dh2, loss8 = _outproj_ln_fwd(y1, wout[1], h1, ln_g[1:2], ln_b[1:2], tgt,
                                       "outproj_ln_loss", ts)
    loss = loss8[0, 0]

    dpre1, dy1, dwout1, dgb1 = _outproj_ln_bwd(dh2, pre1, ln_g[1:2], y1, wout[1].T,
                                               "outproj_ln_bwd1", ts)
    dzg1, dhs, dmkv1 = _gate_mem_bwd(dy1, cat1, z1, mkv1, Z1_GATE, Z1_QMEM, False,
                                     "gate_mem_bwd1", ts)
    du, dwr, dwi, dvec = _lru_bwd(z1, dhs, hprev, cw8, vec8, wr, wi, wr.T, wi.T, ts)
    dh1, dwin1 = _linear_bwd(h1, [du, dzg1], [Z1_U, Z1_GATE], win1.T, dpre1, "in_proj_bwd1", ts)
    dwmkv1 = _wgrad_small(mem, dmkv1, "mem_kv_bwd1")
    dpre0, dy0, dwout0, dgb0 = _outproj_ln_bwd(dh1, pre0, ln_g[0:1], y0, wout[0].T,
                                               "outproj_ln_bwd0", ts)
    dzg0, do, dmkv0 = _gate_mem_bwd(dy0, cat0, z0, mkv0, Z0_GATE, Z0_QMEM, True,
                                    "gate_mem_bwd0", ts)
    dq, dk, dv = _flash_bwd(q, k, v, o, lse, do, tatt)
    dza, dzk, dwuq_p, dwukv_p, dg = _mla_prep_bwd(z0, dq, dk, dv, tabs, gq, gkv,
                                                  wuq.T, wukv.T, ts)
    gx, dwin0_p = _linear_bwd(x, [dza, dzg0, dzk], [Z0_CQ, Z0_GATE, Z0_KR], win0.T, dpre0,
                              "in_proj_bwd0", ts)
    dwmkv0 = _wgrad_small(mem, dmkv0, "mem_kv_bwd0")

    dwukv = jnp.concatenate([_unpad_heads(dwukv_p[:, :QKV_PAD], HEAD_DIM),
                             _unpad_heads(dwukv_p[:, QKV_PAD:], HEAD_DIM)], axis=2)
    grads = {
        "mla_w_in": _win0_from_padded(dwin0_p),
        "mla_q_norm": dg[0, 0:Q_LORA],
        "mla_w_uq": _unpad_heads(dwuq_p, QK_DIM).reshape(Q_LORA, N_TOK_HEADS * QK_DIM),
        "mla_kv_norm": dg[0, Q_LORA:Q_LORA + KV_LORA],
        "mla_w_ukv": dwukv.reshape(KV_LORA, N_TOK_HEADS * 2 * HEAD_DIM),
        "lru_w_in": dwin1,
        "lru_conv_w": dvec[0:CONV_W],
        "lru_conv_b": dvec[4],
        "lru_w_rgate": _diag_blocks(dwr),
        "lru_b_rgate": dvec[5],
        "lru_w_igate": _diag_blocks(dwi),
        "lru_b_igate": dvec[6],
        "lru_lambda": dvec[7],
        "w_mem_kv": jnp.stack([dwmkv0, dwmkv1]),
        "w_out": jnp.stack([dwout0, dwout1]),
        "ln_g": jnp.stack([dgb0[0], dgb1[0]]),
        "ln_b": jnp.stack([dgb0[1], dgb1[1]]),
    }
    return loss, gx, grads


WEIGHT_ORDER = ["mla_w_in", "mla_q_norm", "mla_w_uq", "mla_kv_norm", "mla_w_ukv", "lru_w_in",
                "lru_conv_w", "lru_conv_b", "lru_w_rgate", "lru_b_rgate", "lru_w_igate",
                "lru_b_igate", "lru_lambda", "w_mem_kv", "w_out", "ln_g", "ln_b"]


def kernel(x, mem, positions, mla_w_in, mla_q_norm, mla_w_uq, mla_kv_norm, mla_w_ukv, lru_w_in, lru_conv_w, lru_conv_b, lru_w_rgate, lru_b_rgate, lru_w_igate, lru_b_igate, lru_lambda, w_mem_kv, w_out, ln_g, ln_b, loss_target, m_mla_w_in, m_mla_q_norm, m_mla_w_uq, m_mla_kv_norm, m_mla_w_ukv, m_lru_w_in, m_lru_conv_w, m_lru_conv_b, m_lru_w_rgate, m_lru_b_rgate, m_lru_w_igate, m_lru_b_igate, m_lru_lambda, m_w_mem_kv, m_w_out, m_ln_g, m_ln_b, v_mla_w_in, v_mla_q_norm, v_mla_w_uq, v_mla_kv_norm, v_mla_w_ukv, v_lru_w_in, v_lru_conv_w, v_lru_conv_b, v_lru_w_rgate, v_lru_b_rgate, v_lru_w_igate, v_lru_b_igate, v_lru_lambda, v_w_mem_kv, v_w_out, v_ln_g, v_ln_b):
    w_in = dict(mla_w_in=mla_w_in, mla_q_norm=mla_q_norm, mla_w_uq=mla_w_uq,
                mla_kv_norm=mla_kv_norm, mla_w_ukv=mla_w_ukv, lru_w_in=lru_w_in,
                lru_conv_w=lru_conv_w, lru_conv_b=lru_conv_b, lru_w_rgate=lru_w_rgate,
                lru_b_rgate=lru_b_rgate, lru_w_igate=lru_w_igate, lru_b_igate=lru_b_igate,
                lru_lambda=lru_lambda, w_mem_kv=w_mem_kv, w_out=w_out, ln_g=ln_g, ln_b=ln_b)
    m_in = dict(mla_w_in=m_mla_w_in, mla_q_norm=m_mla_q_norm, mla_w_uq=m_mla_w_uq,
                mla_kv_norm=m_mla_kv_norm, mla_w_ukv=m_mla_w_ukv, lru_w_in=m_lru_w_in,
                lru_conv_w=m_lru_conv_w, lru_conv_b=m_lru_conv_b, lru_w_rgate=m_lru_w_rgate,
                lru_b_rgate=m_lru_b_rgate, lru_w_igate=m_lru_w_igate, lru_b_igate=m_lru_b_igate,
                lru_lambda=m_lru_lambda, w_mem_kv=m_w_mem_kv, w_out=m_w_out, ln_g=m_ln_g,
                ln_b=m_ln_b)
    v_in = dict(mla_w_in=v_mla_w_in, mla_q_norm=v_mla_q_norm, mla_w_uq=v_mla_w_uq,
                mla_kv_norm=v_mla_kv_norm, mla_w_ukv=v_mla_w_ukv, lru_w_in=v_lru_w_in,
                lru_conv_w=v_lru_conv_w, lru_conv_b=v_lru_conv_b, lru_w_rgate=v_lru_w_rgate,
                lru_b_rgate=v_lru_b_rgate, lru_w_igate=v_lru_w_igate, lru_b_igate=v_lru_b_igate,
                lru_lambda=v_lru_lambda, w_mem_kv=v_w_mem_kv, w_out=v_w_out, ln_g=v_ln_g,
                ln_b=v_ln_b)
    s = x.shape[1]
    ts = min(ROW_BLOCK, s)
    tatt = min(ATT_BLOCK, s)
    sharded = BIG + SMALL
    big_sizes = [_size(sh) // N_DEV for _, sh, _ in BIG]
    small_sizes = [_size(sh) // N_DEV for _, sh, _ in SMALL]

    big_local = _pack_rows([w_in[n] for n, _, _ in BIG], BIG_ROWS).astype(BF16)
    small_local = _pack_rows([w_in[n] for n, _, _ in SMALL], SMALL_ROWS)
    big_all, small_all = _allgather([big_local, small_local])
    wts = {}
    for (n, sh, ax), part in zip(BIG, _split_flat(big_all.reshape(N_DEV, -1), big_sizes)):
        wts[n] = _from_chunks(part, sh, ax)
    for (n, sh, ax), part in zip(SMALL, _split_flat(small_all.reshape(N_DEV, -1), small_sizes)):
        wts[n] = _from_chunks(part, sh, ax)
    for n, sh in REPL:
        wts[n] = w_in[n].reshape(sh)

    loss_local, gx, grads = _local_step(x[0], mem[0], positions[0], loss_target[0], wts, ts, tatt)
    loss = lax.psum(loss_local, ("x", "y", "c"))

    chunks = jnp.concatenate([_to_chunks(grads[n].reshape(sh), ax) for n, sh, ax in sharded],
                             axis=1)
    chunks = jnp.pad(chunks, ((0, 0), (0, SHARD_ROWS * LANES - chunks.shape[1])))
    gs = chunks.reshape(N_DEV, SHARD_ROWS, LANES)
    gr = _pack_rows([grads[n] for n, _ in REPL], REPL_ROWS)
    parts_s, parts_r = _exchange_grads(gs, gr)

    pack_s = lambda d: _pack_rows([d[n] for n, _, _ in sharded], SHARD_ROWS)
    pack_r = lambda d: _pack_rows([d[n] for n, _ in REPL], REPL_ROWS)
    res_s = _adamw(parts_s, pack_s(w_in), pack_s(m_in), pack_s(v_in), "adamw_sharded")
    res_r = _adamw(parts_r, pack_r(w_in), pack_r(m_in), pack_r(v_in), "adamw_replicated")
    outs = [{}, {}, {}, {}]
    for kind in range(4):
        flat = res_s[kind].reshape(1, -1)
        for (n, _, _), part in zip(sharded, _split_flat(flat, big_sizes + small_sizes)):
            outs[kind][n] = part.reshape(w_in[n].shape)
        flat = res_r[kind].reshape(1, -1)
        for (n, sh), part in zip(REPL, _split_flat(flat, [_size(sh) for _, sh in REPL])):
            outs[kind][n] = part.reshape(w_in[n].shape)
    result = [loss, gx.reshape(x.shape)]
    for kind in range(4):
        result += [outs[kind][n] for n in WEIGHT_ORDER]
    return tuple(result)
```

```python
import functools
import math

import jax
import jax.numpy as jnp
from jax import lax
from jax.experimental import pallas as pl
from jax.experimental.pallas import tpu as pltpu

F32 = jnp.float32
BF16 = jnp.bfloat16

D_MODEL = 1024
MEM_LEN = 256
HEAD_DIM = 64
N_TOK_HEADS = 12
N_MEM_HEADS = 4
TOK_WIDTH = 768
MEM_WIDTH = 256
MIX_WIDTH = 1024
Q_LORA = 384
KV_LORA = 256
QK_NOPE = 64
QK_ROPE = 32
QK_DIM = 96
ROPE_THETA = 10000.0
CONV_W = 4
LRU_C = 8.0
ALPHA = (2.0 * 2) ** 0.25
NORM_EPS = 1e-6
MLA_IN = 1952
LRU_IN = 2048
ADAM_LR = 0.001
ADAM_B1 = 0.9
ADAM_B2 = 0.999
ADAM_EPS = 1e-08
ADAM_WD = 0.01
ADAM_STEP = 10

N_DEV = 8
LANES = 128
SUBLANES = 8
HEAD_PAD = 128
QKV_PAD = N_TOK_HEADS * HEAD_PAD
ZP = 2048
Z0_CQ, Z0_CKV, Z0_GATE, Z0_QMEM, Z0_KR = 0, 384, 640, 1664, 1920
KR_LANE = 64
Z1_U, Z1_GATE, Z1_QMEM = 0, 768, 1792

ROW_BLOCK = 512
ATT_BLOCK = 512
FWD_HEADS = 4
BWD_HEADS = 2
VMEM_LIMIT = 56 * 1024 * 1024
NEG_BIG = -1e30
STRIP = 32
LOG2E = math.log2(math.e)


def _cp(n_axes):
    return pltpu.CompilerParams(dimension_semantics=("arbitrary",) * n_axes,
                                vmem_limit_bytes=VMEM_LIMIT)


def _dot(a, b):
    return jnp.dot(a, b, preferred_element_type=F32)


def _dot_nt(a, b):
    return lax.dot_general(a, b, (((1,), (1,)), ((), ())), preferred_element_type=F32)


def _dot_tn(a, b):
    return lax.dot_general(a, b, (((0,), (0,)), ((), ())), preferred_element_type=F32)


def _sigmoid(t):
    return 1.0 / (1.0 + jnp.exp(-t))


def _lane(shape):
    return lax.broadcasted_iota(jnp.int32, shape, len(shape) - 1)


def _full(shape):
    nd = len(shape)
    return pl.BlockSpec(shape, lambda *_: (0,) * nd)


def _rows(ts, width, col=0):
    return pl.BlockSpec((ts, width), lambda i: (i, col))


def _heads(ts):
    return pl.BlockSpec((N_TOK_HEADS, ts, HEAD_PAD), lambda i: (0, i, 0))


def _rowmm(x, w, name, ts):
    s, k = x.shape
    n = w.shape[1]

    def body(x_ref, w_ref, o_ref):
        o_ref[...] = _dot(x_ref[...].astype(BF16), w_ref[...])

    return pl.pallas_call(
        body, grid=(s // ts,),
        in_specs=[_rows(ts, k), _full((k, n))],
        out_specs=_rows(ts, n),
        out_shape=jax.ShapeDtypeStruct((s, n), F32),
        name=name, compiler_params=_cp(1))(x, w)


def _rms_parts(t):
    rs = lax.rsqrt(jnp.mean(t * t, axis=-1, keepdims=True) + NORM_EPS)
    return rs


def _rope(t, c, sa, sb):
    return t * c + pltpu.roll(t, LANES - 16, 1) * sa + pltpu.roll(t, 16, 1) * sb


def _rope_t(d, c, sa, sb):
    return d * c + pltpu.roll(d * sa, 16, 1) + pltpu.roll(d * sb, LANES - 16, 1)


def _mla_prep_fwd(z0, tabs, gq, gkv, wuq, wukv, ts):
    s = z0.shape[0]

    def body(z_ref, c_ref, sa_ref, sb_ref, gq_ref, gkv_ref, wuq_ref, wukv_ref,
             q_ref, k_ref, v_ref):
        cq = z_ref[:, Z0_CQ:Z0_CQ + Q_LORA]
        ckv = z_ref[:, Z0_CKV:Z0_CKV + KV_LORA]
        kr = z_ref[:, Z0_KR:Z0_KR + LANES]
        cqn = cq * _rms_parts(cq) * gq_ref[...]
        ckvn = ckv * _rms_parts(ckv) * gkv_ref[...]
        q = _dot(cqn.astype(BF16), wuq_ref[...])
        kv = _dot(ckvn.astype(BF16), wukv_ref[...])
        c, sa, sb = c_ref[...], sa_ref[...], sb_ref[...]
        krope = _rope(kr, c, sa, sb)
        pad_lane = _lane((ts, HEAD_PAD)) >= HEAD_DIM
        for h in range(N_TOK_HEADS):
            sl = slice(h * HEAD_PAD, (h + 1) * HEAD_PAD)
            q_ref[h] = _rope(q[:, sl], c, sa, sb).astype(BF16)
            k_ref[h] = (kv[:, sl] + krope).astype(BF16)
            vh = kv[:, QKV_PAD + h * HEAD_PAD:QKV_PAD + (h + 1) * HEAD_PAD]
            v_ref[h] = jnp.where(pad_lane, 1.0, vh).astype(BF16)

    out = jax.ShapeDtypeStruct((N_TOK_HEADS, s, HEAD_PAD), BF16)
    return pl.pallas_call(
        body, grid=(s // ts,),
        in_specs=[_rows(ts, ZP), _rows(ts, LANES), _rows(ts, LANES), _rows(ts, LANES),
                  _full((1, Q_LORA)), _full((1, KV_LORA)),
                  _full((Q_LORA, QKV_PAD)), _full((KV_LORA, 2 * QKV_PAD))],
        out_specs=[_heads(ts)] * 3,
        out_shape=[out, out, out],
        name="mla_prep_fwd", compiler_params=_cp(1))(z0, *tabs, gq, gkv, wuq, wukv)


def _mla_prep_bwd(z0, dq, dk, dv, tabs, gq, gkv, wuq_t, wukv_t, ts):
    s = z0.shape[0]

    def body(z_ref, dq_ref, dk_ref, dv_ref, c_ref, sa_ref, sb_ref, gq_ref, gkv_ref,
             wuqt_ref, wukvt_ref, dza_ref, dzk_ref, dwuq_ref, dwukv_ref, dg_ref):
        @pl.when(pl.program_id(0) == 0)
        def _():
            dwuq_ref[...] = jnp.zeros_like(dwuq_ref)
            dwukv_ref[...] = jnp.zeros_like(dwukv_ref)
            dg_ref[...] = jnp.zeros_like(dg_ref)

        cq = z_ref[:, Z0_CQ:Z0_CQ + Q_LORA]
        ckv = z_ref[:, Z0_CKV:Z0_CKV + KV_LORA]
        rq, rkv = _rms_parts(cq), _rms_parts(ckv)
        gq_, gkv_ = gq_ref[...], gkv_ref[...]
        cqn = (cq * rq * gq_).astype(BF16)
        ckvn = (ckv * rkv * gkv_).astype(BF16)
        c, sa, sb = c_ref[...], sa_ref[...], sb_ref[...]
        dqp, dksum = [], None
        for h in range(N_TOK_HEADS):
            dqp.append(_rope_t(dq_ref[h], c, sa, sb))
            dksum = dk_ref[h] if dksum is None else dksum + dk_ref[h]
        dqp = jnp.concatenate(dqp, axis=1).astype(BF16)
        lane = _lane(dksum.shape)
        dzk_ref[...] = jnp.where((lane >= KR_LANE) & (lane < KR_LANE + QK_ROPE),
                                 _rope_t(dksum, c, sa, sb), 0.0)
        dkv = jnp.concatenate([dk_ref[h] for h in range(N_TOK_HEADS)]
                              + [dv_ref[h] for h in range(N_TOK_HEADS)], axis=1).astype(BF16)
        dcqn = _dot(dqp, wuqt_ref[...])
        dckvn = _dot(dkv, wukvt_ref[...])
        dwuq_ref[...] += _dot_tn(cqn, dqp)
        dwukv_ref[...] += _dot_tn(ckvn, dkv)
        dg_ref[0:1, 0:Q_LORA] += jnp.sum(dcqn * cq * rq, axis=0, keepdims=True)
        dg_ref[0:1, Q_LORA:Q_LORA + KV_LORA] += jnp.sum(dckvn * ckv * rkv, axis=0, keepdims=True)
        wq = dcqn * gq_
        wkv = dckvn * gkv_
        dcq = rq * wq - cq * (rq * rq * rq) * jnp.mean(wq * cq, axis=-1, keepdims=True)
        dckv = rkv * wkv - ckv * (rkv * rkv * rkv) * jnp.mean(wkv * ckv, axis=-1, keepdims=True)
        dza_ref[:, 0:Q_LORA] = dcq
        dza_ref[:, Q_LORA:Q_LORA + KV_LORA] = dckv

    na = Q_LORA + KV_LORA
    return pl.pallas_call(
        body, grid=(s // ts,),
        in_specs=[_rows(ts, ZP), _heads(ts), _heads(ts), _heads(ts),
                  _rows(ts, LANES), _rows(ts, LANES), _rows(ts, LANES),
                  _full((1, Q_LORA)), _full((1, KV_LORA)),
                  _full((QKV_PAD, Q_LORA)), _full((2 * QKV_PAD, KV_LORA))],
        out_specs=[_rows(ts, na), _rows(ts, LANES), _full((Q_LORA, QKV_PAD)),
                   _full((KV_LORA, 2 * QKV_PAD)), _full((SUBLANES, na))],
        out_shape=[jax.ShapeDtypeStruct((s, na), F32), jax.ShapeDtypeStruct((s, LANES), F32),
                   jax.ShapeDtypeStruct((Q_LORA, QKV_PAD), F32),
                   jax.ShapeDtypeStruct((KV_LORA, 2 * QKV_PAD), F32),
                   jax.ShapeDtypeStruct((SUBLANES, na), F32)],
        name="mla_prep_bwd", compiler_params=_cp(1))(
            z0, dq, dk, dv, *tabs, gq, gkv, wuq_t, wukv_t)


def _flash_fwd(q, k, v, tq, tk, nh):
    s = q.shape[1]
    nq, nk = s // tq, s // tk
    ratio = tq // tk
    c2 = LOG2E / math.sqrt(QK_DIM)
    nch = tk // LANES

    def body(q_ref, k_ref, v_ref, o_ref, lse_ref, m_scr, acc_scr):
        i, j = pl.program_id(1), pl.program_id(2)
        j_last = i * ratio + ratio - 1

        @pl.when(j == 0)
        def _():
            m_scr[...] = jnp.full_like(m_scr, NEG_BIG)
            acc_scr[...] = jnp.zeros_like(acc_scr)

        def softmax_strips(masked, hs, sc):
            ps, als = [], []
            for r0 in range(0, tq, STRIP):
                rows = slice(r0, r0 + STRIP)
                ch = [sc[rows, n * LANES:(n + 1) * LANES] * c2 for n in range(nch)]
                if masked:
                    rr = i * tq + r0 + lax.broadcasted_iota(jnp.int32, (STRIP, LANES), 0)
                    cc = j * tk + lax.broadcasted_iota(jnp.int32, (STRIP, LANES), 1)
                    ch = [jnp.where(cc + n * LANES <= rr, ch[n], NEG_BIG) for n in range(nch)]
                mx = ch[0]
                for n in range(1, nch):
                    mx = jnp.maximum(mx, ch[n])
                m_prev = m_scr[hs, rows, :]
                m_next = jnp.maximum(m_prev, jnp.max(mx, axis=-1, keepdims=True))
                ps.append(jnp.concatenate(
                    [jnp.exp2(ch[n] - m_next).astype(BF16) for n in range(nch)], axis=1))
                als.append(jnp.exp2(m_prev - m_next))
                m_scr[hs, rows, :] = m_next
            return jnp.concatenate(ps, axis=0), jnp.concatenate(als, axis=0)

        def step(masked):
            scores = [_dot_nt(q_ref[hs], k_ref[hs]) for hs in range(nh)]
            for hs, sc in enumerate(scores):
                p, alpha = softmax_strips(masked, hs, sc)
                acc_scr[hs] = alpha * acc_scr[hs] + _dot(p, v_ref[hs])

        unmasked = (j + 1) * tk - 1 <= i * tq

        @pl.when(unmasked)
        def _():
            step(False)

        @pl.when(jnp.logical_and(jnp.logical_not(unmasked), j <= j_last))
        def _():
            step(True)

        @pl.when(j == j_last)
        def _():
            for h in range(nh):
                acc = acc_scr[h]
                l = acc[:, HEAD_DIM:HEAD_DIM + 1]
                o_ref[h] = jnp.where(_lane(acc.shape) < HEAD_DIM, acc / l, 0.0)
                lse_ref[h] = m_scr[h] + jnp.log2(l)

    qspec = pl.BlockSpec((nh, tq, HEAD_PAD), lambda h, i, j: (h, i, 0))
    kspec = pl.BlockSpec((nh, tk, HEAD_PAD),
                         lambda h, i, j: (h, jnp.minimum(j, i * ratio + ratio - 1), 0))
    out = jax.ShapeDtypeStruct((N_TOK_HEADS, s, HEAD_PAD), F32)
    return pl.pallas_call(
        body, grid=(N_TOK_HEADS // nh, nq, nk),
        in_specs=[qspec, kspec, kspec],
        out_specs=[qspec, qspec],
        out_shape=[out, out],
        scratch_shapes=[pltpu.VMEM((nh, tq, HEAD_PAD), F32)] * 2,
        name="flash_fwd", compiler_params=_cp(3))(q, k, v)


def _flash_bwd(q, k, v, stats, do, t, nh):
    s = q.shape[1]
    nb = s // t
    scale = 1.0 / math.sqrt(QK_DIM)
    c2 = LOG2E * scale
    nch = t // LANES

    def body(q_ref, k_ref, v_ref, st_ref, do_ref, dq_ref, dk_ref, dv_ref, dk_scr, dv_scr):
        j, i = pl.program_id(1), pl.program_id(2)
        rows_i = pl.ds(pl.multiple_of(i * t, t), t)

        @pl.when(i == j)
        def _():
            dk_scr[...] = jnp.zeros_like(dk_scr)
            dv_scr[...] = jnp.zeros_like(dv_scr)

        @pl.when(j == 0)
        def _():
            dq_ref[:, rows_i, :] = jnp.zeros((nh, t, HEAD_PAD), F32)

        def prob_strips(masked, h, sc, dp):
            ps, dss = [], []
            low = _lane((STRIP, LANES)) < HEAD_DIM
            for r0 in range(0, t, STRIP):
                rows = slice(r0, r0 + STRIP)
                st = st_ref[h, rows, :]
                swapped = pltpu.roll(st, HEAD_DIM, 1)
                lse = jnp.where(low, st, swapped)
                delta = jnp.where(low, swapped, st)
                if masked:
                    rr = r0 + lax.broadcasted_iota(jnp.int32, (STRIP, LANES), 0)
                    cc = lax.broadcasted_iota(jnp.int32, (STRIP, LANES), 1)
                pcs, dcs = [], []
                for n in range(nch):
                    cols = slice(n * LANES, (n + 1) * LANES)
                    x = sc[rows, cols] * c2
                    if masked:
                        x = jnp.where(cc + n * LANES <= rr, x, NEG_BIG)
                    p = jnp.exp2(x - lse)
                    pcs.append(p.astype(BF16))
                    dcs.append((p * (dp[rows, cols] - delta) * scale).astype(BF16))
                ps.append(jnp.concatenate(pcs, axis=1))
                dss.append(jnp.concatenate(dcs, axis=1))
            return jnp.concatenate(ps, axis=0), jnp.concatenate(dss, axis=0)

        def step(masked):
            scs = [_dot_nt(q_ref[h], k_ref[h]) for h in range(nh)]
            dps = [_dot_nt(do_ref[h], v_ref[h]) for h in range(nh)]
            for h in range(nh):
                p, ds = prob_strips(masked, h, scs[h], dps[h])
                dv_scr[h] += _dot_tn(p, do_ref[h])
                dk_scr[h] += _dot_tn(ds, q_ref[h])
                dq_ref[h, rows_i, :] += _dot(ds, k_ref[h])

        @pl.when(i > j)
        def _():
            step(False)

        @pl.when(i == j)
        def _():
            step(True)

        @pl.when(i == nb - 1)
        def _():
            dk_ref[...] = dk_scr[...]
            dv_ref[...] = dv_scr[...]

    qspec = pl.BlockSpec((nh, t, HEAD_PAD), lambda h, j, i: (h, jnp.maximum(i, j), 0))
    kspec = pl.BlockSpec((nh, t, HEAD_PAD), lambda h, j, i: (h, j, 0))
    dqspec = pl.BlockSpec((nh, s, HEAD_PAD), lambda h, j, i: (h, 0, 0))
    out = jax.ShapeDtypeStruct((N_TOK_HEADS, s, HEAD_PAD), F32)
    return pl.pallas_call(
        body, grid=(N_TOK_HEADS // nh, nb, nb),
        in_specs=[qspec, kspec, kspec, qspec, qspec],
        out_specs=[dqspec, kspec, kspec],
        out_shape=[out, out, out],
        scratch_shapes=[pltpu.VMEM((nh, t, HEAD_PAD), F32)] * 2,
        name="flash_bwd", compiler_params=_cp(3))(q, k, v, stats, do)


def _mem_probs(qp, kp, hh):
    lane = _lane(qp.shape)
    keep = (lane < HEAD_DIM) if hh == 0 else (lane >= HEAD_DIM)
    qh = jnp.where(keep, qp, 0.0).astype(BF16)
    sc = _dot_nt(qh, kp) * (1.0 / math.sqrt(HEAD_DIM))
    e = jnp.exp(sc - jnp.max(sc, axis=-1, keepdims=True))
    return e / jnp.sum(e, axis=-1, keepdims=True), keep


def _gate_mem_fwd(tok, z, memkv, g0, q0, padded, name, ts):
    s = z.shape[0]
    zw = z.shape[1]
    tok_spec = _heads(ts) if padded else _rows(ts, TOK_WIDTH)

    def body(tok_ref, z_ref, mkv_ref, cat_ref, y_ref):
        if padded:
            for p in range(N_TOK_HEADS // 2):
                cat_ref[:, p * LANES:(p + 1) * LANES] = (
                    tok_ref[2 * p] + pltpu.roll(tok_ref[2 * p + 1], HEAD_DIM, 1))
        else:
            cat_ref[:, 0:TOK_WIDTH] = tok_ref[...]
        for pr in range(N_MEM_HEADS // 2):
            sl = slice(pr * LANES, (pr + 1) * LANES)
            qp = z_ref[:, q0 + pr * LANES:q0 + (pr + 1) * LANES]
            kp = mkv_ref[:, sl].astype(BF16)
            vp = mkv_ref[:, MEM_WIDTH + pr * LANES:MEM_WIDTH + (pr + 1) * LANES].astype(BF16)
            outs = []
            for hh in range(2):
                p, _ = _mem_probs(qp, kp, hh)
                outs.append(_dot(p.astype(BF16), vp))
            lane = _lane(outs[0].shape)
            cat_ref[:, TOK_WIDTH + pr * LANES:TOK_WIDTH + (pr + 1) * LANES] = jnp.where(
                lane < HEAD_DIM, outs[0], outs[1])
        gate = z_ref[:, g0:g0 + MIX_WIDTH]
        y_ref[...] = cat_ref[...] * (gate * _sigmoid(gate))

    out = jax.ShapeDtypeStruct((s, MIX_WIDTH), F32)
    return pl.pallas_call(
        body, grid=(s // ts,),
        in_specs=[tok_spec, _rows(ts, zw), _full((MEM_LEN, 2 * MEM_WIDTH))],
        out_specs=[_rows(ts, MIX_WIDTH)] * 2,
        out_shape=[out, out],
        name=name, compiler_params=_cp(1))(tok, z, memkv)


def _gate_mem_bwd(dy, cat, z, memkv, lse, g0, q0, name, ts):
    s = z.shape[0]
    zw = z.shape[1]
    padded = lse is not None
    gq_w = MIX_WIDTH + MEM_WIDTH

    def body(*refs):
        if padded:
            dy_ref, cat_ref, z_ref, mkv_ref, lse_ref, dzg_ref, dtok_ref, dmkv_ref, st_ref = refs
        else:
            dy_ref, cat_ref, z_ref, mkv_ref, dzg_ref, dtok_ref, dmkv_ref = refs

        @pl.when(pl.program_id(0) == 0)
        def _():
            dmkv_ref[...] = jnp.zeros_like(dmkv_ref)

        gate = z_ref[:, g0:g0 + MIX_WIDTH]
        sg = _sigmoid(gate)
        dy_ = dy_ref[...]
        dzg_ref[:, 0:MIX_WIDTH] = dy_ * cat_ref[...] * (sg * (1.0 + gate * (1.0 - sg)))
        dcat = dy_ * (gate * sg)
        if padded:
            low = _lane((ts, LANES)) < HEAD_DIM
            for p in range(N_TOK_HEADS // 2):
                d = dcat[:, p * LANES:(p + 1) * LANES]
                prod = d * cat_ref[:, p * LANES:(p + 1) * LANES]
                first = jnp.sum(jnp.where(low, prod, 0.0), axis=-1, keepdims=True)
                second = jnp.sum(jnp.where(low, 0.0, prod), axis=-1, keepdims=True)
                dtok_ref[2 * p] = jnp.where(low, d, 0.0).astype(BF16)
                dtok_ref[2 * p + 1] = jnp.where(low, pltpu.roll(d, HEAD_DIM, 1), 0.0).astype(BF16)
                st_ref[2 * p] = jnp.where(low, lse_ref[2 * p], first)
                st_ref[2 * p + 1] = jnp.where(low, lse_ref[2 * p + 1], second)
        else:
            dtok_ref[...] = dcat[:, 0:TOK_WIDTH]
        for pr in range(N_MEM_HEADS // 2):
            sl = slice(pr * LANES, (pr + 1) * LANES)
            vsl = slice(MEM_WIDTH + pr * LANES, MEM_WIDTH + (pr + 1) * LANES)
            qp = z_ref[:, q0 + pr * LANES:q0 + (pr + 1) * LANES]
            qpb = qp.astype(BF16)
            kp = mkv_ref[:, sl].astype(BF16)
            vp = mkv_ref[:, vsl].astype(BF16)
            dmo = dcat[:, TOK_WIDTH + pr * LANES:TOK_WIDTH + (pr + 1) * LANES]
            dqp = None
            for hh in range(2):
                p, keep = _mem_probs(qp, kp, hh)
                do_h = jnp.where(keep, dmo, 0.0).astype(BF16)
                dmkv_ref[:, vsl] += _dot_tn(p.astype(BF16), do_h)
                dp = _dot_nt(do_h, vp)
                ds = (p * (dp - jnp.sum(dp * p, axis=-1, keepdims=True))
                      * (1.0 / math.sqrt(HEAD_DIM))).astype(BF16)
                dqh = jnp.where(keep, _dot(ds, kp), 0.0)
                dqp = dqh if dqp is None else dqp + dqh
                dkh = _dot_tn(ds, qpb)
                klane = _lane(dkh.shape)
                kkeep = (klane < HEAD_DIM) if hh == 0 else (klane >= HEAD_DIM)
                dmkv_ref[:, sl] += jnp.where(kkeep, dkh, 0.0)
            dzg_ref[:, MIX_WIDTH + pr * LANES:MIX_WIDTH + (pr + 1) * LANES] = dqp

    in_specs = [_rows(ts, MIX_WIDTH), _rows(ts, MIX_WIDTH), _rows(ts, zw),
                _full((MEM_LEN, 2 * MEM_WIDTH))]
    out_specs = [_rows(ts, gq_w), _heads(ts) if padded else _rows(ts, TOK_WIDTH),
                 _full((MEM_LEN, 2 * MEM_WIDTH))]
    heads_shape = (N_TOK_HEADS, s, HEAD_PAD)
    out_shape = [jax.ShapeDtypeStruct((s, gq_w), F32),
                 jax.ShapeDtypeStruct(heads_shape, BF16) if padded
                 else jax.ShapeDtypeStruct((s, TOK_WIDTH), F32),
                 jax.ShapeDtypeStruct((MEM_LEN, 2 * MEM_WIDTH), F32)]
    args = [dy, cat, z, memkv]
    if padded:
        in_specs.append(_heads(ts))
        out_specs.append(_heads(ts))
        out_shape.append(jax.ShapeDtypeStruct(heads_shape, F32))
        args.append(lse)
    return pl.pallas_call(
        body, grid=(s // ts,), in_specs=in_specs, out_specs=out_specs, out_shape=out_shape,
        name=name, compiler_params=_cp(1))(*args)


def _ln_stats(pre):
    mu = jnp.mean(pre, axis=-1, keepdims=True)
    d = pre - mu
    rstd = lax.rsqrt(jnp.mean(d * d, axis=-1, keepdims=True) + NORM_EPS)
    return d * rstd, rstd


def _outproj_ln_fwd(y, w, h, g, b, tgt, name, ts):
    s = y.shape[0]
    with_loss = tgt is not None

    def body(*refs):
        if with_loss:
            y_ref, w_ref, h_ref, g_ref, b_ref, t_ref, pre_ref, out_ref, loss_ref = refs
        else:
            y_ref, w_ref, h_ref, g_ref, b_ref, pre_ref, out_ref = refs
        pre = ALPHA * h_ref[...] + _dot(y_ref[...].astype(BF16), w_ref[...])
        pre_ref[...] = pre
        xhat, _ = _ln_stats(pre)
        hout = xhat * g_ref[...] + b_ref[...]
        if with_loss:
            @pl.when(pl.program_id(0) == 0)
            def _():
                loss_ref[...] = jnp.zeros_like(loss_ref)
            err = hout - t_ref[...]
            out_ref[...] = err * (1.0 / D_MODEL)
            loss_ref[...] += 0.5 * jnp.sum(jnp.mean(err * err, axis=-1, keepdims=True))
        else:
            out_ref[...] = hout

    act = jax.ShapeDtypeStruct((s, D_MODEL), F32)
    in_specs = [_rows(ts, MIX_WIDTH), _full((MIX_WIDTH, D_MODEL)), _rows(ts, D_MODEL),
                _full((1, D_MODEL)), _full((1, D_MODEL))]
    out_specs = [_rows(ts, D_MODEL)] * 2
    out_shape = [act, act]
    args = [y, w, h, g, b]
    if with_loss:
        in_specs.append(_rows(ts, D_MODEL))
        out_specs.append(_full((SUBLANES, LANES)))
        out_shape.append(jax.ShapeDtypeStruct((SUBLANES, LANES), F32))
        args.append(tgt)
    return pl.pallas_call(
        body, grid=(s // ts,), in_specs=in_specs, out_specs=out_specs, out_shape=out_shape,
        name=name, compiler_params=_cp(1))(*args)


def _outproj_ln_bwd(dh, pre, g, y, w_t, name, ts):
    s = y.shape[0]

    def body(dh_ref, pre_ref, g_ref, y_ref, wt_ref, dpre_ref, dy_ref, dw_ref, dgb_ref):
        @pl.when(pl.program_id(0) == 0)
        def _():
            dw_ref[...] = jnp.zeros_like(dw_ref)
            dgb_ref[...] = jnp.zeros_like(dgb_ref)

        dh_ = dh_ref[...]
        xhat, rstd = _ln_stats(pre_ref[...])
        dxh = dh_ * g_ref[...]
        dpre = rstd * (dxh - jnp.mean(dxh, axis=-1, keepdims=True)
                       - xhat * jnp.mean(dxh * xhat, axis=-1, keepdims=True))
        dpre_ref[...] = dpre
        dgb_ref[0:1, :] += jnp.sum(dh_ * xhat, axis=0, keepdims=True)
        dgb_ref[1:2, :] += jnp.sum(dh_, axis=0, keepdims=True)
        dpb = dpre.astype(BF16)
        dy_ref[...] = _dot(dpb, wt_ref[...])
        dw_ref[...] += _dot_tn(y_ref[...].astype(BF16), dpb)

    act = jax.ShapeDtypeStruct((s, D_MODEL), F32)
    return pl.pallas_call(
        body, grid=(s // ts,),
        in_specs=[_rows(ts, D_MODEL), _rows(ts, D_MODEL), _full((1, D_MODEL)),
                  _rows(ts, MIX_WIDTH), _full((D_MODEL, MIX_WIDTH))],
        out_specs=[_rows(ts, D_MODEL), _rows(ts, MIX_WIDTH), _full((MIX_WIDTH, D_MODEL)),
                   _full((SUBLANES, D_MODEL))],
        out_shape=[act, act, jax.ShapeDtypeStruct((MIX_WIDTH, D_MODEL), F32),
                   jax.ShapeDtypeStruct((SUBLANES, D_MODEL), F32)],
        name=name, compiler_params=_cp(1))(dh, pre, g, y, w_t)


def _linear_bwd(x, dys, offs, w_t, resid, name, ts):
    s, kdim = x.shape
    n = w_t.shape[0]
    widths = [d.shape[1] for d in dys]
    npieces = len(dys)

    def body(*refs):
        x_ref = refs[0]
        dy_refs = refs[1:1 + npieces]
        wt_ref, r_ref, dx_ref, dw_ref = refs[1 + npieces:]

        @pl.when(pl.program_id(0) == 0)
        def _():
            dw_ref[...] = jnp.zeros_like(dw_ref)

        xb = x_ref[...].astype(BF16)
        dx = ALPHA * r_ref[...]
        for dy_ref, off, wd in zip(dy_refs, offs, widths):
            dyb = dy_ref[...].astype(BF16)
            dx = dx + _dot(dyb, wt_ref[off:off + wd, :])
            dw_ref[:, off:off + wd] += _dot_tn(xb, dyb)
        dx_ref[...] = dx

    return pl.pallas_call(
        body, grid=(s // ts,),
        in_specs=[_rows(ts, kdim)] + [_rows(ts, wd) for wd in widths]
                 + [_full((n, kdim)), _rows(ts, kdim)],
        out_specs=[_rows(ts, kdim), _full((kdim, n))],
        out_shape=[jax.ShapeDtypeStruct((s, kdim), F32), jax.ShapeDtypeStruct((kdim, n), F32)],
        name=name, compiler_params=_cp(1))(x, *dys, w_t, resid)


def _wgrad_small(x, dy, name):
    def body(x_ref, dy_ref, dw_ref):
        dw_ref[...] = _dot_tn(x_ref[...].astype(BF16), dy_ref[...].astype(BF16))

    return pl.pallas_call(
        body, out_shape=jax.ShapeDtypeStruct((x.shape[1], dy.shape[1]), F32),
        name=name, compiler_params=pltpu.CompilerParams(vmem_limit_bytes=VMEM_LIMIT))(x, dy)


def _shift_down(u, carry8, k):
    if k == 0:
        return u
    rolled = pltpu.roll(u, k, 0)
    row = lax.broadcasted_iota(jnp.int32, carry8.shape, 0)
    top = jnp.where(row < k, pltpu.roll(carry8, k, 0), rolled[0:SUBLANES])
    return jnp.concatenate([top, rolled[SUBLANES:]], axis=0)


def _shift_up(u, carry8, k):
    if k == 0:
        return u
    n = u.shape[0]
    rolled = pltpu.roll(u, n - k, 0)
    row = lax.broadcasted_iota(jnp.int32, carry8.shape, 0)
    bot = jnp.where(row >= SUBLANES - k, pltpu.roll(carry8, SUBLANES - k, 0),
                    rolled[n - SUBLANES:])
    return jnp.concatenate([rolled[:n - SUBLANES], bot], axis=0)


def _neg_expm1(t):
    e = jnp.exp(t)
    em1 = e - 1.0
    safe = jnp.where(e == 1.0, 1.0, jnp.log(e))
    return -jnp.where(e == 1.0, t, jnp.where(em1 == -1.0, -1.0, em1 * t / safe))


def _lru_gates(u, carry8, cw_ref, vec_ref, wr_ref, wi_ref):
    xc = vec_ref[0:1, :] + cw_ref[3:4, :] * u
    for k in range(1, CONV_W):
        xc = xc + cw_ref[3 - k:4 - k, :] * _shift_down(u, carry8, k)
    xb = xc.astype(BF16)
    r = _sigmoid(_dot(xb, wr_ref[...]) + vec_ref[1:2, :])
    ig = _sigmoid(_dot(xb, wi_ref[...]) + vec_ref[2:3, :])
    nlam = -vec_ref[3:4, :]
    softplus = jnp.maximum(nlam, 0.0) + jnp.log(1.0 + jnp.exp(-jnp.abs(nlam)))
    cneg = -LRU_C * softplus
    log_a = cneg * r
    a = jnp.exp(log_a)
    sq = jnp.sqrt(_neg_expm1(2.0 * log_a))
    return xc, r, ig, cneg, a, sq


def _lru_fwd(z1, cw8, vec8, wr, wi, ts):
    s = z1.shape[0]

    def body(u_ref, cw_ref, vec_ref, wr_ref, wi_ref, hs_ref, hp_ref,
             cu_scr, ch_scr, a_scr, gx_scr):
        @pl.when(pl.program_id(0) == 0)
        def _():
            cu_scr[...] = jnp.zeros_like(cu_scr)
            ch_scr[...] = jnp.zeros_like(ch_scr)

        u = u_ref[...]
        xc, _, ig, _, a, sq = _lru_gates(u, cu_scr[...], cw_ref, vec_ref, wr_ref, wi_ref)
        a_scr[...] = a
        gx_scr[...] = sq * (ig * xc)

        def step(t, h):
            hp_ref[pl.ds(t, 1), :] = h
            h = a_scr[pl.ds(t, 1), :] * h + gx_scr[pl.ds(t, 1), :]
            hs_ref[pl.ds(t, 1), :] = h
            return h

        h = lax.fori_loop(0, ts, step, ch_scr[0:1, :], unroll=8)
        ch_scr[0:1, :] = h
        cu_scr[...] = u[ts - SUBLANES:, :]

    w = TOK_WIDTH
    out = jax.ShapeDtypeStruct((s, w), F32)
    return pl.pallas_call(
        body, grid=(s // ts,),
        in_specs=[_rows(ts, w), _full((SUBLANES, w)), _full((SUBLANES, w)),
                  _full((w, w)), _full((w, w))],
        out_specs=[_rows(ts, w)] * 2,
        out_shape=[out, out],
        scratch_shapes=[pltpu.VMEM((SUBLANES, w), F32), pltpu.VMEM((SUBLANES, w), F32),
                        pltpu.VMEM((ts, w), F32), pltpu.VMEM((ts, w), F32)],
        name="lru_fwd", compiler_params=_cp(1))(z1, cw8, vec8, wr, wi)


def _lru_bwd(z1, dhs, hprev, cw8, vec8, wr, wi, wr_t, wi_t, ts):
    s = z1.shape[0]
    nb = s // ts
    w = TOK_WIDTH
    tiles = ts // SUBLANES

    def body(u_ref, up_ref, dhs_ref, hp_ref, cw_ref, vec_ref, wr_ref, wi_ref, wrt_ref, wit_ref,
             du_ref, dwr_ref, dwi_ref, dvec_ref, cc_scr, cd_scr, a_scr, dh_scr):
        i = pl.program_id(0)

        @pl.when(i == 0)
        def _():
            cc_scr[...] = jnp.zeros_like(cc_scr)
            cd_scr[...] = jnp.zeros_like(cd_scr)
            dwr_ref[...] = jnp.zeros_like(dwr_ref)
            dwi_ref[...] = jnp.zeros_like(dwi_ref)
            dvec_ref[...] = jnp.zeros_like(dvec_ref)

        u = u_ref[...]
        carry8 = jnp.where(i == nb - 1, 0.0, up_ref[...])
        xc, r, ig, cneg, a, sq = _lru_gates(u, carry8, cw_ref, vec_ref, wr_ref, wi_ref)
        a_scr[...] = a

        def step(n, c):
            t = ts - 1 - n
            dh = dhs_ref[pl.ds(t, 1), :] + c
            dh_scr[pl.ds(t, 1), :] = dh
            return a_scr[pl.ds(t, 1), :] * dh

        cc_scr[0:1, :] = lax.fori_loop(0, ts, step, cc_scr[0:1, :], unroll=8)
        dh = dh_scr[...]
        ix = ig * xc
        dix = dh * sq
        dlog_a = dh * hp_ref[...] * a - (dh * ix) * (a * a) / sq
        dpr = (dlog_a * cneg) * r * (1.0 - r)
        dpi = (dix * xc) * ig * (1.0 - ig)
        dprb, dpib = dpr.astype(BF16), dpi.astype(BF16)
        xb = xc.astype(BF16)
        dwr_ref[...] += _dot_tn(xb, dprb)
        dwi_ref[...] += _dot_tn(xb, dpib)
        dxc = dix * ig + _dot(dprb, wrt_ref[...]) + _dot(dpib, wit_ref[...])
        for k in range(CONV_W):
            dvec_ref[3 - k:4 - k, :] += jnp.sum(dxc * _shift_down(u, carry8, k),
                                                axis=0, keepdims=True)
        dvec_ref[4:5, :] += jnp.sum(dxc, axis=0, keepdims=True)
        dvec_ref[5:6, :] += jnp.sum(dpr, axis=0, keepdims=True)
        dvec_ref[6:7, :] += jnp.sum(dpi, axis=0, keepdims=True)
        dvec_ref[7:8, :] += (jnp.sum(dlog_a * r, axis=0, keepdims=True)
                             * (LRU_C * _sigmoid(-vec_ref[3:4, :])))
        nxt = cd_scr[...]
        du = cw_ref[3:4, :] * dxc
        for k in range(1, CONV_W):
            du = du + cw_ref[3 - k:4 - k, :] * _shift_up(dxc, nxt, k)
        du_ref[...] = du
        cd_scr[...] = dxc[0:SUBLANES, :]

    rev = lambda i: (nb - 1 - i, 0)
    prev8 = lambda i: (jnp.maximum((nb - 1 - i) * tiles - 1, 0), 0)
    blk = pl.BlockSpec((ts, w), rev)
    return pl.pallas_call(
        body, grid=(nb,),
        in_specs=[blk, pl.BlockSpec((SUBLANES, w), prev8), blk, blk,
                  _full((SUBLANES, w)), _full((SUBLANES, w)),
                  _full((w, w)), _full((w, w)), _full((w, w)), _full((w, w))],
        out_specs=[blk, _full((w, w)), _full((w, w)), _full((SUBLANES, w))],
        out_shape=[jax.ShapeDtypeStruct((s, w), F32), jax.ShapeDtypeStruct((w, w), F32),
                   jax.ShapeDtypeStruct((w, w), F32), jax.ShapeDtypeStruct((SUBLANES, w), F32)],
        scratch_shapes=[pltpu.VMEM((SUBLANES, w), F32), pltpu.VMEM((SUBLANES, w), F32),
                        pltpu.VMEM((ts, w), F32), pltpu.VMEM((ts, w), F32)],
        name="lru_bwd", compiler_params=_cp(1))(
            z1, z1, dhs, hprev, cw8, vec8, wr, wi, wr_t, wi_t)


def _row_block(rows, cap=1024):
    best = SUBLANES
    for cand in range(SUBLANES, min(rows, cap) + 1, SUBLANES):
        if rows % cand == 0:
            best = cand
    return best


def _adamw(parts, w, m, v, name):
    rows = w.shape[0]
    tr = _row_block(rows)

    def body(p_ref, w_ref, m_ref, v_ref, g_ref, d_ref, nm_ref, nv_ref):
        g = p_ref[0]
        for dev in range(1, N_DEV):
            g = g + p_ref[dev]
        g_ref[...] = g
        nm = ADAM_B1 * m_ref[...] + (1.0 - ADAM_B1) * g
        nv = ADAM_B2 * v_ref[...] + (1.0 - ADAM_B2) * (g * g)
        m_hat = nm / (1.0 - ADAM_B1 ** ADAM_STEP)
        v_hat = nv / (1.0 - ADAM_B2 ** ADAM_STEP)
        d_ref[...] = -ADAM_LR * (m_hat / (jnp.sqrt(v_hat) + ADAM_EPS) + ADAM_WD * w_ref[...])
        nm_ref[...] = nm
        nv_ref[...] = nv

    blk = pl.BlockSpec((tr, LANES), lambda i: (i, 0))
    out = jax.ShapeDtypeStruct((rows, LANES), F32)
    return pl.pallas_call(
        body, grid=(rows // tr,),
        in_specs=[pl.BlockSpec((N_DEV, tr, LANES), lambda i: (0, i, 0)), blk, blk, blk],
        out_specs=[blk] * 4, out_shape=[out] * 4,
        name=name, compiler_params=_cp(1))(parts, w, m, v)


ANY = pl.BlockSpec(memory_space=pl.ANY)
MESH = pl.DeviceIdType.MESH


def _slot(p):
    return 4 * p[0] + 2 * p[1] + p[2]


def _allgather(xs):
    n = len(xs)

    def body(*refs):
        x_refs, o_refs = refs[:n], refs[n:2 * n]
        send_sems, recv_sems, local_sems = refs[2 * n:]
        x, y, c = lax.axis_index("x"), lax.axis_index("y"), lax.axis_index("c")
        me, sibling = (x, y, c), (x, y, 1 - c)
        chips = [(1 - x, y), (x, 1 - y), (1 - x, 1 - y)]

        def copy(a, k, block, to, from_input=False):
            dst = o_refs[a].at[_slot(block)]
            return pltpu.make_async_remote_copy(
                src_ref=x_refs[a] if from_input else dst, dst_ref=dst,
                send_sem=send_sems.at[a, k], recv_sem=recv_sems.at[a, k],
                device_id=to, device_id_type=MESH)

        mine = [pltpu.make_async_copy(x_refs[a], o_refs[a].at[_slot(me)], local_sems.at[a])
                for a in range(n)]
        for cp in mine:
            cp.start()
        first = []
        for a in range(n):
            first.append(copy(a, 0, me, sibling, True))
            first += [copy(a, 1 + j, me, (*chip, c), True) for j, chip in enumerate(chips)]
        for cp in first:
            cp.start()
        passed = []
        for j, chip in enumerate(chips):
            for a in range(n):
                copy(a, 1 + j, (*chip, c), me).wait_recv()
                cp = copy(a, 4 + j, (*chip, c), sibling)
                cp.start()
                passed.append(cp)
        for a in range(n):
            copy(a, 0, sibling, me).wait_recv()
            for j, chip in enumerate(chips):
                copy(a, 4 + j, (*chip, 1 - c), me).wait_recv()
        for cp in first + passed:
            cp.wait_send()
        for cp in mine:
            cp.wait()

    return pl.pallas_call(
        body,
        out_shape=[jax.ShapeDtypeStruct((N_DEV,) + t.shape, t.dtype) for t in xs],
        in_specs=[ANY] * n, out_specs=[ANY] * n,
        scratch_shapes=[pltpu.SemaphoreType.DMA((n, 7)), pltpu.SemaphoreType.DMA((n, 7)),
                        pltpu.SemaphoreType.DMA((n,))],
        name="allgather_weights")(*xs)


def _exchange_grads(gs, gr):
    def body(gs_ref, gr_ref, os_ref, or_ref, send_sems, recv_sems, local_sems):
        x, y, c = lax.axis_index("x"), lax.axis_index("y"), lax.axis_index("c")
        me = _slot((x, y, c))
        own = [pltpu.make_async_copy(gs_ref.at[me], os_ref.at[me], local_sems.at[0]),
               pltpu.make_async_copy(gr_ref, or_ref.at[me], local_sems.at[1])]
        for cp in own:
            cp.start()
        copies = []
        for rel in range(1, N_DEV):
            peer = (x ^ (rel >> 2), y ^ ((rel >> 1) & 1), c ^ (rel & 1))
            copies.append(pltpu.make_async_remote_copy(
                src_ref=gs_ref.at[_slot(peer)], dst_ref=os_ref.at[me],
                send_sem=send_sems.at[0, rel - 1], recv_sem=recv_sems.at[0, rel - 1],
                device_id=peer, device_id_type=MESH))
            copies.append(pltpu.make_async_remote_copy(
                src_ref=gr_ref, dst_ref=or_ref.at[me],
                send_sem=send_sems.at[1, rel - 1], recv_sem=recv_sems.at[1, rel - 1],
                device_id=peer, device_id_type=MESH))
        for cp in copies:
            cp.start()
        for cp in copies:
            cp.wait()
        for cp in own:
            cp.wait()

    return pl.pallas_call(
        body,
        out_shape=[jax.ShapeDtypeStruct(gs.shape, gs.dtype),
                   jax.ShapeDtypeStruct((N_DEV,) + gr.shape, gr.dtype)],
        in_specs=[ANY, ANY], out_specs=[ANY, ANY],
        scratch_shapes=[pltpu.SemaphoreType.DMA((2, 7)), pltpu.SemaphoreType.DMA((2, 7)),
                        pltpu.SemaphoreType.DMA((2,))],
        name="exchange_grads")(gs, gr)


BIG = [("mla_w_in", (D_MODEL, MLA_IN), 1), ("mla_w_uq", (Q_LORA, N_TOK_HEADS * QK_DIM), 1),
       ("mla_w_ukv", (KV_LORA, N_TOK_HEADS * 2 * HEAD_DIM), 1), ("lru_w_in", (D_MODEL, LRU_IN), 1),
       ("w_mem_kv", (2, D_MODEL, 2 * MEM_WIDTH), 1), ("w_out", (2, MIX_WIDTH, D_MODEL), 1)]
SMALL = [("lru_conv_w", (CONV_W, TOK_WIDTH), 1), ("lru_conv_b", (TOK_WIDTH,), 0),
         ("lru_b_rgate", (TOK_WIDTH,), 0), ("lru_b_igate", (TOK_WIDTH,), 0),
         ("lru_lambda", (TOK_WIDTH,), 0)]
REPL = [("mla_q_norm", (Q_LORA,)), ("mla_kv_norm", (KV_LORA,)),
        ("lru_w_rgate", (N_TOK_HEADS, HEAD_DIM, HEAD_DIM)),
        ("lru_w_igate", (N_TOK_HEADS, HEAD_DIM, HEAD_DIM)),
        ("ln_g", (2, D_MODEL)), ("ln_b", (2, D_MODEL))]


def _shard_shape(shape, axis):
    return tuple(d // N_DEV if a == axis else d for a, d in enumerate(shape))


def _size(shape):
    return math.prod(shape)


BIG_ROWS = sum(_size(s) for _, s, _ in BIG) // N_DEV // LANES
SMALL_ROWS = SUBLANES
SHARD_ROWS = 7936
REPL_ROWS = 808


def _pack_rows(flat_parts, rows):
    flat = jnp.concatenate([p.reshape(-1) for p in flat_parts])
    return jnp.pad(flat, (0, rows * LANES - flat.shape[0])).reshape(rows, LANES)


def _to_chunks(full, axis):
    shape = full.shape
    split = shape[:axis] + (N_DEV, shape[axis] // N_DEV) + shape[axis + 1:]
    return jnp.moveaxis(full.reshape(split), axis, 0).reshape(N_DEV, -1)


def _from_chunks(chunks, shape, axis):
    sh = _shard_shape(shape, axis)
    t = chunks.reshape((N_DEV,) + sh)
    t = jnp.moveaxis(t, 0, axis)
    return t.reshape(shape)


def _split_flat(flat2d, table):
    out, off = [], 0
    for size in table:
        out.append(flat2d[:, off:off + size])
        off += size
    return out


def _win0_to_padded(w):
    z = lambda n: jnp.zeros((w.shape[0], n), w.dtype)
    return jnp.concatenate([w[:, 0:640], w[:, 672:1952], z(KR_LANE), w[:, 640:672],
                            z(LANES - KR_LANE - QK_ROPE)], axis=1)


def _win0_from_padded(wp):
    k0 = Z0_KR + KR_LANE
    return jnp.concatenate([wp[:, 0:640], wp[:, k0:k0 + QK_ROPE], wp[:, 640:1920]], axis=1)


def _pad_heads(w, per_head, lo, hi):
    t = w.reshape(w.shape[0], N_TOK_HEADS, per_head)[:, :, lo:hi]
    t = jnp.pad(t, ((0, 0), (0, 0), (0, HEAD_PAD - (hi - lo))))
    return t.reshape(w.shape[0], QKV_PAD)


def _unpad_heads(wp, width):
    return wp.reshape(wp.shape[0], N_TOK_HEADS, HEAD_PAD)[:, :, :width]


def _block_diag(w):
    eye = jnp.eye(N_TOK_HEADS, dtype=w.dtype)
    return (w[:, :, None, :] * eye[:, None, :, None]).reshape(TOK_WIDTH, TOK_WIDTH)


def _diag_blocks(d):
    t = d.reshape(N_TOK_HEADS, HEAD_DIM, N_TOK_HEADS, HEAD_DIM)
    return jnp.stack([t[g, :, g, :] for g in range(N_TOK_HEADS)])


def _rope_tables(positions):
    half = QK_ROPE // 2
    inv_freq = ROPE_THETA ** (-jnp.arange(half, dtype=F32) / half)
    ang = positions.astype(F32)[:, None] * inv_freq
    cos, sin = jnp.cos(ang), jnp.sin(ang)
    s = positions.shape[0]
    one, zero = jnp.ones((s, QK_NOPE), F32), jnp.zeros((s, half), F32)
    tail = jnp.zeros((s, HEAD_PAD - QK_DIM), F32)
    znope = jnp.zeros((s, QK_NOPE), F32)
    c = jnp.concatenate([one, cos, cos, tail], axis=1)
    sa = jnp.concatenate([znope, -sin, zero, tail], axis=1)
    sb = jnp.concatenate([znope, zero, sin, tail], axis=1)
    return c, sa, sb


def _local_step(x, mem, positions, tgt, wts, ts, tatt):
    bf = lambda t: t.astype(BF16)
    win0 = _win0_to_padded(wts["mla_w_in"])
    wuq = _pad_heads(wts["mla_w_uq"], QK_DIM, 0, QK_DIM)
    wukv = jnp.concatenate([_pad_heads(wts["mla_w_ukv"], 2 * HEAD_DIM, 0, QK_NOPE),
                            _pad_heads(wts["mla_w_ukv"], 2 * HEAD_DIM, QK_NOPE, 2 * HEAD_DIM)],
                           axis=1)
    win1 = wts["lru_w_in"]
    wmkv, wout = wts["w_mem_kv"], wts["w_out"]
    gq = wts["mla_q_norm"].reshape(1, Q_LORA)
    gkv = wts["mla_kv_norm"].reshape(1, KV_LORA)
    ln_g, ln_b = wts["ln_g"], wts["ln_b"]
    wr, wi = bf(_block_diag(wts["lru_w_rgate"])), bf(_block_diag(wts["lru_w_igate"]))
    cw8 = jnp.pad(wts["lru_conv_w"], ((0, SUBLANES - CONV_W), (0, 0)))
    vec8 = jnp.pad(jnp.stack([wts["lru_conv_b"], wts["lru_b_rgate"], wts["lru_b_igate"],
                              wts["lru_lambda"]]), ((0, SUBLANES - 4), (0, 0)))
    tabs = _rope_tables(positions)
    tmem = mem.shape[0]

    z0 = _rowmm(x, win0, "in_proj0", ts)
    q, k, v = _mla_prep_fwd(z0, tabs, gq, gkv, wuq, wukv, ts)
    o, lse = _flash_fwd(q, k, v, tatt, tatt, FWD_HEADS)
    mkv0 = _rowmm(mem, wmkv[0], "mem_kv0", tmem)
    cat0, y0 = _gate_mem_fwd(o, z0, mkv0, Z0_GATE, Z0_QMEM, True, "gate_mem_fwd0", ts)
    del o
    pre0, h1 = _outproj_ln_fwd(y0, wout[0], x, ln_g[0:1], ln_b[0:1], None, "outproj_ln_fwd0", ts)
    z1 = _rowmm(h1, win1, "in_proj1", ts)
    hs, hprev = _lru_fwd(z1, cw8, vec8, wr, wi, ts)
    mkv1 = _rowmm(mem, wmkv[1], "mem_kv1", tmem)
    cat1, y1 = _gate_mem_fwd(hs, z1, mkv1, Z1_GATE, Z1_QMEM, False, "gate_mem_fwd1", ts)
    pre1, dh2, loss8 = _outproj_ln_fwd(y1, wout[1], h1, ln_g[1:2], ln_b[1:2], tgt,
                                       "outproj_ln_loss", ts)
    loss = loss8[0, 0]

    dpre1, dy1, dwout1, dgb1 = _outproj_ln_bwd(dh2, pre1, ln_g[1:2], y1, wout[1].T,
                                               "outproj_ln_bwd1", ts)
    dzg1, dhs, dmkv1 = _gate_mem_bwd(dy1, cat1, z1, mkv1, None, Z1_GATE, Z1_QMEM,
                                     "gate_mem_bwd1", ts)
    du, dwr, dwi, dvec = _lru_bwd(z1, dhs, hprev, cw8, vec8, wr, wi, wr.T, wi.T, ts)
    dh1, dwin1 = _linear_bwd(h1, [du, dzg1], [Z1_U, Z1_GATE], win1.T, dpre1, "in_proj_bwd1", ts)
    dwmkv1 = _wgrad_small(mem, dmkv1, "mem_kv_bwd1")
    dpre0, dy0, dwout0, dgb0 = _outproj_ln_bwd(dh1, pre0, ln_g[0:1], y0, wout[0].T,
                                               "outproj_ln_bwd0", ts)
    dzg0, do, dmkv0, stats = _gate_mem_bwd(dy0, cat0, z0, mkv0, lse, Z0_GATE, Z0_QMEM,
                                           "gate_mem_bwd0", ts)
    dq, dk, dv = _flash_bwd(q, k, v, stats, do, tatt, BWD_HEADS)
    dza, dzk, dwuq_p, dwukv_p, dg = _mla_prep_bwd(z0, dq, dk, dv, tabs, gq, gkv,
                                                  wuq.T, wukv.T, ts)
    gx, dwin0_p = _linear_bwd(x, [dza, dzg0, dzk], [Z0_CQ, Z0_GATE, Z0_KR], win0.T, dpre0,
                              "in_proj_bwd0", ts)
    dwmkv0 = _wgrad_small(mem, dmkv0, "mem_kv_bwd0")

    dwukv = jnp.concatenate([_unpad_heads(dwukv_p[:, :QKV_PAD], HEAD_DIM),
                             _unpad_heads(dwukv_p[:, QKV_PAD:], HEAD_DIM)], axis=2)
    grads = {
        "mla_w_in": _win0_from_padded(dwin0_p),
        "mla_q_norm": dg[0, 0:Q_LORA],
        "mla_w_uq": _unpad_heads(dwuq_p, QK_DIM).reshape(Q_LORA, N_TOK_HEADS * QK_DIM),
        "mla_kv_norm": dg[0, Q_LORA:Q_LORA + KV_LORA],
        "mla_w_ukv": dwukv.reshape(KV_LORA, N_TOK_HEADS * 2 * HEAD_DIM),
        "lru_w_in": dwin1,
        "lru_conv_w": dvec[0:CONV_W],
        "lru_conv_b": dvec[4],
        "lru_w_rgate": _diag_blocks(dwr),
        "lru_b_rgate": dvec[5],
        "lru_w_igate": _diag_blocks(dwi),
        "lru_b_igate": dvec[6],
        "lru_lambda": dvec[7],
        "w_mem_kv": jnp.stack([dwmkv0, dwmkv1]),
        "w_out": jnp.stack([dwout0, dwout1]),
        "ln_g": jnp.stack([dgb0[0], dgb1[0]]),
        "ln_b": jnp.stack([dgb0[1], dgb1[1]]),
    }
    return loss, gx, grads


WEIGHT_ORDER = ["mla_w_in", "mla_q_norm", "mla_w_uq", "mla_kv_norm", "mla_w_ukv", "lru_w_in",
                "lru_conv_w", "lru_conv_b", "lru_w_rgate", "lru_b_rgate", "lru_w_igate",
                "lru_b_igate", "lru_lambda", "w_mem_kv", "w_out", "ln_g", "ln_b"]


def kernel(x, mem, positions, mla_w_in, mla_q_norm, mla_w_uq, mla_kv_norm, mla_w_ukv, lru_w_in, lru_conv_w, lru_conv_b, lru_w_rgate, lru_b_rgate, lru_w_igate, lru_b_igate, lru_lambda, w_mem_kv, w_out, ln_g, ln_b, loss_target, m_mla_w_in, m_mla_q_norm, m_mla_w_uq, m_mla_kv_norm, m_mla_w_ukv, m_lru_w_in, m_lru_conv_w, m_lru_conv_b, m_lru_w_rgate, m_lru_b_rgate, m_lru_w_igate, m_lru_b_igate, m_lru_lambda, m_w_mem_kv, m_w_out, m_ln_g, m_ln_b, v_mla_w_in, v_mla_q_norm, v_mla_w_uq, v_mla_kv_norm, v_mla_w_ukv, v_lru_w_in, v_lru_conv_w, v_lru_conv_b, v_lru_w_rgate, v_lru_b_rgate, v_lru_w_igate, v_lru_b_igate, v_lru_lambda, v_w_mem_kv, v_w_out, v_ln_g, v_ln_b):
    w_in = dict(mla_w_in=mla_w_in, mla_q_norm=mla_q_norm, mla_w_uq=mla_w_uq,
                mla_kv_norm=mla_kv_norm, mla_w_ukv=mla_w_ukv, lru_w_in=lru_w_in,
                lru_conv_w=lru_conv_w, lru_conv_b=lru_conv_b, lru_w_rgate=lru_w_rgate,
                lru_b_rgate=lru_b_rgate, lru_w_igate=lru_w_igate, lru_b_igate=lru_b_igate,
                lru_lambda=lru_lambda, w_mem_kv=w_mem_kv, w_out=w_out, ln_g=ln_g, ln_b=ln_b)
    m_in = dict(mla_w_in=m_mla_w_in, mla_q_norm=m_mla_q_norm, mla_w_uq=m_mla_w_uq,
                mla_kv_norm=m_mla_kv_norm, mla_w_ukv=m_mla_w_ukv, lru_w_in=m_lru_w_in,
                lru_conv_w=m_lru_conv_w, lru_conv_b=m_lru_conv_b, lru_w_rgate=m_lru_w_rgate,
                lru_b_rgate=m_lru_b_rgate, lru_w_igate=m_lru_w_igate, lru_b_igate=m_lru_b_igate,
                lru_lambda=m_lru_lambda, w_mem_kv=m_w_mem_kv, w_out=m_w_out, ln_g=m_ln_g,
                ln_b=m_ln_b)
    v_in = dict(mla_w_in=v_mla_w_in, mla_q_norm=v_mla_q_norm, mla_w_uq=v_mla_w_uq,
                mla_kv_norm=v_mla_kv_norm, mla_w_ukv=v_mla_w_ukv, lru_w_in=v_lru_w_in,
                lru_conv_w=v_lru_conv_w, lru_conv_b=v_lru_conv_b, lru_w_rgate=v_lru_w_rgate,
                lru_b_rgate=v_lru_b_rgate, lru_w_igate=v_lru_w_igate, lru_b_igate=v_lru_b_igate,
                lru_lambda=v_lru_lambda, w_mem_kv=v_w_mem_kv, w_out=v_w_out, ln_g=v_ln_g,
                ln_b=v_ln_b)
    s = x.shape[1]
    ts = min(ROW_BLOCK, s)
    tatt = min(ATT_BLOCK, s)
    sharded = BIG + SMALL
    big_sizes = [_size(sh) // N_DEV for _, sh, _ in BIG]
    small_sizes = [_size(sh) // N_DEV for _, sh, _ in SMALL]

    big_local = _pack_rows([w_in[n] for n, _, _ in BIG], BIG_ROWS).astype(BF16)
    small_local = _pack_rows([w_in[n] for n, _, _ in SMALL], SMALL_ROWS)
    big_all, small_all = _allgather([big_local, small_local])
    wts = {}
    for (n, sh, ax), part in zip(BIG, _split_flat(big_all.reshape(N_DEV, -1), big_sizes)):
        wts[n] = _from_chunks(part, sh, ax)
    for (n, sh, ax), part in zip(SMALL, _split_flat(small_all.reshape(N_DEV, -1), small_sizes)):
        wts[n] = _from_chunks(part, sh, ax)
    for n, sh in REPL:
        wts[n] = w_in[n].reshape(sh)

    loss_local, gx, grads = _local_step(x[0], mem[0], positions[0], loss_target[0], wts, ts, tatt)
    loss = lax.psum(loss_local, ("x", "y", "c"))

    chunks = jnp.concatenate([_to_chunks(grads[n].reshape(sh), ax) for n, sh, ax in sharded],
                             axis=1)
    chunks = jnp.pad(chunks, ((0, 0), (0, SHARD_ROWS * LANES - chunks.shape[1])))
    gs = chunks.reshape(N_DEV, SHARD_ROWS, LANES)
    gr = _pack_rows([grads[n] for n, _ in REPL], REPL_ROWS)
    parts_s, parts_r = _exchange_grads(gs, gr)

    pack_s = lambda d: _pack_rows([d[n] for n, _, _ in sharded], SHARD_ROWS)
    pack_r = lambda d: _pack_rows([d[n] for n, _ in REPL], REPL_ROWS)
    res_s = _adamw(parts_s, pack_s(w_in), pack_s(m_in), pack_s(v_in), "adamw_sharded")
    res_r = _adamw(parts_r, pack_r(w_in), pack_r(m_in), pack_r(v_in), "adamw_replicated")
    outs = [{}, {}, {}, {}]
    for kind in range(4):
        flat = res_s[kind].reshape(1, -1)
        for (n, _, _), part in zip(sharded, _split_flat(flat, big_sizes + small_sizes)):
            outs[kind][n] = part.reshape(w_in[n].shape)
        flat = res_r[kind].reshape(1, -1)
        for (n, sh), part in zip(REPL, _split_flat(flat, [_size(sh) for _, sh in REPL])):
            outs[kind][n] = part.reshape(w_in[n].shape)
    result = [loss, gx.reshape(x.shape)]
    for kind in range(4):
        result += [outs[kind][n] for n in WEIGHT_ORDER]
    return tuple(result)
```

```python
import functools
import math

import jax
import jax.numpy as jnp
from jax import lax
from jax.experimental import pallas as pl
from jax.experimental.pallas import tpu as pltpu

F32 = jnp.float32
BF16 = jnp.bfloat16

D_MODEL = 1024
MEM_LEN = 256
HEAD_DIM = 64
N_TOK_HEADS = 12
N_MEM_HEADS = 4
TOK_WIDTH = 768
MEM_WIDTH = 256
MIX_WIDTH = 1024
Q_LORA = 384
KV_LORA = 256
QK_NOPE = 64
QK_ROPE = 32
QK_DIM = 96
ROPE_THETA = 10000.0
CONV_W = 4
LRU_C = 8.0
ALPHA = (2.0 * 2) ** 0.25
NORM_EPS = 1e-6
MLA_IN = 1952
LRU_IN = 2048
ADAM_LR = 0.001
ADAM_B1 = 0.9
ADAM_B2 = 0.999
ADAM_EPS = 1e-08
ADAM_WD = 0.01
ADAM_STEP = 10

N_DEV = 8
LANES = 128
SUBLANES = 8
HEAD_PAD = 128
QKV_PAD = N_TOK_HEADS * HEAD_PAD
ZP = 2048
Z0_CQ, Z0_CKV, Z0_GATE, Z0_QMEM, Z0_KR = 0, 384, 640, 1664, 1920
KR_LANE = 64
Z1_U, Z1_GATE, Z1_QMEM = 0, 768, 1792

ROW_BLOCK = 512
ATT_BLOCK = 512
FWD_HEADS = 4
BWD_HEADS = 2
VMEM_LIMIT = 56 * 1024 * 1024
NEG_BIG = -1e30
STRIP = 32
LOG2E = math.log2(math.e)


def _cp(n_axes):
    return pltpu.CompilerParams(dimension_semantics=("arbitrary",) * n_axes,
                                vmem_limit_bytes=VMEM_LIMIT)


def _dot(a, b):
    return jnp.dot(a, b, preferred_element_type=F32)


def _dot_nt(a, b):
    return lax.dot_general(a, b, (((1,), (1,)), ((), ())), preferred_element_type=F32)


def _dot_tn(a, b):
    return lax.dot_general(a, b, (((0,), (0,)), ((), ())), preferred_element_type=F32)


def _sigmoid(t):
    return 1.0 / (1.0 + jnp.exp(-t))


def _lane(shape):
    return lax.broadcasted_iota(jnp.int32, shape, len(shape) - 1)


def _full(shape):
    nd = len(shape)
    return pl.BlockSpec(shape, lambda *_: (0,) * nd)


def _rows(ts, width, col=0):
    return pl.BlockSpec((ts, width), lambda i: (i, col))


def _heads(ts):
    return pl.BlockSpec((N_TOK_HEADS, ts, HEAD_PAD), lambda i: (0, i, 0))


def _rowmm(x, w, name, ts):
    s, k = x.shape
    n = w.shape[1]

    def body(x_ref, w_ref, o_ref):
        o_ref[...] = _dot(x_ref[...].astype(BF16), w_ref[...])

    return pl.pallas_call(
        body, grid=(s // ts,),
        in_specs=[_rows(ts, k), _full((k, n))],
        out_specs=_rows(ts, n),
        out_shape=jax.ShapeDtypeStruct((s, n), F32),
        name=name, compiler_params=_cp(1))(x, w)


def _rms_parts(t):
    rs = lax.rsqrt(jnp.mean(t * t, axis=-1, keepdims=True) + NORM_EPS)
    return rs


def _rope(t, c, sa, sb):
    return t * c + pltpu.roll(t, LANES - 16, 1) * sa + pltpu.roll(t, 16, 1) * sb


def _rope_t(d, c, sa, sb):
    return d * c + pltpu.roll(d * sa, 16, 1) + pltpu.roll(d * sb, LANES - 16, 1)


def _mla_prep_fwd(z0, tabs, gq, gkv, wuq, wukv, ts):
    s = z0.shape[0]

    def body(z_ref, c_ref, sa_ref, sb_ref, gq_ref, gkv_ref, wuq_ref, wukv_ref,
             q_ref, k_ref, v_ref):
        cq = z_ref[:, Z0_CQ:Z0_CQ + Q_LORA]
        ckv = z_ref[:, Z0_CKV:Z0_CKV + KV_LORA]
        kr = z_ref[:, Z0_KR:Z0_KR + LANES]
        cqn = cq * _rms_parts(cq) * gq_ref[...]
        ckvn = ckv * _rms_parts(ckv) * gkv_ref[...]
        q = _dot(cqn.astype(BF16), wuq_ref[...])
        kv = _dot(ckvn.astype(BF16), wukv_ref[...])
        c, sa, sb = c_ref[...], sa_ref[...], sb_ref[...]
        krope = _rope(kr, c, sa, sb)
        pad_lane = _lane((ts, HEAD_PAD)) >= HEAD_DIM
        for h in range(N_TOK_HEADS):
            sl = slice(h * HEAD_PAD, (h + 1) * HEAD_PAD)
            q_ref[h] = _rope(q[:, sl], c, sa, sb).astype(BF16)
            k_ref[h] = (kv[:, sl] + krope).astype(BF16)
            vh = kv[:, QKV_PAD + h * HEAD_PAD:QKV_PAD + (h + 1) * HEAD_PAD]
            v_ref[h] = jnp.where(pad_lane, 1.0, vh).astype(BF16)

    out = jax.ShapeDtypeStruct((N_TOK_HEADS, s, HEAD_PAD), BF16)
    return pl.pallas_call(
        body, grid=(s // ts,),
        in_specs=[_rows(ts, ZP), _rows(ts, LANES), _rows(ts, LANES), _rows(ts, LANES),
                  _full((1, Q_LORA)), _full((1, KV_LORA)),
                  _full((Q_LORA, QKV_PAD)), _full((KV_LORA, 2 * QKV_PAD))],
        out_specs=[_heads(ts)] * 3,
        out_shape=[out, out, out],
        name="mla_prep_fwd", compiler_params=_cp(1))(z0, *tabs, gq, gkv, wuq, wukv)


def _mla_prep_bwd(z0, dq, dk, dv, tabs, gq, gkv, wuq_t, wukv_t, ts):
    s = z0.shape[0]

    def body(z_ref, dq_ref, dk_ref, dv_ref, c_ref, sa_ref, sb_ref, gq_ref, gkv_ref,
             wuqt_ref, wukvt_ref, dza_ref, dzk_ref, dwuq_ref, dwukv_ref, dg_ref):
        @pl.when(pl.program_id(0) == 0)
        def _():
            dwuq_ref[...] = jnp.zeros_like(dwuq_ref)
            dwukv_ref[...] = jnp.zeros_like(dwukv_ref)
            dg_ref[...] = jnp.zeros_like(dg_ref)

        cq = z_ref[:, Z0_CQ:Z0_CQ + Q_LORA]
        ckv = z_ref[:, Z0_CKV:Z0_CKV + KV_LORA]
        rq, rkv = _rms_parts(cq), _rms_parts(ckv)
        gq_, gkv_ = gq_ref[...], gkv_ref[...]
        cqn = (cq * rq * gq_).astype(BF16)
        ckvn = (ckv * rkv * gkv_).astype(BF16)
        c, sa, sb = c_ref[...], sa_ref[...], sb_ref[...]
        dqp, dksum = [], None
        for h in range(N_TOK_HEADS):
            dqp.append(_rope_t(dq_ref[h], c, sa, sb))
            dksum = dk_ref[h] if dksum is None else dksum + dk_ref[h]
        dqp = jnp.concatenate(dqp, axis=1).astype(BF16)
        lane = _lane(dksum.shape)
        dzk_ref[...] = jnp.where((lane >= KR_LANE) & (lane < KR_LANE + QK_ROPE),
                                 _rope_t(dksum, c, sa, sb), 0.0)
        dkv = jnp.concatenate([dk_ref[h] for h in range(N_TOK_HEADS)]
                              + [dv_ref[h] for h in range(N_TOK_HEADS)], axis=1).astype(BF16)
        dcqn = _dot(dqp, wuqt_ref[...])
        dckvn = _dot(dkv, wukvt_ref[...])
        dwuq_ref[...] += _dot_tn(cqn, dqp)
        dwukv_ref[...] += _dot_tn(ckvn, dkv)
        dg_ref[0:1, 0:Q_LORA] += jnp.sum(dcqn * cq * rq, axis=0, keepdims=True)
        dg_ref[0:1, Q_LORA:Q_LORA + KV_LORA] += jnp.sum(dckvn * ckv * rkv, axis=0, keepdims=True)
        wq = dcqn * gq_
        wkv = dckvn * gkv_
        dcq = rq * wq - cq * (rq * rq * rq) * jnp.mean(wq * cq, axis=-1, keepdims=True)
        dckv = rkv * wkv - ckv * (rkv * rkv * rkv) * jnp.mean(wkv * ckv, axis=-1, keepdims=True)
        dza_ref[:, 0:Q_LORA] = dcq
        dza_ref[:, Q_LORA:Q_LORA + KV_LORA] = dckv

    na = Q_LORA + KV_LORA
    return pl.pallas_call(
        body, grid=(s // ts,),
        in_specs=[_rows(ts, ZP), _heads(ts), _heads(ts), _heads(ts),
                  _rows(ts, LANES), _rows(ts, LANES), _rows(ts, LANES),
                  _full((1, Q_LORA)), _full((1, KV_LORA)),
                  _full((QKV_PAD, Q_LORA)), _full((2 * QKV_PAD, KV_LORA))],
        out_specs=[_rows(ts, na), _rows(ts, LANES), _full((Q_LORA, QKV_PAD)),
                   _full((KV_LORA, 2 * QKV_PAD)), _full((SUBLANES, na))],
        out_shape=[jax.ShapeDtypeStruct((s, na), F32), jax.ShapeDtypeStruct((s, LANES), F32),
                   jax.ShapeDtypeStruct((Q_LORA, QKV_PAD), F32),
                   jax.ShapeDtypeStruct((KV_LORA, 2 * QKV_PAD), F32),
                   jax.ShapeDtypeStruct((SUBLANES, na), F32)],
        name="mla_prep_bwd", compiler_params=_cp(1))(
            z0, dq, dk, dv, *tabs, gq, gkv, wuq_t, wukv_t)


def _causal_pairs(nb, by_key):
    if by_key:
        pairs = [(i, j) for j in range(nb) for i in range(j, nb)]
    else:
        pairs = [(i, j) for i in range(nb) for j in range(i + 1)]
    return (jnp.array([p[0] for p in pairs], jnp.int32),
            jnp.array([p[1] for p in pairs], jnp.int32))


def _flash_fwd(q, k, v, t, nh):
    s = q.shape[1]
    itab, jtab = _causal_pairs(s // t, False)
    c2 = LOG2E / math.sqrt(QK_DIM)
    nch = t // LANES

    def body(it_ref, jt_ref, q_ref, k_ref, v_ref, o_ref, lse_ref, m_scr, acc_scr):
        pair = pl.program_id(1)
        i, j = it_ref[pair], jt_ref[pair]

        @pl.when(j == 0)
        def _():
            m_scr[...] = jnp.full_like(m_scr, NEG_BIG)
            acc_scr[...] = jnp.zeros_like(acc_scr)

        def softmax_strips(masked, hs, sc):
            ps, als = [], []
            for r0 in range(0, t, STRIP):
                rows = slice(r0, r0 + STRIP)
                ch = [sc[rows, n * LANES:(n + 1) * LANES] * c2 for n in range(nch)]
                if masked:
                    rr = r0 + lax.broadcasted_iota(jnp.int32, (STRIP, LANES), 0)
                    cc = lax.broadcasted_iota(jnp.int32, (STRIP, LANES), 1)
                    ch = [jnp.where(cc + n * LANES <= rr, ch[n], NEG_BIG) for n in range(nch)]
                mx = ch[0]
                for n in range(1, nch):
                    mx = jnp.maximum(mx, ch[n])
                m_prev = m_scr[hs, rows, :]
                m_next = jnp.maximum(m_prev, jnp.max(mx, axis=-1, keepdims=True))
                ps.append(jnp.concatenate(
                    [jnp.exp2(ch[n] - m_next).astype(BF16) for n in range(nch)], axis=1))
                als.append(jnp.exp2(m_prev - m_next))
                m_scr[hs, rows, :] = m_next
            return jnp.concatenate(ps, axis=0), jnp.concatenate(als, axis=0)

        def step(masked):
            scores = [_dot_nt(q_ref[hs], k_ref[hs]) for hs in range(nh)]
            for hs, sc in enumerate(scores):
                p, alpha = softmax_strips(masked, hs, sc)
                acc_scr[hs] = alpha * acc_scr[hs] + _dot(p, v_ref[hs])

        @pl.when(j < i)
        def _():
            step(False)

        @pl.when(j == i)
        def _():
            step(True)
            for h in range(nh):
                acc = acc_scr[h]
                l = acc[:, HEAD_DIM:HEAD_DIM + 1]
                o_ref[h] = jnp.where(_lane(acc.shape) < HEAD_DIM, acc / l, 0.0)
                lse_ref[h] = m_scr[h] + jnp.log2(l)

    qspec = pl.BlockSpec((nh, t, HEAD_PAD), lambda h, p, it, jt: (h, it[p], 0))
    kspec = pl.BlockSpec((nh, t, HEAD_PAD), lambda h, p, it, jt: (h, jt[p], 0))
    out = jax.ShapeDtypeStruct((N_TOK_HEADS, s, HEAD_PAD), F32)
    return pl.pallas_call(
        body,
        grid_spec=pltpu.PrefetchScalarGridSpec(
            num_scalar_prefetch=2, grid=(N_TOK_HEADS // nh, itab.shape[0]),
            in_specs=[qspec, kspec, kspec], out_specs=[qspec, qspec],
            scratch_shapes=[pltpu.VMEM((nh, t, HEAD_PAD), F32)] * 2),
        out_shape=[out, out],
        name="flash_fwd", compiler_params=_cp(2))(itab, jtab, q, k, v)


def _flash_bwd(q, k, v, stats, do, t, nh):
    s = q.shape[1]
    nb = s // t
    itab, jtab = _causal_pairs(nb, True)
    scale = 1.0 / math.sqrt(QK_DIM)
    c2 = LOG2E * scale
    nch = t // LANES

    def body(it_ref, jt_ref, q_ref, k_ref, v_ref, st_ref, do_ref, dq_ref, dk_ref, dv_ref,
             dk_scr, dv_scr):
        pair = pl.program_id(1)
        i, j = it_ref[pair], jt_ref[pair]
        rows_i = pl.ds(pl.multiple_of(i * t, t), t)

        @pl.when(i == j)
        def _():
            dk_scr[...] = jnp.zeros_like(dk_scr)
            dv_scr[...] = jnp.zeros_like(dv_scr)

        @pl.when(j == 0)
        def _():
            dq_ref[:, rows_i, :] = jnp.zeros((nh, t, HEAD_PAD), F32)

        def prob_strips(masked, h, sc, dp):
            ps, dss = [], []
            low = _lane((STRIP, LANES)) < HEAD_DIM
            for r0 in range(0, t, STRIP):
                rows = slice(r0, r0 + STRIP)
                st = st_ref[h, rows, :]
                swapped = pltpu.roll(st, HEAD_DIM, 1)
                lse = jnp.where(low, st, swapped)
                delta = jnp.where(low, swapped, st)
                if masked:
                    rr = r0 + lax.broadcasted_iota(jnp.int32, (STRIP, LANES), 0)
                    cc = lax.broadcasted_iota(jnp.int32, (STRIP, LANES), 1)
                pcs, dcs = [], []
                for n in range(nch):
                    cols = slice(n * LANES, (n + 1) * LANES)
                    x = sc[rows, cols] * c2
                    if masked:
                        x = jnp.where(cc + n * LANES <= rr, x, NEG_BIG)
                    p = jnp.exp2(x - lse)
                    pcs.append(p.astype(BF16))
                    dcs.append((p * (dp[rows, cols] - delta) * scale).astype(BF16))
                ps.append(jnp.concatenate(pcs, axis=1))
                dss.append(jnp.concatenate(dcs, axis=1))
            return jnp.concatenate(ps, axis=0), jnp.concatenate(dss, axis=0)

        def step(masked):
            scs = [_dot_nt(q_ref[h], k_ref[h]) for h in range(nh)]
            dps = [_dot_nt(do_ref[h], v_ref[h]) for h in range(nh)]
            for h in range(nh):
                p, ds = prob_strips(masked, h, scs[h], dps[h])
                dv_scr[h] += _dot_tn(p, do_ref[h])
                dk_scr[h] += _dot_tn(ds, q_ref[h])
                dq_ref[h, rows_i, :] += _dot(ds, k_ref[h])

        @pl.when(i > j)
        def _():
            step(False)

        @pl.when(i == j)
        def _():
            step(True)

        @pl.when(i == nb - 1)
        def _():
            dk_ref[...] = dk_scr[...]
            dv_ref[...] = dv_scr[...]

    qspec = pl.BlockSpec((nh, t, HEAD_PAD), lambda h, p, it, jt: (h, it[p], 0))
    kspec = pl.BlockSpec((nh, t, HEAD_PAD), lambda h, p, it, jt: (h, jt[p], 0))
    dqspec = pl.BlockSpec((nh, s, HEAD_PAD), lambda h, p, it, jt: (h, 0, 0))
    out = jax.ShapeDtypeStruct((N_TOK_HEADS, s, HEAD_PAD), F32)
    return pl.pallas_call(
        body,
        grid_spec=pltpu.PrefetchScalarGridSpec(
            num_scalar_prefetch=2, grid=(N_TOK_HEADS // nh, itab.shape[0]),
            in_specs=[qspec, kspec, kspec, qspec, qspec], out_specs=[dqspec, kspec, kspec],
            scratch_shapes=[pltpu.VMEM((nh, t, HEAD_PAD), F32)] * 2),
        out_shape=[out, out, out],
        name="flash_bwd", compiler_params=_cp(2))(itab, jtab, q, k, v, stats, do)


def _mem_probs(qp, kp, hh):
    lane = _lane(qp.shape)
    keep = (lane < HEAD_DIM) if hh == 0 else (lane >= HEAD_DIM)
    qh = jnp.where(keep, qp, 0.0).astype(BF16)
    sc = _dot_nt(qh, kp) * (1.0 / math.sqrt(HEAD_DIM))
    e = jnp.exp(sc - jnp.max(sc, axis=-1, keepdims=True))
    return e / jnp.sum(e, axis=-1, keepdims=True), keep


def _gate_mem_fwd(tok, z, memkv, g0, q0, padded, name, ts):
    s = z.shape[0]
    zw = z.shape[1]
    tok_spec = _heads(ts) if padded else _rows(ts, TOK_WIDTH)

    def body(tok_ref, z_ref, mkv_ref, cat_ref, y_ref):
        if padded:
            for p in range(N_TOK_HEADS // 2):
                cat_ref[:, p * LANES:(p + 1) * LANES] = (
                    tok_ref[2 * p] + pltpu.roll(tok_ref[2 * p + 1], HEAD_DIM, 1))
        else:
            cat_ref[:, 0:TOK_WIDTH] = tok_ref[...]
        for pr in range(N_MEM_HEADS // 2):
            sl = slice(pr * LANES, (pr + 1) * LANES)
            qp = z_ref[:, q0 + pr * LANES:q0 + (pr + 1) * LANES]
            kp = mkv_ref[:, sl].astype(BF16)
            vp = mkv_ref[:, MEM_WIDTH + pr * LANES:MEM_WIDTH + (pr + 1) * LANES].astype(BF16)
            outs = []
            for hh in range(2):
                p, _ = _mem_probs(qp, kp, hh)
                outs.append(_dot(p.astype(BF16), vp))
            lane = _lane(outs[0].shape)
            cat_ref[:, TOK_WIDTH + pr * LANES:TOK_WIDTH + (pr + 1) * LANES] = jnp.where(
                lane < HEAD_DIM, outs[0], outs[1])
        gate = z_ref[:, g0:g0 + MIX_WIDTH]
        y_ref[...] = cat_ref[...] * (gate * _sigmoid(gate))

    out = jax.ShapeDtypeStruct((s, MIX_WIDTH), F32)
    return pl.pallas_call(
        body, grid=(s // ts,),
        in_specs=[tok_spec, _rows(ts, zw), _full((MEM_LEN, 2 * MEM_WIDTH))],
        out_specs=[_rows(ts, MIX_WIDTH)] * 2,
        out_shape=[out, out],
        name=name, compiler_params=_cp(1))(tok, z, memkv)


def _gate_mem_bwd(dy, cat, z, memkv, lse, g0, q0, name, ts):
    s = z.shape[0]
    zw = z.shape[1]
    padded = lse is not None
    gq_w = MIX_WIDTH + MEM_WIDTH

    def body(*refs):
        if padded:
            dy_ref, cat_ref, z_ref, mkv_ref, lse_ref, dzg_ref, dtok_ref, dmkv_ref, st_ref = refs
        else:
            dy_ref, cat_ref, z_ref, mkv_ref, dzg_ref, dtok_ref, dmkv_ref = refs

        @pl.when(pl.program_id(0) == 0)
        def _():
            dmkv_ref[...] = jnp.zeros_like(dmkv_ref)

        gate = z_ref[:, g0:g0 + MIX_WIDTH]
        sg = _sigmoid(gate)
        dy_ = dy_ref[...]
        dzg_ref[:, 0:MIX_WIDTH] = dy_ * cat_ref[...] * (sg * (1.0 + gate * (1.0 - sg)))
        dcat = dy_ * (gate * sg)
        if padded:
            low = _lane((ts, LANES)) < HEAD_DIM
            for p in range(N_TOK_HEADS // 2):
                d = dcat[:, p * LANES:(p + 1) * LANES]
                prod = d * cat_ref[:, p * LANES:(p + 1) * LANES]
                first = jnp.sum(jnp.where(low, prod, 0.0), axis=-1, keepdims=True)
                second = jnp.sum(jnp.where(low, 0.0, prod), axis=-1, keepdims=True)
                dtok_ref[2 * p] = jnp.where(low, d, 0.0).astype(BF16)
                dtok_ref[2 * p + 1] = jnp.where(low, pltpu.roll(d, HEAD_DIM, 1), 0.0).astype(BF16)
                st_ref[2 * p] = jnp.where(low, lse_ref[2 * p], first)
                st_ref[2 * p + 1] = jnp.where(low, lse_ref[2 * p + 1], second)
        else:
            dtok_ref[...] = dcat[:, 0:TOK_WIDTH]
        for pr in range(N_MEM_HEADS // 2):
            sl = slice(pr * LANES, (pr + 1) * LANES)
            vsl = slice(MEM_WIDTH + pr * LANES, MEM_WIDTH + (pr + 1) * LANES)
            qp = z_ref[:, q0 + pr * LANES:q0 + (pr + 1) * LANES]
            qpb = qp.astype(BF16)
            kp = mkv_ref[:, sl].astype(BF16)
            vp = mkv_ref[:, vsl].astype(BF16)
            dmo = dcat[:, TOK_WIDTH + pr * LANES:TOK_WIDTH + (pr + 1) * LANES]
            dqp = None
            for hh in range(2):
                p, keep = _mem_probs(qp, kp, hh)
                do_h = jnp.where(keep, dmo, 0.0).astype(BF16)
                dmkv_ref[:, vsl] += _dot_tn(p.astype(BF16), do_h)
                dp = _dot_nt(do_h, vp)
                ds = (p * (dp - jnp.sum(dp * p, axis=-1, keepdims=True))
                      * (1.0 / math.sqrt(HEAD_DIM))).astype(BF16)
                dqh = jnp.where(keep, _dot(ds, kp), 0.0)
                dqp = dqh if dqp is None else dqp + dqh
                dkh = _dot_tn(ds, qpb)
                klane = _lane(dkh.shape)
                kkeep = (klane < HEAD_DIM) if hh == 0 else (klane >= HEAD_DIM)
                dmkv_ref[:, sl] += jnp.where(kkeep, dkh, 0.0)
            dzg_ref[:, MIX_WIDTH + pr * LANES:MIX_WIDTH + (pr + 1) * LANES] = dqp

    in_specs = [_rows(ts, MIX_WIDTH), _rows(ts, MIX_WIDTH), _rows(ts, zw),
                _full((MEM_LEN, 2 * MEM_WIDTH))]
    out_specs = [_rows(ts, gq_w), _heads(ts) if padded else _rows(ts, TOK_WIDTH),
                 _full((MEM_LEN, 2 * MEM_WIDTH))]
    heads_shape = (N_TOK_HEADS, s, HEAD_PAD)
    out_shape = [jax.ShapeDtypeStruct((s, gq_w), F32),
                 jax.ShapeDtypeStruct(heads_shape, BF16) if padded
                 else jax.ShapeDtypeStruct((s, TOK_WIDTH), F32),
                 jax.ShapeDtypeStruct((MEM_LEN, 2 * MEM_WIDTH), F32)]
    args = [dy, cat, z, memkv]
    if padded:
        in_specs.append(_heads(ts))
        out_specs.append(_heads(ts))
        out_shape.append(jax.ShapeDtypeStruct(heads_shape, F32))
        args.append(lse)
    return pl.pallas_call(
        body, grid=(s // ts,), in_specs=in_specs, out_specs=out_specs, out_shape=out_shape,
        name=name, compiler_params=_cp(1))(*args)


def _ln_stats(pre):
    mu = jnp.mean(pre, axis=-1, keepdims=True)
    d = pre - mu
    rstd = lax.rsqrt(jnp.mean(d * d, axis=-1, keepdims=True) + NORM_EPS)
    return d * rstd, rstd


def _outproj_ln_fwd(y, w, h, g, b, tgt, name, ts):
    s = y.shape[0]
    with_loss = tgt is not None

    def body(*refs):
        if with_loss:
            y_ref, w_ref, h_ref, g_ref, b_ref, t_ref, pre_ref, out_ref, loss_ref = refs
        else:
            y_ref, w_ref, h_ref, g_ref, b_ref, pre_ref, out_ref = refs
        pre = ALPHA * h_ref[...] + _dot(y_ref[...].astype(BF16), w_ref[...])
        pre_ref[...] = pre
        xhat, _ = _ln_stats(pre)
        hout = xhat * g_ref[...] + b_ref[...]
        if with_loss:
            @pl.when(pl.program_id(0) == 0)
            def _():
                loss_ref[...] = jnp.zeros_like(loss_ref)
            err = hout - t_ref[...]
            out_ref[...] = err * (1.0 / D_MODEL)
            loss_ref[...] += 0.5 * jnp.sum(jnp.mean(err * err, axis=-1, keepdims=True))
        else:
            out_ref[...] = hout

    act = jax.ShapeDtypeStruct((s, D_MODEL), F32)
    in_specs = [_rows(ts, MIX_WIDTH), _full((MIX_WIDTH, D_MODEL)), _rows(ts, D_MODEL),
                _full((1, D_MODEL)), _full((1, D_MODEL))]
    out_specs = [_rows(ts, D_MODEL)] * 2
    out_shape = [act, act]
    args = [y, w, h, g, b]
    if with_loss:
        in_specs.append(_rows(ts, D_MODEL))
        out_specs.append(_full((SUBLANES, LANES)))
        out_shape.append(jax.ShapeDtypeStruct((SUBLANES, LANES), F32))
        args.append(tgt)
    return pl.pallas_call(
        body, grid=(s // ts,), in_specs=in_specs, out_specs=out_specs, out_shape=out_shape,
        name=name, compiler_params=_cp(1))(*args)


def _outproj_ln_bwd(dh, pre, g, y, w_t, name, ts):
    s = y.shape[0]

    def body(dh_ref, pre_ref, g_ref, y_ref, wt_ref, dpre_ref, dy_ref, dw_ref, dgb_ref):
        @pl.when(pl.program_id(0) == 0)
        def _():
            dw_ref[...] = jnp.zeros_like(dw_ref)
            dgb_ref[...] = jnp.zeros_like(dgb_ref)

        dh_ = dh_ref[...]
        xhat, rstd = _ln_stats(pre_ref[...])
        dxh = dh_ * g_ref[...]
        dpre = rstd * (dxh - jnp.mean(dxh, axis=-1, keepdims=True)
                       - xhat * jnp.mean(dxh * xhat, axis=-1, keepdims=True))
        dpre_ref[...] = dpre
        dgb_ref[0:1, :] += jnp.sum(dh_ * xhat, axis=0, keepdims=True)
        dgb_ref[1:2, :] += jnp.sum(dh_, axis=0, keepdims=True)
        dpb = dpre.astype(BF16)
        dy_ref[...] = _dot(dpb, wt_ref[...])
        dw_ref[...] += _dot_tn(y_ref[...].astype(BF16), dpb)

    act = jax.ShapeDtypeStruct((s, D_MODEL), F32)
    return pl.pallas_call(
        body, grid=(s // ts,),
        in_specs=[_rows(ts, D_MODEL), _rows(ts, D_MODEL), _full((1, D_MODEL)),
                  _rows(ts, MIX_WIDTH), _full((D_MODEL, MIX_WIDTH))],
        out_specs=[_rows(ts, D_MODEL), _rows(ts, MIX_WIDTH), _full((MIX_WIDTH, D_MODEL)),
                   _full((SUBLANES, D_MODEL))],
        out_shape=[act, act, jax.ShapeDtypeStruct((MIX_WIDTH, D_MODEL), F32),
                   jax.ShapeDtypeStruct((SUBLANES, D_MODEL), F32)],
        name=name, compiler_params=_cp(1))(dh, pre, g, y, w_t)


def _linear_bwd(x, dys, offs, w_t, resid, name, ts):
    s, kdim = x.shape
    n = w_t.shape[0]
    widths = [d.shape[1] for d in dys]
    npieces = len(dys)

    def body(*refs):
        x_ref = refs[0]
        dy_refs = refs[1:1 + npieces]
        wt_ref, r_ref, dx_ref, dw_ref = refs[1 + npieces:]

        @pl.when(pl.program_id(0) == 0)
        def _():
            dw_ref[...] = jnp.zeros_like(dw_ref)

        xb = x_ref[...].astype(BF16)
        dx = ALPHA * r_ref[...]
        for dy_ref, off, wd in zip(dy_refs, offs, widths):
            dyb = dy_ref[...].astype(BF16)
            dx = dx + _dot(dyb, wt_ref[off:off + wd, :])
            dw_ref[:, off:off + wd] += _dot_tn(xb, dyb)
        dx_ref[...] = dx

    return pl.pallas_call(
        body, grid=(s // ts,),
        in_specs=[_rows(ts, kdim)] + [_rows(ts, wd) for wd in widths]
                 + [_full((n, kdim)), _rows(ts, kdim)],
        out_specs=[_rows(ts, kdim), _full((kdim, n))],
        out_shape=[jax.ShapeDtypeStruct((s, kdim), F32), jax.ShapeDtypeStruct((kdim, n), F32)],
        name=name, compiler_params=_cp(1))(x, *dys, w_t, resid)


def _wgrad_small(x, dy, name):
    def body(x_ref, dy_ref, dw_ref):
        dw_ref[...] = _dot_tn(x_ref[...].astype(BF16), dy_ref[...].astype(BF16))

    return pl.pallas_call(
        body, out_shape=jax.ShapeDtypeStruct((x.shape[1], dy.shape[1]), F32),
        name=name, compiler_params=pltpu.CompilerParams(vmem_limit_bytes=VMEM_LIMIT))(x, dy)


def _shift_down(u, carry8, k):
    if k == 0:
        return u
    rolled = pltpu.roll(u, k, 0)
    row = lax.broadcasted_iota(jnp.int32, carry8.shape, 0)
    top = jnp.where(row < k, pltpu.roll(carry8, k, 0), rolled[0:SUBLANES])
    return jnp.concatenate([top, rolled[SUBLANES:]], axis=0)


def _shift_up(u, carry8, k):
    if k == 0:
        return u
    n = u.shape[0]
    rolled = pltpu.roll(u, n - k, 0)
    row = lax.broadcasted_iota(jnp.int32, carry8.shape, 0)
    bot = jnp.where(row >= SUBLANES - k, pltpu.roll(carry8, SUBLANES - k, 0),
                    rolled[n - SUBLANES:])
    return jnp.concatenate([rolled[:n - SUBLANES], bot], axis=0)


def _neg_expm1(t):
    e = jnp.exp(t)
    em1 = e - 1.0
    safe = jnp.where(e == 1.0, 1.0, jnp.log(e))
    return -jnp.where(e == 1.0, t, jnp.where(em1 == -1.0, -1.0, em1 * t / safe))


def _lru_gates(u, carry8, cw_ref, vec_ref, wr_ref, wi_ref):
    xc = vec_ref[0:1, :] + cw_ref[3:4, :] * u
    for k in range(1, CONV_W):
        xc = xc + cw_ref[3 - k:4 - k, :] * _shift_down(u, carry8, k)
    xb = xc.astype(BF16)
    r = _sigmoid(_dot(xb, wr_ref[...]) + vec_ref[1:2, :])
    ig = _sigmoid(_dot(xb, wi_ref[...]) + vec_ref[2:3, :])
    nlam = -vec_ref[3:4, :]
    softplus = jnp.maximum(nlam, 0.0) + jnp.log(1.0 + jnp.exp(-jnp.abs(nlam)))
    cneg = -LRU_C * softplus
    log_a = cneg * r
    a = jnp.exp(log_a)
    sq = jnp.sqrt(_neg_expm1(2.0 * log_a))
    return xc, r, ig, cneg, a, sq


def _lru_fwd(z1, cw8, vec8, wr, wi, ts):
    s = z1.shape[0]

    def body(u_ref, cw_ref, vec_ref, wr_ref, wi_ref, hs_ref, hp_ref,
             cu_scr, ch_scr, a_scr, gx_scr):
        @pl.when(pl.program_id(0) == 0)
        def _():
            cu_scr[...] = jnp.zeros_like(cu_scr)
            ch_scr[...] = jnp.zeros_like(ch_scr)

        u = u_ref[...]
        xc, _, ig, _, a, sq = _lru_gates(u, cu_scr[...], cw_ref, vec_ref, wr_ref, wi_ref)
        a_scr[...] = a
        gx_scr[...] = sq * (ig * xc)

        def step(t, h):
            hp_ref[pl.ds(t, 1), :] = h
            h = a_scr[pl.ds(t, 1), :] * h + gx_scr[pl.ds(t, 1), :]
            hs_ref[pl.ds(t, 1), :] = h
            return h

        h = lax.fori_loop(0, ts, step, ch_scr[0:1, :], unroll=8)
        ch_scr[0:1, :] = h
        cu_scr[...] = u[ts - SUBLANES:, :]

    w = TOK_WIDTH
    out = jax.ShapeDtypeStruct((s, w), F32)
    return pl.pallas_call(
        body, grid=(s // ts,),
        in_specs=[_rows(ts, w), _full((SUBLANES, w)), _full((SUBLANES, w)),
                  _full((w, w)), _full((w, w))],
        out_specs=[_rows(ts, w)] * 2,
        out_shape=[out, out],
        scratch_shapes=[pltpu.VMEM((SUBLANES, w), F32), pltpu.VMEM((SUBLANES, w), F32),
                        pltpu.VMEM((ts, w), F32), pltpu.VMEM((ts, w), F32)],
        name="lru_fwd", compiler_params=_cp(1))(z1, cw8, vec8, wr, wi)


def _lru_bwd(z1, dhs, hprev, cw8, vec8, wr, wi, wr_t, wi_t, ts):
    s = z1.shape[0]
    nb = s // ts
    w = TOK_WIDTH
    tiles = ts // SUBLANES

    def body(u_ref, up_ref, dhs_ref, hp_ref, cw_ref, vec_ref, wr_ref, wi_ref, wrt_ref, wit_ref,
             du_ref, dwr_ref, dwi_ref, dvec_ref, cc_scr, cd_scr, a_scr, dh_scr):
        i = pl.program_id(0)

        @pl.when(i == 0)
        def _():
            cc_scr[...] = jnp.zeros_like(cc_scr)
            cd_scr[...] = jnp.zeros_like(cd_scr)
            dwr_ref[...] = jnp.zeros_like(dwr_ref)
            dwi_ref[...] = jnp.zeros_like(dwi_ref)
            dvec_ref[...] = jnp.zeros_like(dvec_ref)

        u = u_ref[...]
        carry8 = jnp.where(i == nb - 1, 0.0, up_ref[...])
        xc, r, ig, cneg, a, sq = _lru_gates(u, carry8, cw_ref, vec_ref, wr_ref, wi_ref)
        a_scr[...] = a

        def step(n, c):
            t = ts - 1 - n
            dh = dhs_ref[pl.ds(t, 1), :] + c
            dh_scr[pl.ds(t, 1), :] = dh
            return a_scr[pl.ds(t, 1), :] * dh

        cc_scr[0:1, :] = lax.fori_loop(0, ts, step, cc_scr[0:1, :], unroll=8)
        dh = dh_scr[...]
        ix = ig * xc
        dix = dh * sq
        dlog_a = dh * hp_ref[...] * a - (dh * ix) * (a * a) / sq
        dpr = (dlog_a * cneg) * r * (1.0 - r)
        dpi = (dix * xc) * ig * (1.0 - ig)
        dprb, dpib = dpr.astype(BF16), dpi.astype(BF16)
        xb = xc.astype(BF16)
        dwr_ref[...] += _dot_tn(xb, dprb)
        dwi_ref[...] += _dot_tn(xb, dpib)
        dxc = dix * ig + _dot(dprb, wrt_ref[...]) + _dot(dpib, wit_ref[...])
        for k in range(CONV_W):
            dvec_ref[3 - k:4 - k, :] += jnp.sum(dxc * _shift_down(u, carry8, k),
                                                axis=0, keepdims=True)
        dvec_ref[4:5, :] += jnp.sum(dxc, axis=0, keepdims=True)
        dvec_ref[5:6, :] += jnp.sum(dpr, axis=0, keepdims=True)
        dvec_ref[6:7, :] += jnp.sum(dpi, axis=0, keepdims=True)
        dvec_ref[7:8, :] += (jnp.sum(dlog_a * r, axis=0, keepdims=True)
                             * (LRU_C * _sigmoid(-vec_ref[3:4, :])))
        nxt = cd_scr[...]
        du = cw_ref[3:4, :] * dxc
        for k in range(1, CONV_W):
            du = du + cw_ref[3 - k:4 - k, :] * _shift_up(dxc, nxt, k)
        du_ref[...] = du
        cd_scr[...] = dxc[0:SUBLANES, :]

    rev = lambda i: (nb - 1 - i, 0)
    prev8 = lambda i: (jnp.maximum((nb - 1 - i) * tiles - 1, 0), 0)
    blk = pl.BlockSpec((ts, w), rev)
    return pl.pallas_call(
        body, grid=(nb,),
        in_specs=[blk, pl.BlockSpec((SUBLANES, w), prev8), blk, blk,
                  _full((SUBLANES, w)), _full((SUBLANES, w)),
                  _full((w, w)), _full((w, w)), _full((w, w)), _full((w, w))],
        out_specs=[blk, _full((w, w)), _full((w, w)), _full((SUBLANES, w))],
        out_shape=[jax.ShapeDtypeStruct((s, w), F32), jax.ShapeDtypeStruct((w, w), F32),
                   jax.ShapeDtypeStruct((w, w), F32), jax.ShapeDtypeStruct((SUBLANES, w), F32)],
        scratch_shapes=[pltpu.VMEM((SUBLANES, w), F32), pltpu.VMEM((SUBLANES, w), F32),
                        pltpu.VMEM((ts, w), F32), pltpu.VMEM((ts, w), F32)],
        name="lru_bwd", compiler_params=_cp(1))(
            z1, z1, dhs, hprev, cw8, vec8, wr, wi, wr_t, wi_t)


def _adamw(parts, w, m, v, name):
    n = len(parts)
    rows_per = parts[0].shape[1]

    def body(*refs):
        p_refs = refs[:n]
        w_ref, m_ref, v_ref, g_ref, d_ref, nm_ref, nv_ref = refs[n:]
        for l, p_ref in enumerate(p_refs):
            rows = slice(l * rows_per, (l + 1) * rows_per)
            g = p_ref[0]
            for dev in range(1, N_DEV):
                g = g + p_ref[dev]
            g_ref[rows, :] = g
            nm = ADAM_B1 * m_ref[rows, :] + (1.0 - ADAM_B1) * g
            nv = ADAM_B2 * v_ref[rows, :] + (1.0 - ADAM_B2) * (g * g)
            m_hat = nm / (1.0 - ADAM_B1 ** ADAM_STEP)
            v_hat = nv / (1.0 - ADAM_B2 ** ADAM_STEP)
            d_ref[rows, :] = -ADAM_LR * (m_hat / (jnp.sqrt(v_hat) + ADAM_EPS)
                                         + ADAM_WD * w_ref[rows, :])
            nm_ref[rows, :] = nm
            nv_ref[rows, :] = nv

    out = jax.ShapeDtypeStruct(w.shape, F32)
    return pl.pallas_call(
        body, out_shape=[out] * 4, name=name,
        compiler_params=pltpu.CompilerParams(vmem_limit_bytes=VMEM_LIMIT))(*parts, w, m, v)


ANY = pl.BlockSpec(memory_space=pl.ANY)
MESH = pl.DeviceIdType.MESH


def _slot(p):
    return 4 * p[0] + 2 * p[1] + p[2]


def _allgather(xs):
    n = len(xs)

    def body(*refs):
        x_refs, o_refs = refs[:n], refs[n:2 * n]
        send_sems, recv_sems, local_sems = refs[2 * n:]
        x, y, c = lax.axis_index("x"), lax.axis_index("y"), lax.axis_index("c")
        me, sibling = (x, y, c), (x, y, 1 - c)
        chips = [(1 - x, y), (x, 1 - y), (1 - x, 1 - y)]

        def copy(a, k, block, to, from_input=False):
            dst = o_refs[a].at[_slot(block)]
            return pltpu.make_async_remote_copy(
                src_ref=x_refs[a] if from_input else dst, dst_ref=dst,
                send_sem=send_sems.at[a, k], recv_sem=recv_sems.at[a, k],
                device_id=to, device_id_type=MESH)

        mine = [pltpu.make_async_copy(x_refs[a], o_refs[a].at[_slot(me)], local_sems.at[a])
                for a in range(n)]
        for cp in mine:
            cp.start()
        first = []
        for a in range(n):
            first.append(copy(a, 0, me, sibling, True))
            first += [copy(a, 1 + j, me, (*chip, c), True) for j, chip in enumerate(chips)]
        for cp in first:
            cp.start()
        passed = []
        for j, chip in enumerate(chips):
            for a in range(n):
                copy(a, 1 + j, (*chip, c), me).wait_recv()
                cp = copy(a, 4 + j, (*chip, c), sibling)
                cp.start()
                passed.append(cp)
        for a in range(n):
            copy(a, 0, sibling, me).wait_recv()
            for j, chip in enumerate(chips):
                copy(a, 4 + j, (*chip, 1 - c), me).wait_recv()
        for cp in first + passed:
            cp.wait_send()
        for cp in mine:
            cp.wait()

    return pl.pallas_call(
        body,
        out_shape=[jax.ShapeDtypeStruct((N_DEV,) + t.shape, t.dtype) for t in xs],
        in_specs=[ANY] * n, out_specs=[ANY] * n,
        scratch_shapes=[pltpu.SemaphoreType.DMA((n, 7)), pltpu.SemaphoreType.DMA((n, 7)),
                        pltpu.SemaphoreType.DMA((n,))],
        name="allgather_weights")(*xs)


def _exchange_grads(arrays, kinds):
    n = len(arrays)

    def part_shape(arr, kind):
        if kind == "chunks":
            return arr.shape[1:]
        if kind == "cols":
            return (arr.shape[0], arr.shape[1] // N_DEV)
        if kind == "rows":
            return (arr.shape[0] // N_DEV, arr.shape[1])
        return arr.shape

    shapes = [part_shape(arr, kind) for arr, kind in zip(arrays, kinds)]

    def body(*refs):
        in_refs, out_refs = refs[:n], refs[n:2 * n]
        send_sems, recv_sems, local_sems = refs[2 * n:]
        x, y, c = lax.axis_index("x"), lax.axis_index("y"), lax.axis_index("c")
        me = _slot((x, y, c))

        def part(a, dev):
            ref, kind, shp = in_refs[a], kinds[a], shapes[a]
            if kind == "chunks":
                return ref.at[dev]
            if kind == "cols":
                return ref.at[:, pl.ds(pl.multiple_of(dev * shp[1], LANES), shp[1])]
            if kind == "rows":
                return ref.at[pl.ds(pl.multiple_of(dev * shp[0], SUBLANES), shp[0]), :]
            return ref

        own = [pltpu.make_async_copy(part(a, me), out_refs[a].at[me], local_sems.at[a])
               for a in range(n)]
        for cp in own:
            cp.start()
        copies = []
        for rel in range(1, N_DEV):
            peer = (x ^ (rel >> 2), y ^ ((rel >> 1) & 1), c ^ (rel & 1))
            for a in range(n):
                copies.append(pltpu.make_async_remote_copy(
                    src_ref=part(a, _slot(peer)), dst_ref=out_refs[a].at[me],
                    send_sem=send_sems.at[a, rel - 1], recv_sem=recv_sems.at[a, rel - 1],
                    device_id=peer, device_id_type=MESH))
        for cp in copies:
            cp.start()
        for cp in copies:
            cp.wait()
        for cp in own:
            cp.wait()

    return pl.pallas_call(
        body,
        out_shape=[jax.ShapeDtypeStruct((N_DEV,) + shp, F32) for shp in shapes],
        in_specs=[ANY] * n, out_specs=[ANY] * n,
        scratch_shapes=[pltpu.SemaphoreType.DMA((n, 7)), pltpu.SemaphoreType.DMA((n, 7)),
                        pltpu.SemaphoreType.DMA((n,))],
        name="exchange_grads")(*arrays)


BIG = [("mla_w_in", (D_MODEL, MLA_IN), 1), ("mla_w_uq", (Q_LORA, N_TOK_HEADS * QK_DIM), 1),
       ("mla_w_ukv", (KV_LORA, N_TOK_HEADS * 2 * HEAD_DIM), 1), ("lru_w_in", (D_MODEL, LRU_IN), 1),
       ("w_mem_kv", (2, D_MODEL, 2 * MEM_WIDTH), 1), ("w_out", (2, MIX_WIDTH, D_MODEL), 1)]
SMALL = [("lru_conv_w", (CONV_W, TOK_WIDTH), 1), ("lru_conv_b", (TOK_WIDTH,), 0),
         ("lru_b_rgate", (TOK_WIDTH,), 0), ("lru_b_igate", (TOK_WIDTH,), 0),
         ("lru_lambda", (TOK_WIDTH,), 0)]
REPL = [("mla_q_norm", (Q_LORA,)), ("mla_kv_norm", (KV_LORA,)),
        ("lru_w_rgate", (N_TOK_HEADS, HEAD_DIM, HEAD_DIM)),
        ("lru_w_igate", (N_TOK_HEADS, HEAD_DIM, HEAD_DIM)),
        ("ln_g", (2, D_MODEL)), ("ln_b", (2, D_MODEL))]


def _shard_shape(shape, axis):
    return tuple(d // N_DEV if a == axis else d for a, d in enumerate(shape))


def _size(shape):
    return math.prod(shape)


BIG_ROWS = sum(_size(s) for _, s, _ in BIG) // N_DEV // LANES
SMALL_ROWS = SUBLANES


def _pack_rows(flat_parts, rows):
    flat = jnp.concatenate([p.reshape(-1) for p in flat_parts])
    return jnp.pad(flat, (0, rows * LANES - flat.shape[0])).reshape(rows, LANES)


def _to_chunks(full, axis):
    shape = full.shape
    split = shape[:axis] + (N_DEV, shape[axis] // N_DEV) + shape[axis + 1:]
    return jnp.moveaxis(full.reshape(split), axis, 0).reshape(N_DEV, -1)


def _from_chunks(chunks, shape, axis):
    sh = _shard_shape(shape, axis)
    t = chunks.reshape((N_DEV,) + sh)
    t = jnp.moveaxis(t, 0, axis)
    return t.reshape(shape)


def _split_flat(flat2d, table):
    out, off = [], 0
    for size in table:
        out.append(flat2d[:, off:off + size])
        off += size
    return out


def _win0_to_padded(w):
    z = lambda n: jnp.zeros((w.shape[0], n), w.dtype)
    return jnp.concatenate([w[:, 0:640], w[:, 672:1952], z(KR_LANE), w[:, 640:672],
                            z(LANES - KR_LANE - QK_ROPE)], axis=1)


def _win0_from_padded(wp):
    k0 = Z0_KR + KR_LANE
    return jnp.concatenate([wp[:, 0:640], wp[:, k0:k0 + QK_ROPE], wp[:, 640:1920]], axis=1)


def _pad_heads(w, per_head, lo, hi):
    t = w.reshape(w.shape[0], N_TOK_HEADS, per_head)[:, :, lo:hi]
    t = jnp.pad(t, ((0, 0), (0, 0), (0, HEAD_PAD - (hi - lo))))
    return t.reshape(w.shape[0], QKV_PAD)


def _unpad_heads(wp, width):
    return wp.reshape(wp.shape[0], N_TOK_HEADS, HEAD_PAD)[:, :, :width]


def _block_diag(w):
    eye = jnp.eye(N_TOK_HEADS, dtype=w.dtype)
    return (w[:, :, None, :] * eye[:, None, :, None]).reshape(TOK_WIDTH, TOK_WIDTH)


def _diag_blocks(d):
    t = d.reshape(N_TOK_HEADS, HEAD_DIM, N_TOK_HEADS, HEAD_DIM)
    return jnp.stack([t[g, :, g, :] for g in range(N_TOK_HEADS)])


def _rope_tables(positions):
    half = QK_ROPE // 2
    inv_freq = ROPE_THETA ** (-jnp.arange(half, dtype=F32) / half)
    ang = positions.astype(F32)[:, None] * inv_freq
    cos, sin = jnp.cos(ang), jnp.sin(ang)
    s = positions.shape[0]
    one, zero = jnp.ones((s, QK_NOPE), F32), jnp.zeros((s, half), F32)
    tail = jnp.zeros((s, HEAD_PAD - QK_DIM), F32)
    znope = jnp.zeros((s, QK_NOPE), F32)
    c = jnp.concatenate([one, cos, cos, tail], axis=1)
    sa = jnp.concatenate([znope, -sin, zero, tail], axis=1)
    sb = jnp.concatenate([znope, zero, sin, tail], axis=1)
    return c, sa, sb


def _local_step(x, mem, positions, tgt, wts, ts, tatt):
    bf = lambda t: t.astype(BF16)
    win0 = _win0_to_padded(wts["mla_w_in"])
    wuq = _pad_heads(wts["mla_w_uq"], QK_DIM, 0, QK_DIM)
    wukv = jnp.concatenate([_pad_heads(wts["mla_w_ukv"], 2 * HEAD_DIM, 0, QK_NOPE),
                            _pad_heads(wts["mla_w_ukv"], 2 * HEAD_DIM, QK_NOPE, 2 * HEAD_DIM)],
                           axis=1)
    win1 = wts["lru_w_in"]
    wmkv, wout = wts["w_mem_kv"], wts["w_out"]
    gq = wts["mla_q_norm"].reshape(1, Q_LORA)
    gkv = wts["mla_kv_norm"].reshape(1, KV_LORA)
    ln_g, ln_b = wts["ln_g"], wts["ln_b"]
    wr, wi = bf(_block_diag(wts["lru_w_rgate"])), bf(_block_diag(wts["lru_w_igate"]))
    cw8 = jnp.pad(wts["lru_conv_w"], ((0, SUBLANES - CONV_W), (0, 0)))
    vec8 = jnp.pad(jnp.stack([wts["lru_conv_b"], wts["lru_b_rgate"], wts["lru_b_igate"],
                              wts["lru_lambda"]]), ((0, SUBLANES - 4), (0, 0)))
    tabs = _rope_tables(positions)
    tmem = mem.shape[0]

    z0 = _rowmm(x, win0, "in_proj0", ts)
    q, k, v = _mla_prep_fwd(z0, tabs, gq, gkv, wuq, wukv, ts)
    o, lse = _flash_fwd(q, k, v, tatt, FWD_HEADS)
    mkv0 = _rowmm(mem, wmkv[0], "mem_kv0", tmem)
    cat0, y0 = _gate_mem_fwd(o, z0, mkv0, Z0_GATE, Z0_QMEM, True, "gate_mem_fwd0", ts)
    del o
    pre0, h1 = _outproj_ln_fwd(y0, wout[0], x, ln_g[0:1], ln_b[0:1], None, "outproj_ln_fwd0", ts)
    z1 = _rowmm(h1, win1, "in_proj1", ts)
    hs, hprev = _lru_fwd(z1, cw8, vec8, wr, wi, ts)
    mkv1 = _rowmm(mem, wmkv[1], "mem_kv1", tmem)
    cat1, y1 = _gate_mem_fwd(hs, z1, mkv1, Z1_GATE, Z1_QMEM, False, "gate_mem_fwd1", ts)
    pre1, dh2, loss8 = _outproj_ln_fwd(y1, wout[1], h1, ln_g[1:2], ln_b[1:2], tgt,
                                       "outproj_ln_loss", ts)
    loss = loss8[0, 0]

    dpre1, dy1, dwout1, dgb1 = _outproj_ln_bwd(dh2, pre1, ln_g[1:2], y1, wout[1].T,
                                               "outproj_ln_bwd1", ts)
    dzg1, dhs, dmkv1 = _gate_mem_bwd(dy1, cat1, z1, mkv1, None, Z1_GATE, Z1_QMEM,
                                     "gate_mem_bwd1", ts)
    du, dwr, dwi, dvec = _lru_bwd(z1, dhs, hprev, cw8, vec8, wr, wi, wr.T, wi.T, ts)
    dh1, dwin1 = _linear_bwd(h1, [du, dzg1], [Z1_U, Z1_GATE], win1.T, dpre1, "in_proj_bwd1", ts)
    dwmkv1 = _wgrad_small(mem, dmkv1, "mem_kv_bwd1")
    dpre0, dy0, dwout0, dgb0 = _outproj_ln_bwd(dh1, pre0, ln_g[0:1], y0, wout[0].T,
                                               "outproj_ln_bwd0", ts)
    dzg0, do, dmkv0, stats = _gate_mem_bwd(dy0, cat0, z0, mkv0, lse, Z0_GATE, Z0_QMEM,
                                           "gate_mem_bwd0", ts)
    dq, dk, dv = _flash_bwd(q, k, v, stats, do, tatt, BWD_HEADS)
    dza, dzk, dwuq_p, dwukv_p, dg = _mla_prep_bwd(z0, dq, dk, dv, tabs, gq, gkv,
                                                  wuq.T, wukv.T, ts)
    gx, dwin0_p = _linear_bwd(x, [dza, dzg0, dzk], [Z0_CQ, Z0_GATE, Z0_KR], win0.T, dpre0,
                              "in_proj_bwd0", ts)
    dwmkv0 = _wgrad_small(mem, dmkv0, "mem_kv_bwd0")

    dwukv = jnp.concatenate([_unpad_heads(dwukv_p[:, :QKV_PAD], HEAD_DIM),
                             _unpad_heads(dwukv_p[:, QKV_PAD:], HEAD_DIM)], axis=2)
    zrow = jnp.zeros((1, D_MODEL), F32)
    gains = jnp.pad(dg[0:1], ((0, 0), (0, D_MODEL - Q_LORA - KV_LORA)))
    small_repl = jnp.concatenate([dgb0[0:2], dgb1[0:2], gains,
                                  loss * jnp.ones((1, D_MODEL), F32), zrow, zrow], axis=0)
    grads = {
        "mla_w_in": _win0_from_padded(dwin0_p),
        "mla_w_uq": _unpad_heads(dwuq_p, QK_DIM).reshape(Q_LORA, N_TOK_HEADS * QK_DIM),
        "mla_w_ukv": dwukv.reshape(KV_LORA, N_TOK_HEADS * 2 * HEAD_DIM),
        "lru_w_in": dwin1,
        "lru_small": dvec,
        "lru_w_rgate": _diag_blocks(dwr).reshape(TOK_WIDTH, HEAD_DIM),
        "lru_w_igate": _diag_blocks(dwi).reshape(TOK_WIDTH, HEAD_DIM),
        "w_mem_kv": [dwmkv0, dwmkv1],
        "w_out": [dwout0, dwout1],
        "small_repl": small_repl,
    }
    return gx, grads


WEIGHT_ORDER = ["mla_w_in", "mla_q_norm", "mla_w_uq", "mla_kv_norm", "mla_w_ukv", "lru_w_in",
                "lru_conv_w", "lru_conv_b", "lru_w_rgate", "lru_b_rgate", "lru_w_igate",
                "lru_b_igate", "lru_lambda", "w_mem_kv", "w_out", "ln_g", "ln_b"]


def kernel(x, mem, positions, mla_w_in, mla_q_norm, mla_w_uq, mla_kv_norm, mla_w_ukv, lru_w_in, lru_conv_w, lru_conv_b, lru_w_rgate, lru_b_rgate, lru_w_igate, lru_b_igate, lru_lambda, w_mem_kv, w_out, ln_g, ln_b, loss_target, m_mla_w_in, m_mla_q_norm, m_mla_w_uq, m_mla_kv_norm, m_mla_w_ukv, m_lru_w_in, m_lru_conv_w, m_lru_conv_b, m_lru_w_rgate, m_lru_b_rgate, m_lru_w_igate, m_lru_b_igate, m_lru_lambda, m_w_mem_kv, m_w_out, m_ln_g, m_ln_b, v_mla_w_in, v_mla_q_norm, v_mla_w_uq, v_mla_kv_norm, v_mla_w_ukv, v_lru_w_in, v_lru_conv_w, v_lru_conv_b, v_lru_w_rgate, v_lru_b_rgate, v_lru_w_igate, v_lru_b_igate, v_lru_lambda, v_w_mem_kv, v_w_out, v_ln_g, v_ln_b):
    w_in = dict(mla_w_in=mla_w_in, mla_q_norm=mla_q_norm, mla_w_uq=mla_w_uq,
                mla_kv_norm=mla_kv_norm, mla_w_ukv=mla_w_ukv, lru_w_in=lru_w_in,
                lru_conv_w=lru_conv_w, lru_conv_b=lru_conv_b, lru_w_rgate=lru_w_rgate,
                lru_b_rgate=lru_b_rgate, lru_w_igate=lru_w_igate, lru_b_igate=lru_b_igate,
                lru_lambda=lru_lambda, w_mem_kv=w_mem_kv, w_out=w_out, ln_g=ln_g, ln_b=ln_b)
    m_in = dict(mla_w_in=m_mla_w_in, mla_q_norm=m_mla_q_norm, mla_w_uq=m_mla_w_uq,
                mla_kv_norm=m_mla_kv_norm, mla_w_ukv=m_mla_w_ukv, lru_w_in=m_lru_w_in,
                lru_conv_w=m_lru_conv_w, lru_conv_b=m_lru_conv_b, lru_w_rgate=m_lru_w_rgate,
                lru_b_rgate=m_lru_b_rgate, lru_w_igate=m_lru_w_igate, lru_b_igate=m_lru_b_igate,
                lru_lambda=m_lru_lambda, w_mem_kv=m_w_mem_kv, w_out=m_w_out, ln_g=m_ln_g,
                ln_b=m_ln_b)
    v_in = dict(mla_w_in=v_mla_w_in, mla_q_norm=v_mla_q_norm, mla_w_uq=v_mla_w_uq,
                mla_kv_norm=v_mla_kv_norm, mla_w_ukv=v_mla_w_ukv, lru_w_in=v_lru_w_in,
                lru_conv_w=v_lru_conv_w, lru_conv_b=v_lru_conv_b, lru_w_rgate=v_lru_w_rgate,
                lru_b_rgate=v_lru_b_rgate, lru_w_igate=v_lru_w_igate, lru_b_igate=v_lru_b_igate,
                lru_lambda=v_lru_lambda, w_mem_kv=v_w_mem_kv, w_out=v_w_out, ln_g=v_ln_g,
                ln_b=v_ln_b)
    s = x.shape[1]
    ts = min(ROW_BLOCK, s)
    tatt = min(ATT_BLOCK, s)
    big_sizes = [_size(sh) // N_DEV for _, sh, _ in BIG]
    small_sizes = [_size(sh) // N_DEV for _, sh, _ in SMALL]

    big_local = _pack_rows([w_in[n] for n, _, _ in BIG], BIG_ROWS).astype(BF16)
    small_local = _pack_rows([w_in[n] for n, _, _ in SMALL], SMALL_ROWS)
    big_all, small_all = _allgather([big_local, small_local])
    wts = {}
    for (n, sh, ax), part in zip(BIG, _split_flat(big_all.reshape(N_DEV, -1), big_sizes)):
        wts[n] = _from_chunks(part, sh, ax)
    for (n, sh, ax), part in zip(SMALL, _split_flat(small_all.reshape(N_DEV, -1), small_sizes)):
        wts[n] = _from_chunks(part, sh, ax)
    for n, sh in REPL:
        wts[n] = w_in[n].reshape(sh)

    gx, grads = _local_step(x[0], mem[0], positions[0], loss_target[0], wts, ts, tatt)

    def chunked(name, shape):
        w = shape[1] // N_DEV
        return _to_chunks(grads[name], 1).reshape(N_DEV, shape[0], w)

    small_chunks = jnp.moveaxis(grads["lru_small"].reshape(SUBLANES, N_DEV, -1), 1, 0)
    sends = [(chunked("mla_w_in", (D_MODEL, MLA_IN)), "chunks"),
             (chunked("mla_w_uq", (Q_LORA, N_TOK_HEADS * QK_DIM)), "chunks"),
             (chunked("mla_w_ukv", (KV_LORA, N_TOK_HEADS * 2 * HEAD_DIM)), "chunks"),
             (grads["lru_w_in"], "cols"),
             (grads["w_mem_kv"][0], "rows"), (grads["w_mem_kv"][1], "rows"),
             (grads["w_out"][0], "rows"), (grads["w_out"][1], "rows"),
             (small_chunks, "chunks"),
             (grads["lru_w_rgate"], "all"), (grads["lru_w_igate"], "all"),
             (grads["small_repl"], "all")]
    got = _exchange_grads([a for a, _ in sends], [k for _, k in sends])

    def small_sharded(d):
        return jnp.concatenate([d["lru_conv_w"].reshape(CONV_W, -1), d["lru_conv_b"],
                                d["lru_b_rgate"], d["lru_b_igate"], d["lru_lambda"]], axis=0)

    def small_replicated(d):
        gains = jnp.concatenate([d["mla_q_norm"], d["mla_kv_norm"]], axis=1)
        gains = jnp.pad(gains, ((0, 0), (0, D_MODEL - gains.shape[1])))
        return jnp.concatenate([d["ln_g"][0:1], d["ln_b"][0:1], d["ln_g"][1:2], d["ln_b"][1:2],
                                gains, jnp.zeros((3, D_MODEL), F32)], axis=0)

    def flat2(d, name):
        t = d[name]
        return t.reshape(-1, t.shape[-1])

    def update(parts, view, name):
        return _adamw(parts, view(w_in), view(m_in), view(v_in), "adamw_" + name)

    res = {}
    for idx, name in [(0, "mla_w_in"), (1, "mla_w_uq"), (2, "mla_w_ukv"), (3, "lru_w_in"),
                      (9, "lru_w_rgate"), (10, "lru_w_igate")]:
        res[name] = update([got[idx]], functools.partial(flat2, name=name), name)
    res["w_mem_kv"] = update([got[4], got[5]], functools.partial(flat2, name="w_mem_kv"),
                             "w_mem_kv")
    res["w_out"] = update([got[6], got[7]], functools.partial(flat2, name="w_out"), "w_out")
    res_ss = update([got[8]], small_sharded, "small_sharded")
    res_sr = update([got[11]], small_replicated, "small_replicated")
    loss = res_sr[0][5, 0]

    result = [loss, gx.reshape(x.shape)]
    for kind in range(4):
        ss, sr = res_ss[kind], res_sr[kind]
        out = {n: res[n][kind].reshape(w_in[n].shape) for n in res}
        out["lru_conv_w"] = ss[0:CONV_W].reshape(w_in["lru_conv_w"].shape)
        out["lru_conv_b"], out["lru_b_rgate"] = ss[4:5], ss[5:6]
        out["lru_b_igate"], out["lru_lambda"] = ss[6:7], ss[7:8]
        out["ln_g"] = jnp.concatenate([sr[0:1], sr[2:3]], axis=0)
        out["ln_b"] = jnp.concatenate([sr[1:2], sr[3:4]], axis=0)
        out["mla_q_norm"] = sr[4:5, 0:Q_LORA]
        out["mla_kv_norm"] = sr[4:5, Q_LORA:Q_LORA + KV_LORA]
        result += [out[n] for n in WEIGHT_ORDER]
    return tuple(result)
```

```python
import functools
import math

import jax
import jax.numpy as jnp
from jax import lax
from jax.experimental import pallas as pl
from jax.experimental.pallas import tpu as pltpu

F32 = jnp.float32
BF16 = jnp.bfloat16

D_MODEL = 1024
MEM_LEN = 256
HEAD_DIM = 64
N_TOK_HEADS = 12
N_MEM_HEADS = 4
TOK_WIDTH = 768
MEM_WIDTH = 256
MIX_WIDTH = 1024
Q_LORA = 384
KV_LORA = 256
QK_NOPE = 64
QK_ROPE = 32
QK_DIM = 96
ROPE_THETA = 10000.0
CONV_W = 4
LRU_C = 8.0
ALPHA = (2.0 * 2) ** 0.25
NORM_EPS = 1e-6
MLA_IN = 1952
LRU_IN = 2048
ADAM_LR = 0.001
ADAM_B1 = 0.9
ADAM_B2 = 0.999
ADAM_EPS = 1e-08
ADAM_WD = 0.01
ADAM_STEP = 10

N_DEV = 8
LANES = 128
SUBLANES = 8
HEAD_PAD = 128
QKV_PAD = N_TOK_HEADS * HEAD_PAD
ZP = 2048
Z0_CQ, Z0_CKV, Z0_GATE, Z0_QMEM, Z0_KR = 0, 384, 640, 1664, 1920
KR_LANE = 64
Z1_U, Z1_GATE, Z1_QMEM = 0, 768, 1792

ROW_BLOCK = 512
ATT_BLOCK = 512
FWD_HEADS = 4
BWD_HEADS = 2
VMEM_LIMIT = 56 * 1024 * 1024
NEG_BIG = -1e30
STRIP = 32
LOG2E = math.log2(math.e)


def _cp(n_axes):
    return pltpu.CompilerParams(dimension_semantics=("arbitrary",) * n_axes,
                                vmem_limit_bytes=VMEM_LIMIT)


def _dot(a, b):
    return jnp.dot(a, b, preferred_element_type=F32)


def _dot_nt(a, b):
    return lax.dot_general(a, b, (((1,), (1,)), ((), ())), preferred_element_type=F32)


def _dot_tn(a, b):
    return lax.dot_general(a, b, (((0,), (0,)), ((), ())), preferred_element_type=F32)


def _sigmoid(t):
    return 1.0 / (1.0 + jnp.exp(-t))


def _lane(shape):
    return lax.broadcasted_iota(jnp.int32, shape, len(shape) - 1)


def _full(shape):
    nd = len(shape)
    return pl.BlockSpec(shape, lambda *_: (0,) * nd)


def _rows(ts, width, col=0):
    return pl.BlockSpec((ts, width), lambda i: (i, col))


def _heads(ts):
    return pl.BlockSpec((N_TOK_HEADS, ts, HEAD_PAD), lambda i: (0, i, 0))


def _rowmm(x, w, name, ts):
    s, k = x.shape
    n = w.shape[1]

    def body(x_ref, w_ref, o_ref):
        o_ref[...] = _dot(x_ref[...].astype(BF16), w_ref[...])

    return pl.pallas_call(
        body, grid=(s // ts,),
        in_specs=[_rows(ts, k), _full((k, n))],
        out_specs=_rows(ts, n),
        out_shape=jax.ShapeDtypeStruct((s, n), F32),
        name=name, compiler_params=_cp(1))(x, w)


def _rms_parts(t):
    rs = lax.rsqrt(jnp.mean(t * t, axis=-1, keepdims=True) + NORM_EPS)
    return rs


def _rope(t, c, sa, sb):
    return t * c + pltpu.roll(t, LANES - 16, 1) * sa + pltpu.roll(t, 16, 1) * sb


def _rope_t(d, c, sa, sb):
    return d * c + pltpu.roll(d * sa, 16, 1) + pltpu.roll(d * sb, LANES - 16, 1)


def _mla_prep_fwd(z0, tabs, gq, gkv, wuq, wukv, ts):
    s = z0.shape[0]

    def body(z_ref, c_ref, sa_ref, sb_ref, gq_ref, gkv_ref, wuq_ref, wukv_ref,
             q_ref, k_ref, v_ref):
        cq = z_ref[:, Z0_CQ:Z0_CQ + Q_LORA]
        ckv = z_ref[:, Z0_CKV:Z0_CKV + KV_LORA]
        kr = z_ref[:, Z0_KR:Z0_KR + LANES]
        cqn = cq * _rms_parts(cq) * gq_ref[...]
        ckvn = ckv * _rms_parts(ckv) * gkv_ref[...]
        q = _dot(cqn.astype(BF16), wuq_ref[...])
        kv = _dot(ckvn.astype(BF16), wukv_ref[...])
        c, sa, sb = c_ref[...], sa_ref[...], sb_ref[...]
        krope = _rope(kr, c, sa, sb)
        pad_lane = _lane((ts, HEAD_PAD)) >= HEAD_DIM
        for h in range(N_TOK_HEADS):
            sl = slice(h * HEAD_PAD, (h + 1) * HEAD_PAD)
            q_ref[h] = _rope(q[:, sl], c, sa, sb).astype(BF16)
            k_ref[h] = (kv[:, sl] + krope).astype(BF16)
            vh = kv[:, QKV_PAD + h * HEAD_PAD:QKV_PAD + (h + 1) * HEAD_PAD]
            v_ref[h] = jnp.where(pad_lane, 1.0, vh).astype(BF16)

    out = jax.ShapeDtypeStruct((N_TOK_HEADS, s, HEAD_PAD), BF16)
    return pl.pallas_call(
        body, grid=(s // ts,),
        in_specs=[_rows(ts, ZP), _rows(ts, LANES), _rows(ts, LANES), _rows(ts, LANES),
                  _full((1, Q_LORA)), _full((1, KV_LORA)),
                  _full((Q_LORA, QKV_PAD)), _full((KV_LORA, 2 * QKV_PAD))],
        out_specs=[_heads(ts)] * 3,
        out_shape=[out, out, out],
        name="mla_prep_fwd", compiler_params=_cp(1))(z0, *tabs, gq, gkv, wuq, wukv)


def _mla_prep_bwd(z0, dq, dk, dv, tabs, gq, gkv, wuq_t, wukv_t, ts):
    s = z0.shape[0]

    def body(z_ref, dq_ref, dk_ref, dv_ref, c_ref, sa_ref, sb_ref, gq_ref, gkv_ref,
             wuqt_ref, wukvt_ref, dza_ref, dzk_ref, dwuq_ref, dwukv_ref, dg_ref):
        @pl.when(pl.program_id(0) == 0)
        def _():
            dwuq_ref[...] = jnp.zeros_like(dwuq_ref)
            dwukv_ref[...] = jnp.zeros_like(dwukv_ref)
            dg_ref[...] = jnp.zeros_like(dg_ref)

        cq = z_ref[:, Z0_CQ:Z0_CQ + Q_LORA]
        ckv = z_ref[:, Z0_CKV:Z0_CKV + KV_LORA]
        rq, rkv = _rms_parts(cq), _rms_parts(ckv)
        gq_, gkv_ = gq_ref[...], gkv_ref[...]
        cqn = (cq * rq * gq_).astype(BF16)
        ckvn = (ckv * rkv * gkv_).astype(BF16)
        c, sa, sb = c_ref[...], sa_ref[...], sb_ref[...]
        dqp, dksum = [], None
        for h in range(N_TOK_HEADS):
            dqp.append(_rope_t(dq_ref[h], c, sa, sb))
            dksum = dk_ref[h] if dksum is None else dksum + dk_ref[h]
        dqp = jnp.concatenate(dqp, axis=1).astype(BF16)
        lane = _lane(dksum.shape)
        dzk_ref[...] = jnp.where((lane >= KR_LANE) & (lane < KR_LANE + QK_ROPE),
                                 _rope_t(dksum, c, sa, sb), 0.0)
        dkv = jnp.concatenate([dk_ref[h] for h in range(N_TOK_HEADS)]
                              + [dv_ref[h] for h in range(N_TOK_HEADS)], axis=1).astype(BF16)
        dcqn = _dot(dqp, wuqt_ref[...])
        dckvn = _dot(dkv, wukvt_ref[...])
        dwuq_ref[...] += _dot_tn(cqn, dqp)
        dwukv_ref[...] += _dot_tn(ckvn, dkv)
        dg_ref[0:1, 0:Q_LORA] += jnp.sum(dcqn * cq * rq, axis=0, keepdims=True)
        dg_ref[0:1, Q_LORA:Q_LORA + KV_LORA] += jnp.sum(dckvn * ckv * rkv, axis=0, keepdims=True)
        wq = dcqn * gq_
        wkv = dckvn * gkv_
        dcq = rq * wq - cq * (rq * rq * rq) * jnp.mean(wq * cq, axis=-1, keepdims=True)
        dckv = rkv * wkv - ckv * (rkv * rkv * rkv) * jnp.mean(wkv * ckv, axis=-1, keepdims=True)
        dza_ref[:, 0:Q_LORA] = dcq
        dza_ref[:, Q_LORA:Q_LORA + KV_LORA] = dckv

    na = Q_LORA + KV_LORA
    return pl.pallas_call(
        body, grid=(s // ts,),
        in_specs=[_rows(ts, ZP), _heads(ts), _heads(ts), _heads(ts),
                  _rows(ts, LANES), _rows(ts, LANES), _rows(ts, LANES),
                  _full((1, Q_LORA)), _full((1, KV_LORA)),
                  _full((QKV_PAD, Q_LORA)), _full((2 * QKV_PAD, KV_LORA))],
        out_specs=[_rows(ts, na), _rows(ts, LANES), _full((Q_LORA, QKV_PAD)),
                   _full((KV_LORA, 2 * QKV_PAD)), _full((SUBLANES, na))],
        out_shape=[jax.ShapeDtypeStruct((s, na), F32), jax.ShapeDtypeStruct((s, LANES), F32),
                   jax.ShapeDtypeStruct((Q_LORA, QKV_PAD), F32),
                   jax.ShapeDtypeStruct((KV_LORA, 2 * QKV_PAD), F32),
                   jax.ShapeDtypeStruct((SUBLANES, na), F32)],
        name="mla_prep_bwd", compiler_params=_cp(1))(
            z0, dq, dk, dv, *tabs, gq, gkv, wuq_t, wukv_t)


def _causal_pairs(nb, by_key):
    if by_key:
        pairs = [(i, j) for j in range(nb) for i in range(j, nb)]
    else:
        pairs = [(i, j) for i in range(nb) for j in range(i + 1)]
    return (jnp.array([p[0] for p in pairs], jnp.int32),
            jnp.array([p[1] for p in pairs], jnp.int32))


def _flash_fwd(q, k, v, t, nh):
    s = q.shape[1]
    itab, jtab = _causal_pairs(s // t, False)
    c2 = LOG2E / math.sqrt(QK_DIM)
    nch = t // LANES

    def body(it_ref, jt_ref, q_ref, k_ref, v_ref, o_ref, lse_ref, m_scr, acc_scr):
        pair = pl.program_id(1)
        i, j = it_ref[pair], jt_ref[pair]

        @pl.when(j == 0)
        def _():
            m_scr[...] = jnp.full_like(m_scr, NEG_BIG)
            acc_scr[...] = jnp.zeros_like(acc_scr)

        def softmax_strips(masked, hs, sc):
            ps, als = [], []
            for r0 in range(0, t, STRIP):
                rows = slice(r0, r0 + STRIP)
                ch = [sc[rows, n * LANES:(n + 1) * LANES] * c2 for n in range(nch)]
                if masked:
                    rr = r0 + lax.broadcasted_iota(jnp.int32, (STRIP, LANES), 0)
                    cc = lax.broadcasted_iota(jnp.int32, (STRIP, LANES), 1)
                    ch = [jnp.where(cc + n * LANES <= rr, ch[n], NEG_BIG) for n in range(nch)]
                mx = ch[0]
                for n in range(1, nch):
                    mx = jnp.maximum(mx, ch[n])
                m_prev = m_scr[hs, rows, :]
                m_next = jnp.maximum(m_prev, jnp.max(mx, axis=-1, keepdims=True))
                ps.append(jnp.concatenate(
                    [jnp.exp2(ch[n] - m_next).astype(BF16) for n in range(nch)], axis=1))
                als.append(jnp.exp2(m_prev - m_next))
                m_scr[hs, rows, :] = m_next
            return jnp.concatenate(ps, axis=0), jnp.concatenate(als, axis=0)

        def step(masked):
            scores = [_dot_nt(q_ref[hs], k_ref[hs]) for hs in range(nh)]
            for hs, sc in enumerate(scores):
                p, alpha = softmax_strips(masked, hs, sc)
                acc_scr[hs] = alpha * acc_scr[hs] + _dot(p, v_ref[hs])

        @pl.when(j < i)
        def _():
            step(False)

        @pl.when(j == i)
        def _():
            step(True)
            for h in range(nh):
                acc = acc_scr[h]
                l = acc[:, HEAD_DIM:HEAD_DIM + 1]
                o_ref[h] = jnp.where(_lane(acc.shape) < HEAD_DIM, acc / l, 0.0)
                lse_ref[h] = m_scr[h] + jnp.log2(l)

    qspec = pl.BlockSpec((nh, t, HEAD_PAD), lambda h, p, it, jt: (h, it[p], 0))
    kspec = pl.BlockSpec((nh, t, HEAD_PAD), lambda h, p, it, jt: (h, jt[p], 0))
    out = jax.ShapeDtypeStruct((N_TOK_HEADS, s, HEAD_PAD), F32)
    return pl.pallas_call(
        body,
        grid_spec=pltpu.PrefetchScalarGridSpec(
            num_scalar_prefetch=2, grid=(N_TOK_HEADS // nh, itab.shape[0]),
            in_specs=[qspec, kspec, kspec], out_specs=[qspec, qspec],
            scratch_shapes=[pltpu.VMEM((nh, t, HEAD_PAD), F32)] * 2),
        out_shape=[out, out],
        name="flash_fwd", compiler_params=_cp(2))(itab, jtab, q, k, v)


def _flash_bwd(q, k, v, stats, do, t, nh, ex):
    s = q.shape[1]
    nb = s // t
    itab, jtab = _causal_pairs(nb, True)
    npairs = itab.shape[0]
    ngroups = N_TOK_HEADS // nh
    scale = 1.0 / math.sqrt(QK_DIM)
    c2 = LOG2E * scale
    nch = t // LANES
    nx = ex.n if ex is not None else 0
    ex_arrays, ex_out_shape, ex_scratch = (
        (ex.arrays, ex.out_shape, ex.scratch) if ex is not None else ([], [], []))

    def body(it_ref, jt_ref, q_ref, k_ref, v_ref, st_ref, do_ref, *rest):
        ex_in, rest = rest[:nx], rest[nx:]
        dq_ref, dk_ref, dv_ref = rest[:3]
        ex_out, rest = rest[3:3 + nx], rest[3 + nx:]
        dk_scr, dv_scr = rest[:2]
        ex_sems = rest[2:]
        pair = pl.program_id(1)
        i, j = it_ref[pair], jt_ref[pair]
        rows_i = pl.ds(pl.multiple_of(i * t, t), t)

        if nx:
            @pl.when(jnp.logical_and(pl.program_id(0) == 0, pair == 0))
            def _():
                for cp in ex.copies(ex_in, ex_out, ex_sems):
                    cp.start()

        @pl.when(i == j)
        def _():
            dk_scr[...] = jnp.zeros_like(dk_scr)
            dv_scr[...] = jnp.zeros_like(dv_scr)

        @pl.when(j == 0)
        def _():
            dq_ref[:, rows_i, :] = jnp.zeros((nh, t, HEAD_PAD), F32)

        def prob_strips(masked, h, sc, dp):
            ps, dss = [], []
            low = _lane((STRIP, LANES)) < HEAD_DIM
            for r0 in range(0, t, STRIP):
                rows = slice(r0, r0 + STRIP)
                st = st_ref[h, rows, :]
                swapped = pltpu.roll(st, HEAD_DIM, 1)
                lse = jnp.where(low, st, swapped)
                delta = jnp.where(low, swapped, st)
                if masked:
                    rr = r0 + lax.broadcasted_iota(jnp.int32, (STRIP, LANES), 0)
                    cc = lax.broadcasted_iota(jnp.int32, (STRIP, LANES), 1)
                pcs, dcs = [], []
                for n in range(nch):
                    cols = slice(n * LANES, (n + 1) * LANES)
                    x = sc[rows, cols] * c2
                    if masked:
                        x = jnp.where(cc + n * LANES <= rr, x, NEG_BIG)
                    p = jnp.exp2(x - lse)
                    pcs.append(p.astype(BF16))
                    dcs.append((p * (dp[rows, cols] - delta) * scale).astype(BF16))
                ps.append(jnp.concatenate(pcs, axis=1))
                dss.append(jnp.concatenate(dcs, axis=1))
            return jnp.concatenate(ps, axis=0), jnp.concatenate(dss, axis=0)

        def step(masked):
            scs = [_dot_nt(q_ref[h], k_ref[h]) for h in range(nh)]
            dps = [_dot_nt(do_ref[h], v_ref[h]) for h in range(nh)]
            for h in range(nh):
                p, ds = prob_strips(masked, h, scs[h], dps[h])
                dv_scr[h] += _dot_tn(p, do_ref[h])
                dk_scr[h] += _dot_tn(ds, q_ref[h])
                dq_ref[h, rows_i, :] += _dot(ds, k_ref[h])

        @pl.when(i > j)
        def _():
            step(False)

        @pl.when(i == j)
        def _():
            step(True)

        @pl.when(i == nb - 1)
        def _():
            dk_ref[...] = dk_scr[...]
            dv_ref[...] = dv_scr[...]

        if nx:
            @pl.when(jnp.logical_and(pl.program_id(0) == ngroups - 1, pair == npairs - 1))
            def _():
                for cp in ex.copies(ex_in, ex_out, ex_sems):
                    cp.wait()

    qspec = pl.BlockSpec((nh, t, HEAD_PAD), lambda h, p, it, jt: (h, it[p], 0))
    kspec = pl.BlockSpec((nh, t, HEAD_PAD), lambda h, p, it, jt: (h, jt[p], 0))
    dqspec = pl.BlockSpec((nh, s, HEAD_PAD), lambda h, p, it, jt: (h, 0, 0))
    out = jax.ShapeDtypeStruct((N_TOK_HEADS, s, HEAD_PAD), F32)
    res = pl.pallas_call(
        body,
        grid_spec=pltpu.PrefetchScalarGridSpec(
            num_scalar_prefetch=2, grid=(ngroups, npairs),
            in_specs=[qspec, kspec, kspec, qspec, qspec] + [ANY] * nx,
            out_specs=[dqspec, kspec, kspec] + [ANY] * nx,
            scratch_shapes=[pltpu.VMEM((nh, t, HEAD_PAD), F32)] * 2 + ex_scratch),
        out_shape=[out, out, out] + ex_out_shape,
        name="flash_bwd", compiler_params=_cp(2))(itab, jtab, q, k, v, stats, do, *ex_arrays)
    return res[:3], res[3:]


def _mem_probs(qp, kp, hh):
    lane = _lane(qp.shape)
    keep = (lane < HEAD_DIM) if hh == 0 else (lane >= HEAD_DIM)
    qh = jnp.where(keep, qp, 0.0).astype(BF16)
    sc = _dot_nt(qh, kp) * (1.0 / math.sqrt(HEAD_DIM))
    e = jnp.exp(sc - jnp.max(sc, axis=-1, keepdims=True))
    return e / jnp.sum(e, axis=-1, keepdims=True), keep


def _gate_mem_fwd(tok, z, memkv, g0, q0, padded, name, ts):
    s = z.shape[0]
    zw = z.shape[1]
    tok_spec = _heads(ts) if padded else _rows(ts, TOK_WIDTH)

    def body(tok_ref, z_ref, mkv_ref, cat_ref, y_ref):
        if padded:
            for p in range(N_TOK_HEADS // 2):
                cat_ref[:, p * LANES:(p + 1) * LANES] = (
                    tok_ref[2 * p] + pltpu.roll(tok_ref[2 * p + 1], HEAD_DIM, 1))
        else:
            cat_ref[:, 0:TOK_WIDTH] = tok_ref[...]
        for pr in range(N_MEM_HEADS // 2):
            sl = slice(pr * LANES, (pr + 1) * LANES)
            qp = z_ref[:, q0 + pr * LANES:q0 + (pr + 1) * LANES]
            kp = mkv_ref[:, sl].astype(BF16)
            vp = mkv_ref[:, MEM_WIDTH + pr * LANES:MEM_WIDTH + (pr + 1) * LANES].astype(BF16)
            outs = []
            for hh in range(2):
                p, _ = _mem_probs(qp, kp, hh)
                outs.append(_dot(p.astype(BF16), vp))
            lane = _lane(outs[0].shape)
            cat_ref[:, TOK_WIDTH + pr * LANES:TOK_WIDTH + (pr + 1) * LANES] = jnp.where(
                lane < HEAD_DIM, outs[0], outs[1])
        gate = z_ref[:, g0:g0 + MIX_WIDTH]
        y_ref[...] = cat_ref[...] * (gate * _sigmoid(gate))

    out = jax.ShapeDtypeStruct((s, MIX_WIDTH), F32)
    return pl.pallas_call(
        body, grid=(s // ts,),
        in_specs=[tok_spec, _rows(ts, zw), _full((MEM_LEN, 2 * MEM_WIDTH))],
        out_specs=[_rows(ts, MIX_WIDTH)] * 2,
        out_shape=[out, out],
        name=name, compiler_params=_cp(1))(tok, z, memkv)


def _gate_mem_bwd(dy, cat, z, memkv, lse, g0, q0, name, ts):
    s = z.shape[0]
    zw = z.shape[1]
    padded = lse is not None
    gq_w = MIX_WIDTH + MEM_WIDTH

    def body(*refs):
        if padded:
            dy_ref, cat_ref, z_ref, mkv_ref, lse_ref, dzg_ref, dtok_ref, dmkv_ref, st_ref = refs
        else:
            dy_ref, cat_ref, z_ref, mkv_ref, dzg_ref, dtok_ref, dmkv_ref = refs

        @pl.when(pl.program_id(0) == 0)
        def _():
            dmkv_ref[...] = jnp.zeros_like(dmkv_ref)

        gate = z_ref[:, g0:g0 + MIX_WIDTH]
        sg = _sigmoid(gate)
        dy_ = dy_ref[...]
        dzg_ref[:, 0:MIX_WIDTH] = dy_ * cat_ref[...] * (sg * (1.0 + gate * (1.0 - sg)))
        dcat = dy_ * (gate * sg)
        if padded:
            low = _lane((ts, LANES)) < HEAD_DIM
            for p in range(N_TOK_HEADS // 2):
                d = dcat[:, p * LANES:(p + 1) * LANES]
                prod = d * cat_ref[:, p * LANES:(p + 1) * LANES]
                first = jnp.sum(jnp.where(low, prod, 0.0), axis=-1, keepdims=True)
                second = jnp.sum(jnp.where(low, 0.0, prod), axis=-1, keepdims=True)
                dtok_ref[2 * p] = jnp.where(low, d, 0.0).astype(BF16)
                dtok_ref[2 * p + 1] = jnp.where(low, pltpu.roll(d, HEAD_DIM, 1), 0.0).astype(BF16)
                st_ref[2 * p] = jnp.where(low, lse_ref[2 * p], first)
                st_ref[2 * p + 1] = jnp.where(low, lse_ref[2 * p + 1], second)
        else:
            dtok_ref[...] = dcat[:, 0:TOK_WIDTH]
        for pr in range(N_MEM_HEADS // 2):
            sl = slice(pr * LANES, (pr + 1) * LANES)
            vsl = slice(MEM_WIDTH + pr * LANES, MEM_WIDTH + (pr + 1) * LANES)
            qp = z_ref[:, q0 + pr * LANES:q0 + (pr + 1) * LANES]
            qpb = qp.astype(BF16)
            kp = mkv_ref[:, sl].astype(BF16)
            vp = mkv_ref[:, vsl].astype(BF16)
            dmo = dcat[:, TOK_WIDTH + pr * LANES:TOK_WIDTH + (pr + 1) * LANES]
            dqp = None
            for hh in range(2):
                p, keep = _mem_probs(qp, kp, hh)
                do_h = jnp.where(keep, dmo, 0.0).astype(BF16)
                dmkv_ref[:, vsl] += _dot_tn(p.astype(BF16), do_h)
                dp = _dot_nt(do_h, vp)
                ds = (p * (dp - jnp.sum(dp * p, axis=-1, keepdims=True))
                      * (1.0 / math.sqrt(HEAD_DIM))).astype(BF16)
                dqh = jnp.where(keep, _dot(ds, kp), 0.0)
                dqp = dqh if dqp is None else dqp + dqh
                dkh = _dot_tn(ds, qpb)
                klane = _lane(dkh.shape)
                kkeep = (klane < HEAD_DIM) if hh == 0 else (klane >= HEAD_DIM)
                dmkv_ref[:, sl] += jnp.where(kkeep, dkh, 0.0)
            dzg_ref[:, MIX_WIDTH + pr * LANES:MIX_WIDTH + (pr + 1) * LANES] = dqp

    in_specs = [_rows(ts, MIX_WIDTH), _rows(ts, MIX_WIDTH), _rows(ts, zw),
                _full((MEM_LEN, 2 * MEM_WIDTH))]
    out_specs = [_rows(ts, gq_w), _heads(ts) if padded else _rows(ts, TOK_WIDTH),
                 _full((MEM_LEN, 2 * MEM_WIDTH))]
    heads_shape = (N_TOK_HEADS, s, HEAD_PAD)
    out_shape = [jax.ShapeDtypeStruct((s, gq_w), F32),
                 jax.ShapeDtypeStruct(heads_shape, BF16) if padded
                 else jax.ShapeDtypeStruct((s, TOK_WIDTH), F32),
                 jax.ShapeDtypeStruct((MEM_LEN, 2 * MEM_WIDTH), F32)]
    args = [dy, cat, z, memkv]
    if padded:
        in_specs.append(_heads(ts))
        out_specs.append(_heads(ts))
        out_shape.append(jax.ShapeDtypeStruct(heads_shape, F32))
        args.append(lse)
    return pl.pallas_call(
        body, grid=(s // ts,), in_specs=in_specs, out_specs=out_specs, out_shape=out_shape,
        name=name, compiler_params=_cp(1))(*args)


def _ln_stats(pre):
    mu = jnp.mean(pre, axis=-1, keepdims=True)
    d = pre - mu
    rstd = lax.rsqrt(jnp.mean(d * d, axis=-1, keepdims=True) + NORM_EPS)
    return d * rstd, rstd


def _outproj_ln_fwd(y, w, h, g, b, tgt, name, ts):
    s = y.shape[0]
    with_loss = tgt is not None

    def body(*refs):
        if with_loss:
            y_ref, w_ref, h_ref, g_ref, b_ref, t_ref, pre_ref, out_ref, loss_ref = refs
        else:
            y_ref, w_ref, h_ref, g_ref, b_ref, pre_ref, out_ref = refs
        pre = ALPHA * h_ref[...] + _dot(y_ref[...].astype(BF16), w_ref[...])
        pre_ref[...] = pre
        xhat, _ = _ln_stats(pre)
        hout = xhat * g_ref[...] + b_ref[...]
        if with_loss:
            @pl.when(pl.program_id(0) == 0)
            def _():
                loss_ref[...] = jnp.zeros_like(loss_ref)
            err = hout - t_ref[...]
            out_ref[...] = err * (1.0 / D_MODEL)
            loss_ref[...] += 0.5 * jnp.sum(jnp.mean(err * err, axis=-1, keepdims=True))
        else:
            out_ref[...] = hout

    act = jax.ShapeDtypeStruct((s, D_MODEL), F32)
    in_specs = [_rows(ts, MIX_WIDTH), _full((MIX_WIDTH, D_MODEL)), _rows(ts, D_MODEL),
                _full((1, D_MODEL)), _full((1, D_MODEL))]
    out_specs = [_rows(ts, D_MODEL)] * 2
    out_shape = [act, act]
    args = [y, w, h, g, b]
    if with_loss:
        in_specs.append(_rows(ts, D_MODEL))
        out_specs.append(_full((SUBLANES, LANES)))
        out_shape.append(jax.ShapeDtypeStruct((SUBLANES, LANES), F32))
        args.append(tgt)
    return pl.pallas_call(
        body, grid=(s // ts,), in_specs=in_specs, out_specs=out_specs, out_shape=out_shape,
        name=name, compiler_params=_cp(1))(*args)


def _outproj_ln_bwd(dh, pre, g, y, w_t, name, ts):
    s = y.shape[0]

    def body(dh_ref, pre_ref, g_ref, y_ref, wt_ref, dpre_ref, dy_ref, dw_ref, dgb_ref):
        @pl.when(pl.program_id(0) == 0)
        def _():
            dw_ref[...] = jnp.zeros_like(dw_ref)
            dgb_ref[...] = jnp.zeros_like(dgb_ref)

        dh_ = dh_ref[...]
        xhat, rstd = _ln_stats(pre_ref[...])
        dxh = dh_ * g_ref[...]
        dpre = rstd * (dxh - jnp.mean(dxh, axis=-1, keepdims=True)
                       - xhat * jnp.mean(dxh * xhat, axis=-1, keepdims=True))
        dpre_ref[...] = dpre
        dgb_ref[0:1, :] += jnp.sum(dh_ * xhat, axis=0, keepdims=True)
        dgb_ref[1:2, :] += jnp.sum(dh_, axis=0, keepdims=True)
        dpb = dpre.astype(BF16)
        dy_ref[...] = _dot(dpb, wt_ref[...])
        dw_ref[...] += _dot_tn(y_ref[...].astype(BF16), dpb)

    act = jax.ShapeDtypeStruct((s, D_MODEL), F32)
    return pl.pallas_call(
        body, grid=(s // ts,),
        in_specs=[_rows(ts, D_MODEL), _rows(ts, D_MODEL), _full((1, D_MODEL)),
                  _rows(ts, MIX_WIDTH), _full((D_MODEL, MIX_WIDTH))],
        out_specs=[_rows(ts, D_MODEL), _rows(ts, MIX_WIDTH), _full((MIX_WIDTH, D_MODEL)),
                   _full((SUBLANES, D_MODEL))],
        out_shape=[act, act, jax.ShapeDtypeStruct((MIX_WIDTH, D_MODEL), F32),
                   jax.ShapeDtypeStruct((SUBLANES, D_MODEL), F32)],
        name=name, compiler_params=_cp(1))(dh, pre, g, y, w_t)


def _linear_bwd(x, dys, offs, w_t, resid, name, ts):
    s, kdim = x.shape
    n = w_t.shape[0]
    widths = [d.shape[1] for d in dys]
    npieces = len(dys)

    def body(*refs):
        x_ref = refs[0]
        dy_refs = refs[1:1 + npieces]
        wt_ref, r_ref, dx_ref, dw_ref = refs[1 + npieces:]

        @pl.when(pl.program_id(0) == 0)
        def _():
            dw_ref[...] = jnp.zeros_like(dw_ref)

        xb = x_ref[...].astype(BF16)
        dx = ALPHA * r_ref[...]
        for dy_ref, off, wd in zip(dy_refs, offs, widths):
            dyb = dy_ref[...].astype(BF16)
            dx = dx + _dot(dyb, wt_ref[off:off + wd, :])
            dw_ref[:, off:off + wd] += _dot_tn(xb, dyb)
        dx_ref[...] = dx

    return pl.pallas_call(
        body, grid=(s // ts,),
        in_specs=[_rows(ts, kdim)] + [_rows(ts, wd) for wd in widths]
                 + [_full((n, kdim)), _rows(ts, kdim)],
        out_specs=[_rows(ts, kdim), _full((kdim, n))],
        out_shape=[jax.ShapeDtypeStruct((s, kdim), F32), jax.ShapeDtypeStruct((kdim, n), F32)],
        name=name, compiler_params=_cp(1))(x, *dys, w_t, resid)


def _wgrad_small(x, dy, name):
    def body(x_ref, dy_ref, dw_ref):
        dw_ref[...] = _dot_tn(x_ref[...].astype(BF16), dy_ref[...].astype(BF16))

    return pl.pallas_call(
        body, out_shape=jax.ShapeDtypeStruct((x.shape[1], dy.shape[1]), F32),
        name=name, compiler_params=pltpu.CompilerParams(vmem_limit_bytes=VMEM_LIMIT))(x, dy)


def _shift_down(u, carry8, k):
    if k == 0:
        return u
    rolled = pltpu.roll(u, k, 0)
    row = lax.broadcasted_iota(jnp.int32, carry8.shape, 0)
    top = jnp.where(row < k, pltpu.roll(carry8, k, 0), rolled[0:SUBLANES])
    return jnp.concatenate([top, rolled[SUBLANES:]], axis=0)


def _shift_up(u, carry8, k):
    if k == 0:
        return u
    n = u.shape[0]
    rolled = pltpu.roll(u, n - k, 0)
    row = lax.broadcasted_iota(jnp.int32, carry8.shape, 0)
    bot = jnp.where(row >= SUBLANES - k, pltpu.roll(carry8, SUBLANES - k, 0),
                    rolled[n - SUBLANES:])
    return jnp.concatenate([rolled[:n - SUBLANES], bot], axis=0)


def _neg_expm1(t):
    e = jnp.exp(t)
    em1 = e - 1.0
    safe = jnp.where(e == 1.0, 1.0, jnp.log(e))
    return -jnp.where(e == 1.0, t, jnp.where(em1 == -1.0, -1.0, em1 * t / safe))


def _lru_gates(u, carry8, cw_ref, vec_ref, wr_ref, wi_ref):
    xc = vec_ref[0:1, :] + cw_ref[3:4, :] * u
    for k in range(1, CONV_W):
        xc = xc + cw_ref[3 - k:4 - k, :] * _shift_down(u, carry8, k)
    xb = xc.astype(BF16)
    r = _sigmoid(_dot(xb, wr_ref[...]) + vec_ref[1:2, :])
    ig = _sigmoid(_dot(xb, wi_ref[...]) + vec_ref[2:3, :])
    nlam = -vec_ref[3:4, :]
    softplus = jnp.maximum(nlam, 0.0) + jnp.log(1.0 + jnp.exp(-jnp.abs(nlam)))
    cneg = -LRU_C * softplus
    log_a = cneg * r
    a = jnp.exp(log_a)
    sq = jnp.sqrt(_neg_expm1(2.0 * log_a))
    return xc, r, ig, cneg, a, sq


def _lru_fwd(z1, cw8, vec8, wr, wi, ts):
    s = z1.shape[0]

    def body(u_ref, cw_ref, vec_ref, wr_ref, wi_ref, hs_ref, hp_ref,
             cu_scr, ch_scr, a_scr, gx_scr):
        @pl.when(pl.program_id(0) == 0)
        def _():
            cu_scr[...] = jnp.zeros_like(cu_scr)
            ch_scr[...] = jnp.zeros_like(ch_scr)

        u = u_ref[...]
        xc, _, ig, _, a, sq = _lru_gates(u, cu_scr[...], cw_ref, vec_ref, wr_ref, wi_ref)
        a_scr[...] = a
        gx_scr[...] = sq * (ig * xc)

        def step(t, h):
            hp_ref[pl.ds(t, 1), :] = h
            h = a_scr[pl.ds(t, 1), :] * h + gx_scr[pl.ds(t, 1), :]
            hs_ref[pl.ds(t, 1), :] = h
            return h

        h = lax.fori_loop(0, ts, step, ch_scr[0:1, :], unroll=8)
        ch_scr[0:1, :] = h
        cu_scr[...] = u[ts - SUBLANES:, :]

    w = TOK_WIDTH
    out = jax.ShapeDtypeStruct((s, w), F32)
    return pl.pallas_call(
        body, grid=(s // ts,),
        in_specs=[_rows(ts, w), _full((SUBLANES, w)), _full((SUBLANES, w)),
                  _full((w, w)), _full((w, w))],
        out_specs=[_rows(ts, w)] * 2,
        out_shape=[out, out],
        scratch_shapes=[pltpu.VMEM((SUBLANES, w), F32), pltpu.VMEM((SUBLANES, w), F32),
                        pltpu.VMEM((ts, w), F32), pltpu.VMEM((ts, w), F32)],
        name="lru_fwd", compiler_params=_cp(1))(z1, cw8, vec8, wr, wi)


def _lru_bwd(z1, dhs, hprev, cw8, vec8, wr, wi, wr_t, wi_t, ts):
    s = z1.shape[0]
    nb = s // ts
    w = TOK_WIDTH
    tiles = ts // SUBLANES

    def body(u_ref, up_ref, dhs_ref, hp_ref, cw_ref, vec_ref, wr_ref, wi_ref, wrt_ref, wit_ref,
             du_ref, dwr_ref, dwi_ref, dvec_ref, cc_scr, cd_scr, a_scr, dh_scr):
        i = pl.program_id(0)

        @pl.when(i == 0)
        def _():
            cc_scr[...] = jnp.zeros_like(cc_scr)
            cd_scr[...] = jnp.zeros_like(cd_scr)
            dwr_ref[...] = jnp.zeros_like(dwr_ref)
            dwi_ref[...] = jnp.zeros_like(dwi_ref)
            dvec_ref[...] = jnp.zeros_like(dvec_ref)

        u = u_ref[...]
        carry8 = jnp.where(i == nb - 1, 0.0, up_ref[...])
        xc, r, ig, cneg, a, sq = _lru_gates(u, carry8, cw_ref, vec_ref, wr_ref, wi_ref)
        a_scr[...] = a

        def step(n, c):
            t = ts - 1 - n
            dh = dhs_ref[pl.ds(t, 1), :] + c
            dh_scr[pl.ds(t, 1), :] = dh
            return a_scr[pl.ds(t, 1), :] * dh

        cc_scr[0:1, :] = lax.fori_loop(0, ts, step, cc_scr[0:1, :], unroll=8)
        dh = dh_scr[...]
        ix = ig * xc
        dix = dh * sq
        dlog_a = dh * hp_ref[...] * a - (dh * ix) * (a * a) / sq
        dpr = (dlog_a * cneg) * r * (1.0 - r)
        dpi = (dix * xc) * ig * (1.0 - ig)
        dprb, dpib = dpr.astype(BF16), dpi.astype(BF16)
        xb = xc.astype(BF16)
        dwr_ref[...] += _dot_tn(xb, dprb)
        dwi_ref[...] += _dot_tn(xb, dpib)
        dxc = dix * ig + _dot(dprb, wrt_ref[...]) + _dot(dpib, wit_ref[...])
        for k in range(CONV_W):
            dvec_ref[3 - k:4 - k, :] += jnp.sum(dxc * _shift_down(u, carry8, k),
                                                axis=0, keepdims=True)
        dvec_ref[4:5, :] += jnp.sum(dxc, axis=0, keepdims=True)
        dvec_ref[5:6, :] += jnp.sum(dpr, axis=0, keepdims=True)
        dvec_ref[6:7, :] += jnp.sum(dpi, axis=0, keepdims=True)
        dvec_ref[7:8, :] += (jnp.sum(dlog_a * r, axis=0, keepdims=True)
                             * (LRU_C * _sigmoid(-vec_ref[3:4, :])))
        nxt = cd_scr[...]
        du = cw_ref[3:4, :] * dxc
        for k in range(1, CONV_W):
            du = du + cw_ref[3 - k:4 - k, :] * _shift_up(dxc, nxt, k)
        du_ref[...] = du
        cd_scr[...] = dxc[0:SUBLANES, :]

    rev = lambda i: (nb - 1 - i, 0)
    prev8 = lambda i: (jnp.maximum((nb - 1 - i) * tiles - 1, 0), 0)
    blk = pl.BlockSpec((ts, w), rev)
    return pl.pallas_call(
        body, grid=(nb,),
        in_specs=[blk, pl.BlockSpec((SUBLANES, w), prev8), blk, blk,
                  _full((SUBLANES, w)), _full((SUBLANES, w)),
                  _full((w, w)), _full((w, w)), _full((w, w)), _full((w, w))],
        out_specs=[blk, _full((w, w)), _full((w, w)), _full((SUBLANES, w))],
        out_shape=[jax.ShapeDtypeStruct((s, w), F32), jax.ShapeDtypeStruct((w, w), F32),
                   jax.ShapeDtypeStruct((w, w), F32), jax.ShapeDtypeStruct((SUBLANES, w), F32)],
        scratch_shapes=[pltpu.VMEM((SUBLANES, w), F32), pltpu.VMEM((SUBLANES, w), F32),
                        pltpu.VMEM((ts, w), F32), pltpu.VMEM((ts, w), F32)],
        name="lru_bwd", compiler_params=_cp(1))(
            z1, z1, dhs, hprev, cw8, vec8, wr, wi, wr_t, wi_t)


def _adamw(parts, w, m, v, name):
    n = len(parts)
    rows_per = parts[0].shape[1]

    def body(*refs):
        p_refs = refs[:n]
        w_ref, m_ref, v_ref, g_ref, d_ref, nm_ref, nv_ref = refs[n:]
        for l, p_ref in enumerate(p_refs):
            rows = slice(l * rows_per, (l + 1) * rows_per)
            g = p_ref[0]
            for dev in range(1, N_DEV):
                g = g + p_ref[dev]
            g_ref[rows, :] = g
            nm = ADAM_B1 * m_ref[rows, :] + (1.0 - ADAM_B1) * g
            nv = ADAM_B2 * v_ref[rows, :] + (1.0 - ADAM_B2) * (g * g)
            m_hat = nm / (1.0 - ADAM_B1 ** ADAM_STEP)
            v_hat = nv / (1.0 - ADAM_B2 ** ADAM_STEP)
            d_ref[rows, :] = -ADAM_LR * (m_hat / (jnp.sqrt(v_hat) + ADAM_EPS)
                                         + ADAM_WD * w_ref[rows, :])
            nm_ref[rows, :] = nm
            nv_ref[rows, :] = nv

    out = jax.ShapeDtypeStruct(w.shape, F32)
    return pl.pallas_call(
        body, out_shape=[out] * 4, name=name,
        compiler_params=pltpu.CompilerParams(vmem_limit_bytes=VMEM_LIMIT))(*parts, w, m, v)


ANY = pl.BlockSpec(memory_space=pl.ANY)
MESH = pl.DeviceIdType.MESH


def _slot(p):
    return 4 * p[0] + 2 * p[1] + p[2]


def _allgather(xs):
    n = len(xs)

    def body(*refs):
        x_refs, o_refs = refs[:n], refs[n:2 * n]
        send_sems, recv_sems, local_sems = refs[2 * n:]
        x, y, c = lax.axis_index("x"), lax.axis_index("y"), lax.axis_index("c")
        me, sibling = (x, y, c), (x, y, 1 - c)
        chips = [(1 - x, y), (x, 1 - y), (1 - x, 1 - y)]

        def copy(a, k, block, to, from_input=False):
            dst = o_refs[a].at[_slot(block)]
            return pltpu.make_async_remote_copy(
                src_ref=x_refs[a] if from_input else dst, dst_ref=dst,
                send_sem=send_sems.at[a, k], recv_sem=recv_sems.at[a, k],
                device_id=to, device_id_type=MESH)

        mine = [pltpu.make_async_copy(x_refs[a], o_refs[a].at[_slot(me)], local_sems.at[a])
                for a in range(n)]
        for cp in mine:
            cp.start()
        first = []
        for a in range(n):
            first.append(copy(a, 0, me, sibling, True))
            first += [copy(a, 1 + j, me, (*chip, c), True) for j, chip in enumerate(chips)]
        for cp in first:
            cp.start()
        passed = []
        for j, chip in enumerate(chips):
            for a in range(n):
                copy(a, 1 + j, (*chip, c), me).wait_recv()
                cp = copy(a, 4 + j, (*chip, c), sibling)
                cp.start()
                passed.append(cp)
        for a in range(n):
            copy(a, 0, sibling, me).wait_recv()
            for j, chip in enumerate(chips):
                copy(a, 4 + j, (*chip, 1 - c), me).wait_recv()
        for cp in first + passed:
            cp.wait_send()
        for cp in mine:
            cp.wait()

    return pl.pallas_call(
        body,
        out_shape=[jax.ShapeDtypeStruct((N_DEV,) + t.shape, t.dtype) for t in xs],
        in_specs=[ANY] * n, out_specs=[ANY] * n,
        scratch_shapes=[pltpu.SemaphoreType.DMA((n, 7)), pltpu.SemaphoreType.DMA((n, 7)),
                        pltpu.SemaphoreType.DMA((n,))],
        name="allgather_weights")(*xs)


class _Exchange:
    def __init__(self, arrays, kinds):
        self.arrays, self.kinds, self.n = list(arrays), list(kinds), len(arrays)
        self.shapes = [self._part_shape(a, k) for a, k in zip(arrays, kinds)]
        self.out_shape = [jax.ShapeDtypeStruct((N_DEV,) + shp, F32) for shp in self.shapes]
        self.scratch = [pltpu.SemaphoreType.DMA((self.n, N_DEV - 1)),
                        pltpu.SemaphoreType.DMA((self.n, N_DEV - 1)),
                        pltpu.SemaphoreType.DMA((self.n,))]

    @staticmethod
    def _part_shape(arr, kind):
        if kind == "chunks":
            return arr.shape[1:]
        if kind == "cols":
            return (arr.shape[0], arr.shape[1] // N_DEV)
        if kind == "rows":
            return (arr.shape[0] // N_DEV, arr.shape[1])
        return arr.shape

    def copies(self, in_refs, out_refs, sems):
        send_sems, recv_sems, local_sems = sems
        x, y, c = lax.axis_index("x"), lax.axis_index("y"), lax.axis_index("c")
        me = _slot((x, y, c))

        def part(a, dev):
            ref, kind, shp = in_refs[a], self.kinds[a], self.shapes[a]
            if kind == "chunks":
                return ref.at[dev]
            if kind == "cols":
                return ref.at[:, pl.ds(pl.multiple_of(dev * shp[1], LANES), shp[1])]
            if kind == "rows":
                return ref.at[pl.ds(pl.multiple_of(dev * shp[0], SUBLANES), shp[0]), :]
            return ref

        cps = [pltpu.make_async_copy(part(a, me), out_refs[a].at[me], local_sems.at[a])
               for a in range(self.n)]
        for rel in range(1, N_DEV):
            peer = (x ^ (rel >> 2), y ^ ((rel >> 1) & 1), c ^ (rel & 1))
            for a in range(self.n):
                cps.append(pltpu.make_async_remote_copy(
                    src_ref=part(a, _slot(peer)), dst_ref=out_refs[a].at[me],
                    send_sem=send_sems.at[a, rel - 1], recv_sem=recv_sems.at[a, rel - 1],
                    device_id=peer, device_id_type=MESH))
        return cps


def _exchange_grads(arrays, kinds, name):
    ex = _Exchange(arrays, kinds)
    n = ex.n

    def body(*refs):
        cps = ex.copies(refs[:n], refs[n:2 * n], refs[2 * n:])
        for cp in cps:
            cp.start()
        for cp in cps:
            cp.wait()

    return pl.pallas_call(
        body, out_shape=ex.out_shape, in_specs=[ANY] * n, out_specs=[ANY] * n,
        scratch_shapes=ex.scratch, name=name)(*arrays)


BIG = [("mla_w_in", (D_MODEL, MLA_IN), 1), ("mla_w_uq", (Q_LORA, N_TOK_HEADS * QK_DIM), 1),
       ("mla_w_ukv", (KV_LORA, N_TOK_HEADS * 2 * HEAD_DIM), 1), ("lru_w_in", (D_MODEL, LRU_IN), 1),
       ("w_mem_kv", (2, D_MODEL, 2 * MEM_WIDTH), 1), ("w_out", (2, MIX_WIDTH, D_MODEL), 1)]
SMALL = [("lru_conv_w", (CONV_W, TOK_WIDTH), 1), ("lru_conv_b", (TOK_WIDTH,), 0),
         ("lru_b_rgate", (TOK_WIDTH,), 0), ("lru_b_igate", (TOK_WIDTH,), 0),
         ("lru_lambda", (TOK_WIDTH,), 0)]
REPL = [("mla_q_norm", (Q_LORA,)), ("mla_kv_norm", (KV_LORA,)),
        ("lru_w_rgate", (N_TOK_HEADS, HEAD_DIM, HEAD_DIM)),
        ("lru_w_igate", (N_TOK_HEADS, HEAD_DIM, HEAD_DIM)),
        ("ln_g", (2, D_MODEL)), ("ln_b", (2, D_MODEL))]


def _shard_shape(shape, axis):
    return tuple(d // N_DEV if a == axis else d for a, d in enumerate(shape))


def _size(shape):
    return math.prod(shape)


BIG_ROWS = sum(_size(s) for _, s, _ in BIG) // N_DEV // LANES
SMALL_ROWS = SUBLANES


def _pack_rows(flat_parts, rows):
    flat = jnp.concatenate([p.reshape(-1) for p in flat_parts])
    return jnp.pad(flat, (0, rows * LANES - flat.shape[0])).reshape(rows, LANES)


def _to_chunks(full, axis):
    shape = full.shape
    split = shape[:axis] + (N_DEV, shape[axis] // N_DEV) + shape[axis + 1:]
    return jnp.moveaxis(full.reshape(split), axis, 0).reshape(N_DEV, -1)


def _from_chunks(chunks, shape, axis):
    sh = _shard_shape(shape, axis)
    t = chunks.reshape((N_DEV,) + sh)
    t = jnp.moveaxis(t, 0, axis)
    return t.reshape(shape)


def _split_flat(flat2d, table):
    out, off = [], 0
    for size in table:
        out.append(flat2d[:, off:off + size])
        off += size
    return out


def _win0_to_padded(w):
    z = lambda n: jnp.zeros((w.shape[0], n), w.dtype)
    return jnp.concatenate([w[:, 0:640], w[:, 672:1952], z(KR_LANE), w[:, 640:672],
                            z(LANES - KR_LANE - QK_ROPE)], axis=1)


def _win0_from_padded(wp):
    k0 = Z0_KR + KR_LANE
    return jnp.concatenate([wp[:, 0:640], wp[:, k0:k0 + QK_ROPE], wp[:, 640:1920]], axis=1)


def _pad_heads(w, per_head, lo, hi):
    t = w.reshape(w.shape[0], N_TOK_HEADS, per_head)[:, :, lo:hi]
    t = jnp.pad(t, ((0, 0), (0, 0), (0, HEAD_PAD - (hi - lo))))
    return t.reshape(w.shape[0], QKV_PAD)


def _unpad_heads(wp, width):
    return wp.reshape(wp.shape[0], N_TOK_HEADS, HEAD_PAD)[:, :, :width]


def _block_diag(w):
    eye = jnp.eye(N_TOK_HEADS, dtype=w.dtype)
    return (w[:, :, None, :] * eye[:, None, :, None]).reshape(TOK_WIDTH, TOK_WIDTH)


def _diag_blocks(d):
    t = d.reshape(N_TOK_HEADS, HEAD_DIM, N_TOK_HEADS, HEAD_DIM)
    return jnp.stack([t[g, :, g, :] for g in range(N_TOK_HEADS)])


def _rope_tables(positions):
    half = QK_ROPE // 2
    inv_freq = ROPE_THETA ** (-jnp.arange(half, dtype=F32) / half)
    ang = positions.astype(F32)[:, None] * inv_freq
    cos, sin = jnp.cos(ang), jnp.sin(ang)
    s = positions.shape[0]
    one, zero = jnp.ones((s, QK_NOPE), F32), jnp.zeros((s, half), F32)
    tail = jnp.zeros((s, HEAD_PAD - QK_DIM), F32)
    znope = jnp.zeros((s, QK_NOPE), F32)
    c = jnp.concatenate([one, cos, cos, tail], axis=1)
    sa = jnp.concatenate([znope, -sin, zero, tail], axis=1)
    sb = jnp.concatenate([znope, zero, sin, tail], axis=1)
    return c, sa, sb


def _local_step(x, mem, positions, tgt, wts, ts, tatt, early_exchange):
    bf = lambda t: t.astype(BF16)
    win0 = _win0_to_padded(wts["mla_w_in"])
    wuq = _pad_heads(wts["mla_w_uq"], QK_DIM, 0, QK_DIM)
    wukv = jnp.concatenate([_pad_heads(wts["mla_w_ukv"], 2 * HEAD_DIM, 0, QK_NOPE),
                            _pad_heads(wts["mla_w_ukv"], 2 * HEAD_DIM, QK_NOPE, 2 * HEAD_DIM)],
                           axis=1)
    win1 = wts["lru_w_in"]
    wmkv, wout = wts["w_mem_kv"], wts["w_out"]
    gq = wts["mla_q_norm"].reshape(1, Q_LORA)
    gkv = wts["mla_kv_norm"].reshape(1, KV_LORA)
    ln_g, ln_b = wts["ln_g"], wts["ln_b"]
    wr, wi = bf(_block_diag(wts["lru_w_rgate"])), bf(_block_diag(wts["lru_w_igate"]))
    cw8 = jnp.pad(wts["lru_conv_w"], ((0, SUBLANES - CONV_W), (0, 0)))
    vec8 = jnp.pad(jnp.stack([wts["lru_conv_b"], wts["lru_b_rgate"], wts["lru_b_igate"],
                              wts["lru_lambda"]]), ((0, SUBLANES - 4), (0, 0)))
    tabs = _rope_tables(positions)
    tmem = mem.shape[0]

    z0 = _rowmm(x, win0, "in_proj0", ts)
    q, k, v = _mla_prep_fwd(z0, tabs, gq, gkv, wuq, wukv, ts)
    o, lse = _flash_fwd(q, k, v, tatt, FWD_HEADS)
    mkv0 = _rowmm(mem, wmkv[0], "mem_kv0", tmem)
    cat0, y0 = _gate_mem_fwd(o, z0, mkv0, Z0_GATE, Z0_QMEM, True, "gate_mem_fwd0", ts)
    del o
    pre0, h1 = _outproj_ln_fwd(y0, wout[0], x, ln_g[0:1], ln_b[0:1], None, "outproj_ln_fwd0", ts)
    z1 = _rowmm(h1, win1, "in_proj1", ts)
    hs, hprev = _lru_fwd(z1, cw8, vec8, wr, wi, ts)
    mkv1 = _rowmm(mem, wmkv[1], "mem_kv1", tmem)
    cat1, y1 = _gate_mem_fwd(hs, z1, mkv1, Z1_GATE, Z1_QMEM, False, "gate_mem_fwd1", ts)
    pre1, dh2, loss8 = _outproj_ln_fwd(y1, wout[1], h1, ln_g[1:2], ln_b[1:2], tgt,
                                       "outproj_ln_loss", ts)
    loss = loss8[0, 0]

    dpre1, dy1, dwout1, dgb1 = _outproj_ln_bwd(dh2, pre1, ln_g[1:2], y1, wout[1].T,
                                               "outproj_ln_bwd1", ts)
    dzg1, dhs, dmkv1 = _gate_mem_bwd(dy1, cat1, z1, mkv1, None, Z1_GATE, Z1_QMEM,
                                     "gate_mem_bwd1", ts)
    du, dwr, dwi, dvec = _lru_bwd(z1, dhs, hprev, cw8, vec8, wr, wi, wr.T, wi.T, ts)
    dh1, dwin1 = _linear_bwd(h1, [du, dzg1], [Z1_U, Z1_GATE], win1.T, dpre1, "in_proj_bwd1", ts)
    dwmkv1 = _wgrad_small(mem, dmkv1, "mem_kv_bwd1")
    dpre0, dy0, dwout0, dgb0 = _outproj_ln_bwd(dh1, pre0, ln_g[0:1], y0, wout[0].T,
                                               "outproj_ln_bwd0", ts)
    dzg0, do, dmkv0, stats = _gate_mem_bwd(dy0, cat0, z0, mkv0, lse, Z0_GATE, Z0_QMEM,
                                           "gate_mem_bwd0", ts)
    dwmkv0 = _wgrad_small(mem, dmkv0, "mem_kv_bwd0")
    early = {
        "lru_w_in": dwin1,
        "lru_small": dvec,
        "lru_w_rgate": _diag_blocks(dwr).reshape(TOK_WIDTH, HEAD_DIM),
        "lru_w_igate": _diag_blocks(dwi).reshape(TOK_WIDTH, HEAD_DIM),
        "w_mem_kv": [dwmkv0, dwmkv1],
        "w_out": [dwout0, dwout1],
    }
    (dq, dk, dv), got_early = _flash_bwd(q, k, v, stats, do, tatt, BWD_HEADS,
                                         early_exchange(early))
    dza, dzk, dwuq_p, dwukv_p, dg = _mla_prep_bwd(z0, dq, dk, dv, tabs, gq, gkv,
                                                  wuq.T, wukv.T, ts)
    gx, dwin0_p = _linear_bwd(x, [dza, dzg0, dzk], [Z0_CQ, Z0_GATE, Z0_KR], win0.T, dpre0,
                              "in_proj_bwd0", ts)

    dwukv = jnp.concatenate([_unpad_heads(dwukv_p[:, :QKV_PAD], HEAD_DIM),
                             _unpad_heads(dwukv_p[:, QKV_PAD:], HEAD_DIM)], axis=2)
    zrow = jnp.zeros((1, D_MODEL), F32)
    gains = jnp.pad(dg[0:1], ((0, 0), (0, D_MODEL - Q_LORA - KV_LORA)))
    small_repl = jnp.concatenate([dgb0[0:2], dgb1[0:2], gains,
                                  loss * jnp.ones((1, D_MODEL), F32), zrow, zrow], axis=0)
    late = {
        "mla_w_in": _win0_from_padded(dwin0_p),
        "mla_w_uq": _unpad_heads(dwuq_p, QK_DIM).reshape(Q_LORA, N_TOK_HEADS * QK_DIM),
        "mla_w_ukv": dwukv.reshape(KV_LORA, N_TOK_HEADS * 2 * HEAD_DIM),
        "small_repl": small_repl,
    }
    return gx, early, got_early, late


WEIGHT_ORDER = ["mla_w_in", "mla_q_norm", "mla_w_uq", "mla_kv_norm", "mla_w_ukv", "lru_w_in",
                "lru_conv_w", "lru_conv_b", "lru_w_rgate", "lru_b_rgate", "lru_w_igate",
                "lru_b_igate", "lru_lambda", "w_mem_kv", "w_out", "ln_g", "ln_b"]


def kernel(x, mem, positions, mla_w_in, mla_q_norm, mla_w_uq, mla_kv_norm, mla_w_ukv, lru_w_in, lru_conv_w, lru_conv_b, lru_w_rgate, lru_b_rgate, lru_w_igate, lru_b_igate, lru_lambda, w_mem_kv, w_out, ln_g, ln_b, loss_target, m_mla_w_in, m_mla_q_norm, m_mla_w_uq, m_mla_kv_norm, m_mla_w_ukv, m_lru_w_in, m_lru_conv_w, m_lru_conv_b, m_lru_w_rgate, m_lru_b_rgate, m_lru_w_igate, m_lru_b_igate, m_lru_lambda, m_w_mem_kv, m_w_out, m_ln_g, m_ln_b, v_mla_w_in, v_mla_q_norm, v_mla_w_uq, v_mla_kv_norm, v_mla_w_ukv, v_lru_w_in, v_lru_conv_w, v_lru_conv_b, v_lru_w_rgate, v_lru_b_rgate, v_lru_w_igate, v_lru_b_igate, v_lru_lambda, v_w_mem_kv, v_w_out, v_ln_g, v_ln_b):
    w_in = dict(mla_w_in=mla_w_in, mla_q_norm=mla_q_norm, mla_w_uq=mla_w_uq,
                mla_kv_norm=mla_kv_norm, mla_w_ukv=mla_w_ukv, lru_w_in=lru_w_in,
                lru_conv_w=lru_conv_w, lru_conv_b=lru_conv_b, lru_w_rgate=lru_w_rgate,
                lru_b_rgate=lru_b_rgate, lru_w_igate=lru_w_igate, lru_b_igate=lru_b_igate,
                lru_lambda=lru_lambda, w_mem_kv=w_mem_kv, w_out=w_out, ln_g=ln_g, ln_b=ln_b)
    m_in = dict(mla_w_in=m_mla_w_in, mla_q_norm=m_mla_q_norm, mla_w_uq=m_mla_w_uq,
                mla_kv_norm=m_mla_kv_norm, mla_w_ukv=m_mla_w_ukv, lru_w_in=m_lru_w_in,
                lru_conv_w=m_lru_conv_w, lru_conv_b=m_lru_conv_b, lru_w_rgate=m_lru_w_rgate,
                lru_b_rgate=m_lru_b_rgate, lru_w_igate=m_lru_w_igate, lru_b_igate=m_lru_b_igate,
                lru_lambda=m_lru_lambda, w_mem_kv=m_w_mem_kv, w_out=m_w_out, ln_g=m_ln_g,
                ln_b=m_ln_b)
    v_in = dict(mla_w_in=v_mla_w_in, mla_q_norm=v_mla_q_norm, mla_w_uq=v_mla_w_uq,
                mla_kv_norm=v_mla_kv_norm, mla_w_ukv=v_mla_w_ukv, lru_w_in=v_lru_w_in,
                lru_conv_w=v_lru_conv_w, lru_conv_b=v_lru_conv_b, lru_w_rgate=v_lru_w_rgate,
                lru_b_rgate=v_lru_b_rgate, lru_w_igate=v_lru_w_igate, lru_b_igate=v_lru_b_igate,
                lru_lambda=v_lru_lambda, w_mem_kv=v_w_mem_kv, w_out=v_w_out, ln_g=v_ln_g,
                ln_b=v_ln_b)
    s = x.shape[1]
    ts = min(ROW_BLOCK, s)
    tatt = min(ATT_BLOCK, s)
    big_sizes = [_size(sh) // N_DEV for _, sh, _ in BIG]
    small_sizes = [_size(sh) // N_DEV for _, sh, _ in SMALL]

    big_local = _pack_rows([w_in[n] for n, _, _ in BIG], BIG_ROWS).astype(BF16)
    small_local = _pack_rows([w_in[n] for n, _, _ in SMALL], SMALL_ROWS)
    big_all, small_all = _allgather([big_local, small_local])
    wts = {}
    for (n, sh, ax), part in zip(BIG, _split_flat(big_all.reshape(N_DEV, -1), big_sizes)):
        wts[n] = _from_chunks(part, sh, ax)
    for (n, sh, ax), part in zip(SMALL, _split_flat(small_all.reshape(N_DEV, -1), small_sizes)):
        wts[n] = _from_chunks(part, sh, ax)
    for n, sh in REPL:
        wts[n] = w_in[n].reshape(sh)

    def early_exchange(g):
        small_chunks = jnp.moveaxis(g["lru_small"].reshape(SUBLANES, N_DEV, -1), 1, 0)
        sends = [(g["lru_w_in"], "cols"),
                 (g["w_mem_kv"][0], "rows"), (g["w_mem_kv"][1], "rows"),
                 (g["w_out"][0], "rows"), (g["w_out"][1], "rows"),
                 (small_chunks, "chunks"), (g["lru_w_rgate"], "all"), (g["lru_w_igate"], "all")]
        return _Exchange([a for a, _ in sends], [k for _, k in sends])

    gx, _, got_early, late = _local_step(x[0], mem[0], positions[0], loss_target[0], wts,
                                         ts, tatt, early_exchange)

    def chunked(name, shape):
        w = shape[1] // N_DEV
        return _to_chunks(late[name], 1).reshape(N_DEV, shape[0], w)

    got_late = _exchange_grads(
        [chunked("mla_w_in", (D_MODEL, MLA_IN)),
         chunked("mla_w_uq", (Q_LORA, N_TOK_HEADS * QK_DIM)),
         chunked("mla_w_ukv", (KV_LORA, N_TOK_HEADS * 2 * HEAD_DIM)), late["small_repl"]],
        ["chunks", "chunks", "chunks", "all"], "exchange_grads")
    got = list(got_late[:3]) + list(got_early) + [got_late[3]]

    def small_sharded(d):
        return jnp.concatenate([d["lru_conv_w"].reshape(CONV_W, -1), d["lru_conv_b"],
                                d["lru_b_rgate"], d["lru_b_igate"], d["lru_lambda"]], axis=0)

    def small_replicated(d):
        gains = jnp.concatenate([d["mla_q_norm"], d["mla_kv_norm"]], axis=1)
        gains = jnp.pad(gains, ((0, 0), (0, D_MODEL - gains.shape[1])))
        return jnp.concatenate([d["ln_g"][0:1], d["ln_b"][0:1], d["ln_g"][1:2], d["ln_b"][1:2],
                                gains, jnp.zeros((3, D_MODEL), F32)], axis=0)

    def flat2(d, name):
        t = d[name]
        return t.reshape(-1, t.shape[-1])

    def update(parts, view, name):
        return _adamw(parts, view(w_in), view(m_in), view(v_in), "adamw_" + name)

    res = {}
    for idx, name in [(0, "mla_w_in"), (1, "mla_w_uq"), (2, "mla_w_ukv"), (3, "lru_w_in"),
                      (9, "lru_w_rgate"), (10, "lru_w_igate")]:
        res[name] = update([got[idx]], functools.partial(flat2, name=name), name)
    res["w_mem_kv"] = update([got[4], got[5]], functools.partial(flat2, name="w_mem_kv"),
                             "w_mem_kv")
    res["w_out"] = update([got[6], got[7]], functools.partial(flat2, name="w_out"), "w_out")
    res_ss = update([got[8]], small_sharded, "small_sharded")
    res_sr = update([got[11]], small_replicated, "small_replicated")
    loss = res_sr[0][5, 0]

    result = [loss, gx.reshape(x.shape)]
    for kind in range(4):
        ss, sr = res_ss[kind], res_sr[kind]
        out = {n: res[n][kind].reshape(w_in[n].shape) for n in res}
        out["lru_conv_w"] = ss[0:CONV_W].reshape(w_in["lru_conv_w"].shape)
        out["lru_conv_b"], out["lru_b_rgate"] = ss[4:5], ss[5:6]
        out["lru_b_igate"], out["lru_lambda"] = ss[6:7], ss[7:8]
        out["ln_g"] = jnp.concatenate([sr[0:1], sr[2:3]], axis=0)
        out["ln_b"] = jnp.concatenate([sr[1:2], sr[3:4]], axis=0)
        out["mla_q_norm"] = sr[4:5, 0:Q_LORA]
        out["mla_kv_norm"] = sr[4:5, Q_LORA:Q_LORA + KV_LORA]
        result += [out[n] for n in WEIGHT_ORDER]
    return tuple(result)
```

```python
import functools
import math

import jax
import jax.numpy as jnp
from jax import lax
from jax.experimental import pallas as pl
from jax.experimental.pallas import tpu as pltpu

F32 = jnp.float32
BF16 = jnp.bfloat16

D_MODEL = 1024
MEM_LEN = 256
HEAD_DIM = 64
N_TOK_HEADS = 12
N_MEM_HEADS = 4
TOK_WIDTH = 768
MEM_WIDTH = 256
MIX_WIDTH = 1024
Q_LORA = 384
KV_LORA = 256
QK_NOPE = 64
QK_ROPE = 32
QK_DIM = 96
ROPE_THETA = 10000.0
CONV_W = 4
LRU_C = 8.0
ALPHA = (2.0 * 2) ** 0.25
NORM_EPS = 1e-6
MLA_IN = 1952
LRU_IN = 2048
ADAM_LR = 0.001
ADAM_B1 = 0.9
ADAM_B2 = 0.999
ADAM_EPS = 1e-08
ADAM_WD = 0.01
ADAM_STEP = 10

N_DEV = 8
LANES = 128
SUBLANES = 8
HEAD_PAD = 128
QKV_PAD = N_TOK_HEADS * HEAD_PAD
ZP = 2048
Z0_CQ, Z0_CKV, Z0_GATE, Z0_QMEM, Z0_KR = 0, 384, 640, 1664, 1920
KR_LANE = 64
Z1_U, Z1_GATE, Z1_QMEM = 0, 768, 1792

ROW_BLOCK = 512
ATT_BLOCK = 512
LOOKAHEAD = 3
FWD_HEADS = 12
BWD_HEADS = 2
VMEM_LIMIT = 56 * 1024 * 1024
NEG_BIG = -1e30
STRIP = 32
LOG2E = math.log2(math.e)


def _cp(n_axes):
    return pltpu.CompilerParams(dimension_semantics=("arbitrary",) * n_axes,
                                vmem_limit_bytes=VMEM_LIMIT)


def _dot(a, b):
    return jnp.dot(a, b, preferred_element_type=F32)


def _dot_nt(a, b):
    return lax.dot_general(a, b, (((1,), (1,)), ((), ())), preferred_element_type=F32)


def _dot_tn(a, b):
    return lax.dot_general(a, b, (((0,), (0,)), ((), ())), preferred_element_type=F32)


def _sigmoid(t):
    return 1.0 / (1.0 + jnp.exp(-t))


def _lane(shape):
    return lax.broadcasted_iota(jnp.int32, shape, len(shape) - 1)


def _full(shape):
    nd = len(shape)
    return pl.BlockSpec(shape, lambda *_: (0,) * nd)


def _rows(ts, width, col=0):
    return pl.BlockSpec((ts, width), lambda i: (i, col))


def _heads(ts):
    return pl.BlockSpec((N_TOK_HEADS, ts, HEAD_PAD), lambda i: (0, i, 0))


def _rowmm(x, w, name, ts):
    s, k = x.shape
    n = w.shape[1]

    def body(x_ref, w_ref, o_ref):
        o_ref[...] = _dot(x_ref[...].astype(BF16), w_ref[...])

    return pl.pallas_call(
        body, grid=(s // ts,),
        in_specs=[_rows(ts, k), _full((k, n))],
        out_specs=_rows(ts, n),
        out_shape=jax.ShapeDtypeStruct((s, n), F32),
        name=name, compiler_params=_cp(1))(x, w)


def _rms_parts(t):
    rs = lax.rsqrt(jnp.mean(t * t, axis=-1, keepdims=True) + NORM_EPS)
    return rs


def _rope(t, c, sa, sb):
    return t * c + pltpu.roll(t, LANES - 16, 1) * sa + pltpu.roll(t, 16, 1) * sb


def _rope_t(d, c, sa, sb):
    return d * c + pltpu.roll(d * sa, 16, 1) + pltpu.roll(d * sb, LANES - 16, 1)


def _mla_prep_fwd(z0, tabs, gq, gkv, wuq, wukv, ts):
    s = z0.shape[0]

    def body(z_ref, c_ref, sa_ref, sb_ref, gq_ref, gkv_ref, wuq_ref, wukv_ref,
             q_ref, k_ref, v_ref):
        cq = z_ref[:, Z0_CQ:Z0_CQ + Q_LORA]
        ckv = z_ref[:, Z0_CKV:Z0_CKV + KV_LORA]
        kr = z_ref[:, Z0_KR:Z0_KR + LANES]
        cqn = cq * _rms_parts(cq) * gq_ref[...]
        ckvn = ckv * _rms_parts(ckv) * gkv_ref[...]
        q = _dot(cqn.astype(BF16), wuq_ref[...])
        kv = _dot(ckvn.astype(BF16), wukv_ref[...])
        c, sa, sb = c_ref[...], sa_ref[...], sb_ref[...]
        krope = _rope(kr, c, sa, sb)
        pad_lane = _lane((ts, HEAD_PAD)) >= HEAD_DIM
        for h in range(N_TOK_HEADS):
            sl = slice(h * HEAD_PAD, (h + 1) * HEAD_PAD)
            q_ref[h] = _rope(q[:, sl], c, sa, sb).astype(BF16)
            k_ref[h] = (kv[:, sl] + krope).astype(BF16)
            vh = kv[:, QKV_PAD + h * HEAD_PAD:QKV_PAD + (h + 1) * HEAD_PAD]
            v_ref[h] = jnp.where(pad_lane, 1.0, vh).astype(BF16)

    out = jax.ShapeDtypeStruct((N_TOK_HEADS, s, HEAD_PAD), BF16)
    return pl.pallas_call(
        body, grid=(s // ts,),
        in_specs=[_rows(ts, ZP), _rows(ts, LANES), _rows(ts, LANES), _rows(ts, LANES),
                  _full((1, Q_LORA)), _full((1, KV_LORA)),
                  _full((Q_LORA, QKV_PAD)), _full((KV_LORA, 2 * QKV_PAD))],
        out_specs=[_heads(ts)] * 3,
        out_shape=[out, out, out],
        name="mla_prep_fwd", compiler_params=_cp(1))(z0, *tabs, gq, gkv, wuq, wukv)


def _mla_prep_bwd(z0, dq, dk, dv, tabs, gq, gkv, wuq_t, wukv_t, ts):
    s = z0.shape[0]

    def body(z_ref, dq_ref, dk_ref, dv_ref, c_ref, sa_ref, sb_ref, gq_ref, gkv_ref,
             wuqt_ref, wukvt_ref, dza_ref, dzk_ref, dwuq_ref, dwukv_ref, dg_ref):
        @pl.when(pl.program_id(0) == 0)
        def _():
            dwuq_ref[...] = jnp.zeros_like(dwuq_ref)
            dwukv_ref[...] = jnp.zeros_like(dwukv_ref)
            dg_ref[...] = jnp.zeros_like(dg_ref)

        cq = z_ref[:, Z0_CQ:Z0_CQ + Q_LORA]
        ckv = z_ref[:, Z0_CKV:Z0_CKV + KV_LORA]
        rq, rkv = _rms_parts(cq), _rms_parts(ckv)
        gq_, gkv_ = gq_ref[...], gkv_ref[...]
        cqn = (cq * rq * gq_).astype(BF16)
        ckvn = (ckv * rkv * gkv_).astype(BF16)
        c, sa, sb = c_ref[...], sa_ref[...], sb_ref[...]
        dqp, dksum = [], None
        for h in range(N_TOK_HEADS):
            dqp.append(_rope_t(dq_ref[h], c, sa, sb))
            dksum = dk_ref[h] if dksum is None else dksum + dk_ref[h]
        dqp = jnp.concatenate(dqp, axis=1).astype(BF16)
        lane = _lane(dksum.shape)
        dzk_ref[...] = jnp.where((lane >= KR_LANE) & (lane < KR_LANE + QK_ROPE),
                                 _rope_t(dksum, c, sa, sb), 0.0)
        dkv = jnp.concatenate([dk_ref[h] for h in range(N_TOK_HEADS)]
                              + [dv_ref[h] for h in range(N_TOK_HEADS)], axis=1).astype(BF16)
        dcqn = _dot(dqp, wuqt_ref[...])
        dckvn = _dot(dkv, wukvt_ref[...])
        dwuq_ref[...] += _dot_tn(cqn, dqp)
        dwukv_ref[...] += _dot_tn(ckvn, dkv)
        dg_ref[0:1, 0:Q_LORA] += jnp.sum(dcqn * cq * rq, axis=0, keepdims=True)
        dg_ref[0:1, Q_LORA:Q_LORA + KV_LORA] += jnp.sum(dckvn * ckv * rkv, axis=0, keepdims=True)
        wq = dcqn * gq_
        wkv = dckvn * gkv_
        dcq = rq * wq - cq * (rq * rq * rq) * jnp.mean(wq * cq, axis=-1, keepdims=True)
        dckv = rkv * wkv - ckv * (rkv * rkv * rkv) * jnp.mean(wkv * ckv, axis=-1, keepdims=True)
        dza_ref[:, 0:Q_LORA] = dcq
        dza_ref[:, Q_LORA:Q_LORA + KV_LORA] = dckv

    na = Q_LORA + KV_LORA
    return pl.pallas_call(
        body, grid=(s // ts,),
        in_specs=[_rows(ts, ZP), _heads(ts), _heads(ts), _heads(ts),
                  _rows(ts, LANES), _rows(ts, LANES), _rows(ts, LANES),
                  _full((1, Q_LORA)), _full((1, KV_LORA)),
                  _full((QKV_PAD, Q_LORA)), _full((2 * QKV_PAD, KV_LORA))],
        out_specs=[_rows(ts, na), _rows(ts, LANES), _full((Q_LORA, QKV_PAD)),
                   _full((KV_LORA, 2 * QKV_PAD)), _full((SUBLANES, na))],
        out_shape=[jax.ShapeDtypeStruct((s, na), F32), jax.ShapeDtypeStruct((s, LANES), F32),
                   jax.ShapeDtypeStruct((Q_LORA, QKV_PAD), F32),
                   jax.ShapeDtypeStruct((KV_LORA, 2 * QKV_PAD), F32),
                   jax.ShapeDtypeStruct((SUBLANES, na), F32)],
        name="mla_prep_bwd", compiler_params=_cp(1))(
            z0, dq, dk, dv, *tabs, gq, gkv, wuq_t, wukv_t)


def _causal_pairs(nb, by_key):
    if by_key:
        pairs = [(i, j) for j in range(nb) for i in range(j, nb)]
    else:
        pairs = [(i, j) for i in range(nb) for j in range(i + 1)]
    return (jnp.array([p[0] for p in pairs], jnp.int32),
            jnp.array([p[1] for p in pairs], jnp.int32))


def _flash_fwd(q, k, v, t, nh):
    s = q.shape[1]
    itab, jtab = _causal_pairs(s // t, False)
    c2 = LOG2E / math.sqrt(QK_DIM)
    nch = t // LANES

    def body(it_ref, jt_ref, q_ref, k_ref, v_ref, o_ref, lse_ref, m_scr, acc_scr):
        pair = pl.program_id(1)
        i, j = it_ref[pair], jt_ref[pair]

        @pl.when(j == 0)
        def _():
            m_scr[...] = jnp.full_like(m_scr, NEG_BIG)
            acc_scr[...] = jnp.zeros_like(acc_scr)

        def softmax_strips(masked, hs, sc):
            ps, als = [], []
            for r0 in range(0, t, STRIP):
                rows = slice(r0, r0 + STRIP)
                ch = [sc[rows, n * LANES:(n + 1) * LANES] * c2 for n in range(nch)]
                if masked:
                    rr = r0 + lax.broadcasted_iota(jnp.int32, (STRIP, LANES), 0)
                    cc = lax.broadcasted_iota(jnp.int32, (STRIP, LANES), 1)
                    ch = [jnp.where(cc + n * LANES <= rr, ch[n], NEG_BIG) for n in range(nch)]
                mx = ch[0]
                for n in range(1, nch):
                    mx = jnp.maximum(mx, ch[n])
                m_prev = m_scr[hs, rows, :]
                m_next = jnp.maximum(m_prev, jnp.max(mx, axis=-1, keepdims=True))
                ps.append(jnp.concatenate(
                    [jnp.exp2(ch[n] - m_next).astype(BF16) for n in range(nch)], axis=1))
                als.append(jnp.exp2(m_prev - m_next))
                m_scr[hs, rows, :] = m_next
            return jnp.concatenate(ps, axis=0), jnp.concatenate(als, axis=0)

        def step(masked):
            ahead = min(LOOKAHEAD, nh)
            scores = [_dot_nt(q_ref[hs], k_ref[hs]) for hs in range(ahead)]
            for hs in range(nh):
                if hs + ahead < nh:
                    scores.append(_dot_nt(q_ref[hs + ahead], k_ref[hs + ahead]))
                sc = scores[hs]
                p, alpha = softmax_strips(masked, hs, sc)
                acc_scr[hs] = alpha * acc_scr[hs] + _dot(p, v_ref[hs])

        @pl.when(j < i)
        def _():
            step(False)

        @pl.when(j == i)
        def _():
            step(True)
            for h in range(nh):
                acc = acc_scr[h]
                l = acc[:, HEAD_DIM:HEAD_DIM + 1]
                o_ref[h] = jnp.where(_lane(acc.shape) < HEAD_DIM, acc / l, 0.0)
                lse_ref[h] = m_scr[h] + jnp.log2(l)

    qspec = pl.BlockSpec((nh, t, HEAD_PAD), lambda h, p, it, jt: (h, it[p], 0))
    kspec = pl.BlockSpec((nh, t, HEAD_PAD), lambda h, p, it, jt: (h, jt[p], 0))
    out = jax.ShapeDtypeStruct((N_TOK_HEADS, s, HEAD_PAD), F32)
    return pl.pallas_call(
        body,
        grid_spec=pltpu.PrefetchScalarGridSpec(
            num_scalar_prefetch=2, grid=(N_TOK_HEADS // nh, itab.shape[0]),
            in_specs=[qspec, kspec, kspec], out_specs=[qspec, qspec],
            scratch_shapes=[pltpu.VMEM((nh, t, HEAD_PAD), F32)] * 2),
        out_shape=[out, out],
        name="flash_fwd", compiler_params=_cp(2))(itab, jtab, q, k, v)


def _flash_bwd(q, k, v, stats, do, t, nh, ex):
    s = q.shape[1]
    nb = s // t
    itab, jtab = _causal_pairs(nb, True)
    npairs = itab.shape[0]
    ngroups = N_TOK_HEADS // nh
    scale = 1.0 / math.sqrt(QK_DIM)
    c2 = LOG2E * scale
    nch = t // LANES
    nx = ex.n if ex is not None else 0
    ex_arrays, ex_out_shape, ex_scratch = (
        (ex.arrays, ex.out_shape, ex.scratch) if ex is not None else ([], [], []))

    def body(it_ref, jt_ref, q_ref, k_ref, v_ref, st_ref, do_ref, *rest):
        ex_in, rest = rest[:nx], rest[nx:]
        dq_ref, dk_ref, dv_ref = rest[:3]
        ex_out, rest = rest[3:3 + nx], rest[3 + nx:]
        dk_scr, dv_scr = rest[:2]
        ex_sems = rest[2:]
        pair = pl.program_id(1)
        i, j = it_ref[pair], jt_ref[pair]
        rows_i = pl.ds(pl.multiple_of(i * t, t), t)

        if nx:
            @pl.when(jnp.logical_and(pl.program_id(0) == 0, pair == 0))
            def _():
                for cp in ex.copies(ex_in, ex_out, ex_sems):
                    cp.start()

        @pl.when(i == j)
        def _():
            dk_scr[...] = jnp.zeros_like(dk_scr)
            dv_scr[...] = jnp.zeros_like(dv_scr)

        @pl.when(j == 0)
        def _():
            dq_ref[:, rows_i, :] = jnp.zeros((nh, t, HEAD_PAD), F32)

        def prob_strips(masked, h, sc, dp):
            ps, dss = [], []
            low = _lane((STRIP, LANES)) < HEAD_DIM
            for r0 in range(0, t, STRIP):
                rows = slice(r0, r0 + STRIP)
                st = st_ref[h, rows, :]
                swapped = pltpu.roll(st, HEAD_DIM, 1)
                lse = jnp.where(low, st, swapped)
                delta = jnp.where(low, swapped, st)
                if masked:
                    rr = r0 + lax.broadcasted_iota(jnp.int32, (STRIP, LANES), 0)
                    cc = lax.broadcasted_iota(jnp.int32, (STRIP, LANES), 1)
                pcs, dcs = [], []
                for n in range(nch):
                    cols = slice(n * LANES, (n + 1) * LANES)
                    x = sc[rows, cols] * c2
                    if masked:
                        x = jnp.where(cc + n * LANES <= rr, x, NEG_BIG)
                    p = jnp.exp2(x - lse)
                    pcs.append(p.astype(BF16))
                    dcs.append((p * (dp[rows, cols] - delta) * scale).astype(BF16))
                ps.append(jnp.concatenate(pcs, axis=1))
                dss.append(jnp.concatenate(dcs, axis=1))
            return jnp.concatenate(ps, axis=0), jnp.concatenate(dss, axis=0)

        def step(masked):
            scs = [_dot_nt(q_ref[h], k_ref[h]) for h in range(nh)]
            dps = [_dot_nt(do_ref[h], v_ref[h]) for h in range(nh)]
            for h in range(nh):
                p, ds = prob_strips(masked, h, scs[h], dps[h])
                dv_scr[h] += _dot_tn(p, do_ref[h])
                dk_scr[h] += _dot_tn(ds, q_ref[h])
                dq_ref[h, rows_i, :] += _dot(ds, k_ref[h])

        @pl.when(i > j)
        def _():
            step(False)

        @pl.when(i == j)
        def _():
            step(True)

        @pl.when(i == nb - 1)
        def _():
            dk_ref[...] = dk_scr[...]
            dv_ref[...] = dv_scr[...]

        if nx:
            @pl.when(jnp.logical_and(pl.program_id(0) == ngroups - 1, pair == npairs - 1))
            def _():
                for cp in ex.copies(ex_in, ex_out, ex_sems):
                    cp.wait()

    qspec = pl.BlockSpec((nh, t, HEAD_PAD), lambda h, p, it, jt: (h, it[p], 0))
    kspec = pl.BlockSpec((nh, t, HEAD_PAD), lambda h, p, it, jt: (h, jt[p], 0))
    dqspec = pl.BlockSpec((nh, s, HEAD_PAD), lambda h, p, it, jt: (h, 0, 0))
    out = jax.ShapeDtypeStruct((N_TOK_HEADS, s, HEAD_PAD), F32)
    res = pl.pallas_call(
        body,
        grid_spec=pltpu.PrefetchScalarGridSpec(
            num_scalar_prefetch=2, grid=(ngroups, npairs),
            in_specs=[qspec, kspec, kspec, qspec, qspec] + [ANY] * nx,
            out_specs=[dqspec, kspec, kspec] + [ANY] * nx,
            scratch_shapes=[pltpu.VMEM((nh, t, HEAD_PAD), F32)] * 2 + ex_scratch),
        out_shape=[out, out, out] + ex_out_shape,
        name="flash_bwd", compiler_params=_cp(2))(itab, jtab, q, k, v, stats, do, *ex_arrays)
    return res[:3], res[3:]


def _mem_probs(qp, kp, hh):
    lane = _lane(qp.shape)
    keep = (lane < HEAD_DIM) if hh == 0 else (lane >= HEAD_DIM)
    qh = jnp.where(keep, qp, 0.0).astype(BF16)
    sc = _dot_nt(qh, kp) * (1.0 / math.sqrt(HEAD_DIM))
    e = jnp.exp(sc - jnp.max(sc, axis=-1, keepdims=True))
    return e / jnp.sum(e, axis=-1, keepdims=True), keep


def _gate_mem_fwd(tok, z, memkv, g0, q0, padded, name, ts):
    s = z.shape[0]
    zw = z.shape[1]
    tok_spec = _heads(ts) if padded else _rows(ts, TOK_WIDTH)

    def body(tok_ref, z_ref, mkv_ref, cat_ref, y_ref):
        if padded:
            for p in range(N_TOK_HEADS // 2):
                cat_ref[:, p * LANES:(p + 1) * LANES] = (
                    tok_ref[2 * p] + pltpu.roll(tok_ref[2 * p + 1], HEAD_DIM, 1))
        else:
            cat_ref[:, 0:TOK_WIDTH] = tok_ref[...]
        for pr in range(N_MEM_HEADS // 2):
            sl = slice(pr * LANES, (pr + 1) * LANES)
            qp = z_ref[:, q0 + pr * LANES:q0 + (pr + 1) * LANES]
            kp = mkv_ref[:, sl].astype(BF16)
            vp = mkv_ref[:, MEM_WIDTH + pr * LANES:MEM_WIDTH + (pr + 1) * LANES].astype(BF16)
            outs = []
            for hh in range(2):
                p, _ = _mem_probs(qp, kp, hh)
                outs.append(_dot(p.astype(BF16), vp))
            lane = _lane(outs[0].shape)
            cat_ref[:, TOK_WIDTH + pr * LANES:TOK_WIDTH + (pr + 1) * LANES] = jnp.where(
                lane < HEAD_DIM, outs[0], outs[1])
        gate = z_ref[:, g0:g0 + MIX_WIDTH]
        y_ref[...] = cat_ref[...] * (gate * _sigmoid(gate))

    out = jax.ShapeDtypeStruct((s, MIX_WIDTH), F32)
    return pl.pallas_call(
        body, grid=(s // ts,),
        in_specs=[tok_spec, _rows(ts, zw), _full((MEM_LEN, 2 * MEM_WIDTH))],
        out_specs=[_rows(ts, MIX_WIDTH)] * 2,
        out_shape=[out, out],
        name=name, compiler_params=_cp(1))(tok, z, memkv)


def _gate_mem_bwd(dy, cat, z, memkv, lse, g0, q0, name, ts):
    s = z.shape[0]
    zw = z.shape[1]
    padded = lse is not None
    gq_w = MIX_WIDTH + MEM_WIDTH

    def body(*refs):
        if padded:
            dy_ref, cat_ref, z_ref, mkv_ref, lse_ref, dzg_ref, dtok_ref, dmkv_ref, st_ref = refs
        else:
            dy_ref, cat_ref, z_ref, mkv_ref, dzg_ref, dtok_ref, dmkv_ref = refs

        @pl.when(pl.program_id(0) == 0)
        def _():
            dmkv_ref[...] = jnp.zeros_like(dmkv_ref)

        gate = z_ref[:, g0:g0 + MIX_WIDTH]
        sg = _sigmoid(gate)
        dy_ = dy_ref[...]
        dzg_ref[:, 0:MIX_WIDTH] = dy_ * cat_ref[...] * (sg * (1.0 + gate * (1.0 - sg)))
        dcat = dy_ * (gate * sg)
        if padded:
            low = _lane((ts, LANES)) < HEAD_DIM
            for p in range(N_TOK_HEADS // 2):
                d = dcat[:, p * LANES:(p + 1) * LANES]
                prod = d * cat_ref[:, p * LANES:(p + 1) * LANES]
                first = jnp.sum(jnp.where(low, prod, 0.0), axis=-1, keepdims=True)
                second = jnp.sum(jnp.where(low, 0.0, prod), axis=-1, keepdims=True)
                dtok_ref[2 * p] = jnp.where(low, d, 0.0).astype(BF16)
                dtok_ref[2 * p + 1] = jnp.where(low, pltpu.roll(d, HEAD_DIM, 1), 0.0).astype(BF16)
                st_ref[2 * p] = jnp.where(low, lse_ref[2 * p], first)
                st_ref[2 * p + 1] = jnp.where(low, lse_ref[2 * p + 1], second)
        else:
            dtok_ref[...] = dcat[:, 0:TOK_WIDTH]
        for pr in range(N_MEM_HEADS // 2):
            sl = slice(pr * LANES, (pr + 1) * LANES)
            vsl = slice(MEM_WIDTH + pr * LANES, MEM_WIDTH + (pr + 1) * LANES)
            qp = z_ref[:, q0 + pr * LANES:q0 + (pr + 1) * LANES]
            qpb = qp.astype(BF16)
            kp = mkv_ref[:, sl].astype(BF16)
            vp = mkv_ref[:, vsl].astype(BF16)
            dmo = dcat[:, TOK_WIDTH + pr * LANES:TOK_WIDTH + (pr + 1) * LANES]
            dqp = None
            for hh in range(2):
                p, keep = _mem_probs(qp, kp, hh)
                do_h = jnp.where(keep, dmo, 0.0).astype(BF16)
                dmkv_ref[:, vsl] += _dot_tn(p.astype(BF16), do_h)
                dp = _dot_nt(do_h, vp)
                ds = (p * (dp - jnp.sum(dp * p, axis=-1, keepdims=True))
                      * (1.0 / math.sqrt(HEAD_DIM))).astype(BF16)
                dqh = jnp.where(keep, _dot(ds, kp), 0.0)
                dqp = dqh if dqp is None else dqp + dqh
                dkh = _dot_tn(ds, qpb)
                klane = _lane(dkh.shape)
                kkeep = (klane < HEAD_DIM) if hh == 0 else (klane >= HEAD_DIM)
                dmkv_ref[:, sl] += jnp.where(kkeep, dkh, 0.0)
            dzg_ref[:, MIX_WIDTH + pr * LANES:MIX_WIDTH + (pr + 1) * LANES] = dqp

    in_specs = [_rows(ts, MIX_WIDTH), _rows(ts, MIX_WIDTH), _rows(ts, zw),
                _full((MEM_LEN, 2 * MEM_WIDTH))]
    out_specs = [_rows(ts, gq_w), _heads(ts) if padded else _rows(ts, TOK_WIDTH),
                 _full((MEM_LEN, 2 * MEM_WIDTH))]
    heads_shape = (N_TOK_HEADS, s, HEAD_PAD)
    out_shape = [jax.ShapeDtypeStruct((s, gq_w), F32),
                 jax.ShapeDtypeStruct(heads_shape, BF16) if padded
                 else jax.ShapeDtypeStruct((s, TOK_WIDTH), F32),
                 jax.ShapeDtypeStruct((MEM_LEN, 2 * MEM_WIDTH), F32)]
    args = [dy, cat, z, memkv]
    if padded:
        in_specs.append(_heads(ts))
        out_specs.append(_heads(ts))
        out_shape.append(jax.ShapeDtypeStruct(heads_shape, F32))
        args.append(lse)
    return pl.pallas_call(
        body, grid=(s // ts,), in_specs=in_specs, out_specs=out_specs, out_shape=out_shape,
        name=name, compiler_params=_cp(1))(*args)


def _ln_stats(pre):
    mu = jnp.mean(pre, axis=-1, keepdims=True)
    d = pre - mu
    rstd = lax.rsqrt(jnp.mean(d * d, axis=-1, keepdims=True) + NORM_EPS)
    return d * rstd, rstd


def _outproj_ln_fwd(y, w, h, g, b, tgt, name, ts):
    s = y.shape[0]
    with_loss = tgt is not None

    def body(*refs):
        if with_loss:
            y_ref, w_ref, h_ref, g_ref, b_ref, t_ref, pre_ref, out_ref, loss_ref = refs
        else:
            y_ref, w_ref, h_ref, g_ref, b_ref, pre_ref, out_ref = refs
        pre = ALPHA * h_ref[...] + _dot(y_ref[...].astype(BF16), w_ref[...])
        pre_ref[...] = pre
        xhat, _ = _ln_stats(pre)
        hout = xhat * g_ref[...] + b_ref[...]
        if with_loss:
            @pl.when(pl.program_id(0) == 0)
            def _():
                loss_ref[...] = jnp.zeros_like(loss_ref)
            err = hout - t_ref[...]
            out_ref[...] = err * (1.0 / D_MODEL)
            loss_ref[...] += 0.5 * jnp.sum(jnp.mean(err * err, axis=-1, keepdims=True))
        else:
            out_ref[...] = hout

    act = jax.ShapeDtypeStruct((s, D_MODEL), F32)
    in_specs = [_rows(ts, MIX_WIDTH), _full((MIX_WIDTH, D_MODEL)), _rows(ts, D_MODEL),
                _full((1, D_MODEL)), _full((1, D_MODEL))]
    out_specs = [_rows(ts, D_MODEL)] * 2
    out_shape = [act, act]
    args = [y, w, h, g, b]
    if with_loss:
        in_specs.append(_rows(ts, D_MODEL))
        out_specs.append(_full((SUBLANES, LANES)))
        out_shape.append(jax.ShapeDtypeStruct((SUBLANES, LANES), F32))
        args.append(tgt)
    return pl.pallas_call(
        body, grid=(s // ts,), in_specs=in_specs, out_specs=out_specs, out_shape=out_shape,
        name=name, compiler_params=_cp(1))(*args)


def _outproj_ln_bwd(dh, pre, g, y, w_t, name, ts):
    s = y.shape[0]

    def body(dh_ref, pre_ref, g_ref, y_ref, wt_ref, dpre_ref, dy_ref, dw_ref, dgb_ref):
        @pl.when(pl.program_id(0) == 0)
        def _():
            dw_ref[...] = jnp.zeros_like(dw_ref)
            dgb_ref[...] = jnp.zeros_like(dgb_ref)

        dh_ = dh_ref[...]
        xhat, rstd = _ln_stats(pre_ref[...])
        dxh = dh_ * g_ref[...]
        dpre = rstd * (dxh - jnp.mean(dxh, axis=-1, keepdims=True)
                       - xhat * jnp.mean(dxh * xhat, axis=-1, keepdims=True))
        dpre_ref[...] = dpre
        dgb_ref[0:1, :] += jnp.sum(dh_ * xhat, axis=0, keepdims=True)
        dgb_ref[1:2, :] += jnp.sum(dh_, axis=0, keepdims=True)
        dpb = dpre.astype(BF16)
        dy_ref[...] = _dot(dpb, wt_ref[...])
        dw_ref[...] += _dot_tn(y_ref[...].astype(BF16), dpb)

    act = jax.ShapeDtypeStruct((s, D_MODEL), F32)
    return pl.pallas_call(
        body, grid=(s // ts,),
        in_specs=[_rows(ts, D_MODEL), _rows(ts, D_MODEL), _full((1, D_MODEL)),
                  _rows(ts, MIX_WIDTH), _full((D_MODEL, MIX_WIDTH))],
        out_specs=[_rows(ts, D_MODEL), _rows(ts, MIX_WIDTH), _full((MIX_WIDTH, D_MODEL)),
                   _full((SUBLANES, D_MODEL))],
        out_shape=[act, act, jax.ShapeDtypeStruct((MIX_WIDTH, D_MODEL), F32),
                   jax.ShapeDtypeStruct((SUBLANES, D_MODEL), F32)],
        name=name, compiler_params=_cp(1))(dh, pre, g, y, w_t)


def _linear_bwd(x, dys, offs, w_t, resid, name, ts):
    s, kdim = x.shape
    n = w_t.shape[0]
    widths = [d.shape[1] for d in dys]
    npieces = len(dys)

    def body(*refs):
        x_ref = refs[0]
        dy_refs = refs[1:1 + npieces]
        wt_ref, r_ref, dx_ref, dw_ref = refs[1 + npieces:]

        @pl.when(pl.program_id(0) == 0)
        def _():
            dw_ref[...] = jnp.zeros_like(dw_ref)

        xb = x_ref[...].astype(BF16)
        dx = ALPHA * r_ref[...]
        for dy_ref, off, wd in zip(dy_refs, offs, widths):
            dyb = dy_ref[...].astype(BF16)
            dx = dx + _dot(dyb, wt_ref[off:off + wd, :])
            dw_ref[:, off:off + wd] += _dot_tn(xb, dyb)
        dx_ref[...] = dx

    return pl.pallas_call(
        body, grid=(s // ts,),
        in_specs=[_rows(ts, kdim)] + [_rows(ts, wd) for wd in widths]
                 + [_full((n, kdim)), _rows(ts, kdim)],
        out_specs=[_rows(ts, kdim), _full((kdim, n))],
        out_shape=[jax.ShapeDtypeStruct((s, kdim), F32), jax.ShapeDtypeStruct((kdim, n), F32)],
        name=name, compiler_params=_cp(1))(x, *dys, w_t, resid)


def _wgrad_small(x, dy, name):
    def body(x_ref, dy_ref, dw_ref):
        dw_ref[...] = _dot_tn(x_ref[...].astype(BF16), dy_ref[...].astype(BF16))

    return pl.pallas_call(
        body, out_shape=jax.ShapeDtypeStruct((x.shape[1], dy.shape[1]), F32),
        name=name, compiler_params=pltpu.CompilerParams(vmem_limit_bytes=VMEM_LIMIT))(x, dy)


def _shift_down(u, carry8, k):
    if k == 0:
        return u
    rolled = pltpu.roll(u, k, 0)
    row = lax.broadcasted_iota(jnp.int32, carry8.shape, 0)
    top = jnp.where(row < k, pltpu.roll(carry8, k, 0), rolled[0:SUBLANES])
    return jnp.concatenate([top, rolled[SUBLANES:]], axis=0)


def _shift_up(u, carry8, k):
    if k == 0:
        return u
    n = u.shape[0]
    rolled = pltpu.roll(u, n - k, 0)
    row = lax.broadcasted_iota(jnp.int32, carry8.shape, 0)
    bot = jnp.where(row >= SUBLANES - k, pltpu.roll(carry8, SUBLANES - k, 0),
                    rolled[n - SUBLANES:])
    return jnp.concatenate([rolled[:n - SUBLANES], bot], axis=0)


def _neg_expm1(t):
    e = jnp.exp(t)
    em1 = e - 1.0
    safe = jnp.where(e == 1.0, 1.0, jnp.log(e))
    return -jnp.where(e == 1.0, t, jnp.where(em1 == -1.0, -1.0, em1 * t / safe))


def _lru_gates(u, carry8, cw_ref, vec_ref, wr_ref, wi_ref):
    xc = vec_ref[0:1, :] + cw_ref[3:4, :] * u
    for k in range(1, CONV_W):
        xc = xc + cw_ref[3 - k:4 - k, :] * _shift_down(u, carry8, k)
    xb = xc.astype(BF16)
    r = _sigmoid(_dot(xb, wr_ref[...]) + vec_ref[1:2, :])
    ig = _sigmoid(_dot(xb, wi_ref[...]) + vec_ref[2:3, :])
    nlam = -vec_ref[3:4, :]
    softplus = jnp.maximum(nlam, 0.0) + jnp.log(1.0 + jnp.exp(-jnp.abs(nlam)))
    cneg = -LRU_C * softplus
    log_a = cneg * r
    a = jnp.exp(log_a)
    sq = jnp.sqrt(_neg_expm1(2.0 * log_a))
    return xc, r, ig, cneg, a, sq


def _lru_fwd(z1, cw8, vec8, wr, wi, ts):
    s = z1.shape[0]

    def body(u_ref, cw_ref, vec_ref, wr_ref, wi_ref, hs_ref, cu_scr, ch_scr, a_scr, gx_scr):
        @pl.when(pl.program_id(0) == 0)
        def _():
            cu_scr[...] = jnp.zeros_like(cu_scr)
            ch_scr[...] = jnp.zeros_like(ch_scr)

        u = u_ref[...]
        xc, _, ig, _, a, sq = _lru_gates(u, cu_scr[...], cw_ref, vec_ref, wr_ref, wi_ref)
        a_scr[...] = a
        gx_scr[...] = sq * (ig * xc)

        def step(t, h):
            h = a_scr[pl.ds(t, 1), :] * h + gx_scr[pl.ds(t, 1), :]
            hs_ref[pl.ds(t, 1), :] = h
            return h

        ch_scr[0:1, :] = lax.fori_loop(0, ts, step, ch_scr[0:1, :])
        cu_scr[...] = u[ts - SUBLANES:, :]

    w = TOK_WIDTH
    return pl.pallas_call(
        body, grid=(s // ts,),
        in_specs=[_rows(ts, w), _full((SUBLANES, w)), _full((SUBLANES, w)),
                  _full((w, w)), _full((w, w))],
        out_specs=_rows(ts, w),
        out_shape=jax.ShapeDtypeStruct((s, w), F32),
        scratch_shapes=[pltpu.VMEM((SUBLANES, w), F32), pltpu.VMEM((SUBLANES, w), F32),
                        pltpu.VMEM((ts, w), F32), pltpu.VMEM((ts, w), F32)],
        name="lru_fwd", compiler_params=_cp(1))(z1, cw8, vec8, wr, wi)


def _lru_bwd(z1, dhs, hs, cw8, vec8, wr, wi, wr_t, wi_t, ts):
    s = z1.shape[0]
    nb = s // ts
    w = TOK_WIDTH
    tiles = ts // SUBLANES

    def body(u_ref, up_ref, dhs_ref, hs_ref, hsp_ref, cw_ref, vec_ref, wr_ref, wi_ref,
             wrt_ref, wit_ref, du_ref, dwr_ref, dwi_ref, dvec_ref,
             cc_scr, cd_scr, a_scr, dh_scr):
        i = pl.program_id(0)

        @pl.when(i == 0)
        def _():
            cc_scr[...] = jnp.zeros_like(cc_scr)
            cd_scr[...] = jnp.zeros_like(cd_scr)
            dwr_ref[...] = jnp.zeros_like(dwr_ref)
            dwi_ref[...] = jnp.zeros_like(dwi_ref)
            dvec_ref[...] = jnp.zeros_like(dvec_ref)

        u = u_ref[...]
        first = i == nb - 1
        carry8 = jnp.where(first, 0.0, up_ref[...])
        xc, r, ig, cneg, a, sq = _lru_gates(u, carry8, cw_ref, vec_ref, wr_ref, wi_ref)
        a_scr[...] = a

        def step(n, c):
            t = ts - 1 - n
            dh = dhs_ref[pl.ds(t, 1), :] + c
            dh_scr[pl.ds(t, 1), :] = dh
            return a_scr[pl.ds(t, 1), :] * dh

        cc_scr[0:1, :] = lax.fori_loop(0, ts, step, cc_scr[0:1, :])
        dh = dh_scr[...]
        hprev = _shift_down(hs_ref[...], jnp.where(first, 0.0, hsp_ref[...]), 1)
        ix = ig * xc
        dix = dh * sq
        dlog_a = dh * hprev * a - (dh * ix) * (a * a) / sq
        dpr = (dlog_a * cneg) * r * (1.0 - r)
        dpi = (dix * xc) * ig * (1.0 - ig)
        dprb, dpib = dpr.astype(BF16), dpi.astype(BF16)
        xb = xc.astype(BF16)
        dwr_ref[...] += _dot_tn(xb, dprb)
        dwi_ref[...] += _dot_tn(xb, dpib)
        dxc = dix * ig + _dot(dprb, wrt_ref[...]) + _dot(dpib, wit_ref[...])
        for k in range(CONV_W):
            dvec_ref[3 - k:4 - k, :] += jnp.sum(dxc * _shift_down(u, carry8, k),
                                                axis=0, keepdims=True)
        dvec_ref[4:5, :] += jnp.sum(dxc, axis=0, keepdims=True)
        dvec_ref[5:6, :] += jnp.sum(dpr, axis=0, keepdims=True)
        dvec_ref[6:7, :] += jnp.sum(dpi, axis=0, keepdims=True)
        dvec_ref[7:8, :] += (jnp.sum(dlog_a * r, axis=0, keepdims=True)
                             * (LRU_C * _sigmoid(-vec_ref[3:4, :])))
        nxt = cd_scr[...]
        du = cw_ref[3:4, :] * dxc
        for k in range(1, CONV_W):
            du = du + cw_ref[3 - k:4 - k, :] * _shift_up(dxc, nxt, k)
        du_ref[...] = du
        cd_scr[...] = dxc[0:SUBLANES, :]

    rev = lambda i: (nb - 1 - i, 0)
    prev8 = lambda i: (jnp.maximum((nb - 1 - i) * tiles - 1, 0), 0)
    blk = pl.BlockSpec((ts, w), rev)
    before = pl.BlockSpec((SUBLANES, w), prev8)
    scr = pltpu.VMEM((ts, w), F32)
    return pl.pallas_call(
        body, grid=(nb,),
        in_specs=[blk, before, blk, blk, before,
                  _full((SUBLANES, w)), _full((SUBLANES, w)),
                  _full((w, w)), _full((w, w)), _full((w, w)), _full((w, w))],
        out_specs=[blk, _full((w, w)), _full((w, w)), _full((SUBLANES, w))],
        out_shape=[jax.ShapeDtypeStruct((s, w), F32), jax.ShapeDtypeStruct((w, w), F32),
                   jax.ShapeDtypeStruct((w, w), F32), jax.ShapeDtypeStruct((SUBLANES, w), F32)],
        scratch_shapes=[pltpu.VMEM((SUBLANES, w), F32), pltpu.VMEM((SUBLANES, w), F32),
                        scr, scr],
        name="lru_bwd", compiler_params=_cp(1))(
            z1, z1, dhs, hs, hs, cw8, vec8, wr, wi, wr_t, wi_t)


def _adamw(parts, w, m, v, name):
    n = len(parts)
    rows_per = parts[0].shape[1]

    def body(*refs):
        p_refs = refs[:n]
        w_ref, m_ref, v_ref, g_ref, d_ref, nm_ref, nv_ref = refs[n:]
        for l, p_ref in enumerate(p_refs):
            rows = slice(l * rows_per, (l + 1) * rows_per)
            g = p_ref[0].astype(F32)
            for dev in range(1, N_DEV):
                g = g + p_ref[dev].astype(F32)
            g_ref[rows, :] = g
            nm = ADAM_B1 * m_ref[rows, :] + (1.0 - ADAM_B1) * g
            nv = ADAM_B2 * v_ref[rows, :] + (1.0 - ADAM_B2) * (g * g)
            m_hat = nm / (1.0 - ADAM_B1 ** ADAM_STEP)
            v_hat = nv / (1.0 - ADAM_B2 ** ADAM_STEP)
            d_ref[rows, :] = -ADAM_LR * (m_hat / (jnp.sqrt(v_hat) + ADAM_EPS)
                                         + ADAM_WD * w_ref[rows, :])
            nm_ref[rows, :] = nm
            nv_ref[rows, :] = nv

    out = jax.ShapeDtypeStruct(w.shape, F32)
    return pl.pallas_call(
        body, out_shape=[out] * 4, name=name,
        compiler_params=pltpu.CompilerParams(vmem_limit_bytes=VMEM_LIMIT))(*parts, w, m, v)


ANY = pl.BlockSpec(memory_space=pl.ANY)
MESH = pl.DeviceIdType.MESH


def _slot(p):
    return 4 * p[0] + 2 * p[1] + p[2]


def _allgather(xs):
    n = len(xs)

    def body(*refs):
        x_refs, o_refs = refs[:n], refs[n:2 * n]
        send_sems, recv_sems, local_sems = refs[2 * n:]
        x, y, c = lax.axis_index("x"), lax.axis_index("y"), lax.axis_index("c")
        me, sibling = (x, y, c), (x, y, 1 - c)
        chips = [(1 - x, y), (x, 1 - y), (1 - x, 1 - y)]

        def copy(a, k, block, to, from_input=False):
            dst = o_refs[a].at[_slot(block)]
            return pltpu.make_async_remote_copy(
                src_ref=x_refs[a] if from_input else dst, dst_ref=dst,
                send_sem=send_sems.at[a, k], recv_sem=recv_sems.at[a, k],
                device_id=to, device_id_type=MESH)

        mine = [pltpu.make_async_copy(x_refs[a], o_refs[a].at[_slot(me)], local_sems.at[a])
                for a in range(n)]
        for cp in mine:
            cp.start()
        first = []
        for a in range(n):
            first.append(copy(a, 0, me, sibling, True))
            first += [copy(a, 1 + j, me, (*chip, c), True) for j, chip in enumerate(chips)]
        for cp in first:
            cp.start()
        passed = []
        for j, chip in enumerate(chips):
            for a in range(n):
                copy(a, 1 + j, (*chip, c), me).wait_recv()
                cp = copy(a, 4 + j, (*chip, c), sibling)
                cp.start()
                passed.append(cp)
        for a in range(n):
            copy(a, 0, sibling, me).wait_recv()
            for j, chip in enumerate(chips):
                copy(a, 4 + j, (*chip, 1 - c), me).wait_recv()
        for cp in first + passed:
            cp.wait_send()
        for cp in mine:
            cp.wait()

    return pl.pallas_call(
        body,
        out_shape=[jax.ShapeDtypeStruct((N_DEV,) + t.shape, t.dtype) for t in xs],
        in_specs=[ANY] * n, out_specs=[ANY] * n,
        scratch_shapes=[pltpu.SemaphoreType.DMA((n, 7)), pltpu.SemaphoreType.DMA((n, 7)),
                        pltpu.SemaphoreType.DMA((n,))],
        name="allgather_weights")(*xs)


class _Exchange:
    def __init__(self, arrays, kinds):
        self.arrays, self.kinds, self.n = list(arrays), list(kinds), len(arrays)
        self.shapes = [self._part_shape(a, k) for a, k in zip(arrays, kinds)]
        self.out_shape = [jax.ShapeDtypeStruct((N_DEV,) + shp, a.dtype)
                          for shp, a in zip(self.shapes, arrays)]
        self.scratch = [pltpu.SemaphoreType.DMA((self.n, N_DEV - 1)),
                        pltpu.SemaphoreType.DMA((self.n, N_DEV - 1)),
                        pltpu.SemaphoreType.DMA((self.n,))]

    @staticmethod
    def _part_shape(arr, kind):
        if kind == "chunks":
            return arr.shape[1:]
        if kind == "cols":
            return (arr.shape[0], arr.shape[1] // N_DEV)
        if kind == "rows":
            return (arr.shape[0] // N_DEV, arr.shape[1])
        return arr.shape

    def copies(self, in_refs, out_refs, sems):
        send_sems, recv_sems, local_sems = sems
        x, y, c = lax.axis_index("x"), lax.axis_index("y"), lax.axis_index("c")
        me = _slot((x, y, c))

        def part(a, dev):
            ref, kind, shp = in_refs[a], self.kinds[a], self.shapes[a]
            if kind == "chunks":
                return ref.at[dev]
            if kind == "cols":
                return ref.at[:, pl.ds(pl.multiple_of(dev * shp[1], LANES), shp[1])]
            if kind == "rows":
                return ref.at[pl.ds(pl.multiple_of(dev * shp[0], SUBLANES), shp[0]), :]
            return ref

        cps = [pltpu.make_async_copy(part(a, me), out_refs[a].at[me], local_sems.at[a])
               for a in range(self.n)]
        for rel in range(1, N_DEV):
            peer = (x ^ (rel >> 2), y ^ ((rel >> 1) & 1), c ^ (rel & 1))
            for a in range(self.n):
                cps.append(pltpu.make_async_remote_copy(
                    src_ref=part(a, _slot(peer)), dst_ref=out_refs[a].at[me],
                    send_sem=send_sems.at[a, rel - 1], recv_sem=recv_sems.at[a, rel - 1],
                    device_id=peer, device_id_type=MESH))
        return cps


def _exchange_grads(arrays, kinds, name):
    ex = _Exchange(arrays, kinds)
    n = ex.n

    def body(*refs):
        cps = ex.copies(refs[:n], refs[n:2 * n], refs[2 * n:])
        for cp in cps:
            cp.start()
        for cp in cps:
            cp.wait()

    return pl.pallas_call(
        body, out_shape=ex.out_shape, in_specs=[ANY] * n, out_specs=[ANY] * n,
        scratch_shapes=ex.scratch, name=name)(*arrays)


BIG = [("mla_w_in", (D_MODEL, MLA_IN), 1), ("mla_w_uq", (Q_LORA, N_TOK_HEADS * QK_DIM), 1),
       ("mla_w_ukv", (KV_LORA, N_TOK_HEADS * 2 * HEAD_DIM), 1), ("lru_w_in", (D_MODEL, LRU_IN), 1),
       ("w_mem_kv", (2, D_MODEL, 2 * MEM_WIDTH), 1), ("w_out", (2, MIX_WIDTH, D_MODEL), 1)]
SMALL = [("lru_conv_w", (CONV_W, TOK_WIDTH), 1), ("lru_conv_b", (TOK_WIDTH,), 0),
         ("lru_b_rgate", (TOK_WIDTH,), 0), ("lru_b_igate", (TOK_WIDTH,), 0),
         ("lru_lambda", (TOK_WIDTH,), 0)]
REPL = [("mla_q_norm", (Q_LORA,)), ("mla_kv_norm", (KV_LORA,)),
        ("lru_w_rgate", (N_TOK_HEADS, HEAD_DIM, HEAD_DIM)),
        ("lru_w_igate", (N_TOK_HEADS, HEAD_DIM, HEAD_DIM)),
        ("ln_g", (2, D_MODEL)), ("ln_b", (2, D_MODEL))]


def _shard_shape(shape, axis):
    return tuple(d // N_DEV if a == axis else d for a, d in enumerate(shape))


def _size(shape):
    return math.prod(shape)


BIG_ROWS = sum(_size(s) for _, s, _ in BIG) // N_DEV // LANES
SMALL_ROWS = SUBLANES


def _pack_rows(flat_parts, rows):
    flat = jnp.concatenate([p.reshape(-1) for p in flat_parts])
    return jnp.pad(flat, (0, rows * LANES - flat.shape[0])).reshape(rows, LANES)


def _to_chunks(full, axis):
    shape = full.shape
    split = shape[:axis] + (N_DEV, shape[axis] // N_DEV) + shape[axis + 1:]
    return jnp.moveaxis(full.reshape(split), axis, 0).reshape(N_DEV, -1)


def _from_chunks(chunks, shape, axis):
    sh = _shard_shape(shape, axis)
    t = chunks.reshape((N_DEV,) + sh)
    t = jnp.moveaxis(t, 0, axis)
    return t.reshape(shape)


def _split_flat(flat2d, table):
    out, off = [], 0
    for size in table:
        out.append(flat2d[:, off:off + size])
        off += size
    return out


def _win0_to_padded(w):
    z = lambda n: jnp.zeros((w.shape[0], n), w.dtype)
    return jnp.concatenate([w[:, 0:640], w[:, 672:1952], z(KR_LANE), w[:, 640:672],
                            z(LANES - KR_LANE - QK_ROPE)], axis=1)


def _win0_from_padded(wp):
    k0 = Z0_KR + KR_LANE
    return jnp.concatenate([wp[:, 0:640], wp[:, k0:k0 + QK_ROPE], wp[:, 640:1920]], axis=1)


def _pad_heads(w, per_head, lo, hi):
    t = w.reshape(w.shape[0], N_TOK_HEADS, per_head)[:, :, lo:hi]
    t = jnp.pad(t, ((0, 0), (0, 0), (0, HEAD_PAD - (hi - lo))))
    return t.reshape(w.shape[0], QKV_PAD)


def _unpad_heads(wp, width):
    return wp.reshape(wp.shape[0], N_TOK_HEADS, HEAD_PAD)[:, :, :width]


def _block_diag(w):
    eye = jnp.eye(N_TOK_HEADS, dtype=w.dtype)
    return (w[:, :, None, :] * eye[:, None, :, None]).reshape(TOK_WIDTH, TOK_WIDTH)


def _diag_blocks(d):
    t = d.reshape(N_TOK_HEADS, HEAD_DIM, N_TOK_HEADS, HEAD_DIM)
    return jnp.stack([t[g, :, g, :] for g in range(N_TOK_HEADS)])


def _rope_tables(positions):
    half = QK_ROPE // 2
    inv_freq = ROPE_THETA ** (-jnp.arange(half, dtype=F32) / half)
    ang = positions.astype(F32)[:, None] * inv_freq
    cos, sin = jnp.cos(ang), jnp.sin(ang)
    s = positions.shape[0]
    one, zero = jnp.ones((s, QK_NOPE), F32), jnp.zeros((s, half), F32)
    tail = jnp.zeros((s, HEAD_PAD - QK_DIM), F32)
    znope = jnp.zeros((s, QK_NOPE), F32)
    c = jnp.concatenate([one, cos, cos, tail], axis=1)
    sa = jnp.concatenate([znope, -sin, zero, tail], axis=1)
    sb = jnp.concatenate([znope, zero, sin, tail], axis=1)
    return c, sa, sb


def _local_step(x, mem, positions, tgt, wts, ts, tatt, early_exchange):
    bf = lambda t: t.astype(BF16)
    win0 = _win0_to_padded(wts["mla_w_in"])
    wuq = _pad_heads(wts["mla_w_uq"], QK_DIM, 0, QK_DIM)
    wukv = jnp.concatenate([_pad_heads(wts["mla_w_ukv"], 2 * HEAD_DIM, 0, QK_NOPE),
                            _pad_heads(wts["mla_w_ukv"], 2 * HEAD_DIM, QK_NOPE, 2 * HEAD_DIM)],
                           axis=1)
    win1 = wts["lru_w_in"]
    wmkv, wout = wts["w_mem_kv"], wts["w_out"]
    gq = wts["mla_q_norm"].reshape(1, Q_LORA)
    gkv = wts["mla_kv_norm"].reshape(1, KV_LORA)
    ln_g, ln_b = wts["ln_g"], wts["ln_b"]
    wr, wi = bf(_block_diag(wts["lru_w_rgate"])), bf(_block_diag(wts["lru_w_igate"]))
    cw8 = jnp.pad(wts["lru_conv_w"], ((0, SUBLANES - CONV_W), (0, 0)))
    vec8 = jnp.pad(jnp.stack([wts["lru_conv_b"], wts["lru_b_rgate"], wts["lru_b_igate"],
                              wts["lru_lambda"]]), ((0, SUBLANES - 4), (0, 0)))
    tabs = _rope_tables(positions)
    tmem = mem.shape[0]

    z0 = _rowmm(x, win0, "in_proj0", ts)
    q, k, v = _mla_prep_fwd(z0, tabs, gq, gkv, wuq, wukv, ts)
    o, lse = _flash_fwd(q, k, v, tatt, FWD_HEADS)
    mkv0 = _rowmm(mem, wmkv[0], "mem_kv0", tmem)
    cat0, y0 = _gate_mem_fwd(o, z0, mkv0, Z0_GATE, Z0_QMEM, True, "gate_mem_fwd0", ts)
    del o
    pre0, h1 = _outproj_ln_fwd(y0, wout[0], x, ln_g[0:1], ln_b[0:1], None, "outproj_ln_fwd0", ts)
    z1 = _rowmm(h1, win1, "in_proj1", ts)
    hs = _lru_fwd(z1, cw8, vec8, wr, wi, ts)
    mkv1 = _rowmm(mem, wmkv[1], "mem_kv1", tmem)
    cat1, y1 = _gate_mem_fwd(hs, z1, mkv1, Z1_GATE, Z1_QMEM, False, "gate_mem_fwd1", ts)
    pre1, dh2, loss8 = _outproj_ln_fwd(y1, wout[1], h1, ln_g[1:2], ln_b[1:2], tgt,
                                       "outproj_ln_loss", ts)
    loss = loss8[0, 0]

    dpre1, dy1, dwout1, dgb1 = _outproj_ln_bwd(dh2, pre1, ln_g[1:2], y1, wout[1].T,
                                               "outproj_ln_bwd1", ts)
    dzg1, dhs, dmkv1 = _gate_mem_bwd(dy1, cat1, z1, mkv1, None, Z1_GATE, Z1_QMEM,
                                     "gate_mem_bwd1", ts)
    du, dwr, dwi, dvec = _lru_bwd(z1, dhs, hs, cw8, vec8, wr, wi, wr.T, wi.T, ts)
    dh1, dwin1 = _linear_bwd(h1, [du, dzg1], [Z1_U, Z1_GATE], win1.T, dpre1, "in_proj_bwd1", ts)
    dwmkv1 = _wgrad_small(mem, dmkv1, "mem_kv_bwd1")
    dpre0, dy0, dwout0, dgb0 = _outproj_ln_bwd(dh1, pre0, ln_g[0:1], y0, wout[0].T,
                                               "outproj_ln_bwd0", ts)
    dzg0, do, dmkv0, stats = _gate_mem_bwd(dy0, cat0, z0, mkv0, lse, Z0_GATE, Z0_QMEM,
                                           "gate_mem_bwd0", ts)
    dwmkv0 = _wgrad_small(mem, dmkv0, "mem_kv_bwd0")
    early = {
        "lru_w_in": dwin1,
        "lru_small": dvec,
        "lru_w_rgate": _diag_blocks(dwr).reshape(TOK_WIDTH, HEAD_DIM),
        "lru_w_igate": _diag_blocks(dwi).reshape(TOK_WIDTH, HEAD_DIM),
        "w_mem_kv": [dwmkv0, dwmkv1],
        "w_out": [dwout0, dwout1],
    }
    (dq, dk, dv), got_early = _flash_bwd(q, k, v, stats, do, tatt, BWD_HEADS,
                                         early_exchange(early))
    dza, dzk, dwuq_p, dwukv_p, dg = _mla_prep_bwd(z0, dq, dk, dv, tabs, gq, gkv,
                                                  wuq.T, wukv.T, ts)
    gx, dwin0_p = _linear_bwd(x, [dza, dzg0, dzk], [Z0_CQ, Z0_GATE, Z0_KR], win0.T, dpre0,
                              "in_proj_bwd0", ts)

    dwukv = jnp.concatenate([_unpad_heads(dwukv_p[:, :QKV_PAD], HEAD_DIM),
                             _unpad_heads(dwukv_p[:, QKV_PAD:], HEAD_DIM)], axis=2)
    zrow = jnp.zeros((1, D_MODEL), F32)
    gains = jnp.pad(dg[0:1], ((0, 0), (0, D_MODEL - Q_LORA - KV_LORA)))
    small_repl = jnp.concatenate([dgb0[0:2], dgb1[0:2], gains,
                                  loss * jnp.ones((1, D_MODEL), F32), zrow, zrow], axis=0)
    late = {
        "mla_w_in": _win0_from_padded(dwin0_p),
        "mla_w_uq": _unpad_heads(dwuq_p, QK_DIM).reshape(Q_LORA, N_TOK_HEADS * QK_DIM),
        "mla_w_ukv": dwukv.reshape(KV_LORA, N_TOK_HEADS * 2 * HEAD_DIM),
        "small_repl": small_repl,
    }
    return gx, early, got_early, late


WEIGHT_ORDER = ["mla_w_in", "mla_q_norm", "mla_w_uq", "mla_kv_norm", "mla_w_ukv", "lru_w_in",
                "lru_conv_w", "lru_conv_b", "lru_w_rgate", "lru_b_rgate", "lru_w_igate",
                "lru_b_igate", "lru_lambda", "w_mem_kv", "w_out", "ln_g", "ln_b"]


def kernel(x, mem, positions, mla_w_in, mla_q_norm, mla_w_uq, mla_kv_norm, mla_w_ukv, lru_w_in, lru_conv_w, lru_conv_b, lru_w_rgate, lru_b_rgate, lru_w_igate, lru_b_igate, lru_lambda, w_mem_kv, w_out, ln_g, ln_b, loss_target, m_mla_w_in, m_mla_q_norm, m_mla_w_uq, m_mla_kv_norm, m_mla_w_ukv, m_lru_w_in, m_lru_conv_w, m_lru_conv_b, m_lru_w_rgate, m_lru_b_rgate, m_lru_w_igate, m_lru_b_igate, m_lru_lambda, m_w_mem_kv, m_w_out, m_ln_g, m_ln_b, v_mla_w_in, v_mla_q_norm, v_mla_w_uq, v_mla_kv_norm, v_mla_w_ukv, v_lru_w_in, v_lru_conv_w, v_lru_conv_b, v_lru_w_rgate, v_lru_b_rgate, v_lru_w_igate, v_lru_b_igate, v_lru_lambda, v_w_mem_kv, v_w_out, v_ln_g, v_ln_b):
    w_in = dict(mla_w_in=mla_w_in, mla_q_norm=mla_q_norm, mla_w_uq=mla_w_uq,
                mla_kv_norm=mla_kv_norm, mla_w_ukv=mla_w_ukv, lru_w_in=lru_w_in,
                lru_conv_w=lru_conv_w, lru_conv_b=lru_conv_b, lru_w_rgate=lru_w_rgate,
                lru_b_rgate=lru_b_rgate, lru_w_igate=lru_w_igate, lru_b_igate=lru_b_igate,
                lru_lambda=lru_lambda, w_mem_kv=w_mem_kv, w_out=w_out, ln_g=ln_g, ln_b=ln_b)
    m_in = dict(mla_w_in=m_mla_w_in, mla_q_norm=m_mla_q_norm, mla_w_uq=m_mla_w_uq,
                mla_kv_norm=m_mla_kv_norm, mla_w_ukv=m_mla_w_ukv, lru_w_in=m_lru_w_in,
                lru_conv_w=m_lru_conv_w, lru_conv_b=m_lru_conv_b, lru_w_rgate=m_lru_w_rgate,
                lru_b_rgate=m_lru_b_rgate, lru_w_igate=m_lru_w_igate, lru_b_igate=m_lru_b_igate,
                lru_lambda=m_lru_lambda, w_mem_kv=m_w_mem_kv, w_out=m_w_out, ln_g=m_ln_g,
                ln_b=m_ln_b)
    v_in = dict(mla_w_in=v_mla_w_in, mla_q_norm=v_mla_q_norm, mla_w_uq=v_mla_w_uq,
                mla_kv_norm=v_mla_kv_norm, mla_w_ukv=v_mla_w_ukv, lru_w_in=v_lru_w_in,
                lru_conv_w=v_lru_conv_w, lru_conv_b=v_lru_conv_b, lru_w_rgate=v_lru_w_rgate,
                lru_b_rgate=v_lru_b_rgate, lru_w_igate=v_lru_w_igate, lru_b_igate=v_lru_b_igate,
                lru_lambda=v_lru_lambda, w_mem_kv=v_w_mem_kv, w_out=v_w_out, ln_g=v_ln_g,
                ln_b=v_ln_b)
    s = x.shape[1]
    ts = min(ROW_BLOCK, s)
    tatt = min(ATT_BLOCK, s)
    big_sizes = [_size(sh) // N_DEV for _, sh, _ in BIG]
    small_sizes = [_size(sh) // N_DEV for _, sh, _ in SMALL]

    big_local = _pack_rows([w_in[n] for n, _, _ in BIG], BIG_ROWS).astype(BF16)
    small_local = _pack_rows([w_in[n] for n, _, _ in SMALL], SMALL_ROWS)
    big_all, small_all = _allgather([big_local, small_local])
    wts = {}
    for (n, sh, ax), part in zip(BIG, _split_flat(big_all.reshape(N_DEV, -1), big_sizes)):
        wts[n] = _from_chunks(part, sh, ax)
    for (n, sh, ax), part in zip(SMALL, _split_flat(small_all.reshape(N_DEV, -1), small_sizes)):
        wts[n] = _from_chunks(part, sh, ax)
    for n, sh in REPL:
        wts[n] = w_in[n].reshape(sh)

    def early_exchange(g):
        small_chunks = jnp.moveaxis(g["lru_small"].reshape(SUBLANES, N_DEV, -1), 1, 0)
        sends = [(g["lru_w_in"], "cols"),
                 (g["w_mem_kv"][0], "rows"), (g["w_mem_kv"][1], "rows"),
                 (g["w_out"][0], "rows"), (g["w_out"][1], "rows"),
                 (small_chunks, "chunks"), (g["lru_w_rgate"], "all"), (g["lru_w_igate"], "all")]
        return _Exchange([a for a, _ in sends], [k for _, k in sends])

    gx, _, got_early, late = _local_step(x[0], mem[0], positions[0], loss_target[0], wts,
                                         ts, tatt, early_exchange)

    def chunked(name, shape):
        w = shape[1] // N_DEV
        return _to_chunks(late[name], 1).reshape(N_DEV, shape[0], w).astype(BF16)

    got_late = _exchange_grads(
        [chunked("mla_w_in", (D_MODEL, MLA_IN)),
         chunked("mla_w_uq", (Q_LORA, N_TOK_HEADS * QK_DIM)),
         chunked("mla_w_ukv", (KV_LORA, N_TOK_HEADS * 2 * HEAD_DIM)), late["small_repl"]],
        ["chunks", "chunks", "chunks", "all"], "exchange_grads")
    got = list(got_late[:3]) + list(got_early) + [got_late[3]]

    def small_sharded(d):
        return jnp.concatenate([d["lru_conv_w"].reshape(CONV_W, -1), d["lru_conv_b"],
                                d["lru_b_rgate"], d["lru_b_igate"], d["lru_lambda"]], axis=0)

    def small_replicated(d):
        gains = jnp.concatenate([d["mla_q_norm"], d["mla_kv_norm"]], axis=1)
        gains = jnp.pad(gains, ((0, 0), (0, D_MODEL - gains.shape[1])))
        return jnp.concatenate([d["ln_g"][0:1], d["ln_b"][0:1], d["ln_g"][1:2], d["ln_b"][1:2],
                                gains, jnp.zeros((3, D_MODEL), F32)], axis=0)

    def flat2(d, name):
        t = d[name]
        return t.reshape(-1, t.shape[-1])

    def update(parts, view, name):
        return _adamw(parts, view(w_in), view(m_in), view(v_in), "adamw_" + name)

    res = {}
    for idx, name in [(0, "mla_w_in"), (1, "mla_w_uq"), (2, "mla_w_ukv"), (3, "lru_w_in"),
                      (9, "lru_w_rgate"), (10, "lru_w_igate")]:
        res[name] = update([got[idx]], functools.partial(flat2, name=name), name)
    res["w_mem_kv"] = update([got[4], got[5]], functools.partial(flat2, name="w_mem_kv"),
                             "w_mem_kv")
    res["w_out"] = update([got[6], got[7]], functools.partial(flat2, name="w_out"), "w_out")
    res_ss = update([got[8]], small_sharded, "small_sharded")
    res_sr = update([got[11]], small_replicated, "small_replicated")
    loss = res_sr[0][5, 0]

    result = [loss, gx.reshape(x.shape)]
    for kind in range(4):
        ss, sr = res_ss[kind], res_sr[kind]
        out = {n: res[n][kind].reshape(w_in[n].shape) for n in res}
        out["lru_conv_w"] = ss[0:CONV_W].reshape(w_in["lru_conv_w"].shape)
        out["lru_conv_b"], out["lru_b_rgate"] = ss[4:5], ss[5:6]
        out["lru_b_igate"], out["lru_lambda"] = ss[6:7], ss[7:8]
        out["ln_g"] = jnp.concatenate([sr[0:1], sr[2:3]], axis=0)
        out["ln_b"] = jnp.concatenate([sr[1:2], sr[3:4]], axis=0)
        out["mla_q_norm"] = sr[4:5, 0:Q_LORA]
        out["mla_kv_norm"] = sr[4:5, Q_LORA:Q_LORA + KV_LORA]
        result += [out[n] for n in WEIGHT_ORDER]
    return tuple(result)
```

```python
import functools
import math

import jax
import jax.numpy as jnp
from jax import lax
from jax.experimental import pallas as pl
from jax.experimental.pallas import tpu as pltpu

F32 = jnp.float32
BF16 = jnp.bfloat16

D_MODEL = 1024
MEM_LEN = 256
HEAD_DIM = 64
N_TOK_HEADS = 12
N_MEM_HEADS = 4
TOK_WIDTH = 768
MEM_WIDTH = 256
MIX_WIDTH = 1024
Q_LORA = 384
KV_LORA = 256
QK_NOPE = 64
QK_ROPE = 32
QK_DIM = 96
ROPE_THETA = 10000.0
CONV_W = 4
LRU_C = 8.0
ALPHA = (2.0 * 2) ** 0.25
NORM_EPS = 1e-6
MLA_IN = 1952
LRU_IN = 2048
ADAM_LR = 0.001
ADAM_B1 = 0.9
ADAM_B2 = 0.999
ADAM_EPS = 1e-08
ADAM_WD = 0.01
ADAM_STEP = 10

N_DEV = 8
LANES = 128
SUBLANES = 8
HEAD_PAD = 128
QKV_PAD = N_TOK_HEADS * HEAD_PAD
ZP = 2048
Z0_CQ, Z0_CKV, Z0_GATE, Z0_QMEM, Z0_KR = 0, 384, 640, 1664, 1920
KR_LANE = 64
Z1_U, Z1_GATE, Z1_QMEM = 0, 768, 1792

ROW_BLOCK = 512
ATT_BLOCK = 512
LOOKAHEAD = 3
FWD_HEADS = 12
BWD_HEADS = 4
VMEM_LIMIT = 56 * 1024 * 1024
NEG_BIG = -1e30
STRIP = 32
LOG2E = math.log2(math.e)


def _cp(n_axes):
    return pltpu.CompilerParams(dimension_semantics=("arbitrary",) * n_axes,
                                vmem_limit_bytes=VMEM_LIMIT)


def _dot(a, b):
    return jnp.dot(a, b, preferred_element_type=F32)


def _dot_nt(a, b):
    return lax.dot_general(a, b, (((1,), (1,)), ((), ())), preferred_element_type=F32)


def _dot_tn(a, b):
    return lax.dot_general(a, b, (((0,), (0,)), ((), ())), preferred_element_type=F32)


def _sigmoid(t):
    return 1.0 / (1.0 + jnp.exp(-t))


def _lane(shape):
    return lax.broadcasted_iota(jnp.int32, shape, len(shape) - 1)


def _full(shape):
    nd = len(shape)
    return pl.BlockSpec(shape, lambda *_: (0,) * nd)


def _rows(ts, width, col=0):
    return pl.BlockSpec((ts, width), lambda i: (i, col))


def _heads(ts):
    return pl.BlockSpec((N_TOK_HEADS, ts, HEAD_PAD), lambda i: (0, i, 0))


def _rowmm(x, w, name, ts):
    s, k = x.shape
    n = w.shape[1]

    def body(x_ref, w_ref, o_ref):
        o_ref[...] = _dot(x_ref[...].astype(BF16), w_ref[...])

    return pl.pallas_call(
        body, grid=(s // ts,),
        in_specs=[_rows(ts, k), _full((k, n))],
        out_specs=_rows(ts, n),
        out_shape=jax.ShapeDtypeStruct((s, n), F32),
        name=name, compiler_params=_cp(1))(x, w)


def _rms_parts(t):
    rs = lax.rsqrt(jnp.mean(t * t, axis=-1, keepdims=True) + NORM_EPS)
    return rs


def _rope(t, c, sa, sb):
    return t * c + pltpu.roll(t, LANES - 16, 1) * sa + pltpu.roll(t, 16, 1) * sb


def _rope_t(d, c, sa, sb):
    return d * c + pltpu.roll(d * sa, 16, 1) + pltpu.roll(d * sb, LANES - 16, 1)


def _mla_prep_fwd(z0, tabs, gq, gkv, wuq, wukv, ts):
    s = z0.shape[0]

    def body(z_ref, c_ref, sa_ref, sb_ref, gq_ref, gkv_ref, wuq_ref, wukv_ref,
             q_ref, k_ref, v_ref):
        cq = z_ref[:, Z0_CQ:Z0_CQ + Q_LORA]
        ckv = z_ref[:, Z0_CKV:Z0_CKV + KV_LORA]
        kr = z_ref[:, Z0_KR:Z0_KR + LANES]
        cqn = cq * _rms_parts(cq) * gq_ref[...]
        ckvn = ckv * _rms_parts(ckv) * gkv_ref[...]
        q = _dot(cqn.astype(BF16), wuq_ref[...])
        kv = _dot(ckvn.astype(BF16), wukv_ref[...])
        c, sa, sb = c_ref[...], sa_ref[...], sb_ref[...]
        krope = _rope(kr, c, sa, sb)
        pad_lane = _lane((ts, HEAD_PAD)) >= HEAD_DIM
        for h in range(N_TOK_HEADS):
            sl = slice(h * HEAD_PAD, (h + 1) * HEAD_PAD)
            q_ref[h] = _rope(q[:, sl], c, sa, sb).astype(BF16)
            k_ref[h] = (kv[:, sl] + krope).astype(BF16)
            vh = kv[:, QKV_PAD + h * HEAD_PAD:QKV_PAD + (h + 1) * HEAD_PAD]
            v_ref[h] = jnp.where(pad_lane, 1.0, vh).astype(BF16)

    out = jax.ShapeDtypeStruct((N_TOK_HEADS, s, HEAD_PAD), BF16)
    return pl.pallas_call(
        body, grid=(s // ts,),
        in_specs=[_rows(ts, ZP), _rows(ts, LANES), _rows(ts, LANES), _rows(ts, LANES),
                  _full((1, Q_LORA)), _full((1, KV_LORA)),
                  _full((Q_LORA, QKV_PAD)), _full((KV_LORA, 2 * QKV_PAD))],
        out_specs=[_heads(ts)] * 3,
        out_shape=[out, out, out],
        name="mla_prep_fwd", compiler_params=_cp(1))(z0, *tabs, gq, gkv, wuq, wukv)


def _mla_prep_bwd(z0, dq, dk, dv, tabs, gq, gkv, wuq_t, wukv_t, ts):
    s = z0.shape[0]

    def body(z_ref, dq_ref, dk_ref, dv_ref, c_ref, sa_ref, sb_ref, gq_ref, gkv_ref,
             wuqt_ref, wukvt_ref, dza_ref, dzk_ref, dwuq_ref, dwukv_ref, dg_ref):
        @pl.when(pl.program_id(0) == 0)
        def _():
            dwuq_ref[...] = jnp.zeros_like(dwuq_ref)
            dwukv_ref[...] = jnp.zeros_like(dwukv_ref)
            dg_ref[...] = jnp.zeros_like(dg_ref)

        cq = z_ref[:, Z0_CQ:Z0_CQ + Q_LORA]
        ckv = z_ref[:, Z0_CKV:Z0_CKV + KV_LORA]
        rq, rkv = _rms_parts(cq), _rms_parts(ckv)
        gq_, gkv_ = gq_ref[...], gkv_ref[...]
        cqn = (cq * rq * gq_).astype(BF16)
        ckvn = (ckv * rkv * gkv_).astype(BF16)
        c, sa, sb = c_ref[...], sa_ref[...], sb_ref[...]
        dqp, dksum = [], None
        for h in range(N_TOK_HEADS):
            dqp.append(_rope_t(dq_ref[h], c, sa, sb))
            dksum = dk_ref[h] if dksum is None else dksum + dk_ref[h]
        dqp = jnp.concatenate(dqp, axis=1).astype(BF16)
        lane = _lane(dksum.shape)
        dzk_ref[...] = jnp.where((lane >= KR_LANE) & (lane < KR_LANE + QK_ROPE),
                                 _rope_t(dksum, c, sa, sb), 0.0).astype(BF16)
        dkv = jnp.concatenate([dk_ref[h] for h in range(N_TOK_HEADS)]
                              + [dv_ref[h] for h in range(N_TOK_HEADS)], axis=1).astype(BF16)
        dcqn = _dot(dqp, wuqt_ref[...])
        dckvn = _dot(dkv, wukvt_ref[...])
        dwuq_ref[...] += _dot_tn(cqn, dqp)
        dwukv_ref[...] += _dot_tn(ckvn, dkv)
        dg_ref[0:1, 0:Q_LORA] += jnp.sum(dcqn * cq * rq, axis=0, keepdims=True)
        dg_ref[0:1, Q_LORA:Q_LORA + KV_LORA] += jnp.sum(dckvn * ckv * rkv, axis=0, keepdims=True)
        wq = dcqn * gq_
        wkv = dckvn * gkv_
        dcq = rq * wq - cq * (rq * rq * rq) * jnp.mean(wq * cq, axis=-1, keepdims=True)
        dckv = rkv * wkv - ckv * (rkv * rkv * rkv) * jnp.mean(wkv * ckv, axis=-1, keepdims=True)
        dza_ref[:, 0:Q_LORA] = dcq.astype(BF16)
        dza_ref[:, Q_LORA:Q_LORA + KV_LORA] = dckv.astype(BF16)

    na = Q_LORA + KV_LORA
    return pl.pallas_call(
        body, grid=(s // ts,),
        in_specs=[_rows(ts, ZP), _heads(ts), _heads(ts), _heads(ts),
                  _rows(ts, LANES), _rows(ts, LANES), _rows(ts, LANES),
                  _full((1, Q_LORA)), _full((1, KV_LORA)),
                  _full((QKV_PAD, Q_LORA)), _full((2 * QKV_PAD, KV_LORA))],
        out_specs=[_rows(ts, na), _rows(ts, LANES), _full((Q_LORA, QKV_PAD)),
                   _full((KV_LORA, 2 * QKV_PAD)), _full((SUBLANES, na))],
        out_shape=[jax.ShapeDtypeStruct((s, na), BF16), jax.ShapeDtypeStruct((s, LANES), BF16),
                   jax.ShapeDtypeStruct((Q_LORA, QKV_PAD), F32),
                   jax.ShapeDtypeStruct((KV_LORA, 2 * QKV_PAD), F32),
                   jax.ShapeDtypeStruct((SUBLANES, na), F32)],
        name="mla_prep_bwd", compiler_params=_cp(1))(
            z0, dq, dk, dv, *tabs, gq, gkv, wuq_t, wukv_t)


def _causal_pairs(nb, by_key):
    if by_key:
        pairs = [(i, j) for j in range(nb) for i in range(j, nb)]
    else:
        pairs = [(i, j) for i in range(nb) for j in range(i + 1)]
    return (jnp.array([p[0] for p in pairs], jnp.int32),
            jnp.array([p[1] for p in pairs], jnp.int32))


def _flash_fwd(q, k, v, t, nh):
    s = q.shape[1]
    itab, jtab = _causal_pairs(s // t, False)
    c2 = LOG2E / math.sqrt(QK_DIM)

    def body(it_ref, jt_ref, q_ref, k_ref, v_ref, o_ref, lse_ref, m_scr, acc_scr):
        pair = pl.program_id(1)
        i, j = it_ref[pair], jt_ref[pair]

        @pl.when(j == 0)
        def _():
            m_scr[...] = jnp.full_like(m_scr, NEG_BIG)
            acc_scr[...] = jnp.zeros_like(acc_scr)

        def softmax_strips(masked, hs, sc, row0):
            ps, als = [], []
            for r0 in range(0, sc.shape[0], STRIP):
                rows = slice(row0 + r0, row0 + r0 + STRIP)
                ch = [sc[r0:r0 + STRIP, n * LANES:(n + 1) * LANES] * c2
                      for n in range(sc.shape[1] // LANES)]
                if masked:
                    rr = row0 + r0 + lax.broadcasted_iota(jnp.int32, (STRIP, LANES), 0)
                    cc = lax.broadcasted_iota(jnp.int32, (STRIP, LANES), 1)
                    ch = [jnp.where(cc + n * LANES <= rr, c_, NEG_BIG) for n, c_ in enumerate(ch)]
                mx = ch[0]
                for c_ in ch[1:]:
                    mx = jnp.maximum(mx, c_)
                m_prev = m_scr[hs, rows, :]
                m_next = jnp.maximum(m_prev, jnp.max(mx, axis=-1, keepdims=True))
                ps.append(jnp.concatenate(
                    [jnp.exp2(c_ - m_next).astype(BF16) for c_ in ch], axis=1))
                als.append(jnp.exp2(m_prev - m_next))
                m_scr[hs, rows, :] = m_next
            return jnp.concatenate(ps, axis=0), jnp.concatenate(als, axis=0)

        def run(masked, parts):
            def scores_of(hs):
                return [_dot_nt(q_ref[hs, r0:r0 + nr, :], k_ref[hs, 0:nk, :])
                        for r0, nr, nk in parts]

            ahead = min(LOOKAHEAD, nh)
            scores = [scores_of(hs) for hs in range(ahead)]
            for hs in range(nh):
                if hs + ahead < nh:
                    scores.append(scores_of(hs + ahead))
                for (r0, nr, nk), sc in zip(parts, scores[hs]):
                    p, alpha = softmax_strips(masked, hs, sc, r0)
                    acc_scr[hs, r0:r0 + nr, :] = (alpha * acc_scr[hs, r0:r0 + nr, :]
                                                  + _dot(p, v_ref[hs, 0:nk, :]))

        @pl.when(j < i)
        def _():
            run(False, [(0, t, t)])

        @pl.when(j == i)
        def _():
            run(True, [(0, t, t)])
            for h in range(nh):
                acc = acc_scr[h]
                l = acc[:, HEAD_DIM:HEAD_DIM + 1]
                o_ref[h] = jnp.where(_lane(acc.shape) < HEAD_DIM, acc / l, 0.0)
                lse_ref[h] = m_scr[h] + jnp.log2(l)

    qspec = pl.BlockSpec((nh, t, HEAD_PAD), lambda h, p, it, jt: (h, it[p], 0))
    kspec = pl.BlockSpec((nh, t, HEAD_PAD), lambda h, p, it, jt: (h, jt[p], 0))
    out = jax.ShapeDtypeStruct((N_TOK_HEADS, s, HEAD_PAD), F32)
    return pl.pallas_call(
        body,
        grid_spec=pltpu.PrefetchScalarGridSpec(
            num_scalar_prefetch=2, grid=(N_TOK_HEADS // nh, itab.shape[0]),
            in_specs=[qspec, kspec, kspec], out_specs=[qspec, qspec],
            scratch_shapes=[pltpu.VMEM((nh, t, HEAD_PAD), F32)] * 2),
        out_shape=[out, out],
        name="flash_fwd", compiler_params=_cp(2))(itab, jtab, q, k, v)


def _flash_bwd(q, k, v, stats, do, t, nh, ex):
    s = q.shape[1]
    nb = s // t
    itab, jtab = _causal_pairs(nb, True)
    npairs = itab.shape[0]
    ngroups = N_TOK_HEADS // nh
    scale = 1.0 / math.sqrt(QK_DIM)
    c2 = LOG2E * scale
    nx = ex.n if ex is not None else 0
    ex_arrays, ex_out_shape, ex_scratch = (
        (ex.arrays, ex.out_shape, ex.scratch) if ex is not None else ([], [], []))

    def body(it_ref, jt_ref, q_ref, k_ref, v_ref, st_ref, do_ref, *rest):
        ex_in, rest = rest[:nx], rest[nx:]
        dq_ref, dk_ref, dv_ref = rest[:3]
        ex_out, rest = rest[3:3 + nx], rest[3 + nx:]
        dk_scr, dv_scr = rest[:2]
        ex_sems = rest[2:]
        pair = pl.program_id(1)
        i, j = it_ref[pair], jt_ref[pair]
        rows_i = pl.ds(pl.multiple_of(i * t, t), t)

        if nx:
            @pl.when(jnp.logical_and(pl.program_id(0) == 0, pair == 0))
            def _():
                for cp in ex.copies(ex_in, ex_out, ex_sems):
                    cp.start()

        @pl.when(i == j)
        def _():
            dk_scr[...] = jnp.zeros_like(dk_scr)
            dv_scr[...] = jnp.zeros_like(dv_scr)

        @pl.when(j == 0)
        def _():
            dq_ref[:, rows_i, :] = jnp.zeros((nh, t, HEAD_PAD), F32)

        def prob_strips(masked, h, sc, dp, row0):
            ps, dss = [], []
            low = _lane((STRIP, LANES)) < HEAD_DIM
            for r0 in range(0, sc.shape[0], STRIP):
                rows = slice(r0, r0 + STRIP)
                st = st_ref[h, row0 + r0:row0 + r0 + STRIP, :]
                swapped = pltpu.roll(st, HEAD_DIM, 1)
                lse = jnp.where(low, st, swapped)
                delta = jnp.where(low, swapped, st)
                if masked:
                    rr = row0 + r0 + lax.broadcasted_iota(jnp.int32, (STRIP, LANES), 0)
                    cc = lax.broadcasted_iota(jnp.int32, (STRIP, LANES), 1)
                pcs, dcs = [], []
                for n in range(sc.shape[1] // LANES):
                    cols = slice(n * LANES, (n + 1) * LANES)
                    x = sc[rows, cols] * c2
                    if masked:
                        x = jnp.where(cc + n * LANES <= rr, x, NEG_BIG)
                    p = jnp.exp2(x - lse)
                    pcs.append(p.astype(BF16))
                    dcs.append((p * (dp[rows, cols] - delta) * scale).astype(BF16))
                ps.append(jnp.concatenate(pcs, axis=1))
                dss.append(jnp.concatenate(dcs, axis=1))
            return jnp.concatenate(ps, axis=0), jnp.concatenate(dss, axis=0)

        def run(masked, parts):
            def scores_of(h):
                return [(_dot_nt(q_ref[h, r0:r0 + nr, :], k_ref[h, 0:nk, :]),
                         _dot_nt(do_ref[h, r0:r0 + nr, :], v_ref[h, 0:nk, :]))
                        for r0, nr, nk in parts]

            ahead = min(LOOKAHEAD, nh)
            scores = [scores_of(h) for h in range(ahead)]
            for h in range(nh):
                if h + ahead < nh:
                    scores.append(scores_of(h + ahead))
                for (r0, nr, nk), (sc, dp) in zip(parts, scores[h]):
                    p, ds = prob_strips(masked, h, sc, dp, r0)
                    dv_scr[h, 0:nk, :] += _dot_tn(p, do_ref[h, r0:r0 + nr, :])
                    dk_scr[h, 0:nk, :] += _dot_tn(ds, q_ref[h, r0:r0 + nr, :])
                    rows = pl.ds(pl.multiple_of(i * t + r0, t // 2), nr)
                    dq_ref[h, rows, :] += _dot(ds, k_ref[h, 0:nk, :])

        @pl.when(i > j)
        def _():
            run(False, [(0, t, t)])

        @pl.when(i == j)
        def _():
            run(True, [(0, t // 2, t // 2), (t // 2, t // 2, t)])

        @pl.when(i == nb - 1)
        def _():
            dk_ref[...] = dk_scr[...]
            dv_ref[...] = dv_scr[...]

        if nx:
            @pl.when(jnp.logical_and(pl.program_id(0) == ngroups - 1, pair == npairs - 1))
            def _():
                for cp in ex.copies(ex_in, ex_out, ex_sems):
                    cp.wait()

    qspec = pl.BlockSpec((nh, t, HEAD_PAD), lambda h, p, it, jt: (h, it[p], 0))
    kspec = pl.BlockSpec((nh, t, HEAD_PAD), lambda h, p, it, jt: (h, jt[p], 0))
    dqspec = pl.BlockSpec((nh, s, HEAD_PAD), lambda h, p, it, jt: (h, 0, 0))
    out = jax.ShapeDtypeStruct((N_TOK_HEADS, s, HEAD_PAD), F32)
    res = pl.pallas_call(
        body,
        grid_spec=pltpu.PrefetchScalarGridSpec(
            num_scalar_prefetch=2, grid=(ngroups, npairs),
            in_specs=[qspec, kspec, kspec, qspec, qspec] + [ANY] * nx,
            out_specs=[dqspec, kspec, kspec] + [ANY] * nx,
            scratch_shapes=[pltpu.VMEM((nh, t, HEAD_PAD), F32)] * 2 + ex_scratch),
        out_shape=[out, out, out] + ex_out_shape,
        name="flash_bwd", compiler_params=_cp(2))(itab, jtab, q, k, v, stats, do, *ex_arrays)
    return res[:3], res[3:]


def _mem_probs(qp, kp, hh):
    lane = _lane(qp.shape)
    keep = (lane < HEAD_DIM) if hh == 0 else (lane >= HEAD_DIM)
    qh = jnp.where(keep, qp, 0.0).astype(BF16)
    sc = _dot_nt(qh, kp) * (1.0 / math.sqrt(HEAD_DIM))
    e = jnp.exp(sc - jnp.max(sc, axis=-1, keepdims=True))
    return e / jnp.sum(e, axis=-1, keepdims=True), keep


def _gate_mem_fwd(tok, z, memkv, g0, q0, padded, name, ts):
    s = z.shape[0]
    zw = z.shape[1]
    tok_spec = _heads(ts) if padded else _rows(ts, TOK_WIDTH)

    def body(tok_ref, z_ref, mkv_ref, cat_ref, y_ref):
        if padded:
            for p in range(N_TOK_HEADS // 2):
                cat_ref[:, p * LANES:(p + 1) * LANES] = (
                    tok_ref[2 * p] + pltpu.roll(tok_ref[2 * p + 1], HEAD_DIM, 1))
        else:
            cat_ref[:, 0:TOK_WIDTH] = tok_ref[...]
        for pr in range(N_MEM_HEADS // 2):
            sl = slice(pr * LANES, (pr + 1) * LANES)
            qp = z_ref[:, q0 + pr * LANES:q0 + (pr + 1) * LANES]
            kp = mkv_ref[:, sl].astype(BF16)
            vp = mkv_ref[:, MEM_WIDTH + pr * LANES:MEM_WIDTH + (pr + 1) * LANES].astype(BF16)
            outs = []
            for hh in range(2):
                p, _ = _mem_probs(qp, kp, hh)
                outs.append(_dot(p.astype(BF16), vp))
            lane = _lane(outs[0].shape)
            cat_ref[:, TOK_WIDTH + pr * LANES:TOK_WIDTH + (pr + 1) * LANES] = jnp.where(
                lane < HEAD_DIM, outs[0], outs[1])
        gate = z_ref[:, g0:g0 + MIX_WIDTH]
        y_ref[...] = (cat_ref[...] * (gate * _sigmoid(gate))).astype(BF16)

    return pl.pallas_call(
        body, grid=(s // ts,),
        in_specs=[tok_spec, _rows(ts, zw), _full((MEM_LEN, 2 * MEM_WIDTH))],
        out_specs=[_rows(ts, MIX_WIDTH)] * 2,
        out_shape=[jax.ShapeDtypeStruct((s, MIX_WIDTH), F32),
                   jax.ShapeDtypeStruct((s, MIX_WIDTH), BF16)],
        name=name, compiler_params=_cp(1))(tok, z, memkv)


def _gate_mem_bwd(dy, cat, z, memkv, lse, g0, q0, name, ts):
    s = z.shape[0]
    zw = z.shape[1]
    padded = lse is not None
    gq_w = MIX_WIDTH + MEM_WIDTH

    def body(*refs):
        if padded:
            dy_ref, cat_ref, z_ref, mkv_ref, lse_ref, dzg_ref, dtok_ref, dmkv_ref, st_ref = refs
        else:
            dy_ref, cat_ref, z_ref, mkv_ref, dzg_ref, dtok_ref, dmkv_ref = refs

        @pl.when(pl.program_id(0) == 0)
        def _():
            dmkv_ref[...] = jnp.zeros_like(dmkv_ref)

        gate = z_ref[:, g0:g0 + MIX_WIDTH]
        sg = _sigmoid(gate)
        dy_ = dy_ref[...]
        dzg_ref[:, 0:MIX_WIDTH] = (dy_ * cat_ref[...]
                                   * (sg * (1.0 + gate * (1.0 - sg)))).astype(BF16)
        dcat = dy_ * (gate * sg)
        if padded:
            low = _lane((ts, LANES)) < HEAD_DIM
            for p in range(N_TOK_HEADS // 2):
                d = dcat[:, p * LANES:(p + 1) * LANES]
                prod = d * cat_ref[:, p * LANES:(p + 1) * LANES]
                first = jnp.sum(jnp.where(low, prod, 0.0), axis=-1, keepdims=True)
                second = jnp.sum(jnp.where(low, 0.0, prod), axis=-1, keepdims=True)
                dtok_ref[2 * p] = jnp.where(low, d, 0.0).astype(BF16)
                dtok_ref[2 * p + 1] = jnp.where(low, pltpu.roll(d, HEAD_DIM, 1), 0.0).astype(BF16)
                st_ref[2 * p] = jnp.where(low, lse_ref[2 * p], first)
                st_ref[2 * p + 1] = jnp.where(low, lse_ref[2 * p + 1], second)
        else:
            dtok_ref[...] = dcat[:, 0:TOK_WIDTH]
        for pr in range(N_MEM_HEADS // 2):
            sl = slice(pr * LANES, (pr + 1) * LANES)
            vsl = slice(MEM_WIDTH + pr * LANES, MEM_WIDTH + (pr + 1) * LANES)
            qp = z_ref[:, q0 + pr * LANES:q0 + (pr + 1) * LANES]
            qpb = qp.astype(BF16)
            kp = mkv_ref[:, sl].astype(BF16)
            vp = mkv_ref[:, vsl].astype(BF16)
            dmo = dcat[:, TOK_WIDTH + pr * LANES:TOK_WIDTH + (pr + 1) * LANES]
            dqp = None
            for hh in range(2):
                p, keep = _mem_probs(qp, kp, hh)
                do_h = jnp.where(keep, dmo, 0.0).astype(BF16)
                dmkv_ref[:, vsl] += _dot_tn(p.astype(BF16), do_h)
                dp = _dot_nt(do_h, vp)
                ds = (p * (dp - jnp.sum(dp * p, axis=-1, keepdims=True))
                      * (1.0 / math.sqrt(HEAD_DIM))).astype(BF16)
                dqh = jnp.where(keep, _dot(ds, kp), 0.0)
                dqp = dqh if dqp is None else dqp + dqh
                dkh = _dot_tn(ds, qpb)
                klane = _lane(dkh.shape)
                kkeep = (klane < HEAD_DIM) if hh == 0 else (klane >= HEAD_DIM)
                dmkv_ref[:, sl] += jnp.where(kkeep, dkh, 0.0)
            dzg_ref[:, MIX_WIDTH + pr * LANES:MIX_WIDTH + (pr + 1) * LANES] = dqp.astype(BF16)

    in_specs = [_rows(ts, MIX_WIDTH), _rows(ts, MIX_WIDTH), _rows(ts, zw),
                _full((MEM_LEN, 2 * MEM_WIDTH))]
    out_specs = [_rows(ts, gq_w), _heads(ts) if padded else _rows(ts, TOK_WIDTH),
                 _full((MEM_LEN, 2 * MEM_WIDTH))]
    heads_shape = (N_TOK_HEADS, s, HEAD_PAD)
    out_shape = [jax.ShapeDtypeStruct((s, gq_w), BF16),
                 jax.ShapeDtypeStruct(heads_shape, BF16) if padded
                 else jax.ShapeDtypeStruct((s, TOK_WIDTH), F32),
                 jax.ShapeDtypeStruct((MEM_LEN, 2 * MEM_WIDTH), F32)]
    args = [dy, cat, z, memkv]
    if padded:
        in_specs.append(_heads(ts))
        out_specs.append(_heads(ts))
        out_shape.append(jax.ShapeDtypeStruct(heads_shape, F32))
        args.append(lse)
    return pl.pallas_call(
        body, grid=(s // ts,), in_specs=in_specs, out_specs=out_specs, out_shape=out_shape,
        name=name, compiler_params=_cp(1))(*args)


def _ln_stats(pre):
    mu = jnp.mean(pre, axis=-1, keepdims=True)
    d = pre - mu
    rstd = lax.rsqrt(jnp.mean(d * d, axis=-1, keepdims=True) + NORM_EPS)
    return d * rstd, rstd


def _outproj_ln_fwd(y, w, h, g, b, tgt, name, ts):
    s = y.shape[0]
    with_loss = tgt is not None

    def body(*refs):
        if with_loss:
            y_ref, w_ref, h_ref, g_ref, b_ref, t_ref, pre_ref, out_ref, loss_ref = refs
        else:
            y_ref, w_ref, h_ref, g_ref, b_ref, pre_ref, out_ref = refs
        pre = ALPHA * h_ref[...] + _dot(y_ref[...].astype(BF16), w_ref[...])
        pre_ref[...] = pre
        xhat, _ = _ln_stats(pre)
        hout = xhat * g_ref[...] + b_ref[...]
        if with_loss:
            @pl.when(pl.program_id(0) == 0)
            def _():
                loss_ref[...] = jnp.zeros_like(loss_ref)
            err = hout - t_ref[...]
            out_ref[...] = err * (1.0 / D_MODEL)
            loss_ref[...] += 0.5 * jnp.sum(jnp.mean(err * err, axis=-1, keepdims=True))
        else:
            out_ref[...] = hout

    act = jax.ShapeDtypeStruct((s, D_MODEL), F32)
    in_specs = [_rows(ts, MIX_WIDTH), _full((MIX_WIDTH, D_MODEL)), _rows(ts, D_MODEL),
                _full((1, D_MODEL)), _full((1, D_MODEL))]
    out_specs = [_rows(ts, D_MODEL)] * 2
    out_shape = [act, act]
    args = [y, w, h, g, b]
    if with_loss:
        in_specs.append(_rows(ts, D_MODEL))
        out_specs.append(_full((SUBLANES, LANES)))
        out_shape.append(jax.ShapeDtypeStruct((SUBLANES, LANES), F32))
        args.append(tgt)
    return pl.pallas_call(
        body, grid=(s // ts,), in_specs=in_specs, out_specs=out_specs, out_shape=out_shape,
        name=name, compiler_params=_cp(1))(*args)


def _outproj_ln_bwd(dh, pre, g, y, w_t, name, ts):
    s = y.shape[0]

    def body(dh_ref, pre_ref, g_ref, y_ref, wt_ref, dpre_ref, dy_ref, dw_ref, dgb_ref):
        @pl.when(pl.program_id(0) == 0)
        def _():
            dw_ref[...] = jnp.zeros_like(dw_ref)
            dgb_ref[...] = jnp.zeros_like(dgb_ref)

        dh_ = dh_ref[...]
        xhat, rstd = _ln_stats(pre_ref[...])
        dxh = dh_ * g_ref[...]
        dpre = rstd * (dxh - jnp.mean(dxh, axis=-1, keepdims=True)
                       - xhat * jnp.mean(dxh * xhat, axis=-1, keepdims=True))
        dpre_ref[...] = dpre
        dgb_ref[0:1, :] += jnp.sum(dh_ * xhat, axis=0, keepdims=True)
        dgb_ref[1:2, :] += jnp.sum(dh_, axis=0, keepdims=True)
        dpb = dpre.astype(BF16)
        dy_ref[...] = _dot(dpb, wt_ref[...])
        dw_ref[...] += _dot_tn(y_ref[...].astype(BF16), dpb)

    act = jax.ShapeDtypeStruct((s, D_MODEL), F32)
    return pl.pallas_call(
        body, grid=(s // ts,),
        in_specs=[_rows(ts, D_MODEL), _rows(ts, D_MODEL), _full((1, D_MODEL)),
                  _rows(ts, MIX_WIDTH), _full((D_MODEL, MIX_WIDTH))],
        out_specs=[_rows(ts, D_MODEL), _rows(ts, MIX_WIDTH), _full((MIX_WIDTH, D_MODEL)),
                   _full((SUBLANES, D_MODEL))],
        out_shape=[act, act, jax.ShapeDtypeStruct((MIX_WIDTH, D_MODEL), F32),
                   jax.ShapeDtypeStruct((SUBLANES, D_MODEL), F32)],
        name=name, compiler_params=_cp(1))(dh, pre, g, y, w_t)


def _linear_bwd(x, dys, offs, w_t, resid, name, ts):
    s, kdim = x.shape
    n = w_t.shape[0]
    widths = [d.shape[1] for d in dys]
    npieces = len(dys)

    def body(*refs):
        x_ref = refs[0]
        dy_refs = refs[1:1 + npieces]
        wt_ref, r_ref, dx_ref, dw_ref = refs[1 + npieces:]

        @pl.when(pl.program_id(0) == 0)
        def _():
            dw_ref[...] = jnp.zeros_like(dw_ref)

        xb = x_ref[...].astype(BF16)
        dx = ALPHA * r_ref[...]
        for dy_ref, off, wd in zip(dy_refs, offs, widths):
            dyb = dy_ref[...].astype(BF16)
            dx = dx + _dot(dyb, wt_ref[off:off + wd, :])
            dw_ref[:, off:off + wd] += _dot_tn(xb, dyb)
        dx_ref[...] = dx

    return pl.pallas_call(
        body, grid=(s // ts,),
        in_specs=[_rows(ts, kdim)] + [_rows(ts, wd) for wd in widths]
                 + [_full((n, kdim)), _rows(ts, kdim)],
        out_specs=[_rows(ts, kdim), _full((kdim, n))],
        out_shape=[jax.ShapeDtypeStruct((s, kdim), F32), jax.ShapeDtypeStruct((kdim, n), F32)],
        name=name, compiler_params=_cp(1))(x, *dys, w_t, resid)


def _wgrad_small(x, dy, name):
    def body(x_ref, dy_ref, dw_ref):
        dw_ref[...] = _dot_tn(x_ref[...].astype(BF16), dy_ref[...].astype(BF16))

    return pl.pallas_call(
        body, out_shape=jax.ShapeDtypeStruct((x.shape[1], dy.shape[1]), F32),
        name=name, compiler_params=pltpu.CompilerParams(vmem_limit_bytes=VMEM_LIMIT))(x, dy)


def _shift_down(u, carry8, k):
    if k == 0:
        return u
    rolled = pltpu.roll(u, k, 0)
    row = lax.broadcasted_iota(jnp.int32, carry8.shape, 0)
    top = jnp.where(row < k, pltpu.roll(carry8, k, 0), rolled[0:SUBLANES])
    return jnp.concatenate([top, rolled[SUBLANES:]], axis=0)


def _shift_up(u, carry8, k):
    if k == 0:
        return u
    n = u.shape[0]
    rolled = pltpu.roll(u, n - k, 0)
    row = lax.broadcasted_iota(jnp.int32, carry8.shape, 0)
    bot = jnp.where(row >= SUBLANES - k, pltpu.roll(carry8, SUBLANES - k, 0),
                    rolled[n - SUBLANES:])
    return jnp.concatenate([rolled[:n - SUBLANES], bot], axis=0)


def _neg_expm1(t):
    e = jnp.exp(t)
    em1 = e - 1.0
    safe = jnp.where(e == 1.0, 1.0, jnp.log(e))
    return -jnp.where(e == 1.0, t, jnp.where(em1 == -1.0, -1.0, em1 * t / safe))


def _lru_gates(u, carry8, cw_ref, vec_ref, wr_ref, wi_ref):
    xc = vec_ref[0:1, :] + cw_ref[3:4, :] * u
    for k in range(1, CONV_W):
        xc = xc + cw_ref[3 - k:4 - k, :] * _shift_down(u, carry8, k)
    xb = xc.astype(BF16)
    r = _sigmoid(_dot(xb, wr_ref[...]) + vec_ref[1:2, :])
    ig = _sigmoid(_dot(xb, wi_ref[...]) + vec_ref[2:3, :])
    nlam = -vec_ref[3:4, :]
    softplus = jnp.maximum(nlam, 0.0) + jnp.log(1.0 + jnp.exp(-jnp.abs(nlam)))
    cneg = -LRU_C * softplus
    log_a = cneg * r
    a = jnp.exp(log_a)
    sq = jnp.sqrt(_neg_expm1(2.0 * log_a))
    return xc, r, ig, cneg, a, sq


def _lru_fwd(z1, cw8, vec8, wr, wi, ts):
    s = z1.shape[0]

    def body(u_ref, cw_ref, vec_ref, wr_ref, wi_ref, hs_ref, cu_scr, ch_scr, a_scr, gx_scr):
        @pl.when(pl.program_id(0) == 0)
        def _():
            cu_scr[...] = jnp.zeros_like(cu_scr)
            ch_scr[...] = jnp.zeros_like(ch_scr)

        u = u_ref[...]
        xc, _, ig, _, a, sq = _lru_gates(u, cu_scr[...], cw_ref, vec_ref, wr_ref, wi_ref)
        a_scr[...] = a
        gx_scr[...] = sq * (ig * xc)

        def step(t, h):
            h = a_scr[pl.ds(t, 1), :] * h + gx_scr[pl.ds(t, 1), :]
            hs_ref[pl.ds(t, 1), :] = h
            return h

        ch_scr[0:1, :] = lax.fori_loop(0, ts, step, ch_scr[0:1, :])
        cu_scr[...] = u[ts - SUBLANES:, :]

    w = TOK_WIDTH
    return pl.pallas_call(
        body, grid=(s // ts,),
        in_specs=[_rows(ts, w), _full((SUBLANES, w)), _full((SUBLANES, w)),
                  _full((w, w)), _full((w, w))],
        out_specs=_rows(ts, w),
        out_shape=jax.ShapeDtypeStruct((s, w), F32),
        scratch_shapes=[pltpu.VMEM((SUBLANES, w), F32), pltpu.VMEM((SUBLANES, w), F32),
                        pltpu.VMEM((ts, w), F32), pltpu.VMEM((ts, w), F32)],
        name="lru_fwd", compiler_params=_cp(1))(z1, cw8, vec8, wr, wi)


def _lru_bwd(z1, dhs, hs, cw8, vec8, wr, wi, wr_t, wi_t, ts):
    s = z1.shape[0]
    nb = s // ts
    w = TOK_WIDTH
    tiles = ts // SUBLANES

    def body(u_ref, up_ref, dhs_ref, hs_ref, hsp_ref, cw_ref, vec_ref, wr_ref, wi_ref,
             wrt_ref, wit_ref, du_ref, dwr_ref, dwi_ref, dvec_ref,
             cc_scr, cd_scr, a_scr, dh_scr):
        i = pl.program_id(0)

        @pl.when(i == 0)
        def _():
            cc_scr[...] = jnp.zeros_like(cc_scr)
            cd_scr[...] = jnp.zeros_like(cd_scr)
            dwr_ref[...] = jnp.zeros_like(dwr_ref)
            dwi_ref[...] = jnp.zeros_like(dwi_ref)
            dvec_ref[...] = jnp.zeros_like(dvec_ref)

        u = u_ref[...]
        first = i == nb - 1
        carry8 = jnp.where(first, 0.0, up_ref[...])
        xc, r, ig, cneg, a, sq = _lru_gates(u, carry8, cw_ref, vec_ref, wr_ref, wi_ref)
        a_scr[...] = a

        def step(n, c):
            t = ts - 1 - n
            dh = dhs_ref[pl.ds(t, 1), :] + c
            dh_scr[pl.ds(t, 1), :] = dh
            return a_scr[pl.ds(t, 1), :] * dh

        cc_scr[0:1, :] = lax.fori_loop(0, ts, step, cc_scr[0:1, :])
        dh = dh_scr[...]
        hprev = _shift_down(hs_ref[...], jnp.where(first, 0.0, hsp_ref[...]), 1)
        ix = ig * xc
        dix = dh * sq
        dlog_a = dh * hprev * a - (dh * ix) * (a * a) / sq
        dpr = (dlog_a * cneg) * r * (1.0 - r)
        dpi = (dix * xc) * ig * (1.0 - ig)
        dprb, dpib = dpr.astype(BF16), dpi.astype(BF16)
        xb = xc.astype(BF16)
        dwr_ref[...] += _dot_tn(xb, dprb)
        dwi_ref[...] += _dot_tn(xb, dpib)
        dxc = dix * ig + _dot(dprb, wrt_ref[...]) + _dot(dpib, wit_ref[...])
        for k in range(CONV_W):
            dvec_ref[3 - k:4 - k, :] += jnp.sum(dxc * _shift_down(u, carry8, k),
                                                axis=0, keepdims=True)
        dvec_ref[4:5, :] += jnp.sum(dxc, axis=0, keepdims=True)
        dvec_ref[5:6, :] += jnp.sum(dpr, axis=0, keepdims=True)
        dvec_ref[6:7, :] += jnp.sum(dpi, axis=0, keepdims=True)
        dvec_ref[7:8, :] += (jnp.sum(dlog_a * r, axis=0, keepdims=True)
                             * (LRU_C * _sigmoid(-vec_ref[3:4, :])))
        nxt = cd_scr[...]
        du = cw_ref[3:4, :] * dxc
        for k in range(1, CONV_W):
            du = du + cw_ref[3 - k:4 - k, :] * _shift_up(dxc, nxt, k)
        du_ref[...] = du.astype(BF16)
        cd_scr[...] = dxc[0:SUBLANES, :]

    rev = lambda i: (nb - 1 - i, 0)
    prev8 = lambda i: (jnp.maximum((nb - 1 - i) * tiles - 1, 0), 0)
    blk = pl.BlockSpec((ts, w), rev)
    before = pl.BlockSpec((SUBLANES, w), prev8)
    scr = pltpu.VMEM((ts, w), F32)
    return pl.pallas_call(
        body, grid=(nb,),
        in_specs=[blk, before, blk, blk, before,
                  _full((SUBLANES, w)), _full((SUBLANES, w)),
                  _full((w, w)), _full((w, w)), _full((w, w)), _full((w, w))],
        out_specs=[blk, _full((w, w)), _full((w, w)), _full((SUBLANES, w))],
        out_shape=[jax.ShapeDtypeStruct((s, w), BF16), jax.ShapeDtypeStruct((w, w), F32),
                   jax.ShapeDtypeStruct((w, w), F32), jax.ShapeDtypeStruct((SUBLANES, w), F32)],
        scratch_shapes=[pltpu.VMEM((SUBLANES, w), F32), pltpu.VMEM((SUBLANES, w), F32),
                        scr, scr],
        name="lru_bwd", compiler_params=_cp(1))(
            z1, z1, dhs, hs, hs, cw8, vec8, wr, wi, wr_t, wi_t)


def _adamw(parts, w, m, v, name):
    n = len(parts)
    rows_per = parts[0].shape[1]

    def body(*refs):
        p_refs = refs[:n]
        w_ref, m_ref, v_ref, g_ref, d_ref, nm_ref, nv_ref = refs[n:]
        for l, p_ref in enumerate(p_refs):
            rows = slice(l * rows_per, (l + 1) * rows_per)
            g = p_ref[0].astype(F32)
            for dev in range(1, N_DEV):
                g = g + p_ref[dev].astype(F32)
            g_ref[rows, :] = g
            nm = ADAM_B1 * m_ref[rows, :] + (1.0 - ADAM_B1) * g
            nv = ADAM_B2 * v_ref[rows, :] + (1.0 - ADAM_B2) * (g * g)
            m_hat = nm / (1.0 - ADAM_B1 ** ADAM_STEP)
            v_hat = nv / (1.0 - ADAM_B2 ** ADAM_STEP)
            d_ref[rows, :] = -ADAM_LR * (m_hat / (jnp.sqrt(v_hat) + ADAM_EPS)
                                         + ADAM_WD * w_ref[rows, :])
            nm_ref[rows, :] = nm
            nv_ref[rows, :] = nv

    out = jax.ShapeDtypeStruct(w.shape, F32)
    return pl.pallas_call(
        body, out_shape=[out] * 4, name=name,
        compiler_params=pltpu.CompilerParams(vmem_limit_bytes=VMEM_LIMIT))(*parts, w, m, v)


ANY = pl.BlockSpec(memory_space=pl.ANY)
MESH = pl.DeviceIdType.MESH


def _slot(p):
    return 4 * p[0] + 2 * p[1] + p[2]


def _allgather(xs):
    n = len(xs)

    def body(*refs):
        x_refs, o_refs = refs[:n], refs[n:2 * n]
        send_sems, recv_sems, local_sems = refs[2 * n:]
        x, y, c = lax.axis_index("x"), lax.axis_index("y"), lax.axis_index("c")
        me, sibling = (x, y, c), (x, y, 1 - c)
        chips = [(1 - x, y), (x, 1 - y), (1 - x, 1 - y)]

        def copy(a, k, block, to, from_input=False):
            dst = o_refs[a].at[_slot(block)]
            return pltpu.make_async_remote_copy(
                src_ref=x_refs[a] if from_input else dst, dst_ref=dst,
                send_sem=send_sems.at[a, k], recv_sem=recv_sems.at[a, k],
                device_id=to, device_id_type=MESH)

        mine = [pltpu.make_async_copy(x_refs[a], o_refs[a].at[_slot(me)], local_sems.at[a])
                for a in range(n)]
        for cp in mine:
            cp.start()
        first = []
        for a in range(n):
            first.append(copy(a, 0, me, sibling, True))
            first += [copy(a, 1 + j, me, (*chip, c), True) for j, chip in enumerate(chips)]
        for cp in first:
            cp.start()
        passed = []
        for j, chip in enumerate(chips):
            for a in range(n):
                copy(a, 1 + j, (*chip, c), me).wait_recv()
                cp = copy(a, 4 + j, (*chip, c), sibling)
                cp.start()
                passed.append(cp)
        for a in range(n):
            copy(a, 0, sibling, me).wait_recv()
            for j, chip in enumerate(chips):
                copy(a, 4 + j, (*chip, 1 - c), me).wait_recv()
        for cp in first + passed:
            cp.wait_send()
        for cp in mine:
            cp.wait()

    return pl.pallas_call(
        body,
        out_shape=[jax.ShapeDtypeStruct((N_DEV,) + t.shape, t.dtype) for t in xs],
        in_specs=[ANY] * n, out_specs=[ANY] * n,
        scratch_shapes=[pltpu.SemaphoreType.DMA((n, 7)), pltpu.SemaphoreType.DMA((n, 7)),
                        pltpu.SemaphoreType.DMA((n,))],
        name="allgather_weights")(*xs)


class _Exchange:
    def __init__(self, arrays, kinds):
        self.arrays, self.kinds, self.n = list(arrays), list(kinds), len(arrays)
        self.shapes = [self._part_shape(a, k) for a, k in zip(arrays, kinds)]
        self.out_shape = [jax.ShapeDtypeStruct((N_DEV,) + shp, a.dtype)
                          for shp, a in zip(self.shapes, arrays)]
        self.scratch = [pltpu.SemaphoreType.DMA((self.n, N_DEV - 1)),
                        pltpu.SemaphoreType.DMA((self.n, N_DEV - 1)),
                        pltpu.SemaphoreType.DMA((self.n,))]

    @staticmethod
    def _part_shape(arr, kind):
        if kind == "chunks":
            return arr.shape[1:]
        if kind == "cols":
            return (arr.shape[0], arr.shape[1] // N_DEV)
        if kind == "rows":
            return (arr.shape[0] // N_DEV, arr.shape[1])
        return arr.shape

    def copies(self, in_refs, out_refs, sems):
        send_sems, recv_sems, local_sems = sems
        x, y, c = lax.axis_index("x"), lax.axis_index("y"), lax.axis_index("c")
        me = _slot((x, y, c))

        def part(a, dev):
            ref, kind, shp = in_refs[a], self.kinds[a], self.shapes[a]
            if kind == "chunks":
                return ref.at[dev]
            if kind == "cols":
                return ref.at[:, pl.ds(pl.multiple_of(dev * shp[1], LANES), shp[1])]
            if kind == "rows":
                return ref.at[pl.ds(pl.multiple_of(dev * shp[0], SUBLANES), shp[0]), :]
            return ref

        cps = [pltpu.make_async_copy(part(a, me), out_refs[a].at[me], local_sems.at[a])
               for a in range(self.n)]
        for rel in range(1, N_DEV):
            peer = (x ^ (rel >> 2), y ^ ((rel >> 1) & 1), c ^ (rel & 1))
            for a in range(self.n):
                cps.append(pltpu.make_async_remote_copy(
                    src_ref=part(a, _slot(peer)), dst_ref=out_refs[a].at[me],
                    send_sem=send_sems.at[a, rel - 1], recv_sem=recv_sems.at[a, rel - 1],
                    device_id=peer, device_id_type=MESH))
        return cps


def _exchange_grads(arrays, kinds, name):
    ex = _Exchange(arrays, kinds)
    n = ex.n

    def body(*refs):
        cps = ex.copies(refs[:n], refs[n:2 * n], refs[2 * n:])
        for cp in cps:
            cp.start()
        for cp in cps:
            cp.wait()

    return pl.pallas_call(
        body, out_shape=ex.out_shape, in_specs=[ANY] * n, out_specs=[ANY] * n,
        scratch_shapes=ex.scratch, name=name)(*arrays)


BIG = [("mla_w_in", (D_MODEL, MLA_IN), 1), ("mla_w_uq", (Q_LORA, N_TOK_HEADS * QK_DIM), 1),
       ("mla_w_ukv", (KV_LORA, N_TOK_HEADS * 2 * HEAD_DIM), 1), ("lru_w_in", (D_MODEL, LRU_IN), 1),
       ("w_mem_kv", (2, D_MODEL, 2 * MEM_WIDTH), 1), ("w_out", (2, MIX_WIDTH, D_MODEL), 1)]
SMALL = [("lru_conv_w", (CONV_W, TOK_WIDTH), 1), ("lru_conv_b", (TOK_WIDTH,), 0),
         ("lru_b_rgate", (TOK_WIDTH,), 0), ("lru_b_igate", (TOK_WIDTH,), 0),
         ("lru_lambda", (TOK_WIDTH,), 0)]
REPL = [("mla_q_norm", (Q_LORA,)), ("mla_kv_norm", (KV_LORA,)),
        ("lru_w_rgate", (N_TOK_HEADS, HEAD_DIM, HEAD_DIM)),
        ("lru_w_igate", (N_TOK_HEADS, HEAD_DIM, HEAD_DIM)),
        ("ln_g", (2, D_MODEL)), ("ln_b", (2, D_MODEL))]


def _shard_shape(shape, axis):
    return tuple(d // N_DEV if a == axis else d for a, d in enumerate(shape))


def _size(shape):
    return math.prod(shape)


BIG_ROWS = sum(_size(s) for _, s, _ in BIG) // N_DEV // LANES
SMALL_ROWS = SUBLANES


def _pack_rows(flat_parts, rows):
    flat = jnp.concatenate([p.reshape(-1) for p in flat_parts])
    return jnp.pad(flat, (0, rows * LANES - flat.shape[0])).reshape(rows, LANES)


def _to_chunks(full, axis):
    shape = full.shape
    split = shape[:axis] + (N_DEV, shape[axis] // N_DEV) + shape[axis + 1:]
    return jnp.moveaxis(full.reshape(split), axis, 0).reshape(N_DEV, -1)


def _from_chunks(chunks, shape, axis):
    sh = _shard_shape(shape, axis)
    t = chunks.reshape((N_DEV,) + sh)
    t = jnp.moveaxis(t, 0, axis)
    return t.reshape(shape)


def _split_flat(flat2d, table):
    out, off = [], 0
    for size in table:
        out.append(flat2d[:, off:off + size])
        off += size
    return out


def _win0_to_padded(w):
    z = lambda n: jnp.zeros((w.shape[0], n), w.dtype)
    return jnp.concatenate([w[:, 0:640], w[:, 672:1952], z(KR_LANE), w[:, 640:672],
                            z(LANES - KR_LANE - QK_ROPE)], axis=1)


def _win0_from_padded(wp):
    k0 = Z0_KR + KR_LANE
    return jnp.concatenate([wp[:, 0:640], wp[:, k0:k0 + QK_ROPE], wp[:, 640:1920]], axis=1)


def _pad_heads(w, per_head, lo, hi):
    t = w.reshape(w.shape[0], N_TOK_HEADS, per_head)[:, :, lo:hi]
    t = jnp.pad(t, ((0, 0), (0, 0), (0, HEAD_PAD - (hi - lo))))
    return t.reshape(w.shape[0], QKV_PAD)


def _unpad_heads(wp, width):
    return wp.reshape(wp.shape[0], N_TOK_HEADS, HEAD_PAD)[:, :, :width]


def _block_diag(w):
    eye = jnp.eye(N_TOK_HEADS, dtype=w.dtype)
    return (w[:, :, None, :] * eye[:, None, :, None]).reshape(TOK_WIDTH, TOK_WIDTH)


def _diag_blocks(d):
    t = d.reshape(N_TOK_HEADS, HEAD_DIM, N_TOK_HEADS, HEAD_DIM)
    return jnp.stack([t[g, :, g, :] for g in range(N_TOK_HEADS)])


def _rope_tables(positions):
    half = QK_ROPE // 2
    inv_freq = ROPE_THETA ** (-jnp.arange(half, dtype=F32) / half)
    ang = positions.astype(F32)[:, None] * inv_freq
    cos, sin = jnp.cos(ang), jnp.sin(ang)
    s = positions.shape[0]
    one, zero = jnp.ones((s, QK_NOPE), F32), jnp.zeros((s, half), F32)
    tail = jnp.zeros((s, HEAD_PAD - QK_DIM), F32)
    znope = jnp.zeros((s, QK_NOPE), F32)
    c = jnp.concatenate([one, cos, cos, tail], axis=1)
    sa = jnp.concatenate([znope, -sin, zero, tail], axis=1)
    sb = jnp.concatenate([znope, zero, sin, tail], axis=1)
    return c, sa, sb


def _local_step(x, mem, positions, tgt, wts, ts, tatt, early_exchange):
    bf = lambda t: t.astype(BF16)
    win0 = _win0_to_padded(wts["mla_w_in"])
    wuq = _pad_heads(wts["mla_w_uq"], QK_DIM, 0, QK_DIM)
    wukv = jnp.concatenate([_pad_heads(wts["mla_w_ukv"], 2 * HEAD_DIM, 0, QK_NOPE),
                            _pad_heads(wts["mla_w_ukv"], 2 * HEAD_DIM, QK_NOPE, 2 * HEAD_DIM)],
                           axis=1)
    win1 = wts["lru_w_in"]
    wmkv, wout = wts["w_mem_kv"], wts["w_out"]
    gq = wts["mla_q_norm"].reshape(1, Q_LORA)
    gkv = wts["mla_kv_norm"].reshape(1, KV_LORA)
    ln_g, ln_b = wts["ln_g"], wts["ln_b"]
    wr, wi = bf(_block_diag(wts["lru_w_rgate"])), bf(_block_diag(wts["lru_w_igate"]))
    cw8 = jnp.pad(wts["lru_conv_w"], ((0, SUBLANES - CONV_W), (0, 0)))
    vec8 = jnp.pad(jnp.stack([wts["lru_conv_b"], wts["lru_b_rgate"], wts["lru_b_igate"],
                              wts["lru_lambda"]]), ((0, SUBLANES - 4), (0, 0)))
    tabs = _rope_tables(positions)
    tmem = mem.shape[0]

    z0 = _rowmm(x, win0, "in_proj0", ts)
    q, k, v = _mla_prep_fwd(z0, tabs, gq, gkv, wuq, wukv, ts)
    o, lse = _flash_fwd(q, k, v, tatt, FWD_HEADS)
    mkv0 = _rowmm(mem, wmkv[0], "mem_kv0", tmem)
    cat0, y0 = _gate_mem_fwd(o, z0, mkv0, Z0_GATE, Z0_QMEM, True, "gate_mem_fwd0", ts)
    del o
    pre0, h1 = _outproj_ln_fwd(y0, wout[0], x, ln_g[0:1], ln_b[0:1], None, "outproj_ln_fwd0", ts)
    z1 = _rowmm(h1, win1, "in_proj1", ts)
    hs = _lru_fwd(z1, cw8, vec8, wr, wi, ts)
    mkv1 = _rowmm(mem, wmkv[1], "mem_kv1", tmem)
    cat1, y1 = _gate_mem_fwd(hs, z1, mkv1, Z1_GATE, Z1_QMEM, False, "gate_mem_fwd1", ts)
    pre1, dh2, loss8 = _outproj_ln_fwd(y1, wout[1], h1, ln_g[1:2], ln_b[1:2], tgt,
                                       "outproj_ln_loss", ts)
    loss = loss8[0, 0]

    dpre1, dy1, dwout1, dgb1 = _outproj_ln_bwd(dh2, pre1, ln_g[1:2], y1, wout[1].T,
                                               "outproj_ln_bwd1", ts)
    dzg1, dhs, dmkv1 = _gate_mem_bwd(dy1, cat1, z1, mkv1, None, Z1_GATE, Z1_QMEM,
                                     "gate_mem_bwd1", ts)
    du, dwr, dwi, dvec = _lru_bwd(z1, dhs, hs, cw8, vec8, wr, wi, wr.T, wi.T, ts)
    dh1, dwin1 = _linear_bwd(h1, [du, dzg1], [Z1_U, Z1_GATE], win1.T, dpre1, "in_proj_bwd1", ts)
    dwmkv1 = _wgrad_small(mem, dmkv1, "mem_kv_bwd1")
    dpre0, dy0, dwout0, dgb0 = _outproj_ln_bwd(dh1, pre0, ln_g[0:1], y0, wout[0].T,
                                               "outproj_ln_bwd0", ts)
    dzg0, do, dmkv0, stats = _gate_mem_bwd(dy0, cat0, z0, mkv0, lse, Z0_GATE, Z0_QMEM,
                                           "gate_mem_bwd0", ts)
    dwmkv0 = _wgrad_small(mem, dmkv0, "mem_kv_bwd0")
    early = {
        "lru_w_in": dwin1,
        "lru_small": dvec,
        "lru_w_rgate": _diag_blocks(dwr).reshape(TOK_WIDTH, HEAD_DIM),
        "lru_w_igate": _diag_blocks(dwi).reshape(TOK_WIDTH, HEAD_DIM),
        "w_mem_kv": [dwmkv0, dwmkv1],
        "w_out": [dwout0, dwout1],
    }
    (dq, dk, dv), got_early = _flash_bwd(q, k, v, stats, do, tatt, BWD_HEADS,
                                         early_exchange(early))
    dza, dzk, dwuq_p, dwukv_p, dg = _mla_prep_bwd(z0, dq, dk, dv, tabs, gq, gkv,
                                                  wuq.T, wukv.T, ts)
    gx, dwin0_p = _linear_bwd(x, [dza, dzg0, dzk], [Z0_CQ, Z0_GATE, Z0_KR], win0.T, dpre0,
                              "in_proj_bwd0", ts)

    dwukv = jnp.concatenate([_unpad_heads(dwukv_p[:, :QKV_PAD], HEAD_DIM),
                             _unpad_heads(dwukv_p[:, QKV_PAD:], HEAD_DIM)], axis=2)
    zrow = jnp.zeros((1, D_MODEL), F32)
    gains = jnp.pad(dg[0:1], ((0, 0), (0, D_MODEL - Q_LORA - KV_LORA)))
    small_repl = jnp.concatenate([dgb0[0:2], dgb1[0:2], gains,
                                  loss * jnp.ones((1, D_MODEL), F32), zrow, zrow], axis=0)
    late = {
        "mla_w_in": _win0_from_padded(dwin0_p),
        "mla_w_uq": _unpad_heads(dwuq_p, QK_DIM).reshape(Q_LORA, N_TOK_HEADS * QK_DIM),
        "mla_w_ukv": dwukv.reshape(KV_LORA, N_TOK_HEADS * 2 * HEAD_DIM),
        "small_repl": small_repl,
    }
    return gx, early, got_early, late


WEIGHT_ORDER = ["mla_w_in", "mla_q_norm", "mla_w_uq", "mla_kv_norm", "mla_w_ukv", "lru_w_in",
                "lru_conv_w", "lru_conv_b", "lru_w_rgate", "lru_b_rgate", "lru_w_igate",
                "lru_b_igate", "lru_lambda", "w_mem_kv", "w_out", "ln_g", "ln_b"]


def kernel(x, mem, positions, mla_w_in, mla_q_norm, mla_w_uq, mla_kv_norm, mla_w_ukv, lru_w_in, lru_conv_w, lru_conv_b, lru_w_rgate, lru_b_rgate, lru_w_igate, lru_b_igate, lru_lambda, w_mem_kv, w_out, ln_g, ln_b, loss_target, m_mla_w_in, m_mla_q_norm, m_mla_w_uq, m_mla_kv_norm, m_mla_w_ukv, m_lru_w_in, m_lru_conv_w, m_lru_conv_b, m_lru_w_rgate, m_lru_b_rgate, m_lru_w_igate, m_lru_b_igate, m_lru_lambda, m_w_mem_kv, m_w_out, m_ln_g, m_ln_b, v_mla_w_in, v_mla_q_norm, v_mla_w_uq, v_mla_kv_norm, v_mla_w_ukv, v_lru_w_in, v_lru_conv_w, v_lru_conv_b, v_lru_w_rgate, v_lru_b_rgate, v_lru_w_igate, v_lru_b_igate, v_lru_lambda, v_w_mem_kv, v_w_out, v_ln_g, v_ln_b):
    w_in = dict(mla_w_in=mla_w_in, mla_q_norm=mla_q_norm, mla_w_uq=mla_w_uq,
                mla_kv_norm=mla_kv_norm, mla_w_ukv=mla_w_ukv, lru_w_in=lru_w_in,
                lru_conv_w=lru_conv_w, lru_conv_b=lru_conv_b, lru_w_rgate=lru_w_rgate,
                lru_b_rgate=lru_b_rgate, lru_w_igate=lru_w_igate, lru_b_igate=lru_b_igate,
                lru_lambda=lru_lambda, w_mem_kv=w_mem_kv, w_out=w_out, ln_g=ln_g, ln_b=ln_b)
    m_in = dict(mla_w_in=m_mla_w_in, mla_q_norm=m_mla_q_norm, mla_w_uq=m_mla_w_uq,
                mla_kv_norm=m_mla_kv_norm, mla_w_ukv=m_mla_w_ukv, lru_w_in=m_lru_w_in,
                lru_conv_w=m_lru_conv_w, lru_conv_b=m_lru_conv_b, lru_w_rgate=m_lru_w_rgate,
                lru_b_rgate=m_lru_b_rgate, lru_w_igate=m_lru_w_igate, lru_b_igate=m_lru_b_igate,
                lru_lambda=m_lru_lambda, w_mem_kv=m_w_mem_kv, w_out=m_w_out, ln_g=m_ln_g,
                ln_b=m_ln_b)
    v_in = dict(mla_w_in=v_mla_w_in, mla_q_norm=v_mla_q_norm, mla_w_uq=v_mla_w_uq,
                mla_kv_norm=v_mla_kv_norm, mla_w_ukv=v_mla_w_ukv, lru_w_in=v_lru_w_in,
                lru_conv_w=v_lru_conv_w, lru_conv_b=v_lru_conv_b, lru_w_rgate=v_lru_w_rgate,
                lru_b_rgate=v_lru_b_rgate, lru_w_igate=v_lru_w_igate, lru_b_igate=v_lru_b_igate,
                lru_lambda=v_lru_lambda, w_mem_kv=v_w_mem_kv, w_out=v_w_out, ln_g=v_ln_g,
                ln_b=v_ln_b)
    s = x.shape[1]
    ts = min(ROW_BLOCK, s)
    tatt = min(ATT_BLOCK, s)
    big_sizes = [_size(sh) // N_DEV for _, sh, _ in BIG]
    small_sizes = [_size(sh) // N_DEV for _, sh, _ in SMALL]

    big_local = _pack_rows([w_in[n] for n, _, _ in BIG], BIG_ROWS).astype(BF16)
    small_local = _pack_rows([w_in[n] for n, _, _ in SMALL], SMALL_ROWS)
    big_all, small_all = _allgather([big_local, small_local])
    wts = {}
    for (n, sh, ax), part in zip(BIG, _split_flat(big_all.reshape(N_DEV, -1), big_sizes)):
        wts[n] = _from_chunks(part, sh, ax)
    for (n, sh, ax), part in zip(SMALL, _split_flat(small_all.reshape(N_DEV, -1), small_sizes)):
        wts[n] = _from_chunks(part, sh, ax)
    for n, sh in REPL:
        wts[n] = w_in[n].reshape(sh)

    def early_exchange(g):
        small_chunks = jnp.moveaxis(g["lru_small"].reshape(SUBLANES, N_DEV, -1), 1, 0)
        sends = [(g["lru_w_in"], "cols"),
                 (g["w_mem_kv"][0], "rows"), (g["w_mem_kv"][1], "rows"),
                 (g["w_out"][0], "rows"), (g["w_out"][1], "rows"),
                 (small_chunks, "chunks"), (g["lru_w_rgate"], "all"), (g["lru_w_igate"], "all")]
        return _Exchange([a for a, _ in sends], [k for _, k in sends])

    gx, _, got_early, late = _local_step(x[0], mem[0], positions[0], loss_target[0], wts,
                                         ts, tatt, early_exchange)

    def chunked(name, shape):
        w = shape[1] // N_DEV
        return _to_chunks(late[name], 1).reshape(N_DEV, shape[0], w).astype(BF16)

    got_late = _exchange_grads(
        [chunked("mla_w_in", (D_MODEL, MLA_IN)),
         chunked("mla_w_uq", (Q_LORA, N_TOK_HEADS * QK_DIM)),
         chunked("mla_w_ukv", (KV_LORA, N_TOK_HEADS * 2 * HEAD_DIM)), late["small_repl"]],
        ["chunks", "chunks", "chunks", "all"], "exchange_grads")
    got = list(got_late[:3]) + list(got_early) + [got_late[3]]

    def small_sharded(d):
        return jnp.concatenate([d["lru_conv_w"].reshape(CONV_W, -1), d["lru_conv_b"],
                                d["lru_b_rgate"], d["lru_b_igate"], d["lru_lambda"]], axis=0)

    def small_replicated(d):
        gains = jnp.concatenate([d["mla_q_norm"], d["mla_kv_norm"]], axis=1)
        gains = jnp.pad(gains, ((0, 0), (0, D_MODEL - gains.shape[1])))
        return jnp.concatenate([d["ln_g"][0:1], d["ln_b"][0:1], d["ln_g"][1:2], d["ln_b"][1:2],
                                gains, jnp.zeros((3, D_MODEL), F32)], axis=0)

    def flat2(d, name):
        t = d[name]
        return t.reshape(-1, t.shape[-1])

    def update(parts, view, name):
        return _adamw(parts, view(w_in), view(m_in), view(v_in), "adamw_" + name)

    res = {}
    for idx, name in [(0, "mla_w_in"), (1, "mla_w_uq"), (2, "mla_w_ukv"), (3, "lru_w_in"),
                      (9, "lru_w_rgate"), (10, "lru_w_igate")]:
        res[name] = update([got[idx]], functools.partial(flat2, name=name), name)
    res["w_mem_kv"] = update([got[4], got[5]], functools.partial(flat2, name="w_mem_kv"),
                             "w_mem_kv")
    res["w_out"] = update([got[6], got[7]], functools.partial(flat2, name="w_out"), "w_out")
    res_ss = update([got[8]], small_sharded, "small_sharded")
    res_sr = update([got[11]], small_replicated, "small_replicated")
    loss = res_sr[0][5, 0]

    result = [loss, gx.reshape(x.shape)]
    for kind in range(4):
        ss, sr = res_ss[kind], res_sr[kind]
        out = {n: res[n][kind].reshape(w_in[n].shape) for n in res}
        out["lru_conv_w"] = ss[0:CONV_W].reshape(w_in["lru_conv_w"].shape)
        out["lru_conv_b"], out["lru_b_rgate"] = ss[4:5], ss[5:6]
        out["lru_b_igate"], out["lru_lambda"] = ss[6:7], ss[7:8]
        out["ln_g"] = jnp.concatenate([sr[0:1], sr[2:3]], axis=0)
        out["ln_b"] = jnp.concatenate([sr[1:2], sr[3:4]], axis=0)
        out["mla_q_norm"] = sr[4:5, 0:Q_LORA]
        out["mla_kv_norm"] = sr[4:5, Q_LORA:Q_LORA + KV_LORA]
        result += [out[n] for n in WEIGHT_ORDER]
    return tuple(result)
```

```python
import functools
import math

import jax
import jax.numpy as jnp
from jax import lax
from jax.experimental import pallas as pl
from jax.experimental.pallas import tpu as pltpu

F32 = jnp.float32
BF16 = jnp.bfloat16

D_MODEL = 1024
MEM_LEN = 256
HEAD_DIM = 64
N_TOK_HEADS = 12
N_MEM_HEADS = 4
TOK_WIDTH = 768
MEM_WIDTH = 256
MIX_WIDTH = 1024
Q_LORA = 384
KV_LORA = 256
QK_NOPE = 64
QK_ROPE = 32
QK_DIM = 96
ROPE_THETA = 10000.0
CONV_W = 4
LRU_C = 8.0
ALPHA = (2.0 * 2) ** 0.25
NORM_EPS = 1e-6
MLA_IN = 1952
LRU_IN = 2048
ADAM_LR = 0.001
ADAM_B1 = 0.9
ADAM_B2 = 0.999
ADAM_EPS = 1e-08
ADAM_WD = 0.01
ADAM_STEP = 10

N_DEV = 8
LANES = 128
SUBLANES = 8
HEAD_PAD = 128
QKV_PAD = N_TOK_HEADS * HEAD_PAD
ZP = 2048
ZA_W = TOK_WIDTH
ZG_W = MIX_WIDTH + MEM_WIDTH
ZA_CQ, ZA_CKV, ZA_KR = 0, 384, 640
KR_LANE = 64

ROW_BLOCK = 512
ATT_BLOCK = 512
LOOKAHEAD = 3
FWD_HEADS = 12
BWD_HEADS = 4
VMEM_LIMIT = 56 * 1024 * 1024
NEG_BIG = -1e30
STRIP = 32
LOG2E = math.log2(math.e)


def _cp(n_axes):
    return pltpu.CompilerParams(dimension_semantics=("arbitrary",) * n_axes,
                                vmem_limit_bytes=VMEM_LIMIT)


def _dot(a, b):
    return jnp.dot(a, b, preferred_element_type=F32)


def _dot_nt(a, b):
    return lax.dot_general(a, b, (((1,), (1,)), ((), ())), preferred_element_type=F32)


def _dot_tn(a, b):
    return lax.dot_general(a, b, (((0,), (0,)), ((), ())), preferred_element_type=F32)


def _sigmoid(t):
    return 1.0 / (1.0 + jnp.exp(-t))


def _lane(shape):
    return lax.broadcasted_iota(jnp.int32, shape, len(shape) - 1)


def _full(shape):
    nd = len(shape)
    return pl.BlockSpec(shape, lambda *_: (0,) * nd)


def _rows(ts, width, col=0):
    return pl.BlockSpec((ts, width), lambda i: (i, col))


def _heads(ts):
    return pl.BlockSpec((N_TOK_HEADS, ts, HEAD_PAD), lambda i: (0, i, 0))


def _rowmm(x, w, widths, name, ts):
    s, k = x.shape
    n = w.shape[1]
    offs = [sum(widths[:a]) for a in range(len(widths))]

    def body(x_ref, w_ref, *o_refs):
        res = _dot(x_ref[...].astype(BF16), w_ref[...])
        for o_ref, off, wd in zip(o_refs, offs, widths):
            o_ref[...] = res[:, off:off + wd]

    return pl.pallas_call(
        body, grid=(s // ts,),
        in_specs=[_rows(ts, k), _full((k, n))],
        out_specs=[_rows(ts, wd) for wd in widths],
        out_shape=[jax.ShapeDtypeStruct((s, wd), F32) for wd in widths],
        name=name, compiler_params=_cp(1))(x, w)


def _rms_parts(t):
    rs = lax.rsqrt(jnp.mean(t * t, axis=-1, keepdims=True) + NORM_EPS)
    return rs


def _rope(t, c, sa, sb):
    return t * c + pltpu.roll(t, LANES - 16, 1) * sa + pltpu.roll(t, 16, 1) * sb


def _rope_t(d, c, sa, sb):
    return d * c + pltpu.roll(d * sa, 16, 1) + pltpu.roll(d * sb, LANES - 16, 1)


def _mla_prep_fwd(z0, tabs, gq, gkv, wuq, wukv, ts):
    s = z0.shape[0]

    def body(z_ref, c_ref, sa_ref, sb_ref, gq_ref, gkv_ref, wuq_ref, wukv_ref,
             q_ref, k_ref, v_ref):
        cq = z_ref[:, ZA_CQ:ZA_CQ + Q_LORA]
        ckv = z_ref[:, ZA_CKV:ZA_CKV + KV_LORA]
        kr = z_ref[:, ZA_KR:ZA_KR + LANES]
        cqn = cq * _rms_parts(cq) * gq_ref[...]
        ckvn = ckv * _rms_parts(ckv) * gkv_ref[...]
        q = _dot(cqn.astype(BF16), wuq_ref[...])
        kv = _dot(ckvn.astype(BF16), wukv_ref[...])
        c, sa, sb = c_ref[...], sa_ref[...], sb_ref[...]
        krope = _rope(kr, c, sa, sb)
        pad_lane = _lane((ts, HEAD_PAD)) >= HEAD_DIM
        for h in range(N_TOK_HEADS):
            sl = slice(h * HEAD_PAD, (h + 1) * HEAD_PAD)
            q_ref[h] = _rope(q[:, sl], c, sa, sb).astype(BF16)
            k_ref[h] = (kv[:, sl] + krope).astype(BF16)
            vh = kv[:, QKV_PAD + h * HEAD_PAD:QKV_PAD + (h + 1) * HEAD_PAD]
            v_ref[h] = jnp.where(pad_lane, 1.0, vh).astype(BF16)

    out = jax.ShapeDtypeStruct((N_TOK_HEADS, s, HEAD_PAD), BF16)
    return pl.pallas_call(
        body, grid=(s // ts,),
        in_specs=[_rows(ts, ZA_W), _rows(ts, LANES), _rows(ts, LANES), _rows(ts, LANES),
                  _full((1, Q_LORA)), _full((1, KV_LORA)),
                  _full((Q_LORA, QKV_PAD)), _full((KV_LORA, 2 * QKV_PAD))],
        out_specs=[_heads(ts)] * 3,
        out_shape=[out, out, out],
        name="mla_prep_fwd", compiler_params=_cp(1))(z0, *tabs, gq, gkv, wuq, wukv)


def _mla_prep_bwd(z0, dq, dk, dv, tabs, gq, gkv, wuq_t, wukv_t, ts):
    s = z0.shape[0]

    def body(z_ref, dq_ref, dk_ref, dv_ref, c_ref, sa_ref, sb_ref, gq_ref, gkv_ref,
             wuqt_ref, wukvt_ref, dza_ref, dzk_ref, dwuq_ref, dwukv_ref, dg_ref):
        @pl.when(pl.program_id(0) == 0)
        def _():
            dwuq_ref[...] = jnp.zeros_like(dwuq_ref)
            dwukv_ref[...] = jnp.zeros_like(dwukv_ref)
            dg_ref[...] = jnp.zeros_like(dg_ref)

        cq = z_ref[:, ZA_CQ:ZA_CQ + Q_LORA]
        ckv = z_ref[:, ZA_CKV:ZA_CKV + KV_LORA]
        rq, rkv = _rms_parts(cq), _rms_parts(ckv)
        gq_, gkv_ = gq_ref[...], gkv_ref[...]
        cqn = (cq * rq * gq_).astype(BF16)
        ckvn = (ckv * rkv * gkv_).astype(BF16)
        c, sa, sb = c_ref[...], sa_ref[...], sb_ref[...]
        dqp, dksum = [], None
        for h in range(N_TOK_HEADS):
            dqp.append(_rope_t(dq_ref[h], c, sa, sb))
            dksum = dk_ref[h] if dksum is None else dksum + dk_ref[h]
        dqp = jnp.concatenate(dqp, axis=1).astype(BF16)
        lane = _lane(dksum.shape)
        dzk_ref[...] = jnp.where((lane >= KR_LANE) & (lane < KR_LANE + QK_ROPE),
                                 _rope_t(dksum, c, sa, sb), 0.0).astype(BF16)
        dkv = jnp.concatenate([dk_ref[h].astype(BF16) for h in range(N_TOK_HEADS)]
                              + [dv_ref[h] for h in range(N_TOK_HEADS)], axis=1)
        dcqn = _dot(dqp, wuqt_ref[...])
        dckvn = _dot(dkv, wukvt_ref[...])
        dwuq_ref[...] += _dot_tn(cqn, dqp)
        dwukv_ref[...] += _dot_tn(ckvn, dkv)
        dg_ref[0:1, 0:Q_LORA] += jnp.sum(dcqn * cq * rq, axis=0, keepdims=True)
        dg_ref[0:1, Q_LORA:Q_LORA + KV_LORA] += jnp.sum(dckvn * ckv * rkv, axis=0, keepdims=True)
        wq = dcqn * gq_
        wkv = dckvn * gkv_
        dcq = rq * wq - cq * (rq * rq * rq) * jnp.mean(wq * cq, axis=-1, keepdims=True)
        dckv = rkv * wkv - ckv * (rkv * rkv * rkv) * jnp.mean(wkv * ckv, axis=-1, keepdims=True)
        dza_ref[:, 0:Q_LORA] = dcq.astype(BF16)
        dza_ref[:, Q_LORA:Q_LORA + KV_LORA] = dckv.astype(BF16)

    na = Q_LORA + KV_LORA
    return pl.pallas_call(
        body, grid=(s // ts,),
        in_specs=[_rows(ts, ZA_W), _heads(ts), _heads(ts), _heads(ts),
                  _rows(ts, LANES), _rows(ts, LANES), _rows(ts, LANES),
                  _full((1, Q_LORA)), _full((1, KV_LORA)),
                  _full((QKV_PAD, Q_LORA)), _full((2 * QKV_PAD, KV_LORA))],
        out_specs=[_rows(ts, na), _rows(ts, LANES), _full((Q_LORA, QKV_PAD)),
                   _full((KV_LORA, 2 * QKV_PAD)), _full((SUBLANES, na))],
        out_shape=[jax.ShapeDtypeStruct((s, na), BF16), jax.ShapeDtypeStruct((s, LANES), BF16),
                   jax.ShapeDtypeStruct((Q_LORA, QKV_PAD), F32),
                   jax.ShapeDtypeStruct((KV_LORA, 2 * QKV_PAD), F32),
                   jax.ShapeDtypeStruct((SUBLANES, na), F32)],
        name="mla_prep_bwd", compiler_params=_cp(1))(
            z0, dq, dk, dv, *tabs, gq, gkv, wuq_t, wukv_t)


def _causal_pairs(nb, by_key):
    if by_key:
        pairs = [(i, j) for j in range(nb) for i in range(j, nb)]
    else:
        pairs = [(i, j) for i in range(nb) for j in range(i + 1)]
    return (jnp.array([p[0] for p in pairs], jnp.int32),
            jnp.array([p[1] for p in pairs], jnp.int32))


def _flash_fwd(q, k, v, t, nh):
    s = q.shape[1]
    itab, jtab = _causal_pairs(s // t, False)
    c2 = LOG2E / math.sqrt(QK_DIM)

    def body(it_ref, jt_ref, q_ref, k_ref, v_ref, o_ref, lse_ref, m_scr, acc_scr):
        pair = pl.program_id(1)
        i, j = it_ref[pair], jt_ref[pair]

        @pl.when(j == 0)
        def _():
            m_scr[...] = jnp.full_like(m_scr, NEG_BIG)
            acc_scr[...] = jnp.zeros_like(acc_scr)

        def softmax_strips(masked, hs, sc, row0):
            ps, als = [], []
            for r0 in range(0, sc.shape[0], STRIP):
                rows = slice(row0 + r0, row0 + r0 + STRIP)
                ch = [sc[r0:r0 + STRIP, n * LANES:(n + 1) * LANES] * c2
                      for n in range(sc.shape[1] // LANES)]
                if masked:
                    rr = row0 + r0 + lax.broadcasted_iota(jnp.int32, (STRIP, LANES), 0)
                    cc = lax.broadcasted_iota(jnp.int32, (STRIP, LANES), 1)
                    ch = [jnp.where(cc + n * LANES <= rr, c_, NEG_BIG) for n, c_ in enumerate(ch)]
                mx = ch[0]
                for c_ in ch[1:]:
                    mx = jnp.maximum(mx, c_)
                m_prev = m_scr[hs, rows, :]
                m_next = jnp.maximum(m_prev, jnp.max(mx, axis=-1, keepdims=True))
                ps.append(jnp.concatenate(
                    [jnp.exp2(c_ - m_next).astype(BF16) for c_ in ch], axis=1))
                als.append(jnp.exp2(m_prev - m_next))
                m_scr[hs, rows, :] = m_next
            return jnp.concatenate(ps, axis=0), jnp.concatenate(als, axis=0)

        def run(masked, parts):
            def scores_of(hs):
                return [_dot_nt(q_ref[hs, r0:r0 + nr, :], k_ref[hs, 0:nk, :])
                        for r0, nr, nk in parts]

            ahead = min(LOOKAHEAD, nh)
            scores = [scores_of(hs) for hs in range(ahead)]
            for hs in range(nh):
                if hs + ahead < nh:
                    scores.append(scores_of(hs + ahead))
                for (r0, nr, nk), sc in zip(parts, scores[hs]):
                    p, alpha = softmax_strips(masked, hs, sc, r0)
                    acc_scr[hs, r0:r0 + nr, :] = (alpha * acc_scr[hs, r0:r0 + nr, :]
                                                  + _dot(p, v_ref[hs, 0:nk, :]))

        @pl.when(j < i)
        def _():
            run(False, [(0, t, t)])

        @pl.when(j == i)
        def _():
            run(True, [(0, t, t)])
            for h in range(nh):
                acc = acc_scr[h]
                l = acc[:, HEAD_DIM:HEAD_DIM + 1]
                o_ref[h] = jnp.where(_lane(acc.shape) < HEAD_DIM, acc / l, 0.0)
                lse_ref[h] = m_scr[h] + jnp.log2(l)

    qspec = pl.BlockSpec((nh, t, HEAD_PAD), lambda h, p, it, jt: (h, it[p], 0))
    kspec = pl.BlockSpec((nh, t, HEAD_PAD), lambda h, p, it, jt: (h, jt[p], 0))
    out = jax.ShapeDtypeStruct((N_TOK_HEADS, s, HEAD_PAD), F32)
    return pl.pallas_call(
        body,
        grid_spec=pltpu.PrefetchScalarGridSpec(
            num_scalar_prefetch=2, grid=(N_TOK_HEADS // nh, itab.shape[0]),
            in_specs=[qspec, kspec, kspec], out_specs=[qspec, qspec],
            scratch_shapes=[pltpu.VMEM((nh, t, HEAD_PAD), F32)] * 2),
        out_shape=[out, out],
        name="flash_fwd", compiler_params=_cp(2))(itab, jtab, q, k, v)


def _flash_bwd(q, k, v, stats, do, t, nh, ex):
    s = q.shape[1]
    nb = s // t
    itab, jtab = _causal_pairs(nb, True)
    npairs = itab.shape[0]
    ngroups = N_TOK_HEADS // nh
    scale = 1.0 / math.sqrt(QK_DIM)
    c2 = LOG2E * scale
    nx = ex.n if ex is not None else 0
    ex_arrays, ex_out_shape, ex_scratch = (
        (ex.arrays, ex.out_shape, ex.scratch) if ex is not None else ([], [], []))

    def body(it_ref, jt_ref, q_ref, k_ref, v_ref, st_ref, do_ref, *rest):
        ex_in, rest = rest[:nx], rest[nx:]
        dq_ref, dk_ref, dv_ref = rest[:3]
        ex_out, rest = rest[3:3 + nx], rest[3 + nx:]
        dk_scr, dv_scr = rest[:2]
        ex_sems = rest[2:]
        pair = pl.program_id(1)
        i, j = it_ref[pair], jt_ref[pair]
        rows_i = pl.ds(pl.multiple_of(i * t, t), t)

        if nx:
            @pl.when(jnp.logical_and(pl.program_id(0) == 0, pair == 0))
            def _():
                for cp in ex.copies(ex_in, ex_out, ex_sems):
                    cp.start()

        @pl.when(i == j)
        def _():
            dk_scr[...] = jnp.zeros_like(dk_scr)
            dv_scr[...] = jnp.zeros_like(dv_scr)

        @pl.when(j == 0)
        def _():
            dq_ref[:, rows_i, :] = jnp.zeros((nh, t, HEAD_PAD), F32)

        def prob_strips(masked, h, sc, dp, row0):
            ps, dss = [], []
            low = _lane((STRIP, LANES)) < HEAD_DIM
            for r0 in range(0, sc.shape[0], STRIP):
                rows = slice(r0, r0 + STRIP)
                st = st_ref[h, row0 + r0:row0 + r0 + STRIP, :]
                swapped = pltpu.roll(st, HEAD_DIM, 1)
                lse = jnp.where(low, st, swapped)
                delta = jnp.where(low, swapped, st)
                if masked:
                    rr = row0 + r0 + lax.broadcasted_iota(jnp.int32, (STRIP, LANES), 0)
                    cc = lax.broadcasted_iota(jnp.int32, (STRIP, LANES), 1)
                pcs, dcs = [], []
                for n in range(sc.shape[1] // LANES):
                    cols = slice(n * LANES, (n + 1) * LANES)
                    x = sc[rows, cols] * c2
                    if masked:
                        x = jnp.where(cc + n * LANES <= rr, x, NEG_BIG)
                    p = jnp.exp2(x - lse)
                    pcs.append(p.astype(BF16))
                    dcs.append((p * (dp[rows, cols] - delta) * scale).astype(BF16))
                ps.append(jnp.concatenate(pcs, axis=1))
                dss.append(jnp.concatenate(dcs, axis=1))
            return jnp.concatenate(ps, axis=0), jnp.concatenate(dss, axis=0)

        def run(masked, parts):
            def scores_of(h):
                return [(_dot_nt(q_ref[h, r0:r0 + nr, :], k_ref[h, 0:nk, :]),
                         _dot_nt(do_ref[h, r0:r0 + nr, :], v_ref[h, 0:nk, :]))
                        for r0, nr, nk in parts]

            ahead = min(LOOKAHEAD, nh)
            scores = [scores_of(h) for h in range(ahead)]
            for h in range(nh):
                if h + ahead < nh:
                    scores.append(scores_of(h + ahead))
                for (r0, nr, nk), (sc, dp) in zip(parts, scores[h]):
                    p, ds = prob_strips(masked, h, sc, dp, r0)
                    dv_scr[h, 0:nk, :] += _dot_tn(p, do_ref[h, r0:r0 + nr, :])
                    dk_scr[h, 0:nk, :] += _dot_tn(ds, q_ref[h, r0:r0 + nr, :])
                    rows = pl.ds(pl.multiple_of(i * t + r0, t // 2), nr)
                    dq_ref[h, rows, :] += _dot(ds, k_ref[h, 0:nk, :])

        @pl.when(i > j)
        def _():
            run(False, [(0, t, t)])

        @pl.when(i == j)
        def _():
            run(True, [(0, t // 2, t // 2), (t // 2, t // 2, t)])

        @pl.when(i == nb - 1)
        def _():
            dk_ref[...] = dk_scr[...]
            dv_ref[...] = dv_scr[...].astype(BF16)

        if nx:
            @pl.when(jnp.logical_and(pl.program_id(0) == ngroups - 1, pair == npairs - 1))
            def _():
                for cp in ex.copies(ex_in, ex_out, ex_sems):
                    cp.wait()

    qspec = pl.BlockSpec((nh, t, HEAD_PAD), lambda h, p, it, jt: (h, it[p], 0))
    kspec = pl.BlockSpec((nh, t, HEAD_PAD), lambda h, p, it, jt: (h, jt[p], 0))
    dqspec = pl.BlockSpec((nh, s, HEAD_PAD), lambda h, p, it, jt: (h, 0, 0))
    out = jax.ShapeDtypeStruct((N_TOK_HEADS, s, HEAD_PAD), F32)
    res = pl.pallas_call(
        body,
        grid_spec=pltpu.PrefetchScalarGridSpec(
            num_scalar_prefetch=2, grid=(ngroups, npairs),
            in_specs=[qspec, kspec, kspec, qspec, qspec] + [ANY] * nx,
            out_specs=[dqspec, kspec, kspec] + [ANY] * nx,
            scratch_shapes=[pltpu.VMEM((nh, t, HEAD_PAD), F32)] * 2 + ex_scratch),
        out_shape=[out, out, jax.ShapeDtypeStruct(out.shape, BF16)] + ex_out_shape,
        name="flash_bwd", compiler_params=_cp(2))(itab, jtab, q, k, v, stats, do, *ex_arrays)
    return res[:3], res[3:]


def _mem_probs(qp, kp, hh):
    lane = _lane(qp.shape)
    keep = (lane < HEAD_DIM) if hh == 0 else (lane >= HEAD_DIM)
    qh = jnp.where(keep, qp, 0.0).astype(BF16)
    sc = _dot_nt(qh, kp) * (1.0 / math.sqrt(HEAD_DIM))
    e = jnp.exp(sc - jnp.max(sc, axis=-1, keepdims=True))
    return e / jnp.sum(e, axis=-1, keepdims=True), keep


def _gate_mem_fwd(tok, z, memkv, g0, q0, padded, name, ts):
    s = z.shape[0]
    zw = z.shape[1]
    tok_spec = _heads(ts) if padded else _rows(ts, TOK_WIDTH)

    def body(tok_ref, z_ref, mkv_ref, cat_ref, y_ref):
        if padded:
            for p in range(N_TOK_HEADS // 2):
                cat_ref[:, p * LANES:(p + 1) * LANES] = (
                    tok_ref[2 * p] + pltpu.roll(tok_ref[2 * p + 1], HEAD_DIM, 1))
        else:
            cat_ref[:, 0:TOK_WIDTH] = tok_ref[...]
        for pr in range(N_MEM_HEADS // 2):
            sl = slice(pr * LANES, (pr + 1) * LANES)
            qp = z_ref[:, q0 + pr * LANES:q0 + (pr + 1) * LANES]
            kp = mkv_ref[:, sl].astype(BF16)
            vp = mkv_ref[:, MEM_WIDTH + pr * LANES:MEM_WIDTH + (pr + 1) * LANES].astype(BF16)
            outs = []
            for hh in range(2):
                p, _ = _mem_probs(qp, kp, hh)
                outs.append(_dot(p.astype(BF16), vp))
            lane = _lane(outs[0].shape)
            cat_ref[:, TOK_WIDTH + pr * LANES:TOK_WIDTH + (pr + 1) * LANES] = jnp.where(
                lane < HEAD_DIM, outs[0], outs[1])
        gate = z_ref[:, g0:g0 + MIX_WIDTH]
        y_ref[...] = (cat_ref[...] * (gate * _sigmoid(gate))).astype(BF16)

    return pl.pallas_call(
        body, grid=(s // ts,),
        in_specs=[tok_spec, _rows(ts, zw), _full((MEM_LEN, 2 * MEM_WIDTH))],
        out_specs=[_rows(ts, MIX_WIDTH)] * 2,
        out_shape=[jax.ShapeDtypeStruct((s, MIX_WIDTH), F32),
                   jax.ShapeDtypeStruct((s, MIX_WIDTH), BF16)],
        name=name, compiler_params=_cp(1))(tok, z, memkv)


def _gate_mem_bwd(dy, cat, z, memkv, lse, g0, q0, name, ts):
    s = z.shape[0]
    zw = z.shape[1]
    padded = lse is not None
    gq_w = MIX_WIDTH + MEM_WIDTH

    def body(*refs):
        if padded:
            dy_ref, cat_ref, z_ref, mkv_ref, lse_ref, dzg_ref, dtok_ref, dmkv_ref, st_ref = refs
        else:
            dy_ref, cat_ref, z_ref, mkv_ref, dzg_ref, dtok_ref, dmkv_ref = refs

        @pl.when(pl.program_id(0) == 0)
        def _():
            dmkv_ref[...] = jnp.zeros_like(dmkv_ref)

        gate = z_ref[:, g0:g0 + MIX_WIDTH]
        sg = _sigmoid(gate)
        dy_ = dy_ref[...]
        dzg_ref[:, 0:MIX_WIDTH] = (dy_ * cat_ref[...]
                                   * (sg * (1.0 + gate * (1.0 - sg)))).astype(BF16)
        dcat = dy_ * (gate * sg)
        if padded:
            low = _lane((ts, LANES)) < HEAD_DIM
            for p in range(N_TOK_HEADS // 2):
                d = dcat[:, p * LANES:(p + 1) * LANES]
                prod = d * cat_ref[:, p * LANES:(p + 1) * LANES]
                first = jnp.sum(jnp.where(low, prod, 0.0), axis=-1, keepdims=True)
                second = jnp.sum(jnp.where(low, 0.0, prod), axis=-1, keepdims=True)
                dtok_ref[2 * p] = jnp.where(low, d, 0.0).astype(BF16)
                dtok_ref[2 * p + 1] = jnp.where(low, pltpu.roll(d, HEAD_DIM, 1), 0.0).astype(BF16)
                st_ref[2 * p] = jnp.where(low, lse_ref[2 * p], first)
                st_ref[2 * p + 1] = jnp.where(low, lse_ref[2 * p + 1], second)
        else:
            dtok_ref[...] = dcat[:, 0:TOK_WIDTH]
        for pr in range(N_MEM_HEADS // 2):
            sl = slice(pr * LANES, (pr + 1) * LANES)
            vsl = slice(MEM_WIDTH + pr * LANES, MEM_WIDTH + (pr + 1) * LANES)
            qp = z_ref[:, q0 + pr * LANES:q0 + (pr + 1) * LANES]
            qpb = qp.astype(BF16)
            kp = mkv_ref[:, sl].astype(BF16)
            vp = mkv_ref[:, vsl].astype(BF16)
            dmo = dcat[:, TOK_WIDTH + pr * LANES:TOK_WIDTH + (pr + 1) * LANES]
            dqp = None
            for hh in range(2):
                p, keep = _mem_probs(qp, kp, hh)
                do_h = jnp.where(keep, dmo, 0.0).astype(BF16)
                dmkv_ref[:, vsl] += _dot_tn(p.astype(BF16), do_h)
                dp = _dot_nt(do_h, vp)
                ds = (p * (dp - jnp.sum(dp * p, axis=-1, keepdims=True))
                      * (1.0 / math.sqrt(HEAD_DIM))).astype(BF16)
                dqh = jnp.where(keep, _dot(ds, kp), 0.0)
                dqp = dqh if dqp is None else dqp + dqh
                dkh = _dot_tn(ds, qpb)
                klane = _lane(dkh.shape)
                kkeep = (klane < HEAD_DIM) if hh == 0 else (klane >= HEAD_DIM)
                dmkv_ref[:, sl] += jnp.where(kkeep, dkh, 0.0)
            dzg_ref[:, MIX_WIDTH + pr * LANES:MIX_WIDTH + (pr + 1) * LANES] = dqp.astype(BF16)

    in_specs = [_rows(ts, MIX_WIDTH), _rows(ts, MIX_WIDTH), _rows(ts, zw),
                _full((MEM_LEN, 2 * MEM_WIDTH))]
    out_specs = [_rows(ts, gq_w), _heads(ts) if padded else _rows(ts, TOK_WIDTH),
                 _full((MEM_LEN, 2 * MEM_WIDTH))]
    heads_shape = (N_TOK_HEADS, s, HEAD_PAD)
    out_shape = [jax.ShapeDtypeStruct((s, gq_w), BF16),
                 jax.ShapeDtypeStruct(heads_shape, BF16) if padded
                 else jax.ShapeDtypeStruct((s, TOK_WIDTH), F32),
                 jax.ShapeDtypeStruct((MEM_LEN, 2 * MEM_WIDTH), F32)]
    args = [dy, cat, z, memkv]
    if padded:
        in_specs.append(_heads(ts))
        out_specs.append(_heads(ts))
        out_shape.append(jax.ShapeDtypeStruct(heads_shape, F32))
        args.append(lse)
    return pl.pallas_call(
        body, grid=(s // ts,), in_specs=in_specs, out_specs=out_specs, out_shape=out_shape,
        name=name, compiler_params=_cp(1))(*args)


def _ln_stats(pre):
    mu = jnp.mean(pre, axis=-1, keepdims=True)
    d = pre - mu
    rstd = lax.rsqrt(jnp.mean(d * d, axis=-1, keepdims=True) + NORM_EPS)
    return d * rstd, rstd


def _outproj_ln_fwd(y, w, h, g, b, tgt, name, ts):
    s = y.shape[0]
    with_loss = tgt is not None

    def body(*refs):
        if with_loss:
            y_ref, w_ref, h_ref, g_ref, b_ref, t_ref, pre_ref, out_ref, loss_ref = refs
        else:
            y_ref, w_ref, h_ref, g_ref, b_ref, pre_ref, out_ref = refs
        pre = ALPHA * h_ref[...] + _dot(y_ref[...].astype(BF16), w_ref[...])
        pre_ref[...] = pre
        xhat, _ = _ln_stats(pre)
        hout = xhat * g_ref[...] + b_ref[...]
        if with_loss:
            @pl.when(pl.program_id(0) == 0)
            def _():
                loss_ref[...] = jnp.zeros_like(loss_ref)
            err = hout - t_ref[...]
            out_ref[...] = err * (1.0 / D_MODEL)
            loss_ref[...] += 0.5 * jnp.sum(jnp.mean(err * err, axis=-1, keepdims=True))
        else:
            out_ref[...] = hout

    act = jax.ShapeDtypeStruct((s, D_MODEL), F32)
    in_specs = [_rows(ts, MIX_WIDTH), _full((MIX_WIDTH, D_MODEL)), _rows(ts, D_MODEL),
                _full((1, D_MODEL)), _full((1, D_MODEL))]
    out_specs = [_rows(ts, D_MODEL)] * 2
    out_shape = [act, act]
    args = [y, w, h, g, b]
    if with_loss:
        in_specs.append(_rows(ts, D_MODEL))
        out_specs.append(_full((SUBLANES, LANES)))
        out_shape.append(jax.ShapeDtypeStruct((SUBLANES, LANES), F32))
        args.append(tgt)
    return pl.pallas_call(
        body, grid=(s // ts,), in_specs=in_specs, out_specs=out_specs, out_shape=out_shape,
        name=name, compiler_params=_cp(1))(*args)


def _outproj_ln_bwd(dh, pre, g, y, w_t, name, ts):
    s = y.shape[0]

    def body(dh_ref, pre_ref, g_ref, y_ref, wt_ref, dpre_ref, dy_ref, dw_ref, dgb_ref):
        @pl.when(pl.program_id(0) == 0)
        def _():
            dw_ref[...] = jnp.zeros_like(dw_ref)
            dgb_ref[...] = jnp.zeros_like(dgb_ref)

        dh_ = dh_ref[...]
        xhat, rstd = _ln_stats(pre_ref[...])
        dxh = dh_ * g_ref[...]
        dpre = rstd * (dxh - jnp.mean(dxh, axis=-1, keepdims=True)
                       - xhat * jnp.mean(dxh * xhat, axis=-1, keepdims=True))
        dpre_ref[...] = dpre
        dgb_ref[0:1, :] += jnp.sum(dh_ * xhat, axis=0, keepdims=True)
        dgb_ref[1:2, :] += jnp.sum(dh_, axis=0, keepdims=True)
        dpb = dpre.astype(BF16)
        dy_ref[...] = _dot(dpb, wt_ref[...])
        dw_ref[...] += _dot_tn(y_ref[...].astype(BF16), dpb)

    act = jax.ShapeDtypeStruct((s, D_MODEL), F32)
    return pl.pallas_call(
        body, grid=(s // ts,),
        in_specs=[_rows(ts, D_MODEL), _rows(ts, D_MODEL), _full((1, D_MODEL)),
                  _rows(ts, MIX_WIDTH), _full((D_MODEL, MIX_WIDTH))],
        out_specs=[_rows(ts, D_MODEL), _rows(ts, MIX_WIDTH), _full((MIX_WIDTH, D_MODEL)),
                   _full((SUBLANES, D_MODEL))],
        out_shape=[act, act, jax.ShapeDtypeStruct((MIX_WIDTH, D_MODEL), F32),
                   jax.ShapeDtypeStruct((SUBLANES, D_MODEL), F32)],
        name=name, compiler_params=_cp(1))(dh, pre, g, y, w_t)


def _linear_bwd(x, dys, offs, w_t, resid, name, ts):
    s, kdim = x.shape
    n = w_t.shape[0]
    widths = [d.shape[1] for d in dys]
    npieces = len(dys)

    def body(*refs):
        x_ref = refs[0]
        dy_refs = refs[1:1 + npieces]
        wt_ref, r_ref, dx_ref, dw_ref = refs[1 + npieces:]

        @pl.when(pl.program_id(0) == 0)
        def _():
            dw_ref[...] = jnp.zeros_like(dw_ref)

        xb = x_ref[...].astype(BF16)
        dx = ALPHA * r_ref[...]
        for dy_ref, off, wd in zip(dy_refs, offs, widths):
            dyb = dy_ref[...].astype(BF16)
            dx = dx + _dot(dyb, wt_ref[off:off + wd, :])
            dw_ref[:, off:off + wd] += _dot_tn(xb, dyb)
        dx_ref[...] = dx

    return pl.pallas_call(
        body, grid=(s // ts,),
        in_specs=[_rows(ts, kdim)] + [_rows(ts, wd) for wd in widths]
                 + [_full((n, kdim)), _rows(ts, kdim)],
        out_specs=[_rows(ts, kdim), _full((kdim, n))],
        out_shape=[jax.ShapeDtypeStruct((s, kdim), F32), jax.ShapeDtypeStruct((kdim, n), F32)],
        name=name, compiler_params=_cp(1))(x, *dys, w_t, resid)


def _wgrad_small(x, dy, name):
    def body(x_ref, dy_ref, dw_ref):
        dw_ref[...] = _dot_tn(x_ref[...].astype(BF16), dy_ref[...].astype(BF16))

    return pl.pallas_call(
        body, out_shape=jax.ShapeDtypeStruct((x.shape[1], dy.shape[1]), F32),
        name=name, compiler_params=pltpu.CompilerParams(vmem_limit_bytes=VMEM_LIMIT))(x, dy)


def _shift_down(u, carry8, k):
    if k == 0:
        return u
    rolled = pltpu.roll(u, k, 0)
    row = lax.broadcasted_iota(jnp.int32, carry8.shape, 0)
    top = jnp.where(row < k, pltpu.roll(carry8, k, 0), rolled[0:SUBLANES])
    return jnp.concatenate([top, rolled[SUBLANES:]], axis=0)


def _shift_up(u, carry8, k):
    if k == 0:
        return u
    n = u.shape[0]
    rolled = pltpu.roll(u, n - k, 0)
    row = lax.broadcasted_iota(jnp.int32, carry8.shape, 0)
    bot = jnp.where(row >= SUBLANES - k, pltpu.roll(carry8, SUBLANES - k, 0),
                    rolled[n - SUBLANES:])
    return jnp.concatenate([rolled[:n - SUBLANES], bot], axis=0)


def _neg_expm1(t):
    e = jnp.exp(t)
    em1 = e - 1.0
    safe = jnp.where(e == 1.0, 1.0, jnp.log(e))
    return -jnp.where(e == 1.0, t, jnp.where(em1 == -1.0, -1.0, em1 * t / safe))


def _lru_gates(u, carry8, cw_ref, vec_ref, wr_ref, wi_ref):
    taps = [_shift_down(u, carry8, k) for k in range(CONV_W)]
    xc = vec_ref[0:1, :] + cw_ref[3:4, :] * u
    for k in range(1, CONV_W):
        xc = xc + cw_ref[3 - k:4 - k, :] * taps[k]
    xb = xc.astype(BF16)
    r = _sigmoid(_dot(xb, wr_ref[...]) + vec_ref[1:2, :])
    ig = _sigmoid(_dot(xb, wi_ref[...]) + vec_ref[2:3, :])
    nlam = -vec_ref[3:4, :]
    softplus = jnp.maximum(nlam, 0.0) + jnp.log(1.0 + jnp.exp(-jnp.abs(nlam)))
    cneg = -LRU_C * softplus
    log_a = cneg * r
    a = jnp.exp(log_a)
    sq = jnp.sqrt(_neg_expm1(2.0 * log_a))
    return xc, r, ig, cneg, a, sq, taps


def _lru_fwd(z1, cw8, vec8, wr, wi, ts):
    s = z1.shape[0]

    def body(u_ref, cw_ref, vec_ref, wr_ref, wi_ref, hs_ref, cu_scr, ch_scr, a_scr, gx_scr):
        @pl.when(pl.program_id(0) == 0)
        def _():
            cu_scr[...] = jnp.zeros_like(cu_scr)
            ch_scr[...] = jnp.zeros_like(ch_scr)

        u = u_ref[...]
        xc, _, ig, _, a, sq, _ = _lru_gates(u, cu_scr[...], cw_ref, vec_ref, wr_ref, wi_ref)
        a_scr[...] = a
        gx_scr[...] = sq * (ig * xc)

        def step(t, h):
            h = a_scr[pl.ds(t, 1), :] * h + gx_scr[pl.ds(t, 1), :]
            hs_ref[pl.ds(t, 1), :] = h
            return h

        ch_scr[0:1, :] = lax.fori_loop(0, ts, step, ch_scr[0:1, :])
        cu_scr[...] = u[ts - SUBLANES:, :]

    w = TOK_WIDTH
    return pl.pallas_call(
        body, grid=(s // ts,),
        in_specs=[_rows(ts, w), _full((SUBLANES, w)), _full((SUBLANES, w)),
                  _full((w, w)), _full((w, w))],
        out_specs=_rows(ts, w),
        out_shape=jax.ShapeDtypeStruct((s, w), F32),
        scratch_shapes=[pltpu.VMEM((SUBLANES, w), F32), pltpu.VMEM((SUBLANES, w), F32),
                        pltpu.VMEM((ts, w), F32), pltpu.VMEM((ts, w), F32)],
        name="lru_fwd", compiler_params=_cp(1))(z1, cw8, vec8, wr, wi)


def _lru_bwd(z1, dhs, hs, cw8, vec8, wr, wi, wr_t, wi_t, ts):
    s = z1.shape[0]
    nb = s // ts
    w = TOK_WIDTH
    tiles = ts // SUBLANES

    def body(u_ref, up_ref, dhs_ref, hs_ref, hsp_ref, cw_ref, vec_ref, wr_ref, wi_ref,
             wrt_ref, wit_ref, du_ref, dwr_ref, dwi_ref, dvec_ref,
             cc_scr, cd_scr, a_scr, dh_scr):
        i = pl.program_id(0)

        @pl.when(i == 0)
        def _():
            cc_scr[...] = jnp.zeros_like(cc_scr)
            cd_scr[...] = jnp.zeros_like(cd_scr)
            dwr_ref[...] = jnp.zeros_like(dwr_ref)
            dwi_ref[...] = jnp.zeros_like(dwi_ref)
            dvec_ref[...] = jnp.zeros_like(dvec_ref)

        u = u_ref[...]
        first = i == nb - 1
        carry8 = jnp.where(first, 0.0, up_ref[...])
        xc, r, ig, cneg, a, sq, taps = _lru_gates(u, carry8, cw_ref, vec_ref, wr_ref, wi_ref)
        a_scr[...] = a

        def step(n, c):
            t = ts - 1 - n
            dh = dhs_ref[pl.ds(t, 1), :] + c
            dh_scr[pl.ds(t, 1), :] = dh
            return a_scr[pl.ds(t, 1), :] * dh

        cc_scr[0:1, :] = lax.fori_loop(0, ts, step, cc_scr[0:1, :])
        dh = dh_scr[...]
        hprev = _shift_down(hs_ref[...], jnp.where(first, 0.0, hsp_ref[...]), 1)
        ix = ig * xc
        dix = dh * sq
        dlog_a = dh * hprev * a - (dh * ix) * (a * a) / sq
        dpr = (dlog_a * cneg) * r * (1.0 - r)
        dpi = (dix * xc) * ig * (1.0 - ig)
        dprb, dpib = dpr.astype(BF16), dpi.astype(BF16)
        xb = xc.astype(BF16)
        dwr_ref[...] += _dot_tn(xb, dprb)
        dwi_ref[...] += _dot_tn(xb, dpib)
        dxc = dix * ig + _dot(dprb, wrt_ref[...]) + _dot(dpib, wit_ref[...])
        for k in range(CONV_W):
            dvec_ref[3 - k:4 - k, :] += jnp.sum(dxc * taps[k], axis=0, keepdims=True)
        dvec_ref[4:5, :] += jnp.sum(dxc, axis=0, keepdims=True)
        dvec_ref[5:6, :] += jnp.sum(dpr, axis=0, keepdims=True)
        dvec_ref[6:7, :] += jnp.sum(dpi, axis=0, keepdims=True)
        dvec_ref[7:8, :] += (jnp.sum(dlog_a * r, axis=0, keepdims=True)
                             * (LRU_C * _sigmoid(-vec_ref[3:4, :])))
        nxt = cd_scr[...]
        du = cw_ref[3:4, :] * dxc
        for k in range(1, CONV_W):
            du = du + cw_ref[3 - k:4 - k, :] * _shift_up(dxc, nxt, k)
        du_ref[...] = du.astype(BF16)
        cd_scr[...] = dxc[0:SUBLANES, :]

    rev = lambda i: (nb - 1 - i, 0)
    prev8 = lambda i: (jnp.maximum((nb - 1 - i) * tiles - 1, 0), 0)
    blk = pl.BlockSpec((ts, w), rev)
    before = pl.BlockSpec((SUBLANES, w), prev8)
    scr = pltpu.VMEM((ts, w), F32)
    return pl.pallas_call(
        body, grid=(nb,),
        in_specs=[blk, before, blk, blk, before,
                  _full((SUBLANES, w)), _full((SUBLANES, w)),
                  _full((w, w)), _full((w, w)), _full((w, w)), _full((w, w))],
        out_specs=[blk, _full((w, w)), _full((w, w)), _full((SUBLANES, w))],
        out_shape=[jax.ShapeDtypeStruct((s, w), BF16), jax.ShapeDtypeStruct((w, w), F32),
                   jax.ShapeDtypeStruct((w, w), F32), jax.ShapeDtypeStruct((SUBLANES, w), F32)],
        scratch_shapes=[pltpu.VMEM((SUBLANES, w), F32), pltpu.VMEM((SUBLANES, w), F32),
                        scr, scr],
        name="lru_bwd", compiler_params=_cp(1))(
            z1, z1, dhs, hs, hs, cw8, vec8, wr, wi, wr_t, wi_t)


def _adamw(parts, w, m, v, name):
    n = len(parts)
    rows_per = parts[0].shape[1]

    def body(*refs):
        p_refs = refs[:n]
        w_ref, m_ref, v_ref, g_ref, d_ref, nm_ref, nv_ref = refs[n:]
        for l, p_ref in enumerate(p_refs):
            rows = slice(l * rows_per, (l + 1) * rows_per)
            g = p_ref[0].astype(F32)
            for dev in range(1, N_DEV):
                g = g + p_ref[dev].astype(F32)
            g_ref[rows, :] = g
            nm = ADAM_B1 * m_ref[rows, :] + (1.0 - ADAM_B1) * g
            nv = ADAM_B2 * v_ref[rows, :] + (1.0 - ADAM_B2) * (g * g)
            m_hat = nm / (1.0 - ADAM_B1 ** ADAM_STEP)
            v_hat = nv / (1.0 - ADAM_B2 ** ADAM_STEP)
            d_ref[rows, :] = -ADAM_LR * (m_hat / (jnp.sqrt(v_hat) + ADAM_EPS)
                                         + ADAM_WD * w_ref[rows, :])
            nm_ref[rows, :] = nm
            nv_ref[rows, :] = nv

    out = jax.ShapeDtypeStruct(w.shape, F32)
    return pl.pallas_call(
        body, out_shape=[out] * 4, name=name,
        compiler_params=pltpu.CompilerParams(vmem_limit_bytes=VMEM_LIMIT))(*parts, w, m, v)


ANY = pl.BlockSpec(memory_space=pl.ANY)
MESH = pl.DeviceIdType.MESH


def _slot(p):
    return 4 * p[0] + 2 * p[1] + p[2]


def _allgather(xs):
    n = len(xs)

    def body(*refs):
        x_refs, o_refs = refs[:n], refs[n:2 * n]
        send_sems, recv_sems, local_sems = refs[2 * n:]
        x, y, c = lax.axis_index("x"), lax.axis_index("y"), lax.axis_index("c")
        me, sibling = (x, y, c), (x, y, 1 - c)
        chips = [(1 - x, y), (x, 1 - y), (1 - x, 1 - y)]

        def copy(a, k, block, to, from_input=False):
            dst = o_refs[a].at[_slot(block)]
            return pltpu.make_async_remote_copy(
                src_ref=x_refs[a] if from_input else dst, dst_ref=dst,
                send_sem=send_sems.at[a, k], recv_sem=recv_sems.at[a, k],
                device_id=to, device_id_type=MESH)

        mine = [pltpu.make_async_copy(x_refs[a], o_refs[a].at[_slot(me)], local_sems.at[a])
                for a in range(n)]
        for cp in mine:
            cp.start()
        first = []
        for a in range(n):
            first.append(copy(a, 0, me, sibling, True))
            first += [copy(a, 1 + j, me, (*chip, c), True) for j, chip in enumerate(chips)]
        for cp in first:
            cp.start()
        passed = []
        for j, chip in enumerate(chips):
            for a in range(n):
                copy(a, 1 + j, (*chip, c), me).wait_recv()
                cp = copy(a, 4 + j, (*chip, c), sibling)
                cp.start()
                passed.append(cp)
        for a in range(n):
            copy(a, 0, sibling, me).wait_recv()
            for j, chip in enumerate(chips):
                copy(a, 4 + j, (*chip, 1 - c), me).wait_recv()
        for cp in first + passed:
            cp.wait_send()
        for cp in mine:
            cp.wait()

    return pl.pallas_call(
        body,
        out_shape=[jax.ShapeDtypeStruct((N_DEV,) + t.shape, t.dtype) for t in xs],
        in_specs=[ANY] * n, out_specs=[ANY] * n,
        scratch_shapes=[pltpu.SemaphoreType.DMA((n, 7)), pltpu.SemaphoreType.DMA((n, 7)),
                        pltpu.SemaphoreType.DMA((n,))],
        name="allgather_weights")(*xs)


class _Exchange:
    def __init__(self, arrays, kinds):
        self.arrays, self.kinds, self.n = list(arrays), list(kinds), len(arrays)
        self.shapes = [self._part_shape(a, k) for a, k in zip(arrays, kinds)]
        self.out_shape = [jax.ShapeDtypeStruct((N_DEV,) + shp, a.dtype)
                          for shp, a in zip(self.shapes, arrays)]
        self.scratch = [pltpu.SemaphoreType.DMA((self.n, N_DEV - 1)),
                        pltpu.SemaphoreType.DMA((self.n, N_DEV - 1)),
                        pltpu.SemaphoreType.DMA((self.n,))]

    @staticmethod
    def _part_shape(arr, kind):
        if kind == "chunks":
            return arr.shape[1:]
        if kind == "cols":
            return (arr.shape[0], arr.shape[1] // N_DEV)
        if kind == "rows":
            return (arr.shape[0] // N_DEV, arr.shape[1])
        return arr.shape

    def copies(self, in_refs, out_refs, sems):
        send_sems, recv_sems, local_sems = sems
        x, y, c = lax.axis_index("x"), lax.axis_index("y"), lax.axis_index("c")
        me = _slot((x, y, c))

        def part(a, dev):
            ref, kind, shp = in_refs[a], self.kinds[a], self.shapes[a]
            if kind == "chunks":
                return ref.at[dev]
            if kind == "cols":
                return ref.at[:, pl.ds(pl.multiple_of(dev * shp[1], LANES), shp[1])]
            if kind == "rows":
                return ref.at[pl.ds(pl.multiple_of(dev * shp[0], SUBLANES), shp[0]), :]
            return ref

        cps = [pltpu.make_async_copy(part(a, me), out_refs[a].at[me], local_sems.at[a])
               for a in range(self.n)]
        for rel in range(1, N_DEV):
            peer = (x ^ (rel >> 2), y ^ ((rel >> 1) & 1), c ^ (rel & 1))
            for a in range(self.n):
                cps.append(pltpu.make_async_remote_copy(
                    src_ref=part(a, _slot(peer)), dst_ref=out_refs[a].at[me],
                    send_sem=send_sems.at[a, rel - 1], recv_sem=recv_sems.at[a, rel - 1],
                    device_id=peer, device_id_type=MESH))
        return cps


def _exchange_grads(arrays, kinds, name):
    ex = _Exchange(arrays, kinds)
    n = ex.n

    def body(*refs):
        cps = ex.copies(refs[:n], refs[n:2 * n], refs[2 * n:])
        for cp in cps:
            cp.start()
        for cp in cps:
            cp.wait()

    return pl.pallas_call(
        body, out_shape=ex.out_shape, in_specs=[ANY] * n, out_specs=[ANY] * n,
        scratch_shapes=ex.scratch, name=name)(*arrays)


BIG = [("mla_w_in", (D_MODEL, MLA_IN), 1), ("mla_w_uq", (Q_LORA, N_TOK_HEADS * QK_DIM), 1),
       ("mla_w_ukv", (KV_LORA, N_TOK_HEADS * 2 * HEAD_DIM), 1), ("lru_w_in", (D_MODEL, LRU_IN), 1),
       ("w_mem_kv", (2, D_MODEL, 2 * MEM_WIDTH), 1), ("w_out", (2, MIX_WIDTH, D_MODEL), 1)]
SMALL = [("lru_conv_w", (CONV_W, TOK_WIDTH), 1), ("lru_conv_b", (TOK_WIDTH,), 0),
         ("lru_b_rgate", (TOK_WIDTH,), 0), ("lru_b_igate", (TOK_WIDTH,), 0),
         ("lru_lambda", (TOK_WIDTH,), 0)]
REPL = [("mla_q_norm", (Q_LORA,)), ("mla_kv_norm", (KV_LORA,)),
        ("lru_w_rgate", (N_TOK_HEADS, HEAD_DIM, HEAD_DIM)),
        ("lru_w_igate", (N_TOK_HEADS, HEAD_DIM, HEAD_DIM)),
        ("ln_g", (2, D_MODEL)), ("ln_b", (2, D_MODEL))]


def _shard_shape(shape, axis):
    return tuple(d // N_DEV if a == axis else d for a, d in enumerate(shape))


def _size(shape):
    return math.prod(shape)


BIG_ROWS = sum(_size(s) for _, s, _ in BIG) // N_DEV // LANES
SMALL_ROWS = SUBLANES


def _pack_rows(flat_parts, rows):
    flat = jnp.concatenate([p.reshape(-1) for p in flat_parts])
    return jnp.pad(flat, (0, rows * LANES - flat.shape[0])).reshape(rows, LANES)


def _to_chunks(full, axis):
    shape = full.shape
    split = shape[:axis] + (N_DEV, shape[axis] // N_DEV) + shape[axis + 1:]
    return jnp.moveaxis(full.reshape(split), axis, 0).reshape(N_DEV, -1)


def _from_chunks(chunks, shape, axis):
    sh = _shard_shape(shape, axis)
    t = chunks.reshape((N_DEV,) + sh)
    t = jnp.moveaxis(t, 0, axis)
    return t.reshape(shape)


def _split_flat(flat2d, table):
    out, off = [], 0
    for size in table:
        out.append(flat2d[:, off:off + size])
        off += size
    return out


def _win0_to_padded(w):
    z = lambda n: jnp.zeros((w.shape[0], n), w.dtype)
    return jnp.concatenate([w[:, 0:640], z(KR_LANE), w[:, 640:672],
                            z(LANES - KR_LANE - QK_ROPE), w[:, 672:1952]], axis=1)


def _win0_from_padded(wp):
    k0 = ZA_KR + KR_LANE
    return jnp.concatenate([wp[:, 0:640], wp[:, k0:k0 + QK_ROPE], wp[:, ZA_W:ZP]], axis=1)


def _pad_heads(w, per_head, lo, hi):
    t = w.reshape(w.shape[0], N_TOK_HEADS, per_head)[:, :, lo:hi]
    t = jnp.pad(t, ((0, 0), (0, 0), (0, HEAD_PAD - (hi - lo))))
    return t.reshape(w.shape[0], QKV_PAD)


def _unpad_heads(wp, width):
    return wp.reshape(wp.shape[0], N_TOK_HEADS, HEAD_PAD)[:, :, :width]


def _block_diag(w):
    eye = jnp.eye(N_TOK_HEADS, dtype=w.dtype)
    return (w[:, :, None, :] * eye[:, None, :, None]).reshape(TOK_WIDTH, TOK_WIDTH)


def _diag_blocks(d):
    t = d.reshape(N_TOK_HEADS, HEAD_DIM, N_TOK_HEADS, HEAD_DIM)
    return jnp.stack([t[g, :, g, :] for g in range(N_TOK_HEADS)])


def _rope_tables(positions):
    half = QK_ROPE // 2
    inv_freq = ROPE_THETA ** (-jnp.arange(half, dtype=F32) / half)
    ang = positions.astype(F32)[:, None] * inv_freq
    cos, sin = jnp.cos(ang), jnp.sin(ang)
    s = positions.shape[0]
    one, zero = jnp.ones((s, QK_NOPE), F32), jnp.zeros((s, half), F32)
    tail = jnp.zeros((s, HEAD_PAD - QK_DIM), F32)
    znope = jnp.zeros((s, QK_NOPE), F32)
    c = jnp.concatenate([one, cos, cos, tail], axis=1)
    sa = jnp.concatenate([znope, -sin, zero, tail], axis=1)
    sb = jnp.concatenate([znope, zero, sin, tail], axis=1)
    return c, sa, sb


def _local_step(x, mem, positions, tgt, wts, ts, tatt, early_exchange):
    bf = lambda t: t.astype(BF16)
    win0 = _win0_to_padded(wts["mla_w_in"])
    wuq = _pad_heads(wts["mla_w_uq"], QK_DIM, 0, QK_DIM)
    wukv = jnp.concatenate([_pad_heads(wts["mla_w_ukv"], 2 * HEAD_DIM, 0, QK_NOPE),
                            _pad_heads(wts["mla_w_ukv"], 2 * HEAD_DIM, QK_NOPE, 2 * HEAD_DIM)],
                           axis=1)
    win1 = wts["lru_w_in"]
    wmkv, wout = wts["w_mem_kv"], wts["w_out"]
    gq = wts["mla_q_norm"].reshape(1, Q_LORA)
    gkv = wts["mla_kv_norm"].reshape(1, KV_LORA)
    ln_g, ln_b = wts["ln_g"], wts["ln_b"]
    wr, wi = bf(_block_diag(wts["lru_w_rgate"])), bf(_block_diag(wts["lru_w_igate"]))
    cw8 = jnp.pad(wts["lru_conv_w"], ((0, SUBLANES - CONV_W), (0, 0)))
    vec8 = jnp.pad(jnp.stack([wts["lru_conv_b"], wts["lru_b_rgate"], wts["lru_b_igate"],
                              wts["lru_lambda"]]), ((0, SUBLANES - 4), (0, 0)))
    tabs = _rope_tables(positions)
    tmem = mem.shape[0]

    za0, zg0 = _rowmm(x, win0, [ZA_W, ZG_W], "in_proj0", ts)
    q, k, v = _mla_prep_fwd(za0, tabs, gq, gkv, wuq, wukv, ts)
    o, lse = _flash_fwd(q, k, v, tatt, FWD_HEADS)
    mkv0, = _rowmm(mem, wmkv[0], [2 * MEM_WIDTH], "mem_kv0", tmem)
    cat0, y0 = _gate_mem_fwd(o, zg0, mkv0, 0, MIX_WIDTH, True, "gate_mem_fwd0", ts)
    del o
    pre0, h1 = _outproj_ln_fwd(y0, wout[0], x, ln_g[0:1], ln_b[0:1], None, "outproj_ln_fwd0", ts)
    u1, zg1 = _rowmm(h1, win1, [ZA_W, ZG_W], "in_proj1", ts)
    hs = _lru_fwd(u1, cw8, vec8, wr, wi, ts)
    mkv1, = _rowmm(mem, wmkv[1], [2 * MEM_WIDTH], "mem_kv1", tmem)
    cat1, y1 = _gate_mem_fwd(hs, zg1, mkv1, 0, MIX_WIDTH, False, "gate_mem_fwd1", ts)
    pre1, dh2, loss8 = _outproj_ln_fwd(y1, wout[1], h1, ln_g[1:2], ln_b[1:2], tgt,
                                       "outproj_ln_loss", ts)
    loss = loss8[0, 0]

    dpre1, dy1, dwout1, dgb1 = _outproj_ln_bwd(dh2, pre1, ln_g[1:2], y1, wout[1].T,
                                               "outproj_ln_bwd1", ts)
    dzg1, dhs, dmkv1 = _gate_mem_bwd(dy1, cat1, zg1, mkv1, None, 0, MIX_WIDTH,
                                     "gate_mem_bwd1", ts)
    du, dwr, dwi, dvec = _lru_bwd(u1, dhs, hs, cw8, vec8, wr, wi, wr.T, wi.T, ts)
    dh1, dwin1 = _linear_bwd(h1, [du, dzg1], [0, ZA_W], win1.T, dpre1, "in_proj_bwd1", ts)
    dwmkv1 = _wgrad_small(mem, dmkv1, "mem_kv_bwd1")
    dpre0, dy0, dwout0, dgb0 = _outproj_ln_bwd(dh1, pre0, ln_g[0:1], y0, wout[0].T,
                                               "outproj_ln_bwd0", ts)
    dzg0, do, dmkv0, stats = _gate_mem_bwd(dy0, cat0, zg0, mkv0, lse, 0, MIX_WIDTH,
                                           "gate_mem_bwd0", ts)
    dwmkv0 = _wgrad_small(mem, dmkv0, "mem_kv_bwd0")
    early = {
        "lru_w_in": dwin1,
        "lru_small": dvec,
        "lru_w_rgate": _diag_blocks(dwr).reshape(TOK_WIDTH, HEAD_DIM),
        "lru_w_igate": _diag_blocks(dwi).reshape(TOK_WIDTH, HEAD_DIM),
        "w_mem_kv": [dwmkv0, dwmkv1],
        "w_out": [dwout0, dwout1],
    }
    (dq, dk, dv), got_early = _flash_bwd(q, k, v, stats, do, tatt, BWD_HEADS,
                                         early_exchange(early))
    dza, dzk, dwuq_p, dwukv_p, dg = _mla_prep_bwd(za0, dq, dk, dv, tabs, gq, gkv,
                                                  wuq.T, wukv.T, ts)
    gx, dwin0_p = _linear_bwd(x, [dza, dzk, dzg0], [ZA_CQ, ZA_KR, ZA_W], win0.T, dpre0,
                              "in_proj_bwd0", ts)

    dwukv = jnp.concatenate([_unpad_heads(dwukv_p[:, :QKV_PAD], HEAD_DIM),
                             _unpad_heads(dwukv_p[:, QKV_PAD:], HEAD_DIM)], axis=2)
    zrow = jnp.zeros((1, D_MODEL), F32)
    gains = jnp.pad(dg[0:1], ((0, 0), (0, D_MODEL - Q_LORA - KV_LORA)))
    small_repl = jnp.concatenate([dgb0[0:2], dgb1[0:2], gains,
                                  loss * jnp.ones((1, D_MODEL), F32), zrow, zrow], axis=0)
    late = {
        "mla_w_in": _win0_from_padded(dwin0_p),
        "mla_w_uq": _unpad_heads(dwuq_p, QK_DIM).reshape(Q_LORA, N_TOK_HEADS * QK_DIM),
        "mla_w_ukv": dwukv.reshape(KV_LORA, N_TOK_HEADS * 2 * HEAD_DIM),
        "small_repl": small_repl,
    }
    return gx, early, got_early, late


WEIGHT_ORDER = ["mla_w_in", "mla_q_norm", "mla_w_uq", "mla_kv_norm", "mla_w_ukv", "lru_w_in",
                "lru_conv_w", "lru_conv_b", "lru_w_rgate", "lru_b_rgate", "lru_w_igate",
                "lru_b_igate", "lru_lambda", "w_mem_kv", "w_out", "ln_g", "ln_b"]


def kernel(x, mem, positions, mla_w_in, mla_q_norm, mla_w_uq, mla_kv_norm, mla_w_ukv, lru_w_in, lru_conv_w, lru_conv_b, lru_w_rgate, lru_b_rgate, lru_w_igate, lru_b_igate, lru_lambda, w_mem_kv, w_out, ln_g, ln_b, loss_target, m_mla_w_in, m_mla_q_norm, m_mla_w_uq, m_mla_kv_norm, m_mla_w_ukv, m_lru_w_in, m_lru_conv_w, m_lru_conv_b, m_lru_w_rgate, m_lru_b_rgate, m_lru_w_igate, m_lru_b_igate, m_lru_lambda, m_w_mem_kv, m_w_out, m_ln_g, m_ln_b, v_mla_w_in, v_mla_q_norm, v_mla_w_uq, v_mla_kv_norm, v_mla_w_ukv, v_lru_w_in, v_lru_conv_w, v_lru_conv_b, v_lru_w_rgate, v_lru_b_rgate, v_lru_w_igate, v_lru_b_igate, v_lru_lambda, v_w_mem_kv, v_w_out, v_ln_g, v_ln_b):
    w_in = dict(mla_w_in=mla_w_in, mla_q_norm=mla_q_norm, mla_w_uq=mla_w_uq,
                mla_kv_norm=mla_kv_norm, mla_w_ukv=mla_w_ukv, lru_w_in=lru_w_in,
                lru_conv_w=lru_conv_w, lru_conv_b=lru_conv_b, lru_w_rgate=lru_w_rgate,
                lru_b_rgate=lru_b_rgate, lru_w_igate=lru_w_igate, lru_b_igate=lru_b_igate,
                lru_lambda=lru_lambda, w_mem_kv=w_mem_kv, w_out=w_out, ln_g=ln_g, ln_b=ln_b)
    m_in = dict(mla_w_in=m_mla_w_in, mla_q_norm=m_mla_q_norm, mla_w_uq=m_mla_w_uq,
                mla_kv_norm=m_mla_kv_norm, mla_w_ukv=m_mla_w_ukv, lru_w_in=m_lru_w_in,
                lru_conv_w=m_lru_conv_w, lru_conv_b=m_lru_conv_b, lru_w_rgate=m_lru_w_rgate,
                lru_b_rgate=m_lru_b_rgate, lru_w_igate=m_lru_w_igate, lru_b_igate=m_lru_b_igate,
                lru_lambda=m_lru_lambda, w_mem_kv=m_w_mem_kv, w_out=m_w_out, ln_g=m_ln_g,
                ln_b=m_ln_b)
    v_in = dict(mla_w_in=v_mla_w_in, mla_q_norm=v_mla_q_norm, mla_w_uq=v_mla_w_uq,
                mla_kv_norm=v_mla_kv_norm, mla_w_ukv=v_mla_w_ukv, lru_w_in=v_lru_w_in,
                lru_conv_w=v_lru_conv_w, lru_conv_b=v_lru_conv_b, lru_w_rgate=v_lru_w_rgate,
                lru_b_rgate=v_lru_b_rgate, lru_w_igate=v_lru_w_igate, lru_b_igate=v_lru_b_igate,
                lru_lambda=v_lru_lambda, w_mem_kv=v_w_mem_kv, w_out=v_w_out, ln_g=v_ln_g,
                ln_b=v_ln_b)
    s = x.shape[1]
    ts = min(ROW_BLOCK, s)
    tatt = min(ATT_BLOCK, s)
    big_sizes = [_size(sh) // N_DEV for _, sh, _ in BIG]
    small_sizes = [_size(sh) // N_DEV for _, sh, _ in SMALL]

    big_local = _pack_rows([w_in[n] for n, _, _ in BIG], BIG_ROWS).astype(BF16)
    small_local = _pack_rows([w_in[n] for n, _, _ in SMALL], SMALL_ROWS)
    big_all, small_all = _allgather([big_local, small_local])
    wts = {}
    for (n, sh, ax), part in zip(BIG, _split_flat(big_all.reshape(N_DEV, -1), big_sizes)):
        wts[n] = _from_chunks(part, sh, ax)
    for (n, sh, ax), part in zip(SMALL, _split_flat(small_all.reshape(N_DEV, -1), small_sizes)):
        wts[n] = _from_chunks(part, sh, ax)
    for n, sh in REPL:
        wts[n] = w_in[n].reshape(sh)

    def early_exchange(g):
        small_chunks = jnp.moveaxis(g["lru_small"].reshape(SUBLANES, N_DEV, -1), 1, 0)
        sends = [(g["lru_w_in"], "cols"),
                 (g["w_mem_kv"][0], "rows"), (g["w_mem_kv"][1], "rows"),
                 (g["w_out"][0], "rows"), (g["w_out"][1], "rows"),
                 (small_chunks, "chunks"), (g["lru_w_rgate"], "all"), (g["lru_w_igate"], "all")]
        return _Exchange([a for a, _ in sends], [k for _, k in sends])

    gx, _, got_early, late = _local_step(x[0], mem[0], positions[0], loss_target[0], wts,
                                         ts, tatt, early_exchange)

    def chunked(name, shape):
        w = shape[1] // N_DEV
        return _to_chunks(late[name], 1).reshape(N_DEV, shape[0], w).astype(BF16)

    got_late = _exchange_grads(
        [chunked("mla_w_in", (D_MODEL, MLA_IN)),
         chunked("mla_w_uq", (Q_LORA, N_TOK_HEADS * QK_DIM)),
         chunked("mla_w_ukv", (KV_LORA, N_TOK_HEADS * 2 * HEAD_DIM)), late["small_repl"]],
        ["chunks", "chunks", "chunks", "all"], "exchange_grads")
    got = list(got_late[:3]) + list(got_early) + [got_late[3]]

    def small_sharded(d):
        return jnp.concatenate([d["lru_conv_w"].reshape(CONV_W, -1), d["lru_conv_b"],
                                d["lru_b_rgate"], d["lru_b_igate"], d["lru_lambda"]], axis=0)

    def small_replicated(d):
        gains = jnp.concatenate([d["mla_q_norm"], d["mla_kv_norm"]], axis=1)
        gains = jnp.pad(gains, ((0, 0), (0, D_MODEL - gains.shape[1])))
        return jnp.concatenate([d["ln_g"][0:1], d["ln_b"][0:1], d["ln_g"][1:2], d["ln_b"][1:2],
                                gains, jnp.zeros((3, D_MODEL), F32)], axis=0)

    def flat2(d, name):
        t = d[name]
        return t.reshape(-1, t.shape[-1])

    def update(parts, view, name):
        return _adamw(parts, view(w_in), view(m_in), view(v_in), "adamw_" + name)

    res = {}
    for idx, name in [(0, "mla_w_in"), (1, "mla_w_uq"), (2, "mla_w_ukv"), (3, "lru_w_in"),
                      (9, "lru_w_rgate"), (10, "lru_w_igate")]:
        res[name] = update([got[idx]], functools.partial(flat2, name=name), name)
    res["w_mem_kv"] = update([got[4], got[5]], functools.partial(flat2, name="w_mem_kv"),
                             "w_mem_kv")
    res["w_out"] = update([got[6], got[7]], functools.partial(flat2, name="w_out"), "w_out")
    res_ss = update([got[8]], small_sharded, "small_sharded")
    res_sr = update([got[11]], small_replicated, "small_replicated")
    loss = res_sr[0][5, 0]

    result = [loss, gx.reshape(x.shape)]
    for kind in range(4):
        ss, sr = res_ss[kind], res_sr[kind]
        out = {n: res[n][kind].reshape(w_in[n].shape) for n in res}
        out["lru_conv_w"] = ss[0:CONV_W].reshape(w_in["lru_conv_w"].shape)
        out["lru_conv_b"], out["lru_b_rgate"] = ss[4:5], ss[5:6]
        out["lru_b_igate"], out["lru_lambda"] = ss[6:7], ss[7:8]
        out["ln_g"] = jnp.concatenate([sr[0:1], sr[2:3]], axis=0)
        out["ln_b"] = jnp.concatenate([sr[1:2], sr[3:4]], axis=0)
        out["mla_q_norm"] = sr[4:5, 0:Q_LORA]
        out["mla_kv_norm"] = sr[4:5, Q_LORA:Q_LORA + KV_LORA]
        result += [out[n] for n in WEIGHT_ORDER]
    return tuple(result)
```

```python
import functools
import math

import jax
import jax.numpy as jnp
from jax import lax
from jax.experimental import pallas as pl
from jax.experimental.pallas import tpu as pltpu

F32 = jnp.float32
BF16 = jnp.bfloat16

D_MODEL = 1024
MEM_LEN = 256
HEAD_DIM = 64
N_TOK_HEADS = 12
N_MEM_HEADS = 4
TOK_WIDTH = 768
MEM_WIDTH = 256
MIX_WIDTH = 1024
Q_LORA = 384
KV_LORA = 256
QK_NOPE = 64
QK_ROPE = 32
QK_DIM = 96
ROPE_THETA = 10000.0
CONV_W = 4
LRU_C = 8.0
ALPHA = (2.0 * 2) ** 0.25
NORM_EPS = 1e-6
MLA_IN = 1952
LRU_IN = 2048
ADAM_LR = 0.001
ADAM_B1 = 0.9
ADAM_B2 = 0.999
ADAM_EPS = 1e-08
ADAM_WD = 0.01
ADAM_STEP = 10

N_DEV = 8
LANES = 128
SUBLANES = 8
HEAD_PAD = 128
QKV_PAD = N_TOK_HEADS * HEAD_PAD
ZP = 2048
ZA_W = TOK_WIDTH
ZG_W = MIX_WIDTH + MEM_WIDTH
ZA_CQ, ZA_CKV, ZA_KR = 0, 384, 640
KR_LANE = 64

ROW_BLOCK = 512
ATT_BLOCK = 512
LOOKAHEAD = 3
FWD_HEADS = 12
BWD_HEADS = 4
VMEM_LIMIT = 56 * 1024 * 1024
NEG_BIG = -1e30
STRIP = 32
LOG2E = math.log2(math.e)


def _cp(n_axes):
    return pltpu.CompilerParams(dimension_semantics=("arbitrary",) * n_axes,
                                vmem_limit_bytes=VMEM_LIMIT)


def _dot(a, b):
    return jnp.dot(a, b, preferred_element_type=F32)


def _dot_nt(a, b):
    return lax.dot_general(a, b, (((1,), (1,)), ((), ())), preferred_element_type=F32)


def _dot_tn(a, b):
    return lax.dot_general(a, b, (((0,), (0,)), ((), ())), preferred_element_type=F32)


def _sigmoid(t):
    return 1.0 / (1.0 + jnp.exp(-t))


def _lane(shape):
    return lax.broadcasted_iota(jnp.int32, shape, len(shape) - 1)


def _full(shape):
    nd = len(shape)
    return pl.BlockSpec(shape, lambda *_: (0,) * nd)


def _rows(ts, width, col=0):
    return pl.BlockSpec((ts, width), lambda i: (i, col))


def _heads(ts):
    return pl.BlockSpec((N_TOK_HEADS, ts, HEAD_PAD), lambda i: (0, i, 0))


def _rowmm(x, w, widths, name, ts):
    s, k = x.shape
    n = w.shape[1]
    offs = [sum(widths[:a]) for a in range(len(widths))]

    def body(x_ref, w_ref, *o_refs):
        res = _dot(x_ref[...].astype(BF16), w_ref[...])
        for o_ref, off, wd in zip(o_refs, offs, widths):
            o_ref[...] = res[:, off:off + wd]

    return pl.pallas_call(
        body, grid=(s // ts,),
        in_specs=[_rows(ts, k), _full((k, n))],
        out_specs=[_rows(ts, wd) for wd in widths],
        out_shape=[jax.ShapeDtypeStruct((s, wd), F32) for wd in widths],
        name=name, compiler_params=_cp(1))(x, w)


def _rms_parts(t):
    rs = lax.rsqrt(jnp.mean(t * t, axis=-1, keepdims=True) + NORM_EPS)
    return rs


def _rope(t, c, sa, sb):
    return t * c + pltpu.roll(t, LANES - 16, 1) * sa + pltpu.roll(t, 16, 1) * sb


def _rope_t(d, c, sa, sb):
    return d * c + pltpu.roll(d * sa, 16, 1) + pltpu.roll(d * sb, LANES - 16, 1)


def _mla_prep_fwd(z0, tabs, gq, gkv, wuq, wukv, ts):
    s = z0.shape[0]

    def body(z_ref, c_ref, sa_ref, sb_ref, gq_ref, gkv_ref, wuq_ref, wukv_ref,
             q_ref, k_ref, v_ref):
        cq = z_ref[:, ZA_CQ:ZA_CQ + Q_LORA]
        ckv = z_ref[:, ZA_CKV:ZA_CKV + KV_LORA]
        kr = z_ref[:, ZA_KR:ZA_KR + LANES]
        cqn = cq * _rms_parts(cq) * gq_ref[...]
        ckvn = ckv * _rms_parts(ckv) * gkv_ref[...]
        q = _dot(cqn.astype(BF16), wuq_ref[...])
        kv = _dot(ckvn.astype(BF16), wukv_ref[...])
        c, sa, sb = c_ref[...], sa_ref[...], sb_ref[...]
        krope = _rope(kr, c, sa, sb)
        pad_lane = _lane((ts, HEAD_PAD)) >= HEAD_DIM
        for h in range(N_TOK_HEADS):
            sl = slice(h * HEAD_PAD, (h + 1) * HEAD_PAD)
            q_ref[h] = _rope(q[:, sl], c, sa, sb).astype(BF16)
            k_ref[h] = (kv[:, sl] + krope).astype(BF16)
            vh = kv[:, QKV_PAD + h * HEAD_PAD:QKV_PAD + (h + 1) * HEAD_PAD]
            v_ref[h] = jnp.where(pad_lane, 1.0, vh).astype(BF16)

    out = jax.ShapeDtypeStruct((N_TOK_HEADS, s, HEAD_PAD), BF16)
    return pl.pallas_call(
        body, grid=(s // ts,),
        in_specs=[_rows(ts, ZA_W), _rows(ts, LANES), _rows(ts, LANES), _rows(ts, LANES),
                  _full((1, Q_LORA)), _full((1, KV_LORA)),
                  _full((Q_LORA, QKV_PAD)), _full((KV_LORA, 2 * QKV_PAD))],
        out_specs=[_heads(ts)] * 3,
        out_shape=[out, out, out],
        name="mla_prep_fwd", compiler_params=_cp(1))(z0, *tabs, gq, gkv, wuq, wukv)


def _mla_prep_bwd(z0, dq, dk, dv, tabs, gq, gkv, wuq_t, wukv_t, ts):
    s = z0.shape[0]

    def body(z_ref, dq_ref, dk_ref, dv_ref, c_ref, sa_ref, sb_ref, gq_ref, gkv_ref,
             wuqt_ref, wukvt_ref, dza_ref, dzk_ref, dwuq_ref, dwukv_ref, dg_ref):
        @pl.when(pl.program_id(0) == 0)
        def _():
            dwuq_ref[...] = jnp.zeros_like(dwuq_ref)
            dwukv_ref[...] = jnp.zeros_like(dwukv_ref)
            dg_ref[...] = jnp.zeros_like(dg_ref)

        cq = z_ref[:, ZA_CQ:ZA_CQ + Q_LORA]
        ckv = z_ref[:, ZA_CKV:ZA_CKV + KV_LORA]
        rq, rkv = _rms_parts(cq), _rms_parts(ckv)
        gq_, gkv_ = gq_ref[...], gkv_ref[...]
        cqn = (cq * rq * gq_).astype(BF16)
        ckvn = (ckv * rkv * gkv_).astype(BF16)
        c, sa, sb = c_ref[...], sa_ref[...], sb_ref[...]
        dqp, dksum = [], None
        for h in range(N_TOK_HEADS):
            dqp.append(_rope_t(dq_ref[h], c, sa, sb))
            dksum = dk_ref[h] if dksum is None else dksum + dk_ref[h]
        dqp = jnp.concatenate(dqp, axis=1).astype(BF16)
        lane = _lane(dksum.shape)
        dzk_ref[...] = jnp.where((lane >= KR_LANE) & (lane < KR_LANE + QK_ROPE),
                                 _rope_t(dksum, c, sa, sb), 0.0).astype(BF16)
        dkv = jnp.concatenate([dk_ref[h].astype(BF16) for h in range(N_TOK_HEADS)]
                              + [dv_ref[h] for h in range(N_TOK_HEADS)], axis=1)
        dcqn = _dot(dqp, wuqt_ref[...])
        dckvn = _dot(dkv, wukvt_ref[...])
        dwuq_ref[...] += _dot_tn(cqn, dqp)
        dwukv_ref[...] += _dot_tn(ckvn, dkv)
        dg_ref[0:1, 0:Q_LORA] += jnp.sum(dcqn * cq * rq, axis=0, keepdims=True)
        dg_ref[0:1, Q_LORA:Q_LORA + KV_LORA] += jnp.sum(dckvn * ckv * rkv, axis=0, keepdims=True)
        wq = dcqn * gq_
        wkv = dckvn * gkv_
        dcq = rq * wq - cq * (rq * rq * rq) * jnp.mean(wq * cq, axis=-1, keepdims=True)
        dckv = rkv * wkv - ckv * (rkv * rkv * rkv) * jnp.mean(wkv * ckv, axis=-1, keepdims=True)
        dza_ref[:, 0:Q_LORA] = dcq.astype(BF16)
        dza_ref[:, Q_LORA:Q_LORA + KV_LORA] = dckv.astype(BF16)

    na = Q_LORA + KV_LORA
    return pl.pallas_call(
        body, grid=(s // ts,),
        in_specs=[_rows(ts, ZA_W), _heads(ts), _heads(ts), _heads(ts),
                  _rows(ts, LANES), _rows(ts, LANES), _rows(ts, LANES),
                  _full((1, Q_LORA)), _full((1, KV_LORA)),
                  _full((QKV_PAD, Q_LORA)), _full((2 * QKV_PAD, KV_LORA))],
        out_specs=[_rows(ts, na), _rows(ts, LANES), _full((Q_LORA, QKV_PAD)),
                   _full((KV_LORA, 2 * QKV_PAD)), _full((SUBLANES, na))],
        out_shape=[jax.ShapeDtypeStruct((s, na), BF16), jax.ShapeDtypeStruct((s, LANES), BF16),
                   jax.ShapeDtypeStruct((Q_LORA, QKV_PAD), F32),
                   jax.ShapeDtypeStruct((KV_LORA, 2 * QKV_PAD), F32),
                   jax.ShapeDtypeStruct((SUBLANES, na), F32)],
        name="mla_prep_bwd", compiler_params=_cp(1))(
            z0, dq, dk, dv, *tabs, gq, gkv, wuq_t, wukv_t)


def _causal_pairs(nb, by_key):
    if by_key:
        pairs = [(i, j) for j in range(nb) for i in range(j, nb)]
    else:
        pairs = [(i, j) for i in range(nb) for j in range(i + 1)]
    return (jnp.array([p[0] for p in pairs], jnp.int32),
            jnp.array([p[1] for p in pairs], jnp.int32))


def _flash_fwd(q, k, v, t, nh):
    s = q.shape[1]
    itab, jtab = _causal_pairs(s // t, False)
    c2 = LOG2E / math.sqrt(QK_DIM)

    def body(it_ref, jt_ref, q_ref, k_ref, v_ref, o_ref, lse_ref, m_scr, acc_scr):
        pair = pl.program_id(1)
        i, j = it_ref[pair], jt_ref[pair]

        @pl.when(j == 0)
        def _():
            m_scr[...] = jnp.full_like(m_scr, NEG_BIG)
            acc_scr[...] = jnp.zeros_like(acc_scr)

        def softmax_strips(masked, hs, sc, row0):
            ps, als = [], []
            for r0 in range(0, sc.shape[0], STRIP):
                rows = slice(row0 + r0, row0 + r0 + STRIP)
                ch = [sc[r0:r0 + STRIP, n * LANES:(n + 1) * LANES] * c2
                      for n in range(sc.shape[1] // LANES)]
                if masked:
                    rr = row0 + r0 + lax.broadcasted_iota(jnp.int32, (STRIP, LANES), 0)
                    cc = lax.broadcasted_iota(jnp.int32, (STRIP, LANES), 1)
                    ch = [jnp.where(cc + n * LANES <= rr, c_, NEG_BIG) for n, c_ in enumerate(ch)]
                mx = ch[0]
                for c_ in ch[1:]:
                    mx = jnp.maximum(mx, c_)
                m_prev = m_scr[hs, rows, :]
                m_next = jnp.maximum(m_prev, jnp.max(mx, axis=-1, keepdims=True))
                ps.append(jnp.concatenate(
                    [jnp.exp2(c_ - m_next).astype(BF16) for c_ in ch], axis=1))
                als.append(jnp.exp2(m_prev - m_next))
                m_scr[hs, rows, :] = m_next
            return jnp.concatenate(ps, axis=0), jnp.concatenate(als, axis=0)

        def run(masked, parts):
            def scores_of(hs):
                return [_dot_nt(q_ref[hs, r0:r0 + nr, :], k_ref[hs, 0:nk, :])
                        for r0, nr, nk in parts]

            ahead = min(LOOKAHEAD, nh)
            scores = [scores_of(hs) for hs in range(ahead)]
            for hs in range(nh):
                if hs + ahead < nh:
                    scores.append(scores_of(hs + ahead))
                for (r0, nr, nk), sc in zip(parts, scores[hs]):
                    p, alpha = softmax_strips(masked, hs, sc, r0)
                    acc_scr[hs, r0:r0 + nr, :] = (alpha * acc_scr[hs, r0:r0 + nr, :]
                                                  + _dot(p, v_ref[hs, 0:nk, :]))

        @pl.when(j < i)
        def _():
            run(False, [(0, t, t)])

        @pl.when(j == i)
        def _():
            run(True, [(0, t, t)])
            for h in range(nh):
                acc = acc_scr[h]
                l = acc[:, HEAD_DIM:HEAD_DIM + 1]
                o_ref[h] = jnp.where(_lane(acc.shape) < HEAD_DIM, acc / l, 0.0)
                lse_ref[h] = m_scr[h] + jnp.log2(l)

    qspec = pl.BlockSpec((nh, t, HEAD_PAD), lambda h, p, it, jt: (h, it[p], 0))
    kspec = pl.BlockSpec((nh, t, HEAD_PAD), lambda h, p, it, jt: (h, jt[p], 0))
    out = jax.ShapeDtypeStruct((N_TOK_HEADS, s, HEAD_PAD), F32)
    return pl.pallas_call(
        body,
        grid_spec=pltpu.PrefetchScalarGridSpec(
            num_scalar_prefetch=2, grid=(N_TOK_HEADS // nh, itab.shape[0]),
            in_specs=[qspec, kspec, kspec], out_specs=[qspec, qspec],
            scratch_shapes=[pltpu.VMEM((nh, t, HEAD_PAD), F32)] * 2),
        out_shape=[out, out],
        name="flash_fwd", compiler_params=_cp(2))(itab, jtab, q, k, v)


def _flash_bwd(q, k, v, stats, do, t, nh, ex):
    s = q.shape[1]
    nb = s // t
    itab, jtab = _causal_pairs(nb, True)
    npairs = itab.shape[0]
    ngroups = N_TOK_HEADS // nh
    scale = 1.0 / math.sqrt(QK_DIM)
    c2 = LOG2E * scale
    nx = ex.n if ex is not None else 0
    ex_arrays, ex_out_shape, ex_scratch = (
        (ex.arrays, ex.out_shape, ex.scratch) if ex is not None else ([], [], []))

    def body(it_ref, jt_ref, q_ref, k_ref, v_ref, st_ref, do_ref, *rest):
        ex_in, rest = rest[:nx], rest[nx:]
        dq_ref, dk_ref, dv_ref = rest[:3]
        ex_out, rest = rest[3:3 + nx], rest[3 + nx:]
        dk_scr, dv_scr = rest[:2]
        ex_sems = rest[2:]
        pair = pl.program_id(1)
        i, j = it_ref[pair], jt_ref[pair]
        rows_i = pl.ds(pl.multiple_of(i * t, t), t)

        if nx:
            spread = npairs // N_DEV
            for batch in range(N_DEV):
                at = min(batch * spread, npairs - 1)

                @pl.when(jnp.logical_and(pl.program_id(0) == 0, pair == at))
                def _(batch=batch):
                    for cp in ex.copies(ex_in, ex_out, ex_sems, batch):
                        cp.start()

        @pl.when(i == j)
        def _():
            dk_scr[...] = jnp.zeros_like(dk_scr)
            dv_scr[...] = jnp.zeros_like(dv_scr)

        @pl.when(j == 0)
        def _():
            dq_ref[:, rows_i, :] = jnp.zeros((nh, t, HEAD_PAD), F32)

        def prob_strips(masked, h, sc, dp, row0):
            ps, dss = [], []
            low = _lane((STRIP, LANES)) < HEAD_DIM
            for r0 in range(0, sc.shape[0], STRIP):
                rows = slice(r0, r0 + STRIP)
                st = st_ref[h, row0 + r0:row0 + r0 + STRIP, :]
                swapped = pltpu.roll(st, HEAD_DIM, 1)
                lse = jnp.where(low, st, swapped)
                delta = jnp.where(low, swapped, st)
                if masked:
                    rr = row0 + r0 + lax.broadcasted_iota(jnp.int32, (STRIP, LANES), 0)
                    cc = lax.broadcasted_iota(jnp.int32, (STRIP, LANES), 1)
                pcs, dcs = [], []
                for n in range(sc.shape[1] // LANES):
                    cols = slice(n * LANES, (n + 1) * LANES)
                    x = sc[rows, cols] * c2
                    if masked:
                        x = jnp.where(cc + n * LANES <= rr, x, NEG_BIG)
                    p = jnp.exp2(x - lse)
                    pcs.append(p.astype(BF16))
                    dcs.append((p * (dp[rows, cols] - delta) * scale).astype(BF16))
                ps.append(jnp.concatenate(pcs, axis=1))
                dss.append(jnp.concatenate(dcs, axis=1))
            return jnp.concatenate(ps, axis=0), jnp.concatenate(dss, axis=0)

        def run(masked, parts):
            def scores_of(h):
                return [(_dot_nt(q_ref[h, r0:r0 + nr, :], k_ref[h, 0:nk, :]),
                         _dot_nt(do_ref[h, r0:r0 + nr, :], v_ref[h, 0:nk, :]))
                        for r0, nr, nk in parts]

            ahead = min(LOOKAHEAD, nh)
            scores = [scores_of(h) for h in range(ahead)]
            for h in range(nh):
                if h + ahead < nh:
                    scores.append(scores_of(h + ahead))
                for (r0, nr, nk), (sc, dp) in zip(parts, scores[h]):
                    p, ds = prob_strips(masked, h, sc, dp, r0)
                    dv_scr[h, 0:nk, :] += _dot_tn(p, do_ref[h, r0:r0 + nr, :])
                    dk_scr[h, 0:nk, :] += _dot_tn(ds, q_ref[h, r0:r0 + nr, :])
                    rows = pl.ds(pl.multiple_of(i * t + r0, t // 2), nr)
                    dq_ref[h, rows, :] += _dot(ds, k_ref[h, 0:nk, :])

        @pl.when(i > j)
        def _():
            run(False, [(0, t, t)])

        @pl.when(i == j)
        def _():
            run(True, [(0, t // 2, t // 2), (t // 2, t // 2, t)])

        @pl.when(i == nb - 1)
        def _():
            dk_ref[...] = dk_scr[...]
            dv_ref[...] = dv_scr[...].astype(BF16)

        if nx:
            @pl.when(jnp.logical_and(pl.program_id(0) == ngroups - 1, pair == npairs - 1))
            def _():
                for cp in ex.copies(ex_in, ex_out, ex_sems):
                    cp.wait()

    qspec = pl.BlockSpec((nh, t, HEAD_PAD), lambda h, p, it, jt: (h, it[p], 0))
    kspec = pl.BlockSpec((nh, t, HEAD_PAD), lambda h, p, it, jt: (h, jt[p], 0))
    dqspec = pl.BlockSpec((nh, s, HEAD_PAD), lambda h, p, it, jt: (h, 0, 0))
    out = jax.ShapeDtypeStruct((N_TOK_HEADS, s, HEAD_PAD), F32)
    res = pl.pallas_call(
        body,
        grid_spec=pltpu.PrefetchScalarGridSpec(
            num_scalar_prefetch=2, grid=(ngroups, npairs),
            in_specs=[qspec, kspec, kspec, qspec, qspec] + [ANY] * nx,
            out_specs=[dqspec, kspec, kspec] + [ANY] * nx,
            scratch_shapes=[pltpu.VMEM((nh, t, HEAD_PAD), F32)] * 2 + ex_scratch),
        out_shape=[out, out, jax.ShapeDtypeStruct(out.shape, BF16)] + ex_out_shape,
        name="flash_bwd", compiler_params=_cp(2))(itab, jtab, q, k, v, stats, do, *ex_arrays)
    return res[:3], res[3:]


def _mem_probs(qp, kp, hh):
    lane = _lane(qp.shape)
    keep = (lane < HEAD_DIM) if hh == 0 else (lane >= HEAD_DIM)
    qh = jnp.where(keep, qp, 0.0).astype(BF16)
    sc = _dot_nt(qh, kp) * (1.0 / math.sqrt(HEAD_DIM))
    e = jnp.exp(sc - jnp.max(sc, axis=-1, keepdims=True))
    return e / jnp.sum(e, axis=-1, keepdims=True), keep


def _gate_mem_fwd(tok, z, memkv, g0, q0, padded, name, ts):
    s = z.shape[0]
    zw = z.shape[1]
    tok_spec = _heads(ts) if padded else _rows(ts, TOK_WIDTH)

    def body(tok_ref, z_ref, mkv_ref, cat_ref, y_ref):
        if padded:
            for p in range(N_TOK_HEADS // 2):
                cat_ref[:, p * LANES:(p + 1) * LANES] = (
                    tok_ref[2 * p] + pltpu.roll(tok_ref[2 * p + 1], HEAD_DIM, 1))
        else:
            cat_ref[:, 0:TOK_WIDTH] = tok_ref[...]
        for pr in range(N_MEM_HEADS // 2):
            sl = slice(pr * LANES, (pr + 1) * LANES)
            qp = z_ref[:, q0 + pr * LANES:q0 + (pr + 1) * LANES]
            kp = mkv_ref[:, sl].astype(BF16)
            vp = mkv_ref[:, MEM_WIDTH + pr * LANES:MEM_WIDTH + (pr + 1) * LANES].astype(BF16)
            outs = []
            for hh in range(2):
                p, _ = _mem_probs(qp, kp, hh)
                outs.append(_dot(p.astype(BF16), vp))
            lane = _lane(outs[0].shape)
            cat_ref[:, TOK_WIDTH + pr * LANES:TOK_WIDTH + (pr + 1) * LANES] = jnp.where(
                lane < HEAD_DIM, outs[0], outs[1])
        gate = z_ref[:, g0:g0 + MIX_WIDTH]
        y_ref[...] = (cat_ref[...] * (gate * _sigmoid(gate))).astype(BF16)

    return pl.pallas_call(
        body, grid=(s // ts,),
        in_specs=[tok_spec, _rows(ts, zw), _full((MEM_LEN, 2 * MEM_WIDTH))],
        out_specs=[_rows(ts, MIX_WIDTH)] * 2,
        out_shape=[jax.ShapeDtypeStruct((s, MIX_WIDTH), F32),
                   jax.ShapeDtypeStruct((s, MIX_WIDTH), BF16)],
        name=name, compiler_params=_cp(1))(tok, z, memkv)


def _gate_mem_bwd(dy, cat, z, memkv, lse, g0, q0, name, ts):
    s = z.shape[0]
    zw = z.shape[1]
    padded = lse is not None
    gq_w = MIX_WIDTH + MEM_WIDTH

    def body(*refs):
        if padded:
            dy_ref, cat_ref, z_ref, mkv_ref, lse_ref, dzg_ref, dtok_ref, dmkv_ref, st_ref = refs
        else:
            dy_ref, cat_ref, z_ref, mkv_ref, dzg_ref, dtok_ref, dmkv_ref = refs

        @pl.when(pl.program_id(0) == 0)
        def _():
            dmkv_ref[...] = jnp.zeros_like(dmkv_ref)

        gate = z_ref[:, g0:g0 + MIX_WIDTH]
        sg = _sigmoid(gate)
        dy_ = dy_ref[...]
        dzg_ref[:, 0:MIX_WIDTH] = (dy_ * cat_ref[...]
                                   * (sg * (1.0 + gate * (1.0 - sg)))).astype(BF16)
        dcat = dy_ * (gate * sg)
        if padded:
            low = _lane((ts, LANES)) < HEAD_DIM
            for p in range(N_TOK_HEADS // 2):
                d = dcat[:, p * LANES:(p + 1) * LANES]
                prod = d * cat_ref[:, p * LANES:(p + 1) * LANES]
                first = jnp.sum(jnp.where(low, prod, 0.0), axis=-1, keepdims=True)
                second = jnp.sum(jnp.where(low, 0.0, prod), axis=-1, keepdims=True)
                dtok_ref[2 * p] = jnp.where(low, d, 0.0).astype(BF16)
                dtok_ref[2 * p + 1] = jnp.where(low, pltpu.roll(d, HEAD_DIM, 1), 0.0).astype(BF16)
                st_ref[2 * p] = jnp.where(low, lse_ref[2 * p], first)
                st_ref[2 * p + 1] = jnp.where(low, lse_ref[2 * p + 1], second)
        else:
            dtok_ref[...] = dcat[:, 0:TOK_WIDTH]
        for pr in range(N_MEM_HEADS // 2):
            sl = slice(pr * LANES, (pr + 1) * LANES)
            vsl = slice(MEM_WIDTH + pr * LANES, MEM_WIDTH + (pr + 1) * LANES)
            qp = z_ref[:, q0 + pr * LANES:q0 + (pr + 1) * LANES]
            qpb = qp.astype(BF16)
            kp = mkv_ref[:, sl].astype(BF16)
            vp = mkv_ref[:, vsl].astype(BF16)
            dmo = dcat[:, TOK_WIDTH + pr * LANES:TOK_WIDTH + (pr + 1) * LANES]
            dqp = None
            for hh in range(2):
                p, keep = _mem_probs(qp, kp, hh)
                do_h = jnp.where(keep, dmo, 0.0).astype(BF16)
                dmkv_ref[:, vsl] += _dot_tn(p.astype(BF16), do_h)
                dp = _dot_nt(do_h, vp)
                ds = (p * (dp - jnp.sum(dp * p, axis=-1, keepdims=True))
                      * (1.0 / math.sqrt(HEAD_DIM))).astype(BF16)
                dqh = jnp.where(keep, _dot(ds, kp), 0.0)
                dqp = dqh if dqp is None else dqp + dqh
                dkh = _dot_tn(ds, qpb)
                klane = _lane(dkh.shape)
                kkeep = (klane < HEAD_DIM) if hh == 0 else (klane >= HEAD_DIM)
                dmkv_ref[:, sl] += jnp.where(kkeep, dkh, 0.0)
            dzg_ref[:, MIX_WIDTH + pr * LANES:MIX_WIDTH + (pr + 1) * LANES] = dqp.astype(BF16)

    in_specs = [_rows(ts, MIX_WIDTH), _rows(ts, MIX_WIDTH), _rows(ts, zw),
                _full((MEM_LEN, 2 * MEM_WIDTH))]
    out_specs = [_rows(ts, gq_w), _heads(ts) if padded else _rows(ts, TOK_WIDTH),
                 _full((MEM_LEN, 2 * MEM_WIDTH))]
    heads_shape = (N_TOK_HEADS, s, HEAD_PAD)
    out_shape = [jax.ShapeDtypeStruct((s, gq_w), BF16),
                 jax.ShapeDtypeStruct(heads_shape, BF16) if padded
                 else jax.ShapeDtypeStruct((s, TOK_WIDTH), F32),
                 jax.ShapeDtypeStruct((MEM_LEN, 2 * MEM_WIDTH), F32)]
    args = [dy, cat, z, memkv]
    if padded:
        in_specs.append(_heads(ts))
        out_specs.append(_heads(ts))
        out_shape.append(jax.ShapeDtypeStruct(heads_shape, F32))
        args.append(lse)
    return pl.pallas_call(
        body, grid=(s // ts,), in_specs=in_specs, out_specs=out_specs, out_shape=out_shape,
        name=name, compiler_params=_cp(1))(*args)


def _ln_stats(pre):
    mu = jnp.mean(pre, axis=-1, keepdims=True)
    d = pre - mu
    rstd = lax.rsqrt(jnp.mean(d * d, axis=-1, keepdims=True) + NORM_EPS)
    return d * rstd, rstd


def _outproj_ln_fwd(y, w, h, g, b, tgt, name, ts):
    s = y.shape[0]
    with_loss = tgt is not None

    def body(*refs):
        if with_loss:
            y_ref, w_ref, h_ref, g_ref, b_ref, t_ref, pre_ref, out_ref, loss_ref = refs
        else:
            y_ref, w_ref, h_ref, g_ref, b_ref, pre_ref, out_ref = refs
        pre = ALPHA * h_ref[...] + _dot(y_ref[...].astype(BF16), w_ref[...])
        pre_ref[...] = pre
        xhat, _ = _ln_stats(pre)
        hout = xhat * g_ref[...] + b_ref[...]
        if with_loss:
            @pl.when(pl.program_id(0) == 0)
            def _():
                loss_ref[...] = jnp.zeros_like(loss_ref)
            err = hout - t_ref[...]
            out_ref[...] = err * (1.0 / D_MODEL)
            loss_ref[...] += 0.5 * jnp.sum(jnp.mean(err * err, axis=-1, keepdims=True))
        else:
            out_ref[...] = hout

    act = jax.ShapeDtypeStruct((s, D_MODEL), F32)
    in_specs = [_rows(ts, MIX_WIDTH), _full((MIX_WIDTH, D_MODEL)), _rows(ts, D_MODEL),
                _full((1, D_MODEL)), _full((1, D_MODEL))]
    out_specs = [_rows(ts, D_MODEL)] * 2
    out_shape = [act, act]
    args = [y, w, h, g, b]
    if with_loss:
        in_specs.append(_rows(ts, D_MODEL))
        out_specs.append(_full((SUBLANES, LANES)))
        out_shape.append(jax.ShapeDtypeStruct((SUBLANES, LANES), F32))
        args.append(tgt)
    return pl.pallas_call(
        body, grid=(s // ts,), in_specs=in_specs, out_specs=out_specs, out_shape=out_shape,
        name=name, compiler_params=_cp(1))(*args)


def _outproj_ln_bwd(dh, pre, g, y, w_t, name, ts):
    s = y.shape[0]

    def body(dh_ref, pre_ref, g_ref, y_ref, wt_ref, dpre_ref, dy_ref, dw_ref, dgb_ref):
        @pl.when(pl.program_id(0) == 0)
        def _():
            dw_ref[...] = jnp.zeros_like(dw_ref)
            dgb_ref[...] = jnp.zeros_like(dgb_ref)

        dh_ = dh_ref[...]
        xhat, rstd = _ln_stats(pre_ref[...])
        dxh = dh_ * g_ref[...]
        dpre = rstd * (dxh - jnp.mean(dxh, axis=-1, keepdims=True)
                       - xhat * jnp.mean(dxh * xhat, axis=-1, keepdims=True))
        dpre_ref[...] = dpre
        dgb_ref[0:1, :] += jnp.sum(dh_ * xhat, axis=0, keepdims=True)
        dgb_ref[1:2, :] += jnp.sum(dh_, axis=0, keepdims=True)
        dpb = dpre.astype(BF16)
        dy_ref[...] = _dot(dpb, wt_ref[...])
        dw_ref[...] += _dot_tn(y_ref[...].astype(BF16), dpb)

    act = jax.ShapeDtypeStruct((s, D_MODEL), F32)
    return pl.pallas_call(
        body, grid=(s // ts,),
        in_specs=[_rows(ts, D_MODEL), _rows(ts, D_MODEL), _full((1, D_MODEL)),
                  _rows(ts, MIX_WIDTH), _full((D_MODEL, MIX_WIDTH))],
        out_specs=[_rows(ts, D_MODEL), _rows(ts, MIX_WIDTH), _full((MIX_WIDTH, D_MODEL)),
                   _full((SUBLANES, D_MODEL))],
        out_shape=[act, act, jax.ShapeDtypeStruct((MIX_WIDTH, D_MODEL), F32),
                   jax.ShapeDtypeStruct((SUBLANES, D_MODEL), F32)],
        name=name, compiler_params=_cp(1))(dh, pre, g, y, w_t)


def _linear_bwd(x, dys, offs, w_t, resid, name, ts):
    s, kdim = x.shape
    n = w_t.shape[0]
    widths = [d.shape[1] for d in dys]
    npieces = len(dys)

    def body(*refs):
        x_ref = refs[0]
        dy_refs = refs[1:1 + npieces]
        wt_ref, r_ref, dx_ref, dw_ref = refs[1 + npieces:]

        @pl.when(pl.program_id(0) == 0)
        def _():
            dw_ref[...] = jnp.zeros_like(dw_ref)

        xb = x_ref[...].astype(BF16)
        dx = ALPHA * r_ref[...]
        for dy_ref, off, wd in zip(dy_refs, offs, widths):
            dyb = dy_ref[...].astype(BF16)
            dx = dx + _dot(dyb, wt_ref[off:off + wd, :])
            dw_ref[:, off:off + wd] += _dot_tn(xb, dyb)
        dx_ref[...] = dx

    return pl.pallas_call(
        body, grid=(s // ts,),
        in_specs=[_rows(ts, kdim)] + [_rows(ts, wd) for wd in widths]
                 + [_full((n, kdim)), _rows(ts, kdim)],
        out_specs=[_rows(ts, kdim), _full((kdim, n))],
        out_shape=[jax.ShapeDtypeStruct((s, kdim), F32), jax.ShapeDtypeStruct((kdim, n), F32)],
        name=name, compiler_params=_cp(1))(x, *dys, w_t, resid)


def _wgrad_small(x, dy, name):
    def body(x_ref, dy_ref, dw_ref):
        dw_ref[...] = _dot_tn(x_ref[...].astype(BF16), dy_ref[...].astype(BF16))

    return pl.pallas_call(
        body, out_shape=jax.ShapeDtypeStruct((x.shape[1], dy.shape[1]), F32),
        name=name, compiler_params=pltpu.CompilerParams(vmem_limit_bytes=VMEM_LIMIT))(x, dy)


def _shift_down(u, carry8, k):
    if k == 0:
        return u
    rolled = pltpu.roll(u, k, 0)
    row = lax.broadcasted_iota(jnp.int32, carry8.shape, 0)
    top = jnp.where(row < k, pltpu.roll(carry8, k, 0), rolled[0:SUBLANES])
    return jnp.concatenate([top, rolled[SUBLANES:]], axis=0)


def _shift_up(u, carry8, k):
    if k == 0:
        return u
    n = u.shape[0]
    rolled = pltpu.roll(u, n - k, 0)
    row = lax.broadcasted_iota(jnp.int32, carry8.shape, 0)
    bot = jnp.where(row >= SUBLANES - k, pltpu.roll(carry8, SUBLANES - k, 0),
                    rolled[n - SUBLANES:])
    return jnp.concatenate([rolled[:n - SUBLANES], bot], axis=0)


def _neg_expm1(t):
    e = jnp.exp(t)
    em1 = e - 1.0
    safe = jnp.where(e == 1.0, 1.0, jnp.log(e))
    return -jnp.where(e == 1.0, t, jnp.where(em1 == -1.0, -1.0, em1 * t / safe))


def _lru_gates(u, carry8, cw_ref, vec_ref, wr_ref, wi_ref):
    taps = [_shift_down(u, carry8, k) for k in range(CONV_W)]
    xc = vec_ref[0:1, :] + cw_ref[3:4, :] * u
    for k in range(1, CONV_W):
        xc = xc + cw_ref[3 - k:4 - k, :] * taps[k]
    xb = xc.astype(BF16)
    r = _sigmoid(_dot(xb, wr_ref[...]) + vec_ref[1:2, :])
    ig = _sigmoid(_dot(xb, wi_ref[...]) + vec_ref[2:3, :])
    nlam = -vec_ref[3:4, :]
    softplus = jnp.maximum(nlam, 0.0) + jnp.log(1.0 + jnp.exp(-jnp.abs(nlam)))
    cneg = -LRU_C * softplus
    log_a = cneg * r
    a = jnp.exp(log_a)
    sq = jnp.sqrt(_neg_expm1(2.0 * log_a))
    return xc, r, ig, cneg, a, sq, taps


def _lru_fwd(z1, cw8, vec8, wr, wi, ts):
    s = z1.shape[0]

    def body(u_ref, cw_ref, vec_ref, wr_ref, wi_ref, hs_ref, cu_scr, ch_scr, a_scr, gx_scr):
        @pl.when(pl.program_id(0) == 0)
        def _():
            cu_scr[...] = jnp.zeros_like(cu_scr)
            ch_scr[...] = jnp.zeros_like(ch_scr)

        u = u_ref[...]
        xc, _, ig, _, a, sq, _ = _lru_gates(u, cu_scr[...], cw_ref, vec_ref, wr_ref, wi_ref)
        a_scr[...] = a
        gx_scr[...] = sq * (ig * xc)

        def step(t, h):
            h = a_scr[pl.ds(t, 1), :] * h + gx_scr[pl.ds(t, 1), :]
            hs_ref[pl.ds(t, 1), :] = h
            return h

        ch_scr[0:1, :] = lax.fori_loop(0, ts, step, ch_scr[0:1, :])
        cu_scr[...] = u[ts - SUBLANES:, :]

    w = TOK_WIDTH
    return pl.pallas_call(
        body, grid=(s // ts,),
        in_specs=[_rows(ts, w), _full((SUBLANES, w)), _full((SUBLANES, w)),
                  _full((w, w)), _full((w, w))],
        out_specs=_rows(ts, w),
        out_shape=jax.ShapeDtypeStruct((s, w), F32),
        scratch_shapes=[pltpu.VMEM((SUBLANES, w), F32), pltpu.VMEM((SUBLANES, w), F32),
                        pltpu.VMEM((ts, w), F32), pltpu.VMEM((ts, w), F32)],
        name="lru_fwd", compiler_params=_cp(1))(z1, cw8, vec8, wr, wi)


def _lru_bwd(z1, dhs, hs, cw8, vec8, wr, wi, wr_t, wi_t, ts):
    s = z1.shape[0]
    nb = s // ts
    w = TOK_WIDTH
    tiles = ts // SUBLANES

    def body(u_ref, up_ref, dhs_ref, hs_ref, hsp_ref, cw_ref, vec_ref, wr_ref, wi_ref,
             wrt_ref, wit_ref, du_ref, dwr_ref, dwi_ref, dvec_ref,
             cc_scr, cd_scr, a_scr, dh_scr):
        i = pl.program_id(0)

        @pl.when(i == 0)
        def _():
            cc_scr[...] = jnp.zeros_like(cc_scr)
            cd_scr[...] = jnp.zeros_like(cd_scr)
            dwr_ref[...] = jnp.zeros_like(dwr_ref)
            dwi_ref[...] = jnp.zeros_like(dwi_ref)
            dvec_ref[...] = jnp.zeros_like(dvec_ref)

        u = u_ref[...]
        first = i == nb - 1
        carry8 = jnp.where(first, 0.0, up_ref[...])
        xc, r, ig, cneg, a, sq, taps = _lru_gates(u, carry8, cw_ref, vec_ref, wr_ref, wi_ref)
        a_scr[...] = a

        def step(n, c):
            t = ts - 1 - n
            dh = dhs_ref[pl.ds(t, 1), :] + c
            dh_scr[pl.ds(t, 1), :] = dh
            return a_scr[pl.ds(t, 1), :] * dh

        cc_scr[0:1, :] = lax.fori_loop(0, ts, step, cc_scr[0:1, :])
        dh = dh_scr[...]
        hprev = _shift_down(hs_ref[...], jnp.where(first, 0.0, hsp_ref[...]), 1)
        ix = ig * xc
        dix = dh * sq
        dlog_a = dh * hprev * a - (dh * ix) * (a * a) / sq
        dpr = (dlog_a * cneg) * r * (1.0 - r)
        dpi = (dix * xc) * ig * (1.0 - ig)
        dprb, dpib = dpr.astype(BF16), dpi.astype(BF16)
        xb = xc.astype(BF16)
        dwr_ref[...] += _dot_tn(xb, dprb)
        dwi_ref[...] += _dot_tn(xb, dpib)
        dxc = dix * ig + _dot(dprb, wrt_ref[...]) + _dot(dpib, wit_ref[...])
        for k in range(CONV_W):
            dvec_ref[3 - k:4 - k, :] += jnp.sum(dxc * taps[k], axis=0, keepdims=True)
        dvec_ref[4:5, :] += jnp.sum(dxc, axis=0, keepdims=True)
        dvec_ref[5:6, :] += jnp.sum(dpr, axis=0, keepdims=True)
        dvec_ref[6:7, :] += jnp.sum(dpi, axis=0, keepdims=True)
        dvec_ref[7:8, :] += (jnp.sum(dlog_a * r, axis=0, keepdims=True)
                             * (LRU_C * _sigmoid(-vec_ref[3:4, :])))
        nxt = cd_scr[...]
        du = cw_ref[3:4, :] * dxc
        for k in range(1, CONV_W):
            du = du + cw_ref[3 - k:4 - k, :] * _shift_up(dxc, nxt, k)
        du_ref[...] = du.astype(BF16)
        cd_scr[...] = dxc[0:SUBLANES, :]

    rev = lambda i: (nb - 1 - i, 0)
    prev8 = lambda i: (jnp.maximum((nb - 1 - i) * tiles - 1, 0), 0)
    blk = pl.BlockSpec((ts, w), rev)
    before = pl.BlockSpec((SUBLANES, w), prev8)
    scr = pltpu.VMEM((ts, w), F32)
    return pl.pallas_call(
        body, grid=(nb,),
        in_specs=[blk, before, blk, blk, before,
                  _full((SUBLANES, w)), _full((SUBLANES, w)),
                  _full((w, w)), _full((w, w)), _full((w, w)), _full((w, w))],
        out_specs=[blk, _full((w, w)), _full((w, w)), _full((SUBLANES, w))],
        out_shape=[jax.ShapeDtypeStruct((s, w), BF16), jax.ShapeDtypeStruct((w, w), F32),
                   jax.ShapeDtypeStruct((w, w), F32), jax.ShapeDtypeStruct((SUBLANES, w), F32)],
        scratch_shapes=[pltpu.VMEM((SUBLANES, w), F32), pltpu.VMEM((SUBLANES, w), F32),
                        scr, scr],
        name="lru_bwd", compiler_params=_cp(1))(
            z1, z1, dhs, hs, hs, cw8, vec8, wr, wi, wr_t, wi_t)


def _adamw(parts, w, m, v, name):
    n = len(parts)
    rows_per = parts[0].shape[1]

    def body(*refs):
        p_refs = refs[:n]
        w_ref, m_ref, v_ref, g_ref, d_ref, nm_ref, nv_ref = refs[n:]
        for l, p_ref in enumerate(p_refs):
            rows = slice(l * rows_per, (l + 1) * rows_per)
            g = p_ref[0].astype(F32)
            for dev in range(1, N_DEV):
                g = g + p_ref[dev].astype(F32)
            g_ref[rows, :] = g
            nm = ADAM_B1 * m_ref[rows, :] + (1.0 - ADAM_B1) * g
            nv = ADAM_B2 * v_ref[rows, :] + (1.0 - ADAM_B2) * (g * g)
            m_hat = nm / (1.0 - ADAM_B1 ** ADAM_STEP)
            v_hat = nv / (1.0 - ADAM_B2 ** ADAM_STEP)
            d_ref[rows, :] = -ADAM_LR * (m_hat / (jnp.sqrt(v_hat) + ADAM_EPS)
                                         + ADAM_WD * w_ref[rows, :])
            nm_ref[rows, :] = nm
            nv_ref[rows, :] = nv

    out = jax.ShapeDtypeStruct(w.shape, F32)
    return pl.pallas_call(
        body, out_shape=[out] * 4, name=name,
        compiler_params=pltpu.CompilerParams(vmem_limit_bytes=VMEM_LIMIT))(*parts, w, m, v)


ANY = pl.BlockSpec(memory_space=pl.ANY)
MESH = pl.DeviceIdType.MESH


def _slot(p):
    return 4 * p[0] + 2 * p[1] + p[2]


def _allgather(xs):
    n = len(xs)

    def body(*refs):
        x_refs, o_refs = refs[:n], refs[n:2 * n]
        send_sems, recv_sems, local_sems = refs[2 * n:]
        x, y, c = lax.axis_index("x"), lax.axis_index("y"), lax.axis_index("c")
        me, sibling = (x, y, c), (x, y, 1 - c)
        chips = [(1 - x, y), (x, 1 - y), (1 - x, 1 - y)]

        def copy(a, k, block, to, from_input=False):
            dst = o_refs[a].at[_slot(block)]
            return pltpu.make_async_remote_copy(
                src_ref=x_refs[a] if from_input else dst, dst_ref=dst,
                send_sem=send_sems.at[a, k], recv_sem=recv_sems.at[a, k],
                device_id=to, device_id_type=MESH)

        mine = [pltpu.make_async_copy(x_refs[a], o_refs[a].at[_slot(me)], local_sems.at[a])
                for a in range(n)]
        for cp in mine:
            cp.start()
        first = []
        for a in range(n):
            first.append(copy(a, 0, me, sibling, True))
            first += [copy(a, 1 + j, me, (*chip, c), True) for j, chip in enumerate(chips)]
        for cp in first:
            cp.start()
        passed = []
        for j, chip in enumerate(chips):
            for a in range(n):
                copy(a, 1 + j, (*chip, c), me).wait_recv()
                cp = copy(a, 4 + j, (*chip, c), sibling)
                cp.start()
                passed.append(cp)
        for a in range(n):
            copy(a, 0, sibling, me).wait_recv()
            for j, chip in enumerate(chips):
                copy(a, 4 + j, (*chip, 1 - c), me).wait_recv()
        for cp in first + passed:
            cp.wait_send()
        for cp in mine:
            cp.wait()

    return pl.pallas_call(
        body,
        out_shape=[jax.ShapeDtypeStruct((N_DEV,) + t.shape, t.dtype) for t in xs],
        in_specs=[ANY] * n, out_specs=[ANY] * n,
        scratch_shapes=[pltpu.SemaphoreType.DMA((n, 7)), pltpu.SemaphoreType.DMA((n, 7)),
                        pltpu.SemaphoreType.DMA((n,))],
        name="allgather_weights")(*xs)


class _Exchange:
    def __init__(self, arrays, kinds):
        self.arrays, self.kinds, self.n = list(arrays), list(kinds), len(arrays)
        self.shapes = [self._part_shape(a, k) for a, k in zip(arrays, kinds)]
        self.out_shape = [jax.ShapeDtypeStruct((N_DEV,) + shp, a.dtype)
                          for shp, a in zip(self.shapes, arrays)]
        self.scratch = [pltpu.SemaphoreType.DMA((self.n, N_DEV - 1)),
                        pltpu.SemaphoreType.DMA((self.n, N_DEV - 1)),
                        pltpu.SemaphoreType.DMA((self.n,))]

    @staticmethod
    def _part_shape(arr, kind):
        if kind == "chunks":
            return arr.shape[1:]
        if kind == "cols":
            return (arr.shape[0], arr.shape[1] // N_DEV)
        if kind == "rows":
            return (arr.shape[0] // N_DEV, arr.shape[1])
        return arr.shape

    def copies(self, in_refs, out_refs, sems, batch=None):
        send_sems, recv_sems, local_sems = sems
        x, y, c = lax.axis_index("x"), lax.axis_index("y"), lax.axis_index("c")
        me = _slot((x, y, c))

        def part(a, dev):
            ref, kind, shp = in_refs[a], self.kinds[a], self.shapes[a]
            if kind == "chunks":
                return ref.at[dev]
            if kind == "cols":
                return ref.at[:, pl.ds(pl.multiple_of(dev * shp[1], LANES), shp[1])]
            if kind == "rows":
                return ref.at[pl.ds(pl.multiple_of(dev * shp[0], SUBLANES), shp[0]), :]
            return ref

        cps = []
        if batch in (None, 0):
            cps = [pltpu.make_async_copy(part(a, me), out_refs[a].at[me], local_sems.at[a])
                   for a in range(self.n)]
        for rel in range(1, N_DEV):
            if batch not in (None, rel):
                continue
            peer = (x ^ (rel >> 2), y ^ ((rel >> 1) & 1), c ^ (rel & 1))
            for a in range(self.n):
                cps.append(pltpu.make_async_remote_copy(
                    src_ref=part(a, _slot(peer)), dst_ref=out_refs[a].at[me],
                    send_sem=send_sems.at[a, rel - 1], recv_sem=recv_sems.at[a, rel - 1],
                    device_id=peer, device_id_type=MESH))
        return cps


def _exchange_grads(arrays, kinds, name):
    ex = _Exchange(arrays, kinds)
    n = ex.n

    def body(*refs):
        cps = ex.copies(refs[:n], refs[n:2 * n], refs[2 * n:])
        for cp in cps:
            cp.start()
        for cp in cps:
            cp.wait()

    return pl.pallas_call(
        body, out_shape=ex.out_shape, in_specs=[ANY] * n, out_specs=[ANY] * n,
        scratch_shapes=ex.scratch, name=name)(*arrays)


BIG = [("mla_w_in", (D_MODEL, MLA_IN), 1), ("mla_w_uq", (Q_LORA, N_TOK_HEADS * QK_DIM), 1),
       ("mla_w_ukv", (KV_LORA, N_TOK_HEADS * 2 * HEAD_DIM), 1), ("lru_w_in", (D_MODEL, LRU_IN), 1),
       ("w_mem_kv", (2, D_MODEL, 2 * MEM_WIDTH), 1), ("w_out", (2, MIX_WIDTH, D_MODEL), 1)]
SMALL = [("lru_conv_w", (CONV_W, TOK_WIDTH), 1), ("lru_conv_b", (TOK_WIDTH,), 0),
         ("lru_b_rgate", (TOK_WIDTH,), 0), ("lru_b_igate", (TOK_WIDTH,), 0),
         ("lru_lambda", (TOK_WIDTH,), 0)]
REPL = [("mla_q_norm", (Q_LORA,)), ("mla_kv_norm", (KV_LORA,)),
        ("lru_w_rgate", (N_TOK_HEADS, HEAD_DIM, HEAD_DIM)),
        ("lru_w_igate", (N_TOK_HEADS, HEAD_DIM, HEAD_DIM)),
        ("ln_g", (2, D_MODEL)), ("ln_b", (2, D_MODEL))]


def _shard_shape(shape, axis):
    return tuple(d // N_DEV if a == axis else d for a, d in enumerate(shape))


def _size(shape):
    return math.prod(shape)


BIG_ROWS = sum(_size(s) for _, s, _ in BIG) // N_DEV // LANES
SMALL_ROWS = SUBLANES


def _pack_rows(flat_parts, rows):
    flat = jnp.concatenate([p.reshape(-1) for p in flat_parts])
    return jnp.pad(flat, (0, rows * LANES - flat.shape[0])).reshape(rows, LANES)


def _to_chunks(full, axis):
    shape = full.shape
    split = shape[:axis] + (N_DEV, shape[axis] // N_DEV) + shape[axis + 1:]
    return jnp.moveaxis(full.reshape(split), axis, 0).reshape(N_DEV, -1)


def _from_chunks(chunks, shape, axis):
    sh = _shard_shape(shape, axis)
    t = chunks.reshape((N_DEV,) + sh)
    t = jnp.moveaxis(t, 0, axis)
    return t.reshape(shape)


def _split_flat(flat2d, table):
    out, off = [], 0
    for size in table:
        out.append(flat2d[:, off:off + size])
        off += size
    return out


def _win0_to_padded(w):
    z = lambda n: jnp.zeros((w.shape[0], n), w.dtype)
    return jnp.concatenate([w[:, 0:640], z(KR_LANE), w[:, 640:672],
                            z(LANES - KR_LANE - QK_ROPE), w[:, 672:1952]], axis=1)


def _win0_from_padded(wp):
    k0 = ZA_KR + KR_LANE
    return jnp.concatenate([wp[:, 0:640], wp[:, k0:k0 + QK_ROPE], wp[:, ZA_W:ZP]], axis=1)


def _pad_heads(w, per_head, lo, hi):
    t = w.reshape(w.shape[0], N_TOK_HEADS, per_head)[:, :, lo:hi]
    t = jnp.pad(t, ((0, 0), (0, 0), (0, HEAD_PAD - (hi - lo))))
    return t.reshape(w.shape[0], QKV_PAD)


def _unpad_heads(wp, width):
    return wp.reshape(wp.shape[0], N_TOK_HEADS, HEAD_PAD)[:, :, :width]


def _block_diag(w):
    eye = jnp.eye(N_TOK_HEADS, dtype=w.dtype)
    return (w[:, :, None, :] * eye[:, None, :, None]).reshape(TOK_WIDTH, TOK_WIDTH)


def _diag_blocks(d):
    t = d.reshape(N_TOK_HEADS, HEAD_DIM, N_TOK_HEADS, HEAD_DIM)
    return jnp.stack([t[g, :, g, :] for g in range(N_TOK_HEADS)])


def _rope_tables(positions):
    half = QK_ROPE // 2
    inv_freq = ROPE_THETA ** (-jnp.arange(half, dtype=F32) / half)
    ang = positions.astype(F32)[:, None] * inv_freq
    cos, sin = jnp.cos(ang), jnp.sin(ang)
    s = positions.shape[0]
    one, zero = jnp.ones((s, QK_NOPE), F32), jnp.zeros((s, half), F32)
    tail = jnp.zeros((s, HEAD_PAD - QK_DIM), F32)
    znope = jnp.zeros((s, QK_NOPE), F32)
    c = jnp.concatenate([one, cos, cos, tail], axis=1)
    sa = jnp.concatenate([znope, -sin, zero, tail], axis=1)
    sb = jnp.concatenate([znope, zero, sin, tail], axis=1)
    return c, sa, sb


def _local_step(x, mem, positions, tgt, wts, ts, tatt, early_exchange):
    bf = lambda t: t.astype(BF16)
    win0 = _win0_to_padded(wts["mla_w_in"])
    wuq = _pad_heads(wts["mla_w_uq"], QK_DIM, 0, QK_DIM)
    wukv = jnp.concatenate([_pad_heads(wts["mla_w_ukv"], 2 * HEAD_DIM, 0, QK_NOPE),
                            _pad_heads(wts["mla_w_ukv"], 2 * HEAD_DIM, QK_NOPE, 2 * HEAD_DIM)],
                           axis=1)
    win1 = wts["lru_w_in"]
    wmkv, wout = wts["w_mem_kv"], wts["w_out"]
    gq = wts["mla_q_norm"].reshape(1, Q_LORA)
    gkv = wts["mla_kv_norm"].reshape(1, KV_LORA)
    ln_g, ln_b = wts["ln_g"], wts["ln_b"]
    wr, wi = bf(_block_diag(wts["lru_w_rgate"])), bf(_block_diag(wts["lru_w_igate"]))
    cw8 = jnp.pad(wts["lru_conv_w"], ((0, SUBLANES - CONV_W), (0, 0)))
    vec8 = jnp.pad(jnp.stack([wts["lru_conv_b"], wts["lru_b_rgate"], wts["lru_b_igate"],
                              wts["lru_lambda"]]), ((0, SUBLANES - 4), (0, 0)))
    tabs = _rope_tables(positions)
    tmem = mem.shape[0]

    za0, zg0 = _rowmm(x, win0, [ZA_W, ZG_W], "in_proj0", ts)
    q, k, v = _mla_prep_fwd(za0, tabs, gq, gkv, wuq, wukv, ts)
    o, lse = _flash_fwd(q, k, v, tatt, FWD_HEADS)
    mkv0, = _rowmm(mem, wmkv[0], [2 * MEM_WIDTH], "mem_kv0", tmem)
    cat0, y0 = _gate_mem_fwd(o, zg0, mkv0, 0, MIX_WIDTH, True, "gate_mem_fwd0", ts)
    del o
    pre0, h1 = _outproj_ln_fwd(y0, wout[0], x, ln_g[0:1], ln_b[0:1], None, "outproj_ln_fwd0", ts)
    u1, zg1 = _rowmm(h1, win1, [ZA_W, ZG_W], "in_proj1", ts)
    hs = _lru_fwd(u1, cw8, vec8, wr, wi, ts)
    mkv1, = _rowmm(mem, wmkv[1], [2 * MEM_WIDTH], "mem_kv1", tmem)
    cat1, y1 = _gate_mem_fwd(hs, zg1, mkv1, 0, MIX_WIDTH, False, "gate_mem_fwd1", ts)
    pre1, dh2, loss8 = _outproj_ln_fwd(y1, wout[1], h1, ln_g[1:2], ln_b[1:2], tgt,
                                       "outproj_ln_loss", ts)
    loss = loss8[0, 0]

    dpre1, dy1, dwout1, dgb1 = _outproj_ln_bwd(dh2, pre1, ln_g[1:2], y1, wout[1].T,
                                               "outproj_ln_bwd1", ts)
    dzg1, dhs, dmkv1 = _gate_mem_bwd(dy1, cat1, zg1, mkv1, None, 0, MIX_WIDTH,
                                     "gate_mem_bwd1", ts)
    du, dwr, dwi, dvec = _lru_bwd(u1, dhs, hs, cw8, vec8, wr, wi, wr.T, wi.T, ts)
    dh1, dwin1 = _linear_bwd(h1, [du, dzg1], [0, ZA_W], win1.T, dpre1, "in_proj_bwd1", ts)
    dwmkv1 = _wgrad_small(mem, dmkv1, "mem_kv_bwd1")
    dpre0, dy0, dwout0, dgb0 = _outproj_ln_bwd(dh1, pre0, ln_g[0:1], y0, wout[0].T,
                                               "outproj_ln_bwd0", ts)
    dzg0, do, dmkv0, stats = _gate_mem_bwd(dy0, cat0, zg0, mkv0, lse, 0, MIX_WIDTH,
                                           "gate_mem_bwd0", ts)
    dwmkv0 = _wgrad_small(mem, dmkv0, "mem_kv_bwd0")
    early = {
        "lru_w_in": dwin1,
        "lru_small": dvec,
        "lru_w_rgate": _diag_blocks(dwr).reshape(TOK_WIDTH, HEAD_DIM),
        "lru_w_igate": _diag_blocks(dwi).reshape(TOK_WIDTH, HEAD_DIM),
        "w_mem_kv": [dwmkv0, dwmkv1],
        "w_out": [dwout0, dwout1],
    }
    (dq, dk, dv), got_early = _flash_bwd(q, k, v, stats, do, tatt, BWD_HEADS,
                                         early_exchange(early))
    dza, dzk, dwuq_p, dwukv_p, dg = _mla_prep_bwd(za0, dq, dk, dv, tabs, gq, gkv,
                                                  wuq.T, wukv.T, ts)
    gx, dwin0_p = _linear_bwd(x, [dza, dzk, dzg0], [ZA_CQ, ZA_KR, ZA_W], win0.T, dpre0,
                              "in_proj_bwd0", ts)

    dwukv = jnp.concatenate([_unpad_heads(dwukv_p[:, :QKV_PAD], HEAD_DIM),
                             _unpad_heads(dwukv_p[:, QKV_PAD:], HEAD_DIM)], axis=2)
    zrow = jnp.zeros((1, D_MODEL), F32)
    gains = jnp.pad(dg[0:1], ((0, 0), (0, D_MODEL - Q_LORA - KV_LORA)))
    small_repl = jnp.concatenate([dgb0[0:2], dgb1[0:2], gains,
                                  loss * jnp.ones((1, D_MODEL), F32), zrow, zrow], axis=0)
    late = {
        "mla_w_in": _win0_from_padded(dwin0_p),
        "mla_w_uq": _unpad_heads(dwuq_p, QK_DIM).reshape(Q_LORA, N_TOK_HEADS * QK_DIM),
        "mla_w_ukv": dwukv.reshape(KV_LORA, N_TOK_HEADS * 2 * HEAD_DIM),
        "small_repl": small_repl,
    }
    return gx, early, got_early, late


WEIGHT_ORDER = ["mla_w_in", "mla_q_norm", "mla_w_uq", "mla_kv_norm", "mla_w_ukv", "lru_w_in",
                "lru_conv_w", "lru_conv_b", "lru_w_rgate", "lru_b_rgate", "lru_w_igate",
                "lru_b_igate", "lru_lambda", "w_mem_kv", "w_out", "ln_g", "ln_b"]


def kernel(x, mem, positions, mla_w_in, mla_q_norm, mla_w_uq, mla_kv_norm, mla_w_ukv, lru_w_in, lru_conv_w, lru_conv_b, lru_w_rgate, lru_b_rgate, lru_w_igate, lru_b_igate, lru_lambda, w_mem_kv, w_out, ln_g, ln_b, loss_target, m_mla_w_in, m_mla_q_norm, m_mla_w_uq, m_mla_kv_norm, m_mla_w_ukv, m_lru_w_in, m_lru_conv_w, m_lru_conv_b, m_lru_w_rgate, m_lru_b_rgate, m_lru_w_igate, m_lru_b_igate, m_lru_lambda, m_w_mem_kv, m_w_out, m_ln_g, m_ln_b, v_mla_w_in, v_mla_q_norm, v_mla_w_uq, v_mla_kv_norm, v_mla_w_ukv, v_lru_w_in, v_lru_conv_w, v_lru_conv_b, v_lru_w_rgate, v_lru_b_rgate, v_lru_w_igate, v_lru_b_igate, v_lru_lambda, v_w_mem_kv, v_w_out, v_ln_g, v_ln_b):
    w_in = dict(mla_w_in=mla_w_in, mla_q_norm=mla_q_norm, mla_w_uq=mla_w_uq,
                mla_kv_norm=mla_kv_norm, mla_w_ukv=mla_w_ukv, lru_w_in=lru_w_in,
                lru_conv_w=lru_conv_w, lru_conv_b=lru_conv_b, lru_w_rgate=lru_w_rgate,
                lru_b_rgate=lru_b_rgate, lru_w_igate=lru_w_igate, lru_b_igate=lru_b_igate,
                lru_lambda=lru_lambda, w_mem_kv=w_mem_kv, w_out=w_out, ln_g=ln_g, ln_b=ln_b)
    m_in = dict(mla_w_in=m_mla_w_in, mla_q_norm=m_mla_q_norm, mla_w_uq=m_mla_w_uq,
                mla_kv_norm=m_mla_kv_norm, mla_w_ukv=m_mla_w_ukv, lru_w_in=m_lru_w_in,
                lru_conv_w=m_lru_conv_w, lru_conv_b=m_lru_conv_b, lru_w_rgate=m_lru_w_rgate,
                lru_b_rgate=m_lru_b_rgate, lru_w_igate=m_lru_w_igate, lru_b_igate=m_lru_b_igate,
                lru_lambda=m_lru_lambda, w_mem_kv=m_w_mem_kv, w_out=m_w_out, ln_g=m_ln_g,
                ln_b=m_ln_b)
    v_in = dict(mla_w_in=v_mla_w_in, mla_q_norm=v_mla_q_norm, mla_w_uq=v_mla_w_uq,
                mla_kv_norm=v_mla_kv_norm, mla_w_ukv=v_mla_w_ukv, lru_w_in=v_lru_w_in,
                lru_conv_w=v_lru_conv_w, lru_conv_b=v_lru_conv_b, lru_w_rgate=v_lru_w_rgate,
                lru_b_rgate=v_lru_b_rgate, lru_w_igate=v_lru_w_igate, lru_b_igate=v_lru_b_igate,
                lru_lambda=v_lru_lambda, w_mem_kv=v_w_mem_kv, w_out=v_w_out, ln_g=v_ln_g,
                ln_b=v_ln_b)
    s = x.shape[1]
    ts = min(ROW_BLOCK, s)
    tatt = min(ATT_BLOCK, s)
    big_sizes = [_size(sh) // N_DEV for _, sh, _ in BIG]
    small_sizes = [_size(sh) // N_DEV for _, sh, _ in SMALL]

    big_local = _pack_rows([w_in[n] for n, _, _ in BIG], BIG_ROWS).astype(BF16)
    small_local = _pack_rows([w_in[n] for n, _, _ in SMALL], SMALL_ROWS)
    big_all, small_all = _allgather([big_local, small_local])
    wts = {}
    for (n, sh, ax), part in zip(BIG, _split_flat(big_all.reshape(N_DEV, -1), big_sizes)):
        wts[n] = _from_chunks(part, sh, ax)
    for (n, sh, ax), part in zip(SMALL, _split_flat(small_all.reshape(N_DEV, -1), small_sizes)):
        wts[n] = _from_chunks(part, sh, ax)
    for n, sh in REPL:
        wts[n] = w_in[n].reshape(sh)

    def early_exchange(g):
        small_chunks = jnp.moveaxis(g["lru_small"].reshape(SUBLANES, N_DEV, -1), 1, 0)
        bf = lambda t: t.astype(BF16)
        sends = [(bf(g["lru_w_in"]), "cols"),
                 (bf(g["w_mem_kv"][0]), "rows"), (bf(g["w_mem_kv"][1]), "rows"),
                 (bf(g["w_out"][0]), "rows"), (bf(g["w_out"][1]), "rows"),
                 (small_chunks, "chunks"),
                 (bf(g["lru_w_rgate"]), "all"), (bf(g["lru_w_igate"]), "all")]
        return _Exchange([a for a, _ in sends], [k for _, k in sends])

    gx, _, got_early, late = _local_step(x[0], mem[0], positions[0], loss_target[0], wts,
                                         ts, tatt, early_exchange)

    def chunked(name, shape):
        w = shape[1] // N_DEV
        return _to_chunks(late[name], 1).reshape(N_DEV, shape[0], w).astype(BF16)

    got_late = _exchange_grads(
        [chunked("mla_w_in", (D_MODEL, MLA_IN)),
         chunked("mla_w_uq", (Q_LORA, N_TOK_HEADS * QK_DIM)),
         chunked("mla_w_ukv", (KV_LORA, N_TOK_HEADS * 2 * HEAD_DIM)), late["small_repl"]],
        ["chunks", "chunks", "chunks", "all"], "exchange_grads")
    got = list(got_late[:3]) + list(got_early) + [got_late[3]]

    def small_sharded(d):
        return jnp.concatenate([d["lru_conv_w"].reshape(CONV_W, -1), d["lru_conv_b"],
                                d["lru_b_rgate"], d["lru_b_igate"], d["lru_lambda"]], axis=0)

    def small_replicated(d):
        gains = jnp.concatenate([d["mla_q_norm"], d["mla_kv_norm"]], axis=1)
        gains = jnp.pad(gains, ((0, 0), (0, D_MODEL - gains.shape[1])))
        return jnp.concatenate([d["ln_g"][0:1], d["ln_b"][0:1], d["ln_g"][1:2], d["ln_b"][1:2],
                                gains, jnp.zeros((3, D_MODEL), F32)], axis=0)

    def flat2(d, name):
        t = d[name]
        return t.reshape(-1, t.shape[-1])

    def update(parts, view, name):
        return _adamw(parts, view(w_in), view(m_in), view(v_in), "adamw_" + name)

    res = {}
    for idx, name in [(0, "mla_w_in"), (1, "mla_w_uq"), (2, "mla_w_ukv"), (3, "lru_w_in"),
                      (9, "lru_w_rgate"), (10, "lru_w_igate")]:
        res[name] = update([got[idx]], functools.partial(flat2, name=name), name)
    res["w_mem_kv"] = update([got[4], got[5]], functools.partial(flat2, name="w_mem_kv"),
                             "w_mem_kv")
    res["w_out"] = update([got[6], got[7]], functools.partial(flat2, name="w_out"), "w_out")
    res_ss = update([got[8]], small_sharded, "small_sharded")
    res_sr = update([got[11]], small_replicated, "small_replicated")
    loss = res_sr[0][5, 0]

    result = [loss, gx.reshape(x.shape)]
    for kind in range(4):
        ss, sr = res_ss[kind], res_sr[kind]
        out = {n: res[n][kind].reshape(w_in[n].shape) for n in res}
        out["lru_conv_w"] = ss[0:CONV_W].reshape(w_in["lru_conv_w"].shape)
        out["lru_conv_b"], out["lru_b_rgate"] = ss[4:5], ss[5:6]
        out["lru_b_igate"], out["lru_lambda"] = ss[6:7], ss[7:8]
        out["ln_g"] = jnp.concatenate([sr[0:1], sr[2:3]], axis=0)
        out["ln_b"] = jnp.concatenate([sr[1:2], sr[3:4]], axis=0)
        out["mla_q_norm"] = sr[4:5, 0:Q_LORA]
        out["mla_kv_norm"] = sr[4:5, Q_LORA:Q_LORA + KV_LORA]
        result += [out[n] for n in WEIGHT_ORDER]
    return tuple(result)
```

```python
import functools
import math

import jax
import jax.numpy as jnp
from jax import lax
from jax.experimental import pallas as pl
from jax.experimental.pallas import tpu as pltpu

F32 = jnp.float32
BF16 = jnp.bfloat16

D_MODEL = 1024
MEM_LEN = 256
HEAD_DIM = 64
N_TOK_HEADS = 12
N_MEM_HEADS = 4
TOK_WIDTH = 768
MEM_WIDTH = 256
MIX_WIDTH = 1024
Q_LORA = 384
KV_LORA = 256
QK_NOPE = 64
QK_ROPE = 32
QK_DIM = 96
ROPE_THETA = 10000.0
CONV_W = 4
LRU_C = 8.0
ALPHA = (2.0 * 2) ** 0.25
NORM_EPS = 1e-6
MLA_IN = 1952
LRU_IN = 2048
ADAM_LR = 0.001
ADAM_B1 = 0.9
ADAM_B2 = 0.999
ADAM_EPS = 1e-08
ADAM_WD = 0.01
ADAM_STEP = 10

N_DEV = 8
LANES = 128
SUBLANES = 8
HEAD_PAD = 128
QKV_PAD = N_TOK_HEADS * HEAD_PAD
ZP = 2048
ZA_W = TOK_WIDTH
ZG_W = MIX_WIDTH + MEM_WIDTH
ZA_CQ, ZA_CKV, ZA_KR = 0, 384, 640
KR_LANE = 64

ROW_BLOCK = 512
ATT_BLOCK = 512
LOOKAHEAD = 3
FWD_HEADS = 12
BWD_HEADS = 4
VMEM_LIMIT = 56 * 1024 * 1024
NEG_BIG = -1e30
STRIP = 32
LOG2E = math.log2(math.e)


def _cp(n_axes):
    return pltpu.CompilerParams(dimension_semantics=("arbitrary",) * n_axes,
                                vmem_limit_bytes=VMEM_LIMIT)


def _dot(a, b):
    return jnp.dot(a, b, preferred_element_type=F32)


def _dot_nt(a, b):
    return lax.dot_general(a, b, (((1,), (1,)), ((), ())), preferred_element_type=F32)


def _dot_tn(a, b):
    return lax.dot_general(a, b, (((0,), (0,)), ((), ())), preferred_element_type=F32)


def _sigmoid(t):
    return 1.0 / (1.0 + jnp.exp(-t))


def _lane(shape):
    return lax.broadcasted_iota(jnp.int32, shape, len(shape) - 1)


def _full(shape):
    nd = len(shape)
    return pl.BlockSpec(shape, lambda *_: (0,) * nd)


def _rows(ts, width, col=0):
    return pl.BlockSpec((ts, width), lambda i: (i, col))


def _heads(ts):
    return pl.BlockSpec((N_TOK_HEADS, ts, HEAD_PAD), lambda i: (0, i, 0))


def _rowmm(x, w, widths, name, ts):
    s, k = x.shape
    n = w.shape[1]
    offs = [sum(widths[:a]) for a in range(len(widths))]

    def body(x_ref, w_ref, *o_refs):
        res = _dot(x_ref[...].astype(BF16), w_ref[...])
        for o_ref, off, wd in zip(o_refs, offs, widths):
            o_ref[...] = res[:, off:off + wd]

    return pl.pallas_call(
        body, grid=(s // ts,),
        in_specs=[_rows(ts, k), _full((k, n))],
        out_specs=[_rows(ts, wd) for wd in widths],
        out_shape=[jax.ShapeDtypeStruct((s, wd), F32) for wd in widths],
        name=name, compiler_params=_cp(1))(x, w)


def _rms_parts(t):
    rs = lax.rsqrt(jnp.mean(t * t, axis=-1, keepdims=True) + NORM_EPS)
    return rs


def _rope(t, c, sa, sb):
    return t * c + pltpu.roll(t, LANES - 16, 1) * sa + pltpu.roll(t, 16, 1) * sb


def _rope_t(d, c, sa, sb):
    return d * c + pltpu.roll(d * sa, 16, 1) + pltpu.roll(d * sb, LANES - 16, 1)


def _mla_prep_fwd(z0, tabs, gq, gkv, wuq, wukv, ts):
    s = z0.shape[0]

    def body(z_ref, c_ref, sa_ref, sb_ref, gq_ref, gkv_ref, wuq_ref, wukv_ref,
             q_ref, k_ref, v_ref):
        cq = z_ref[:, ZA_CQ:ZA_CQ + Q_LORA]
        ckv = z_ref[:, ZA_CKV:ZA_CKV + KV_LORA]
        kr = z_ref[:, ZA_KR:ZA_KR + LANES]
        cqn = cq * _rms_parts(cq) * gq_ref[...]
        ckvn = ckv * _rms_parts(ckv) * gkv_ref[...]
        q = _dot(cqn.astype(BF16), wuq_ref[...])
        kv = _dot(ckvn.astype(BF16), wukv_ref[...])
        c, sa, sb = c_ref[...], sa_ref[...], sb_ref[...]
        krope = _rope(kr, c, sa, sb)
        pad_lane = _lane((ts, HEAD_PAD)) >= HEAD_DIM
        for h in range(N_TOK_HEADS):
            sl = slice(h * HEAD_PAD, (h + 1) * HEAD_PAD)
            q_ref[h] = _rope(q[:, sl], c, sa, sb).astype(BF16)
            k_ref[h] = (kv[:, sl] + krope).astype(BF16)
            vh = kv[:, QKV_PAD + h * HEAD_PAD:QKV_PAD + (h + 1) * HEAD_PAD]
            v_ref[h] = jnp.where(pad_lane, 1.0, vh).astype(BF16)

    out = jax.ShapeDtypeStruct((N_TOK_HEADS, s, HEAD_PAD), BF16)
    return pl.pallas_call(
        body, grid=(s // ts,),
        in_specs=[_rows(ts, ZA_W), _rows(ts, LANES), _rows(ts, LANES), _rows(ts, LANES),
                  _full((1, Q_LORA)), _full((1, KV_LORA)),
                  _full((Q_LORA, QKV_PAD)), _full((KV_LORA, 2 * QKV_PAD))],
        out_specs=[_heads(ts)] * 3,
        out_shape=[out, out, out],
        name="mla_prep_fwd", compiler_params=_cp(1))(z0, *tabs, gq, gkv, wuq, wukv)


def _mla_prep_bwd(z0, dq, dk, dv, tabs, gq, gkv, wuq_t, wukv_t, ts):
    s = z0.shape[0]

    def body(z_ref, dq_ref, dk_ref, dv_ref, c_ref, sa_ref, sb_ref, gq_ref, gkv_ref,
             wuqt_ref, wukvt_ref, dza_ref, dzk_ref, dwuq_ref, dwukv_ref, dg_ref):
        @pl.when(pl.program_id(0) == 0)
        def _():
            dwuq_ref[...] = jnp.zeros_like(dwuq_ref)
            dwukv_ref[...] = jnp.zeros_like(dwukv_ref)
            dg_ref[...] = jnp.zeros_like(dg_ref)

        cq = z_ref[:, ZA_CQ:ZA_CQ + Q_LORA]
        ckv = z_ref[:, ZA_CKV:ZA_CKV + KV_LORA]
        rq, rkv = _rms_parts(cq), _rms_parts(ckv)
        gq_, gkv_ = gq_ref[...], gkv_ref[...]
        cqn = (cq * rq * gq_).astype(BF16)
        ckvn = (ckv * rkv * gkv_).astype(BF16)
        c, sa, sb = c_ref[...], sa_ref[...], sb_ref[...]
        dqp, dksum = [], None
        for h in range(N_TOK_HEADS):
            dqp.append(_rope_t(dq_ref[h], c, sa, sb))
            dksum = dk_ref[h] if dksum is None else dksum + dk_ref[h]
        dqp = jnp.concatenate(dqp, axis=1).astype(BF16)
        lane = _lane(dksum.shape)
        dzk_ref[...] = jnp.where((lane >= KR_LANE) & (lane < KR_LANE + QK_ROPE),
                                 _rope_t(dksum, c, sa, sb), 0.0).astype(BF16)
        dkv = jnp.concatenate([dk_ref[h].astype(BF16) for h in range(N_TOK_HEADS)]
                              + [dv_ref[h] for h in range(N_TOK_HEADS)], axis=1)
        dcqn = _dot(dqp, wuqt_ref[...])
        dckvn = _dot(dkv, wukvt_ref[...])
        dwuq_ref[...] += _dot_tn(cqn, dqp)
        dwukv_ref[...] += _dot_tn(ckvn, dkv)
        dg_ref[0:1, 0:Q_LORA] += jnp.sum(dcqn * cq * rq, axis=0, keepdims=True)
        dg_ref[0:1, Q_LORA:Q_LORA + KV_LORA] += jnp.sum(dckvn * ckv * rkv, axis=0, keepdims=True)
        wq = dcqn * gq_
        wkv = dckvn * gkv_
        dcq = rq * wq - cq * (rq * rq * rq) * jnp.mean(wq * cq, axis=-1, keepdims=True)
        dckv = rkv * wkv - ckv * (rkv * rkv * rkv) * jnp.mean(wkv * ckv, axis=-1, keepdims=True)
        dza_ref[:, 0:Q_LORA] = dcq.astype(BF16)
        dza_ref[:, Q_LORA:Q_LORA + KV_LORA] = dckv.astype(BF16)

    na = Q_LORA + KV_LORA
    return pl.pallas_call(
        body, grid=(s // ts,),
        in_specs=[_rows(ts, ZA_W), _heads(ts), _heads(ts), _heads(ts),
                  _rows(ts, LANES), _rows(ts, LANES), _rows(ts, LANES),
                  _full((1, Q_LORA)), _full((1, KV_LORA)),
                  _full((QKV_PAD, Q_LORA)), _full((2 * QKV_PAD, KV_LORA))],
        out_specs=[_rows(ts, na), _rows(ts, LANES), _full((Q_LORA, QKV_PAD)),
                   _full((KV_LORA, 2 * QKV_PAD)), _full((SUBLANES, na))],
        out_shape=[jax.ShapeDtypeStruct((s, na), BF16), jax.ShapeDtypeStruct((s, LANES), BF16),
                   jax.ShapeDtypeStruct((Q_LORA, QKV_PAD), F32),
                   jax.ShapeDtypeStruct((KV_LORA, 2 * QKV_PAD), F32),
                   jax.ShapeDtypeStruct((SUBLANES, na), F32)],
        name="mla_prep_bwd", compiler_params=_cp(1))(
            z0, dq, dk, dv, *tabs, gq, gkv, wuq_t, wukv_t)


def _causal_pairs(nb, by_key):
    if by_key:
        pairs = [(i, j) for j in range(nb) for i in range(j, nb)]
    else:
        pairs = [(i, j) for i in range(nb) for j in range(i + 1)]
    return (jnp.array([p[0] for p in pairs], jnp.int32),
            jnp.array([p[1] for p in pairs], jnp.int32))


def _flash_fwd(q, k, v, t, nh):
    s = q.shape[1]
    itab, jtab = _causal_pairs(s // t, False)
    c2 = LOG2E / math.sqrt(QK_DIM)

    def body(it_ref, jt_ref, q_ref, k_ref, v_ref, o_ref, lse_ref, m_scr, acc_scr):
        pair = pl.program_id(1)
        i, j = it_ref[pair], jt_ref[pair]

        @pl.when(j == 0)
        def _():
            m_scr[...] = jnp.full_like(m_scr, NEG_BIG)
            acc_scr[...] = jnp.zeros_like(acc_scr)

        def softmax_strips(masked, hs, sc, row0):
            ps, als = [], []
            for r0 in range(0, sc.shape[0], STRIP):
                rows = slice(row0 + r0, row0 + r0 + STRIP)
                ch = [sc[r0:r0 + STRIP, n * LANES:(n + 1) * LANES] * c2
                      for n in range(sc.shape[1] // LANES)]
                if masked:
                    rr = row0 + r0 + lax.broadcasted_iota(jnp.int32, (STRIP, LANES), 0)
                    cc = lax.broadcasted_iota(jnp.int32, (STRIP, LANES), 1)
                    ch = [jnp.where(cc + n * LANES <= rr, c_, NEG_BIG) for n, c_ in enumerate(ch)]
                mx = ch[0]
                for c_ in ch[1:]:
                    mx = jnp.maximum(mx, c_)
                m_prev = m_scr[hs, rows, :]
                m_next = jnp.maximum(m_prev, jnp.max(mx, axis=-1, keepdims=True))
                ps.append(jnp.concatenate(
                    [jnp.exp2(c_ - m_next).astype(BF16) for c_ in ch], axis=1))
                als.append(jnp.exp2(m_prev - m_next))
                m_scr[hs, rows, :] = m_next
            return jnp.concatenate(ps, axis=0), jnp.concatenate(als, axis=0)

        def run(masked, parts):
            def scores_of(hs):
                return [_dot_nt(q_ref[hs, r0:r0 + nr, :], k_ref[hs, 0:nk, :])
                        for r0, nr, nk in parts]

            ahead = min(LOOKAHEAD, nh)
            scores = [scores_of(hs) for hs in range(ahead)]
            for hs in range(nh):
                if hs + ahead < nh:
                    scores.append(scores_of(hs + ahead))
                for (r0, nr, nk), sc in zip(parts, scores[hs]):
                    p, alpha = softmax_strips(masked, hs, sc, r0)
                    acc_scr[hs, r0:r0 + nr, :] = (alpha * acc_scr[hs, r0:r0 + nr, :]
                                                  + _dot(p, v_ref[hs, 0:nk, :]))

        @pl.when(j < i)
        def _():
            run(False, [(0, t, t)])

        @pl.when(j == i)
        def _():
            run(True, [(0, t, t)])
            for h in range(nh):
                acc = acc_scr[h]
                l = acc[:, HEAD_DIM:HEAD_DIM + 1]
                o_ref[h] = jnp.where(_lane(acc.shape) < HEAD_DIM, acc / l, 0.0)
                lse_ref[h] = (m_scr[h] + jnp.log2(l)).T[0:1, :]

    qspec = pl.BlockSpec((nh, t, HEAD_PAD), lambda h, p, it, jt: (h, it[p], 0))
    kspec = pl.BlockSpec((nh, t, HEAD_PAD), lambda h, p, it, jt: (h, jt[p], 0))
    lspec = pl.BlockSpec((nh, 1, t), lambda h, p, it, jt: (h, 0, it[p]))
    out = jax.ShapeDtypeStruct((N_TOK_HEADS, s, HEAD_PAD), F32)
    return pl.pallas_call(
        body,
        grid_spec=pltpu.PrefetchScalarGridSpec(
            num_scalar_prefetch=2, grid=(N_TOK_HEADS // nh, itab.shape[0]),
            in_specs=[qspec, kspec, kspec], out_specs=[qspec, lspec],
            scratch_shapes=[pltpu.VMEM((nh, t, HEAD_PAD), F32)] * 2),
        out_shape=[out, jax.ShapeDtypeStruct((N_TOK_HEADS, 1, s), F32)],
        name="flash_fwd", compiler_params=_cp(2))(itab, jtab, q, k, v)


def _flash_bwd(q, k, v, stats, do, t, nh, ex):
    s = q.shape[1]
    nb = s // t
    itab, jtab = _causal_pairs(nb, True)
    npairs = itab.shape[0]
    ngroups = N_TOK_HEADS // nh
    scale = 1.0 / math.sqrt(QK_DIM)
    c2 = LOG2E * scale
    nx = ex.n if ex is not None else 0
    ex_arrays, ex_out_shape, ex_scratch = (
        (ex.arrays, ex.out_shape, ex.scratch) if ex is not None else ([], [], []))

    def body(it_ref, jt_ref, q_ref, k_ref, v_ref, st_ref, do_ref, *rest):
        ex_in, rest = rest[:nx], rest[nx:]
        dq_ref, dk_ref, dv_ref = rest[:3]
        ex_out, rest = rest[3:3 + nx], rest[3 + nx:]
        dk_scr, dv_scr = rest[:2]
        ex_sems = rest[2:]
        pair = pl.program_id(1)
        i, j = it_ref[pair], jt_ref[pair]
        rows_i = pl.ds(pl.multiple_of(i * t, t), t)

        if nx:
            @pl.when(jnp.logical_and(pl.program_id(0) == 0, pair == 0))
            def _():
                for cp in ex.copies(ex_in, ex_out, ex_sems):
                    cp.start()

        @pl.when(i == j)
        def _():
            dk_scr[...] = jnp.zeros_like(dk_scr)
            dv_scr[...] = jnp.zeros_like(dv_scr)

        @pl.when(j == 0)
        def _():
            dq_ref[:, rows_i, :] = jnp.zeros((nh, t, HEAD_PAD), F32)

        def prob_strips(masked, h, sct, dpt, k0, q0):
            ps, dss = [], []
            for r0 in range(0, sct.shape[0], STRIP):
                rows = slice(r0, r0 + STRIP)
                if masked:
                    kk = k0 + r0 + lax.broadcasted_iota(jnp.int32, (STRIP, LANES), 0)
                    qq = q0 + lax.broadcasted_iota(jnp.int32, (STRIP, LANES), 1)
                pcs, dcs = [], []
                for n in range(sct.shape[1] // LANES):
                    cols = slice(n * LANES, (n + 1) * LANES)
                    qcols = slice(q0 + n * LANES, q0 + (n + 1) * LANES)
                    x = sct[rows, cols] * c2
                    if masked:
                        x = jnp.where(kk <= qq + n * LANES, x, NEG_BIG)
                    p = jnp.exp2(x - st_ref[h, 0:1, qcols])
                    pcs.append(p.astype(BF16))
                    dcs.append((p * (dpt[rows, cols] - st_ref[h, 1:2, qcols]) * scale).astype(BF16))
                ps.append(jnp.concatenate(pcs, axis=1))
                dss.append(jnp.concatenate(dcs, axis=1))
            return jnp.concatenate(ps, axis=0), jnp.concatenate(dss, axis=0)

        def run(masked, parts):
            def scores_of(h):
                return [(_dot_nt(k_ref[h, k0:k0 + nk, :], q_ref[h, q0:q0 + nq, :]),
                         _dot_nt(v_ref[h, k0:k0 + nk, :], do_ref[h, q0:q0 + nq, :]))
                        for k0, nk, q0, nq in parts]

            ahead = min(LOOKAHEAD, nh)
            scores = [scores_of(h) for h in range(ahead)]
            for h in range(nh):
                if h + ahead < nh:
                    scores.append(scores_of(h + ahead))
                for (k0, nk, q0, nq), (sct, dpt) in zip(parts, scores[h]):
                    pt, dst = prob_strips(masked, h, sct, dpt, k0, q0)
                    dv_scr[h, k0:k0 + nk, :] += _dot(pt, do_ref[h, q0:q0 + nq, :])
                    dk_scr[h, k0:k0 + nk, :] += _dot(dst, q_ref[h, q0:q0 + nq, :])
                    rows = pl.ds(pl.multiple_of(i * t + q0, t // 2), nq)
                    dq_ref[h, rows, :] += _dot_tn(dst, k_ref[h, k0:k0 + nk, :])

        @pl.when(i > j)
        def _():
            run(False, [(0, t, 0, t)])

        @pl.when(i == j)
        def _():
            run(True, [(0, t // 2, 0, t), (t // 2, t // 2, t // 2, t // 2)])

        @pl.when(i == nb - 1)
        def _():
            dk_ref[...] = dk_scr[...]
            dv_ref[...] = dv_scr[...].astype(BF16)

        if nx:
            @pl.when(jnp.logical_and(pl.program_id(0) == ngroups - 1, pair == npairs - 1))
            def _():
                for cp in ex.copies(ex_in, ex_out, ex_sems):
                    cp.wait()

    qspec = pl.BlockSpec((nh, t, HEAD_PAD), lambda h, p, it, jt: (h, it[p], 0))
    kspec = pl.BlockSpec((nh, t, HEAD_PAD), lambda h, p, it, jt: (h, jt[p], 0))
    dqspec = pl.BlockSpec((nh, s, HEAD_PAD), lambda h, p, it, jt: (h, 0, 0))
    stspec = pl.BlockSpec((nh, 2, t), lambda h, p, it, jt: (h, 0, it[p]))
    out = jax.ShapeDtypeStruct((N_TOK_HEADS, s, HEAD_PAD), F32)
    res = pl.pallas_call(
        body,
        grid_spec=pltpu.PrefetchScalarGridSpec(
            num_scalar_prefetch=2, grid=(ngroups, npairs),
            in_specs=[qspec, kspec, kspec, stspec, qspec] + [ANY] * nx,
            out_specs=[dqspec, kspec, kspec] + [ANY] * nx,
            scratch_shapes=[pltpu.VMEM((nh, t, HEAD_PAD), F32)] * 2 + ex_scratch),
        out_shape=[out, out, jax.ShapeDtypeStruct(out.shape, BF16)] + ex_out_shape,
        name="flash_bwd", compiler_params=_cp(2))(itab, jtab, q, k, v, stats, do, *ex_arrays)
    return res[:3], res[3:]


def _mem_probs(qp, kp, hh):
    lane = _lane(qp.shape)
    keep = (lane < HEAD_DIM) if hh == 0 else (lane >= HEAD_DIM)
    qh = jnp.where(keep, qp, 0.0).astype(BF16)
    sc = _dot_nt(qh, kp) * (1.0 / math.sqrt(HEAD_DIM))
    e = jnp.exp(sc - jnp.max(sc, axis=-1, keepdims=True))
    return e / jnp.sum(e, axis=-1, keepdims=True), keep


def _gate_mem_fwd(tok, z, memkv, g0, q0, padded, name, ts):
    s = z.shape[0]
    zw = z.shape[1]
    tok_spec = _heads(ts) if padded else _rows(ts, TOK_WIDTH)

    def body(tok_ref, z_ref, mkv_ref, cat_ref, y_ref):
        if padded:
            for p in range(N_TOK_HEADS // 2):
                cat_ref[:, p * LANES:(p + 1) * LANES] = (
                    tok_ref[2 * p] + pltpu.roll(tok_ref[2 * p + 1], HEAD_DIM, 1))
        else:
            cat_ref[:, 0:TOK_WIDTH] = tok_ref[...]
        for pr in range(N_MEM_HEADS // 2):
            sl = slice(pr * LANES, (pr + 1) * LANES)
            qp = z_ref[:, q0 + pr * LANES:q0 + (pr + 1) * LANES]
            kp = mkv_ref[:, sl].astype(BF16)
            vp = mkv_ref[:, MEM_WIDTH + pr * LANES:MEM_WIDTH + (pr + 1) * LANES].astype(BF16)
            outs = []
            for hh in range(2):
                p, _ = _mem_probs(qp, kp, hh)
                outs.append(_dot(p.astype(BF16), vp))
            lane = _lane(outs[0].shape)
            cat_ref[:, TOK_WIDTH + pr * LANES:TOK_WIDTH + (pr + 1) * LANES] = jnp.where(
                lane < HEAD_DIM, outs[0], outs[1])
        gate = z_ref[:, g0:g0 + MIX_WIDTH]
        y_ref[...] = (cat_ref[...] * (gate * _sigmoid(gate))).astype(BF16)

    return pl.pallas_call(
        body, grid=(s // ts,),
        in_specs=[tok_spec, _rows(ts, zw), _full((MEM_LEN, 2 * MEM_WIDTH))],
        out_specs=[_rows(ts, MIX_WIDTH)] * 2,
        out_shape=[jax.ShapeDtypeStruct((s, MIX_WIDTH), F32),
                   jax.ShapeDtypeStruct((s, MIX_WIDTH), BF16)],
        name=name, compiler_params=_cp(1))(tok, z, memkv)


def _gate_mem_bwd(dy, cat, z, memkv, lse, g0, q0, name, ts):
    s = z.shape[0]
    zw = z.shape[1]
    padded = lse is not None
    gq_w = MIX_WIDTH + MEM_WIDTH

    def body(*refs):
        if padded:
            dy_ref, cat_ref, z_ref, mkv_ref, lse_ref, dzg_ref, dtok_ref, dmkv_ref, st_ref = refs
        else:
            dy_ref, cat_ref, z_ref, mkv_ref, dzg_ref, dtok_ref, dmkv_ref = refs

        @pl.when(pl.program_id(0) == 0)
        def _():
            dmkv_ref[...] = jnp.zeros_like(dmkv_ref)

        gate = z_ref[:, g0:g0 + MIX_WIDTH]
        sg = _sigmoid(gate)
        dy_ = dy_ref[...]
        dzg_ref[:, 0:MIX_WIDTH] = (dy_ * cat_ref[...]
                                   * (sg * (1.0 + gate * (1.0 - sg)))).astype(BF16)
        dcat = dy_ * (gate * sg)
        if padded:
            low = _lane((ts, LANES)) < HEAD_DIM
            for p in range(N_TOK_HEADS // 2):
                d = dcat[:, p * LANES:(p + 1) * LANES]
                prod = d * cat_ref[:, p * LANES:(p + 1) * LANES]
                first = jnp.sum(jnp.where(low, prod, 0.0), axis=-1, keepdims=True)
                second = jnp.sum(jnp.where(low, 0.0, prod), axis=-1, keepdims=True)
                dtok_ref[2 * p] = jnp.where(low, d, 0.0).astype(BF16)
                dtok_ref[2 * p + 1] = jnp.where(low, pltpu.roll(d, HEAD_DIM, 1), 0.0).astype(BF16)
                for hh, delta in ((2 * p, first), (2 * p + 1, second)):
                    st_ref[hh, 0:1, :] = lse_ref[hh]
                    st_ref[hh, 1:2, :] = jnp.broadcast_to(delta, (ts, LANES)).T[0:1, :]
        else:
            dtok_ref[...] = dcat[:, 0:TOK_WIDTH]
        for pr in range(N_MEM_HEADS // 2):
            sl = slice(pr * LANES, (pr + 1) * LANES)
            vsl = slice(MEM_WIDTH + pr * LANES, MEM_WIDTH + (pr + 1) * LANES)
            qp = z_ref[:, q0 + pr * LANES:q0 + (pr + 1) * LANES]
            qpb = qp.astype(BF16)
            kp = mkv_ref[:, sl].astype(BF16)
            vp = mkv_ref[:, vsl].astype(BF16)
            dmo = dcat[:, TOK_WIDTH + pr * LANES:TOK_WIDTH + (pr + 1) * LANES]
            dqp = None
            for hh in range(2):
                p, keep = _mem_probs(qp, kp, hh)
                do_h = jnp.where(keep, dmo, 0.0).astype(BF16)
                dmkv_ref[:, vsl] += _dot_tn(p.astype(BF16), do_h)
                dp = _dot_nt(do_h, vp)
                ds = (p * (dp - jnp.sum(dp * p, axis=-1, keepdims=True))
                      * (1.0 / math.sqrt(HEAD_DIM))).astype(BF16)
                dqh = jnp.where(keep, _dot(ds, kp), 0.0)
                dqp = dqh if dqp is None else dqp + dqh
                dkh = _dot_tn(ds, qpb)
                klane = _lane(dkh.shape)
                kkeep = (klane < HEAD_DIM) if hh == 0 else (klane >= HEAD_DIM)
                dmkv_ref[:, sl] += jnp.where(kkeep, dkh, 0.0)
            dzg_ref[:, MIX_WIDTH + pr * LANES:MIX_WIDTH + (pr + 1) * LANES] = dqp.astype(BF16)

    in_specs = [_rows(ts, MIX_WIDTH), _rows(ts, MIX_WIDTH), _rows(ts, zw),
                _full((MEM_LEN, 2 * MEM_WIDTH))]
    out_specs = [_rows(ts, gq_w), _heads(ts) if padded else _rows(ts, TOK_WIDTH),
                 _full((MEM_LEN, 2 * MEM_WIDTH))]
    heads_shape = (N_TOK_HEADS, s, HEAD_PAD)
    out_shape = [jax.ShapeDtypeStruct((s, gq_w), BF16),
                 jax.ShapeDtypeStruct(heads_shape, BF16) if padded
                 else jax.ShapeDtypeStruct((s, TOK_WIDTH), F32),
                 jax.ShapeDtypeStruct((MEM_LEN, 2 * MEM_WIDTH), F32)]
    args = [dy, cat, z, memkv]
    if padded:
        in_specs.append(pl.BlockSpec((N_TOK_HEADS, 1, ts), lambda i: (0, 0, i)))
        out_specs.append(pl.BlockSpec((N_TOK_HEADS, 2, ts), lambda i: (0, 0, i)))
        out_shape.append(jax.ShapeDtypeStruct((N_TOK_HEADS, 2, s), F32))
        args.append(lse)
    return pl.pallas_call(
        body, grid=(s // ts,), in_specs=in_specs, out_specs=out_specs, out_shape=out_shape,
        name=name, compiler_params=_cp(1))(*args)


def _ln_stats(pre):
    mu = jnp.mean(pre, axis=-1, keepdims=True)
    d = pre - mu
    rstd = lax.rsqrt(jnp.mean(d * d, axis=-1, keepdims=True) + NORM_EPS)
    return d * rstd, rstd


def _outproj_ln_fwd(y, w, h, g, b, tgt, name, ts):
    s = y.shape[0]
    with_loss = tgt is not None

    def body(*refs):
        if with_loss:
            y_ref, w_ref, h_ref, g_ref, b_ref, t_ref, pre_ref, out_ref, loss_ref = refs
        else:
            y_ref, w_ref, h_ref, g_ref, b_ref, pre_ref, out_ref = refs
        pre = ALPHA * h_ref[...] + _dot(y_ref[...].astype(BF16), w_ref[...])
        pre_ref[...] = pre
        xhat, _ = _ln_stats(pre)
        hout = xhat * g_ref[...] + b_ref[...]
        if with_loss:
            @pl.when(pl.program_id(0) == 0)
            def _():
                loss_ref[...] = jnp.zeros_like(loss_ref)
            err = hout - t_ref[...]
            out_ref[...] = err * (1.0 / D_MODEL)
            loss_ref[...] += 0.5 * jnp.sum(jnp.mean(err * err, axis=-1, keepdims=True))
        else:
            out_ref[...] = hout

    act = jax.ShapeDtypeStruct((s, D_MODEL), F32)
    in_specs = [_rows(ts, MIX_WIDTH), _full((MIX_WIDTH, D_MODEL)), _rows(ts, D_MODEL),
                _full((1, D_MODEL)), _full((1, D_MODEL))]
    out_specs = [_rows(ts, D_MODEL)] * 2
    out_shape = [act, act]
    args = [y, w, h, g, b]
    if with_loss:
        in_specs.append(_rows(ts, D_MODEL))
        out_specs.append(_full((SUBLANES, LANES)))
        out_shape.append(jax.ShapeDtypeStruct((SUBLANES, LANES), F32))
        args.append(tgt)
    return pl.pallas_call(
        body, grid=(s // ts,), in_specs=in_specs, out_specs=out_specs, out_shape=out_shape,
        name=name, compiler_params=_cp(1))(*args)


def _outproj_ln_bwd(dh, pre, g, y, w_t, name, ts):
    s = y.shape[0]

    def body(dh_ref, pre_ref, g_ref, y_ref, wt_ref, dpre_ref, dy_ref, dw_ref, dgb_ref):
        @pl.when(pl.program_id(0) == 0)
        def _():
            dw_ref[...] = jnp.zeros_like(dw_ref)
            dgb_ref[...] = jnp.zeros_like(dgb_ref)

        dh_ = dh_ref[...]
        xhat, rstd = _ln_stats(pre_ref[...])
        dxh = dh_ * g_ref[...]
        dpre = rstd * (dxh - jnp.mean(dxh, axis=-1, keepdims=True)
                       - xhat * jnp.mean(dxh * xhat, axis=-1, keepdims=True))
        dpre_ref[...] = dpre
        dgb_ref[0:1, :] += jnp.sum(dh_ * xhat, axis=0, keepdims=True)
        dgb_ref[1:2, :] += jnp.sum(dh_, axis=0, keepdims=True)
        dpb = dpre.astype(BF16)
        dy_ref[...] = _dot(dpb, wt_ref[...])
        dw_ref[...] += _dot_tn(y_ref[...].astype(BF16), dpb)

    act = jax.ShapeDtypeStruct((s, D_MODEL), F32)
    return pl.pallas_call(
        body, grid=(s // ts,),
        in_specs=[_rows(ts, D_MODEL), _rows(ts, D_MODEL), _full((1, D_MODEL)),
                  _rows(ts, MIX_WIDTH), _full((D_MODEL, MIX_WIDTH))],
        out_specs=[_rows(ts, D_MODEL), _rows(ts, MIX_WIDTH), _full((MIX_WIDTH, D_MODEL)),
                   _full((SUBLANES, D_MODEL))],
        out_shape=[act, act, jax.ShapeDtypeStruct((MIX_WIDTH, D_MODEL), F32),
                   jax.ShapeDtypeStruct((SUBLANES, D_MODEL), F32)],
        name=name, compiler_params=_cp(1))(dh, pre, g, y, w_t)


def _linear_bwd(x, dys, offs, w_t, resid, name, ts):
    s, kdim = x.shape
    n = w_t.shape[0]
    widths = [d.shape[1] for d in dys]
    npieces = len(dys)

    def body(*refs):
        x_ref = refs[0]
        dy_refs = refs[1:1 + npieces]
        wt_ref, r_ref, dx_ref, dw_ref = refs[1 + npieces:]

        @pl.when(pl.program_id(0) == 0)
        def _():
            dw_ref[...] = jnp.zeros_like(dw_ref)

        xb = x_ref[...].astype(BF16)
        dx = ALPHA * r_ref[...]
        for dy_ref, off, wd in zip(dy_refs, offs, widths):
            dyb = dy_ref[...].astype(BF16)
            dx = dx + _dot(dyb, wt_ref[off:off + wd, :])
            dw_ref[:, off:off + wd] += _dot_tn(xb, dyb)
        dx_ref[...] = dx

    return pl.pallas_call(
        body, grid=(s // ts,),
        in_specs=[_rows(ts, kdim)] + [_rows(ts, wd) for wd in widths]
                 + [_full((n, kdim)), _rows(ts, kdim)],
        out_specs=[_rows(ts, kdim), _full((kdim, n))],
        out_shape=[jax.ShapeDtypeStruct((s, kdim), F32), jax.ShapeDtypeStruct((kdim, n), F32)],
        name=name, compiler_params=_cp(1))(x, *dys, w_t, resid)


def _wgrad_small(x, dy, name):
    def body(x_ref, dy_ref, dw_ref):
        dw_ref[...] = _dot_tn(x_ref[...].astype(BF16), dy_ref[...].astype(BF16))

    return pl.pallas_call(
        body, out_shape=jax.ShapeDtypeStruct((x.shape[1], dy.shape[1]), F32),
        name=name, compiler_params=pltpu.CompilerParams(vmem_limit_bytes=VMEM_LIMIT))(x, dy)


def _shift_down(u, carry8, k):
    if k == 0:
        return u
    rolled = pltpu.roll(u, k, 0)
    row = lax.broadcasted_iota(jnp.int32, carry8.shape, 0)
    top = jnp.where(row < k, pltpu.roll(carry8, k, 0), rolled[0:SUBLANES])
    return jnp.concatenate([top, rolled[SUBLANES:]], axis=0)


def _shift_up(u, carry8, k):
    if k == 0:
        return u
    n = u.shape[0]
    rolled = pltpu.roll(u, n - k, 0)
    row = lax.broadcasted_iota(jnp.int32, carry8.shape, 0)
    bot = jnp.where(row >= SUBLANES - k, pltpu.roll(carry8, SUBLANES - k, 0),
                    rolled[n - SUBLANES:])
    return jnp.concatenate([rolled[:n - SUBLANES], bot], axis=0)


def _neg_expm1(t):
    e = jnp.exp(t)
    em1 = e - 1.0
    safe = jnp.where(e == 1.0, 1.0, jnp.log(e))
    return -jnp.where(e == 1.0, t, jnp.where(em1 == -1.0, -1.0, em1 * t / safe))


def _lru_gates(u, carry8, cw_ref, vec_ref, wr_ref, wi_ref):
    taps = [_shift_down(u, carry8, k) for k in range(CONV_W)]
    xc = vec_ref[0:1, :] + cw_ref[3:4, :] * u
    for k in range(1, CONV_W):
        xc = xc + cw_ref[3 - k:4 - k, :] * taps[k]
    xb = xc.astype(BF16)
    r = _sigmoid(_dot(xb, wr_ref[...]) + vec_ref[1:2, :])
    ig = _sigmoid(_dot(xb, wi_ref[...]) + vec_ref[2:3, :])
    nlam = -vec_ref[3:4, :]
    softplus = jnp.maximum(nlam, 0.0) + jnp.log(1.0 + jnp.exp(-jnp.abs(nlam)))
    cneg = -LRU_C * softplus
    log_a = cneg * r
    a = jnp.exp(log_a)
    sq = jnp.sqrt(_neg_expm1(2.0 * log_a))
    return xc, r, ig, cneg, a, sq, taps


def _lru_fwd(z1, cw8, vec8, wr, wi, ts):
    s = z1.shape[0]

    def body(u_ref, cw_ref, vec_ref, wr_ref, wi_ref, hs_ref, cu_scr, ch_scr, a_scr, gx_scr):
        @pl.when(pl.program_id(0) == 0)
        def _():
            cu_scr[...] = jnp.zeros_like(cu_scr)
            ch_scr[...] = jnp.zeros_like(ch_scr)

        u = u_ref[...]
        xc, _, ig, _, a, sq, _ = _lru_gates(u, cu_scr[...], cw_ref, vec_ref, wr_ref, wi_ref)
        a_scr[...] = a
        gx_scr[...] = sq * (ig * xc)

        def step(t, h):
            h = a_scr[pl.ds(t, 1), :] * h + gx_scr[pl.ds(t, 1), :]
            hs_ref[pl.ds(t, 1), :] = h
            return h

        ch_scr[0:1, :] = lax.fori_loop(0, ts, step, ch_scr[0:1, :])
        cu_scr[...] = u[ts - SUBLANES:, :]

    w = TOK_WIDTH
    return pl.pallas_call(
        body, grid=(s // ts,),
        in_specs=[_rows(ts, w), _full((SUBLANES, w)), _full((SUBLANES, w)),
                  _full((w, w)), _full((w, w))],
        out_specs=_rows(ts, w),
        out_shape=jax.ShapeDtypeStruct((s, w), F32),
        scratch_shapes=[pltpu.VMEM((SUBLANES, w), F32), pltpu.VMEM((SUBLANES, w), F32),
                        pltpu.VMEM((ts, w), F32), pltpu.VMEM((ts, w), F32)],
        name="lru_fwd", compiler_params=_cp(1))(z1, cw8, vec8, wr, wi)


def _lru_bwd(z1, dhs, hs, cw8, vec8, wr, wi, wr_t, wi_t, ts):
    s = z1.shape[0]
    nb = s // ts
    w = TOK_WIDTH
    tiles = ts // SUBLANES

    def body(u_ref, up_ref, dhs_ref, hs_ref, hsp_ref, cw_ref, vec_ref, wr_ref, wi_ref,
             wrt_ref, wit_ref, du_ref, dwr_ref, dwi_ref, dvec_ref,
             cc_scr, cd_scr, a_scr, dh_scr):
        i = pl.program_id(0)

        @pl.when(i == 0)
        def _():
            cc_scr[...] = jnp.zeros_like(cc_scr)
            cd_scr[...] = jnp.zeros_like(cd_scr)
            dwr_ref[...] = jnp.zeros_like(dwr_ref)
            dwi_ref[...] = jnp.zeros_like(dwi_ref)
            dvec_ref[...] = jnp.zeros_like(dvec_ref)

        u = u_ref[...]
        first = i == nb - 1
        carry8 = jnp.where(first, 0.0, up_ref[...])
        xc, r, ig, cneg, a, sq, taps = _lru_gates(u, carry8, cw_ref, vec_ref, wr_ref, wi_ref)
        a_scr[...] = a

        def step(n, c):
            t = ts - 1 - n
            dh = dhs_ref[pl.ds(t, 1), :] + c
            dh_scr[pl.ds(t, 1), :] = dh
            return a_scr[pl.ds(t, 1), :] * dh

        cc_scr[0:1, :] = lax.fori_loop(0, ts, step, cc_scr[0:1, :])
        dh = dh_scr[...]
        hprev = _shift_down(hs_ref[...], jnp.where(first, 0.0, hsp_ref[...]), 1)
        ix = ig * xc
        dix = dh * sq
        dlog_a = dh * hprev * a - (dh * ix) * (a * a) / sq
        dpr = (dlog_a * cneg) * r * (1.0 - r)
        dpi = (dix * xc) * ig * (1.0 - ig)
        dprb, dpib = dpr.astype(BF16), dpi.astype(BF16)
        xb = xc.astype(BF16)
        dwr_ref[...] += _dot_tn(xb, dprb)
        dwi_ref[...] += _dot_tn(xb, dpib)
        dxc = dix * ig + _dot(dprb, wrt_ref[...]) + _dot(dpib, wit_ref[...])
        for k in range(CONV_W):
            dvec_ref[3 - k:4 - k, :] += jnp.sum(dxc * taps[k], axis=0, keepdims=True)
        dvec_ref[4:5, :] += jnp.sum(dxc, axis=0, keepdims=True)
        dvec_ref[5:6, :] += jnp.sum(dpr, axis=0, keepdims=True)
        dvec_ref[6:7, :] += jnp.sum(dpi, axis=0, keepdims=True)
        dvec_ref[7:8, :] += (jnp.sum(dlog_a * r, axis=0, keepdims=True)
                             * (LRU_C * _sigmoid(-vec_ref[3:4, :])))
        nxt = cd_scr[...]
        du = cw_ref[3:4, :] * dxc
        for k in range(1, CONV_W):
            du = du + cw_ref[3 - k:4 - k, :] * _shift_up(dxc, nxt, k)
        du_ref[...] = du.astype(BF16)
        cd_scr[...] = dxc[0:SUBLANES, :]

    rev = lambda i: (nb - 1 - i, 0)
    prev8 = lambda i: (jnp.maximum((nb - 1 - i) * tiles - 1, 0), 0)
    blk = pl.BlockSpec((ts, w), rev)
    before = pl.BlockSpec((SUBLANES, w), prev8)
    scr = pltpu.VMEM((ts, w), F32)
    return pl.pallas_call(
        body, grid=(nb,),
        in_specs=[blk, before, blk, blk, before,
                  _full((SUBLANES, w)), _full((SUBLANES, w)),
                  _full((w, w)), _full((w, w)), _full((w, w)), _full((w, w))],
        out_specs=[blk, _full((w, w)), _full((w, w)), _full((SUBLANES, w))],
        out_shape=[jax.ShapeDtypeStruct((s, w), BF16), jax.ShapeDtypeStruct((w, w), F32),
                   jax.ShapeDtypeStruct((w, w), F32), jax.ShapeDtypeStruct((SUBLANES, w), F32)],
        scratch_shapes=[pltpu.VMEM((SUBLANES, w), F32), pltpu.VMEM((SUBLANES, w), F32),
                        scr, scr],
        name="lru_bwd", compiler_params=_cp(1))(
            z1, z1, dhs, hs, hs, cw8, vec8, wr, wi, wr_t, wi_t)


def _adamw(parts, w, m, v, name):
    n = len(parts)
    rows_per = parts[0].shape[1]

    def body(*refs):
        p_refs = refs[:n]
        w_ref, m_ref, v_ref, g_ref, d_ref, nm_ref, nv_ref = refs[n:]
        for l, p_ref in enumerate(p_refs):
            rows = slice(l * rows_per, (l + 1) * rows_per)
            g = p_ref[0].astype(F32)
            for dev in range(1, N_DEV):
                g = g + p_ref[dev].astype(F32)
            g_ref[rows, :] = g
            nm = ADAM_B1 * m_ref[rows, :] + (1.0 - ADAM_B1) * g
            nv = ADAM_B2 * v_ref[rows, :] + (1.0 - ADAM_B2) * (g * g)
            m_hat = nm / (1.0 - ADAM_B1 ** ADAM_STEP)
            v_hat = nv / (1.0 - ADAM_B2 ** ADAM_STEP)
            d_ref[rows, :] = -ADAM_LR * (m_hat / (jnp.sqrt(v_hat) + ADAM_EPS)
                                         + ADAM_WD * w_ref[rows, :])
            nm_ref[rows, :] = nm
            nv_ref[rows, :] = nv

    out = jax.ShapeDtypeStruct(w.shape, F32)
    return pl.pallas_call(
        body, out_shape=[out] * 4, name=name,
        compiler_params=pltpu.CompilerParams(vmem_limit_bytes=VMEM_LIMIT))(*parts, w, m, v)


ANY = pl.BlockSpec(memory_space=pl.ANY)
MESH = pl.DeviceIdType.MESH


def _slot(p):
    return 4 * p[0] + 2 * p[1] + p[2]


def _allgather(xs):
    n = len(xs)

    def body(*refs):
        x_refs, o_refs = refs[:n], refs[n:2 * n]
        send_sems, recv_sems, local_sems = refs[2 * n:]
        x, y, c = lax.axis_index("x"), lax.axis_index("y"), lax.axis_index("c")
        me, sibling = (x, y, c), (x, y, 1 - c)
        chips = [(1 - x, y), (x, 1 - y), (1 - x, 1 - y)]

        def copy(a, k, block, to, from_input=False):
            dst = o_refs[a].at[_slot(block)]
            return pltpu.make_async_remote_copy(
                src_ref=x_refs[a] if from_input else dst, dst_ref=dst,
                send_sem=send_sems.at[a, k], recv_sem=recv_sems.at[a, k],
                device_id=to, device_id_type=MESH)

        mine = [pltpu.make_async_copy(x_refs[a], o_refs[a].at[_slot(me)], local_sems.at[a])
                for a in range(n)]
        for cp in mine:
            cp.start()
        first = []
        for a in range(n):
            first.append(copy(a, 0, me, sibling, True))
            first += [copy(a, 1 + j, me, (*chip, c), True) for j, chip in enumerate(chips)]
        for cp in first:
            cp.start()
        passed = []
        for j, chip in enumerate(chips):
            for a in range(n):
                copy(a, 1 + j, (*chip, c), me).wait_recv()
                cp = copy(a, 4 + j, (*chip, c), sibling)
                cp.start()
                passed.append(cp)
        for a in range(n):
            copy(a, 0, sibling, me).wait_recv()
            for j, chip in enumerate(chips):
                copy(a, 4 + j, (*chip, 1 - c), me).wait_recv()
        for cp in first + passed:
            cp.wait_send()
        for cp in mine:
            cp.wait()

    return pl.pallas_call(
        body,
        out_shape=[jax.ShapeDtypeStruct((N_DEV,) + t.shape, t.dtype) for t in xs],
        in_specs=[ANY] * n, out_specs=[ANY] * n,
        scratch_shapes=[pltpu.SemaphoreType.DMA((n, 7)), pltpu.SemaphoreType.DMA((n, 7)),
                        pltpu.SemaphoreType.DMA((n,))],
        name="allgather_weights")(*xs)


class _Exchange:
    def __init__(self, arrays, kinds):
        self.arrays, self.kinds, self.n = list(arrays), list(kinds), len(arrays)
        self.shapes = [self._part_shape(a, k) for a, k in zip(arrays, kinds)]
        self.out_shape = [jax.ShapeDtypeStruct((N_DEV,) + shp, a.dtype)
                          for shp, a in zip(self.shapes, arrays)]
        self.scratch = [pltpu.SemaphoreType.DMA((self.n, N_DEV - 1)),
                        pltpu.SemaphoreType.DMA((self.n, N_DEV - 1)),
                        pltpu.SemaphoreType.DMA((self.n,))]

    @staticmethod
    def _part_shape(arr, kind):
        if kind == "chunks":
            return arr.shape[1:]
        if kind == "cols":
            return (arr.shape[0], arr.shape[1] // N_DEV)
        if kind == "rows":
            return (arr.shape[0] // N_DEV, arr.shape[1])
        return arr.shape

    def copies(self, in_refs, out_refs, sems):
        send_sems, recv_sems, local_sems = sems
        x, y, c = lax.axis_index("x"), lax.axis_index("y"), lax.axis_index("c")
        me = _slot((x, y, c))

        def part(a, dev):
            ref, kind, shp = in_refs[a], self.kinds[a], self.shapes[a]
            if kind == "chunks":
                return ref.at[dev]
            if kind == "cols":
                return ref.at[:, pl.ds(pl.multiple_of(dev * shp[1], LANES), shp[1])]
            if kind == "rows":
                return ref.at[pl.ds(pl.multiple_of(dev * shp[0], SUBLANES), shp[0]), :]
            return ref

        cps = [pltpu.make_async_copy(part(a, me), out_refs[a].at[me], local_sems.at[a])
               for a in range(self.n)]
        for rel in range(1, N_DEV):
            peer = (x ^ (rel >> 2), y ^ ((rel >> 1) & 1), c ^ (rel & 1))
            for a in range(self.n):
                cps.append(pltpu.make_async_remote_copy(
                    src_ref=part(a, _slot(peer)), dst_ref=out_refs[a].at[me],
                    send_sem=send_sems.at[a, rel - 1], recv_sem=recv_sems.at[a, rel - 1],
                    device_id=peer, device_id_type=MESH))
        return cps


def _exchange_grads(arrays, kinds, name):
    ex = _Exchange(arrays, kinds)
    n = ex.n

    def body(*refs):
        cps = ex.copies(refs[:n], refs[n:2 * n], refs[2 * n:])
        for cp in cps:
            cp.start()
        for cp in cps:
            cp.wait()

    return pl.pallas_call(
        body, out_shape=ex.out_shape, in_specs=[ANY] * n, out_specs=[ANY] * n,
        scratch_shapes=ex.scratch, name=name)(*arrays)


BIG = [("mla_w_in", (D_MODEL, MLA_IN), 1), ("mla_w_uq", (Q_LORA, N_TOK_HEADS * QK_DIM), 1),
       ("mla_w_ukv", (KV_LORA, N_TOK_HEADS * 2 * HEAD_DIM), 1), ("lru_w_in", (D_MODEL, LRU_IN), 1),
       ("w_mem_kv", (2, D_MODEL, 2 * MEM_WIDTH), 1), ("w_out", (2, MIX_WIDTH, D_MODEL), 1)]
SMALL = [("lru_conv_w", (CONV_W, TOK_WIDTH), 1), ("lru_conv_b", (TOK_WIDTH,), 0),
         ("lru_b_rgate", (TOK_WIDTH,), 0), ("lru_b_igate", (TOK_WIDTH,), 0),
         ("lru_lambda", (TOK_WIDTH,), 0)]
REPL = [("mla_q_norm", (Q_LORA,)), ("mla_kv_norm", (KV_LORA,)),
        ("lru_w_rgate", (N_TOK_HEADS, HEAD_DIM, HEAD_DIM)),
        ("lru_w_igate", (N_TOK_HEADS, HEAD_DIM, HEAD_DIM)),
        ("ln_g", (2, D_MODEL)), ("ln_b", (2, D_MODEL))]


def _shard_shape(shape, axis):
    return tuple(d // N_DEV if a == axis else d for a, d in enumerate(shape))


def _size(shape):
    return math.prod(shape)


BIG_ROWS = sum(_size(s) for _, s, _ in BIG) // N_DEV // LANES
SMALL_ROWS = SUBLANES


def _pack_rows(flat_parts, rows):
    flat = jnp.concatenate([p.reshape(-1) for p in flat_parts])
    return jnp.pad(flat, (0, rows * LANES - flat.shape[0])).reshape(rows, LANES)


def _to_chunks(full, axis):
    shape = full.shape
    split = shape[:axis] + (N_DEV, shape[axis] // N_DEV) + shape[axis + 1:]
    return jnp.moveaxis(full.reshape(split), axis, 0).reshape(N_DEV, -1)


def _from_chunks(chunks, shape, axis):
    sh = _shard_shape(shape, axis)
    t = chunks.reshape((N_DEV,) + sh)
    t = jnp.moveaxis(t, 0, axis)
    return t.reshape(shape)


def _split_flat(flat2d, table):
    out, off = [], 0
    for size in table:
        out.append(flat2d[:, off:off + size])
        off += size
    return out


def _win0_to_padded(w):
    z = lambda n: jnp.zeros((w.shape[0], n), w.dtype)
    return jnp.concatenate([w[:, 0:640], z(KR_LANE), w[:, 640:672],
                            z(LANES - KR_LANE - QK_ROPE), w[:, 672:1952]], axis=1)


def _win0_from_padded(wp):
    k0 = ZA_KR + KR_LANE
    return jnp.concatenate([wp[:, 0:640], wp[:, k0:k0 + QK_ROPE], wp[:, ZA_W:ZP]], axis=1)


def _pad_heads(w, per_head, lo, hi):
    t = w.reshape(w.shape[0], N_TOK_HEADS, per_head)[:, :, lo:hi]
    t = jnp.pad(t, ((0, 0), (0, 0), (0, HEAD_PAD - (hi - lo))))
    return t.reshape(w.shape[0], QKV_PAD)


def _unpad_heads(wp, width):
    return wp.reshape(wp.shape[0], N_TOK_HEADS, HEAD_PAD)[:, :, :width]


def _block_diag(w):
    eye = jnp.eye(N_TOK_HEADS, dtype=w.dtype)
    return (w[:, :, None, :] * eye[:, None, :, None]).reshape(TOK_WIDTH, TOK_WIDTH)


def _diag_blocks(d):
    t = d.reshape(N_TOK_HEADS, HEAD_DIM, N_TOK_HEADS, HEAD_DIM)
    return jnp.stack([t[g, :, g, :] for g in range(N_TOK_HEADS)])


def _rope_tables(positions):
    half = QK_ROPE // 2
    inv_freq = ROPE_THETA ** (-jnp.arange(half, dtype=F32) / half)
    ang = positions.astype(F32)[:, None] * inv_freq
    cos, sin = jnp.cos(ang), jnp.sin(ang)
    s = positions.shape[0]
    one, zero = jnp.ones((s, QK_NOPE), F32), jnp.zeros((s, half), F32)
    tail = jnp.zeros((s, HEAD_PAD - QK_DIM), F32)
    znope = jnp.zeros((s, QK_NOPE), F32)
    c = jnp.concatenate([one, cos, cos, tail], axis=1)
    sa = jnp.concatenate([znope, -sin, zero, tail], axis=1)
    sb = jnp.concatenate([znope, zero, sin, tail], axis=1)
    return c, sa, sb


def _local_step(x, mem, positions, tgt, wts, ts, tatt, early_exchange):
    bf = lambda t: t.astype(BF16)
    win0 = _win0_to_padded(wts["mla_w_in"])
    wuq = _pad_heads(wts["mla_w_uq"], QK_DIM, 0, QK_DIM)
    wukv = jnp.concatenate([_pad_heads(wts["mla_w_ukv"], 2 * HEAD_DIM, 0, QK_NOPE),
                            _pad_heads(wts["mla_w_ukv"], 2 * HEAD_DIM, QK_NOPE, 2 * HEAD_DIM)],
                           axis=1)
    win1 = wts["lru_w_in"]
    wmkv, wout = wts["w_mem_kv"], wts["w_out"]
    gq = wts["mla_q_norm"].reshape(1, Q_LORA)
    gkv = wts["mla_kv_norm"].reshape(1, KV_LORA)
    ln_g, ln_b = wts["ln_g"], wts["ln_b"]
    wr, wi = bf(_block_diag(wts["lru_w_rgate"])), bf(_block_diag(wts["lru_w_igate"]))
    cw8 = jnp.pad(wts["lru_conv_w"], ((0, SUBLANES - CONV_W), (0, 0)))
    vec8 = jnp.pad(jnp.stack([wts["lru_conv_b"], wts["lru_b_rgate"], wts["lru_b_igate"],
                              wts["lru_lambda"]]), ((0, SUBLANES - 4), (0, 0)))
    tabs = _rope_tables(positions)
    tmem = mem.shape[0]

    za0, zg0 = _rowmm(x, win0, [ZA_W, ZG_W], "in_proj0", ts)
    q, k, v = _mla_prep_fwd(za0, tabs, gq, gkv, wuq, wukv, ts)
    o, lse = _flash_fwd(q, k, v, tatt, FWD_HEADS)
    mkv0, = _rowmm(mem, wmkv[0], [2 * MEM_WIDTH], "mem_kv0", tmem)
    cat0, y0 = _gate_mem_fwd(o, zg0, mkv0, 0, MIX_WIDTH, True, "gate_mem_fwd0", ts)
    del o
    pre0, h1 = _outproj_ln_fwd(y0, wout[0], x, ln_g[0:1], ln_b[0:1], None, "outproj_ln_fwd0", ts)
    u1, zg1 = _rowmm(h1, win1, [ZA_W, ZG_W], "in_proj1", ts)
    hs = _lru_fwd(u1, cw8, vec8, wr, wi, ts)
    mkv1, = _rowmm(mem, wmkv[1], [2 * MEM_WIDTH], "mem_kv1", tmem)
    cat1, y1 = _gate_mem_fwd(hs, zg1, mkv1, 0, MIX_WIDTH, False, "gate_mem_fwd1", ts)
    pre1, dh2, loss8 = _outproj_ln_fwd(y1, wout[1], h1, ln_g[1:2], ln_b[1:2], tgt,
                                       "outproj_ln_loss", ts)
    loss = loss8[0, 0]

    dpre1, dy1, dwout1, dgb1 = _outproj_ln_bwd(dh2, pre1, ln_g[1:2], y1, wout[1].T,
                                               "outproj_ln_bwd1", ts)
    dzg1, dhs, dmkv1 = _gate_mem_bwd(dy1, cat1, zg1, mkv1, None, 0, MIX_WIDTH,
                                     "gate_mem_bwd1", ts)
    du, dwr, dwi, dvec = _lru_bwd(u1, dhs, hs, cw8, vec8, wr, wi, wr.T, wi.T, ts)
    dh1, dwin1 = _linear_bwd(h1, [du, dzg1], [0, ZA_W], win1.T, dpre1, "in_proj_bwd1", ts)
    dwmkv1 = _wgrad_small(mem, dmkv1, "mem_kv_bwd1")
    dpre0, dy0, dwout0, dgb0 = _outproj_ln_bwd(dh1, pre0, ln_g[0:1], y0, wout[0].T,
                                               "outproj_ln_bwd0", ts)
    dzg0, do, dmkv0, stats = _gate_mem_bwd(dy0, cat0, zg0, mkv0, lse, 0, MIX_WIDTH,
                                           "gate_mem_bwd0", ts)
    dwmkv0 = _wgrad_small(mem, dmkv0, "mem_kv_bwd0")
    early = {
        "lru_w_in": dwin1,
        "lru_small": dvec,
        "lru_w_rgate": _diag_blocks(dwr).reshape(TOK_WIDTH, HEAD_DIM),
        "lru_w_igate": _diag_blocks(dwi).reshape(TOK_WIDTH, HEAD_DIM),
        "w_mem_kv": [dwmkv0, dwmkv1],
        "w_out": [dwout0, dwout1],
    }
    (dq, dk, dv), got_early = _flash_bwd(q, k, v, stats, do, tatt, BWD_HEADS,
                                         early_exchange(early))
    dza, dzk, dwuq_p, dwukv_p, dg = _mla_prep_bwd(za0, dq, dk, dv, tabs, gq, gkv,
                                                  wuq.T, wukv.T, ts)
    gx, dwin0_p = _linear_bwd(x, [dza, dzk, dzg0], [ZA_CQ, ZA_KR, ZA_W], win0.T, dpre0,
                              "in_proj_bwd0", ts)

    dwukv = jnp.concatenate([_unpad_heads(dwukv_p[:, :QKV_PAD], HEAD_DIM),
                             _unpad_heads(dwukv_p[:, QKV_PAD:], HEAD_DIM)], axis=2)
    zrow = jnp.zeros((1, D_MODEL), F32)
    gains = jnp.pad(dg[0:1], ((0, 0), (0, D_MODEL - Q_LORA - KV_LORA)))
    small_repl = jnp.concatenate([dgb0[0:2], dgb1[0:2], gains,
                                  loss * jnp.ones((1, D_MODEL), F32), zrow, zrow], axis=0)
    late = {
        "mla_w_in": _win0_from_padded(dwin0_p),
        "mla_w_uq": _unpad_heads(dwuq_p, QK_DIM).reshape(Q_LORA, N_TOK_HEADS * QK_DIM),
        "mla_w_ukv": dwukv.reshape(KV_LORA, N_TOK_HEADS * 2 * HEAD_DIM),
        "small_repl": small_repl,
    }
    return gx, early, got_early, late


WEIGHT_ORDER = ["mla_w_in", "mla_q_norm", "mla_w_uq", "mla_kv_norm", "mla_w_ukv", "lru_w_in",
                "lru_conv_w", "lru_conv_b", "lru_w_rgate", "lru_b_rgate", "lru_w_igate",
                "lru_b_igate", "lru_lambda", "w_mem_kv", "w_out", "ln_g", "ln_b"]


def kernel(x, mem, positions, mla_w_in, mla_q_norm, mla_w_uq, mla_kv_norm, mla_w_ukv, lru_w_in, lru_conv_w, lru_conv_b, lru_w_rgate, lru_b_rgate, lru_w_igate, lru_b_igate, lru_lambda, w_mem_kv, w_out, ln_g, ln_b, loss_target, m_mla_w_in, m_mla_q_norm, m_mla_w_uq, m_mla_kv_norm, m_mla_w_ukv, m_lru_w_in, m_lru_conv_w, m_lru_conv_b, m_lru_w_rgate, m_lru_b_rgate, m_lru_w_igate, m_lru_b_igate, m_lru_lambda, m_w_mem_kv, m_w_out, m_ln_g, m_ln_b, v_mla_w_in, v_mla_q_norm, v_mla_w_uq, v_mla_kv_norm, v_mla_w_ukv, v_lru_w_in, v_lru_conv_w, v_lru_conv_b, v_lru_w_rgate, v_lru_b_rgate, v_lru_w_igate, v_lru_b_igate, v_lru_lambda, v_w_mem_kv, v_w_out, v_ln_g, v_ln_b):
    w_in = dict(mla_w_in=mla_w_in, mla_q_norm=mla_q_norm, mla_w_uq=mla_w_uq,
                mla_kv_norm=mla_kv_norm, mla_w_ukv=mla_w_ukv, lru_w_in=lru_w_in,
                lru_conv_w=lru_conv_w, lru_conv_b=lru_conv_b, lru_w_rgate=lru_w_rgate,
                lru_b_rgate=lru_b_rgate, lru_w_igate=lru_w_igate, lru_b_igate=lru_b_igate,
                lru_lambda=lru_lambda, w_mem_kv=w_mem_kv, w_out=w_out, ln_g=ln_g, ln_b=ln_b)
    m_in = dict(mla_w_in=m_mla_w_in, mla_q_norm=m_mla_q_norm, mla_w_uq=m_mla_w_uq,
                mla_kv_norm=m_mla_kv_norm, mla_w_ukv=m_mla_w_ukv, lru_w_in=m_lru_w_in,
                lru_conv_w=m_lru_conv_w, lru_conv_b=m_lru_conv_b, lru_w_rgate=m_lru_w_rgate,
                lru_b_rgate=m_lru_b_rgate, lru_w_igate=m_lru_w_igate, lru_b_igate=m_lru_b_igate,
                lru_lambda=m_lru_lambda, w_mem_kv=m_w_mem_kv, w_out=m_w_out, ln_g=m_ln_g,
                ln_b=m_ln_b)
    v_in = dict(mla_w_in=v_mla_w_in, mla_q_norm=v_mla_q_norm, mla_w_uq=v_mla_w_uq,
                mla_kv_norm=v_mla_kv_norm, mla_w_ukv=v_mla_w_ukv, lru_w_in=v_lru_w_in,
                lru_conv_w=v_lru_conv_w, lru_conv_b=v_lru_conv_b, lru_w_rgate=v_lru_w_rgate,
                lru_b_rgate=v_lru_b_rgate, lru_w_igate=v_lru_w_igate, lru_b_igate=v_lru_b_igate,
                lru_lambda=v_lru_lambda, w_mem_kv=v_w_mem_kv, w_out=v_w_out, ln_g=v_ln_g,
                ln_b=v_ln_b)
    s = x.shape[1]
    ts = min(ROW_BLOCK, s)
    tatt = min(ATT_BLOCK, s)
    big_sizes = [_size(sh) // N_DEV for _, sh, _ in BIG]
    small_sizes = [_size(sh) // N_DEV for _, sh, _ in SMALL]

    big_local = _pack_rows([w_in[n] for n, _, _ in BIG], BIG_ROWS).astype(BF16)
    small_local = _pack_rows([w_in[n] for n, _, _ in SMALL], SMALL_ROWS)
    big_all, small_all = _allgather([big_local, small_local])
    wts = {}
    for (n, sh, ax), part in zip(BIG, _split_flat(big_all.reshape(N_DEV, -1), big_sizes)):
        wts[n] = _from_chunks(part, sh, ax)
    for (n, sh, ax), part in zip(SMALL, _split_flat(small_all.reshape(N_DEV, -1), small_sizes)):
        wts[n] = _from_chunks(part, sh, ax)
    for n, sh in REPL:
        wts[n] = w_in[n].reshape(sh)

    def early_exchange(g):
        small_chunks = jnp.moveaxis(g["lru_small"].reshape(SUBLANES, N_DEV, -1), 1, 0)
        sends = [(g["lru_w_in"], "cols"),
                 (g["w_mem_kv"][0], "rows"), (g["w_mem_kv"][1], "rows"),
                 (g["w_out"][0], "rows"), (g["w_out"][1], "rows"),
                 (small_chunks, "chunks"), (g["lru_w_rgate"], "all"), (g["lru_w_igate"], "all")]
        return _Exchange([a for a, _ in sends], [k for _, k in sends])

    gx, _, got_early, late = _local_step(x[0], mem[0], positions[0], loss_target[0], wts,
                                         ts, tatt, early_exchange)

    def chunked(name, shape):
        w = shape[1] // N_DEV
        return _to_chunks(late[name], 1).reshape(N_DEV, shape[0], w).astype(BF16)

    got_late = _exchange_grads(
        [chunked("mla_w_in", (D_MODEL, MLA_IN)),
         chunked("mla_w_uq", (Q_LORA, N_TOK_HEADS * QK_DIM)),
         chunked("mla_w_ukv", (KV_LORA, N_TOK_HEADS * 2 * HEAD_DIM)), late["small_repl"]],
        ["chunks", "chunks", "chunks", "all"], "exchange_grads")
    got = list(got_late[:3]) + list(got_early) + [got_late[3]]

    def small_sharded(d):
        return jnp.concatenate([d["lru_conv_w"].reshape(CONV_W, -1), d["lru_conv_b"],
                                d["lru_b_rgate"], d["lru_b_igate"], d["lru_lambda"]], axis=0)

    def small_replicated(d):
        gains = jnp.concatenate([d["mla_q_norm"], d["mla_kv_norm"]], axis=1)
        gains = jnp.pad(gains, ((0, 0), (0, D_MODEL - gains.shape[1])))
        return jnp.concatenate([d["ln_g"][0:1], d["ln_b"][0:1], d["ln_g"][1:2], d["ln_b"][1:2],
                                gains, jnp.zeros((3, D_MODEL), F32)], axis=0)

    def flat2(d, name):
        t = d[name]
        return t.reshape(-1, t.shape[-1])

    def update(parts, view, name):
        return _adamw(parts, view(w_in), view(m_in), view(v_in), "adamw_" + name)

    res = {}
    for idx, name in [(0, "mla_w_in"), (1, "mla_w_uq"), (2, "mla_w_ukv"), (3, "lru_w_in"),
                      (9, "lru_w_rgate"), (10, "lru_w_igate")]:
        res[name] = update([got[idx]], functools.partial(flat2, name=name), name)
    res["w_mem_kv"] = update([got[4], got[5]], functools.partial(flat2, name="w_mem_kv"),
                             "w_mem_kv")
    res["w_out"] = update([got[6], got[7]], functools.partial(flat2, name="w_out"), "w_out")
    res_ss = update([got[8]], small_sharded, "small_sharded")
    res_sr = update([got[11]], small_replicated, "small_replicated")
    loss = res_sr[0][5, 0]

    result = [loss, gx.reshape(x.shape)]
    for kind in range(4):
        ss, sr = res_ss[kind], res_sr[kind]
        out = {n: res[n][kind].reshape(w_in[n].shape) for n in res}
        out["lru_conv_w"] = ss[0:CONV_W].reshape(w_in["lru_conv_w"].shape)
        out["lru_conv_b"], out["lru_b_rgate"] = ss[4:5], ss[5:6]
        out["lru_b_igate"], out["lru_lambda"] = ss[6:7], ss[7:8]
        out["ln_g"] = jnp.concatenate([sr[0:1], sr[2:3]], axis=0)
        out["ln_b"] = jnp.concatenate([sr[1:2], sr[3:4]], axis=0)
        out["mla_q_norm"] = sr[4:5, 0:Q_LORA]
        out["mla_kv_norm"] = sr[4:5, Q_LORA:Q_LORA + KV_LORA]
        result += [out[n] for n in WEIGHT_ORDER]
    return tuple(result)
```

```python
import functools
import math

import jax
import jax.numpy as jnp
from jax import lax
from jax.experimental import pallas as pl
from jax.experimental.pallas import tpu as pltpu

F32 = jnp.float32
BF16 = jnp.bfloat16

D_MODEL = 1024
MEM_LEN = 256
HEAD_DIM = 64
N_TOK_HEADS = 12
N_MEM_HEADS = 4
TOK_WIDTH = 768
MEM_WIDTH = 256
MIX_WIDTH = 1024
Q_LORA = 384
KV_LORA = 256
QK_NOPE = 64
QK_ROPE = 32
QK_DIM = 96
ROPE_THETA = 10000.0
CONV_W = 4
LRU_C = 8.0
ALPHA = (2.0 * 2) ** 0.25
NORM_EPS = 1e-6
MLA_IN = 1952
LRU_IN = 2048
ADAM_LR = 0.001
ADAM_B1 = 0.9
ADAM_B2 = 0.999
ADAM_EPS = 1e-08
ADAM_WD = 0.01
ADAM_STEP = 10

N_DEV = 8
LANES = 128
SUBLANES = 8
HEAD_PAD = 128
QKV_PAD = N_TOK_HEADS * HEAD_PAD
ZP = 2048
ZA_W = TOK_WIDTH
ZG_W = MIX_WIDTH + MEM_WIDTH
ZA_CQ, ZA_CKV, ZA_KR = 0, 384, 640
KR_LANE = 64

ROW_BLOCK = 512
ATT_BLOCK = 512
LOOKAHEAD = 3
FWD_HEADS = 12
BWD_HEADS = 4
VMEM_LIMIT = 56 * 1024 * 1024
NEG_BIG = -1e30
STRIP = 32
SCAN_SEGMENTS = 4
LOG2E = math.log2(math.e)


def _cp(n_axes):
    return pltpu.CompilerParams(dimension_semantics=("arbitrary",) * n_axes,
                                vmem_limit_bytes=VMEM_LIMIT)


def _dot(a, b):
    return jnp.dot(a, b, preferred_element_type=F32)


def _dot_nt(a, b):
    return lax.dot_general(a, b, (((1,), (1,)), ((), ())), preferred_element_type=F32)


def _dot_tn(a, b):
    return lax.dot_general(a, b, (((0,), (0,)), ((), ())), preferred_element_type=F32)


def _sigmoid(t):
    return 1.0 / (1.0 + jnp.exp(-t))


def _lane(shape):
    return lax.broadcasted_iota(jnp.int32, shape, len(shape) - 1)


def _full(shape):
    nd = len(shape)
    return pl.BlockSpec(shape, lambda *_: (0,) * nd)


def _rows(ts, width, col=0):
    return pl.BlockSpec((ts, width), lambda i: (i, col))


def _heads(ts):
    return pl.BlockSpec((N_TOK_HEADS, ts, HEAD_PAD), lambda i: (0, i, 0))


def _rowmm(x, w, widths, name, ts):
    s, k = x.shape
    n = w.shape[1]
    offs = [sum(widths[:a]) for a in range(len(widths))]

    def body(x_ref, w_ref, *o_refs):
        res = _dot(x_ref[...].astype(BF16), w_ref[...])
        for o_ref, off, wd in zip(o_refs, offs, widths):
            o_ref[...] = res[:, off:off + wd]

    return pl.pallas_call(
        body, grid=(s // ts,),
        in_specs=[_rows(ts, k), _full((k, n))],
        out_specs=[_rows(ts, wd) for wd in widths],
        out_shape=[jax.ShapeDtypeStruct((s, wd), F32) for wd in widths],
        name=name, compiler_params=_cp(1))(x, w)


def _rms_parts(t):
    rs = lax.rsqrt(jnp.mean(t * t, axis=-1, keepdims=True) + NORM_EPS)
    return rs


def _rope(t, c, sa, sb):
    return t * c + pltpu.roll(t, LANES - 16, 1) * sa + pltpu.roll(t, 16, 1) * sb


def _rope_t(d, c, sa, sb):
    return d * c + pltpu.roll(d * sa, 16, 1) + pltpu.roll(d * sb, LANES - 16, 1)


def _mla_prep_fwd(z0, tabs, gq, gkv, wuq, wukv, ts):
    s = z0.shape[0]

    def body(z_ref, c_ref, sa_ref, sb_ref, gq_ref, gkv_ref, wuq_ref, wukv_ref,
             q_ref, k_ref, v_ref):
        cq = z_ref[:, ZA_CQ:ZA_CQ + Q_LORA]
        ckv = z_ref[:, ZA_CKV:ZA_CKV + KV_LORA]
        kr = z_ref[:, ZA_KR:ZA_KR + LANES]
        cqn = cq * _rms_parts(cq) * gq_ref[...]
        ckvn = ckv * _rms_parts(ckv) * gkv_ref[...]
        q = _dot(cqn.astype(BF16), wuq_ref[...])
        kv = _dot(ckvn.astype(BF16), wukv_ref[...])
        c, sa, sb = c_ref[...], sa_ref[...], sb_ref[...]
        krope = _rope(kr, c, sa, sb)
        pad_lane = _lane((ts, HEAD_PAD)) >= HEAD_DIM
        for h in range(N_TOK_HEADS):
            sl = slice(h * HEAD_PAD, (h + 1) * HEAD_PAD)
            q_ref[h] = _rope(q[:, sl], c, sa, sb).astype(BF16)
            k_ref[h] = (kv[:, sl] + krope).astype(BF16)
            vh = kv[:, QKV_PAD + h * HEAD_PAD:QKV_PAD + (h + 1) * HEAD_PAD]
            v_ref[h] = jnp.where(pad_lane, 1.0, vh).astype(BF16)

    out = jax.ShapeDtypeStruct((N_TOK_HEADS, s, HEAD_PAD), BF16)
    return pl.pallas_call(
        body, grid=(s // ts,),
        in_specs=[_rows(ts, ZA_W), _rows(ts, LANES), _rows(ts, LANES), _rows(ts, LANES),
                  _full((1, Q_LORA)), _full((1, KV_LORA)),
                  _full((Q_LORA, QKV_PAD)), _full((KV_LORA, 2 * QKV_PAD))],
        out_specs=[_heads(ts)] * 3,
        out_shape=[out, out, out],
        name="mla_prep_fwd", compiler_params=_cp(1))(z0, *tabs, gq, gkv, wuq, wukv)


def _mla_prep_bwd(z0, dq, dk, dv, tabs, gq, gkv, wuq_t, wukv_t, ts):
    s = z0.shape[0]

    def body(z_ref, dq_ref, dk_ref, dv_ref, c_ref, sa_ref, sb_ref, gq_ref, gkv_ref,
             wuqt_ref, wukvt_ref, dza_ref, dzk_ref, dwuq_ref, dwukv_ref, dg_ref):
        @pl.when(pl.program_id(0) == 0)
        def _():
            dwuq_ref[...] = jnp.zeros_like(dwuq_ref)
            dwukv_ref[...] = jnp.zeros_like(dwukv_ref)
            dg_ref[...] = jnp.zeros_like(dg_ref)

        cq = z_ref[:, ZA_CQ:ZA_CQ + Q_LORA]
        ckv = z_ref[:, ZA_CKV:ZA_CKV + KV_LORA]
        rq, rkv = _rms_parts(cq), _rms_parts(ckv)
        gq_, gkv_ = gq_ref[...], gkv_ref[...]
        cqn = (cq * rq * gq_).astype(BF16)
        ckvn = (ckv * rkv * gkv_).astype(BF16)
        c, sa, sb = c_ref[...], sa_ref[...], sb_ref[...]
        dqp, dksum = [], None
        for h in range(N_TOK_HEADS):
            dqp.append(_rope_t(dq_ref[h], c, sa, sb))
            dksum = dk_ref[h] if dksum is None else dksum + dk_ref[h]
        dqp = jnp.concatenate(dqp, axis=1).astype(BF16)
        lane = _lane(dksum.shape)
        dzk_ref[...] = jnp.where((lane >= KR_LANE) & (lane < KR_LANE + QK_ROPE),
                                 _rope_t(dksum, c, sa, sb), 0.0).astype(BF16)
        dkv = jnp.concatenate([dk_ref[h].astype(BF16) for h in range(N_TOK_HEADS)]
                              + [dv_ref[h] for h in range(N_TOK_HEADS)], axis=1)
        dcqn = _dot(dqp, wuqt_ref[...])
        dckvn = _dot(dkv, wukvt_ref[...])
        dwuq_ref[...] += _dot_tn(cqn, dqp)
        dwukv_ref[...] += _dot_tn(ckvn, dkv)
        dg_ref[0:1, 0:Q_LORA] += jnp.sum(dcqn * cq * rq, axis=0, keepdims=True)
        dg_ref[0:1, Q_LORA:Q_LORA + KV_LORA] += jnp.sum(dckvn * ckv * rkv, axis=0, keepdims=True)
        wq = dcqn * gq_
        wkv = dckvn * gkv_
        dcq = rq * wq - cq * (rq * rq * rq) * jnp.mean(wq * cq, axis=-1, keepdims=True)
        dckv = rkv * wkv - ckv * (rkv * rkv * rkv) * jnp.mean(wkv * ckv, axis=-1, keepdims=True)
        dza_ref[:, 0:Q_LORA] = dcq.astype(BF16)
        dza_ref[:, Q_LORA:Q_LORA + KV_LORA] = dckv.astype(BF16)

    na = Q_LORA + KV_LORA
    return pl.pallas_call(
        body, grid=(s // ts,),
        in_specs=[_rows(ts, ZA_W), _heads(ts), _heads(ts), _heads(ts),
                  _rows(ts, LANES), _rows(ts, LANES), _rows(ts, LANES),
                  _full((1, Q_LORA)), _full((1, KV_LORA)),
                  _full((QKV_PAD, Q_LORA)), _full((2 * QKV_PAD, KV_LORA))],
        out_specs=[_rows(ts, na), _rows(ts, LANES), _full((Q_LORA, QKV_PAD)),
                   _full((KV_LORA, 2 * QKV_PAD)), _full((SUBLANES, na))],
        out_shape=[jax.ShapeDtypeStruct((s, na), BF16), jax.ShapeDtypeStruct((s, LANES), BF16),
                   jax.ShapeDtypeStruct((Q_LORA, QKV_PAD), F32),
                   jax.ShapeDtypeStruct((KV_LORA, 2 * QKV_PAD), F32),
                   jax.ShapeDtypeStruct((SUBLANES, na), F32)],
        name="mla_prep_bwd", compiler_params=_cp(1))(
            z0, dq, dk, dv, *tabs, gq, gkv, wuq_t, wukv_t)


def _causal_pairs(nb, by_key):
    if by_key:
        pairs = [(i, j) for j in range(nb) for i in range(j, nb)]
    else:
        pairs = [(i, j) for i in range(nb) for j in range(i + 1)]
    return (jnp.array([p[0] for p in pairs], jnp.int32),
            jnp.array([p[1] for p in pairs], jnp.int32))


def _flash_fwd(q, k, v, t, nh):
    s = q.shape[1]
    itab, jtab = _causal_pairs(s // t, False)
    c2 = LOG2E / math.sqrt(QK_DIM)

    def body(it_ref, jt_ref, q_ref, k_ref, v_ref, o_ref, lse_ref, m_scr, acc_scr):
        pair = pl.program_id(1)
        i, j = it_ref[pair], jt_ref[pair]

        @pl.when(j == 0)
        def _():
            m_scr[...] = jnp.full_like(m_scr, NEG_BIG)
            acc_scr[...] = jnp.zeros_like(acc_scr)

        def softmax_strips(masked, hs, sc, row0):
            ps, als = [], []
            for r0 in range(0, sc.shape[0], STRIP):
                rows = slice(row0 + r0, row0 + r0 + STRIP)
                ch = [sc[r0:r0 + STRIP, n * LANES:(n + 1) * LANES] * c2
                      for n in range(sc.shape[1] // LANES)]
                if masked:
                    rr = row0 + r0 + lax.broadcasted_iota(jnp.int32, (STRIP, LANES), 0)
                    cc = lax.broadcasted_iota(jnp.int32, (STRIP, LANES), 1)
                    ch = [jnp.where(cc + n * LANES <= rr, c_, NEG_BIG) for n, c_ in enumerate(ch)]
                mx = ch[0]
                for c_ in ch[1:]:
                    mx = jnp.maximum(mx, c_)
                m_prev = m_scr[hs, rows, :]
                m_next = jnp.maximum(m_prev, jnp.max(mx, axis=-1, keepdims=True))
                ps.append(jnp.concatenate(
                    [jnp.exp2(c_ - m_next).astype(BF16) for c_ in ch], axis=1))
                als.append(jnp.exp2(m_prev - m_next))
                m_scr[hs, rows, :] = m_next
            return jnp.concatenate(ps, axis=0), jnp.concatenate(als, axis=0)

        def run(masked, parts):
            def scores_of(hs):
                return [_dot_nt(q_ref[hs, r0:r0 + nr, :], k_ref[hs, 0:nk, :])
                        for r0, nr, nk in parts]

            ahead = min(LOOKAHEAD, nh)
            scores = [scores_of(hs) for hs in range(ahead)]
            for hs in range(nh):
                if hs + ahead < nh:
                    scores.append(scores_of(hs + ahead))
                for (r0, nr, nk), sc in zip(parts, scores[hs]):
                    p, alpha = softmax_strips(masked, hs, sc, r0)
                    acc_scr[hs, r0:r0 + nr, :] = (alpha * acc_scr[hs, r0:r0 + nr, :]
                                                  + _dot(p, v_ref[hs, 0:nk, :]))

        @pl.when(j < i)
        def _():
            run(False, [(0, t, t)])

        @pl.when(j == i)
        def _():
            run(True, [(0, t, t)])
            for h in range(nh):
                acc = acc_scr[h]
                l = acc[:, HEAD_DIM:HEAD_DIM + 1]
                o_ref[h] = jnp.where(_lane(acc.shape) < HEAD_DIM, acc / l, 0.0)
                lse_ref[h] = (m_scr[h] + jnp.log2(l)).T[0:1, :]

    qspec = pl.BlockSpec((nh, t, HEAD_PAD), lambda h, p, it, jt: (h, it[p], 0))
    kspec = pl.BlockSpec((nh, t, HEAD_PAD), lambda h, p, it, jt: (h, jt[p], 0))
    lspec = pl.BlockSpec((nh, 1, t), lambda h, p, it, jt: (h, 0, it[p]))
    out = jax.ShapeDtypeStruct((N_TOK_HEADS, s, HEAD_PAD), F32)
    return pl.pallas_call(
        body,
        grid_spec=pltpu.PrefetchScalarGridSpec(
            num_scalar_prefetch=2, grid=(N_TOK_HEADS // nh, itab.shape[0]),
            in_specs=[qspec, kspec, kspec], out_specs=[qspec, lspec],
            scratch_shapes=[pltpu.VMEM((nh, t, HEAD_PAD), F32)] * 2),
        out_shape=[out, jax.ShapeDtypeStruct((N_TOK_HEADS, 1, s), F32)],
        name="flash_fwd", compiler_params=_cp(2))(itab, jtab, q, k, v)


def _flash_bwd(q, k, v, stats, do, t, nh, ex):
    s = q.shape[1]
    nb = s // t
    itab, jtab = _causal_pairs(nb, True)
    npairs = itab.shape[0]
    ngroups = N_TOK_HEADS // nh
    scale = 1.0 / math.sqrt(QK_DIM)
    c2 = LOG2E * scale
    nx = ex.n if ex is not None else 0
    ex_arrays, ex_out_shape, ex_scratch = (
        (ex.arrays, ex.out_shape, ex.scratch) if ex is not None else ([], [], []))

    def body(it_ref, jt_ref, q_ref, k_ref, v_ref, st_ref, do_ref, *rest):
        ex_in, rest = rest[:nx], rest[nx:]
        dq_ref, dk_ref, dv_ref = rest[:3]
        ex_out, rest = rest[3:3 + nx], rest[3 + nx:]
        dk_scr, dv_scr = rest[:2]
        ex_sems = rest[2:]
        pair = pl.program_id(1)
        i, j = it_ref[pair], jt_ref[pair]
        rows_i = pl.ds(pl.multiple_of(i * t, t), t)

        if nx:
            @pl.when(jnp.logical_and(pl.program_id(0) == 0, pair == 0))
            def _():
                for cp in ex.copies(ex_in, ex_out, ex_sems):
                    cp.start()

        @pl.when(i == j)
        def _():
            dk_scr[...] = jnp.zeros_like(dk_scr)
            dv_scr[...] = jnp.zeros_like(dv_scr)

        @pl.when(j == 0)
        def _():
            dq_ref[:, rows_i, :] = jnp.zeros((nh, t, HEAD_PAD), F32)

        def prob_strips(masked, h, sct, dpt, k0, q0):
            ps, dss = [], []
            for r0 in range(0, sct.shape[0], STRIP):
                rows = slice(r0, r0 + STRIP)
                if masked:
                    kk = k0 + r0 + lax.broadcasted_iota(jnp.int32, (STRIP, LANES), 0)
                    qq = q0 + lax.broadcasted_iota(jnp.int32, (STRIP, LANES), 1)
                pcs, dcs = [], []
                for n in range(sct.shape[1] // LANES):
                    cols = slice(n * LANES, (n + 1) * LANES)
                    qcols = slice(q0 + n * LANES, q0 + (n + 1) * LANES)
                    x = sct[rows, cols] * c2
                    if masked:
                        x = jnp.where(kk <= qq + n * LANES, x, NEG_BIG)
                    p = jnp.exp2(x - st_ref[h, 0:1, qcols])
                    pcs.append(p.astype(BF16))
                    dcs.append((p * (dpt[rows, cols] - st_ref[h, 1:2, qcols]) * scale).astype(BF16))
                ps.append(jnp.concatenate(pcs, axis=1))
                dss.append(jnp.concatenate(dcs, axis=1))
            return jnp.concatenate(ps, axis=0), jnp.concatenate(dss, axis=0)

        def run(masked, parts):
            def scores_of(h):
                return [(_dot_nt(k_ref[h, k0:k0 + nk, :], q_ref[h, q0:q0 + nq, :]),
                         _dot_nt(v_ref[h, k0:k0 + nk, :], do_ref[h, q0:q0 + nq, :]))
                        for k0, nk, q0, nq in parts]

            ahead = min(LOOKAHEAD, nh)
            scores = [scores_of(h) for h in range(ahead)]
            for h in range(nh):
                if h + ahead < nh:
                    scores.append(scores_of(h + ahead))
                for (k0, nk, q0, nq), (sct, dpt) in zip(parts, scores[h]):
                    pt, dst = prob_strips(masked, h, sct, dpt, k0, q0)
                    dv_scr[h, k0:k0 + nk, :] += _dot(pt, do_ref[h, q0:q0 + nq, :])
                    dk_scr[h, k0:k0 + nk, :] += _dot(dst, q_ref[h, q0:q0 + nq, :])
                    rows = pl.ds(pl.multiple_of(i * t + q0, t // 2), nq)
                    dq_ref[h, rows, :] += _dot_tn(dst, k_ref[h, k0:k0 + nk, :])

        @pl.when(i > j)
        def _():
            run(False, [(0, t, 0, t)])

        @pl.when(i == j)
        def _():
            run(True, [(0, t // 2, 0, t), (t // 2, t // 2, t // 2, t // 2)])

        @pl.when(i == nb - 1)
        def _():
            dk_ref[...] = dk_scr[...]
            dv_ref[...] = dv_scr[...].astype(BF16)

        if nx:
            @pl.when(jnp.logical_and(pl.program_id(0) == ngroups - 1, pair == npairs - 1))
            def _():
                for cp in ex.copies(ex_in, ex_out, ex_sems):
                    cp.wait()

    qspec = pl.BlockSpec((nh, t, HEAD_PAD), lambda h, p, it, jt: (h, it[p], 0))
    kspec = pl.BlockSpec((nh, t, HEAD_PAD), lambda h, p, it, jt: (h, jt[p], 0))
    dqspec = pl.BlockSpec((nh, s, HEAD_PAD), lambda h, p, it, jt: (h, 0, 0))
    stspec = pl.BlockSpec((nh, 2, t), lambda h, p, it, jt: (h, 0, it[p]))
    out = jax.ShapeDtypeStruct((N_TOK_HEADS, s, HEAD_PAD), F32)
    res = pl.pallas_call(
        body,
        grid_spec=pltpu.PrefetchScalarGridSpec(
            num_scalar_prefetch=2, grid=(ngroups, npairs),
            in_specs=[qspec, kspec, kspec, stspec, qspec] + [ANY] * nx,
            out_specs=[dqspec, kspec, kspec] + [ANY] * nx,
            scratch_shapes=[pltpu.VMEM((nh, t, HEAD_PAD), F32)] * 2 + ex_scratch),
        out_shape=[out, out, jax.ShapeDtypeStruct(out.shape, BF16)] + ex_out_shape,
        name="flash_bwd", compiler_params=_cp(2))(itab, jtab, q, k, v, stats, do, *ex_arrays)
    return res[:3], res[3:]


def _mem_probs(qp, kp, hh):
    lane = _lane(qp.shape)
    keep = (lane < HEAD_DIM) if hh == 0 else (lane >= HEAD_DIM)
    qh = jnp.where(keep, qp, 0.0).astype(BF16)
    sc = _dot_nt(qh, kp) * (1.0 / math.sqrt(HEAD_DIM))
    e = jnp.exp(sc - jnp.max(sc, axis=-1, keepdims=True))
    return e / jnp.sum(e, axis=-1, keepdims=True), keep


def _gate_mem_fwd(tok, z, memkv, g0, q0, padded, name, ts):
    s = z.shape[0]
    zw = z.shape[1]
    tok_spec = _heads(ts) if padded else _rows(ts, TOK_WIDTH)

    def body(tok_ref, z_ref, mkv_ref, cat_ref, y_ref):
        if padded:
            for p in range(N_TOK_HEADS // 2):
                cat_ref[:, p * LANES:(p + 1) * LANES] = (
                    tok_ref[2 * p] + pltpu.roll(tok_ref[2 * p + 1], HEAD_DIM, 1))
        else:
            cat_ref[:, 0:TOK_WIDTH] = tok_ref[...]
        for pr in range(N_MEM_HEADS // 2):
            sl = slice(pr * LANES, (pr + 1) * LANES)
            qp = z_ref[:, q0 + pr * LANES:q0 + (pr + 1) * LANES]
            kp = mkv_ref[:, sl].astype(BF16)
            vp = mkv_ref[:, MEM_WIDTH + pr * LANES:MEM_WIDTH + (pr + 1) * LANES].astype(BF16)
            outs = []
            for hh in range(2):
                p, _ = _mem_probs(qp, kp, hh)
                outs.append(_dot(p.astype(BF16), vp))
            lane = _lane(outs[0].shape)
            cat_ref[:, TOK_WIDTH + pr * LANES:TOK_WIDTH + (pr + 1) * LANES] = jnp.where(
                lane < HEAD_DIM, outs[0], outs[1])
        gate = z_ref[:, g0:g0 + MIX_WIDTH]
        y_ref[...] = (cat_ref[...] * (gate * _sigmoid(gate))).astype(BF16)

    return pl.pallas_call(
        body, grid=(s // ts,),
        in_specs=[tok_spec, _rows(ts, zw), _full((MEM_LEN, 2 * MEM_WIDTH))],
        out_specs=[_rows(ts, MIX_WIDTH)] * 2,
        out_shape=[jax.ShapeDtypeStruct((s, MIX_WIDTH), F32),
                   jax.ShapeDtypeStruct((s, MIX_WIDTH), BF16)],
        name=name, compiler_params=_cp(1))(tok, z, memkv)


def _gate_mem_bwd(dy, cat, z, memkv, lse, g0, q0, name, ts):
    s = z.shape[0]
    zw = z.shape[1]
    padded = lse is not None
    gq_w = MIX_WIDTH + MEM_WIDTH

    def body(*refs):
        if padded:
            dy_ref, cat_ref, z_ref, mkv_ref, lse_ref, dzg_ref, dtok_ref, dmkv_ref, st_ref = refs
        else:
            dy_ref, cat_ref, z_ref, mkv_ref, dzg_ref, dtok_ref, dmkv_ref = refs

        @pl.when(pl.program_id(0) == 0)
        def _():
            dmkv_ref[...] = jnp.zeros_like(dmkv_ref)

        gate = z_ref[:, g0:g0 + MIX_WIDTH]
        sg = _sigmoid(gate)
        dy_ = dy_ref[...]
        dzg_ref[:, 0:MIX_WIDTH] = (dy_ * cat_ref[...]
                                   * (sg * (1.0 + gate * (1.0 - sg)))).astype(BF16)
        dcat = dy_ * (gate * sg)
        if padded:
            low = _lane((ts, LANES)) < HEAD_DIM
            for p in range(N_TOK_HEADS // 2):
                d = dcat[:, p * LANES:(p + 1) * LANES]
                prod = d * cat_ref[:, p * LANES:(p + 1) * LANES]
                first = jnp.sum(jnp.where(low, prod, 0.0), axis=-1, keepdims=True)
                second = jnp.sum(jnp.where(low, 0.0, prod), axis=-1, keepdims=True)
                dtok_ref[2 * p] = jnp.where(low, d, 0.0).astype(BF16)
                dtok_ref[2 * p + 1] = jnp.where(low, pltpu.roll(d, HEAD_DIM, 1), 0.0).astype(BF16)
                for hh, delta in ((2 * p, first), (2 * p + 1, second)):
                    st_ref[hh, 0:1, :] = lse_ref[hh]
                    st_ref[hh, 1:2, :] = jnp.broadcast_to(delta, (ts, LANES)).T[0:1, :]
        else:
            dtok_ref[...] = dcat[:, 0:TOK_WIDTH]
        for pr in range(N_MEM_HEADS // 2):
            sl = slice(pr * LANES, (pr + 1) * LANES)
            vsl = slice(MEM_WIDTH + pr * LANES, MEM_WIDTH + (pr + 1) * LANES)
            qp = z_ref[:, q0 + pr * LANES:q0 + (pr + 1) * LANES]
            qpb = qp.astype(BF16)
            kp = mkv_ref[:, sl].astype(BF16)
            vp = mkv_ref[:, vsl].astype(BF16)
            dmo = dcat[:, TOK_WIDTH + pr * LANES:TOK_WIDTH + (pr + 1) * LANES]
            dqp = None
            for hh in range(2):
                p, keep = _mem_probs(qp, kp, hh)
                do_h = jnp.where(keep, dmo, 0.0).astype(BF16)
                dmkv_ref[:, vsl] += _dot_tn(p.astype(BF16), do_h)
                dp = _dot_nt(do_h, vp)
                ds = (p * (dp - jnp.sum(dp * p, axis=-1, keepdims=True))
                      * (1.0 / math.sqrt(HEAD_DIM))).astype(BF16)
                dqh = jnp.where(keep, _dot(ds, kp), 0.0)
                dqp = dqh if dqp is None else dqp + dqh
                dkh = _dot_tn(ds, qpb)
                klane = _lane(dkh.shape)
                kkeep = (klane < HEAD_DIM) if hh == 0 else (klane >= HEAD_DIM)
                dmkv_ref[:, sl] += jnp.where(kkeep, dkh, 0.0)
            dzg_ref[:, MIX_WIDTH + pr * LANES:MIX_WIDTH + (pr + 1) * LANES] = dqp.astype(BF16)

    in_specs = [_rows(ts, MIX_WIDTH), _rows(ts, MIX_WIDTH), _rows(ts, zw),
                _full((MEM_LEN, 2 * MEM_WIDTH))]
    out_specs = [_rows(ts, gq_w), _heads(ts) if padded else _rows(ts, TOK_WIDTH),
                 _full((MEM_LEN, 2 * MEM_WIDTH))]
    heads_shape = (N_TOK_HEADS, s, HEAD_PAD)
    out_shape = [jax.ShapeDtypeStruct((s, gq_w), BF16),
                 jax.ShapeDtypeStruct(heads_shape, BF16) if padded
                 else jax.ShapeDtypeStruct((s, TOK_WIDTH), F32),
                 jax.ShapeDtypeStruct((MEM_LEN, 2 * MEM_WIDTH), F32)]
    args = [dy, cat, z, memkv]
    if padded:
        in_specs.append(pl.BlockSpec((N_TOK_HEADS, 1, ts), lambda i: (0, 0, i)))
        out_specs.append(pl.BlockSpec((N_TOK_HEADS, 2, ts), lambda i: (0, 0, i)))
        out_shape.append(jax.ShapeDtypeStruct((N_TOK_HEADS, 2, s), F32))
        args.append(lse)
    return pl.pallas_call(
        body, grid=(s // ts,), in_specs=in_specs, out_specs=out_specs, out_shape=out_shape,
        name=name, compiler_params=_cp(1))(*args)


def _ln_stats(pre):
    mu = jnp.mean(pre, axis=-1, keepdims=True)
    d = pre - mu
    rstd = lax.rsqrt(jnp.mean(d * d, axis=-1, keepdims=True) + NORM_EPS)
    return d * rstd, rstd


def _ln_bwd(dh, xhat, rstd, g):
    dxh = dh * g
    return rstd * (dxh - jnp.mean(dxh, axis=-1, keepdims=True)
                   - xhat * jnp.mean(dxh * xhat, axis=-1, keepdims=True))


def _outproj_ln_fwd(y, w, h, g, b, tgt, name, ts):
    s = y.shape[0]
    with_loss = tgt is not None

    def body(*refs):
        if with_loss:
            y_ref, w_ref, h_ref, g_ref, b_ref, t_ref, dpre_ref, dgb_ref, loss_ref = refs
        else:
            y_ref, w_ref, h_ref, g_ref, b_ref, pre_ref, out_ref = refs
        pre = ALPHA * h_ref[...] + _dot(y_ref[...].astype(BF16), w_ref[...])
        xhat, rstd = _ln_stats(pre)
        hout = xhat * g_ref[...] + b_ref[...]
        if with_loss:
            @pl.when(pl.program_id(0) == 0)
            def _():
                loss_ref[...] = jnp.zeros_like(loss_ref)
                dgb_ref[...] = jnp.zeros_like(dgb_ref)
            err = hout - t_ref[...]
            loss_ref[...] += 0.5 * jnp.sum(jnp.mean(err * err, axis=-1, keepdims=True))
            dh = err * (1.0 / D_MODEL)
            dpre_ref[...] = _ln_bwd(dh, xhat, rstd, g_ref[...])
            dgb_ref[0:1, :] += jnp.sum(dh * xhat, axis=0, keepdims=True)
            dgb_ref[1:2, :] += jnp.sum(dh, axis=0, keepdims=True)
        else:
            pre_ref[...] = pre
            out_ref[...] = hout

    act = jax.ShapeDtypeStruct((s, D_MODEL), F32)
    in_specs = [_rows(ts, MIX_WIDTH), _full((MIX_WIDTH, D_MODEL)), _rows(ts, D_MODEL),
                _full((1, D_MODEL)), _full((1, D_MODEL))]
    args = [y, w, h, g, b]
    if with_loss:
        in_specs.append(_rows(ts, D_MODEL))
        out_specs = [_rows(ts, D_MODEL), _full((SUBLANES, D_MODEL)), _full((SUBLANES, LANES))]
        out_shape = [act, jax.ShapeDtypeStruct((SUBLANES, D_MODEL), F32),
                     jax.ShapeDtypeStruct((SUBLANES, LANES), F32)]
        args.append(tgt)
    else:
        out_specs = [_rows(ts, D_MODEL)] * 2
        out_shape = [act, act]
    return pl.pallas_call(
        body, grid=(s // ts,), in_specs=in_specs, out_specs=out_specs, out_shape=out_shape,
        name=name, compiler_params=_cp(1))(*args)


def _outproj_ln_bwd(dh, pre, g, y, w_t, name, ts):
    s = y.shape[0]

    def body(dh_ref, pre_ref, g_ref, y_ref, wt_ref, dpre_ref, dy_ref, dw_ref, dgb_ref):
        @pl.when(pl.program_id(0) == 0)
        def _():
            dw_ref[...] = jnp.zeros_like(dw_ref)
            dgb_ref[...] = jnp.zeros_like(dgb_ref)

        dh_ = dh_ref[...]
        xhat, rstd = _ln_stats(pre_ref[...])
        dpre = _ln_bwd(dh_, xhat, rstd, g_ref[...])
        dpre_ref[...] = dpre
        dgb_ref[0:1, :] += jnp.sum(dh_ * xhat, axis=0, keepdims=True)
        dgb_ref[1:2, :] += jnp.sum(dh_, axis=0, keepdims=True)
        dpb = dpre.astype(BF16)
        dy_ref[...] = _dot(dpb, wt_ref[...])
        dw_ref[...] += _dot_tn(y_ref[...].astype(BF16), dpb)

    act = jax.ShapeDtypeStruct((s, D_MODEL), F32)
    return pl.pallas_call(
        body, grid=(s // ts,),
        in_specs=[_rows(ts, D_MODEL), _rows(ts, D_MODEL), _full((1, D_MODEL)),
                  _rows(ts, MIX_WIDTH), _full((D_MODEL, MIX_WIDTH))],
        out_specs=[_rows(ts, D_MODEL), _rows(ts, MIX_WIDTH), _full((MIX_WIDTH, D_MODEL)),
                   _full((SUBLANES, D_MODEL))],
        out_shape=[act, act, jax.ShapeDtypeStruct((MIX_WIDTH, D_MODEL), F32),
                   jax.ShapeDtypeStruct((SUBLANES, D_MODEL), F32)],
        name=name, compiler_params=_cp(1))(dh, pre, g, y, w_t)


def _outproj_bwd(dpre, y, w_t, name, ts):
    s = y.shape[0]

    def body(dpre_ref, y_ref, wt_ref, dy_ref, dw_ref):
        @pl.when(pl.program_id(0) == 0)
        def _():
            dw_ref[...] = jnp.zeros_like(dw_ref)

        dpb = dpre_ref[...].astype(BF16)
        dy_ref[...] = _dot(dpb, wt_ref[...])
        dw_ref[...] += _dot_tn(y_ref[...].astype(BF16), dpb)

    return pl.pallas_call(
        body, grid=(s // ts,),
        in_specs=[_rows(ts, D_MODEL), _rows(ts, MIX_WIDTH), _full((D_MODEL, MIX_WIDTH))],
        out_specs=[_rows(ts, MIX_WIDTH), _full((MIX_WIDTH, D_MODEL))],
        out_shape=[jax.ShapeDtypeStruct((s, MIX_WIDTH), F32),
                   jax.ShapeDtypeStruct((MIX_WIDTH, D_MODEL), F32)],
        name=name, compiler_params=_cp(1))(dpre, y, w_t)


def _linear_bwd(x, dys, offs, w_t, resid, name, ts):
    s, kdim = x.shape
    n = w_t.shape[0]
    widths = [d.shape[1] for d in dys]
    npieces = len(dys)

    def body(*refs):
        x_ref = refs[0]
        dy_refs = refs[1:1 + npieces]
        wt_ref, r_ref, dx_ref, dw_ref = refs[1 + npieces:]

        @pl.when(pl.program_id(0) == 0)
        def _():
            dw_ref[...] = jnp.zeros_like(dw_ref)

        xb = x_ref[...].astype(BF16)
        dx = ALPHA * r_ref[...]
        for dy_ref, off, wd in zip(dy_refs, offs, widths):
            dyb = dy_ref[...].astype(BF16)
            dx = dx + _dot(dyb, wt_ref[off:off + wd, :])
            dw_ref[:, off:off + wd] += _dot_tn(xb, dyb)
        dx_ref[...] = dx

    return pl.pallas_call(
        body, grid=(s // ts,),
        in_specs=[_rows(ts, kdim)] + [_rows(ts, wd) for wd in widths]
                 + [_full((n, kdim)), _rows(ts, kdim)],
        out_specs=[_rows(ts, kdim), _full((kdim, n))],
        out_shape=[jax.ShapeDtypeStruct((s, kdim), F32), jax.ShapeDtypeStruct((kdim, n), F32)],
        name=name, compiler_params=_cp(1))(x, *dys, w_t, resid)


def _wgrad_small(x, dy, name):
    def body(x_ref, dy_ref, dw_ref):
        dw_ref[...] = _dot_tn(x_ref[...].astype(BF16), dy_ref[...].astype(BF16))

    return pl.pallas_call(
        body, out_shape=jax.ShapeDtypeStruct((x.shape[1], dy.shape[1]), F32),
        name=name, compiler_params=pltpu.CompilerParams(vmem_limit_bytes=VMEM_LIMIT))(x, dy)


def _shift_down(u, carry8, k):
    if k == 0:
        return u
    rolled = pltpu.roll(u, k, 0)
    row = lax.broadcasted_iota(jnp.int32, carry8.shape, 0)
    top = jnp.where(row < k, pltpu.roll(carry8, k, 0), rolled[0:SUBLANES])
    return jnp.concatenate([top, rolled[SUBLANES:]], axis=0)


def _shift_up(u, carry8, k):
    if k == 0:
        return u
    n = u.shape[0]
    rolled = pltpu.roll(u, n - k, 0)
    row = lax.broadcasted_iota(jnp.int32, carry8.shape, 0)
    bot = jnp.where(row >= SUBLANES - k, pltpu.roll(carry8, SUBLANES - k, 0),
                    rolled[n - SUBLANES:])
    return jnp.concatenate([rolled[:n - SUBLANES], bot], axis=0)


def _neg_expm1(t):
    e = jnp.exp(t)
    em1 = e - 1.0
    safe = jnp.where(e == 1.0, 1.0, jnp.log(e))
    return -jnp.where(e == 1.0, t, jnp.where(em1 == -1.0, -1.0, em1 * t / safe))


def _lru_gates(u, carry8, cw_ref, vec_ref, wr_ref, wi_ref):
    taps = [_shift_down(u, carry8, k) for k in range(CONV_W)]
    xc = vec_ref[0:1, :] + cw_ref[3:4, :] * u
    for k in range(1, CONV_W):
        xc = xc + cw_ref[3 - k:4 - k, :] * taps[k]
    xb = xc.astype(BF16)
    r = _sigmoid(_dot(xb, wr_ref[...]) + vec_ref[1:2, :])
    ig = _sigmoid(_dot(xb, wi_ref[...]) + vec_ref[2:3, :])
    nlam = -vec_ref[3:4, :]
    softplus = jnp.maximum(nlam, 0.0) + jnp.log(1.0 + jnp.exp(-jnp.abs(nlam)))
    cneg = -LRU_C * softplus
    log_a = cneg * r
    a = jnp.exp(log_a)
    sq = jnp.sqrt(_neg_expm1(2.0 * log_a))
    return xc, r, ig, cneg, a, sq, taps


def _chained_scan(a_ref, b_ref, out_ref, cum_scr, x_in):
    rows_total, w = a_ref.shape
    nseg = SCAN_SEGMENTS
    seg = rows_total // nseg

    def step(t, carry):
        xs, ps = carry
        new_x, new_p = [], []
        for sg in range(nseg):
            row = pl.ds(sg * seg + t, 1)
            a = a_ref[row, :]
            x = a * xs[sg] + b_ref[row, :]
            out_ref[row, :] = x
            new_x.append(x)
            if sg > 0:
                p = a * ps[sg - 1]
                cum_scr[row, :] = p
                new_p.append(p)
        return tuple(new_x), tuple(new_p)

    zero, one = jnp.zeros((1, w), F32), jnp.ones((1, w), F32)
    xs, _ = lax.fori_loop(0, seg, step, ((x_in,) + (zero,) * (nseg - 1), (one,) * (nseg - 1)))
    x_prev = xs[0]
    for sg in range(1, nseg):
        rows = slice(sg * seg, (sg + 1) * seg)
        out_ref[rows, :] = out_ref[rows, :] + cum_scr[rows, :] * x_prev
        x_prev = out_ref[(sg + 1) * seg - 1:(sg + 1) * seg, :]
    return x_prev


def _lru_fwd(z1, cw8, vec8, wr, wi, ts):
    s = z1.shape[0]

    def body(u_ref, cw_ref, vec_ref, wr_ref, wi_ref, hs_ref,
             cu_scr, ch_scr, a_scr, gx_scr, cum_scr):
        @pl.when(pl.program_id(0) == 0)
        def _():
            cu_scr[...] = jnp.zeros_like(cu_scr)
            ch_scr[...] = jnp.zeros_like(ch_scr)

        u = u_ref[...]
        xc, _, ig, _, a, sq, _ = _lru_gates(u, cu_scr[...], cw_ref, vec_ref, wr_ref, wi_ref)
        a_scr[...] = a
        gx_scr[...] = sq * (ig * xc)
        ch_scr[0:1, :] = _chained_scan(a_scr, gx_scr, hs_ref, cum_scr, ch_scr[0:1, :])
        cu_scr[...] = u[ts - SUBLANES:, :]

    w = TOK_WIDTH
    return pl.pallas_call(
        body, grid=(s // ts,),
        in_specs=[_rows(ts, w), _full((SUBLANES, w)), _full((SUBLANES, w)),
                  _full((w, w)), _full((w, w))],
        out_specs=_rows(ts, w),
        out_shape=jax.ShapeDtypeStruct((s, w), F32),
        scratch_shapes=[pltpu.VMEM((SUBLANES, w), F32), pltpu.VMEM((SUBLANES, w), F32)]
                       + [pltpu.VMEM((ts, w), F32)] * 3,
        name="lru_fwd", compiler_params=_cp(1))(z1, cw8, vec8, wr, wi)


def _lru_bwd(z1, dhs, hs, cw8, vec8, wr, wi, wr_t, wi_t, ts):
    s = z1.shape[0]
    nb = s // ts
    w = TOK_WIDTH
    tiles = ts // SUBLANES

    def body(u_ref, up_ref, dhs_ref, hs_ref, hsp_ref, cw_ref, vec_ref, wr_ref, wi_ref,
             wrt_ref, wit_ref, du_ref, dwr_ref, dwi_ref, dvec_ref,
             cc_scr, cd_scr, a_scr, dh_scr):
        i = pl.program_id(0)

        @pl.when(i == 0)
        def _():
            cc_scr[...] = jnp.zeros_like(cc_scr)
            cd_scr[...] = jnp.zeros_like(cd_scr)
            dwr_ref[...] = jnp.zeros_like(dwr_ref)
            dwi_ref[...] = jnp.zeros_like(dwi_ref)
            dvec_ref[...] = jnp.zeros_like(dvec_ref)

        u = u_ref[...]
        first = i == nb - 1
        carry8 = jnp.where(first, 0.0, up_ref[...])
        xc, r, ig, cneg, a, sq, taps = _lru_gates(u, carry8, cw_ref, vec_ref, wr_ref, wi_ref)
        a_scr[...] = a

        def step(n, c):
            t = ts - 1 - n
            dh = dhs_ref[pl.ds(t, 1), :] + c
            dh_scr[pl.ds(t, 1), :] = dh
            return a_scr[pl.ds(t, 1), :] * dh

        cc_scr[0:1, :] = lax.fori_loop(0, ts, step, cc_scr[0:1, :])
        dh = dh_scr[...]
        hprev = _shift_down(hs_ref[...], jnp.where(first, 0.0, hsp_ref[...]), 1)
        ix = ig * xc
        dix = dh * sq
        dlog_a = dh * hprev * a - (dh * ix) * (a * a) / sq
        dpr = (dlog_a * cneg) * r * (1.0 - r)
        dpi = (dix * xc) * ig * (1.0 - ig)
        dprb, dpib = dpr.astype(BF16), dpi.astype(BF16)
        xb = xc.astype(BF16)
        dwr_ref[...] += _dot_tn(xb, dprb)
        dwi_ref[...] += _dot_tn(xb, dpib)
        dxc = dix * ig + _dot(dprb, wrt_ref[...]) + _dot(dpib, wit_ref[...])
        for k in range(CONV_W):
            dvec_ref[3 - k:4 - k, :] += jnp.sum(dxc * taps[k], axis=0, keepdims=True)
        dvec_ref[4:5, :] += jnp.sum(dxc, axis=0, keepdims=True)
        dvec_ref[5:6, :] += jnp.sum(dpr, axis=0, keepdims=True)
        dvec_ref[6:7, :] += jnp.sum(dpi, axis=0, keepdims=True)
        dvec_ref[7:8, :] += (jnp.sum(dlog_a * r, axis=0, keepdims=True)
                             * (LRU_C * _sigmoid(-vec_ref[3:4, :])))
        nxt = cd_scr[...]
        du = cw_ref[3:4, :] * dxc
        for k in range(1, CONV_W):
            du = du + cw_ref[3 - k:4 - k, :] * _shift_up(dxc, nxt, k)
        du_ref[...] = du.astype(BF16)
        cd_scr[...] = dxc[0:SUBLANES, :]

    rev = lambda i: (nb - 1 - i, 0)
    prev8 = lambda i: (jnp.maximum((nb - 1 - i) * tiles - 1, 0), 0)
    blk = pl.BlockSpec((ts, w), rev)
    before = pl.BlockSpec((SUBLANES, w), prev8)
    scr = pltpu.VMEM((ts, w), F32)
    return pl.pallas_call(
        body, grid=(nb,),
        in_specs=[blk, before, blk, blk, before,
                  _full((SUBLANES, w)), _full((SUBLANES, w)),
                  _full((w, w)), _full((w, w)), _full((w, w)), _full((w, w))],
        out_specs=[blk, _full((w, w)), _full((w, w)), _full((SUBLANES, w))],
        out_shape=[jax.ShapeDtypeStruct((s, w), BF16), jax.ShapeDtypeStruct((w, w), F32),
                   jax.ShapeDtypeStruct((w, w), F32), jax.ShapeDtypeStruct((SUBLANES, w), F32)],
        scratch_shapes=[pltpu.VMEM((SUBLANES, w), F32), pltpu.VMEM((SUBLANES, w), F32),
                        scr, scr],
        name="lru_bwd", compiler_params=_cp(1))(
            z1, z1, dhs, hs, hs, cw8, vec8, wr, wi, wr_t, wi_t)


def _adamw(parts, w, m, v, name):
    n = len(parts)
    rows_per = parts[0].shape[1]

    def body(*refs):
        p_refs = refs[:n]
        w_ref, m_ref, v_ref, g_ref, d_ref, nm_ref, nv_ref = refs[n:]
        for l, p_ref in enumerate(p_refs):
            rows = slice(l * rows_per, (l + 1) * rows_per)
            g = p_ref[0].astype(F32)
            for dev in range(1, N_DEV):
                g = g + p_ref[dev].astype(F32)
            g_ref[rows, :] = g
            nm = ADAM_B1 * m_ref[rows, :] + (1.0 - ADAM_B1) * g
            nv = ADAM_B2 * v_ref[rows, :] + (1.0 - ADAM_B2) * (g * g)
            m_hat = nm / (1.0 - ADAM_B1 ** ADAM_STEP)
            v_hat = nv / (1.0 - ADAM_B2 ** ADAM_STEP)
            d_ref[rows, :] = -ADAM_LR * (m_hat / (jnp.sqrt(v_hat) + ADAM_EPS)
                                         + ADAM_WD * w_ref[rows, :])
            nm_ref[rows, :] = nm
            nv_ref[rows, :] = nv

    out = jax.ShapeDtypeStruct(w.shape, F32)
    return pl.pallas_call(
        body, out_shape=[out] * 4, name=name,
        compiler_params=pltpu.CompilerParams(vmem_limit_bytes=VMEM_LIMIT))(*parts, w, m, v)


ANY = pl.BlockSpec(memory_space=pl.ANY)
MESH = pl.DeviceIdType.MESH


def _slot(p):
    return 4 * p[0] + 2 * p[1] + p[2]


def _allgather(xs):
    n = len(xs)

    def body(*refs):
        x_refs, o_refs = refs[:n], refs[n:2 * n]
        send_sems, recv_sems, local_sems = refs[2 * n:]
        x, y, c = lax.axis_index("x"), lax.axis_index("y"), lax.axis_index("c")
        me, sibling = (x, y, c), (x, y, 1 - c)
        chips = [(1 - x, y), (x, 1 - y), (1 - x, 1 - y)]

        def copy(a, k, block, to, from_input=False):
            dst = o_refs[a].at[_slot(block)]
            return pltpu.make_async_remote_copy(
                src_ref=x_refs[a] if from_input else dst, dst_ref=dst,
                send_sem=send_sems.at[a, k], recv_sem=recv_sems.at[a, k],
                device_id=to, device_id_type=MESH)

        mine = [pltpu.make_async_copy(x_refs[a], o_refs[a].at[_slot(me)], local_sems.at[a])
                for a in range(n)]
        for cp in mine:
            cp.start()
        first = []
        for a in range(n):
            first.append(copy(a, 0, me, sibling, True))
            first += [copy(a, 1 + j, me, (*chip, c), True) for j, chip in enumerate(chips)]
        for cp in first:
            cp.start()
        passed = []
        for j, chip in enumerate(chips):
            for a in range(n):
                copy(a, 1 + j, (*chip, c), me).wait_recv()
                cp = copy(a, 4 + j, (*chip, c), sibling)
                cp.start()
                passed.append(cp)
        for a in range(n):
            copy(a, 0, sibling, me).wait_recv()
            for j, chip in enumerate(chips):
                copy(a, 4 + j, (*chip, 1 - c), me).wait_recv()
        for cp in first + passed:
            cp.wait_send()
        for cp in mine:
            cp.wait()

    return pl.pallas_call(
        body,
        out_shape=[jax.ShapeDtypeStruct((N_DEV,) + t.shape, t.dtype) for t in xs],
        in_specs=[ANY] * n, out_specs=[ANY] * n,
        scratch_shapes=[pltpu.SemaphoreType.DMA((n, 7)), pltpu.SemaphoreType.DMA((n, 7)),
                        pltpu.SemaphoreType.DMA((n,))],
        name="allgather_weights")(*xs)


class _Exchange:
    def __init__(self, arrays, kinds):
        self.arrays, self.kinds, self.n = list(arrays), list(kinds), len(arrays)
        self.shapes = [self._part_shape(a, k) for a, k in zip(arrays, kinds)]
        self.out_shape = [jax.ShapeDtypeStruct((N_DEV,) + shp, a.dtype)
                          for shp, a in zip(self.shapes, arrays)]
        self.scratch = [pltpu.SemaphoreType.DMA((self.n, N_DEV - 1)),
                        pltpu.SemaphoreType.DMA((self.n, N_DEV - 1)),
                        pltpu.SemaphoreType.DMA((self.n,))]

    @staticmethod
    def _part_shape(arr, kind):
        if kind == "chunks":
            return arr.shape[1:]
        if kind == "cols":
            return (arr.shape[0], arr.shape[1] // N_DEV)
        if kind == "rows":
            return (arr.shape[0] // N_DEV, arr.shape[1])
        return arr.shape

    def copies(self, in_refs, out_refs, sems):
        send_sems, recv_sems, local_sems = sems
        x, y, c = lax.axis_index("x"), lax.axis_index("y"), lax.axis_index("c")
        me = _slot((x, y, c))

        def part(a, dev):
            ref, kind, shp = in_refs[a], self.kinds[a], self.shapes[a]
            if kind == "chunks":
                return ref.at[dev]
            if kind == "cols":
                return ref.at[:, pl.ds(pl.multiple_of(dev * shp[1], LANES), shp[1])]
            if kind == "rows":
                return ref.at[pl.ds(pl.multiple_of(dev * shp[0], SUBLANES), shp[0]), :]
            return ref

        cps = [pltpu.make_async_copy(part(a, me), out_refs[a].at[me], local_sems.at[a])
               for a in range(self.n)]
        for rel in range(1, N_DEV):
            peer = (x ^ (rel >> 2), y ^ ((rel >> 1) & 1), c ^ (rel & 1))
            for a in range(self.n):
                cps.append(pltpu.make_async_remote_copy(
                    src_ref=part(a, _slot(peer)), dst_ref=out_refs[a].at[me],
                    send_sem=send_sems.at[a, rel - 1], recv_sem=recv_sems.at[a, rel - 1],
                    device_id=peer, device_id_type=MESH))
        return cps


def _exchange_grads(arrays, kinds, name):
    ex = _Exchange(arrays, kinds)
    n = ex.n

    def body(*refs):
        cps = ex.copies(refs[:n], refs[n:2 * n], refs[2 * n:])
        for cp in cps:
            cp.start()
        for cp in cps:
            cp.wait()

    return pl.pallas_call(
        body, out_shape=ex.out_shape, in_specs=[ANY] * n, out_specs=[ANY] * n,
        scratch_shapes=ex.scratch, name=name)(*arrays)


BIG = [("mla_w_in", (D_MODEL, MLA_IN), 1), ("mla_w_uq", (Q_LORA, N_TOK_HEADS * QK_DIM), 1),
       ("mla_w_ukv", (KV_LORA, N_TOK_HEADS * 2 * HEAD_DIM), 1), ("lru_w_in", (D_MODEL, LRU_IN), 1),
       ("w_mem_kv", (2, D_MODEL, 2 * MEM_WIDTH), 1), ("w_out", (2, MIX_WIDTH, D_MODEL), 1)]
SMALL = [("lru_conv_w", (CONV_W, TOK_WIDTH), 1), ("lru_conv_b", (TOK_WIDTH,), 0),
         ("lru_b_rgate", (TOK_WIDTH,), 0), ("lru_b_igate", (TOK_WIDTH,), 0),
         ("lru_lambda", (TOK_WIDTH,), 0)]
REPL = [("mla_q_norm", (Q_LORA,)), ("mla_kv_norm", (KV_LORA,)),
        ("lru_w_rgate", (N_TOK_HEADS, HEAD_DIM, HEAD_DIM)),
        ("lru_w_igate", (N_TOK_HEADS, HEAD_DIM, HEAD_DIM)),
        ("ln_g", (2, D_MODEL)), ("ln_b", (2, D_MODEL))]


def _shard_shape(shape, axis):
    return tuple(d // N_DEV if a == axis else d for a, d in enumerate(shape))


def _size(shape):
    return math.prod(shape)


BIG_ROWS = sum(_size(s) for _, s, _ in BIG) // N_DEV // LANES
SMALL_ROWS = SUBLANES


def _pack_rows(flat_parts, rows):
    flat = jnp.concatenate([p.reshape(-1) for p in flat_parts])
    return jnp.pad(flat, (0, rows * LANES - flat.shape[0])).reshape(rows, LANES)


def _to_chunks(full, axis):
    shape = full.shape
    split = shape[:axis] + (N_DEV, shape[axis] // N_DEV) + shape[axis + 1:]
    return jnp.moveaxis(full.reshape(split), axis, 0).reshape(N_DEV, -1)


def _from_chunks(chunks, shape, axis):
    sh = _shard_shape(shape, axis)
    t = chunks.reshape((N_DEV,) + sh)
    t = jnp.moveaxis(t, 0, axis)
    return t.reshape(shape)


def _split_flat(flat2d, table):
    out, off = [], 0
    for size in table:
        out.append(flat2d[:, off:off + size])
        off += size
    return out


def _win0_to_padded(w):
    z = lambda n: jnp.zeros((w.shape[0], n), w.dtype)
    return jnp.concatenate([w[:, 0:640], z(KR_LANE), w[:, 640:672],
                            z(LANES - KR_LANE - QK_ROPE), w[:, 672:1952]], axis=1)


def _win0_from_padded(wp):
    k0 = ZA_KR + KR_LANE
    return jnp.concatenate([wp[:, 0:640], wp[:, k0:k0 + QK_ROPE], wp[:, ZA_W:ZP]], axis=1)


def _pad_heads(w, per_head, lo, hi):
    t = w.reshape(w.shape[0], N_TOK_HEADS, per_head)[:, :, lo:hi]
    t = jnp.pad(t, ((0, 0), (0, 0), (0, HEAD_PAD - (hi - lo))))
    return t.reshape(w.shape[0], QKV_PAD)


def _unpad_heads(wp, width):
    return wp.reshape(wp.shape[0], N_TOK_HEADS, HEAD_PAD)[:, :, :width]


def _block_diag(w):
    eye = jnp.eye(N_TOK_HEADS, dtype=w.dtype)
    return (w[:, :, None, :] * eye[:, None, :, None]).reshape(TOK_WIDTH, TOK_WIDTH)


def _diag_blocks(d):
    t = d.reshape(N_TOK_HEADS, HEAD_DIM, N_TOK_HEADS, HEAD_DIM)
    return jnp.stack([t[g, :, g, :] for g in range(N_TOK_HEADS)])


def _rope_tables(positions):
    half = QK_ROPE // 2
    inv_freq = ROPE_THETA ** (-jnp.arange(half, dtype=F32) / half)
    ang = positions.astype(F32)[:, None] * inv_freq
    cos, sin = jnp.cos(ang), jnp.sin(ang)
    s = positions.shape[0]
    one, zero = jnp.ones((s, QK_NOPE), F32), jnp.zeros((s, half), F32)
    tail = jnp.zeros((s, HEAD_PAD - QK_DIM), F32)
    znope = jnp.zeros((s, QK_NOPE), F32)
    c = jnp.concatenate([one, cos, cos, tail], axis=1)
    sa = jnp.concatenate([znope, -sin, zero, tail], axis=1)
    sb = jnp.concatenate([znope, zero, sin, tail], axis=1)
    return c, sa, sb


def _local_step(x, mem, positions, tgt, wts, ts, tatt, early_exchange):
    bf = lambda t: t.astype(BF16)
    win0 = _win0_to_padded(wts["mla_w_in"])
    wuq = _pad_heads(wts["mla_w_uq"], QK_DIM, 0, QK_DIM)
    wukv = jnp.concatenate([_pad_heads(wts["mla_w_ukv"], 2 * HEAD_DIM, 0, QK_NOPE),
                            _pad_heads(wts["mla_w_ukv"], 2 * HEAD_DIM, QK_NOPE, 2 * HEAD_DIM)],
                           axis=1)
    win1 = wts["lru_w_in"]
    wmkv, wout = wts["w_mem_kv"], wts["w_out"]
    gq = wts["mla_q_norm"].reshape(1, Q_LORA)
    gkv = wts["mla_kv_norm"].reshape(1, KV_LORA)
    ln_g, ln_b = wts["ln_g"], wts["ln_b"]
    wr, wi = bf(_block_diag(wts["lru_w_rgate"])), bf(_block_diag(wts["lru_w_igate"]))
    cw8 = jnp.pad(wts["lru_conv_w"], ((0, SUBLANES - CONV_W), (0, 0)))
    vec8 = jnp.pad(jnp.stack([wts["lru_conv_b"], wts["lru_b_rgate"], wts["lru_b_igate"],
                              wts["lru_lambda"]]), ((0, SUBLANES - 4), (0, 0)))
    tabs = _rope_tables(positions)
    tmem = mem.shape[0]

    za0, zg0 = _rowmm(x, win0, [ZA_W, ZG_W], "in_proj0", ts)
    q, k, v = _mla_prep_fwd(za0, tabs, gq, gkv, wuq, wukv, ts)
    o, lse = _flash_fwd(q, k, v, tatt, FWD_HEADS)
    mkv0, = _rowmm(mem, wmkv[0], [2 * MEM_WIDTH], "mem_kv0", tmem)
    cat0, y0 = _gate_mem_fwd(o, zg0, mkv0, 0, MIX_WIDTH, True, "gate_mem_fwd0", ts)
    del o
    pre0, h1 = _outproj_ln_fwd(y0, wout[0], x, ln_g[0:1], ln_b[0:1], None, "outproj_ln_fwd0", ts)
    u1, zg1 = _rowmm(h1, win1, [ZA_W, ZG_W], "in_proj1", ts)
    hs = _lru_fwd(u1, cw8, vec8, wr, wi, ts)
    mkv1, = _rowmm(mem, wmkv[1], [2 * MEM_WIDTH], "mem_kv1", tmem)
    cat1, y1 = _gate_mem_fwd(hs, zg1, mkv1, 0, MIX_WIDTH, False, "gate_mem_fwd1", ts)
    dpre1, dgb1, loss8 = _outproj_ln_fwd(y1, wout[1], h1, ln_g[1:2], ln_b[1:2], tgt,
                                         "outproj_ln_loss", ts)
    loss = loss8[0, 0]

    dy1, dwout1 = _outproj_bwd(dpre1, y1, wout[1].T, "outproj_bwd1", ts)
    dzg1, dhs, dmkv1 = _gate_mem_bwd(dy1, cat1, zg1, mkv1, None, 0, MIX_WIDTH,
                                     "gate_mem_bwd1", ts)
    du, dwr, dwi, dvec = _lru_bwd(u1, dhs, hs, cw8, vec8, wr, wi, wr.T, wi.T, ts)
    dh1, dwin1 = _linear_bwd(h1, [du, dzg1], [0, ZA_W], win1.T, dpre1, "in_proj_bwd1", ts)
    dwmkv1 = _wgrad_small(mem, dmkv1, "mem_kv_bwd1")
    dpre0, dy0, dwout0, dgb0 = _outproj_ln_bwd(dh1, pre0, ln_g[0:1], y0, wout[0].T,
                                               "outproj_ln_bwd0", ts)
    dzg0, do, dmkv0, stats = _gate_mem_bwd(dy0, cat0, zg0, mkv0, lse, 0, MIX_WIDTH,
                                           "gate_mem_bwd0", ts)
    dwmkv0 = _wgrad_small(mem, dmkv0, "mem_kv_bwd0")
    early = {
        "lru_w_in": dwin1,
        "lru_small": dvec,
        "lru_w_rgate": _diag_blocks(dwr).reshape(TOK_WIDTH, HEAD_DIM),
        "lru_w_igate": _diag_blocks(dwi).reshape(TOK_WIDTH, HEAD_DIM),
        "w_mem_kv": [dwmkv0, dwmkv1],
        "w_out": [dwout0, dwout1],
    }
    (dq, dk, dv), got_early = _flash_bwd(q, k, v, stats, do, tatt, BWD_HEADS,
                                         early_exchange(early))
    dza, dzk, dwuq_p, dwukv_p, dg = _mla_prep_bwd(za0, dq, dk, dv, tabs, gq, gkv,
                                                  wuq.T, wukv.T, ts)
    gx, dwin0_p = _linear_bwd(x, [dza, dzk, dzg0], [ZA_CQ, ZA_KR, ZA_W], win0.T, dpre0,
                              "in_proj_bwd0", ts)

    dwukv = jnp.concatenate([_unpad_heads(dwukv_p[:, :QKV_PAD], HEAD_DIM),
                             _unpad_heads(dwukv_p[:, QKV_PAD:], HEAD_DIM)], axis=2)
    zrow = jnp.zeros((1, D_MODEL), F32)
    gains = jnp.pad(dg[0:1], ((0, 0), (0, D_MODEL - Q_LORA - KV_LORA)))
    small_repl = jnp.concatenate([dgb0[0:2], dgb1[0:2], gains,
                                  loss * jnp.ones((1, D_MODEL), F32), zrow, zrow], axis=0)
    late = {
        "mla_w_in": _win0_from_padded(dwin0_p),
        "mla_w_uq": _unpad_heads(dwuq_p, QK_DIM).reshape(Q_LORA, N_TOK_HEADS * QK_DIM),
        "mla_w_ukv": dwukv.reshape(KV_LORA, N_TOK_HEADS * 2 * HEAD_DIM),
        "small_repl": small_repl,
    }
    return gx, early, got_early, late


WEIGHT_ORDER = ["mla_w_in", "mla_q_norm", "mla_w_uq", "mla_kv_norm", "mla_w_ukv", "lru_w_in",
                "lru_conv_w", "lru_conv_b", "lru_w_rgate", "lru_b_rgate", "lru_w_igate",
                "lru_b_igate", "lru_lambda", "w_mem_kv", "w_out", "ln_g", "ln_b"]


def kernel(x, mem, positions, mla_w_in, mla_q_norm, mla_w_uq, mla_kv_norm, mla_w_ukv, lru_w_in, lru_conv_w, lru_conv_b, lru_w_rgate, lru_b_rgate, lru_w_igate, lru_b_igate, lru_lambda, w_mem_kv, w_out, ln_g, ln_b, loss_target, m_mla_w_in, m_mla_q_norm, m_mla_w_uq, m_mla_kv_norm, m_mla_w_ukv, m_lru_w_in, m_lru_conv_w, m_lru_conv_b, m_lru_w_rgate, m_lru_b_rgate, m_lru_w_igate, m_lru_b_igate, m_lru_lambda, m_w_mem_kv, m_w_out, m_ln_g, m_ln_b, v_mla_w_in, v_mla_q_norm, v_mla_w_uq, v_mla_kv_norm, v_mla_w_ukv, v_lru_w_in, v_lru_conv_w, v_lru_conv_b, v_lru_w_rgate, v_lru_b_rgate, v_lru_w_igate, v_lru_b_igate, v_lru_lambda, v_w_mem_kv, v_w_out, v_ln_g, v_ln_b):
    w_in = dict(mla_w_in=mla_w_in, mla_q_norm=mla_q_norm, mla_w_uq=mla_w_uq,
                mla_kv_norm=mla_kv_norm, mla_w_ukv=mla_w_ukv, lru_w_in=lru_w_in,
                lru_conv_w=lru_conv_w, lru_conv_b=lru_conv_b, lru_w_rgate=lru_w_rgate,
                lru_b_rgate=lru_b_rgate, lru_w_igate=lru_w_igate, lru_b_igate=lru_b_igate,
                lru_lambda=lru_lambda, w_mem_kv=w_mem_kv, w_out=w_out, ln_g=ln_g, ln_b=ln_b)
    m_in = dict(mla_w_in=m_mla_w_in, mla_q_norm=m_mla_q_norm, mla_w_uq=m_mla_w_uq,
                mla_kv_norm=m_mla_kv_norm, mla_w_ukv=m_mla_w_ukv, lru_w_in=m_lru_w_in,
                lru_conv_w=m_lru_conv_w, lru_conv_b=m_lru_conv_b, lru_w_rgate=m_lru_w_rgate,
                lru_b_rgate=m_lru_b_rgate, lru_w_igate=m_lru_w_igate, lru_b_igate=m_lru_b_igate,
                lru_lambda=m_lru_lambda, w_mem_kv=m_w_mem_kv, w_out=m_w_out, ln_g=m_ln_g,
                ln_b=m_ln_b)
    v_in = dict(mla_w_in=v_mla_w_in, mla_q_norm=v_mla_q_norm, mla_w_uq=v_mla_w_uq,
                mla_kv_norm=v_mla_kv_norm, mla_w_ukv=v_mla_w_ukv, lru_w_in=v_lru_w_in,
                lru_conv_w=v_lru_conv_w, lru_conv_b=v_lru_conv_b, lru_w_rgate=v_lru_w_rgate,
                lru_b_rgate=v_lru_b_rgate, lru_w_igate=v_lru_w_igate, lru_b_igate=v_lru_b_igate,
                lru_lambda=v_lru_lambda, w_mem_kv=v_w_mem_kv, w_out=v_w_out, ln_g=v_ln_g,
                ln_b=v_ln_b)
    s = x.shape[1]
    ts = min(ROW_BLOCK, s)
    tatt = min(ATT_BLOCK, s)
    big_sizes = [_size(sh) // N_DEV for _, sh, _ in BIG]
    small_sizes = [_size(sh) // N_DEV for _, sh, _ in SMALL]

    big_local = _pack_rows([w_in[n] for n, _, _ in BIG], BIG_ROWS).astype(BF16)
    small_local = _pack_rows([w_in[n] for n, _, _ in SMALL], SMALL_ROWS)
    big_all, small_all = _allgather([big_local, small_local])
    wts = {}
    for (n, sh, ax), part in zip(BIG, _split_flat(big_all.reshape(N_DEV, -1), big_sizes)):
        wts[n] = _from_chunks(part, sh, ax)
    for (n, sh, ax), part in zip(SMALL, _split_flat(small_all.reshape(N_DEV, -1), small_sizes)):
        wts[n] = _from_chunks(part, sh, ax)
    for n, sh in REPL:
        wts[n] = w_in[n].reshape(sh)

    def early_exchange(g):
        small_chunks = jnp.moveaxis(g["lru_small"].reshape(SUBLANES, N_DEV, -1), 1, 0)
        sends = [(g["lru_w_in"], "cols"),
                 (g["w_mem_kv"][0], "rows"), (g["w_mem_kv"][1], "rows"),
                 (g["w_out"][0], "rows"), (g["w_out"][1], "rows"),
                 (small_chunks, "chunks"), (g["lru_w_rgate"], "all"), (g["lru_w_igate"], "all")]
        return _Exchange([a for a, _ in sends], [k for _, k in sends])

    gx, _, got_early, late = _local_step(x[0], mem[0], positions[0], loss_target[0], wts,
                                         ts, tatt, early_exchange)

    def chunked(name, shape):
        w = shape[1] // N_DEV
        return _to_chunks(late[name], 1).reshape(N_DEV, shape[0], w).astype(BF16)

    got_late = _exchange_grads(
        [chunked("mla_w_in", (D_MODEL, MLA_IN)),
         chunked("mla_w_uq", (Q_LORA, N_TOK_HEADS * QK_DIM)),
         chunked("mla_w_ukv", (KV_LORA, N_TOK_HEADS * 2 * HEAD_DIM)), late["small_repl"]],
        ["chunks", "chunks", "chunks", "all"], "exchange_grads")
    got = list(got_late[:3]) + list(got_early) + [got_late[3]]

    def small_sharded(d):
        return jnp.concatenate([d["lru_conv_w"].reshape(CONV_W, -1), d["lru_conv_b"],
                                d["lru_b_rgate"], d["lru_b_igate"], d["lru_lambda"]], axis=0)

    def small_replicated(d):
        gains = jnp.concatenate([d["mla_q_norm"], d["mla_kv_norm"]], axis=1)
        gains = jnp.pad(gains, ((0, 0), (0, D_MODEL - gains.shape[1])))
        return jnp.concatenate([d["ln_g"][0:1], d["ln_b"][0:1], d["ln_g"][1:2], d["ln_b"][1:2],
                                gains, jnp.zeros((3, D_MODEL), F32)], axis=0)

    def flat2(d, name):
        t = d[name]
        return t.reshape(-1, t.shape[-1])

    def update(parts, view, name):
        return _adamw(parts, view(w_in), view(m_in), view(v_in), "adamw_" + name)

    res = {}
    for idx, name in [(0, "mla_w_in"), (1, "mla_w_uq"), (2, "mla_w_ukv"), (3, "lru_w_in"),
                      (9, "lru_w_rgate"), (10, "lru_w_igate")]:
        res[name] = update([got[idx]], functools.partial(flat2, name=name), name)
    res["w_mem_kv"] = update([got[4], got[5]], functools.partial(flat2, name="w_mem_kv"),
                             "w_mem_kv")
    res["w_out"] = update([got[6], got[7]], functools.partial(flat2, name="w_out"), "w_out")
    res_ss = update([got[8]], small_sharded, "small_sharded")
    res_sr = update([got[11]], small_replicated, "small_replicated")
    loss = res_sr[0][5, 0]

    result = [loss, gx.reshape(x.shape)]
    for kind in range(4):
        ss, sr = res_ss[kind], res_sr[kind]
        out = {n: res[n][kind].reshape(w_in[n].shape) for n in res}
        out["lru_conv_w"] = ss[0:CONV_W].reshape(w_in["lru_conv_w"].shape)
        out["lru_conv_b"], out["lru_b_rgate"] = ss[4:5], ss[5:6]
        out["lru_b_igate"], out["lru_lambda"] = ss[6:7], ss[7:8]
        out["ln_g"] = jnp.concatenate([sr[0:1], sr[2:3]], axis=0)
        out["ln_b"] = jnp.concatenate([sr[1:2], sr[3:4]], axis=0)
        out["mla_q_norm"] = sr[4:5, 0:Q_LORA]
        out["mla_kv_norm"] = sr[4:5, Q_LORA:Q_LORA + KV_LORA]
        result += [out[n] for n in WEIGHT_ORDER]
    return tuple(result)
```

```python
import functools
import math

import jax
import jax.numpy as jnp
from jax import lax
from jax.experimental import pallas as pl
from jax.experimental.pallas import tpu as pltpu

F32 = jnp.float32
BF16 = jnp.bfloat16

D_MODEL = 1024
MEM_LEN = 256
HEAD_DIM = 64
N_TOK_HEADS = 12
N_MEM_HEADS = 4
TOK_WIDTH = 768
MEM_WIDTH = 256
MIX_WIDTH = 1024
Q_LORA = 384
KV_LORA = 256
QK_NOPE = 64
QK_ROPE = 32
QK_DIM = 96
ROPE_THETA = 10000.0
CONV_W = 4
LRU_C = 8.0
ALPHA = (2.0 * 2) ** 0.25
NORM_EPS = 1e-6
MLA_IN = 1952
LRU_IN = 2048
ADAM_LR = 0.001
ADAM_B1 = 0.9
ADAM_B2 = 0.999
ADAM_EPS = 1e-08
ADAM_WD = 0.01
ADAM_STEP = 10

N_DEV = 8
LANES = 128
SUBLANES = 8
HEAD_PAD = 128
QKV_PAD = N_TOK_HEADS * HEAD_PAD
ZP = 2048
ZA_W = TOK_WIDTH
ZG_W = MIX_WIDTH + MEM_WIDTH
ZA_CQ, ZA_CKV, ZA_KR = 0, 384, 640
KR_LANE = 64

ROW_BLOCK = 512
ATT_BLOCK = 512
LOOKAHEAD = 3
FWD_HEADS = 12
BWD_HEADS = 4
VMEM_LIMIT = 56 * 1024 * 1024
NEG_BIG = -1e30
STRIP = 32
SCAN_SEGMENTS = 4
LOG2E = math.log2(math.e)


def _cp(n_axes):
    return pltpu.CompilerParams(dimension_semantics=("arbitrary",) * n_axes,
                                vmem_limit_bytes=VMEM_LIMIT)


def _dot(a, b):
    return jnp.dot(a, b, preferred_element_type=F32)


def _dot_nt(a, b):
    return lax.dot_general(a, b, (((1,), (1,)), ((), ())), preferred_element_type=F32)


def _dot_tn(a, b):
    return lax.dot_general(a, b, (((0,), (0,)), ((), ())), preferred_element_type=F32)


def _sigmoid(t):
    return 1.0 / (1.0 + jnp.exp(-t))


def _lane(shape):
    return lax.broadcasted_iota(jnp.int32, shape, len(shape) - 1)


def _full(shape):
    nd = len(shape)
    return pl.BlockSpec(shape, lambda *_: (0,) * nd)


def _rows(ts, width, col=0):
    return pl.BlockSpec((ts, width), lambda i: (i, col))


def _heads(ts):
    return pl.BlockSpec((N_TOK_HEADS, ts, HEAD_PAD), lambda i: (0, i, 0))


def _rowmm(x, w, widths, name, ts):
    s, k = x.shape
    n = w.shape[1]
    offs = [sum(widths[:a]) for a in range(len(widths))]

    def body(x_ref, w_ref, *o_refs):
        res = _dot(x_ref[...].astype(BF16), w_ref[...])
        for o_ref, off, wd in zip(o_refs, offs, widths):
            o_ref[...] = res[:, off:off + wd]

    return pl.pallas_call(
        body, grid=(s // ts,),
        in_specs=[_rows(ts, k), _full((k, n))],
        out_specs=[_rows(ts, wd) for wd in widths],
        out_shape=[jax.ShapeDtypeStruct((s, wd), F32) for wd in widths],
        name=name, compiler_params=_cp(1))(x, w)


def _rms_parts(t):
    rs = lax.rsqrt(jnp.mean(t * t, axis=-1, keepdims=True) + NORM_EPS)
    return rs


def _rope(t, c, sa, sb):
    return t * c + pltpu.roll(t, LANES - 16, 1) * sa + pltpu.roll(t, 16, 1) * sb


def _rope_t(d, c, sa, sb):
    return d * c + pltpu.roll(d * sa, 16, 1) + pltpu.roll(d * sb, LANES - 16, 1)


def _mla_prep_fwd(z0, tabs, gq, gkv, wuq, wukv, ts):
    s = z0.shape[0]

    def body(z_ref, c_ref, sa_ref, sb_ref, gq_ref, gkv_ref, wuq_ref, wukv_ref,
             q_ref, k_ref, v_ref):
        cq = z_ref[:, ZA_CQ:ZA_CQ + Q_LORA]
        ckv = z_ref[:, ZA_CKV:ZA_CKV + KV_LORA]
        kr = z_ref[:, ZA_KR:ZA_KR + LANES]
        cqn = cq * _rms_parts(cq) * gq_ref[...]
        ckvn = ckv * _rms_parts(ckv) * gkv_ref[...]
        q = _dot(cqn.astype(BF16), wuq_ref[...])
        kv = _dot(ckvn.astype(BF16), wukv_ref[...])
        c, sa, sb = c_ref[...], sa_ref[...], sb_ref[...]
        krope = _rope(kr, c, sa, sb)
        pad_lane = _lane((ts, HEAD_PAD)) >= HEAD_DIM
        for h in range(N_TOK_HEADS):
            sl = slice(h * HEAD_PAD, (h + 1) * HEAD_PAD)
            q_ref[h] = _rope(q[:, sl], c, sa, sb).astype(BF16)
            k_ref[h] = (kv[:, sl] + krope).astype(BF16)
            vh = kv[:, QKV_PAD + h * HEAD_PAD:QKV_PAD + (h + 1) * HEAD_PAD]
            v_ref[h] = jnp.where(pad_lane, 1.0, vh).astype(BF16)

    out = jax.ShapeDtypeStruct((N_TOK_HEADS, s, HEAD_PAD), BF16)
    return pl.pallas_call(
        body, grid=(s // ts,),
        in_specs=[_rows(ts, ZA_W), _rows(ts, LANES), _rows(ts, LANES), _rows(ts, LANES),
                  _full((1, Q_LORA)), _full((1, KV_LORA)),
                  _full((Q_LORA, QKV_PAD)), _full((KV_LORA, 2 * QKV_PAD))],
        out_specs=[_heads(ts)] * 3,
        out_shape=[out, out, out],
        name="mla_prep_fwd", compiler_params=_cp(1))(z0, *tabs, gq, gkv, wuq, wukv)


def _mla_prep_bwd(z0, dq, dk, dv, tabs, gq, gkv, wuq_t, wukv_t, ts):
    s = z0.shape[0]

    def body(z_ref, dq_ref, dk_ref, dv_ref, c_ref, sa_ref, sb_ref, gq_ref, gkv_ref,
             wuqt_ref, wukvt_ref, dza_ref, dzk_ref, dwuq_ref, dwukv_ref, dg_ref):
        @pl.when(pl.program_id(0) == 0)
        def _():
            dwuq_ref[...] = jnp.zeros_like(dwuq_ref)
            dwukv_ref[...] = jnp.zeros_like(dwukv_ref)
            dg_ref[...] = jnp.zeros_like(dg_ref)

        cq = z_ref[:, ZA_CQ:ZA_CQ + Q_LORA]
        ckv = z_ref[:, ZA_CKV:ZA_CKV + KV_LORA]
        rq, rkv = _rms_parts(cq), _rms_parts(ckv)
        gq_, gkv_ = gq_ref[...], gkv_ref[...]
        cqn = (cq * rq * gq_).astype(BF16)
        ckvn = (ckv * rkv * gkv_).astype(BF16)
        c, sa, sb = c_ref[...], sa_ref[...], sb_ref[...]
        dqp, dksum = [], None
        for h in range(N_TOK_HEADS):
            dqp.append(_rope_t(dq_ref[h], c, sa, sb))
            dksum = dk_ref[h] if dksum is None else dksum + dk_ref[h]
        dqp = jnp.concatenate(dqp, axis=1).astype(BF16)
        lane = _lane(dksum.shape)
        dzk_ref[...] = jnp.where((lane >= KR_LANE) & (lane < KR_LANE + QK_ROPE),
                                 _rope_t(dksum, c, sa, sb), 0.0).astype(BF16)
        dkv = jnp.concatenate([dk_ref[h].astype(BF16) for h in range(N_TOK_HEADS)]
                              + [dv_ref[h] for h in range(N_TOK_HEADS)], axis=1)
        dcqn = _dot(dqp, wuqt_ref[...])
        dckvn = _dot(dkv, wukvt_ref[...])
        dwuq_ref[...] += _dot_tn(cqn, dqp)
        dwukv_ref[...] += _dot_tn(ckvn, dkv)
        dg_ref[0:1, 0:Q_LORA] += jnp.sum(dcqn * cq * rq, axis=0, keepdims=True)
        dg_ref[0:1, Q_LORA:Q_LORA + KV_LORA] += jnp.sum(dckvn * ckv * rkv, axis=0, keepdims=True)
        wq = dcqn * gq_
        wkv = dckvn * gkv_
        dcq = rq * wq - cq * (rq * rq * rq) * jnp.mean(wq * cq, axis=-1, keepdims=True)
        dckv = rkv * wkv - ckv * (rkv * rkv * rkv) * jnp.mean(wkv * ckv, axis=-1, keepdims=True)
        dza_ref[:, 0:Q_LORA] = dcq.astype(BF16)
        dza_ref[:, Q_LORA:Q_LORA + KV_LORA] = dckv.astype(BF16)

    na = Q_LORA + KV_LORA
    return pl.pallas_call(
        body, grid=(s // ts,),
        in_specs=[_rows(ts, ZA_W), _heads(ts), _heads(ts), _heads(ts),
                  _rows(ts, LANES), _rows(ts, LANES), _rows(ts, LANES),
                  _full((1, Q_LORA)), _full((1, KV_LORA)),
                  _full((QKV_PAD, Q_LORA)), _full((2 * QKV_PAD, KV_LORA))],
        out_specs=[_rows(ts, na), _rows(ts, LANES), _full((Q_LORA, QKV_PAD)),
                   _full((KV_LORA, 2 * QKV_PAD)), _full((SUBLANES, na))],
        out_shape=[jax.ShapeDtypeStruct((s, na), BF16), jax.ShapeDtypeStruct((s, LANES), BF16),
                   jax.ShapeDtypeStruct((Q_LORA, QKV_PAD), F32),
                   jax.ShapeDtypeStruct((KV_LORA, 2 * QKV_PAD), F32),
                   jax.ShapeDtypeStruct((SUBLANES, na), F32)],
        name="mla_prep_bwd", compiler_params=_cp(1))(
            z0, dq, dk, dv, *tabs, gq, gkv, wuq_t, wukv_t)


def _causal_pairs(nb, by_key):
    if by_key:
        pairs = [(i, j) for j in range(nb) for i in range(j, nb)]
    else:
        pairs = [(i, j) for i in range(nb) for j in range(i + 1)]
    return (jnp.array([p[0] for p in pairs], jnp.int32),
            jnp.array([p[1] for p in pairs], jnp.int32))


def _flash_fwd(q, k, v, t, nh):
    s = q.shape[1]
    itab, jtab = _causal_pairs(s // t, False)
    c2 = LOG2E / math.sqrt(QK_DIM)

    def body(it_ref, jt_ref, q_ref, k_ref, v_ref, o_ref, lse_ref, m_scr, acc_scr):
        pair = pl.program_id(1)
        i, j = it_ref[pair], jt_ref[pair]

        @pl.when(j == 0)
        def _():
            m_scr[...] = jnp.full_like(m_scr, NEG_BIG)
            acc_scr[...] = jnp.zeros_like(acc_scr)

        def softmax_strips(masked, hs, sc, row0):
            ps, als = [], []
            for r0 in range(0, sc.shape[0], STRIP):
                rows = slice(row0 + r0, row0 + r0 + STRIP)
                ch = [sc[r0:r0 + STRIP, n * LANES:(n + 1) * LANES] * c2
                      for n in range(sc.shape[1] // LANES)]
                if masked:
                    rr = row0 + r0 + lax.broadcasted_iota(jnp.int32, (STRIP, LANES), 0)
                    cc = lax.broadcasted_iota(jnp.int32, (STRIP, LANES), 1)
                    ch = [jnp.where(cc + n * LANES <= rr, c_, NEG_BIG) for n, c_ in enumerate(ch)]
                mx = ch[0]
                for c_ in ch[1:]:
                    mx = jnp.maximum(mx, c_)
                m_prev = m_scr[hs, rows, :]
                m_next = jnp.maximum(m_prev, jnp.max(mx, axis=-1, keepdims=True))
                ps.append(jnp.concatenate(
                    [jnp.exp2(c_ - m_next).astype(BF16) for c_ in ch], axis=1))
                als.append(jnp.exp2(m_prev - m_next))
                m_scr[hs, rows, :] = m_next
            return jnp.concatenate(ps, axis=0), jnp.concatenate(als, axis=0)

        def run(masked, parts):
            def scores_of(hs):
                return [_dot_nt(q_ref[hs, r0:r0 + nr, :], k_ref[hs, 0:nk, :])
                        for r0, nr, nk in parts]

            ahead = min(LOOKAHEAD, nh)
            scores = [scores_of(hs) for hs in range(ahead)]
            for hs in range(nh):
                if hs + ahead < nh:
                    scores.append(scores_of(hs + ahead))
                for (r0, nr, nk), sc in zip(parts, scores[hs]):
                    p, alpha = softmax_strips(masked, hs, sc, r0)
                    acc_scr[hs, r0:r0 + nr, :] = (alpha * acc_scr[hs, r0:r0 + nr, :]
                                                  + _dot(p, v_ref[hs, 0:nk, :]))

        @pl.when(j < i)
        def _():
            run(False, [(0, t, t)])

        @pl.when(j == i)
        def _():
            run(True, [(0, t, t)])
            for h in range(nh):
                acc = acc_scr[h]
                l = acc[:, HEAD_DIM:HEAD_DIM + 1]
                o_ref[h] = jnp.where(_lane(acc.shape) < HEAD_DIM, acc / l, 0.0)
                lse_ref[h] = (m_scr[h] + jnp.log2(l)).T[0:1, :]

    qspec = pl.BlockSpec((nh, t, HEAD_PAD), lambda h, p, it, jt: (h, it[p], 0))
    kspec = pl.BlockSpec((nh, t, HEAD_PAD), lambda h, p, it, jt: (h, jt[p], 0))
    lspec = pl.BlockSpec((nh, 1, t), lambda h, p, it, jt: (h, 0, it[p]))
    out = jax.ShapeDtypeStruct((N_TOK_HEADS, s, HEAD_PAD), F32)
    return pl.pallas_call(
        body,
        grid_spec=pltpu.PrefetchScalarGridSpec(
            num_scalar_prefetch=2, grid=(N_TOK_HEADS // nh, itab.shape[0]),
            in_specs=[qspec, kspec, kspec], out_specs=[qspec, lspec],
            scratch_shapes=[pltpu.VMEM((nh, t, HEAD_PAD), F32)] * 2),
        out_shape=[out, jax.ShapeDtypeStruct((N_TOK_HEADS, 1, s), F32)],
        name="flash_fwd", compiler_params=_cp(2))(itab, jtab, q, k, v)


def _flash_bwd(q, k, v, stats, do, t, nh, ex):
    s = q.shape[1]
    nb = s // t
    itab, jtab = _causal_pairs(nb, True)
    npairs = itab.shape[0]
    ngroups = N_TOK_HEADS // nh
    scale = 1.0 / math.sqrt(QK_DIM)
    c2 = LOG2E * scale
    nx = ex.n if ex is not None else 0
    ex_arrays, ex_out_shape, ex_scratch = (
        (ex.arrays, ex.out_shape, ex.scratch) if ex is not None else ([], [], []))

    def body(it_ref, jt_ref, q_ref, k_ref, v_ref, st_ref, do_ref, *rest):
        ex_in, rest = rest[:nx], rest[nx:]
        dq_ref, dk_ref, dv_ref = rest[:3]
        ex_out, rest = rest[3:3 + nx], rest[3 + nx:]
        dk_scr, dv_scr = rest[:2]
        ex_sems = rest[2:]
        pair = pl.program_id(1)
        i, j = it_ref[pair], jt_ref[pair]
        rows_i = pl.ds(pl.multiple_of(i * t, t), t)

        if nx:
            @pl.when(jnp.logical_and(pl.program_id(0) == 0, pair == 0))
            def _():
                for cp in ex.copies(ex_in, ex_out, ex_sems):
                    cp.start()

        @pl.when(i == j)
        def _():
            dk_scr[...] = jnp.zeros_like(dk_scr)
            dv_scr[...] = jnp.zeros_like(dv_scr)

        @pl.when(j == 0)
        def _():
            dq_ref[:, rows_i, :] = jnp.zeros((nh, t, HEAD_PAD), F32)

        def prob_strips(masked, h, sct, dpt, k0, q0):
            ps, dss = [], []
            for r0 in range(0, sct.shape[0], STRIP):
                rows = slice(r0, r0 + STRIP)
                if masked:
                    kk = k0 + r0 + lax.broadcasted_iota(jnp.int32, (STRIP, LANES), 0)
                    qq = q0 + lax.broadcasted_iota(jnp.int32, (STRIP, LANES), 1)
                pcs, dcs = [], []
                for n in range(sct.shape[1] // LANES):
                    cols = slice(n * LANES, (n + 1) * LANES)
                    qcols = slice(q0 + n * LANES, q0 + (n + 1) * LANES)
                    x = sct[rows, cols] * c2
                    if masked:
                        x = jnp.where(kk <= qq + n * LANES, x, NEG_BIG)
                    p = jnp.exp2(x - st_ref[h, 0:1, qcols])
                    pcs.append(p.astype(BF16))
                    dcs.append((p * (dpt[rows, cols] - st_ref[h, 1:2, qcols]) * scale).astype(BF16))
                ps.append(jnp.concatenate(pcs, axis=1))
                dss.append(jnp.concatenate(dcs, axis=1))
            return jnp.concatenate(ps, axis=0), jnp.concatenate(dss, axis=0)

        def run(masked, parts):
            def scores_of(h):
                return [(_dot_nt(k_ref[h, k0:k0 + nk, :], q_ref[h, q0:q0 + nq, :]),
                         _dot_nt(v_ref[h, k0:k0 + nk, :], do_ref[h, q0:q0 + nq, :]))
                        for k0, nk, q0, nq in parts]

            ahead = min(LOOKAHEAD, nh)
            scores = [scores_of(h) for h in range(ahead)]
            for h in range(nh):
                if h + ahead < nh:
                    scores.append(scores_of(h + ahead))
                for (k0, nk, q0, nq), (sct, dpt) in zip(parts, scores[h]):
                    pt, dst = prob_strips(masked, h, sct, dpt, k0, q0)
                    dv_scr[h, k0:k0 + nk, :] += _dot(pt, do_ref[h, q0:q0 + nq, :])
                    dk_scr[h, k0:k0 + nk, :] += _dot(dst, q_ref[h, q0:q0 + nq, :])
                    rows = pl.ds(pl.multiple_of(i * t + q0, t // 2), nq)
                    dq_ref[h, rows, :] += _dot_tn(dst, k_ref[h, k0:k0 + nk, :])

        @pl.when(i > j)
        def _():
            run(False, [(0, t, 0, t)])

        @pl.when(i == j)
        def _():
            run(True, [(0, t // 2, 0, t), (t // 2, t // 2, t // 2, t // 2)])

        @pl.when(i == nb - 1)
        def _():
            dk_ref[...] = dk_scr[...]
            dv_ref[...] = dv_scr[...].astype(BF16)

        if nx:
            @pl.when(jnp.logical_and(pl.program_id(0) == ngroups - 1, pair == npairs - 1))
            def _():
                for cp in ex.copies(ex_in, ex_out, ex_sems):
                    cp.wait()

    qspec = pl.BlockSpec((nh, t, HEAD_PAD), lambda h, p, it, jt: (h, it[p], 0))
    kspec = pl.BlockSpec((nh, t, HEAD_PAD), lambda h, p, it, jt: (h, jt[p], 0))
    dqspec = pl.BlockSpec((nh, s, HEAD_PAD), lambda h, p, it, jt: (h, 0, 0))
    stspec = pl.BlockSpec((nh, 2, t), lambda h, p, it, jt: (h, 0, it[p]))
    out = jax.ShapeDtypeStruct((N_TOK_HEADS, s, HEAD_PAD), F32)
    res = pl.pallas_call(
        body,
        grid_spec=pltpu.PrefetchScalarGridSpec(
            num_scalar_prefetch=2, grid=(ngroups, npairs),
            in_specs=[qspec, kspec, kspec, stspec, qspec] + [ANY] * nx,
            out_specs=[dqspec, kspec, kspec] + [ANY] * nx,
            scratch_shapes=[pltpu.VMEM((nh, t, HEAD_PAD), F32)] * 2 + ex_scratch),
        out_shape=[out, out, jax.ShapeDtypeStruct(out.shape, BF16)] + ex_out_shape,
        name="flash_bwd", compiler_params=_cp(2))(itab, jtab, q, k, v, stats, do, *ex_arrays)
    return res[:3], res[3:]


def _mem_probs(qp, kp, hh):
    lane = _lane(qp.shape)
    keep = (lane < HEAD_DIM) if hh == 0 else (lane >= HEAD_DIM)
    qh = jnp.where(keep, qp, 0.0).astype(BF16)
    sc = _dot_nt(qh, kp) * (1.0 / math.sqrt(HEAD_DIM))
    e = jnp.exp(sc - jnp.max(sc, axis=-1, keepdims=True))
    return e / jnp.sum(e, axis=-1, keepdims=True), keep


def _gate_mem_fwd(tok, z, memkv, g0, q0, padded, name, ts):
    s = z.shape[0]
    zw = z.shape[1]
    tok_spec = _heads(ts) if padded else _rows(ts, TOK_WIDTH)

    def body(tok_ref, z_ref, mkv_ref, cat_ref, y_ref):
        if padded:
            for p in range(N_TOK_HEADS // 2):
                cat_ref[:, p * LANES:(p + 1) * LANES] = (
                    tok_ref[2 * p] + pltpu.roll(tok_ref[2 * p + 1], HEAD_DIM, 1))
        else:
            cat_ref[:, 0:TOK_WIDTH] = tok_ref[...]
        for pr in range(N_MEM_HEADS // 2):
            sl = slice(pr * LANES, (pr + 1) * LANES)
            qp = z_ref[:, q0 + pr * LANES:q0 + (pr + 1) * LANES]
            kp = mkv_ref[:, sl].astype(BF16)
            vp = mkv_ref[:, MEM_WIDTH + pr * LANES:MEM_WIDTH + (pr + 1) * LANES].astype(BF16)
            outs = []
            for hh in range(2):
                p, _ = _mem_probs(qp, kp, hh)
                outs.append(_dot(p.astype(BF16), vp))
            lane = _lane(outs[0].shape)
            cat_ref[:, TOK_WIDTH + pr * LANES:TOK_WIDTH + (pr + 1) * LANES] = jnp.where(
                lane < HEAD_DIM, outs[0], outs[1])
        gate = z_ref[:, g0:g0 + MIX_WIDTH]
        y_ref[...] = (cat_ref[...] * (gate * _sigmoid(gate))).astype(BF16)

    return pl.pallas_call(
        body, grid=(s // ts,),
        in_specs=[tok_spec, _rows(ts, zw), _full((MEM_LEN, 2 * MEM_WIDTH))],
        out_specs=[_rows(ts, MIX_WIDTH)] * 2,
        out_shape=[jax.ShapeDtypeStruct((s, MIX_WIDTH), F32),
                   jax.ShapeDtypeStruct((s, MIX_WIDTH), BF16)],
        name=name, compiler_params=_cp(1))(tok, z, memkv)


def _gate_mem_bwd(dpre, y, w_t, cat, z, memkv, lse, g0, q0, name, ts):
    s = z.shape[0]
    zw = z.shape[1]
    padded = lse is not None
    gq_w = MIX_WIDTH + MEM_WIDTH

    def body(*refs):
        if padded:
            (dpre_ref, y_ref, wt_ref, cat_ref, z_ref, mkv_ref, lse_ref,
             dzg_ref, dtok_ref, dmkv_ref, dw_ref, st_ref) = refs
        else:
            (dpre_ref, y_ref, wt_ref, cat_ref, z_ref, mkv_ref,
             dzg_ref, dtok_ref, dmkv_ref, dw_ref) = refs

        @pl.when(pl.program_id(0) == 0)
        def _():
            dmkv_ref[...] = jnp.zeros_like(dmkv_ref)
            dw_ref[...] = jnp.zeros_like(dw_ref)

        dpb = dpre_ref[...].astype(BF16)
        dy_ = _dot(dpb, wt_ref[...])
        dw_ref[...] += _dot_tn(y_ref[...], dpb)
        gate = z_ref[:, g0:g0 + MIX_WIDTH]
        sg = _sigmoid(gate)
        dzg_ref[:, 0:MIX_WIDTH] = (dy_ * cat_ref[...]
                                   * (sg * (1.0 + gate * (1.0 - sg)))).astype(BF16)
        dcat = dy_ * (gate * sg)
        if padded:
            low = _lane((ts, LANES)) < HEAD_DIM
            for p in range(N_TOK_HEADS // 2):
                d = dcat[:, p * LANES:(p + 1) * LANES]
                prod = d * cat_ref[:, p * LANES:(p + 1) * LANES]
                first = jnp.sum(jnp.where(low, prod, 0.0), axis=-1, keepdims=True)
                second = jnp.sum(jnp.where(low, 0.0, prod), axis=-1, keepdims=True)
                dtok_ref[2 * p] = jnp.where(low, d, 0.0).astype(BF16)
                dtok_ref[2 * p + 1] = jnp.where(low, pltpu.roll(d, HEAD_DIM, 1), 0.0).astype(BF16)
                for hh, delta in ((2 * p, first), (2 * p + 1, second)):
                    st_ref[hh, 0:1, :] = lse_ref[hh]
                    st_ref[hh, 1:2, :] = jnp.broadcast_to(delta, (ts, LANES)).T[0:1, :]
        else:
            dtok_ref[...] = dcat[:, 0:TOK_WIDTH]
        for pr in range(N_MEM_HEADS // 2):
            sl = slice(pr * LANES, (pr + 1) * LANES)
            vsl = slice(MEM_WIDTH + pr * LANES, MEM_WIDTH + (pr + 1) * LANES)
            qp = z_ref[:, q0 + pr * LANES:q0 + (pr + 1) * LANES]
            qpb = qp.astype(BF16)
            kp = mkv_ref[:, sl].astype(BF16)
            vp = mkv_ref[:, vsl].astype(BF16)
            dmo = dcat[:, TOK_WIDTH + pr * LANES:TOK_WIDTH + (pr + 1) * LANES]
            dqp = None
            for hh in range(2):
                p, keep = _mem_probs(qp, kp, hh)
                do_h = jnp.where(keep, dmo, 0.0).astype(BF16)
                dmkv_ref[:, vsl] += _dot_tn(p.astype(BF16), do_h)
                dp = _dot_nt(do_h, vp)
                ds = (p * (dp - jnp.sum(dp * p, axis=-1, keepdims=True))
                      * (1.0 / math.sqrt(HEAD_DIM))).astype(BF16)
                dqh = jnp.where(keep, _dot(ds, kp), 0.0)
                dqp = dqh if dqp is None else dqp + dqh
                dkh = _dot_tn(ds, qpb)
                klane = _lane(dkh.shape)
                kkeep = (klane < HEAD_DIM) if hh == 0 else (klane >= HEAD_DIM)
                dmkv_ref[:, sl] += jnp.where(kkeep, dkh, 0.0)
            dzg_ref[:, MIX_WIDTH + pr * LANES:MIX_WIDTH + (pr + 1) * LANES] = dqp.astype(BF16)

    in_specs = [_rows(ts, D_MODEL), _rows(ts, MIX_WIDTH), _full((D_MODEL, MIX_WIDTH)),
                _rows(ts, MIX_WIDTH), _rows(ts, zw), _full((MEM_LEN, 2 * MEM_WIDTH))]
    out_specs = [_rows(ts, gq_w), _heads(ts) if padded else _rows(ts, TOK_WIDTH),
                 _full((MEM_LEN, 2 * MEM_WIDTH)), _full((MIX_WIDTH, D_MODEL))]
    heads_shape = (N_TOK_HEADS, s, HEAD_PAD)
    out_shape = [jax.ShapeDtypeStruct((s, gq_w), BF16),
                 jax.ShapeDtypeStruct(heads_shape, BF16) if padded
                 else jax.ShapeDtypeStruct((s, TOK_WIDTH), F32),
                 jax.ShapeDtypeStruct((MEM_LEN, 2 * MEM_WIDTH), F32),
                 jax.ShapeDtypeStruct((MIX_WIDTH, D_MODEL), F32)]
    args = [dpre, y, w_t, cat, z, memkv]
    if padded:
        in_specs.append(pl.BlockSpec((N_TOK_HEADS, 1, ts), lambda i: (0, 0, i)))
        out_specs.append(pl.BlockSpec((N_TOK_HEADS, 2, ts), lambda i: (0, 0, i)))
        out_shape.append(jax.ShapeDtypeStruct((N_TOK_HEADS, 2, s), F32))
        args.append(lse)
    return pl.pallas_call(
        body, grid=(s // ts,), in_specs=in_specs, out_specs=out_specs, out_shape=out_shape,
        name=name, compiler_params=_cp(1))(*args)


def _ln_stats(pre):
    mu = jnp.mean(pre, axis=-1, keepdims=True)
    d = pre - mu
    rstd = lax.rsqrt(jnp.mean(d * d, axis=-1, keepdims=True) + NORM_EPS)
    return d * rstd, rstd


def _ln_bwd(dh, xhat, rstd, g):
    dxh = dh * g
    return rstd * (dxh - jnp.mean(dxh, axis=-1, keepdims=True)
                   - xhat * jnp.mean(dxh * xhat, axis=-1, keepdims=True))


def _outproj_ln_fwd(y, w, h, g, b, tgt, name, ts):
    s = y.shape[0]
    with_loss = tgt is not None

    def body(*refs):
        if with_loss:
            y_ref, w_ref, h_ref, g_ref, b_ref, t_ref, dpre_ref, dgb_ref, loss_ref = refs
        else:
            y_ref, w_ref, h_ref, g_ref, b_ref, pre_ref, out_ref = refs
        pre = ALPHA * h_ref[...] + _dot(y_ref[...].astype(BF16), w_ref[...])
        xhat, rstd = _ln_stats(pre)
        hout = xhat * g_ref[...] + b_ref[...]
        if with_loss:
            @pl.when(pl.program_id(0) == 0)
            def _():
                loss_ref[...] = jnp.zeros_like(loss_ref)
                dgb_ref[...] = jnp.zeros_like(dgb_ref)
            err = hout - t_ref[...]
            loss_ref[...] += 0.5 * jnp.sum(jnp.mean(err * err, axis=-1, keepdims=True))
            dh = err * (1.0 / D_MODEL)
            dpre_ref[...] = _ln_bwd(dh, xhat, rstd, g_ref[...])
            dgb_ref[0:1, :] += jnp.sum(dh * xhat, axis=0, keepdims=True)
            dgb_ref[1:2, :] += jnp.sum(dh, axis=0, keepdims=True)
        else:
            pre_ref[...] = pre
            out_ref[...] = hout

    act = jax.ShapeDtypeStruct((s, D_MODEL), F32)
    in_specs = [_rows(ts, MIX_WIDTH), _full((MIX_WIDTH, D_MODEL)), _rows(ts, D_MODEL),
                _full((1, D_MODEL)), _full((1, D_MODEL))]
    args = [y, w, h, g, b]
    if with_loss:
        in_specs.append(_rows(ts, D_MODEL))
        out_specs = [_rows(ts, D_MODEL), _full((SUBLANES, D_MODEL)), _full((SUBLANES, LANES))]
        out_shape = [act, jax.ShapeDtypeStruct((SUBLANES, D_MODEL), F32),
                     jax.ShapeDtypeStruct((SUBLANES, LANES), F32)]
        args.append(tgt)
    else:
        out_specs = [_rows(ts, D_MODEL)] * 2
        out_shape = [act, act]
    return pl.pallas_call(
        body, grid=(s // ts,), in_specs=in_specs, out_specs=out_specs, out_shape=out_shape,
        name=name, compiler_params=_cp(1))(*args)


def _linear_bwd(x, dys, offs, w_t, resid, ln, name, ts):
    s, kdim = x.shape
    n = w_t.shape[0]
    widths = [d.shape[1] for d in dys]
    npieces = len(dys)
    with_ln = ln is not None

    def body(*refs):
        x_ref = refs[0]
        dy_refs = refs[1:1 + npieces]
        if with_ln:
            wt_ref, r_ref, pre_ref, g_ref, dx_ref, dw_ref, dgb_ref = refs[1 + npieces:]
        else:
            wt_ref, r_ref, dx_ref, dw_ref = refs[1 + npieces:]

        @pl.when(pl.program_id(0) == 0)
        def _():
            dw_ref[...] = jnp.zeros_like(dw_ref)
            if with_ln:
                dgb_ref[...] = jnp.zeros_like(dgb_ref)

        xb = x_ref[...].astype(BF16)
        dx = ALPHA * r_ref[...]
        for dy_ref, off, wd in zip(dy_refs, offs, widths):
            dyb = dy_ref[...].astype(BF16)
            dx = dx + _dot(dyb, wt_ref[off:off + wd, :])
            dw_ref[:, off:off + wd] += _dot_tn(xb, dyb)
        if with_ln:
            xhat, rstd = _ln_stats(pre_ref[...])
            dx_ref[...] = _ln_bwd(dx, xhat, rstd, g_ref[...])
            dgb_ref[0:1, :] += jnp.sum(dx * xhat, axis=0, keepdims=True)
            dgb_ref[1:2, :] += jnp.sum(dx, axis=0, keepdims=True)
        else:
            dx_ref[...] = dx

    in_specs = ([_rows(ts, kdim)] + [_rows(ts, wd) for wd in widths]
                + [_full((n, kdim)), _rows(ts, kdim)])
    out_specs = [_rows(ts, kdim), _full((kdim, n))]
    out_shape = [jax.ShapeDtypeStruct((s, kdim), F32), jax.ShapeDtypeStruct((kdim, n), F32)]
    args = [x, *dys, w_t, resid]
    if with_ln:
        in_specs += [_rows(ts, kdim), _full((1, kdim))]
        out_specs.append(_full((SUBLANES, kdim)))
        out_shape.append(jax.ShapeDtypeStruct((SUBLANES, kdim), F32))
        args += list(ln)
    return pl.pallas_call(
        body, grid=(s // ts,), in_specs=in_specs, out_specs=out_specs, out_shape=out_shape,
        name=name, compiler_params=_cp(1))(*args)


def _wgrad_small(x, dy, name):
    def body(x_ref, dy_ref, dw_ref):
        dw_ref[...] = _dot_tn(x_ref[...].astype(BF16), dy_ref[...].astype(BF16))

    return pl.pallas_call(
        body, out_shape=jax.ShapeDtypeStruct((x.shape[1], dy.shape[1]), F32),
        name=name, compiler_params=pltpu.CompilerParams(vmem_limit_bytes=VMEM_LIMIT))(x, dy)


def _shift_down(u, carry8, k):
    if k == 0:
        return u
    rolled = pltpu.roll(u, k, 0)
    row = lax.broadcasted_iota(jnp.int32, carry8.shape, 0)
    top = jnp.where(row < k, pltpu.roll(carry8, k, 0), rolled[0:SUBLANES])
    return jnp.concatenate([top, rolled[SUBLANES:]], axis=0)


def _shift_up(u, carry8, k):
    if k == 0:
        return u
    n = u.shape[0]
    rolled = pltpu.roll(u, n - k, 0)
    row = lax.broadcasted_iota(jnp.int32, carry8.shape, 0)
    bot = jnp.where(row >= SUBLANES - k, pltpu.roll(carry8, SUBLANES - k, 0),
                    rolled[n - SUBLANES:])
    return jnp.concatenate([rolled[:n - SUBLANES], bot], axis=0)


def _neg_expm1(t):
    e = jnp.exp(t)
    em1 = e - 1.0
    safe = jnp.where(e == 1.0, 1.0, jnp.log(e))
    return -jnp.where(e == 1.0, t, jnp.where(em1 == -1.0, -1.0, em1 * t / safe))


def _lru_gates(u, carry8, cw_ref, vec_ref, wr_ref, wi_ref):
    taps = [_shift_down(u, carry8, k) for k in range(CONV_W)]
    xc = vec_ref[0:1, :] + cw_ref[3:4, :] * u
    for k in range(1, CONV_W):
        xc = xc + cw_ref[3 - k:4 - k, :] * taps[k]
    xb = xc.astype(BF16)
    r = _sigmoid(_dot(xb, wr_ref[...]) + vec_ref[1:2, :])
    ig = _sigmoid(_dot(xb, wi_ref[...]) + vec_ref[2:3, :])
    nlam = -vec_ref[3:4, :]
    softplus = jnp.maximum(nlam, 0.0) + jnp.log(1.0 + jnp.exp(-jnp.abs(nlam)))
    cneg = -LRU_C * softplus
    log_a = cneg * r
    a = jnp.exp(log_a)
    sq = jnp.sqrt(_neg_expm1(2.0 * log_a))
    return xc, r, ig, cneg, a, sq, taps


def _chained_scan(a_ref, b_ref, out_ref, cum_scr, x_in):
    rows_total, w = a_ref.shape
    nseg = SCAN_SEGMENTS
    seg = rows_total // nseg

    def step(t, carry):
        xs, ps = carry
        new_x, new_p = [], []
        for sg in range(nseg):
            row = pl.ds(sg * seg + t, 1)
            a = a_ref[row, :]
            x = a * xs[sg] + b_ref[row, :]
            out_ref[row, :] = x
            new_x.append(x)
            if sg > 0:
                p = a * ps[sg - 1]
                cum_scr[row, :] = p
                new_p.append(p)
        return tuple(new_x), tuple(new_p)

    zero, one = jnp.zeros((1, w), F32), jnp.ones((1, w), F32)
    xs, _ = lax.fori_loop(0, seg, step, ((x_in,) + (zero,) * (nseg - 1), (one,) * (nseg - 1)))
    x_prev = xs[0]
    for sg in range(1, nseg):
        rows = slice(sg * seg, (sg + 1) * seg)
        out_ref[rows, :] = out_ref[rows, :] + cum_scr[rows, :] * x_prev
        x_prev = out_ref[(sg + 1) * seg - 1:(sg + 1) * seg, :]
    return x_prev


def _lru_fwd(z1, cw8, vec8, wr, wi, ts):
    s = z1.shape[0]

    def body(u_ref, cw_ref, vec_ref, wr_ref, wi_ref, hs_ref,
             cu_scr, ch_scr, a_scr, gx_scr, cum_scr):
        @pl.when(pl.program_id(0) == 0)
        def _():
            cu_scr[...] = jnp.zeros_like(cu_scr)
            ch_scr[...] = jnp.zeros_like(ch_scr)

        u = u_ref[...]
        xc, _, ig, _, a, sq, _ = _lru_gates(u, cu_scr[...], cw_ref, vec_ref, wr_ref, wi_ref)
        a_scr[...] = a
        gx_scr[...] = sq * (ig * xc)
        ch_scr[0:1, :] = _chained_scan(a_scr, gx_scr, hs_ref, cum_scr, ch_scr[0:1, :])
        cu_scr[...] = u[ts - SUBLANES:, :]

    w = TOK_WIDTH
    return pl.pallas_call(
        body, grid=(s // ts,),
        in_specs=[_rows(ts, w), _full((SUBLANES, w)), _full((SUBLANES, w)),
                  _full((w, w)), _full((w, w))],
        out_specs=_rows(ts, w),
        out_shape=jax.ShapeDtypeStruct((s, w), F32),
        scratch_shapes=[pltpu.VMEM((SUBLANES, w), F32), pltpu.VMEM((SUBLANES, w), F32)]
                       + [pltpu.VMEM((ts, w), F32)] * 3,
        name="lru_fwd", compiler_params=_cp(1))(z1, cw8, vec8, wr, wi)


def _lru_bwd(z1, dhs, hs, cw8, vec8, wr, wi, wr_t, wi_t, ts):
    s = z1.shape[0]
    nb = s // ts
    w = TOK_WIDTH
    tiles = ts // SUBLANES

    def body(u_ref, up_ref, dhs_ref, hs_ref, hsp_ref, cw_ref, vec_ref, wr_ref, wi_ref,
             wrt_ref, wit_ref, du_ref, dwr_ref, dwi_ref, dvec_ref,
             cc_scr, cd_scr, a_scr, dh_scr):
        i = pl.program_id(0)

        @pl.when(i == 0)
        def _():
            cc_scr[...] = jnp.zeros_like(cc_scr)
            cd_scr[...] = jnp.zeros_like(cd_scr)
            dwr_ref[...] = jnp.zeros_like(dwr_ref)
            dwi_ref[...] = jnp.zeros_like(dwi_ref)
            dvec_ref[...] = jnp.zeros_like(dvec_ref)

        u = u_ref[...]
        first = i == nb - 1
        carry8 = jnp.where(first, 0.0, up_ref[...])
        xc, r, ig, cneg, a, sq, taps = _lru_gates(u, carry8, cw_ref, vec_ref, wr_ref, wi_ref)
        a_scr[...] = a

        def step(n, c):
            t = ts - 1 - n
            dh = dhs_ref[pl.ds(t, 1), :] + c
            dh_scr[pl.ds(t, 1), :] = dh
            return a_scr[pl.ds(t, 1), :] * dh

        cc_scr[0:1, :] = lax.fori_loop(0, ts, step, cc_scr[0:1, :])
        dh = dh_scr[...]
        hprev = _shift_down(hs_ref[...], jnp.where(first, 0.0, hsp_ref[...]), 1)
        ix = ig * xc
        dix = dh * sq
        dlog_a = dh * hprev * a - (dh * ix) * (a * a) / sq
        dpr = (dlog_a * cneg) * r * (1.0 - r)
        dpi = (dix * xc) * ig * (1.0 - ig)
        dprb, dpib = dpr.astype(BF16), dpi.astype(BF16)
        xb = xc.astype(BF16)
        dwr_ref[...] += _dot_tn(xb, dprb)
        dwi_ref[...] += _dot_tn(xb, dpib)
        dxc = dix * ig + _dot(dprb, wrt_ref[...]) + _dot(dpib, wit_ref[...])
        for k in range(CONV_W):
            dvec_ref[3 - k:4 - k, :] += jnp.sum(dxc * taps[k], axis=0, keepdims=True)
        dvec_ref[4:5, :] += jnp.sum(dxc, axis=0, keepdims=True)
        dvec_ref[5:6, :] += jnp.sum(dpr, axis=0, keepdims=True)
        dvec_ref[6:7, :] += jnp.sum(dpi, axis=0, keepdims=True)
        dvec_ref[7:8, :] += (jnp.sum(dlog_a * r, axis=0, keepdims=True)
                             * (LRU_C * _sigmoid(-vec_ref[3:4, :])))
        nxt = cd_scr[...]
        du = cw_ref[3:4, :] * dxc
        for k in range(1, CONV_W):
            du = du + cw_ref[3 - k:4 - k, :] * _shift_up(dxc, nxt, k)
        du_ref[...] = du.astype(BF16)
        cd_scr[...] = dxc[0:SUBLANES, :]

    rev = lambda i: (nb - 1 - i, 0)
    prev8 = lambda i: (jnp.maximum((nb - 1 - i) * tiles - 1, 0), 0)
    blk = pl.BlockSpec((ts, w), rev)
    before = pl.BlockSpec((SUBLANES, w), prev8)
    scr = pltpu.VMEM((ts, w), F32)
    return pl.pallas_call(
        body, grid=(nb,),
        in_specs=[blk, before, blk, blk, before,
                  _full((SUBLANES, w)), _full((SUBLANES, w)),
                  _full((w, w)), _full((w, w)), _full((w, w)), _full((w, w))],
        out_specs=[blk, _full((w, w)), _full((w, w)), _full((SUBLANES, w))],
        out_shape=[jax.ShapeDtypeStruct((s, w), BF16), jax.ShapeDtypeStruct((w, w), F32),
                   jax.ShapeDtypeStruct((w, w), F32), jax.ShapeDtypeStruct((SUBLANES, w), F32)],
        scratch_shapes=[pltpu.VMEM((SUBLANES, w), F32), pltpu.VMEM((SUBLANES, w), F32),
                        scr, scr],
        name="lru_bwd", compiler_params=_cp(1))(
            z1, z1, dhs, hs, hs, cw8, vec8, wr, wi, wr_t, wi_t)


def _adamw(parts, w, m, v, name):
    n = len(parts)
    rows_per = parts[0].shape[1]

    def body(*refs):
        p_refs = refs[:n]
        w_ref, m_ref, v_ref, g_ref, d_ref, nm_ref, nv_ref = refs[n:]
        for l, p_ref in enumerate(p_refs):
            rows = slice(l * rows_per, (l + 1) * rows_per)
            g = p_ref[0].astype(F32)
            for dev in range(1, N_DEV):
                g = g + p_ref[dev].astype(F32)
            g_ref[rows, :] = g
            nm = ADAM_B1 * m_ref[rows, :] + (1.0 - ADAM_B1) * g
            nv = ADAM_B2 * v_ref[rows, :] + (1.0 - ADAM_B2) * (g * g)
            m_hat = nm / (1.0 - ADAM_B1 ** ADAM_STEP)
            v_hat = nv / (1.0 - ADAM_B2 ** ADAM_STEP)
            d_ref[rows, :] = -ADAM_LR * (m_hat / (jnp.sqrt(v_hat) + ADAM_EPS)
                                         + ADAM_WD * w_ref[rows, :])
            nm_ref[rows, :] = nm
            nv_ref[rows, :] = nv

    out = jax.ShapeDtypeStruct(w.shape, F32)
    return pl.pallas_call(
        body, out_shape=[out] * 4, name=name,
        compiler_params=pltpu.CompilerParams(vmem_limit_bytes=VMEM_LIMIT))(*parts, w, m, v)


ANY = pl.BlockSpec(memory_space=pl.ANY)
MESH = pl.DeviceIdType.MESH


def _slot(p):
    return 4 * p[0] + 2 * p[1] + p[2]


def _allgather(xs):
    n = len(xs)

    def body(*refs):
        x_refs, o_refs = refs[:n], refs[n:2 * n]
        send_sems, recv_sems, local_sems = refs[2 * n:]
        x, y, c = lax.axis_index("x"), lax.axis_index("y"), lax.axis_index("c")
        me, sibling = (x, y, c), (x, y, 1 - c)
        chips = [(1 - x, y), (x, 1 - y), (1 - x, 1 - y)]

        def copy(a, k, block, to, from_input=False):
            dst = o_refs[a].at[_slot(block)]
            return pltpu.make_async_remote_copy(
                src_ref=x_refs[a] if from_input else dst, dst_ref=dst,
                send_sem=send_sems.at[a, k], recv_sem=recv_sems.at[a, k],
                device_id=to, device_id_type=MESH)

        mine = [pltpu.make_async_copy(x_refs[a], o_refs[a].at[_slot(me)], local_sems.at[a])
                for a in range(n)]
        for cp in mine:
            cp.start()
        first = []
        for a in range(n):
            first.append(copy(a, 0, me, sibling, True))
            first += [copy(a, 1 + j, me, (*chip, c), True) for j, chip in enumerate(chips)]
        for cp in first:
            cp.start()
        passed = []
        for j, chip in enumerate(chips):
            for a in range(n):
                copy(a, 1 + j, (*chip, c), me).wait_recv()
                cp = copy(a, 4 + j, (*chip, c), sibling)
                cp.start()
                passed.append(cp)
        for a in range(n):
            copy(a, 0, sibling, me).wait_recv()
            for j, chip in enumerate(chips):
                copy(a, 4 + j, (*chip, 1 - c), me).wait_recv()
        for cp in first + passed:
            cp.wait_send()
        for cp in mine:
            cp.wait()

    return pl.pallas_call(
        body,
        out_shape=[jax.ShapeDtypeStruct((N_DEV,) + t.shape, t.dtype) for t in xs],
        in_specs=[ANY] * n, out_specs=[ANY] * n,
        scratch_shapes=[pltpu.SemaphoreType.DMA((n, 7)), pltpu.SemaphoreType.DMA((n, 7)),
                        pltpu.SemaphoreType.DMA((n,))],
        name="allgather_weights")(*xs)


class _Exchange:
    def __init__(self, arrays, kinds):
        self.arrays, self.kinds, self.n = list(arrays), list(kinds), len(arrays)
        self.shapes = [self._part_shape(a, k) for a, k in zip(arrays, kinds)]
        self.out_shape = [jax.ShapeDtypeStruct((N_DEV,) + shp, a.dtype)
                          for shp, a in zip(self.shapes, arrays)]
        self.scratch = [pltpu.SemaphoreType.DMA((self.n, N_DEV - 1)),
                        pltpu.SemaphoreType.DMA((self.n, N_DEV - 1)),
                        pltpu.SemaphoreType.DMA((self.n,))]

    @staticmethod
    def _part_shape(arr, kind):
        if kind == "chunks":
            return arr.shape[1:]
        if kind == "cols":
            return (arr.shape[0], arr.shape[1] // N_DEV)
        if kind == "rows":
            return (arr.shape[0] // N_DEV, arr.shape[1])
        return arr.shape

    def copies(self, in_refs, out_refs, sems):
        send_sems, recv_sems, local_sems = sems
        x, y, c = lax.axis_index("x"), lax.axis_index("y"), lax.axis_index("c")
        me = _slot((x, y, c))

        def part(a, dev):
            ref, kind, shp = in_refs[a], self.kinds[a], self.shapes[a]
            if kind == "chunks":
                return ref.at[dev]
            if kind == "cols":
                return ref.at[:, pl.ds(pl.multiple_of(dev * shp[1], LANES), shp[1])]
            if kind == "rows":
                return ref.at[pl.ds(pl.multiple_of(dev * shp[0], SUBLANES), shp[0]), :]
            return ref

        cps = [pltpu.make_async_copy(part(a, me), out_refs[a].at[me], local_sems.at[a])
               for a in range(self.n)]
        for rel in range(1, N_DEV):
            peer = (x ^ (rel >> 2), y ^ ((rel >> 1) & 1), c ^ (rel & 1))
            for a in range(self.n):
                cps.append(pltpu.make_async_remote_copy(
                    src_ref=part(a, _slot(peer)), dst_ref=out_refs[a].at[me],
                    send_sem=send_sems.at[a, rel - 1], recv_sem=recv_sems.at[a, rel - 1],
                    device_id=peer, device_id_type=MESH))
        return cps


def _exchange_grads(arrays, kinds, name):
    ex = _Exchange(arrays, kinds)
    n = ex.n

    def body(*refs):
        cps = ex.copies(refs[:n], refs[n:2 * n], refs[2 * n:])
        for cp in cps:
            cp.start()
        for cp in cps:
            cp.wait()

    return pl.pallas_call(
        body, out_shape=ex.out_shape, in_specs=[ANY] * n, out_specs=[ANY] * n,
        scratch_shapes=ex.scratch, name=name)(*arrays)


BIG = [("mla_w_in", (D_MODEL, MLA_IN), 1), ("mla_w_uq", (Q_LORA, N_TOK_HEADS * QK_DIM), 1),
       ("mla_w_ukv", (KV_LORA, N_TOK_HEADS * 2 * HEAD_DIM), 1), ("lru_w_in", (D_MODEL, LRU_IN), 1),
       ("w_mem_kv", (2, D_MODEL, 2 * MEM_WIDTH), 1), ("w_out", (2, MIX_WIDTH, D_MODEL), 1)]
SMALL = [("lru_conv_w", (CONV_W, TOK_WIDTH), 1), ("lru_conv_b", (TOK_WIDTH,), 0),
         ("lru_b_rgate", (TOK_WIDTH,), 0), ("lru_b_igate", (TOK_WIDTH,), 0),
         ("lru_lambda", (TOK_WIDTH,), 0)]
REPL = [("mla_q_norm", (Q_LORA,)), ("mla_kv_norm", (KV_LORA,)),
        ("lru_w_rgate", (N_TOK_HEADS, HEAD_DIM, HEAD_DIM)),
        ("lru_w_igate", (N_TOK_HEADS, HEAD_DIM, HEAD_DIM)),
        ("ln_g", (2, D_MODEL)), ("ln_b", (2, D_MODEL))]


def _shard_shape(shape, axis):
    return tuple(d // N_DEV if a == axis else d for a, d in enumerate(shape))


def _size(shape):
    return math.prod(shape)


BIG_ROWS = sum(_size(s) for _, s, _ in BIG) // N_DEV // LANES
SMALL_ROWS = SUBLANES


def _pack_rows(flat_parts, rows):
    flat = jnp.concatenate([p.reshape(-1) for p in flat_parts])
    return jnp.pad(flat, (0, rows * LANES - flat.shape[0])).reshape(rows, LANES)


def _to_chunks(full, axis):
    shape = full.shape
    split = shape[:axis] + (N_DEV, shape[axis] // N_DEV) + shape[axis + 1:]
    return jnp.moveaxis(full.reshape(split), axis, 0).reshape(N_DEV, -1)


def _from_chunks(chunks, shape, axis):
    sh = _shard_shape(shape, axis)
    t = chunks.reshape((N_DEV,) + sh)
    t = jnp.moveaxis(t, 0, axis)
    return t.reshape(shape)


def _split_flat(flat2d, table):
    out, off = [], 0
    for size in table:
        out.append(flat2d[:, off:off + size])
        off += size
    return out


def _win0_to_padded(w):
    z = lambda n: jnp.zeros((w.shape[0], n), w.dtype)
    return jnp.concatenate([w[:, 0:640], z(KR_LANE), w[:, 640:672],
                            z(LANES - KR_LANE - QK_ROPE), w[:, 672:1952]], axis=1)


def _win0_from_padded(wp):
    k0 = ZA_KR + KR_LANE
    return jnp.concatenate([wp[:, 0:640], wp[:, k0:k0 + QK_ROPE], wp[:, ZA_W:ZP]], axis=1)


def _pad_heads(w, per_head, lo, hi):
    t = w.reshape(w.shape[0], N_TOK_HEADS, per_head)[:, :, lo:hi]
    t = jnp.pad(t, ((0, 0), (0, 0), (0, HEAD_PAD - (hi - lo))))
    return t.reshape(w.shape[0], QKV_PAD)


def _unpad_heads(wp, width):
    return wp.reshape(wp.shape[0], N_TOK_HEADS, HEAD_PAD)[:, :, :width]


def _block_diag(w):
    eye = jnp.eye(N_TOK_HEADS, dtype=w.dtype)
    return (w[:, :, None, :] * eye[:, None, :, None]).reshape(TOK_WIDTH, TOK_WIDTH)


def _diag_blocks(d):
    t = d.reshape(N_TOK_HEADS, HEAD_DIM, N_TOK_HEADS, HEAD_DIM)
    return jnp.stack([t[g, :, g, :] for g in range(N_TOK_HEADS)])


def _rope_tables(positions):
    half = QK_ROPE // 2
    inv_freq = ROPE_THETA ** (-jnp.arange(half, dtype=F32) / half)
    ang = positions.astype(F32)[:, None] * inv_freq
    cos, sin = jnp.cos(ang), jnp.sin(ang)
    s = positions.shape[0]
    one, zero = jnp.ones((s, QK_NOPE), F32), jnp.zeros((s, half), F32)
    tail = jnp.zeros((s, HEAD_PAD - QK_DIM), F32)
    znope = jnp.zeros((s, QK_NOPE), F32)
    c = jnp.concatenate([one, cos, cos, tail], axis=1)
    sa = jnp.concatenate([znope, -sin, zero, tail], axis=1)
    sb = jnp.concatenate([znope, zero, sin, tail], axis=1)
    return c, sa, sb


def _local_step(x, mem, positions, tgt, wts, ts, tatt, early_exchange):
    bf = lambda t: t.astype(BF16)
    win0 = _win0_to_padded(wts["mla_w_in"])
    wuq = _pad_heads(wts["mla_w_uq"], QK_DIM, 0, QK_DIM)
    wukv = jnp.concatenate([_pad_heads(wts["mla_w_ukv"], 2 * HEAD_DIM, 0, QK_NOPE),
                            _pad_heads(wts["mla_w_ukv"], 2 * HEAD_DIM, QK_NOPE, 2 * HEAD_DIM)],
                           axis=1)
    win1 = wts["lru_w_in"]
    wmkv, wout = wts["w_mem_kv"], wts["w_out"]
    gq = wts["mla_q_norm"].reshape(1, Q_LORA)
    gkv = wts["mla_kv_norm"].reshape(1, KV_LORA)
    ln_g, ln_b = wts["ln_g"], wts["ln_b"]
    wr, wi = bf(_block_diag(wts["lru_w_rgate"])), bf(_block_diag(wts["lru_w_igate"]))
    cw8 = jnp.pad(wts["lru_conv_w"], ((0, SUBLANES - CONV_W), (0, 0)))
    vec8 = jnp.pad(jnp.stack([wts["lru_conv_b"], wts["lru_b_rgate"], wts["lru_b_igate"],
                              wts["lru_lambda"]]), ((0, SUBLANES - 4), (0, 0)))
    tabs = _rope_tables(positions)
    tmem = mem.shape[0]

    za0, zg0 = _rowmm(x, win0, [ZA_W, ZG_W], "in_proj0", ts)
    q, k, v = _mla_prep_fwd(za0, tabs, gq, gkv, wuq, wukv, ts)
    o, lse = _flash_fwd(q, k, v, tatt, FWD_HEADS)
    mkv0, = _rowmm(mem, wmkv[0], [2 * MEM_WIDTH], "mem_kv0", tmem)
    cat0, y0 = _gate_mem_fwd(o, zg0, mkv0, 0, MIX_WIDTH, True, "gate_mem_fwd0", ts)
    del o
    pre0, h1 = _outproj_ln_fwd(y0, wout[0], x, ln_g[0:1], ln_b[0:1], None, "outproj_ln_fwd0", ts)
    u1, zg1 = _rowmm(h1, win1, [ZA_W, ZG_W], "in_proj1", ts)
    hs = _lru_fwd(u1, cw8, vec8, wr, wi, ts)
    mkv1, = _rowmm(mem, wmkv[1], [2 * MEM_WIDTH], "mem_kv1", tmem)
    cat1, y1 = _gate_mem_fwd(hs, zg1, mkv1, 0, MIX_WIDTH, False, "gate_mem_fwd1", ts)
    dpre1, dgb1, loss8 = _outproj_ln_fwd(y1, wout[1], h1, ln_g[1:2], ln_b[1:2], tgt,
                                         "outproj_ln_loss", ts)
    loss = loss8[0, 0]

    dzg1, dhs, dmkv1, dwout1 = _gate_mem_bwd(dpre1, y1, wout[1].T, cat1, zg1, mkv1, None,
                                             0, MIX_WIDTH, "gate_mem_bwd1", ts)
    du, dwr, dwi, dvec = _lru_bwd(u1, dhs, hs, cw8, vec8, wr, wi, wr.T, wi.T, ts)
    dpre0, dwin1, dgb0 = _linear_bwd(h1, [du, dzg1], [0, ZA_W], win1.T, dpre1,
                                     (pre0, ln_g[0:1]), "in_proj_bwd1", ts)
    dwmkv1 = _wgrad_small(mem, dmkv1, "mem_kv_bwd1")
    dzg0, do, dmkv0, dwout0, stats = _gate_mem_bwd(dpre0, y0, wout[0].T, cat0, zg0, mkv0, lse,
                                                   0, MIX_WIDTH, "gate_mem_bwd0", ts)
    dwmkv0 = _wgrad_small(mem, dmkv0, "mem_kv_bwd0")
    early = {
        "lru_w_in": dwin1,
        "lru_small": dvec,
        "lru_w_rgate": _diag_blocks(dwr).reshape(TOK_WIDTH, HEAD_DIM),
        "lru_w_igate": _diag_blocks(dwi).reshape(TOK_WIDTH, HEAD_DIM),
        "w_mem_kv": [dwmkv0, dwmkv1],
        "w_out": [dwout0, dwout1],
    }
    (dq, dk, dv), got_early = _flash_bwd(q, k, v, stats, do, tatt, BWD_HEADS,
                                         early_exchange(early))
    dza, dzk, dwuq_p, dwukv_p, dg = _mla_prep_bwd(za0, dq, dk, dv, tabs, gq, gkv,
                                                  wuq.T, wukv.T, ts)
    gx, dwin0_p = _linear_bwd(x, [dza, dzk, dzg0], [ZA_CQ, ZA_KR, ZA_W], win0.T, dpre0,
                              None, "in_proj_bwd0", ts)

    dwukv = jnp.concatenate([_unpad_heads(dwukv_p[:, :QKV_PAD], HEAD_DIM),
                             _unpad_heads(dwukv_p[:, QKV_PAD:], HEAD_DIM)], axis=2)
    zrow = jnp.zeros((1, D_MODEL), F32)
    gains = jnp.pad(dg[0:1], ((0, 0), (0, D_MODEL - Q_LORA - KV_LORA)))
    small_repl = jnp.concatenate([dgb0[0:2], dgb1[0:2], gains,
                                  loss * jnp.ones((1, D_MODEL), F32), zrow, zrow], axis=0)
    late = {
        "mla_w_in": _win0_from_padded(dwin0_p),
        "mla_w_uq": _unpad_heads(dwuq_p, QK_DIM).reshape(Q_LORA, N_TOK_HEADS * QK_DIM),
        "mla_w_ukv": dwukv.reshape(KV_LORA, N_TOK_HEADS * 2 * HEAD_DIM),
        "small_repl": small_repl,
    }
    return gx, early, got_early, late


WEIGHT_ORDER = ["mla_w_in", "mla_q_norm", "mla_w_uq", "mla_kv_norm", "mla_w_ukv", "lru_w_in",
                "lru_conv_w", "lru_conv_b", "lru_w_rgate", "lru_b_rgate", "lru_w_igate",
                "lru_b_igate", "lru_lambda", "w_mem_kv", "w_out", "ln_g", "ln_b"]


def kernel(x, mem, positions, mla_w_in, mla_q_norm, mla_w_uq, mla_kv_norm, mla_w_ukv, lru_w_in, lru_conv_w, lru_conv_b, lru_w_rgate, lru_b_rgate, lru_w_igate, lru_b_igate, lru_lambda, w_mem_kv, w_out, ln_g, ln_b, loss_target, m_mla_w_in, m_mla_q_norm, m_mla_w_uq, m_mla_kv_norm, m_mla_w_ukv, m_lru_w_in, m_lru_conv_w, m_lru_conv_b, m_lru_w_rgate, m_lru_b_rgate, m_lru_w_igate, m_lru_b_igate, m_lru_lambda, m_w_mem_kv, m_w_out, m_ln_g, m_ln_b, v_mla_w_in, v_mla_q_norm, v_mla_w_uq, v_mla_kv_norm, v_mla_w_ukv, v_lru_w_in, v_lru_conv_w, v_lru_conv_b, v_lru_w_rgate, v_lru_b_rgate, v_lru_w_igate, v_lru_b_igate, v_lru_lambda, v_w_mem_kv, v_w_out, v_ln_g, v_ln_b):
    w_in = dict(mla_w_in=mla_w_in, mla_q_norm=mla_q_norm, mla_w_uq=mla_w_uq,
                mla_kv_norm=mla_kv_norm, mla_w_ukv=mla_w_ukv, lru_w_in=lru_w_in,
                lru_conv_w=lru_conv_w, lru_conv_b=lru_conv_b, lru_w_rgate=lru_w_rgate,
                lru_b_rgate=lru_b_rgate, lru_w_igate=lru_w_igate, lru_b_igate=lru_b_igate,
                lru_lambda=lru_lambda, w_mem_kv=w_mem_kv, w_out=w_out, ln_g=ln_g, ln_b=ln_b)
    m_in = dict(mla_w_in=m_mla_w_in, mla_q_norm=m_mla_q_norm, mla_w_uq=m_mla_w_uq,
                mla_kv_norm=m_mla_kv_norm, mla_w_ukv=m_mla_w_ukv, lru_w_in=m_lru_w_in,
                lru_conv_w=m_lru_conv_w, lru_conv_b=m_lru_conv_b, lru_w_rgate=m_lru_w_rgate,
                lru_b_rgate=m_lru_b_rgate, lru_w_igate=m_lru_w_igate, lru_b_igate=m_lru_b_igate,
                lru_lambda=m_lru_lambda, w_mem_kv=m_w_mem_kv, w_out=m_w_out, ln_g=m_ln_g,
                ln_b=m_ln_b)
    v_in = dict(mla_w_in=v_mla_w_in, mla_q_norm=v_mla_q_norm, mla_w_uq=v_mla_w_uq,
                mla_kv_norm=v_mla_kv_norm, mla_w_ukv=v_mla_w_ukv, lru_w_in=v_lru_w_in,
                lru_conv_w=v_lru_conv_w, lru_conv_b=v_lru_conv_b, lru_w_rgate=v_lru_w_rgate,
                lru_b_rgate=v_lru_b_rgate, lru_w_igate=v_lru_w_igate, lru_b_igate=v_lru_b_igate,
                lru_lambda=v_lru_lambda, w_mem_kv=v_w_mem_kv, w_out=v_w_out, ln_g=v_ln_g,
                ln_b=v_ln_b)
    s = x.shape[1]
    ts = min(ROW_BLOCK, s)
    tatt = min(ATT_BLOCK, s)
    big_sizes = [_size(sh) // N_DEV for _, sh, _ in BIG]
    small_sizes = [_size(sh) // N_DEV for _, sh, _ in SMALL]

    big_local = _pack_rows([w_in[n] for n, _, _ in BIG], BIG_ROWS).astype(BF16)
    small_local = _pack_rows([w_in[n] for n, _, _ in SMALL], SMALL_ROWS)
    big_all, small_all = _allgather([big_local, small_local])
    wts = {}
    for (n, sh, ax), part in zip(BIG, _split_flat(big_all.reshape(N_DEV, -1), big_sizes)):
        wts[n] = _from_chunks(part, sh, ax)
    for (n, sh, ax), part in zip(SMALL, _split_flat(small_all.reshape(N_DEV, -1), small_sizes)):
        wts[n] = _from_chunks(part, sh, ax)
    for n, sh in REPL:
        wts[n] = w_in[n].reshape(sh)

    def early_exchange(g):
        small_chunks = jnp.moveaxis(g["lru_small"].reshape(SUBLANES, N_DEV, -1), 1, 0)
        sends = [(g["lru_w_in"], "cols"),
                 (g["w_mem_kv"][0], "rows"), (g["w_mem_kv"][1], "rows"),
                 (g["w_out"][0], "rows"), (g["w_out"][1], "rows"),
                 (small_chunks, "chunks"), (g["lru_w_rgate"], "all"), (g["lru_w_igate"], "all")]
        return _Exchange([a for a, _ in sends], [k for _, k in sends])

    gx, _, got_early, late = _local_step(x[0], mem[0], positions[0], loss_target[0], wts,
                                         ts, tatt, early_exchange)

    def chunked(name, shape):
        w = shape[1] // N_DEV
        return _to_chunks(late[name], 1).reshape(N_DEV, shape[0], w).astype(BF16)

    got_late = _exchange_grads(
        [chunked("mla_w_in", (D_MODEL, MLA_IN)),
         chunked("mla_w_uq", (Q_LORA, N_TOK_HEADS * QK_DIM)),
         chunked("mla_w_ukv", (KV_LORA, N_TOK_HEADS * 2 * HEAD_DIM)), late["small_repl"]],
        ["chunks", "chunks", "chunks", "all"], "exchange_grads")
    got = list(got_late[:3]) + list(got_early) + [got_late[3]]

    def small_sharded(d):
        return jnp.concatenate([d["lru_conv_w"].reshape(CONV_W, -1), d["lru_conv_b"],
                                d["lru_b_rgate"], d["lru_b_igate"], d["lru_lambda"]], axis=0)

    def small_replicated(d):
        gains = jnp.concatenate([d["mla_q_norm"], d["mla_kv_norm"]], axis=1)
        gains = jnp.pad(gains, ((0, 0), (0, D_MODEL - gains.shape[1])))
        return jnp.concatenate([d["ln_g"][0:1], d["ln_b"][0:1], d["ln_g"][1:2], d["ln_b"][1:2],
                                gains, jnp.zeros((3, D_MODEL), F32)], axis=0)

    def flat2(d, name):
        t = d[name]
        return t.reshape(-1, t.shape[-1])

    def update(parts, view, name):
        return _adamw(parts, view(w_in), view(m_in), view(v_in), "adamw_" + name)

    res = {}
    for idx, name in [(0, "mla_w_in"), (1, "mla_w_uq"), (2, "mla_w_ukv"), (3, "lru_w_in"),
                      (9, "lru_w_rgate"), (10, "lru_w_igate")]:
        res[name] = update([got[idx]], functools.partial(flat2, name=name), name)
    res["w_mem_kv"] = update([got[4], got[5]], functools.partial(flat2, name="w_mem_kv"),
                             "w_mem_kv")
    res["w_out"] = update([got[6], got[7]], functools.partial(flat2, name="w_out"), "w_out")
    res_ss = update([got[8]], small_sharded, "small_sharded")
    res_sr = update([got[11]], small_replicated, "small_replicated")
    loss = res_sr[0][5, 0]

    result = [loss, gx.reshape(x.shape)]
    for kind in range(4):
        ss, sr = res_ss[kind], res_sr[kind]
        out = {n: res[n][kind].reshape(w_in[n].shape) for n in res}
        out["lru_conv_w"] = ss[0:CONV_W].reshape(w_in["lru_conv_w"].shape)
        out["lru_conv_b"], out["lru_b_rgate"] = ss[4:5], ss[5:6]
        out["lru_b_igate"], out["lru_lambda"] = ss[6:7], ss[7:8]
        out["ln_g"] = jnp.concatenate([sr[0:1], sr[2:3]], axis=0)
        out["ln_b"] = jnp.concatenate([sr[1:2], sr[3:4]], axis=0)
        out["mla_q_norm"] = sr[4:5, 0:Q_LORA]
        out["mla_kv_norm"] = sr[4:5, Q_LORA:Q_LORA + KV_LORA]
        result += [out[n] for n in WEIGHT_ORDER]
    return tuple(result)
```

```python
import functools
import math

import jax
import jax.numpy as jnp
from jax import lax
from jax.experimental import pallas as pl
from jax.experimental.pallas import tpu as pltpu

F32 = jnp.float32
BF16 = jnp.bfloat16

D_MODEL = 1024
MEM_LEN = 256
HEAD_DIM = 64
N_TOK_HEADS = 12
N_MEM_HEADS = 4
TOK_WIDTH = 768
MEM_WIDTH = 256
MIX_WIDTH = 1024
Q_LORA = 384
KV_LORA = 256
QK_NOPE = 64
QK_ROPE = 32
QK_DIM = 96
ROPE_THETA = 10000.0
CONV_W = 4
LRU_C = 8.0
ALPHA = (2.0 * 2) ** 0.25
NORM_EPS = 1e-6
MLA_IN = 1952
LRU_IN = 2048
ADAM_LR = 0.001
ADAM_B1 = 0.9
ADAM_B2 = 0.999
ADAM_EPS = 1e-08
ADAM_WD = 0.01
ADAM_STEP = 10

N_DEV = 8
LANES = 128
SUBLANES = 8
HEAD_PAD = 128
QKV_PAD = N_TOK_HEADS * HEAD_PAD
ZP = 2048
ZA_W = TOK_WIDTH
ZG_W = MIX_WIDTH + MEM_WIDTH
ZA_CQ, ZA_CKV, ZA_KR = 0, 384, 640
KR_LANE = 64

ROW_BLOCK = 512
ATT_BLOCK = 512
LOOKAHEAD = 3
FWD_HEADS = 12
BWD_HEADS = 4
VMEM_LIMIT = 56 * 1024 * 1024
NEG_BIG = -1e30
STRIP = 32
SCAN_SEGMENTS = 4
LOG2E = math.log2(math.e)


def _cp(n_axes):
    return pltpu.CompilerParams(dimension_semantics=("arbitrary",) * n_axes,
                                vmem_limit_bytes=VMEM_LIMIT)


def _dot(a, b):
    return jnp.dot(a, b, preferred_element_type=F32)


def _dot_nt(a, b):
    return lax.dot_general(a, b, (((1,), (1,)), ((), ())), preferred_element_type=F32)


def _dot_tn(a, b):
    return lax.dot_general(a, b, (((0,), (0,)), ((), ())), preferred_element_type=F32)


def _sigmoid(t):
    return 1.0 / (1.0 + jnp.exp(-t))


def _lane(shape):
    return lax.broadcasted_iota(jnp.int32, shape, len(shape) - 1)


def _full(shape):
    nd = len(shape)
    return pl.BlockSpec(shape, lambda *_: (0,) * nd)


def _rows(ts, width, col=0):
    return pl.BlockSpec((ts, width), lambda i: (i, col))


def _heads(ts):
    return pl.BlockSpec((N_TOK_HEADS, ts, HEAD_PAD), lambda i: (0, i, 0))


def _rowmm(x, w, widths, name, ts):
    s, k = x.shape
    n = w.shape[1]
    offs = [sum(widths[:a]) for a in range(len(widths))]

    def body(x_ref, w_ref, *o_refs):
        res = _dot(x_ref[...].astype(BF16), w_ref[...])
        for o_ref, off, wd in zip(o_refs, offs, widths):
            o_ref[...] = res[:, off:off + wd]

    return pl.pallas_call(
        body, grid=(s // ts,),
        in_specs=[_rows(ts, k), _full((k, n))],
        out_specs=[_rows(ts, wd) for wd in widths],
        out_shape=[jax.ShapeDtypeStruct((s, wd), F32) for wd in widths],
        name=name, compiler_params=_cp(1))(x, w)


def _rms_parts(t):
    rs = lax.rsqrt(jnp.mean(t * t, axis=-1, keepdims=True) + NORM_EPS)
    return rs


def _rope(t, c, sa, sb):
    return t * c + pltpu.roll(t, LANES - 16, 1) * sa + pltpu.roll(t, 16, 1) * sb


def _rope_t(d, c, sa, sb):
    return d * c + pltpu.roll(d * sa, 16, 1) + pltpu.roll(d * sb, LANES - 16, 1)


def _mla_prep_fwd(x, win, tabs, gq, gkv, wuq, wukv, ts):
    s = x.shape[0]

    def body(x_ref, win_ref, c_ref, sa_ref, sb_ref, gq_ref, gkv_ref, wuq_ref, wukv_ref,
             z_ref, zg_ref, q_ref, k_ref, v_ref):
        zfull = _dot(x_ref[...].astype(BF16), win_ref[...])
        z_ref[...] = zfull[:, 0:ZA_W]
        zg_ref[...] = zfull[:, ZA_W:ZP]
        cq = zfull[:, ZA_CQ:ZA_CQ + Q_LORA]
        ckv = zfull[:, ZA_CKV:ZA_CKV + KV_LORA]
        kr = zfull[:, ZA_KR:ZA_KR + LANES]
        cqn = cq * _rms_parts(cq) * gq_ref[...]
        ckvn = ckv * _rms_parts(ckv) * gkv_ref[...]
        q = _dot(cqn.astype(BF16), wuq_ref[...])
        kv = _dot(ckvn.astype(BF16), wukv_ref[...])
        c, sa, sb = c_ref[...], sa_ref[...], sb_ref[...]
        krope = _rope(kr, c, sa, sb)
        pad_lane = _lane((ts, HEAD_PAD)) >= HEAD_DIM
        for h in range(N_TOK_HEADS):
            sl = slice(h * HEAD_PAD, (h + 1) * HEAD_PAD)
            q_ref[h] = _rope(q[:, sl], c, sa, sb).astype(BF16)
            k_ref[h] = (kv[:, sl] + krope).astype(BF16)
            vh = kv[:, QKV_PAD + h * HEAD_PAD:QKV_PAD + (h + 1) * HEAD_PAD]
            v_ref[h] = jnp.where(pad_lane, 1.0, vh).astype(BF16)

    out = jax.ShapeDtypeStruct((N_TOK_HEADS, s, HEAD_PAD), BF16)
    return pl.pallas_call(
        body, grid=(s // ts,),
        in_specs=[_rows(ts, D_MODEL), _full((D_MODEL, ZP)),
                  _rows(ts, LANES), _rows(ts, LANES), _rows(ts, LANES),
                  _full((1, Q_LORA)), _full((1, KV_LORA)),
                  _full((Q_LORA, QKV_PAD)), _full((KV_LORA, 2 * QKV_PAD))],
        out_specs=[_rows(ts, ZA_W), _rows(ts, ZG_W)] + [_heads(ts)] * 3,
        out_shape=[jax.ShapeDtypeStruct((s, ZA_W), F32), jax.ShapeDtypeStruct((s, ZG_W), F32),
                   out, out, out],
        name="mla_prep_fwd", compiler_params=_cp(1))(x, win, *tabs, gq, gkv, wuq, wukv)


def _mla_prep_bwd(z0, dq, dk, dv, tabs, gq, gkv, wuq_t, wukv_t, ts):
    s = z0.shape[0]

    def body(z_ref, dq_ref, dk_ref, dv_ref, c_ref, sa_ref, sb_ref, gq_ref, gkv_ref,
             wuqt_ref, wukvt_ref, dza_ref, dzk_ref, dwuq_ref, dwukv_ref, dg_ref):
        @pl.when(pl.program_id(0) == 0)
        def _():
            dwuq_ref[...] = jnp.zeros_like(dwuq_ref)
            dwukv_ref[...] = jnp.zeros_like(dwukv_ref)
            dg_ref[...] = jnp.zeros_like(dg_ref)

        cq = z_ref[:, ZA_CQ:ZA_CQ + Q_LORA]
        ckv = z_ref[:, ZA_CKV:ZA_CKV + KV_LORA]
        rq, rkv = _rms_parts(cq), _rms_parts(ckv)
        gq_, gkv_ = gq_ref[...], gkv_ref[...]
        cqn = (cq * rq * gq_).astype(BF16)
        ckvn = (ckv * rkv * gkv_).astype(BF16)
        c, sa, sb = c_ref[...], sa_ref[...], sb_ref[...]
        dqp, dksum = [], None
        for h in range(N_TOK_HEADS):
            dqp.append(_rope_t(dq_ref[h], c, sa, sb))
            dksum = dk_ref[h] if dksum is None else dksum + dk_ref[h]
        dqp = jnp.concatenate(dqp, axis=1).astype(BF16)
        lane = _lane(dksum.shape)
        dzk_ref[...] = jnp.where((lane >= KR_LANE) & (lane < KR_LANE + QK_ROPE),
                                 _rope_t(dksum, c, sa, sb), 0.0).astype(BF16)
        dkv = jnp.concatenate([dk_ref[h].astype(BF16) for h in range(N_TOK_HEADS)]
                              + [dv_ref[h] for h in range(N_TOK_HEADS)], axis=1)
        dcqn = _dot(dqp, wuqt_ref[...])
        dckvn = _dot(dkv, wukvt_ref[...])
        dwuq_ref[...] += _dot_tn(cqn, dqp)
        dwukv_ref[...] += _dot_tn(ckvn, dkv)
        dg_ref[0:1, 0:Q_LORA] += jnp.sum(dcqn * cq * rq, axis=0, keepdims=True)
        dg_ref[0:1, Q_LORA:Q_LORA + KV_LORA] += jnp.sum(dckvn * ckv * rkv, axis=0, keepdims=True)
        wq = dcqn * gq_
        wkv = dckvn * gkv_
        dcq = rq * wq - cq * (rq * rq * rq) * jnp.mean(wq * cq, axis=-1, keepdims=True)
        dckv = rkv * wkv - ckv * (rkv * rkv * rkv) * jnp.mean(wkv * ckv, axis=-1, keepdims=True)
        dza_ref[:, 0:Q_LORA] = dcq.astype(BF16)
        dza_ref[:, Q_LORA:Q_LORA + KV_LORA] = dckv.astype(BF16)

    na = Q_LORA + KV_LORA
    return pl.pallas_call(
        body, grid=(s // ts,),
        in_specs=[_rows(ts, ZA_W), _heads(ts), _heads(ts), _heads(ts),
                  _rows(ts, LANES), _rows(ts, LANES), _rows(ts, LANES),
                  _full((1, Q_LORA)), _full((1, KV_LORA)),
                  _full((QKV_PAD, Q_LORA)), _full((2 * QKV_PAD, KV_LORA))],
        out_specs=[_rows(ts, na), _rows(ts, LANES), _full((Q_LORA, QKV_PAD)),
                   _full((KV_LORA, 2 * QKV_PAD)), _full((SUBLANES, na))],
        out_shape=[jax.ShapeDtypeStruct((s, na), BF16), jax.ShapeDtypeStruct((s, LANES), BF16),
                   jax.ShapeDtypeStruct((Q_LORA, QKV_PAD), F32),
                   jax.ShapeDtypeStruct((KV_LORA, 2 * QKV_PAD), F32),
                   jax.ShapeDtypeStruct((SUBLANES, na), F32)],
        name="mla_prep_bwd", compiler_params=_cp(1))(
            z0, dq, dk, dv, *tabs, gq, gkv, wuq_t, wukv_t)


def _causal_pairs(nb, by_key):
    if by_key:
        pairs = [(i, j) for j in range(nb) for i in range(j, nb)]
    else:
        pairs = [(i, j) for i in range(nb) for j in range(i + 1)]
    return (jnp.array([p[0] for p in pairs], jnp.int32),
            jnp.array([p[1] for p in pairs], jnp.int32))


def _flash_fwd(q, k, v, t, nh):
    s = q.shape[1]
    itab, jtab = _causal_pairs(s // t, False)
    c2 = LOG2E / math.sqrt(QK_DIM)

    def body(it_ref, jt_ref, q_ref, k_ref, v_ref, o_ref, lse_ref, m_scr, acc_scr):
        pair = pl.program_id(1)
        i, j = it_ref[pair], jt_ref[pair]

        @pl.when(j == 0)
        def _():
            m_scr[...] = jnp.full_like(m_scr, NEG_BIG)
            acc_scr[...] = jnp.zeros_like(acc_scr)

        def softmax_strips(masked, hs, sc, row0):
            ps, als = [], []
            for r0 in range(0, sc.shape[0], STRIP):
                rows = slice(row0 + r0, row0 + r0 + STRIP)
                ch = [sc[r0:r0 + STRIP, n * LANES:(n + 1) * LANES] * c2
                      for n in range(sc.shape[1] // LANES)]
                if masked:
                    rr = row0 + r0 + lax.broadcasted_iota(jnp.int32, (STRIP, LANES), 0)
                    cc = lax.broadcasted_iota(jnp.int32, (STRIP, LANES), 1)
                    ch = [jnp.where(cc + n * LANES <= rr, c_, NEG_BIG) for n, c_ in enumerate(ch)]
                mx = ch[0]
                for c_ in ch[1:]:
                    mx = jnp.maximum(mx, c_)
                m_prev = m_scr[hs, rows, :]
                m_next = jnp.maximum(m_prev, jnp.max(mx, axis=-1, keepdims=True))
                ps.append(jnp.concatenate(
                    [jnp.exp2(c_ - m_next).astype(BF16) for c_ in ch], axis=1))
                als.append(jnp.exp2(m_prev - m_next))
                m_scr[hs, rows, :] = m_next
            return jnp.concatenate(ps, axis=0), jnp.concatenate(als, axis=0)

        def run(masked, parts):
            def scores_of(hs):
                return [_dot_nt(q_ref[hs, r0:r0 + nr, :], k_ref[hs, 0:nk, :])
                        for r0, nr, nk in parts]

            ahead = min(LOOKAHEAD, nh)
            scores = [scores_of(hs) for hs in range(ahead)]
            for hs in range(nh):
                if hs + ahead < nh:
                    scores.append(scores_of(hs + ahead))
                for (r0, nr, nk), sc in zip(parts, scores[hs]):
                    p, alpha = softmax_strips(masked, hs, sc, r0)
                    acc_scr[hs, r0:r0 + nr, :] = (alpha * acc_scr[hs, r0:r0 + nr, :]
                                                  + _dot(p, v_ref[hs, 0:nk, :]))

        @pl.when(j < i)
        def _():
            run(False, [(0, t, t)])

        @pl.when(j == i)
        def _():
            run(True, [(0, t, t)])
            for h in range(nh):
                acc = acc_scr[h]
                l = acc[:, HEAD_DIM:HEAD_DIM + 1]
                o_ref[h] = jnp.where(_lane(acc.shape) < HEAD_DIM, acc / l, 0.0)
                lse_ref[h] = (m_scr[h] + jnp.log2(l)).T[0:1, :]

    qspec = pl.BlockSpec((nh, t, HEAD_PAD), lambda h, p, it, jt: (h, it[p], 0))
    kspec = pl.BlockSpec((nh, t, HEAD_PAD), lambda h, p, it, jt: (h, jt[p], 0))
    lspec = pl.BlockSpec((nh, 1, t), lambda h, p, it, jt: (h, 0, it[p]))
    out = jax.ShapeDtypeStruct((N_TOK_HEADS, s, HEAD_PAD), F32)
    return pl.pallas_call(
        body,
        grid_spec=pltpu.PrefetchScalarGridSpec(
            num_scalar_prefetch=2, grid=(N_TOK_HEADS // nh, itab.shape[0]),
            in_specs=[qspec, kspec, kspec], out_specs=[qspec, lspec],
            scratch_shapes=[pltpu.VMEM((nh, t, HEAD_PAD), F32)] * 2),
        out_shape=[out, jax.ShapeDtypeStruct((N_TOK_HEADS, 1, s), F32)],
        name="flash_fwd", compiler_params=_cp(2))(itab, jtab, q, k, v)


def _flash_bwd(q, k, v, stats, do, t, nh, ex):
    s = q.shape[1]
    nb = s // t
    itab, jtab = _causal_pairs(nb, True)
    npairs = itab.shape[0]
    ngroups = N_TOK_HEADS // nh
    scale = 1.0 / math.sqrt(QK_DIM)
    c2 = LOG2E * scale
    nx = ex.n if ex is not None else 0
    ex_arrays, ex_out_shape, ex_scratch = (
        (ex.arrays, ex.out_shape, ex.scratch) if ex is not None else ([], [], []))

    def body(it_ref, jt_ref, q_ref, k_ref, v_ref, st_ref, do_ref, *rest):
        ex_in, rest = rest[:nx], rest[nx:]
        dq_ref, dk_ref, dv_ref = rest[:3]
        ex_out, rest = rest[3:3 + nx], rest[3 + nx:]
        dk_scr, dv_scr = rest[:2]
        ex_sems = rest[2:]
        pair = pl.program_id(1)
        i, j = it_ref[pair], jt_ref[pair]
        rows_i = pl.ds(pl.multiple_of(i * t, t), t)

        if nx:
            @pl.when(jnp.logical_and(pl.program_id(0) == 0, pair == 0))
            def _():
                for cp in ex.copies(ex_in, ex_out, ex_sems):
                    cp.start()

        @pl.when(i == j)
        def _():
            dk_scr[...] = jnp.zeros_like(dk_scr)
            dv_scr[...] = jnp.zeros_like(dv_scr)

        @pl.when(j == 0)
        def _():
            dq_ref[:, rows_i, :] = jnp.zeros((nh, t, HEAD_PAD), F32)

        def prob_strips(masked, h, sct, dpt, k0, q0):
            ps, dss = [], []
            for r0 in range(0, sct.shape[0], STRIP):
                rows = slice(r0, r0 + STRIP)
                if masked:
                    kk = k0 + r0 + lax.broadcasted_iota(jnp.int32, (STRIP, LANES), 0)
                    qq = q0 + lax.broadcasted_iota(jnp.int32, (STRIP, LANES), 1)
                pcs, dcs = [], []
                for n in range(sct.shape[1] // LANES):
                    cols = slice(n * LANES, (n + 1) * LANES)
                    qcols = slice(q0 + n * LANES, q0 + (n + 1) * LANES)
                    x = sct[rows, cols] * c2
                    if masked:
                        x = jnp.where(kk <= qq + n * LANES, x, NEG_BIG)
                    p = jnp.exp2(x - st_ref[h, 0:1, qcols])
                    pcs.append(p.astype(BF16))
                    dcs.append((p * (dpt[rows, cols] - st_ref[h, 1:2, qcols]) * scale).astype(BF16))
                ps.append(jnp.concatenate(pcs, axis=1))
                dss.append(jnp.concatenate(dcs, axis=1))
            return jnp.concatenate(ps, axis=0), jnp.concatenate(dss, axis=0)

        def run(masked, parts):
            def scores_of(h):
                return [(_dot_nt(k_ref[h, k0:k0 + nk, :], q_ref[h, q0:q0 + nq, :]),
                         _dot_nt(v_ref[h, k0:k0 + nk, :], do_ref[h, q0:q0 + nq, :]))
                        for k0, nk, q0, nq in parts]

            ahead = min(LOOKAHEAD, nh)
            scores = [scores_of(h) for h in range(ahead)]
            for h in range(nh):
                if h + ahead < nh:
                    scores.append(scores_of(h + ahead))
                for (k0, nk, q0, nq), (sct, dpt) in zip(parts, scores[h]):
                    pt, dst = prob_strips(masked, h, sct, dpt, k0, q0)
                    dv_scr[h, k0:k0 + nk, :] += _dot(pt, do_ref[h, q0:q0 + nq, :])
                    dk_scr[h, k0:k0 + nk, :] += _dot(dst, q_ref[h, q0:q0 + nq, :])
                    rows = pl.ds(pl.multiple_of(i * t + q0, t // 2), nq)
                    dq_ref[h, rows, :] += _dot_tn(dst, k_ref[h, k0:k0 + nk, :])

        @pl.when(i > j)
        def _():
            run(False, [(0, t, 0, t)])

        @pl.when(i == j)
        def _():
            run(True, [(0, t // 2, 0, t), (t // 2, t // 2, t // 2, t // 2)])

        @pl.when(i == nb - 1)
        def _():
            dk_ref[...] = dk_scr[...]
            dv_ref[...] = dv_scr[...].astype(BF16)

        if nx:
            @pl.when(jnp.logical_and(pl.program_id(0) == ngroups - 1, pair == npairs - 1))
            def _():
                for cp in ex.copies(ex_in, ex_out, ex_sems):
                    cp.wait()

    qspec = pl.BlockSpec((nh, t, HEAD_PAD), lambda h, p, it, jt: (h, it[p], 0))
    kspec = pl.BlockSpec((nh, t, HEAD_PAD), lambda h, p, it, jt: (h, jt[p], 0))
    dqspec = pl.BlockSpec((nh, s, HEAD_PAD), lambda h, p, it, jt: (h, 0, 0))
    stspec = pl.BlockSpec((nh, 2, t), lambda h, p, it, jt: (h, 0, it[p]))
    out = jax.ShapeDtypeStruct((N_TOK_HEADS, s, HEAD_PAD), F32)
    res = pl.pallas_call(
        body,
        grid_spec=pltpu.PrefetchScalarGridSpec(
            num_scalar_prefetch=2, grid=(ngroups, npairs),
            in_specs=[qspec, kspec, kspec, stspec, qspec] + [ANY] * nx,
            out_specs=[dqspec, kspec, kspec] + [ANY] * nx,
            scratch_shapes=[pltpu.VMEM((nh, t, HEAD_PAD), F32)] * 2 + ex_scratch),
        out_shape=[out, out, jax.ShapeDtypeStruct(out.shape, BF16)] + ex_out_shape,
        name="flash_bwd", compiler_params=_cp(2))(itab, jtab, q, k, v, stats, do, *ex_arrays)
    return res[:3], res[3:]


def _mem_probs(qp, kp, hh):
    lane = _lane(qp.shape)
    keep = (lane < HEAD_DIM) if hh == 0 else (lane >= HEAD_DIM)
    qh = jnp.where(keep, qp, 0.0).astype(BF16)
    sc = _dot_nt(qh, kp) * (1.0 / math.sqrt(HEAD_DIM))
    e = jnp.exp(sc - jnp.max(sc, axis=-1, keepdims=True))
    return e / jnp.sum(e, axis=-1, keepdims=True), keep


def _gate_mem_fwd(tok, z, memkv, g0, q0, padded, name, ts):
    s = z.shape[0]
    zw = z.shape[1]
    tok_spec = _heads(ts) if padded else _rows(ts, TOK_WIDTH)

    def body(tok_ref, z_ref, mkv_ref, cat_ref, y_ref):
        if padded:
            for p in range(N_TOK_HEADS // 2):
                cat_ref[:, p * LANES:(p + 1) * LANES] = (
                    tok_ref[2 * p] + pltpu.roll(tok_ref[2 * p + 1], HEAD_DIM, 1))
        else:
            cat_ref[:, 0:TOK_WIDTH] = tok_ref[...]
        for pr in range(N_MEM_HEADS // 2):
            sl = slice(pr * LANES, (pr + 1) * LANES)
            qp = z_ref[:, q0 + pr * LANES:q0 + (pr + 1) * LANES]
            kp = mkv_ref[:, sl].astype(BF16)
            vp = mkv_ref[:, MEM_WIDTH + pr * LANES:MEM_WIDTH + (pr + 1) * LANES].astype(BF16)
            outs = []
            for hh in range(2):
                p, _ = _mem_probs(qp, kp, hh)
                outs.append(_dot(p.astype(BF16), vp))
            lane = _lane(outs[0].shape)
            cat_ref[:, TOK_WIDTH + pr * LANES:TOK_WIDTH + (pr + 1) * LANES] = jnp.where(
                lane < HEAD_DIM, outs[0], outs[1])
        gate = z_ref[:, g0:g0 + MIX_WIDTH]
        y_ref[...] = (cat_ref[...] * (gate * _sigmoid(gate))).astype(BF16)

    return pl.pallas_call(
        body, grid=(s // ts,),
        in_specs=[tok_spec, _rows(ts, zw), _full((MEM_LEN, 2 * MEM_WIDTH))],
        out_specs=[_rows(ts, MIX_WIDTH)] * 2,
        out_shape=[jax.ShapeDtypeStruct((s, MIX_WIDTH), F32),
                   jax.ShapeDtypeStruct((s, MIX_WIDTH), BF16)],
        name=name, compiler_params=_cp(1))(tok, z, memkv)


def _gate_mem_bwd(dpre, y, w_t, cat, z, memkv, lse, g0, q0, name, ts):
    s = z.shape[0]
    zw = z.shape[1]
    padded = lse is not None
    gq_w = MIX_WIDTH + MEM_WIDTH

    def body(*refs):
        if padded:
            (dpre_ref, y_ref, wt_ref, cat_ref, z_ref, mkv_ref, lse_ref,
             dzg_ref, dtok_ref, dmkv_ref, dw_ref, st_ref) = refs
        else:
            (dpre_ref, y_ref, wt_ref, cat_ref, z_ref, mkv_ref,
             dzg_ref, dtok_ref, dmkv_ref, dw_ref) = refs

        @pl.when(pl.program_id(0) == 0)
        def _():
            dmkv_ref[...] = jnp.zeros_like(dmkv_ref)
            dw_ref[...] = jnp.zeros_like(dw_ref)

        dpb = dpre_ref[...].astype(BF16)
        dy_ = _dot(dpb, wt_ref[...])
        dw_ref[...] += _dot_tn(y_ref[...], dpb)
        gate = z_ref[:, g0:g0 + MIX_WIDTH]
        sg = _sigmoid(gate)
        dzg_ref[:, 0:MIX_WIDTH] = (dy_ * cat_ref[...]
                                   * (sg * (1.0 + gate * (1.0 - sg)))).astype(BF16)
        dcat = dy_ * (gate * sg)
        if padded:
            low = _lane((ts, LANES)) < HEAD_DIM
            for p in range(N_TOK_HEADS // 2):
                d = dcat[:, p * LANES:(p + 1) * LANES]
                prod = d * cat_ref[:, p * LANES:(p + 1) * LANES]
                first = jnp.sum(jnp.where(low, prod, 0.0), axis=-1, keepdims=True)
                second = jnp.sum(jnp.where(low, 0.0, prod), axis=-1, keepdims=True)
                dtok_ref[2 * p] = jnp.where(low, d, 0.0).astype(BF16)
                dtok_ref[2 * p + 1] = jnp.where(low, pltpu.roll(d, HEAD_DIM, 1), 0.0).astype(BF16)
                for hh, delta in ((2 * p, first), (2 * p + 1, second)):
                    st_ref[hh, 0:1, :] = lse_ref[hh]
                    st_ref[hh, 1:2, :] = jnp.broadcast_to(delta, (ts, LANES)).T[0:1, :]
        else:
            dtok_ref[...] = dcat[:, 0:TOK_WIDTH]
        for pr in range(N_MEM_HEADS // 2):
            sl = slice(pr * LANES, (pr + 1) * LANES)
            vsl = slice(MEM_WIDTH + pr * LANES, MEM_WIDTH + (pr + 1) * LANES)
            qp = z_ref[:, q0 + pr * LANES:q0 + (pr + 1) * LANES]
            qpb = qp.astype(BF16)
            kp = mkv_ref[:, sl].astype(BF16)
            vp = mkv_ref[:, vsl].astype(BF16)
            dmo = dcat[:, TOK_WIDTH + pr * LANES:TOK_WIDTH + (pr + 1) * LANES]
            dqp = None
            for hh in range(2):
                p, keep = _mem_probs(qp, kp, hh)
                do_h = jnp.where(keep, dmo, 0.0).astype(BF16)
                dmkv_ref[:, vsl] += _dot_tn(p.astype(BF16), do_h)
                dp = _dot_nt(do_h, vp)
                ds = (p * (dp - jnp.sum(dp * p, axis=-1, keepdims=True))
                      * (1.0 / math.sqrt(HEAD_DIM))).astype(BF16)
                dqh = jnp.where(keep, _dot(ds, kp), 0.0)
                dqp = dqh if dqp is None else dqp + dqh
                dkh = _dot_tn(ds, qpb)
                klane = _lane(dkh.shape)
                kkeep = (klane < HEAD_DIM) if hh == 0 else (klane >= HEAD_DIM)
                dmkv_ref[:, sl] += jnp.where(kkeep, dkh, 0.0)
            dzg_ref[:, MIX_WIDTH + pr * LANES:MIX_WIDTH + (pr + 1) * LANES] = dqp.astype(BF16)

    in_specs = [_rows(ts, D_MODEL), _rows(ts, MIX_WIDTH), _full((D_MODEL, MIX_WIDTH)),
                _rows(ts, MIX_WIDTH), _rows(ts, zw), _full((MEM_LEN, 2 * MEM_WIDTH))]
    out_specs = [_rows(ts, gq_w), _heads(ts) if padded else _rows(ts, TOK_WIDTH),
                 _full((MEM_LEN, 2 * MEM_WIDTH)), _full((MIX_WIDTH, D_MODEL))]
    heads_shape = (N_TOK_HEADS, s, HEAD_PAD)
    out_shape = [jax.ShapeDtypeStruct((s, gq_w), BF16),
                 jax.ShapeDtypeStruct(heads_shape, BF16) if padded
                 else jax.ShapeDtypeStruct((s, TOK_WIDTH), F32),
                 jax.ShapeDtypeStruct((MEM_LEN, 2 * MEM_WIDTH), F32),
                 jax.ShapeDtypeStruct((MIX_WIDTH, D_MODEL), F32)]
    args = [dpre, y, w_t, cat, z, memkv]
    if padded:
        in_specs.append(pl.BlockSpec((N_TOK_HEADS, 1, ts), lambda i: (0, 0, i)))
        out_specs.append(pl.BlockSpec((N_TOK_HEADS, 2, ts), lambda i: (0, 0, i)))
        out_shape.append(jax.ShapeDtypeStruct((N_TOK_HEADS, 2, s), F32))
        args.append(lse)
    return pl.pallas_call(
        body, grid=(s // ts,), in_specs=in_specs, out_specs=out_specs, out_shape=out_shape,
        name=name, compiler_params=_cp(1))(*args)


def _ln_stats(pre):
    mu = jnp.mean(pre, axis=-1, keepdims=True)
    d = pre - mu
    rstd = lax.rsqrt(jnp.mean(d * d, axis=-1, keepdims=True) + NORM_EPS)
    return d * rstd, rstd


def _ln_bwd(dh, xhat, rstd, g):
    dxh = dh * g
    return rstd * (dxh - jnp.mean(dxh, axis=-1, keepdims=True)
                   - xhat * jnp.mean(dxh * xhat, axis=-1, keepdims=True))


def _outproj_ln_fwd(y, w, h, g, b, tgt, name, ts):
    s = y.shape[0]
    with_loss = tgt is not None

    def body(*refs):
        if with_loss:
            y_ref, w_ref, h_ref, g_ref, b_ref, t_ref, dpre_ref, dgb_ref, loss_ref = refs
        else:
            y_ref, w_ref, h_ref, g_ref, b_ref, pre_ref, out_ref = refs
        pre = ALPHA * h_ref[...] + _dot(y_ref[...].astype(BF16), w_ref[...])
        xhat, rstd = _ln_stats(pre)
        hout = xhat * g_ref[...] + b_ref[...]
        if with_loss:
            @pl.when(pl.program_id(0) == 0)
            def _():
                loss_ref[...] = jnp.zeros_like(loss_ref)
                dgb_ref[...] = jnp.zeros_like(dgb_ref)
            err = hout - t_ref[...]
            loss_ref[...] += 0.5 * jnp.sum(jnp.mean(err * err, axis=-1, keepdims=True))
            dh = err * (1.0 / D_MODEL)
            dpre_ref[...] = _ln_bwd(dh, xhat, rstd, g_ref[...])
            dgb_ref[0:1, :] += jnp.sum(dh * xhat, axis=0, keepdims=True)
            dgb_ref[1:2, :] += jnp.sum(dh, axis=0, keepdims=True)
        else:
            pre_ref[...] = pre
            out_ref[...] = hout

    act = jax.ShapeDtypeStruct((s, D_MODEL), F32)
    in_specs = [_rows(ts, MIX_WIDTH), _full((MIX_WIDTH, D_MODEL)), _rows(ts, D_MODEL),
                _full((1, D_MODEL)), _full((1, D_MODEL))]
    args = [y, w, h, g, b]
    if with_loss:
        in_specs.append(_rows(ts, D_MODEL))
        out_specs = [_rows(ts, D_MODEL), _full((SUBLANES, D_MODEL)), _full((SUBLANES, LANES))]
        out_shape = [act, jax.ShapeDtypeStruct((SUBLANES, D_MODEL), F32),
                     jax.ShapeDtypeStruct((SUBLANES, LANES), F32)]
        args.append(tgt)
    else:
        out_specs = [_rows(ts, D_MODEL)] * 2
        out_shape = [act, act]
    return pl.pallas_call(
        body, grid=(s // ts,), in_specs=in_specs, out_specs=out_specs, out_shape=out_shape,
        name=name, compiler_params=_cp(1))(*args)


def _linear_bwd(x, dys, offs, w_t, resid, ln, name, ts):
    s, kdim = x.shape
    n = w_t.shape[0]
    widths = [d.shape[1] for d in dys]
    npieces = len(dys)
    with_ln = ln is not None

    def body(*refs):
        x_ref = refs[0]
        dy_refs = refs[1:1 + npieces]
        if with_ln:
            wt_ref, r_ref, pre_ref, g_ref, dx_ref, dw_ref, dgb_ref = refs[1 + npieces:]
        else:
            wt_ref, r_ref, dx_ref, dw_ref = refs[1 + npieces:]

        @pl.when(pl.program_id(0) == 0)
        def _():
            dw_ref[...] = jnp.zeros_like(dw_ref)
            if with_ln:
                dgb_ref[...] = jnp.zeros_like(dgb_ref)

        xb = x_ref[...].astype(BF16)
        dx = ALPHA * r_ref[...]
        for dy_ref, off, wd in zip(dy_refs, offs, widths):
            dyb = dy_ref[...].astype(BF16)
            dx = dx + _dot(dyb, wt_ref[off:off + wd, :])
            dw_ref[:, off:off + wd] += _dot_tn(xb, dyb)
        if with_ln:
            xhat, rstd = _ln_stats(pre_ref[...])
            dx_ref[...] = _ln_bwd(dx, xhat, rstd, g_ref[...])
            dgb_ref[0:1, :] += jnp.sum(dx * xhat, axis=0, keepdims=True)
            dgb_ref[1:2, :] += jnp.sum(dx, axis=0, keepdims=True)
        else:
            dx_ref[...] = dx

    in_specs = ([_rows(ts, kdim)] + [_rows(ts, wd) for wd in widths]
                + [_full((n, kdim)), _rows(ts, kdim)])
    out_specs = [_rows(ts, kdim), _full((kdim, n))]
    out_shape = [jax.ShapeDtypeStruct((s, kdim), F32), jax.ShapeDtypeStruct((kdim, n), F32)]
    args = [x, *dys, w_t, resid]
    if with_ln:
        in_specs += [_rows(ts, kdim), _full((1, kdim))]
        out_specs.append(_full((SUBLANES, kdim)))
        out_shape.append(jax.ShapeDtypeStruct((SUBLANES, kdim), F32))
        args += list(ln)
    return pl.pallas_call(
        body, grid=(s // ts,), in_specs=in_specs, out_specs=out_specs, out_shape=out_shape,
        name=name, compiler_params=_cp(1))(*args)


def _wgrad_small(x, dy, name):
    def body(x_ref, dy_ref, dw_ref):
        dw_ref[...] = _dot_tn(x_ref[...].astype(BF16), dy_ref[...].astype(BF16))

    return pl.pallas_call(
        body, out_shape=jax.ShapeDtypeStruct((x.shape[1], dy.shape[1]), F32),
        name=name, compiler_params=pltpu.CompilerParams(vmem_limit_bytes=VMEM_LIMIT))(x, dy)


def _shift_down(u, carry8, k):
    if k == 0:
        return u
    rolled = pltpu.roll(u, k, 0)
    row = lax.broadcasted_iota(jnp.int32, carry8.shape, 0)
    top = jnp.where(row < k, pltpu.roll(carry8, k, 0), rolled[0:SUBLANES])
    return jnp.concatenate([top, rolled[SUBLANES:]], axis=0)


def _shift_up(u, carry8, k):
    if k == 0:
        return u
    n = u.shape[0]
    rolled = pltpu.roll(u, n - k, 0)
    row = lax.broadcasted_iota(jnp.int32, carry8.shape, 0)
    bot = jnp.where(row >= SUBLANES - k, pltpu.roll(carry8, SUBLANES - k, 0),
                    rolled[n - SUBLANES:])
    return jnp.concatenate([rolled[:n - SUBLANES], bot], axis=0)


def _neg_expm1(t):
    e = jnp.exp(t)
    em1 = e - 1.0
    safe = jnp.where(e == 1.0, 1.0, jnp.log(e))
    return -jnp.where(e == 1.0, t, jnp.where(em1 == -1.0, -1.0, em1 * t / safe))


def _lru_gates(u, carry8, cw_ref, vec_ref, wr_ref, wi_ref):
    taps = [_shift_down(u, carry8, k) for k in range(CONV_W)]
    xc = vec_ref[0:1, :] + cw_ref[3:4, :] * u
    for k in range(1, CONV_W):
        xc = xc + cw_ref[3 - k:4 - k, :] * taps[k]
    xb = xc.astype(BF16)
    r = _sigmoid(_dot(xb, wr_ref[...]) + vec_ref[1:2, :])
    ig = _sigmoid(_dot(xb, wi_ref[...]) + vec_ref[2:3, :])
    nlam = -vec_ref[3:4, :]
    softplus = jnp.maximum(nlam, 0.0) + jnp.log(1.0 + jnp.exp(-jnp.abs(nlam)))
    cneg = -LRU_C * softplus
    log_a = cneg * r
    a = jnp.exp(log_a)
    sq = jnp.sqrt(_neg_expm1(2.0 * log_a))
    return xc, r, ig, cneg, a, sq, taps


def _chained_scan(a_ref, b_ref, out_ref, cum_scr, x_in):
    rows_total, w = a_ref.shape
    nseg = SCAN_SEGMENTS
    seg = rows_total // nseg

    def step(t, carry):
        xs, ps = carry
        new_x, new_p = [], []
        for sg in range(nseg):
            row = pl.ds(sg * seg + t, 1)
            a = a_ref[row, :]
            x = a * xs[sg] + b_ref[row, :]
            out_ref[row, :] = x
            new_x.append(x)
            if sg > 0:
                p = a * ps[sg - 1]
                cum_scr[row, :] = p
                new_p.append(p)
        return tuple(new_x), tuple(new_p)

    zero, one = jnp.zeros((1, w), F32), jnp.ones((1, w), F32)
    xs, _ = lax.fori_loop(0, seg, step, ((x_in,) + (zero,) * (nseg - 1), (one,) * (nseg - 1)))
    x_prev = xs[0]
    for sg in range(1, nseg):
        rows = slice(sg * seg, (sg + 1) * seg)
        out_ref[rows, :] = out_ref[rows, :] + cum_scr[rows, :] * x_prev
        x_prev = out_ref[(sg + 1) * seg - 1:(sg + 1) * seg, :]
    return x_prev


def _lru_fwd(x, win, cw8, vec8, wr, wi, ts):
    s = x.shape[0]

    def body(x_ref, win_ref, cw_ref, vec_ref, wr_ref, wi_ref, u_ref, zg_ref, hs_ref,
             cu_scr, ch_scr, a_scr, gx_scr, cum_scr):
        @pl.when(pl.program_id(0) == 0)
        def _():
            cu_scr[...] = jnp.zeros_like(cu_scr)
            ch_scr[...] = jnp.zeros_like(ch_scr)

        zfull = _dot(x_ref[...].astype(BF16), win_ref[...])
        u = zfull[:, 0:ZA_W]
        u_ref[...] = u
        zg_ref[...] = zfull[:, ZA_W:ZP]
        xc, _, ig, _, a, sq, _ = _lru_gates(u, cu_scr[...], cw_ref, vec_ref, wr_ref, wi_ref)
        a_scr[...] = a
        gx_scr[...] = sq * (ig * xc)
        ch_scr[0:1, :] = _chained_scan(a_scr, gx_scr, hs_ref, cum_scr, ch_scr[0:1, :])
        cu_scr[...] = u[ts - SUBLANES:, :]

    w = TOK_WIDTH
    return pl.pallas_call(
        body, grid=(s // ts,),
        in_specs=[_rows(ts, D_MODEL), _full((D_MODEL, ZP)),
                  _full((SUBLANES, w)), _full((SUBLANES, w)), _full((w, w)), _full((w, w))],
        out_specs=[_rows(ts, w), _rows(ts, ZG_W), _rows(ts, w)],
        out_shape=[jax.ShapeDtypeStruct((s, w), F32), jax.ShapeDtypeStruct((s, ZG_W), F32),
                   jax.ShapeDtypeStruct((s, w), F32)],
        scratch_shapes=[pltpu.VMEM((SUBLANES, w), F32), pltpu.VMEM((SUBLANES, w), F32)]
                       + [pltpu.VMEM((ts, w), F32)] * 3,
        name="lru_fwd", compiler_params=_cp(1))(x, win, cw8, vec8, wr, wi)


def _lru_bwd(z1, dhs, hs, cw8, vec8, wr, wi, wr_t, wi_t, ts):
    s = z1.shape[0]
    nb = s // ts
    w = TOK_WIDTH
    tiles = ts // SUBLANES

    def body(u_ref, up_ref, dhs_ref, hs_ref, hsp_ref, cw_ref, vec_ref, wr_ref, wi_ref,
             wrt_ref, wit_ref, du_ref, dwr_ref, dwi_ref, dvec_ref,
             cc_scr, cd_scr, a_scr, dh_scr):
        i = pl.program_id(0)

        @pl.when(i == 0)
        def _():
            cc_scr[...] = jnp.zeros_like(cc_scr)
            cd_scr[...] = jnp.zeros_like(cd_scr)
            dwr_ref[...] = jnp.zeros_like(dwr_ref)
            dwi_ref[...] = jnp.zeros_like(dwi_ref)
            dvec_ref[...] = jnp.zeros_like(dvec_ref)

        u = u_ref[...]
        first = i == nb - 1
        carry8 = jnp.where(first, 0.0, up_ref[...])
        xc, r, ig, cneg, a, sq, taps = _lru_gates(u, carry8, cw_ref, vec_ref, wr_ref, wi_ref)
        a_scr[...] = a

        def step(n, c):
            t = ts - 1 - n
            dh = dhs_ref[pl.ds(t, 1), :] + c
            dh_scr[pl.ds(t, 1), :] = dh
            return a_scr[pl.ds(t, 1), :] * dh

        cc_scr[0:1, :] = lax.fori_loop(0, ts, step, cc_scr[0:1, :])
        dh = dh_scr[...]
        hprev = _shift_down(hs_ref[...], jnp.where(first, 0.0, hsp_ref[...]), 1)
        ix = ig * xc
        dix = dh * sq
        dlog_a = dh * hprev * a - (dh * ix) * (a * a) / sq
        dpr = (dlog_a * cneg) * r * (1.0 - r)
        dpi = (dix * xc) * ig * (1.0 - ig)
        dprb, dpib = dpr.astype(BF16), dpi.astype(BF16)
        xb = xc.astype(BF16)
        dwr_ref[...] += _dot_tn(xb, dprb)
        dwi_ref[...] += _dot_tn(xb, dpib)
        dxc = dix * ig + _dot(dprb, wrt_ref[...]) + _dot(dpib, wit_ref[...])
        for k in range(CONV_W):
            dvec_ref[3 - k:4 - k, :] += jnp.sum(dxc * taps[k], axis=0, keepdims=True)
        dvec_ref[4:5, :] += jnp.sum(dxc, axis=0, keepdims=True)
        dvec_ref[5:6, :] += jnp.sum(dpr, axis=0, keepdims=True)
        dvec_ref[6:7, :] += jnp.sum(dpi, axis=0, keepdims=True)
        dvec_ref[7:8, :] += (jnp.sum(dlog_a * r, axis=0, keepdims=True)
                             * (LRU_C * _sigmoid(-vec_ref[3:4, :])))
        nxt = cd_scr[...]
        du = cw_ref[3:4, :] * dxc
        for k in range(1, CONV_W):
            du = du + cw_ref[3 - k:4 - k, :] * _shift_up(dxc, nxt, k)
        du_ref[...] = du.astype(BF16)
        cd_scr[...] = dxc[0:SUBLANES, :]

    rev = lambda i: (nb - 1 - i, 0)
    prev8 = lambda i: (jnp.maximum((nb - 1 - i) * tiles - 1, 0), 0)
    blk = pl.BlockSpec((ts, w), rev)
    before = pl.BlockSpec((SUBLANES, w), prev8)
    scr = pltpu.VMEM((ts, w), F32)
    return pl.pallas_call(
        body, grid=(nb,),
        in_specs=[blk, before, blk, blk, before,
                  _full((SUBLANES, w)), _full((SUBLANES, w)),
                  _full((w, w)), _full((w, w)), _full((w, w)), _full((w, w))],
        out_specs=[blk, _full((w, w)), _full((w, w)), _full((SUBLANES, w))],
        out_shape=[jax.ShapeDtypeStruct((s, w), BF16), jax.ShapeDtypeStruct((w, w), F32),
                   jax.ShapeDtypeStruct((w, w), F32), jax.ShapeDtypeStruct((SUBLANES, w), F32)],
        scratch_shapes=[pltpu.VMEM((SUBLANES, w), F32), pltpu.VMEM((SUBLANES, w), F32),
                        scr, scr],
        name="lru_bwd", compiler_params=_cp(1))(
            z1, z1, dhs, hs, hs, cw8, vec8, wr, wi, wr_t, wi_t)


def _adamw(parts, w, m, v, name):
    n = len(parts)
    rows_per = parts[0].shape[1]

    def body(*refs):
        p_refs = refs[:n]
        w_ref, m_ref, v_ref, g_ref, d_ref, nm_ref, nv_ref = refs[n:]
        for l, p_ref in enumerate(p_refs):
            rows = slice(l * rows_per, (l + 1) * rows_per)
            g = p_ref[0].astype(F32)
            for dev in range(1, N_DEV):
                g = g + p_ref[dev].astype(F32)
            g_ref[rows, :] = g
            nm = ADAM_B1 * m_ref[rows, :] + (1.0 - ADAM_B1) * g
            nv = ADAM_B2 * v_ref[rows, :] + (1.0 - ADAM_B2) * (g * g)
            m_hat = nm / (1.0 - ADAM_B1 ** ADAM_STEP)
            v_hat = nv / (1.0 - ADAM_B2 ** ADAM_STEP)
            d_ref[rows, :] = -ADAM_LR * (m_hat / (jnp.sqrt(v_hat) + ADAM_EPS)
                                         + ADAM_WD * w_ref[rows, :])
            nm_ref[rows, :] = nm
            nv_ref[rows, :] = nv

    out = jax.ShapeDtypeStruct(w.shape, F32)
    return pl.pallas_call(
        body, out_shape=[out] * 4, name=name,
        compiler_params=pltpu.CompilerParams(vmem_limit_bytes=VMEM_LIMIT))(*parts, w, m, v)


ANY = pl.BlockSpec(memory_space=pl.ANY)
MESH = pl.DeviceIdType.MESH


def _slot(p):
    return 4 * p[0] + 2 * p[1] + p[2]


def _allgather(xs):
    n = len(xs)

    def body(*refs):
        x_refs, o_refs = refs[:n], refs[n:2 * n]
        send_sems, recv_sems, local_sems = refs[2 * n:]
        x, y, c = lax.axis_index("x"), lax.axis_index("y"), lax.axis_index("c")
        me, sibling = (x, y, c), (x, y, 1 - c)
        chips = [(1 - x, y), (x, 1 - y), (1 - x, 1 - y)]

        def copy(a, k, block, to, from_input=False):
            dst = o_refs[a].at[_slot(block)]
            return pltpu.make_async_remote_copy(
                src_ref=x_refs[a] if from_input else dst, dst_ref=dst,
                send_sem=send_sems.at[a, k], recv_sem=recv_sems.at[a, k],
                device_id=to, device_id_type=MESH)

        mine = [pltpu.make_async_copy(x_refs[a], o_refs[a].at[_slot(me)], local_sems.at[a])
                for a in range(n)]
        for cp in mine:
            cp.start()
        first = []
        for a in range(n):
            first.append(copy(a, 0, me, sibling, True))
            first += [copy(a, 1 + j, me, (*chip, c), True) for j, chip in enumerate(chips)]
        for cp in first:
            cp.start()
        passed = []
        for j, chip in enumerate(chips):
            for a in range(n):
                copy(a, 1 + j, (*chip, c), me).wait_recv()
                cp = copy(a, 4 + j, (*chip, c), sibling)
                cp.start()
                passed.append(cp)
        for a in range(n):
            copy(a, 0, sibling, me).wait_recv()
            for j, chip in enumerate(chips):
                copy(a, 4 + j, (*chip, 1 - c), me).wait_recv()
        for cp in first + passed:
            cp.wait_send()
        for cp in mine:
            cp.wait()

    return pl.pallas_call(
        body,
        out_shape=[jax.ShapeDtypeStruct((N_DEV,) + t.shape, t.dtype) for t in xs],
        in_specs=[ANY] * n, out_specs=[ANY] * n,
        scratch_shapes=[pltpu.SemaphoreType.DMA((n, 7)), pltpu.SemaphoreType.DMA((n, 7)),
                        pltpu.SemaphoreType.DMA((n,))],
        name="allgather_weights")(*xs)


class _Exchange:
    def __init__(self, arrays, kinds):
        self.arrays, self.kinds, self.n = list(arrays), list(kinds), len(arrays)
        self.shapes = [self._part_shape(a, k) for a, k in zip(arrays, kinds)]
        self.out_shape = [jax.ShapeDtypeStruct((N_DEV,) + shp, a.dtype)
                          for shp, a in zip(self.shapes, arrays)]
        self.scratch = [pltpu.SemaphoreType.DMA((self.n, N_DEV - 1)),
                        pltpu.SemaphoreType.DMA((self.n, N_DEV - 1)),
                        pltpu.SemaphoreType.DMA((self.n,))]

    @staticmethod
    def _part_shape(arr, kind):
        if kind == "chunks":
            return arr.shape[1:]
        if kind == "cols":
            return (arr.shape[0], arr.shape[1] // N_DEV)
        if kind == "rows":
            return (arr.shape[0] // N_DEV, arr.shape[1])
        return arr.shape

    def copies(self, in_refs, out_refs, sems):
        send_sems, recv_sems, local_sems = sems
        x, y, c = lax.axis_index("x"), lax.axis_index("y"), lax.axis_index("c")
        me = _slot((x, y, c))

        def part(a, dev):
            ref, kind, shp = in_refs[a], self.kinds[a], self.shapes[a]
            if kind == "chunks":
                return ref.at[dev]
            if kind == "cols":
                return ref.at[:, pl.ds(pl.multiple_of(dev * shp[1], LANES), shp[1])]
            if kind == "rows":
                return ref.at[pl.ds(pl.multiple_of(dev * shp[0], SUBLANES), shp[0]), :]
            return ref

        cps = [pltpu.make_async_copy(part(a, me), out_refs[a].at[me], local_sems.at[a])
               for a in range(self.n)]
        for rel in range(1, N_DEV):
            peer = (x ^ (rel >> 2), y ^ ((rel >> 1) & 1), c ^ (rel & 1))
            for a in range(self.n):
                cps.append(pltpu.make_async_remote_copy(
                    src_ref=part(a, _slot(peer)), dst_ref=out_refs[a].at[me],
                    send_sem=send_sems.at[a, rel - 1], recv_sem=recv_sems.at[a, rel - 1],
                    device_id=peer, device_id_type=MESH))
        return cps


def _exchange_grads(arrays, kinds, name):
    ex = _Exchange(arrays, kinds)
    n = ex.n

    def body(*refs):
        cps = ex.copies(refs[:n], refs[n:2 * n], refs[2 * n:])
        for cp in cps:
            cp.start()
        for cp in cps:
            cp.wait()

    return pl.pallas_call(
        body, out_shape=ex.out_shape, in_specs=[ANY] * n, out_specs=[ANY] * n,
        scratch_shapes=ex.scratch, name=name)(*arrays)


BIG = [("mla_w_in", (D_MODEL, MLA_IN), 1), ("mla_w_uq", (Q_LORA, N_TOK_HEADS * QK_DIM), 1),
       ("mla_w_ukv", (KV_LORA, N_TOK_HEADS * 2 * HEAD_DIM), 1), ("lru_w_in", (D_MODEL, LRU_IN), 1),
       ("w_mem_kv", (2, D_MODEL, 2 * MEM_WIDTH), 1), ("w_out", (2, MIX_WIDTH, D_MODEL), 1)]
SMALL = [("lru_conv_w", (CONV_W, TOK_WIDTH), 1), ("lru_conv_b", (TOK_WIDTH,), 0),
         ("lru_b_rgate", (TOK_WIDTH,), 0), ("lru_b_igate", (TOK_WIDTH,), 0),
         ("lru_lambda", (TOK_WIDTH,), 0)]
REPL = [("mla_q_norm", (Q_LORA,)), ("mla_kv_norm", (KV_LORA,)),
        ("lru_w_rgate", (N_TOK_HEADS, HEAD_DIM, HEAD_DIM)),
        ("lru_w_igate", (N_TOK_HEADS, HEAD_DIM, HEAD_DIM)),
        ("ln_g", (2, D_MODEL)), ("ln_b", (2, D_MODEL))]


def _shard_shape(shape, axis):
    return tuple(d // N_DEV if a == axis else d for a, d in enumerate(shape))


def _size(shape):
    return math.prod(shape)


BIG_ROWS = sum(_size(s) for _, s, _ in BIG) // N_DEV // LANES
SMALL_ROWS = SUBLANES


def _pack_rows(flat_parts, rows):
    flat = jnp.concatenate([p.reshape(-1) for p in flat_parts])
    return jnp.pad(flat, (0, rows * LANES - flat.shape[0])).reshape(rows, LANES)


def _to_chunks(full, axis):
    shape = full.shape
    split = shape[:axis] + (N_DEV, shape[axis] // N_DEV) + shape[axis + 1:]
    return jnp.moveaxis(full.reshape(split), axis, 0).reshape(N_DEV, -1)


def _from_chunks(chunks, shape, axis):
    sh = _shard_shape(shape, axis)
    t = chunks.reshape((N_DEV,) + sh)
    t = jnp.moveaxis(t, 0, axis)
    return t.reshape(shape)


def _split_flat(flat2d, table):
    out, off = [], 0
    for size in table:
        out.append(flat2d[:, off:off + size])
        off += size
    return out


def _win0_to_padded(w):
    z = lambda n: jnp.zeros((w.shape[0], n), w.dtype)
    return jnp.concatenate([w[:, 0:640], z(KR_LANE), w[:, 640:672],
                            z(LANES - KR_LANE - QK_ROPE), w[:, 672:1952]], axis=1)


def _win0_from_padded(wp):
    k0 = ZA_KR + KR_LANE
    return jnp.concatenate([wp[:, 0:640], wp[:, k0:k0 + QK_ROPE], wp[:, ZA_W:ZP]], axis=1)


def _pad_heads(w, per_head, lo, hi):
    t = w.reshape(w.shape[0], N_TOK_HEADS, per_head)[:, :, lo:hi]
    t = jnp.pad(t, ((0, 0), (0, 0), (0, HEAD_PAD - (hi - lo))))
    return t.reshape(w.shape[0], QKV_PAD)


def _unpad_heads(wp, width):
    return wp.reshape(wp.shape[0], N_TOK_HEADS, HEAD_PAD)[:, :, :width]


def _block_diag(w):
    eye = jnp.eye(N_TOK_HEADS, dtype=w.dtype)
    return (w[:, :, None, :] * eye[:, None, :, None]).reshape(TOK_WIDTH, TOK_WIDTH)


def _diag_blocks(d):
    t = d.reshape(N_TOK_HEADS, HEAD_DIM, N_TOK_HEADS, HEAD_DIM)
    return jnp.stack([t[g, :, g, :] for g in range(N_TOK_HEADS)])


def _rope_tables(positions):
    half = QK_ROPE // 2
    inv_freq = ROPE_THETA ** (-jnp.arange(half, dtype=F32) / half)
    ang = positions.astype(F32)[:, None] * inv_freq
    cos, sin = jnp.cos(ang), jnp.sin(ang)
    s = positions.shape[0]
    one, zero = jnp.ones((s, QK_NOPE), F32), jnp.zeros((s, half), F32)
    tail = jnp.zeros((s, HEAD_PAD - QK_DIM), F32)
    znope = jnp.zeros((s, QK_NOPE), F32)
    c = jnp.concatenate([one, cos, cos, tail], axis=1)
    sa = jnp.concatenate([znope, -sin, zero, tail], axis=1)
    sb = jnp.concatenate([znope, zero, sin, tail], axis=1)
    return c, sa, sb


def _local_step(x, mem, positions, tgt, wts, ts, tatt, early_exchange):
    bf = lambda t: t.astype(BF16)
    win0 = _win0_to_padded(wts["mla_w_in"])
    wuq = _pad_heads(wts["mla_w_uq"], QK_DIM, 0, QK_DIM)
    wukv = jnp.concatenate([_pad_heads(wts["mla_w_ukv"], 2 * HEAD_DIM, 0, QK_NOPE),
                            _pad_heads(wts["mla_w_ukv"], 2 * HEAD_DIM, QK_NOPE, 2 * HEAD_DIM)],
                           axis=1)
    win1 = wts["lru_w_in"]
    wmkv, wout = wts["w_mem_kv"], wts["w_out"]
    gq = wts["mla_q_norm"].reshape(1, Q_LORA)
    gkv = wts["mla_kv_norm"].reshape(1, KV_LORA)
    ln_g, ln_b = wts["ln_g"], wts["ln_b"]
    wr, wi = bf(_block_diag(wts["lru_w_rgate"])), bf(_block_diag(wts["lru_w_igate"]))
    cw8 = jnp.pad(wts["lru_conv_w"], ((0, SUBLANES - CONV_W), (0, 0)))
    vec8 = jnp.pad(jnp.stack([wts["lru_conv_b"], wts["lru_b_rgate"], wts["lru_b_igate"],
                              wts["lru_lambda"]]), ((0, SUBLANES - 4), (0, 0)))
    tabs = _rope_tables(positions)
    tmem = mem.shape[0]

    za0, zg0, q, k, v = _mla_prep_fwd(x, win0, tabs, gq, gkv, wuq, wukv, ts)
    o, lse = _flash_fwd(q, k, v, tatt, FWD_HEADS)
    mkv0, = _rowmm(mem, wmkv[0], [2 * MEM_WIDTH], "mem_kv0", tmem)
    cat0, y0 = _gate_mem_fwd(o, zg0, mkv0, 0, MIX_WIDTH, True, "gate_mem_fwd0", ts)
    del o
    pre0, h1 = _outproj_ln_fwd(y0, wout[0], x, ln_g[0:1], ln_b[0:1], None, "outproj_ln_fwd0", ts)
    u1, zg1, hs = _lru_fwd(h1, win1, cw8, vec8, wr, wi, ts)
    mkv1, = _rowmm(mem, wmkv[1], [2 * MEM_WIDTH], "mem_kv1", tmem)
    cat1, y1 = _gate_mem_fwd(hs, zg1, mkv1, 0, MIX_WIDTH, False, "gate_mem_fwd1", ts)
    dpre1, dgb1, loss8 = _outproj_ln_fwd(y1, wout[1], h1, ln_g[1:2], ln_b[1:2], tgt,
                                         "outproj_ln_loss", ts)
    loss = loss8[0, 0]

    dzg1, dhs, dmkv1, dwout1 = _gate_mem_bwd(dpre1, y1, wout[1].T, cat1, zg1, mkv1, None,
                                             0, MIX_WIDTH, "gate_mem_bwd1", ts)
    du, dwr, dwi, dvec = _lru_bwd(u1, dhs, hs, cw8, vec8, wr, wi, wr.T, wi.T, ts)
    dpre0, dwin1, dgb0 = _linear_bwd(h1, [du, dzg1], [0, ZA_W], win1.T, dpre1,
                                     (pre0, ln_g[0:1]), "in_proj_bwd1", ts)
    dwmkv1 = _wgrad_small(mem, dmkv1, "mem_kv_bwd1")
    dzg0, do, dmkv0, dwout0, stats = _gate_mem_bwd(dpre0, y0, wout[0].T, cat0, zg0, mkv0, lse,
                                                   0, MIX_WIDTH, "gate_mem_bwd0", ts)
    dwmkv0 = _wgrad_small(mem, dmkv0, "mem_kv_bwd0")
    early = {
        "lru_w_in": dwin1,
        "lru_small": dvec,
        "lru_w_rgate": _diag_blocks(dwr).reshape(TOK_WIDTH, HEAD_DIM),
        "lru_w_igate": _diag_blocks(dwi).reshape(TOK_WIDTH, HEAD_DIM),
        "w_mem_kv": [dwmkv0, dwmkv1],
        "w_out": [dwout0, dwout1],
    }
    (dq, dk, dv), got_early = _flash_bwd(q, k, v, stats, do, tatt, BWD_HEADS,
                                         early_exchange(early))
    dza, dzk, dwuq_p, dwukv_p, dg = _mla_prep_bwd(za0, dq, dk, dv, tabs, gq, gkv,
                                                  wuq.T, wukv.T, ts)
    gx, dwin0_p = _linear_bwd(x, [dza, dzk, dzg0], [ZA_CQ, ZA_KR, ZA_W], win0.T, dpre0,
                              None, "in_proj_bwd0", ts)

    dwukv = jnp.concatenate([_unpad_heads(dwukv_p[:, :QKV_PAD], HEAD_DIM),
                             _unpad_heads(dwukv_p[:, QKV_PAD:], HEAD_DIM)], axis=2)
    zrow = jnp.zeros((1, D_MODEL), F32)
    gains = jnp.pad(dg[0:1], ((0, 0), (0, D_MODEL - Q_LORA - KV_LORA)))
    small_repl = jnp.concatenate([dgb0[0:2], dgb1[0:2], gains,
                                  loss * jnp.ones((1, D_MODEL), F32), zrow, zrow], axis=0)
    late = {
        "mla_w_in": _win0_from_padded(dwin0_p),
        "mla_w_uq": _unpad_heads(dwuq_p, QK_DIM).reshape(Q_LORA, N_TOK_HEADS * QK_DIM),
        "mla_w_ukv": dwukv.reshape(KV_LORA, N_TOK_HEADS * 2 * HEAD_DIM),
        "small_repl": small_repl,
    }
    return gx, early, got_early, late


WEIGHT_ORDER = ["mla_w_in", "mla_q_norm", "mla_w_uq", "mla_kv_norm", "mla_w_ukv", "lru_w_in",
                "lru_conv_w", "lru_conv_b", "lru_w_rgate", "lru_b_rgate", "lru_w_igate",
                "lru_b_igate", "lru_lambda", "w_mem_kv", "w_out", "ln_g", "ln_b"]


def kernel(x, mem, positions, mla_w_in, mla_q_norm, mla_w_uq, mla_kv_norm, mla_w_ukv, lru_w_in, lru_conv_w, lru_conv_b, lru_w_rgate, lru_b_rgate, lru_w_igate, lru_b_igate, lru_lambda, w_mem_kv, w_out, ln_g, ln_b, loss_target, m_mla_w_in, m_mla_q_norm, m_mla_w_uq, m_mla_kv_norm, m_mla_w_ukv, m_lru_w_in, m_lru_conv_w, m_lru_conv_b, m_lru_w_rgate, m_lru_b_rgate, m_lru_w_igate, m_lru_b_igate, m_lru_lambda, m_w_mem_kv, m_w_out, m_ln_g, m_ln_b, v_mla_w_in, v_mla_q_norm, v_mla_w_uq, v_mla_kv_norm, v_mla_w_ukv, v_lru_w_in, v_lru_conv_w, v_lru_conv_b, v_lru_w_rgate, v_lru_b_rgate, v_lru_w_igate, v_lru_b_igate, v_lru_lambda, v_w_mem_kv, v_w_out, v_ln_g, v_ln_b):
    w_in = dict(mla_w_in=mla_w_in, mla_q_norm=mla_q_norm, mla_w_uq=mla_w_uq,
                mla_kv_norm=mla_kv_norm, mla_w_ukv=mla_w_ukv, lru_w_in=lru_w_in,
                lru_conv_w=lru_conv_w, lru_conv_b=lru_conv_b, lru_w_rgate=lru_w_rgate,
                lru_b_rgate=lru_b_rgate, lru_w_igate=lru_w_igate, lru_b_igate=lru_b_igate,
                lru_lambda=lru_lambda, w_mem_kv=w_mem_kv, w_out=w_out, ln_g=ln_g, ln_b=ln_b)
    m_in = dict(mla_w_in=m_mla_w_in, mla_q_norm=m_mla_q_norm, mla_w_uq=m_mla_w_uq,
                mla_kv_norm=m_mla_kv_norm, mla_w_ukv=m_mla_w_ukv, lru_w_in=m_lru_w_in,
                lru_conv_w=m_lru_conv_w, lru_conv_b=m_lru_conv_b, lru_w_rgate=m_lru_w_rgate,
                lru_b_rgate=m_lru_b_rgate, lru_w_igate=m_lru_w_igate, lru_b_igate=m_lru_b_igate,
                lru_lambda=m_lru_lambda, w_mem_kv=m_w_mem_kv, w_out=m_w_out, ln_g=m_ln_g,
                ln_b=m_ln_b)
    v_in = dict(mla_w_in=v_mla_w_in, mla_q_norm=v_mla_q_norm, mla_w_uq=v_mla_w_uq,
                mla_kv_norm=v_mla_kv_norm, mla_w_ukv=v_mla_w_ukv, lru_w_in=v_lru_w_in,
                lru_conv_w=v_lru_conv_w, lru_conv_b=v_lru_conv_b, lru_w_rgate=v_lru_w_rgate,
                lru_b_rgate=v_lru_b_rgate, lru_w_igate=v_lru_w_igate, lru_b_igate=v_lru_b_igate,
                lru_lambda=v_lru_lambda, w_mem_kv=v_w_mem_kv, w_out=v_w_out, ln_g=v_ln_g,
                ln_b=v_ln_b)
    s = x.shape[1]
    ts = min(ROW_BLOCK, s)
    tatt = min(ATT_BLOCK, s)
    big_sizes = [_size(sh) // N_DEV for _, sh, _ in BIG]
    small_sizes = [_size(sh) // N_DEV for _, sh, _ in SMALL]

    big_local = _pack_rows([w_in[n] for n, _, _ in BIG], BIG_ROWS).astype(BF16)
    small_local = _pack_rows([w_in[n] for n, _, _ in SMALL], SMALL_ROWS)
    big_all, small_all = _allgather([big_local, small_local])
    wts = {}
    for (n, sh, ax), part in zip(BIG, _split_flat(big_all.reshape(N_DEV, -1), big_sizes)):
        wts[n] = _from_chunks(part, sh, ax)
    for (n, sh, ax), part in zip(SMALL, _split_flat(small_all.reshape(N_DEV, -1), small_sizes)):
        wts[n] = _from_chunks(part, sh, ax)
    for n, sh in REPL:
        wts[n] = w_in[n].reshape(sh)

    def early_exchange(g):
        small_chunks = jnp.moveaxis(g["lru_small"].reshape(SUBLANES, N_DEV, -1), 1, 0)
        sends = [(g["lru_w_in"], "cols"),
                 (g["w_mem_kv"][0], "rows"), (g["w_mem_kv"][1], "rows"),
                 (g["w_out"][0], "rows"), (g["w_out"][1], "rows"),
                 (small_chunks, "chunks"), (g["lru_w_rgate"], "all"), (g["lru_w_igate"], "all")]
        return _Exchange([a for a, _ in sends], [k for _, k in sends])

    gx, _, got_early, late = _local_step(x[0], mem[0], positions[0], loss_target[0], wts,
                                         ts, tatt, early_exchange)

    def chunked(name, shape):
        w = shape[1] // N_DEV
        return _to_chunks(late[name], 1).reshape(N_DEV, shape[0], w).astype(BF16)

    got_late = _exchange_grads(
        [chunked("mla_w_in", (D_MODEL, MLA_IN)),
         chunked("mla_w_uq", (Q_LORA, N_TOK_HEADS * QK_DIM)),
         chunked("mla_w_ukv", (KV_LORA, N_TOK_HEADS * 2 * HEAD_DIM)), late["small_repl"]],
        ["chunks", "chunks", "chunks", "all"], "exchange_grads")
    got = list(got_late[:3]) + list(got_early) + [got_late[3]]

    def small_sharded(d):
        return jnp.concatenate([d["lru_conv_w"].reshape(CONV_W, -1), d["lru_conv_b"],
                                d["lru_b_rgate"], d["lru_b_igate"], d["lru_lambda"]], axis=0)

    def small_replicated(d):
        gains = jnp.concatenate([d["mla_q_norm"], d["mla_kv_norm"]], axis=1)
        gains = jnp.pad(gains, ((0, 0), (0, D_MODEL - gains.shape[1])))
        return jnp.concatenate([d["ln_g"][0:1], d["ln_b"][0:1], d["ln_g"][1:2], d["ln_b"][1:2],
                                gains, jnp.zeros((3, D_MODEL), F32)], axis=0)

    def flat2(d, name):
        t = d[name]
        return t.reshape(-1, t.shape[-1])

    def update(parts, view, name):
        return _adamw(parts, view(w_in), view(m_in), view(v_in), "adamw_" + name)

    res = {}
    for idx, name in [(0, "mla_w_in"), (1, "mla_w_uq"), (2, "mla_w_ukv"), (3, "lru_w_in"),
                      (9, "lru_w_rgate"), (10, "lru_w_igate")]:
        res[name] = update([got[idx]], functools.partial(flat2, name=name), name)
    res["w_mem_kv"] = update([got[4], got[5]], functools.partial(flat2, name="w_mem_kv"),
                             "w_mem_kv")
    res["w_out"] = update([got[6], got[7]], functools.partial(flat2, name="w_out"), "w_out")
    res_ss = update([got[8]], small_sharded, "small_sharded")
    res_sr = update([got[11]], small_replicated, "small_replicated")
    loss = res_sr[0][5, 0]

    result = [loss, gx.reshape(x.shape)]
    for kind in range(4):
        ss, sr = res_ss[kind], res_sr[kind]
        out = {n: res[n][kind].reshape(w_in[n].shape) for n in res}
        out["lru_conv_w"] = ss[0:CONV_W].reshape(w_in["lru_conv_w"].shape)
        out["lru_conv_b"], out["lru_b_rgate"] = ss[4:5], ss[5:6]
        out["lru_b_igate"], out["lru_lambda"] = ss[6:7], ss[7:8]
        out["ln_g"] = jnp.concatenate([sr[0:1], sr[2:3]], axis=0)
        out["ln_b"] = jnp.concatenate([sr[1:2], sr[3:4]], axis=0)
        out["mla_q_norm"] = sr[4:5, 0:Q_LORA]
        out["mla_kv_norm"] = sr[4:5, Q_LORA:Q_LORA + KV_LORA]
        result += [out[n] for n in WEIGHT_ORDER]
    return tuple(result)
```

```python
import functools
import math

import jax
import jax.numpy as jnp
from jax import lax
from jax.experimental import pallas as pl
from jax.experimental.pallas import tpu as pltpu

F32 = jnp.float32
BF16 = jnp.bfloat16

D_MODEL = 1024
MEM_LEN = 256
HEAD_DIM = 64
N_TOK_HEADS = 12
N_MEM_HEADS = 4
TOK_WIDTH = 768
MEM_WIDTH = 256
MIX_WIDTH = 1024
Q_LORA = 384
KV_LORA = 256
QK_NOPE = 64
QK_ROPE = 32
QK_DIM = 96
ROPE_THETA = 10000.0
CONV_W = 4
LRU_C = 8.0
ALPHA = (2.0 * 2) ** 0.25
NORM_EPS = 1e-6
MLA_IN = 1952
LRU_IN = 2048
ADAM_LR = 0.001
ADAM_B1 = 0.9
ADAM_B2 = 0.999
ADAM_EPS = 1e-08
ADAM_WD = 0.01
ADAM_STEP = 10

N_DEV = 8
LANES = 128
SUBLANES = 8
HEAD_PAD = 128
QKV_PAD = N_TOK_HEADS * HEAD_PAD
ZP = 2048
ZA_W = TOK_WIDTH
ZG_W = MIX_WIDTH + MEM_WIDTH
ZA_CQ, ZA_CKV, ZA_KR = 0, 384, 640
KR_LANE = 64

ROW_BLOCK = 512
ATT_BLOCK = 512
LOOKAHEAD = 3
FWD_HEADS = 12
BWD_HEADS = 4
VMEM_LIMIT = 56 * 1024 * 1024
NEG_BIG = -1e30
STRIP = 32
SCAN_SEGMENTS = 4
LOG2E = math.log2(math.e)


def _cp(n_axes):
    return pltpu.CompilerParams(dimension_semantics=("arbitrary",) * n_axes,
                                vmem_limit_bytes=VMEM_LIMIT)


def _dot(a, b):
    return jnp.dot(a, b, preferred_element_type=F32)


def _dot_nt(a, b):
    return lax.dot_general(a, b, (((1,), (1,)), ((), ())), preferred_element_type=F32)


def _dot_tn(a, b):
    return lax.dot_general(a, b, (((0,), (0,)), ((), ())), preferred_element_type=F32)


def _sigmoid(t):
    return 1.0 / (1.0 + jnp.exp(-t))


def _lane(shape):
    return lax.broadcasted_iota(jnp.int32, shape, len(shape) - 1)


def _full(shape):
    nd = len(shape)
    return pl.BlockSpec(shape, lambda *_: (0,) * nd)


def _rows(ts, width, col=0):
    return pl.BlockSpec((ts, width), lambda i: (i, col))


def _heads(ts):
    return pl.BlockSpec((N_TOK_HEADS, ts, HEAD_PAD), lambda i: (0, i, 0))


def _rowmm(x, w, widths, name, ts):
    s, k = x.shape
    n = w.shape[1]
    offs = [sum(widths[:a]) for a in range(len(widths))]

    def body(x_ref, w_ref, *o_refs):
        res = _dot(x_ref[...].astype(BF16), w_ref[...])
        for o_ref, off, wd in zip(o_refs, offs, widths):
            o_ref[...] = res[:, off:off + wd]

    return pl.pallas_call(
        body, grid=(s // ts,),
        in_specs=[_rows(ts, k), _full((k, n))],
        out_specs=[_rows(ts, wd) for wd in widths],
        out_shape=[jax.ShapeDtypeStruct((s, wd), F32) for wd in widths],
        name=name, compiler_params=_cp(1))(x, w)


def _rms_parts(t):
    rs = lax.rsqrt(jnp.mean(t * t, axis=-1, keepdims=True) + NORM_EPS)
    return rs


def _rope(t, c, sa, sb):
    return t * c + pltpu.roll(t, LANES - 16, 1) * sa + pltpu.roll(t, 16, 1) * sb


def _rope_t(d, c, sa, sb):
    return d * c + pltpu.roll(d * sa, 16, 1) + pltpu.roll(d * sb, LANES - 16, 1)


def _mla_prep_fwd(x, win, tabs, gq, gkv, wuq, wukv, ts):
    s = x.shape[0]

    def body(x_ref, win_ref, c_ref, sa_ref, sb_ref, gq_ref, gkv_ref, wuq_ref, wukv_ref,
             z_ref, zg_ref, q_ref, k_ref, v_ref):
        zfull = _dot(x_ref[...].astype(BF16), win_ref[...])
        z_ref[...] = zfull[:, 0:ZA_W]
        zg_ref[...] = zfull[:, ZA_W:ZP]
        cq = zfull[:, ZA_CQ:ZA_CQ + Q_LORA]
        ckv = zfull[:, ZA_CKV:ZA_CKV + KV_LORA]
        kr = zfull[:, ZA_KR:ZA_KR + LANES]
        cqn = cq * _rms_parts(cq) * gq_ref[...]
        ckvn = ckv * _rms_parts(ckv) * gkv_ref[...]
        q = _dot(cqn.astype(BF16), wuq_ref[...])
        kv = _dot(ckvn.astype(BF16), wukv_ref[...])
        c, sa, sb = c_ref[...], sa_ref[...], sb_ref[...]
        krope = _rope(kr, c, sa, sb)
        pad_lane = _lane((ts, HEAD_PAD)) >= HEAD_DIM
        for h in range(N_TOK_HEADS):
            sl = slice(h * HEAD_PAD, (h + 1) * HEAD_PAD)
            q_ref[h] = _rope(q[:, sl], c, sa, sb).astype(BF16)
            k_ref[h] = (kv[:, sl] + krope).astype(BF16)
            vh = kv[:, QKV_PAD + h * HEAD_PAD:QKV_PAD + (h + 1) * HEAD_PAD]
            v_ref[h] = jnp.where(pad_lane, 1.0, vh).astype(BF16)

    out = jax.ShapeDtypeStruct((N_TOK_HEADS, s, HEAD_PAD), BF16)
    return pl.pallas_call(
        body, grid=(s // ts,),
        in_specs=[_rows(ts, D_MODEL), _full((D_MODEL, ZP)),
                  _rows(ts, LANES), _rows(ts, LANES), _rows(ts, LANES),
                  _full((1, Q_LORA)), _full((1, KV_LORA)),
                  _full((Q_LORA, QKV_PAD)), _full((KV_LORA, 2 * QKV_PAD))],
        out_specs=[_rows(ts, ZA_W), _rows(ts, ZG_W)] + [_heads(ts)] * 3,
        out_shape=[jax.ShapeDtypeStruct((s, ZA_W), F32), jax.ShapeDtypeStruct((s, ZG_W), F32),
                   out, out, out],
        name="mla_prep_fwd", compiler_params=_cp(1))(x, win, *tabs, gq, gkv, wuq, wukv)


def _mla_prep_bwd(z0, dq, dk, dv, tabs, gq, gkv, wuq_t, wukv_t, ts):
    s = z0.shape[0]

    def body(z_ref, dq_ref, dk_ref, dv_ref, c_ref, sa_ref, sb_ref, gq_ref, gkv_ref,
             wuqt_ref, wukvt_ref, dza_ref, dzk_ref, dwuq_ref, dwukv_ref, dg_ref):
        @pl.when(pl.program_id(0) == 0)
        def _():
            dwuq_ref[...] = jnp.zeros_like(dwuq_ref)
            dwukv_ref[...] = jnp.zeros_like(dwukv_ref)
            dg_ref[...] = jnp.zeros_like(dg_ref)

        cq = z_ref[:, ZA_CQ:ZA_CQ + Q_LORA]
        ckv = z_ref[:, ZA_CKV:ZA_CKV + KV_LORA]
        rq, rkv = _rms_parts(cq), _rms_parts(ckv)
        gq_, gkv_ = gq_ref[...], gkv_ref[...]
        cqn = (cq * rq * gq_).astype(BF16)
        ckvn = (ckv * rkv * gkv_).astype(BF16)
        c, sa, sb = c_ref[...], sa_ref[...], sb_ref[...]
        dqp, dksum = [], None
        for h in range(N_TOK_HEADS):
            dqp.append(_rope_t(dq_ref[h], c, sa, sb))
            dksum = dk_ref[h] if dksum is None else dksum + dk_ref[h]
        dqp = jnp.concatenate(dqp, axis=1).astype(BF16)
        lane = _lane(dksum.shape)
        dzk_ref[...] = jnp.where((lane >= KR_LANE) & (lane < KR_LANE + QK_ROPE),
                                 _rope_t(dksum, c, sa, sb), 0.0).astype(BF16)
        dkv = jnp.concatenate([dk_ref[h].astype(BF16) for h in range(N_TOK_HEADS)]
                              + [dv_ref[h] for h in range(N_TOK_HEADS)], axis=1)
        dcqn = _dot(dqp, wuqt_ref[...])
        dckvn = _dot(dkv, wukvt_ref[...])
        dwuq_ref[...] += _dot_tn(cqn, dqp)
        dwukv_ref[...] += _dot_tn(ckvn, dkv)
        dg_ref[0:1, 0:Q_LORA] += jnp.sum(dcqn * cq * rq, axis=0, keepdims=True)
        dg_ref[0:1, Q_LORA:Q_LORA + KV_LORA] += jnp.sum(dckvn * ckv * rkv, axis=0, keepdims=True)
        wq = dcqn * gq_
        wkv = dckvn * gkv_
        dcq = rq * wq - cq * (rq * rq * rq) * jnp.mean(wq * cq, axis=-1, keepdims=True)
        dckv = rkv * wkv - ckv * (rkv * rkv * rkv) * jnp.mean(wkv * ckv, axis=-1, keepdims=True)
        dza_ref[:, 0:Q_LORA] = dcq.astype(BF16)
        dza_ref[:, Q_LORA:Q_LORA + KV_LORA] = dckv.astype(BF16)

    na = Q_LORA + KV_LORA
    return pl.pallas_call(
        body, grid=(s // ts,),
        in_specs=[_rows(ts, ZA_W), _heads(ts), _heads(ts), _heads(ts),
                  _rows(ts, LANES), _rows(ts, LANES), _rows(ts, LANES),
                  _full((1, Q_LORA)), _full((1, KV_LORA)),
                  _full((QKV_PAD, Q_LORA)), _full((2 * QKV_PAD, KV_LORA))],
        out_specs=[_rows(ts, na), _rows(ts, LANES), _full((Q_LORA, QKV_PAD)),
                   _full((KV_LORA, 2 * QKV_PAD)), _full((SUBLANES, na))],
        out_shape=[jax.ShapeDtypeStruct((s, na), BF16), jax.ShapeDtypeStruct((s, LANES), BF16),
                   jax.ShapeDtypeStruct((Q_LORA, QKV_PAD), F32),
                   jax.ShapeDtypeStruct((KV_LORA, 2 * QKV_PAD), F32),
                   jax.ShapeDtypeStruct((SUBLANES, na), F32)],
        name="mla_prep_bwd", compiler_params=_cp(1))(
            z0, dq, dk, dv, *tabs, gq, gkv, wuq_t, wukv_t)


def _causal_pairs(nb, by_key):
    if by_key:
        pairs = [(i, j) for j in range(nb) for i in range(j, nb)]
    else:
        pairs = [(i, j) for i in range(nb) for j in range(i + 1)]
    return (jnp.array([p[0] for p in pairs], jnp.int32),
            jnp.array([p[1] for p in pairs], jnp.int32))


def _flash_fwd(q, k, v, t, nh):
    s = q.shape[1]
    itab, jtab = _causal_pairs(s // t, False)
    c2 = LOG2E / math.sqrt(QK_DIM)

    def body(it_ref, jt_ref, q_ref, k_ref, v_ref, o_ref, lse_ref, m_scr, acc_scr):
        pair = pl.program_id(1)
        i, j = it_ref[pair], jt_ref[pair]

        @pl.when(j == 0)
        def _():
            m_scr[...] = jnp.full_like(m_scr, NEG_BIG)
            acc_scr[...] = jnp.zeros_like(acc_scr)

        def softmax_strips(masked, hs, sc, row0):
            ps, als = [], []
            for r0 in range(0, sc.shape[0], STRIP):
                rows = slice(row0 + r0, row0 + r0 + STRIP)
                ch = [sc[r0:r0 + STRIP, n * LANES:(n + 1) * LANES] * c2
                      for n in range(sc.shape[1] // LANES)]
                if masked:
                    rr = row0 + r0 + lax.broadcasted_iota(jnp.int32, (STRIP, LANES), 0)
                    cc = lax.broadcasted_iota(jnp.int32, (STRIP, LANES), 1)
                    ch = [jnp.where(cc + n * LANES <= rr, c_, NEG_BIG) for n, c_ in enumerate(ch)]
                mx = ch[0]
                for c_ in ch[1:]:
                    mx = jnp.maximum(mx, c_)
                m_prev = m_scr[hs, rows, :]
                m_next = jnp.maximum(m_prev, jnp.max(mx, axis=-1, keepdims=True))
                ps.append(jnp.concatenate(
                    [jnp.exp2(c_ - m_next).astype(BF16) for c_ in ch], axis=1))
                als.append(jnp.exp2(m_prev - m_next))
                m_scr[hs, rows, :] = m_next
            return jnp.concatenate(ps, axis=0), jnp.concatenate(als, axis=0)

        def run(masked, parts):
            def scores_of(hs):
                return [_dot_nt(q_ref[hs, r0:r0 + nr, :], k_ref[hs, 0:nk, :])
                        for r0, nr, nk in parts]

            ahead = min(LOOKAHEAD, nh)
            scores = [scores_of(hs) for hs in range(ahead)]
            for hs in range(nh):
                if hs + ahead < nh:
                    scores.append(scores_of(hs + ahead))
                for (r0, nr, nk), sc in zip(parts, scores[hs]):
                    p, alpha = softmax_strips(masked, hs, sc, r0)
                    acc_scr[hs, r0:r0 + nr, :] = (alpha * acc_scr[hs, r0:r0 + nr, :]
                                                  + _dot(p, v_ref[hs, 0:nk, :]))

        @pl.when(j < i)
        def _():
            run(False, [(0, t, t)])

        @pl.when(j == i)
        def _():
            run(True, [(0, t, t)])
            for h in range(nh):
                acc = acc_scr[h]
                l = acc[:, HEAD_DIM:HEAD_DIM + 1]
                o_ref[h] = jnp.where(_lane(acc.shape) < HEAD_DIM, acc / l, 0.0)
                lse_ref[h] = (m_scr[h] + jnp.log2(l)).T[0:1, :]

    qspec = pl.BlockSpec((nh, t, HEAD_PAD), lambda h, p, it, jt: (h, it[p], 0))
    kspec = pl.BlockSpec((nh, t, HEAD_PAD), lambda h, p, it, jt: (h, jt[p], 0))
    lspec = pl.BlockSpec((nh, 1, t), lambda h, p, it, jt: (h, 0, it[p]))
    out = jax.ShapeDtypeStruct((N_TOK_HEADS, s, HEAD_PAD), F32)
    return pl.pallas_call(
        body,
        grid_spec=pltpu.PrefetchScalarGridSpec(
            num_scalar_prefetch=2, grid=(N_TOK_HEADS // nh, itab.shape[0]),
            in_specs=[qspec, kspec, kspec], out_specs=[qspec, lspec],
            scratch_shapes=[pltpu.VMEM((nh, t, HEAD_PAD), F32)] * 2),
        out_shape=[out, jax.ShapeDtypeStruct((N_TOK_HEADS, 1, s), F32)],
        name="flash_fwd", compiler_params=_cp(2))(itab, jtab, q, k, v)


def _flash_bwd(q, k, v, stats, do, t, nh, ex):
    s = q.shape[1]
    nb = s // t
    itab, jtab = _causal_pairs(nb, True)
    npairs = itab.shape[0]
    ngroups = N_TOK_HEADS // nh
    scale = 1.0 / math.sqrt(QK_DIM)
    c2 = LOG2E * scale
    nx = ex.n if ex is not None else 0
    ex_arrays, ex_out_shape, ex_scratch = (
        (ex.arrays, ex.out_shape, ex.scratch) if ex is not None else ([], [], []))

    def body(it_ref, jt_ref, q_ref, k_ref, v_ref, st_ref, do_ref, *rest):
        ex_in, rest = rest[:nx], rest[nx:]
        dq_ref, dk_ref, dv_ref = rest[:3]
        ex_out, rest = rest[3:3 + nx], rest[3 + nx:]
        dk_scr, dv_scr = rest[:2]
        ex_sems = rest[2:]
        pair = pl.program_id(1)
        i, j = it_ref[pair], jt_ref[pair]
        rows_i = pl.ds(pl.multiple_of(i * t, t), t)

        if nx:
            @pl.when(jnp.logical_and(pl.program_id(0) == 0, pair == 0))
            def _():
                for cp in ex.copies(ex_in, ex_out, ex_sems):
                    cp.start()

        @pl.when(i == j)
        def _():
            dk_scr[...] = jnp.zeros_like(dk_scr)
            dv_scr[...] = jnp.zeros_like(dv_scr)

        @pl.when(j == 0)
        def _():
            dq_ref[:, rows_i, :] = jnp.zeros((nh, t, HEAD_PAD), F32)

        def prob_strips(masked, h, sct, dpt, k0, q0):
            ps, dss = [], []
            for r0 in range(0, sct.shape[0], STRIP):
                rows = slice(r0, r0 + STRIP)
                if masked:
                    kk = k0 + r0 + lax.broadcasted_iota(jnp.int32, (STRIP, LANES), 0)
                    qq = q0 + lax.broadcasted_iota(jnp.int32, (STRIP, LANES), 1)
                pcs, dcs = [], []
                for n in range(sct.shape[1] // LANES):
                    cols = slice(n * LANES, (n + 1) * LANES)
                    qcols = slice(q0 + n * LANES, q0 + (n + 1) * LANES)
                    x = sct[rows, cols] * c2
                    if masked:
                        x = jnp.where(kk <= qq + n * LANES, x, NEG_BIG)
                    p = jnp.exp2(x - st_ref[h, 0:1, qcols])
                    pcs.append(p.astype(BF16))
                    dcs.append((p * (dpt[rows, cols] - st_ref[h, 1:2, qcols]) * scale).astype(BF16))
                ps.append(jnp.concatenate(pcs, axis=1))
                dss.append(jnp.concatenate(dcs, axis=1))
            return jnp.concatenate(ps, axis=0), jnp.concatenate(dss, axis=0)

        def run(masked, parts):
            def scores_of(h):
                return [(_dot_nt(k_ref[h, k0:k0 + nk, :], q_ref[h, q0:q0 + nq, :]),
                         _dot_nt(v_ref[h, k0:k0 + nk, :], do_ref[h, q0:q0 + nq, :]))
                        for k0, nk, q0, nq in parts]

            ahead = min(LOOKAHEAD, nh)
            scores = [scores_of(h) for h in range(ahead)]
            for h in range(nh):
                if h + ahead < nh:
                    scores.append(scores_of(h + ahead))
                for (k0, nk, q0, nq), (sct, dpt) in zip(parts, scores[h]):
                    pt, dst = prob_strips(masked, h, sct, dpt, k0, q0)
                    dv_scr[h, k0:k0 + nk, :] += _dot(pt, do_ref[h, q0:q0 + nq, :])
                    dk_scr[h, k0:k0 + nk, :] += _dot(dst, q_ref[h, q0:q0 + nq, :])
                    rows = pl.ds(pl.multiple_of(i * t + q0, t // 2), nq)
                    dq_ref[h, rows, :] += _dot_tn(dst, k_ref[h, k0:k0 + nk, :])

        @pl.when(i > j)
        def _():
            run(False, [(0, t, 0, t)])

        @pl.when(i == j)
        def _():
            run(True, [(0, t // 2, 0, t), (t // 2, t // 2, t // 2, t // 2)])

        @pl.when(i == nb - 1)
        def _():
            dk_ref[...] = dk_scr[...]
            dv_ref[...] = dv_scr[...].astype(BF16)

        if nx:
            @pl.when(jnp.logical_and(pl.program_id(0) == ngroups - 1, pair == npairs - 1))
            def _():
                for cp in ex.copies(ex_in, ex_out, ex_sems):
                    cp.wait()

    qspec = pl.BlockSpec((nh, t, HEAD_PAD), lambda h, p, it, jt: (h, it[p], 0))
    kspec = pl.BlockSpec((nh, t, HEAD_PAD), lambda h, p, it, jt: (h, jt[p], 0))
    dqspec = pl.BlockSpec((nh, s, HEAD_PAD), lambda h, p, it, jt: (h, 0, 0))
    stspec = pl.BlockSpec((nh, 2, t), lambda h, p, it, jt: (h, 0, it[p]))
    out = jax.ShapeDtypeStruct((N_TOK_HEADS, s, HEAD_PAD), F32)
    res = pl.pallas_call(
        body,
        grid_spec=pltpu.PrefetchScalarGridSpec(
            num_scalar_prefetch=2, grid=(ngroups, npairs),
            in_specs=[qspec, kspec, kspec, stspec, qspec] + [ANY] * nx,
            out_specs=[dqspec, kspec, kspec] + [ANY] * nx,
            scratch_shapes=[pltpu.VMEM((nh, t, HEAD_PAD), F32)] * 2 + ex_scratch),
        out_shape=[out, out, jax.ShapeDtypeStruct(out.shape, BF16)] + ex_out_shape,
        name="flash_bwd", compiler_params=_cp(2))(itab, jtab, q, k, v, stats, do, *ex_arrays)
    return res[:3], res[3:]


def _mem_probs(qp, kp, hh):
    lane = _lane(qp.shape)
    keep = (lane < HEAD_DIM) if hh == 0 else (lane >= HEAD_DIM)
    qh = jnp.where(keep, qp, 0.0).astype(BF16)
    sc = _dot_nt(qh, kp) * (1.0 / math.sqrt(HEAD_DIM))
    e = jnp.exp(sc - jnp.max(sc, axis=-1, keepdims=True))
    return e / jnp.sum(e, axis=-1, keepdims=True), keep


def _mix_out_fwd(tok, z, memkv, w, h, g, b, tgt, g0, q0, padded, name, ts):
    s = z.shape[0]
    zw = z.shape[1]
    tok_spec = _heads(ts) if padded else _rows(ts, TOK_WIDTH)
    with_loss = tgt is not None

    def body(*refs):
        tok_ref, z_ref, mkv_ref, w_ref, h_ref, g_ref, b_ref = refs[:7]
        if with_loss:
            t_ref, cat_ref, y_ref, dpre_ref, dgb_ref, loss_ref = refs[7:]
        else:
            cat_ref, y_ref, pre_ref, out_ref = refs[7:]
        if padded:
            for p in range(N_TOK_HEADS // 2):
                cat_ref[:, p * LANES:(p + 1) * LANES] = (
                    tok_ref[2 * p] + pltpu.roll(tok_ref[2 * p + 1], HEAD_DIM, 1))
        else:
            cat_ref[:, 0:TOK_WIDTH] = tok_ref[...]
        for pr in range(N_MEM_HEADS // 2):
            sl = slice(pr * LANES, (pr + 1) * LANES)
            qp = z_ref[:, q0 + pr * LANES:q0 + (pr + 1) * LANES]
            kp = mkv_ref[:, sl].astype(BF16)
            vp = mkv_ref[:, MEM_WIDTH + pr * LANES:MEM_WIDTH + (pr + 1) * LANES].astype(BF16)
            outs = []
            for hh in range(2):
                p, _ = _mem_probs(qp, kp, hh)
                outs.append(_dot(p.astype(BF16), vp))
            lane = _lane(outs[0].shape)
            cat_ref[:, TOK_WIDTH + pr * LANES:TOK_WIDTH + (pr + 1) * LANES] = jnp.where(
                lane < HEAD_DIM, outs[0], outs[1])
        gate = z_ref[:, g0:g0 + MIX_WIDTH]
        yb = (cat_ref[...] * (gate * _sigmoid(gate))).astype(BF16)
        y_ref[...] = yb
        pre = ALPHA * h_ref[...] + _dot(yb, w_ref[...])
        xhat, rstd = _ln_stats(pre)
        hout = xhat * g_ref[...] + b_ref[...]
        if with_loss:
            @pl.when(pl.program_id(0) == 0)
            def _():
                loss_ref[...] = jnp.zeros_like(loss_ref)
                dgb_ref[...] = jnp.zeros_like(dgb_ref)
            err = hout - t_ref[...]
            loss_ref[...] += 0.5 * jnp.sum(jnp.mean(err * err, axis=-1, keepdims=True))
            dh = err * (1.0 / D_MODEL)
            dpre_ref[...] = _ln_bwd(dh, xhat, rstd, g_ref[...])
            dgb_ref[0:1, :] += jnp.sum(dh * xhat, axis=0, keepdims=True)
            dgb_ref[1:2, :] += jnp.sum(dh, axis=0, keepdims=True)
        else:
            pre_ref[...] = pre
            out_ref[...] = hout

    act = jax.ShapeDtypeStruct((s, D_MODEL), F32)
    in_specs = [tok_spec, _rows(ts, zw), _full((MEM_LEN, 2 * MEM_WIDTH)),
                _full((MIX_WIDTH, D_MODEL)), _rows(ts, D_MODEL),
                _full((1, D_MODEL)), _full((1, D_MODEL))]
    out_specs = [_rows(ts, MIX_WIDTH)] * 2
    out_shape = [jax.ShapeDtypeStruct((s, MIX_WIDTH), F32),
                 jax.ShapeDtypeStruct((s, MIX_WIDTH), BF16)]
    args = [tok, z, memkv, w, h, g, b]
    if with_loss:
        in_specs.append(_rows(ts, D_MODEL))
        out_specs += [_rows(ts, D_MODEL), _full((SUBLANES, D_MODEL)), _full((SUBLANES, LANES))]
        out_shape += [act, jax.ShapeDtypeStruct((SUBLANES, D_MODEL), F32),
                      jax.ShapeDtypeStruct((SUBLANES, LANES), F32)]
        args.append(tgt)
    else:
        out_specs += [_rows(ts, D_MODEL)] * 2
        out_shape += [act, act]
    return pl.pallas_call(
        body, grid=(s // ts,), in_specs=in_specs, out_specs=out_specs, out_shape=out_shape,
        name=name, compiler_params=_cp(1))(*args)


def _gate_mem_bwd(dpre, y, w_t, cat, z, memkv, lse, g0, q0, name, ts):
    s = z.shape[0]
    zw = z.shape[1]
    padded = lse is not None
    gq_w = MIX_WIDTH + MEM_WIDTH

    def body(*refs):
        if padded:
            (dpre_ref, y_ref, wt_ref, cat_ref, z_ref, mkv_ref, lse_ref,
             dzg_ref, dtok_ref, dmkv_ref, dw_ref, st_ref) = refs
        else:
            (dpre_ref, y_ref, wt_ref, cat_ref, z_ref, mkv_ref,
             dzg_ref, dtok_ref, dmkv_ref, dw_ref) = refs

        @pl.when(pl.program_id(0) == 0)
        def _():
            dmkv_ref[...] = jnp.zeros_like(dmkv_ref)
            dw_ref[...] = jnp.zeros_like(dw_ref)

        dpb = dpre_ref[...].astype(BF16)
        dy_ = _dot(dpb, wt_ref[...])
        dw_ref[...] += _dot_tn(y_ref[...], dpb)
        gate = z_ref[:, g0:g0 + MIX_WIDTH]
        sg = _sigmoid(gate)
        dzg_ref[:, 0:MIX_WIDTH] = (dy_ * cat_ref[...]
                                   * (sg * (1.0 + gate * (1.0 - sg)))).astype(BF16)
        dcat = dy_ * (gate * sg)
        if padded:
            low = _lane((ts, LANES)) < HEAD_DIM
            for p in range(N_TOK_HEADS // 2):
                d = dcat[:, p * LANES:(p + 1) * LANES]
                prod = d * cat_ref[:, p * LANES:(p + 1) * LANES]
                first = jnp.sum(jnp.where(low, prod, 0.0), axis=-1, keepdims=True)
                second = jnp.sum(jnp.where(low, 0.0, prod), axis=-1, keepdims=True)
                dtok_ref[2 * p] = jnp.where(low, d, 0.0).astype(BF16)
                dtok_ref[2 * p + 1] = jnp.where(low, pltpu.roll(d, HEAD_DIM, 1), 0.0).astype(BF16)
                for hh, delta in ((2 * p, first), (2 * p + 1, second)):
                    st_ref[hh, 0:1, :] = lse_ref[hh]
                    st_ref[hh, 1:2, :] = jnp.broadcast_to(delta, (ts, LANES)).T[0:1, :]
        else:
            dtok_ref[...] = dcat[:, 0:TOK_WIDTH]
        for pr in range(N_MEM_HEADS // 2):
            sl = slice(pr * LANES, (pr + 1) * LANES)
            vsl = slice(MEM_WIDTH + pr * LANES, MEM_WIDTH + (pr + 1) * LANES)
            qp = z_ref[:, q0 + pr * LANES:q0 + (pr + 1) * LANES]
            qpb = qp.astype(BF16)
            kp = mkv_ref[:, sl].astype(BF16)
            vp = mkv_ref[:, vsl].astype(BF16)
            dmo = dcat[:, TOK_WIDTH + pr * LANES:TOK_WIDTH + (pr + 1) * LANES]
            dqp = None
            for hh in range(2):
                p, keep = _mem_probs(qp, kp, hh)
                do_h = jnp.where(keep, dmo, 0.0).astype(BF16)
                dmkv_ref[:, vsl] += _dot_tn(p.astype(BF16), do_h)
                dp = _dot_nt(do_h, vp)
                ds = (p * (dp - jnp.sum(dp * p, axis=-1, keepdims=True))
                      * (1.0 / math.sqrt(HEAD_DIM))).astype(BF16)
                dqh = jnp.where(keep, _dot(ds, kp), 0.0)
                dqp = dqh if dqp is None else dqp + dqh
                dkh = _dot_tn(ds, qpb)
                klane = _lane(dkh.shape)
                kkeep = (klane < HEAD_DIM) if hh == 0 else (klane >= HEAD_DIM)
                dmkv_ref[:, sl] += jnp.where(kkeep, dkh, 0.0)
            dzg_ref[:, MIX_WIDTH + pr * LANES:MIX_WIDTH + (pr + 1) * LANES] = dqp.astype(BF16)

    in_specs = [_rows(ts, D_MODEL), _rows(ts, MIX_WIDTH), _full((D_MODEL, MIX_WIDTH)),
                _rows(ts, MIX_WIDTH), _rows(ts, zw), _full((MEM_LEN, 2 * MEM_WIDTH))]
    out_specs = [_rows(ts, gq_w), _heads(ts) if padded else _rows(ts, TOK_WIDTH),
                 _full((MEM_LEN, 2 * MEM_WIDTH)), _full((MIX_WIDTH, D_MODEL))]
    heads_shape = (N_TOK_HEADS, s, HEAD_PAD)
    out_shape = [jax.ShapeDtypeStruct((s, gq_w), BF16),
                 jax.ShapeDtypeStruct(heads_shape, BF16) if padded
                 else jax.ShapeDtypeStruct((s, TOK_WIDTH), F32),
                 jax.ShapeDtypeStruct((MEM_LEN, 2 * MEM_WIDTH), F32),
                 jax.ShapeDtypeStruct((MIX_WIDTH, D_MODEL), F32)]
    args = [dpre, y, w_t, cat, z, memkv]
    if padded:
        in_specs.append(pl.BlockSpec((N_TOK_HEADS, 1, ts), lambda i: (0, 0, i)))
        out_specs.append(pl.BlockSpec((N_TOK_HEADS, 2, ts), lambda i: (0, 0, i)))
        out_shape.append(jax.ShapeDtypeStruct((N_TOK_HEADS, 2, s), F32))
        args.append(lse)
    return pl.pallas_call(
        body, grid=(s // ts,), in_specs=in_specs, out_specs=out_specs, out_shape=out_shape,
        name=name, compiler_params=_cp(1))(*args)


def _ln_stats(pre):
    mu = jnp.mean(pre, axis=-1, keepdims=True)
    d = pre - mu
    rstd = lax.rsqrt(jnp.mean(d * d, axis=-1, keepdims=True) + NORM_EPS)
    return d * rstd, rstd


def _ln_bwd(dh, xhat, rstd, g):
    dxh = dh * g
    return rstd * (dxh - jnp.mean(dxh, axis=-1, keepdims=True)
                   - xhat * jnp.mean(dxh * xhat, axis=-1, keepdims=True))


def _linear_bwd(x, dys, offs, w_t, resid, ln, name, ts):
    s, kdim = x.shape
    n = w_t.shape[0]
    widths = [d.shape[1] for d in dys]
    npieces = len(dys)
    with_ln = ln is not None

    def body(*refs):
        x_ref = refs[0]
        dy_refs = refs[1:1 + npieces]
        if with_ln:
            wt_ref, r_ref, pre_ref, g_ref, dx_ref, dw_ref, dgb_ref = refs[1 + npieces:]
        else:
            wt_ref, r_ref, dx_ref, dw_ref = refs[1 + npieces:]

        @pl.when(pl.program_id(0) == 0)
        def _():
            dw_ref[...] = jnp.zeros_like(dw_ref)
            if with_ln:
                dgb_ref[...] = jnp.zeros_like(dgb_ref)

        xb = x_ref[...].astype(BF16)
        dx = ALPHA * r_ref[...]
        for dy_ref, off, wd in zip(dy_refs, offs, widths):
            dyb = dy_ref[...].astype(BF16)
            dx = dx + _dot(dyb, wt_ref[off:off + wd, :])
            dw_ref[:, off:off + wd] += _dot_tn(xb, dyb)
        if with_ln:
            xhat, rstd = _ln_stats(pre_ref[...])
            dx_ref[...] = _ln_bwd(dx, xhat, rstd, g_ref[...])
            dgb_ref[0:1, :] += jnp.sum(dx * xhat, axis=0, keepdims=True)
            dgb_ref[1:2, :] += jnp.sum(dx, axis=0, keepdims=True)
        else:
            dx_ref[...] = dx

    in_specs = ([_rows(ts, kdim)] + [_rows(ts, wd) for wd in widths]
                + [_full((n, kdim)), _rows(ts, kdim)])
    out_specs = [_rows(ts, kdim), _full((kdim, n))]
    out_shape = [jax.ShapeDtypeStruct((s, kdim), F32), jax.ShapeDtypeStruct((kdim, n), F32)]
    args = [x, *dys, w_t, resid]
    if with_ln:
        in_specs += [_rows(ts, kdim), _full((1, kdim))]
        out_specs.append(_full((SUBLANES, kdim)))
        out_shape.append(jax.ShapeDtypeStruct((SUBLANES, kdim), F32))
        args += list(ln)
    return pl.pallas_call(
        body, grid=(s // ts,), in_specs=in_specs, out_specs=out_specs, out_shape=out_shape,
        name=name, compiler_params=_cp(1))(*args)


def _wgrad_small(x, dy, name):
    def body(x_ref, dy_ref, dw_ref):
        dw_ref[...] = _dot_tn(x_ref[...].astype(BF16), dy_ref[...].astype(BF16))

    return pl.pallas_call(
        body, out_shape=jax.ShapeDtypeStruct((x.shape[1], dy.shape[1]), F32),
        name=name, compiler_params=pltpu.CompilerParams(vmem_limit_bytes=VMEM_LIMIT))(x, dy)


def _shift_down(u, carry8, k):
    if k == 0:
        return u
    rolled = pltpu.roll(u, k, 0)
    row = lax.broadcasted_iota(jnp.int32, carry8.shape, 0)
    top = jnp.where(row < k, pltpu.roll(carry8, k, 0), rolled[0:SUBLANES])
    return jnp.concatenate([top, rolled[SUBLANES:]], axis=0)


def _shift_up(u, carry8, k):
    if k == 0:
        return u
    n = u.shape[0]
    rolled = pltpu.roll(u, n - k, 0)
    row = lax.broadcasted_iota(jnp.int32, carry8.shape, 0)
    bot = jnp.where(row >= SUBLANES - k, pltpu.roll(carry8, SUBLANES - k, 0),
                    rolled[n - SUBLANES:])
    return jnp.concatenate([rolled[:n - SUBLANES], bot], axis=0)


def _neg_expm1(t):
    e = jnp.exp(t)
    em1 = e - 1.0
    safe = jnp.where(e == 1.0, 1.0, jnp.log(e))
    return -jnp.where(e == 1.0, t, jnp.where(em1 == -1.0, -1.0, em1 * t / safe))


def _lru_gates(u, carry8, cw_ref, vec_ref, wr_ref, wi_ref):
    taps = [_shift_down(u, carry8, k) for k in range(CONV_W)]
    xc = vec_ref[0:1, :] + cw_ref[3:4, :] * u
    for k in range(1, CONV_W):
        xc = xc + cw_ref[3 - k:4 - k, :] * taps[k]
    xb = xc.astype(BF16)
    r = _sigmoid(_dot(xb, wr_ref[...]) + vec_ref[1:2, :])
    ig = _sigmoid(_dot(xb, wi_ref[...]) + vec_ref[2:3, :])
    nlam = -vec_ref[3:4, :]
    softplus = jnp.maximum(nlam, 0.0) + jnp.log(1.0 + jnp.exp(-jnp.abs(nlam)))
    cneg = -LRU_C * softplus
    log_a = cneg * r
    a = jnp.exp(log_a)
    sq = jnp.sqrt(_neg_expm1(2.0 * log_a))
    return xc, r, ig, cneg, a, sq, taps


def _chained_scan(a_ref, b_ref, out_ref, cum_scr, x_in):
    rows_total, w = a_ref.shape
    nseg = SCAN_SEGMENTS
    seg = rows_total // nseg

    def step(t, carry):
        xs, ps = carry
        new_x, new_p = [], []
        for sg in range(nseg):
            row = pl.ds(sg * seg + t, 1)
            a = a_ref[row, :]
            x = a * xs[sg] + b_ref[row, :]
            out_ref[row, :] = x
            new_x.append(x)
            if sg > 0:
                p = a * ps[sg - 1]
                cum_scr[row, :] = p
                new_p.append(p)
        return tuple(new_x), tuple(new_p)

    zero, one = jnp.zeros((1, w), F32), jnp.ones((1, w), F32)
    xs, _ = lax.fori_loop(0, seg, step, ((x_in,) + (zero,) * (nseg - 1), (one,) * (nseg - 1)))
    x_prev = xs[0]
    for sg in range(1, nseg):
        rows = slice(sg * seg, (sg + 1) * seg)
        out_ref[rows, :] = out_ref[rows, :] + cum_scr[rows, :] * x_prev
        x_prev = out_ref[(sg + 1) * seg - 1:(sg + 1) * seg, :]
    return x_prev


def _lru_fwd(x, win, cw8, vec8, wr, wi, ts):
    s = x.shape[0]

    def body(x_ref, win_ref, cw_ref, vec_ref, wr_ref, wi_ref, u_ref, zg_ref, hs_ref,
             cu_scr, ch_scr, a_scr, gx_scr, cum_scr):
        @pl.when(pl.program_id(0) == 0)
        def _():
            cu_scr[...] = jnp.zeros_like(cu_scr)
            ch_scr[...] = jnp.zeros_like(ch_scr)

        zfull = _dot(x_ref[...].astype(BF16), win_ref[...])
        u = zfull[:, 0:ZA_W]
        u_ref[...] = u
        zg_ref[...] = zfull[:, ZA_W:ZP]
        xc, _, ig, _, a, sq, _ = _lru_gates(u, cu_scr[...], cw_ref, vec_ref, wr_ref, wi_ref)
        a_scr[...] = a
        gx_scr[...] = sq * (ig * xc)
        ch_scr[0:1, :] = _chained_scan(a_scr, gx_scr, hs_ref, cum_scr, ch_scr[0:1, :])
        cu_scr[...] = u[ts - SUBLANES:, :]

    w = TOK_WIDTH
    return pl.pallas_call(
        body, grid=(s // ts,),
        in_specs=[_rows(ts, D_MODEL), _full((D_MODEL, ZP)),
                  _full((SUBLANES, w)), _full((SUBLANES, w)), _full((w, w)), _full((w, w))],
        out_specs=[_rows(ts, w), _rows(ts, ZG_W), _rows(ts, w)],
        out_shape=[jax.ShapeDtypeStruct((s, w), F32), jax.ShapeDtypeStruct((s, ZG_W), F32),
                   jax.ShapeDtypeStruct((s, w), F32)],
        scratch_shapes=[pltpu.VMEM((SUBLANES, w), F32), pltpu.VMEM((SUBLANES, w), F32)]
                       + [pltpu.VMEM((ts, w), F32)] * 3,
        name="lru_fwd", compiler_params=_cp(1))(x, win, cw8, vec8, wr, wi)


def _lru_bwd(z1, dhs, hs, cw8, vec8, wr, wi, wr_t, wi_t, ts):
    s = z1.shape[0]
    nb = s // ts
    w = TOK_WIDTH
    tiles = ts // SUBLANES

    def body(u_ref, up_ref, dhs_ref, hs_ref, hsp_ref, cw_ref, vec_ref, wr_ref, wi_ref,
             wrt_ref, wit_ref, du_ref, dwr_ref, dwi_ref, dvec_ref,
             cc_scr, cd_scr, a_scr, dh_scr):
        i = pl.program_id(0)

        @pl.when(i == 0)
        def _():
            cc_scr[...] = jnp.zeros_like(cc_scr)
            cd_scr[...] = jnp.zeros_like(cd_scr)
            dwr_ref[...] = jnp.zeros_like(dwr_ref)
            dwi_ref[...] = jnp.zeros_like(dwi_ref)
            dvec_ref[...] = jnp.zeros_like(dvec_ref)

        u = u_ref[...]
        first = i == nb - 1
        carry8 = jnp.where(first, 0.0, up_ref[...])
        xc, r, ig, cneg, a, sq, taps = _lru_gates(u, carry8, cw_ref, vec_ref, wr_ref, wi_ref)
        a_scr[...] = a

        def step(n, c):
            t = ts - 1 - n
            dh = dhs_ref[pl.ds(t, 1), :] + c
            dh_scr[pl.ds(t, 1), :] = dh
            return a_scr[pl.ds(t, 1), :] * dh

        cc_scr[0:1, :] = lax.fori_loop(0, ts, step, cc_scr[0:1, :])
        dh = dh_scr[...]
        hprev = _shift_down(hs_ref[...], jnp.where(first, 0.0, hsp_ref[...]), 1)
        ix = ig * xc
        dix = dh * sq
        dlog_a = dh * hprev * a - (dh * ix) * (a * a) / sq
        dpr = (dlog_a * cneg) * r * (1.0 - r)
        dpi = (dix * xc) * ig * (1.0 - ig)
        dprb, dpib = dpr.astype(BF16), dpi.astype(BF16)
        xb = xc.astype(BF16)
        dwr_ref[...] += _dot_tn(xb, dprb)
        dwi_ref[...] += _dot_tn(xb, dpib)
        dxc = dix * ig + _dot(dprb, wrt_ref[...]) + _dot(dpib, wit_ref[...])
        for k in range(CONV_W):
            dvec_ref[3 - k:4 - k, :] += jnp.sum(dxc * taps[k], axis=0, keepdims=True)
        dvec_ref[4:5, :] += jnp.sum(dxc, axis=0, keepdims=True)
        dvec_ref[5:6, :] += jnp.sum(dpr, axis=0, keepdims=True)
        dvec_ref[6:7, :] += jnp.sum(dpi, axis=0, keepdims=True)
        dvec_ref[7:8, :] += (jnp.sum(dlog_a * r, axis=0, keepdims=True)
                             * (LRU_C * _sigmoid(-vec_ref[3:4, :])))
        nxt = cd_scr[...]
        du = cw_ref[3:4, :] * dxc
        for k in range(1, CONV_W):
            du = du + cw_ref[3 - k:4 - k, :] * _shift_up(dxc, nxt, k)
        du_ref[...] = du.astype(BF16)
        cd_scr[...] = dxc[0:SUBLANES, :]

    rev = lambda i: (nb - 1 - i, 0)
    prev8 = lambda i: (jnp.maximum((nb - 1 - i) * tiles - 1, 0), 0)
    blk = pl.BlockSpec((ts, w), rev)
    before = pl.BlockSpec((SUBLANES, w), prev8)
    scr = pltpu.VMEM((ts, w), F32)
    return pl.pallas_call(
        body, grid=(nb,),
        in_specs=[blk, before, blk, blk, before,
                  _full((SUBLANES, w)), _full((SUBLANES, w)),
                  _full((w, w)), _full((w, w)), _full((w, w)), _full((w, w))],
        out_specs=[blk, _full((w, w)), _full((w, w)), _full((SUBLANES, w))],
        out_shape=[jax.ShapeDtypeStruct((s, w), BF16), jax.ShapeDtypeStruct((w, w), F32),
                   jax.ShapeDtypeStruct((w, w), F32), jax.ShapeDtypeStruct((SUBLANES, w), F32)],
        scratch_shapes=[pltpu.VMEM((SUBLANES, w), F32), pltpu.VMEM((SUBLANES, w), F32),
                        scr, scr],
        name="lru_bwd", compiler_params=_cp(1))(
            z1, z1, dhs, hs, hs, cw8, vec8, wr, wi, wr_t, wi_t)


def _adamw(parts, w, m, v, name):
    n = len(parts)
    rows_per = parts[0].shape[1]

    def body(*refs):
        p_refs = refs[:n]
        w_ref, m_ref, v_ref, g_ref, d_ref, nm_ref, nv_ref = refs[n:]
        for l, p_ref in enumerate(p_refs):
            rows = slice(l * rows_per, (l + 1) * rows_per)
            g = p_ref[0].astype(F32)
            for dev in range(1, N_DEV):
                g = g + p_ref[dev].astype(F32)
            g_ref[rows, :] = g
            nm = ADAM_B1 * m_ref[rows, :] + (1.0 - ADAM_B1) * g
            nv = ADAM_B2 * v_ref[rows, :] + (1.0 - ADAM_B2) * (g * g)
            m_hat = nm / (1.0 - ADAM_B1 ** ADAM_STEP)
            v_hat = nv / (1.0 - ADAM_B2 ** ADAM_STEP)
            d_ref[rows, :] = -ADAM_LR * (m_hat / (jnp.sqrt(v_hat) + ADAM_EPS)
                                         + ADAM_WD * w_ref[rows, :])
            nm_ref[rows, :] = nm
            nv_ref[rows, :] = nv

    out = jax.ShapeDtypeStruct(w.shape, F32)
    return pl.pallas_call(
        body, out_shape=[out] * 4, name=name,
        compiler_params=pltpu.CompilerParams(vmem_limit_bytes=VMEM_LIMIT))(*parts, w, m, v)


ANY = pl.BlockSpec(memory_space=pl.ANY)
MESH = pl.DeviceIdType.MESH


def _slot(p):
    return 4 * p[0] + 2 * p[1] + p[2]


def _allgather(xs):
    n = len(xs)

    def body(*refs):
        x_refs, o_refs = refs[:n], refs[n:2 * n]
        send_sems, recv_sems, local_sems = refs[2 * n:]
        x, y, c = lax.axis_index("x"), lax.axis_index("y"), lax.axis_index("c")
        me, sibling = (x, y, c), (x, y, 1 - c)
        chips = [(1 - x, y), (x, 1 - y), (1 - x, 1 - y)]

        def copy(a, k, block, to, from_input=False):
            dst = o_refs[a].at[_slot(block)]
            return pltpu.make_async_remote_copy(
                src_ref=x_refs[a] if from_input else dst, dst_ref=dst,
                send_sem=send_sems.at[a, k], recv_sem=recv_sems.at[a, k],
                device_id=to, device_id_type=MESH)

        mine = [pltpu.make_async_copy(x_refs[a], o_refs[a].at[_slot(me)], local_sems.at[a])
                for a in range(n)]
        for cp in mine:
            cp.start()
        first = []
        for a in range(n):
            first.append(copy(a, 0, me, sibling, True))
            first += [copy(a, 1 + j, me, (*chip, c), True) for j, chip in enumerate(chips)]
        for cp in first:
            cp.start()
        passed = []
        for j, chip in enumerate(chips):
            for a in range(n):
                copy(a, 1 + j, (*chip, c), me).wait_recv()
                cp = copy(a, 4 + j, (*chip, c), sibling)
                cp.start()
                passed.append(cp)
        for a in range(n):
            copy(a, 0, sibling, me).wait_recv()
            for j, chip in enumerate(chips):
                copy(a, 4 + j, (*chip, 1 - c), me).wait_recv()
        for cp in first + passed:
            cp.wait_send()
        for cp in mine:
            cp.wait()

    return pl.pallas_call(
        body,
        out_shape=[jax.ShapeDtypeStruct((N_DEV,) + t.shape, t.dtype) for t in xs],
        in_specs=[ANY] * n, out_specs=[ANY] * n,
        scratch_shapes=[pltpu.SemaphoreType.DMA((n, 7)), pltpu.SemaphoreType.DMA((n, 7)),
                        pltpu.SemaphoreType.DMA((n,))],
        name="allgather_weights")(*xs)


class _Exchange:
    def __init__(self, arrays, kinds):
        self.arrays, self.kinds, self.n = list(arrays), list(kinds), len(arrays)
        self.shapes = [self._part_shape(a, k) for a, k in zip(arrays, kinds)]
        self.out_shape = [jax.ShapeDtypeStruct((N_DEV,) + shp, a.dtype)
                          for shp, a in zip(self.shapes, arrays)]
        self.scratch = [pltpu.SemaphoreType.DMA((self.n, N_DEV - 1)),
                        pltpu.SemaphoreType.DMA((self.n, N_DEV - 1)),
                        pltpu.SemaphoreType.DMA((self.n,))]

    @staticmethod
    def _part_shape(arr, kind):
        if kind == "chunks":
            return arr.shape[1:]
        if kind == "cols":
            return (arr.shape[0], arr.shape[1] // N_DEV)
        if kind == "rows":
            return (arr.shape[0] // N_DEV, arr.shape[1])
        return arr.shape

    def copies(self, in_refs, out_refs, sems):
        send_sems, recv_sems, local_sems = sems
        x, y, c = lax.axis_index("x"), lax.axis_index("y"), lax.axis_index("c")
        me = _slot((x, y, c))

        def part(a, dev):
            ref, kind, shp = in_refs[a], self.kinds[a], self.shapes[a]
            if kind == "chunks":
                return ref.at[dev]
            if kind == "cols":
                return ref.at[:, pl.ds(pl.multiple_of(dev * shp[1], LANES), shp[1])]
            if kind == "rows":
                return ref.at[pl.ds(pl.multiple_of(dev * shp[0], SUBLANES), shp[0]), :]
            return ref

        cps = [pltpu.make_async_copy(part(a, me), out_refs[a].at[me], local_sems.at[a])
               for a in range(self.n)]
        for rel in range(1, N_DEV):
            peer = (x ^ (rel >> 2), y ^ ((rel >> 1) & 1), c ^ (rel & 1))
            for a in range(self.n):
                cps.append(pltpu.make_async_remote_copy(
                    src_ref=part(a, _slot(peer)), dst_ref=out_refs[a].at[me],
                    send_sem=send_sems.at[a, rel - 1], recv_sem=recv_sems.at[a, rel - 1],
                    device_id=peer, device_id_type=MESH))
        return cps


def _exchange_grads(arrays, kinds, name):
    ex = _Exchange(arrays, kinds)
    n = ex.n

    def body(*refs):
        cps = ex.copies(refs[:n], refs[n:2 * n], refs[2 * n:])
        for cp in cps:
            cp.start()
        for cp in cps:
            cp.wait()

    return pl.pallas_call(
        body, out_shape=ex.out_shape, in_specs=[ANY] * n, out_specs=[ANY] * n,
        scratch_shapes=ex.scratch, name=name)(*arrays)


BIG = [("mla_w_in", (D_MODEL, MLA_IN), 1), ("mla_w_uq", (Q_LORA, N_TOK_HEADS * QK_DIM), 1),
       ("mla_w_ukv", (KV_LORA, N_TOK_HEADS * 2 * HEAD_DIM), 1), ("lru_w_in", (D_MODEL, LRU_IN), 1),
       ("w_mem_kv", (2, D_MODEL, 2 * MEM_WIDTH), 1), ("w_out", (2, MIX_WIDTH, D_MODEL), 1)]
SMALL = [("lru_conv_w", (CONV_W, TOK_WIDTH), 1), ("lru_conv_b", (TOK_WIDTH,), 0),
         ("lru_b_rgate", (TOK_WIDTH,), 0), ("lru_b_igate", (TOK_WIDTH,), 0),
         ("lru_lambda", (TOK_WIDTH,), 0)]
REPL = [("mla_q_norm", (Q_LORA,)), ("mla_kv_norm", (KV_LORA,)),
        ("lru_w_rgate", (N_TOK_HEADS, HEAD_DIM, HEAD_DIM)),
        ("lru_w_igate", (N_TOK_HEADS, HEAD_DIM, HEAD_DIM)),
        ("ln_g", (2, D_MODEL)), ("ln_b", (2, D_MODEL))]


def _shard_shape(shape, axis):
    return tuple(d // N_DEV if a == axis else d for a, d in enumerate(shape))


def _size(shape):
    return math.prod(shape)


BIG_ROWS = sum(_size(s) for _, s, _ in BIG) // N_DEV // LANES
SMALL_ROWS = SUBLANES


def _pack_rows(flat_parts, rows):
    flat = jnp.concatenate([p.reshape(-1) for p in flat_parts])
    return jnp.pad(flat, (0, rows * LANES - flat.shape[0])).reshape(rows, LANES)


def _to_chunks(full, axis):
    shape = full.shape
    split = shape[:axis] + (N_DEV, shape[axis] // N_DEV) + shape[axis + 1:]
    return jnp.moveaxis(full.reshape(split), axis, 0).reshape(N_DEV, -1)


def _from_chunks(chunks, shape, axis):
    sh = _shard_shape(shape, axis)
    t = chunks.reshape((N_DEV,) + sh)
    t = jnp.moveaxis(t, 0, axis)
    return t.reshape(shape)


def _split_flat(flat2d, table):
    out, off = [], 0
    for size in table:
        out.append(flat2d[:, off:off + size])
        off += size
    return out


def _win0_to_padded(w):
    z = lambda n: jnp.zeros((w.shape[0], n), w.dtype)
    return jnp.concatenate([w[:, 0:640], z(KR_LANE), w[:, 640:672],
                            z(LANES - KR_LANE - QK_ROPE), w[:, 672:1952]], axis=1)


def _win0_from_padded(wp):
    k0 = ZA_KR + KR_LANE
    return jnp.concatenate([wp[:, 0:640], wp[:, k0:k0 + QK_ROPE], wp[:, ZA_W:ZP]], axis=1)


def _pad_heads(w, per_head, lo, hi):
    t = w.reshape(w.shape[0], N_TOK_HEADS, per_head)[:, :, lo:hi]
    t = jnp.pad(t, ((0, 0), (0, 0), (0, HEAD_PAD - (hi - lo))))
    return t.reshape(w.shape[0], QKV_PAD)


def _unpad_heads(wp, width):
    return wp.reshape(wp.shape[0], N_TOK_HEADS, HEAD_PAD)[:, :, :width]


def _block_diag(w):
    eye = jnp.eye(N_TOK_HEADS, dtype=w.dtype)
    return (w[:, :, None, :] * eye[:, None, :, None]).reshape(TOK_WIDTH, TOK_WIDTH)


def _diag_blocks(d):
    t = d.reshape(N_TOK_HEADS, HEAD_DIM, N_TOK_HEADS, HEAD_DIM)
    return jnp.stack([t[g, :, g, :] for g in range(N_TOK_HEADS)])


def _rope_tables(positions):
    half = QK_ROPE // 2
    inv_freq = ROPE_THETA ** (-jnp.arange(half, dtype=F32) / half)
    ang = positions.astype(F32)[:, None] * inv_freq
    cos, sin = jnp.cos(ang), jnp.sin(ang)
    s = positions.shape[0]
    one, zero = jnp.ones((s, QK_NOPE), F32), jnp.zeros((s, half), F32)
    tail = jnp.zeros((s, HEAD_PAD - QK_DIM), F32)
    znope = jnp.zeros((s, QK_NOPE), F32)
    c = jnp.concatenate([one, cos, cos, tail], axis=1)
    sa = jnp.concatenate([znope, -sin, zero, tail], axis=1)
    sb = jnp.concatenate([znope, zero, sin, tail], axis=1)
    return c, sa, sb


def _local_step(x, mem, positions, tgt, wts, ts, tatt, early_exchange):
    bf = lambda t: t.astype(BF16)
    win0 = _win0_to_padded(wts["mla_w_in"])
    wuq = _pad_heads(wts["mla_w_uq"], QK_DIM, 0, QK_DIM)
    wukv = jnp.concatenate([_pad_heads(wts["mla_w_ukv"], 2 * HEAD_DIM, 0, QK_NOPE),
                            _pad_heads(wts["mla_w_ukv"], 2 * HEAD_DIM, QK_NOPE, 2 * HEAD_DIM)],
                           axis=1)
    win1 = wts["lru_w_in"]
    wmkv, wout = wts["w_mem_kv"], wts["w_out"]
    gq = wts["mla_q_norm"].reshape(1, Q_LORA)
    gkv = wts["mla_kv_norm"].reshape(1, KV_LORA)
    ln_g, ln_b = wts["ln_g"], wts["ln_b"]
    wr, wi = bf(_block_diag(wts["lru_w_rgate"])), bf(_block_diag(wts["lru_w_igate"]))
    cw8 = jnp.pad(wts["lru_conv_w"], ((0, SUBLANES - CONV_W), (0, 0)))
    vec8 = jnp.pad(jnp.stack([wts["lru_conv_b"], wts["lru_b_rgate"], wts["lru_b_igate"],
                              wts["lru_lambda"]]), ((0, SUBLANES - 4), (0, 0)))
    tabs = _rope_tables(positions)
    tmem = mem.shape[0]

    za0, zg0, q, k, v = _mla_prep_fwd(x, win0, tabs, gq, gkv, wuq, wukv, ts)
    o, lse = _flash_fwd(q, k, v, tatt, FWD_HEADS)
    mkv0, = _rowmm(mem, wmkv[0], [2 * MEM_WIDTH], "mem_kv0", tmem)
    cat0, y0, pre0, h1 = _mix_out_fwd(o, zg0, mkv0, wout[0], x, ln_g[0:1], ln_b[0:1], None,
                                      0, MIX_WIDTH, True, "mix_out_fwd0", ts)
    del o
    u1, zg1, hs = _lru_fwd(h1, win1, cw8, vec8, wr, wi, ts)
    mkv1, = _rowmm(mem, wmkv[1], [2 * MEM_WIDTH], "mem_kv1", tmem)
    cat1, y1, dpre1, dgb1, loss8 = _mix_out_fwd(hs, zg1, mkv1, wout[1], h1, ln_g[1:2],
                                                ln_b[1:2], tgt, 0, MIX_WIDTH, False,
                                                "mix_out_loss", ts)
    loss = loss8[0, 0]

    dzg1, dhs, dmkv1, dwout1 = _gate_mem_bwd(dpre1, y1, wout[1].T, cat1, zg1, mkv1, None,
                                             0, MIX_WIDTH, "gate_mem_bwd1", ts)
    du, dwr, dwi, dvec = _lru_bwd(u1, dhs, hs, cw8, vec8, wr, wi, wr.T, wi.T, ts)
    dpre0, dwin1, dgb0 = _linear_bwd(h1, [du, dzg1], [0, ZA_W], win1.T, dpre1,
                                     (pre0, ln_g[0:1]), "in_proj_bwd1", ts)
    dwmkv1 = _wgrad_small(mem, dmkv1, "mem_kv_bwd1")
    dzg0, do, dmkv0, dwout0, stats = _gate_mem_bwd(dpre0, y0, wout[0].T, cat0, zg0, mkv0, lse,
                                                   0, MIX_WIDTH, "gate_mem_bwd0", ts)
    dwmkv0 = _wgrad_small(mem, dmkv0, "mem_kv_bwd0")
    early = {
        "lru_w_in": dwin1,
        "lru_small": dvec,
        "lru_w_rgate": _diag_blocks(dwr).reshape(TOK_WIDTH, HEAD_DIM),
        "lru_w_igate": _diag_blocks(dwi).reshape(TOK_WIDTH, HEAD_DIM),
        "w_mem_kv": [dwmkv0, dwmkv1],
        "w_out": [dwout0, dwout1],
    }
    (dq, dk, dv), got_early = _flash_bwd(q, k, v, stats, do, tatt, BWD_HEADS,
                                         early_exchange(early))
    dza, dzk, dwuq_p, dwukv_p, dg = _mla_prep_bwd(za0, dq, dk, dv, tabs, gq, gkv,
                                                  wuq.T, wukv.T, ts)
    gx, dwin0_p = _linear_bwd(x, [dza, dzk, dzg0], [ZA_CQ, ZA_KR, ZA_W], win0.T, dpre0,
                              None, "in_proj_bwd0", ts)

    dwukv = jnp.concatenate([_unpad_heads(dwukv_p[:, :QKV_PAD], HEAD_DIM),
                             _unpad_heads(dwukv_p[:, QKV_PAD:], HEAD_DIM)], axis=2)
    zrow = jnp.zeros((1, D_MODEL), F32)
    gains = jnp.pad(dg[0:1], ((0, 0), (0, D_MODEL - Q_LORA - KV_LORA)))
    small_repl = jnp.concatenate([dgb0[0:2], dgb1[0:2], gains,
                                  loss * jnp.ones((1, D_MODEL), F32), zrow, zrow], axis=0)
    late = {
        "mla_w_in": _win0_from_padded(dwin0_p),
        "mla_w_uq": _unpad_heads(dwuq_p, QK_DIM).reshape(Q_LORA, N_TOK_HEADS * QK_DIM),
        "mla_w_ukv": dwukv.reshape(KV_LORA, N_TOK_HEADS * 2 * HEAD_DIM),
        "small_repl": small_repl,
    }
    return gx, early, got_early, late


WEIGHT_ORDER = ["mla_w_in", "mla_q_norm", "mla_w_uq", "mla_kv_norm", "mla_w_ukv", "lru_w_in",
                "lru_conv_w", "lru_conv_b", "lru_w_rgate", "lru_b_rgate", "lru_w_igate",
                "lru_b_igate", "lru_lambda", "w_mem_kv", "w_out", "ln_g", "ln_b"]


def kernel(x, mem, positions, mla_w_in, mla_q_norm, mla_w_uq, mla_kv_norm, mla_w_ukv, lru_w_in, lru_conv_w, lru_conv_b, lru_w_rgate, lru_b_rgate, lru_w_igate, lru_b_igate, lru_lambda, w_mem_kv, w_out, ln_g, ln_b, loss_target, m_mla_w_in, m_mla_q_norm, m_mla_w_uq, m_mla_kv_norm, m_mla_w_ukv, m_lru_w_in, m_lru_conv_w, m_lru_conv_b, m_lru_w_rgate, m_lru_b_rgate, m_lru_w_igate, m_lru_b_igate, m_lru_lambda, m_w_mem_kv, m_w_out, m_ln_g, m_ln_b, v_mla_w_in, v_mla_q_norm, v_mla_w_uq, v_mla_kv_norm, v_mla_w_ukv, v_lru_w_in, v_lru_conv_w, v_lru_conv_b, v_lru_w_rgate, v_lru_b_rgate, v_lru_w_igate, v_lru_b_igate, v_lru_lambda, v_w_mem_kv, v_w_out, v_ln_g, v_ln_b):
    w_in = dict(mla_w_in=mla_w_in, mla_q_norm=mla_q_norm, mla_w_uq=mla_w_uq,
                mla_kv_norm=mla_kv_norm, mla_w_ukv=mla_w_ukv, lru_w_in=lru_w_in,
                lru_conv_w=lru_conv_w, lru_conv_b=lru_conv_b, lru_w_rgate=lru_w_rgate,
                lru_b_rgate=lru_b_rgate, lru_w_igate=lru_w_igate, lru_b_igate=lru_b_igate,
                lru_lambda=lru_lambda, w_mem_kv=w_mem_kv, w_out=w_out, ln_g=ln_g, ln_b=ln_b)
    m_in = dict(mla_w_in=m_mla_w_in, mla_q_norm=m_mla_q_norm, mla_w_uq=m_mla_w_uq,
                mla_kv_norm=m_mla_kv_norm, mla_w_ukv=m_mla_w_ukv, lru_w_in=m_lru_w_in,
                lru_conv_w=m_lru_conv_w, lru_conv_b=m_lru_conv_b, lru_w_rgate=m_lru_w_rgate,
                lru_b_rgate=m_lru_b_rgate, lru_w_igate=m_lru_w_igate, lru_b_igate=m_lru_b_igate,
                lru_lambda=m_lru_lambda, w_mem_kv=m_w_mem_kv, w_out=m_w_out, ln_g=m_ln_g,
                ln_b=m_ln_b)
    v_in = dict(mla_w_in=v_mla_w_in, mla_q_norm=v_mla_q_norm, mla_w_uq=v_mla_w_uq,
                mla_kv_norm=v_mla_kv_norm, mla_w_ukv=v_mla_w_ukv, lru_w_in=v_lru_w_in,
                lru_conv_w=v_lru_conv_w, lru_conv_b=v_lru_conv_b, lru_w_rgate=v_lru_w_rgate,
                lru_b_rgate=v_lru_b_rgate, lru_w_igate=v_lru_w_igate, lru_b_igate=v_lru_b_igate,
                lru_lambda=v_lru_lambda, w_mem_kv=v_w_mem_kv, w_out=v_w_out, ln_g=v_ln_g,
                ln_b=v_ln_b)
    s = x.shape[1]
    ts = min(ROW_BLOCK, s)
    tatt = min(ATT_BLOCK, s)
    big_sizes = [_size(sh) // N_DEV for _, sh, _ in BIG]
    small_sizes = [_size(sh) // N_DEV for _, sh, _ in SMALL]

    big_local = _pack_rows([w_in[n] for n, _, _ in BIG], BIG_ROWS).astype(BF16)
    small_local = _pack_rows([w_in[n] for n, _, _ in SMALL], SMALL_ROWS)
    big_all, small_all = _allgather([big_local, small_local])
    wts = {}
    for (n, sh, ax), part in zip(BIG, _split_flat(big_all.reshape(N_DEV, -1), big_sizes)):
        wts[n] = _from_chunks(part, sh, ax)
    for (n, sh, ax), part in zip(SMALL, _split_flat(small_all.reshape(N_DEV, -1), small_sizes)):
        wts[n] = _from_chunks(part, sh, ax)
    for n, sh in REPL:
        wts[n] = w_in[n].reshape(sh)

    def early_exchange(g):
        small_chunks = jnp.moveaxis(g["lru_small"].reshape(SUBLANES, N_DEV, -1), 1, 0)
        sends = [(g["lru_w_in"], "cols"),
                 (g["w_mem_kv"][0], "rows"), (g["w_mem_kv"][1], "rows"),
                 (g["w_out"][0], "rows"), (g["w_out"][1], "rows"),
                 (small_chunks, "chunks"), (g["lru_w_rgate"], "all"), (g["lru_w_igate"], "all")]
        return _Exchange([a for a, _ in sends], [k for _, k in sends])

    gx, _, got_early, late = _local_step(x[0], mem[0], positions[0], loss_target[0], wts,
                                         ts, tatt, early_exchange)

    def chunked(name, shape):
        w = shape[1] // N_DEV
        return _to_chunks(late[name], 1).reshape(N_DEV, shape[0], w).astype(BF16)

    got_late = _exchange_grads(
        [chunked("mla_w_in", (D_MODEL, MLA_IN)),
         chunked("mla_w_uq", (Q_LORA, N_TOK_HEADS * QK_DIM)),
         chunked("mla_w_ukv", (KV_LORA, N_TOK_HEADS * 2 * HEAD_DIM)), late["small_repl"]],
        ["chunks", "chunks", "chunks", "all"], "exchange_grads")
    got = list(got_late[:3]) + list(got_early) + [got_late[3]]

    def small_sharded(d):
        return jnp.concatenate([d["lru_conv_w"].reshape(CONV_W, -1), d["lru_conv_b"],
                                d["lru_b_rgate"], d["lru_b_igate"], d["lru_lambda"]], axis=0)

    def small_replicated(d):
        gains = jnp.concatenate([d["mla_q_norm"], d["mla_kv_norm"]], axis=1)
        gains = jnp.pad(gains, ((0, 0), (0, D_MODEL - gains.shape[1])))
        return jnp.concatenate([d["ln_g"][0:1], d["ln_b"][0:1], d["ln_g"][1:2], d["ln_b"][1:2],
                                gains, jnp.zeros((3, D_MODEL), F32)], axis=0)

    def flat2(d, name):
        t = d[name]
        return t.reshape(-1, t.shape[-1])

    def update(parts, view, name):
        return _adamw(parts, view(w_in), view(m_in), view(v_in), "adamw_" + name)

    res = {}
    for idx, name in [(0, "mla_w_in"), (1, "mla_w_uq"), (2, "mla_w_ukv"), (3, "lru_w_in"),
                      (9, "lru_w_rgate"), (10, "lru_w_igate")]:
        res[name] = update([got[idx]], functools.partial(flat2, name=name), name)
    res["w_mem_kv"] = update([got[4], got[5]], functools.partial(flat2, name="w_mem_kv"),
                             "w_mem_kv")
    res["w_out"] = update([got[6], got[7]], functools.partial(flat2, name="w_out"), "w_out")
    res_ss = update([got[8]], small_sharded, "small_sharded")
    res_sr = update([got[11]], small_replicated, "small_replicated")
    loss = res_sr[0][5, 0]

    result = [loss, gx.reshape(x.shape)]
    for kind in range(4):
        ss, sr = res_ss[kind], res_sr[kind]
        out = {n: res[n][kind].reshape(w_in[n].shape) for n in res}
        out["lru_conv_w"] = ss[0:CONV_W].reshape(w_in["lru_conv_w"].shape)
        out["lru_conv_b"], out["lru_b_rgate"] = ss[4:5], ss[5:6]
        out["lru_b_igate"], out["lru_lambda"] = ss[6:7], ss[7:8]
        out["ln_g"] = jnp.concatenate([sr[0:1], sr[2:3]], axis=0)
        out["ln_b"] = jnp.concatenate([sr[1:2], sr[3:4]], axis=0)
        out["mla_q_norm"] = sr[4:5, 0:Q_LORA]
        out["mla_kv_norm"] = sr[4:5, Q_LORA:Q_LORA + KV_LORA]
        result += [out[n] for n in WEIGHT_ORDER]
    return tuple(result)
```

```python
import functools
import math

import jax
import jax.numpy as jnp
from jax import lax
from jax.experimental import pallas as pl
from jax.experimental.pallas import tpu as pltpu

F32 = jnp.float32
BF16 = jnp.bfloat16

D_MODEL = 1024
MEM_LEN = 256
HEAD_DIM = 64
N_TOK_HEADS = 12
N_MEM_HEADS = 4
TOK_WIDTH = 768
MEM_WIDTH = 256
MIX_WIDTH = 1024
Q_LORA = 384
KV_LORA = 256
QK_NOPE = 64
QK_ROPE = 32
QK_DIM = 96
ROPE_THETA = 10000.0
CONV_W = 4
LRU_C = 8.0
ALPHA = (2.0 * 2) ** 0.25
NORM_EPS = 1e-6
MLA_IN = 1952
LRU_IN = 2048
ADAM_LR = 0.001
ADAM_B1 = 0.9
ADAM_B2 = 0.999
ADAM_EPS = 1e-08
ADAM_WD = 0.01
ADAM_STEP = 10

N_DEV = 8
LANES = 128
SUBLANES = 8
HEAD_PAD = 128
QKV_PAD = N_TOK_HEADS * HEAD_PAD
ZP = 2048
ZA_W = TOK_WIDTH
ZG_W = MIX_WIDTH + MEM_WIDTH
ZA_CQ, ZA_CKV, ZA_KR = 0, 384, 640
KR_LANE = 64

ROW_BLOCK = 512
ATT_BLOCK = 512
LOOKAHEAD = 3
FWD_HEADS = 12
BWD_HEADS = 4
VMEM_LIMIT = 56 * 1024 * 1024
NEG_BIG = -1e30
STRIP = 32
SCAN_SEGMENTS = 4
LOG2E = math.log2(math.e)


def _cp(n_axes):
    return pltpu.CompilerParams(dimension_semantics=("arbitrary",) * n_axes,
                                vmem_limit_bytes=VMEM_LIMIT)


def _dot(a, b):
    return jnp.dot(a, b, preferred_element_type=F32)


def _dot_nt(a, b):
    return lax.dot_general(a, b, (((1,), (1,)), ((), ())), preferred_element_type=F32)


def _dot_tn(a, b):
    return lax.dot_general(a, b, (((0,), (0,)), ((), ())), preferred_element_type=F32)


def _sigmoid(t):
    return 1.0 / (1.0 + jnp.exp(-t))


def _lane(shape):
    return lax.broadcasted_iota(jnp.int32, shape, len(shape) - 1)


def _full(shape):
    nd = len(shape)
    return pl.BlockSpec(shape, lambda *_: (0,) * nd)


def _rows(ts, width, col=0):
    return pl.BlockSpec((ts, width), lambda i: (i, col))


def _heads(ts):
    return pl.BlockSpec((N_TOK_HEADS, ts, HEAD_PAD), lambda i: (0, i, 0))


def _rowmm(x, w, widths, name, ts):
    s, k = x.shape
    n = w.shape[1]
    offs = [sum(widths[:a]) for a in range(len(widths))]

    def body(x_ref, w_ref, *o_refs):
        res = _dot(x_ref[...].astype(BF16), w_ref[...])
        for o_ref, off, wd in zip(o_refs, offs, widths):
            o_ref[...] = res[:, off:off + wd]

    return pl.pallas_call(
        body, grid=(s // ts,),
        in_specs=[_rows(ts, k), _full((k, n))],
        out_specs=[_rows(ts, wd) for wd in widths],
        out_shape=[jax.ShapeDtypeStruct((s, wd), F32) for wd in widths],
        name=name, compiler_params=_cp(1))(x, w)


def _rms_parts(t):
    rs = lax.rsqrt(jnp.mean(t * t, axis=-1, keepdims=True) + NORM_EPS)
    return rs


def _rope(t, c, sa, sb):
    return t * c + pltpu.roll(t, LANES - 16, 1) * sa + pltpu.roll(t, 16, 1) * sb


def _rope_t(d, c, sa, sb):
    return d * c + pltpu.roll(d * sa, 16, 1) + pltpu.roll(d * sb, LANES - 16, 1)


def _mla_prep_fwd(x, win, tabs, gq, gkv, wuq, wukv, ts):
    s = x.shape[0]

    def body(x_ref, win_ref, c_ref, sa_ref, sb_ref, gq_ref, gkv_ref, wuq_ref, wukv_ref,
             z_ref, zg_ref, q_ref, k_ref, v_ref):
        zfull = _dot(x_ref[...].astype(BF16), win_ref[...])
        z_ref[...] = zfull[:, 0:ZA_W]
        zg_ref[...] = zfull[:, ZA_W:ZP]
        cq = zfull[:, ZA_CQ:ZA_CQ + Q_LORA]
        ckv = zfull[:, ZA_CKV:ZA_CKV + KV_LORA]
        kr = zfull[:, ZA_KR:ZA_KR + LANES]
        cqn = cq * _rms_parts(cq) * gq_ref[...]
        ckvn = ckv * _rms_parts(ckv) * gkv_ref[...]
        q = _dot(cqn.astype(BF16), wuq_ref[...])
        kv = _dot(ckvn.astype(BF16), wukv_ref[...])
        c, sa, sb = c_ref[...], sa_ref[...], sb_ref[...]
        krope = _rope(kr, c, sa, sb)
        pad_lane = _lane((ts, HEAD_PAD)) >= HEAD_DIM
        for h in range(N_TOK_HEADS):
            sl = slice(h * HEAD_PAD, (h + 1) * HEAD_PAD)
            q_ref[h] = _rope(q[:, sl], c, sa, sb).astype(BF16)
            k_ref[h] = (kv[:, sl] + krope).astype(BF16)
            vh = kv[:, QKV_PAD + h * HEAD_PAD:QKV_PAD + (h + 1) * HEAD_PAD]
            v_ref[h] = jnp.where(pad_lane, 1.0, vh).astype(BF16)

    out = jax.ShapeDtypeStruct((N_TOK_HEADS, s, HEAD_PAD), BF16)
    return pl.pallas_call(
        body, grid=(s // ts,),
        in_specs=[_rows(ts, D_MODEL), _full((D_MODEL, ZP)),
                  _rows(ts, LANES), _rows(ts, LANES), _rows(ts, LANES),
                  _full((1, Q_LORA)), _full((1, KV_LORA)),
                  _full((Q_LORA, QKV_PAD)), _full((KV_LORA, 2 * QKV_PAD))],
        out_specs=[_rows(ts, ZA_W), _rows(ts, ZG_W)] + [_heads(ts)] * 3,
        out_shape=[jax.ShapeDtypeStruct((s, ZA_W), F32), jax.ShapeDtypeStruct((s, ZG_W), F32),
                   out, out, out],
        name="mla_prep_fwd", compiler_params=_cp(1))(x, win, *tabs, gq, gkv, wuq, wukv)


def _mla_prep_bwd(z0, dq, dk, dv, tabs, gq, gkv, wuq_t, wukv_t, ts):
    s = z0.shape[0]

    def body(z_ref, dq_ref, dk_ref, dv_ref, c_ref, sa_ref, sb_ref, gq_ref, gkv_ref,
             wuqt_ref, wukvt_ref, dza_ref, dzk_ref, dwuq_ref, dwukv_ref, dg_ref):
        @pl.when(pl.program_id(0) == 0)
        def _():
            dwuq_ref[...] = jnp.zeros_like(dwuq_ref)
            dwukv_ref[...] = jnp.zeros_like(dwukv_ref)
            dg_ref[...] = jnp.zeros_like(dg_ref)

        cq = z_ref[:, ZA_CQ:ZA_CQ + Q_LORA]
        ckv = z_ref[:, ZA_CKV:ZA_CKV + KV_LORA]
        rq, rkv = _rms_parts(cq), _rms_parts(ckv)
        gq_, gkv_ = gq_ref[...], gkv_ref[...]
        cqn = (cq * rq * gq_).astype(BF16)
        ckvn = (ckv * rkv * gkv_).astype(BF16)
        c, sa, sb = c_ref[...], sa_ref[...], sb_ref[...]
        dqp, dksum = [], None
        for h in range(N_TOK_HEADS):
            dqp.append(_rope_t(dq_ref[h], c, sa, sb))
            dksum = dk_ref[h] if dksum is None else dksum + dk_ref[h]
        dqp = jnp.concatenate(dqp, axis=1).astype(BF16)
        lane = _lane(dksum.shape)
        dzk_ref[...] = jnp.where((lane >= KR_LANE) & (lane < KR_LANE + QK_ROPE),
                                 _rope_t(dksum, c, sa, sb), 0.0).astype(BF16)
        dkv = jnp.concatenate([dk_ref[h].astype(BF16) for h in range(N_TOK_HEADS)]
                              + [dv_ref[h] for h in range(N_TOK_HEADS)], axis=1)
        dcqn = _dot(dqp, wuqt_ref[...])
        dckvn = _dot(dkv, wukvt_ref[...])
        dwuq_ref[...] += _dot_tn(cqn, dqp)
        dwukv_ref[...] += _dot_tn(ckvn, dkv)
        dg_ref[0:1, 0:Q_LORA] += jnp.sum(dcqn * cq * rq, axis=0, keepdims=True)
        dg_ref[0:1, Q_LORA:Q_LORA + KV_LORA] += jnp.sum(dckvn * ckv * rkv, axis=0, keepdims=True)
        wq = dcqn * gq_
        wkv = dckvn * gkv_
        dcq = rq * wq - cq * (rq * rq * rq) * jnp.mean(wq * cq, axis=-1, keepdims=True)
        dckv = rkv * wkv - ckv * (rkv * rkv * rkv) * jnp.mean(wkv * ckv, axis=-1, keepdims=True)
        dza_ref[:, 0:Q_LORA] = dcq.astype(BF16)
        dza_ref[:, Q_LORA:Q_LORA + KV_LORA] = dckv.astype(BF16)

    na = Q_LORA + KV_LORA
    return pl.pallas_call(
        body, grid=(s // ts,),
        in_specs=[_rows(ts, ZA_W), _heads(ts), _heads(ts), _heads(ts),
                  _rows(ts, LANES), _rows(ts, LANES), _rows(ts, LANES),
                  _full((1, Q_LORA)), _full((1, KV_LORA)),
                  _full((QKV_PAD, Q_LORA)), _full((2 * QKV_PAD, KV_LORA))],
        out_specs=[_rows(ts, na), _rows(ts, LANES), _full((Q_LORA, QKV_PAD)),
                   _full((KV_LORA, 2 * QKV_PAD)), _full((SUBLANES, na))],
        out_shape=[jax.ShapeDtypeStruct((s, na), BF16), jax.ShapeDtypeStruct((s, LANES), BF16),
                   jax.ShapeDtypeStruct((Q_LORA, QKV_PAD), F32),
                   jax.ShapeDtypeStruct((KV_LORA, 2 * QKV_PAD), F32),
                   jax.ShapeDtypeStruct((SUBLANES, na), F32)],
        name="mla_prep_bwd", compiler_params=_cp(1))(
            z0, dq, dk, dv, *tabs, gq, gkv, wuq_t, wukv_t)


def _causal_pairs(nb, by_key):
    if by_key:
        pairs = [(i, j) for j in range(nb) for i in range(j, nb)]
    else:
        pairs = [(i, j) for i in range(nb) for j in range(i + 1)]
    return (jnp.array([p[0] for p in pairs], jnp.int32),
            jnp.array([p[1] for p in pairs], jnp.int32))


def _flash_fwd(q, k, v, t, nh):
    s = q.shape[1]
    itab, jtab = _causal_pairs(s // t, False)
    c2 = LOG2E / math.sqrt(QK_DIM)

    def body(it_ref, jt_ref, q_ref, k_ref, v_ref, o_ref, lse_ref, m_scr, acc_scr):
        pair = pl.program_id(1)
        i, j = it_ref[pair], jt_ref[pair]

        @pl.when(j == 0)
        def _():
            m_scr[...] = jnp.full_like(m_scr, NEG_BIG)
            acc_scr[...] = jnp.zeros_like(acc_scr)

        def softmax_strips(masked, hs, sc, row0):
            ps, als = [], []
            for r0 in range(0, sc.shape[0], STRIP):
                rows = slice(row0 + r0, row0 + r0 + STRIP)
                ch = [sc[r0:r0 + STRIP, n * LANES:(n + 1) * LANES] * c2
                      for n in range(sc.shape[1] // LANES)]
                if masked:
                    rr = row0 + r0 + lax.broadcasted_iota(jnp.int32, (STRIP, LANES), 0)
                    cc = lax.broadcasted_iota(jnp.int32, (STRIP, LANES), 1)
                    ch = [jnp.where(cc + n * LANES <= rr, c_, NEG_BIG) for n, c_ in enumerate(ch)]
                mx = ch[0]
                for c_ in ch[1:]:
                    mx = jnp.maximum(mx, c_)
                m_prev = m_scr[hs, rows, :]
                m_next = jnp.maximum(m_prev, jnp.max(mx, axis=-1, keepdims=True))
                ps.append(jnp.concatenate(
                    [jnp.exp2(c_ - m_next).astype(BF16) for c_ in ch], axis=1))
                als.append(jnp.exp2(m_prev - m_next))
                m_scr[hs, rows, :] = m_next
            return jnp.concatenate(ps, axis=0), jnp.concatenate(als, axis=0)

        def run(masked, parts):
            def scores_of(hs):
                return [_dot_nt(q_ref[hs, r0:r0 + nr, :], k_ref[hs, 0:nk, :])
                        for r0, nr, nk in parts]

            ahead = min(LOOKAHEAD, nh)
            scores = [scores_of(hs) for hs in range(ahead)]
            for hs in range(nh):
                if hs + ahead < nh:
                    scores.append(scores_of(hs + ahead))
                for (r0, nr, nk), sc in zip(parts, scores[hs]):
                    p, alpha = softmax_strips(masked, hs, sc, r0)
                    acc_scr[hs, r0:r0 + nr, :] = (alpha * acc_scr[hs, r0:r0 + nr, :]
                                                  + _dot(p, v_ref[hs, 0:nk, :]))

        @pl.when(j < i)
        def _():
            run(False, [(0, t, t)])

        @pl.when(j == i)
        def _():
            run(True, [(0, t, t)])
            for h in range(nh):
                acc = acc_scr[h]
                l = acc[:, HEAD_DIM:HEAD_DIM + 1]
                o_ref[h] = jnp.where(_lane(acc.shape) < HEAD_DIM, acc / l, 0.0)
                lse_ref[h] = m_scr[h] + jnp.log2(l)

    qspec = pl.BlockSpec((nh, t, HEAD_PAD), lambda h, p, it, jt: (h, it[p], 0))
    kspec = pl.BlockSpec((nh, t, HEAD_PAD), lambda h, p, it, jt: (h, jt[p], 0))
    out = jax.ShapeDtypeStruct((N_TOK_HEADS, s, HEAD_PAD), F32)
    return pl.pallas_call(
        body,
        grid_spec=pltpu.PrefetchScalarGridSpec(
            num_scalar_prefetch=2, grid=(N_TOK_HEADS // nh, itab.shape[0]),
            in_specs=[qspec, kspec, kspec], out_specs=[qspec, qspec],
            scratch_shapes=[pltpu.VMEM((nh, t, HEAD_PAD), F32)] * 2),
        out_shape=[out, out],
        name="flash_fwd", compiler_params=_cp(2))(itab, jtab, q, k, v)


def _flash_bwd(q, k, v, stats, do, t, nh, ex):
    s = q.shape[1]
    nb = s // t
    itab, jtab = _causal_pairs(nb, True)
    npairs = itab.shape[0]
    ngroups = N_TOK_HEADS // nh
    scale = 1.0 / math.sqrt(QK_DIM)
    c2 = LOG2E * scale
    nx = ex.n if ex is not None else 0
    ex_arrays, ex_out_shape, ex_scratch = (
        (ex.arrays, ex.out_shape, ex.scratch) if ex is not None else ([], [], []))

    def body(it_ref, jt_ref, q_ref, k_ref, v_ref, st_ref, do_ref, *rest):
        ex_in, rest = rest[:nx], rest[nx:]
        dq_ref, dk_ref, dv_ref = rest[:3]
        ex_out, rest = rest[3:3 + nx], rest[3 + nx:]
        dk_scr, dv_scr = rest[:2]
        ex_sems = rest[2:]
        pair = pl.program_id(1)
        i, j = it_ref[pair], jt_ref[pair]
        rows_i = pl.ds(pl.multiple_of(i * t, t), t)

        if nx:
            @pl.when(jnp.logical_and(pl.program_id(0) == 0, pair == 0))
            def _():
                for cp in ex.copies(ex_in, ex_out, ex_sems):
                    cp.start()

        @pl.when(i == j)
        def _():
            dk_scr[...] = jnp.zeros_like(dk_scr)
            dv_scr[...] = jnp.zeros_like(dv_scr)

        @pl.when(j == 0)
        def _():
            dq_ref[:, rows_i, :] = jnp.zeros((nh, t, HEAD_PAD), F32)

        def prob_strips(masked, h, sct, dpt, k0, q0):
            ps, dss = [], []
            for r0 in range(0, sct.shape[0], STRIP):
                rows = slice(r0, r0 + STRIP)
                if masked:
                    kk = k0 + r0 + lax.broadcasted_iota(jnp.int32, (STRIP, LANES), 0)
                    qq = q0 + lax.broadcasted_iota(jnp.int32, (STRIP, LANES), 1)
                pcs, dcs = [], []
                for n in range(sct.shape[1] // LANES):
                    cols = slice(n * LANES, (n + 1) * LANES)
                    qcols = slice(q0 + n * LANES, q0 + (n + 1) * LANES)
                    x = sct[rows, cols] * c2
                    if masked:
                        x = jnp.where(kk <= qq + n * LANES, x, NEG_BIG)
                    p = jnp.exp2(x - st_ref[h, 0:1, qcols])
                    pcs.append(p.astype(BF16))
                    dcs.append((p * (dpt[rows, cols] - st_ref[h, 1:2, qcols]) * scale).astype(BF16))
                ps.append(jnp.concatenate(pcs, axis=1))
                dss.append(jnp.concatenate(dcs, axis=1))
            return jnp.concatenate(ps, axis=0), jnp.concatenate(dss, axis=0)

        def run(masked, parts):
            def scores_of(h):
                return [(_dot_nt(k_ref[h, k0:k0 + nk, :], q_ref[h, q0:q0 + nq, :]),
                         _dot_nt(v_ref[h, k0:k0 + nk, :], do_ref[h, q0:q0 + nq, :]))
                        for k0, nk, q0, nq in parts]

            ahead = min(LOOKAHEAD, nh)
            scores = [scores_of(h) for h in range(ahead)]
            for h in range(nh):
                if h + ahead < nh:
                    scores.append(scores_of(h + ahead))
                for (k0, nk, q0, nq), (sct, dpt) in zip(parts, scores[h]):
                    pt, dst = prob_strips(masked, h, sct, dpt, k0, q0)
                    dv_scr[h, k0:k0 + nk, :] += _dot(pt, do_ref[h, q0:q0 + nq, :])
                    dk_scr[h, k0:k0 + nk, :] += _dot(dst, q_ref[h, q0:q0 + nq, :])
                    rows = pl.ds(pl.multiple_of(i * t + q0, t // 2), nq)
                    dq_ref[h, rows, :] += _dot_tn(dst, k_ref[h, k0:k0 + nk, :])

        @pl.when(i > j)
        def _():
            run(False, [(0, t, 0, t)])

        @pl.when(i == j)
        def _():
            run(True, [(0, t // 2, 0, t), (t // 2, t // 2, t // 2, t // 2)])

        @pl.when(i == nb - 1)
        def _():
            dk_ref[...] = dk_scr[...]
            dv_ref[...] = dv_scr[...].astype(BF16)

        if nx:
            @pl.when(jnp.logical_and(pl.program_id(0) == ngroups - 1, pair == npairs - 1))
            def _():
                for cp in ex.copies(ex_in, ex_out, ex_sems):
                    cp.wait()

    qspec = pl.BlockSpec((nh, t, HEAD_PAD), lambda h, p, it, jt: (h, it[p], 0))
    kspec = pl.BlockSpec((nh, t, HEAD_PAD), lambda h, p, it, jt: (h, jt[p], 0))
    dqspec = pl.BlockSpec((nh, s, HEAD_PAD), lambda h, p, it, jt: (h, 0, 0))
    stspec = pl.BlockSpec((nh, 2, t), lambda h, p, it, jt: (h, 0, it[p]))
    out = jax.ShapeDtypeStruct((N_TOK_HEADS, s, HEAD_PAD), F32)
    res = pl.pallas_call(
        body,
        grid_spec=pltpu.PrefetchScalarGridSpec(
            num_scalar_prefetch=2, grid=(ngroups, npairs),
            in_specs=[qspec, kspec, kspec, stspec, qspec] + [ANY] * nx,
            out_specs=[dqspec, kspec, kspec] + [ANY] * nx,
            scratch_shapes=[pltpu.VMEM((nh, t, HEAD_PAD), F32)] * 2 + ex_scratch),
        out_shape=[out, out, jax.ShapeDtypeStruct(out.shape, BF16)] + ex_out_shape,
        name="flash_bwd", compiler_params=_cp(2))(itab, jtab, q, k, v, stats, do, *ex_arrays)
    return res[:3], res[3:]


def _mem_probs(qp, kp, hh):
    lane = _lane(qp.shape)
    keep = (lane < HEAD_DIM) if hh == 0 else (lane >= HEAD_DIM)
    qh = jnp.where(keep, qp, 0.0).astype(BF16)
    sc = _dot_nt(qh, kp) * (1.0 / math.sqrt(HEAD_DIM))
    e = jnp.exp(sc - jnp.max(sc, axis=-1, keepdims=True))
    return e / jnp.sum(e, axis=-1, keepdims=True), keep


def _mix_out_fwd(tok, z, memkv, w, h, g, b, tgt, g0, q0, padded, name, ts):
    s = z.shape[0]
    zw = z.shape[1]
    tok_spec = _heads(ts) if padded else _rows(ts, TOK_WIDTH)
    with_loss = tgt is not None

    def body(*refs):
        tok_ref, z_ref, mkv_ref, w_ref, h_ref, g_ref, b_ref = refs[:7]
        if with_loss:
            t_ref, cat_ref, y_ref, dpre_ref, dgb_ref, loss_ref = refs[7:]
        else:
            cat_ref, y_ref, pre_ref, out_ref = refs[7:]
        if padded:
            for p in range(N_TOK_HEADS // 2):
                cat_ref[:, p * LANES:(p + 1) * LANES] = (
                    tok_ref[2 * p] + pltpu.roll(tok_ref[2 * p + 1], HEAD_DIM, 1))
        else:
            cat_ref[:, 0:TOK_WIDTH] = tok_ref[...]
        for pr in range(N_MEM_HEADS // 2):
            sl = slice(pr * LANES, (pr + 1) * LANES)
            qp = z_ref[:, q0 + pr * LANES:q0 + (pr + 1) * LANES]
            kp = mkv_ref[:, sl].astype(BF16)
            vp = mkv_ref[:, MEM_WIDTH + pr * LANES:MEM_WIDTH + (pr + 1) * LANES].astype(BF16)
            outs = []
            for hh in range(2):
                p, _ = _mem_probs(qp, kp, hh)
                outs.append(_dot(p.astype(BF16), vp))
            lane = _lane(outs[0].shape)
            cat_ref[:, TOK_WIDTH + pr * LANES:TOK_WIDTH + (pr + 1) * LANES] = jnp.where(
                lane < HEAD_DIM, outs[0], outs[1])
        gate = z_ref[:, g0:g0 + MIX_WIDTH]
        yb = (cat_ref[...] * (gate * _sigmoid(gate))).astype(BF16)
        y_ref[...] = yb
        pre = ALPHA * h_ref[...] + _dot(yb, w_ref[...])
        xhat, rstd = _ln_stats(pre)
        hout = xhat * g_ref[...] + b_ref[...]
        if with_loss:
            @pl.when(pl.program_id(0) == 0)
            def _():
                loss_ref[...] = jnp.zeros_like(loss_ref)
                dgb_ref[...] = jnp.zeros_like(dgb_ref)
            err = hout - t_ref[...]
            loss_ref[...] += 0.5 * jnp.sum(jnp.mean(err * err, axis=-1, keepdims=True))
            dh = err * (1.0 / D_MODEL)
            dpre_ref[...] = _ln_bwd(dh, xhat, rstd, g_ref[...])
            dgb_ref[0:1, :] += jnp.sum(dh * xhat, axis=0, keepdims=True)
            dgb_ref[1:2, :] += jnp.sum(dh, axis=0, keepdims=True)
        else:
            pre_ref[...] = pre
            out_ref[...] = hout

    act = jax.ShapeDtypeStruct((s, D_MODEL), F32)
    in_specs = [tok_spec, _rows(ts, zw), _full((MEM_LEN, 2 * MEM_WIDTH)),
                _full((MIX_WIDTH, D_MODEL)), _rows(ts, D_MODEL),
                _full((1, D_MODEL)), _full((1, D_MODEL))]
    out_specs = [_rows(ts, MIX_WIDTH)] * 2
    out_shape = [jax.ShapeDtypeStruct((s, MIX_WIDTH), F32),
                 jax.ShapeDtypeStruct((s, MIX_WIDTH), BF16)]
    args = [tok, z, memkv, w, h, g, b]
    if with_loss:
        in_specs.append(_rows(ts, D_MODEL))
        out_specs += [_rows(ts, D_MODEL), _full((SUBLANES, D_MODEL)), _full((SUBLANES, LANES))]
        out_shape += [act, jax.ShapeDtypeStruct((SUBLANES, D_MODEL), F32),
                      jax.ShapeDtypeStruct((SUBLANES, LANES), F32)]
        args.append(tgt)
    else:
        out_specs += [_rows(ts, D_MODEL)] * 2
        out_shape += [act, act]
    return pl.pallas_call(
        body, grid=(s // ts,), in_specs=in_specs, out_specs=out_specs, out_shape=out_shape,
        name=name, compiler_params=_cp(1))(*args)


def _gate_mem_bwd(dpre, y, w_t, cat, z, memkv, lse, g0, q0, name, ts):
    s = z.shape[0]
    zw = z.shape[1]
    padded = lse is not None
    gq_w = MIX_WIDTH + MEM_WIDTH

    def body(*refs):
        if padded:
            (dpre_ref, y_ref, wt_ref, cat_ref, z_ref, mkv_ref, lse_ref,
             dzg_ref, dtok_ref, dmkv_ref, dw_ref, st_ref) = refs
        else:
            (dpre_ref, y_ref, wt_ref, cat_ref, z_ref, mkv_ref,
             dzg_ref, dtok_ref, dmkv_ref, dw_ref) = refs

        @pl.when(pl.program_id(0) == 0)
        def _():
            dmkv_ref[...] = jnp.zeros_like(dmkv_ref)
            dw_ref[...] = jnp.zeros_like(dw_ref)

        dpb = dpre_ref[...].astype(BF16)
        dy_ = _dot(dpb, wt_ref[...])
        dw_ref[...] += _dot_tn(y_ref[...], dpb)
        gate = z_ref[:, g0:g0 + MIX_WIDTH]
        sg = _sigmoid(gate)
        dzg_ref[:, 0:MIX_WIDTH] = (dy_ * cat_ref[...]
                                   * (sg * (1.0 + gate * (1.0 - sg)))).astype(BF16)
        dcat = dy_ * (gate * sg)
        if padded:
            low = _lane((ts, LANES)) < HEAD_DIM
            for p in range(N_TOK_HEADS // 2):
                d = dcat[:, p * LANES:(p + 1) * LANES]
                prod = d * cat_ref[:, p * LANES:(p + 1) * LANES]
                first = jnp.sum(jnp.where(low, prod, 0.0), axis=-1, keepdims=True)
                second = jnp.sum(jnp.where(low, 0.0, prod), axis=-1, keepdims=True)
                dtok_ref[2 * p] = jnp.where(low, d, 0.0).astype(BF16)
                dtok_ref[2 * p + 1] = jnp.where(low, pltpu.roll(d, HEAD_DIM, 1), 0.0).astype(BF16)
                for hh, delta in ((2 * p, first), (2 * p + 1, second)):
                    both = jnp.where(low, lse_ref[hh], delta).T
                    st_ref[hh, 0:1, :] = both[0:1, :]
                    st_ref[hh, 1:2, :] = both[HEAD_DIM:HEAD_DIM + 1, :]
        else:
            dtok_ref[...] = dcat[:, 0:TOK_WIDTH]
        for pr in range(N_MEM_HEADS // 2):
            sl = slice(pr * LANES, (pr + 1) * LANES)
            vsl = slice(MEM_WIDTH + pr * LANES, MEM_WIDTH + (pr + 1) * LANES)
            qp = z_ref[:, q0 + pr * LANES:q0 + (pr + 1) * LANES]
            qpb = qp.astype(BF16)
            kp = mkv_ref[:, sl].astype(BF16)
            vp = mkv_ref[:, vsl].astype(BF16)
            dmo = dcat[:, TOK_WIDTH + pr * LANES:TOK_WIDTH + (pr + 1) * LANES]
            dqp = None
            for hh in range(2):
                p, keep = _mem_probs(qp, kp, hh)
                do_h = jnp.where(keep, dmo, 0.0).astype(BF16)
                dmkv_ref[:, vsl] += _dot_tn(p.astype(BF16), do_h)
                dp = _dot_nt(do_h, vp)
                ds = (p * (dp - jnp.sum(dp * p, axis=-1, keepdims=True))
                      * (1.0 / math.sqrt(HEAD_DIM))).astype(BF16)
                dqh = jnp.where(keep, _dot(ds, kp), 0.0)
                dqp = dqh if dqp is None else dqp + dqh
                dkh = _dot_tn(ds, qpb)
                klane = _lane(dkh.shape)
                kkeep = (klane < HEAD_DIM) if hh == 0 else (klane >= HEAD_DIM)
                dmkv_ref[:, sl] += jnp.where(kkeep, dkh, 0.0)
            dzg_ref[:, MIX_WIDTH + pr * LANES:MIX_WIDTH + (pr + 1) * LANES] = dqp.astype(BF16)

    in_specs = [_rows(ts, D_MODEL), _rows(ts, MIX_WIDTH), _full((D_MODEL, MIX_WIDTH)),
                _rows(ts, MIX_WIDTH), _rows(ts, zw), _full((MEM_LEN, 2 * MEM_WIDTH))]
    out_specs = [_rows(ts, gq_w), _heads(ts) if padded else _rows(ts, TOK_WIDTH),
                 _full((MEM_LEN, 2 * MEM_WIDTH)), _full((MIX_WIDTH, D_MODEL))]
    heads_shape = (N_TOK_HEADS, s, HEAD_PAD)
    out_shape = [jax.ShapeDtypeStruct((s, gq_w), BF16),
                 jax.ShapeDtypeStruct(heads_shape, BF16) if padded
                 else jax.ShapeDtypeStruct((s, TOK_WIDTH), F32),
                 jax.ShapeDtypeStruct((MEM_LEN, 2 * MEM_WIDTH), F32),
                 jax.ShapeDtypeStruct((MIX_WIDTH, D_MODEL), F32)]
    args = [dpre, y, w_t, cat, z, memkv]
    if padded:
        in_specs.append(_heads(ts))
        out_specs.append(pl.BlockSpec((N_TOK_HEADS, 2, ts), lambda i: (0, 0, i)))
        out_shape.append(jax.ShapeDtypeStruct((N_TOK_HEADS, 2, s), F32))
        args.append(lse)
    return pl.pallas_call(
        body, grid=(s // ts,), in_specs=in_specs, out_specs=out_specs, out_shape=out_shape,
        name=name, compiler_params=_cp(1))(*args)


def _ln_stats(pre):
    mu = jnp.mean(pre, axis=-1, keepdims=True)
    d = pre - mu
    rstd = lax.rsqrt(jnp.mean(d * d, axis=-1, keepdims=True) + NORM_EPS)
    return d * rstd, rstd


def _ln_bwd(dh, xhat, rstd, g):
    dxh = dh * g
    return rstd * (dxh - jnp.mean(dxh, axis=-1, keepdims=True)
                   - xhat * jnp.mean(dxh * xhat, axis=-1, keepdims=True))


def _linear_bwd(x, dys, offs, w_t, resid, ln, name, ts):
    s, kdim = x.shape
    n = w_t.shape[0]
    widths = [d.shape[1] for d in dys]
    npieces = len(dys)
    with_ln = ln is not None

    def body(*refs):
        x_ref = refs[0]
        dy_refs = refs[1:1 + npieces]
        if with_ln:
            wt_ref, r_ref, pre_ref, g_ref, dx_ref, dw_ref, dgb_ref = refs[1 + npieces:]
        else:
            wt_ref, r_ref, dx_ref, dw_ref = refs[1 + npieces:]

        @pl.when(pl.program_id(0) == 0)
        def _():
            dw_ref[...] = jnp.zeros_like(dw_ref)
            if with_ln:
                dgb_ref[...] = jnp.zeros_like(dgb_ref)

        xb = x_ref[...].astype(BF16)
        dx = ALPHA * r_ref[...]
        for dy_ref, off, wd in zip(dy_refs, offs, widths):
            dyb = dy_ref[...].astype(BF16)
            dx = dx + _dot(dyb, wt_ref[off:off + wd, :])
            dw_ref[:, off:off + wd] += _dot_tn(xb, dyb)
        if with_ln:
            xhat, rstd = _ln_stats(pre_ref[...])
            dx_ref[...] = _ln_bwd(dx, xhat, rstd, g_ref[...])
            dgb_ref[0:1, :] += jnp.sum(dx * xhat, axis=0, keepdims=True)
            dgb_ref[1:2, :] += jnp.sum(dx, axis=0, keepdims=True)
        else:
            dx_ref[...] = dx

    in_specs = ([_rows(ts, kdim)] + [_rows(ts, wd) for wd in widths]
                + [_full((n, kdim)), _rows(ts, kdim)])
    out_specs = [_rows(ts, kdim), _full((kdim, n))]
    out_shape = [jax.ShapeDtypeStruct((s, kdim), F32), jax.ShapeDtypeStruct((kdim, n), F32)]
    args = [x, *dys, w_t, resid]
    if with_ln:
        in_specs += [_rows(ts, kdim), _full((1, kdim))]
        out_specs.append(_full((SUBLANES, kdim)))
        out_shape.append(jax.ShapeDtypeStruct((SUBLANES, kdim), F32))
        args += list(ln)
    return pl.pallas_call(
        body, grid=(s // ts,), in_specs=in_specs, out_specs=out_specs, out_shape=out_shape,
        name=name, compiler_params=_cp(1))(*args)


def _wgrad_small(x, dy, name):
    def body(x_ref, dy_ref, dw_ref):
        dw_ref[...] = _dot_tn(x_ref[...].astype(BF16), dy_ref[...].astype(BF16))

    return pl.pallas_call(
        body, out_shape=jax.ShapeDtypeStruct((x.shape[1], dy.shape[1]), F32),
        name=name, compiler_params=pltpu.CompilerParams(vmem_limit_bytes=VMEM_LIMIT))(x, dy)


def _shift_down(u, carry8, k):
    if k == 0:
        return u
    rolled = pltpu.roll(u, k, 0)
    row = lax.broadcasted_iota(jnp.int32, carry8.shape, 0)
    top = jnp.where(row < k, pltpu.roll(carry8, k, 0), rolled[0:SUBLANES])
    return jnp.concatenate([top, rolled[SUBLANES:]], axis=0)


def _shift_up(u, carry8, k):
    if k == 0:
        return u
    n = u.shape[0]
    rolled = pltpu.roll(u, n - k, 0)
    row = lax.broadcasted_iota(jnp.int32, carry8.shape, 0)
    bot = jnp.where(row >= SUBLANES - k, pltpu.roll(carry8, SUBLANES - k, 0),
                    rolled[n - SUBLANES:])
    return jnp.concatenate([rolled[:n - SUBLANES], bot], axis=0)


def _neg_expm1(t):
    e = jnp.exp(t)
    em1 = e - 1.0
    safe = jnp.where(e == 1.0, 1.0, jnp.log(e))
    return -jnp.where(e == 1.0, t, jnp.where(em1 == -1.0, -1.0, em1 * t / safe))


def _lru_gates(u, carry8, cw_ref, vec_ref, wr_ref, wi_ref):
    taps = [_shift_down(u, carry8, k) for k in range(CONV_W)]
    xc = vec_ref[0:1, :] + cw_ref[3:4, :] * u
    for k in range(1, CONV_W):
        xc = xc + cw_ref[3 - k:4 - k, :] * taps[k]
    xb = xc.astype(BF16)
    r = _sigmoid(_dot(xb, wr_ref[...]) + vec_ref[1:2, :])
    ig = _sigmoid(_dot(xb, wi_ref[...]) + vec_ref[2:3, :])
    nlam = -vec_ref[3:4, :]
    softplus = jnp.maximum(nlam, 0.0) + jnp.log(1.0 + jnp.exp(-jnp.abs(nlam)))
    cneg = -LRU_C * softplus
    log_a = cneg * r
    a = jnp.exp(log_a)
    sq = jnp.sqrt(_neg_expm1(2.0 * log_a))
    return xc, r, ig, cneg, a, sq, taps


def _chained_scan(a_ref, b_ref, out_ref, cum_scr, x_in):
    rows_total, w = a_ref.shape
    nseg = SCAN_SEGMENTS
    seg = rows_total // nseg

    def step(t, carry):
        xs, ps = carry
        new_x, new_p = [], []
        for sg in range(nseg):
            row = pl.ds(sg * seg + t, 1)
            a = a_ref[row, :]
            x = a * xs[sg] + b_ref[row, :]
            out_ref[row, :] = x
            new_x.append(x)
            if sg > 0:
                p = a * ps[sg - 1]
                cum_scr[row, :] = p
                new_p.append(p)
        return tuple(new_x), tuple(new_p)

    zero, one = jnp.zeros((1, w), F32), jnp.ones((1, w), F32)
    xs, _ = lax.fori_loop(0, seg, step, ((x_in,) + (zero,) * (nseg - 1), (one,) * (nseg - 1)))
    x_prev = xs[0]
    for sg in range(1, nseg):
        rows = slice(sg * seg, (sg + 1) * seg)
        out_ref[rows, :] = out_ref[rows, :] + cum_scr[rows, :] * x_prev
        x_prev = out_ref[(sg + 1) * seg - 1:(sg + 1) * seg, :]
    return x_prev


def _lru_fwd(x, win, cw8, vec8, wr, wi, ts):
    s = x.shape[0]

    def body(x_ref, win_ref, cw_ref, vec_ref, wr_ref, wi_ref, u_ref, zg_ref, hs_ref,
             cu_scr, ch_scr, a_scr, gx_scr, cum_scr):
        @pl.when(pl.program_id(0) == 0)
        def _():
            cu_scr[...] = jnp.zeros_like(cu_scr)
            ch_scr[...] = jnp.zeros_like(ch_scr)

        zfull = _dot(x_ref[...].astype(BF16), win_ref[...])
        u = zfull[:, 0:ZA_W]
        u_ref[...] = u
        zg_ref[...] = zfull[:, ZA_W:ZP]
        xc, _, ig, _, a, sq, _ = _lru_gates(u, cu_scr[...], cw_ref, vec_ref, wr_ref, wi_ref)
        a_scr[...] = a
        gx_scr[...] = sq * (ig * xc)
        ch_scr[0:1, :] = _chained_scan(a_scr, gx_scr, hs_ref, cum_scr, ch_scr[0:1, :])
        cu_scr[...] = u[ts - SUBLANES:, :]

    w = TOK_WIDTH
    return pl.pallas_call(
        body, grid=(s // ts,),
        in_specs=[_rows(ts, D_MODEL), _full((D_MODEL, ZP)),
                  _full((SUBLANES, w)), _full((SUBLANES, w)), _full((w, w)), _full((w, w))],
        out_specs=[_rows(ts, w), _rows(ts, ZG_W), _rows(ts, w)],
        out_shape=[jax.ShapeDtypeStruct((s, w), F32), jax.ShapeDtypeStruct((s, ZG_W), F32),
                   jax.ShapeDtypeStruct((s, w), F32)],
        scratch_shapes=[pltpu.VMEM((SUBLANES, w), F32), pltpu.VMEM((SUBLANES, w), F32)]
                       + [pltpu.VMEM((ts, w), F32)] * 3,
        name="lru_fwd", compiler_params=_cp(1))(x, win, cw8, vec8, wr, wi)


def _lru_bwd(z1, dhs, hs, cw8, vec8, wr, wi, wr_t, wi_t, ts):
    s = z1.shape[0]
    nb = s // ts
    w = TOK_WIDTH
    tiles = ts // SUBLANES

    def body(u_ref, up_ref, dhs_ref, hs_ref, hsp_ref, cw_ref, vec_ref, wr_ref, wi_ref,
             wrt_ref, wit_ref, du_ref, dwr_ref, dwi_ref, dvec_ref,
             cc_scr, cd_scr, a_scr, dh_scr):
        i = pl.program_id(0)

        @pl.when(i == 0)
        def _():
            cc_scr[...] = jnp.zeros_like(cc_scr)
            cd_scr[...] = jnp.zeros_like(cd_scr)
            dwr_ref[...] = jnp.zeros_like(dwr_ref)
            dwi_ref[...] = jnp.zeros_like(dwi_ref)
            dvec_ref[...] = jnp.zeros_like(dvec_ref)

        u = u_ref[...]
        first = i == nb - 1
        carry8 = jnp.where(first, 0.0, up_ref[...])
        xc, r, ig, cneg, a, sq, taps = _lru_gates(u, carry8, cw_ref, vec_ref, wr_ref, wi_ref)
        a_scr[...] = a

        def step(n, c):
            t = ts - 1 - n
            dh = dhs_ref[pl.ds(t, 1), :] + c
            dh_scr[pl.ds(t, 1), :] = dh
            return a_scr[pl.ds(t, 1), :] * dh

        cc_scr[0:1, :] = lax.fori_loop(0, ts, step, cc_scr[0:1, :])
        dh = dh_scr[...]
        hprev = _shift_down(hs_ref[...], jnp.where(first, 0.0, hsp_ref[...]), 1)
        ix = ig * xc
        dix = dh * sq
        dlog_a = dh * hprev * a - (dh * ix) * (a * a) / sq
        dpr = (dlog_a * cneg) * r * (1.0 - r)
        dpi = (dix * xc) * ig * (1.0 - ig)
        dprb, dpib = dpr.astype(BF16), dpi.astype(BF16)
        xb = xc.astype(BF16)
        dwr_ref[...] += _dot_tn(xb, dprb)
        dwi_ref[...] += _dot_tn(xb, dpib)
        dxc = dix * ig + _dot(dprb, wrt_ref[...]) + _dot(dpib, wit_ref[...])
        for k in range(CONV_W):
            dvec_ref[3 - k:4 - k, :] += jnp.sum(dxc * taps[k], axis=0, keepdims=True)
        dvec_ref[4:5, :] += jnp.sum(dxc, axis=0, keepdims=True)
        dvec_ref[5:6, :] += jnp.sum(dpr, axis=0, keepdims=True)
        dvec_ref[6:7, :] += jnp.sum(dpi, axis=0, keepdims=True)
        dvec_ref[7:8, :] += (jnp.sum(dlog_a * r, axis=0, keepdims=True)
                             * (LRU_C * _sigmoid(-vec_ref[3:4, :])))
        nxt = cd_scr[...]
        du = cw_ref[3:4, :] * dxc
        for k in range(1, CONV_W):
            du = du + cw_ref[3 - k:4 - k, :] * _shift_up(dxc, nxt, k)
        du_ref[...] = du.astype(BF16)
        cd_scr[...] = dxc[0:SUBLANES, :]

    rev = lambda i: (nb - 1 - i, 0)
    prev8 = lambda i: (jnp.maximum((nb - 1 - i) * tiles - 1, 0), 0)
    blk = pl.BlockSpec((ts, w), rev)
    before = pl.BlockSpec((SUBLANES, w), prev8)
    scr = pltpu.VMEM((ts, w), F32)
    return pl.pallas_call(
        body, grid=(nb,),
        in_specs=[blk, before, blk, blk, before,
                  _full((SUBLANES, w)), _full((SUBLANES, w)),
                  _full((w, w)), _full((w, w)), _full((w, w)), _full((w, w))],
        out_specs=[blk, _full((w, w)), _full((w, w)), _full((SUBLANES, w))],
        out_shape=[jax.ShapeDtypeStruct((s, w), BF16), jax.ShapeDtypeStruct((w, w), F32),
                   jax.ShapeDtypeStruct((w, w), F32), jax.ShapeDtypeStruct((SUBLANES, w), F32)],
        scratch_shapes=[pltpu.VMEM((SUBLANES, w), F32), pltpu.VMEM((SUBLANES, w), F32),
                        scr, scr],
        name="lru_bwd", compiler_params=_cp(1))(
            z1, z1, dhs, hs, hs, cw8, vec8, wr, wi, wr_t, wi_t)


def _adamw(parts, w, m, v, name):
    n = len(parts)
    rows_per = parts[0].shape[1]

    def body(*refs):
        p_refs = refs[:n]
        w_ref, m_ref, v_ref, g_ref, d_ref, nm_ref, nv_ref = refs[n:]
        for l, p_ref in enumerate(p_refs):
            rows = slice(l * rows_per, (l + 1) * rows_per)
            g = p_ref[0].astype(F32)
            for dev in range(1, N_DEV):
                g = g + p_ref[dev].astype(F32)
            g_ref[rows, :] = g
            nm = ADAM_B1 * m_ref[rows, :] + (1.0 - ADAM_B1) * g
            nv = ADAM_B2 * v_ref[rows, :] + (1.0 - ADAM_B2) * (g * g)
            m_hat = nm / (1.0 - ADAM_B1 ** ADAM_STEP)
            v_hat = nv / (1.0 - ADAM_B2 ** ADAM_STEP)
            d_ref[rows, :] = -ADAM_LR * (m_hat / (jnp.sqrt(v_hat) + ADAM_EPS)
                                         + ADAM_WD * w_ref[rows, :])
            nm_ref[rows, :] = nm
            nv_ref[rows, :] = nv

    out = jax.ShapeDtypeStruct(w.shape, F32)
    return pl.pallas_call(
        body, out_shape=[out] * 4, name=name,
        compiler_params=pltpu.CompilerParams(vmem_limit_bytes=VMEM_LIMIT))(*parts, w, m, v)


ANY = pl.BlockSpec(memory_space=pl.ANY)
MESH = pl.DeviceIdType.MESH


def _slot(p):
    return 4 * p[0] + 2 * p[1] + p[2]


def _allgather(xs):
    n = len(xs)

    def body(*refs):
        x_refs, o_refs = refs[:n], refs[n:2 * n]
        send_sems, recv_sems, local_sems = refs[2 * n:]
        x, y, c = lax.axis_index("x"), lax.axis_index("y"), lax.axis_index("c")
        me, sibling = (x, y, c), (x, y, 1 - c)
        chips = [(1 - x, y), (x, 1 - y), (1 - x, 1 - y)]

        def copy(a, k, block, to, from_input=False):
            dst = o_refs[a].at[_slot(block)]
            return pltpu.make_async_remote_copy(
                src_ref=x_refs[a] if from_input else dst, dst_ref=dst,
                send_sem=send_sems.at[a, k], recv_sem=recv_sems.at[a, k],
                device_id=to, device_id_type=MESH)

        mine = [pltpu.make_async_copy(x_refs[a], o_refs[a].at[_slot(me)], local_sems.at[a])
                for a in range(n)]
        for cp in mine:
            cp.start()
        first = []
        for a in range(n):
            first.append(copy(a, 0, me, sibling, True))
            first += [copy(a, 1 + j, me, (*chip, c), True) for j, chip in enumerate(chips)]
        for cp in first:
            cp.start()
        passed = []
        for j, chip in enumerate(chips):
            for a in range(n):
                copy(a, 1 + j, (*chip, c), me).wait_recv()
                cp = copy(a, 4 + j, (*chip, c), sibling)
                cp.start()
                passed.append(cp)
        for a in range(n):
            copy(a, 0, sibling, me).wait_recv()
            for j, chip in enumerate(chips):
                copy(a, 4 + j, (*chip, 1 - c), me).wait_recv()
        for cp in first + passed:
            cp.wait_send()
        for cp in mine:
            cp.wait()

    return pl.pallas_call(
        body,
        out_shape=[jax.ShapeDtypeStruct((N_DEV,) + t.shape, t.dtype) for t in xs],
        in_specs=[ANY] * n, out_specs=[ANY] * n,
        scratch_shapes=[pltpu.SemaphoreType.DMA((n, 7)), pltpu.SemaphoreType.DMA((n, 7)),
                        pltpu.SemaphoreType.DMA((n,))],
        name="allgather_weights")(*xs)


class _Exchange:
    def __init__(self, arrays, kinds):
        self.arrays, self.kinds, self.n = list(arrays), list(kinds), len(arrays)
        self.shapes = [self._part_shape(a, k) for a, k in zip(arrays, kinds)]
        self.out_shape = [jax.ShapeDtypeStruct((N_DEV,) + shp, a.dtype)
                          for shp, a in zip(self.shapes, arrays)]
        self.scratch = [pltpu.SemaphoreType.DMA((self.n, N_DEV - 1)),
                        pltpu.SemaphoreType.DMA((self.n, N_DEV - 1)),
                        pltpu.SemaphoreType.DMA((self.n,))]

    @staticmethod
    def _part_shape(arr, kind):
        if kind == "chunks":
            return arr.shape[1:]
        if kind == "cols":
            return (arr.shape[0], arr.shape[1] // N_DEV)
        if kind == "rows":
            return (arr.shape[0] // N_DEV, arr.shape[1])
        return arr.shape

    def copies(self, in_refs, out_refs, sems):
        send_sems, recv_sems, local_sems = sems
        x, y, c = lax.axis_index("x"), lax.axis_index("y"), lax.axis_index("c")
        me = _slot((x, y, c))

        def part(a, dev):
            ref, kind, shp = in_refs[a], self.kinds[a], self.shapes[a]
            if kind == "chunks":
                return ref.at[dev]
            if kind == "cols":
                return ref.at[:, pl.ds(pl.multiple_of(dev * shp[1], LANES), shp[1])]
            if kind == "rows":
                return ref.at[pl.ds(pl.multiple_of(dev * shp[0], SUBLANES), shp[0]), :]
            return ref

        cps = [pltpu.make_async_copy(part(a, me), out_refs[a].at[me], local_sems.at[a])
               for a in range(self.n)]
        for rel in range(1, N_DEV):
            peer = (x ^ (rel >> 2), y ^ ((rel >> 1) & 1), c ^ (rel & 1))
            for a in range(self.n):
                cps.append(pltpu.make_async_remote_copy(
                    src_ref=part(a, _slot(peer)), dst_ref=out_refs[a].at[me],
                    send_sem=send_sems.at[a, rel - 1], recv_sem=recv_sems.at[a, rel - 1],
                    device_id=peer, device_id_type=MESH))
        return cps


def _exchange_grads(arrays, kinds, name):
    ex = _Exchange(arrays, kinds)
    n = ex.n

    def body(*refs):
        cps = ex.copies(refs[:n], refs[n:2 * n], refs[2 * n:])
        for cp in cps:
            cp.start()
        for cp in cps:
            cp.wait()

    return pl.pallas_call(
        body, out_shape=ex.out_shape, in_specs=[ANY] * n, out_specs=[ANY] * n,
        scratch_shapes=ex.scratch, name=name)(*arrays)


BIG = [("mla_w_in", (D_MODEL, MLA_IN), 1), ("mla_w_uq", (Q_LORA, N_TOK_HEADS * QK_DIM), 1),
       ("mla_w_ukv", (KV_LORA, N_TOK_HEADS * 2 * HEAD_DIM), 1), ("lru_w_in", (D_MODEL, LRU_IN), 1),
       ("w_mem_kv", (2, D_MODEL, 2 * MEM_WIDTH), 1), ("w_out", (2, MIX_WIDTH, D_MODEL), 1)]
SMALL = [("lru_conv_w", (CONV_W, TOK_WIDTH), 1), ("lru_conv_b", (TOK_WIDTH,), 0),
         ("lru_b_rgate", (TOK_WIDTH,), 0), ("lru_b_igate", (TOK_WIDTH,), 0),
         ("lru_lambda", (TOK_WIDTH,), 0)]
REPL = [("mla_q_norm", (Q_LORA,)), ("mla_kv_norm", (KV_LORA,)),
        ("lru_w_rgate", (N_TOK_HEADS, HEAD_DIM, HEAD_DIM)),
        ("lru_w_igate", (N_TOK_HEADS, HEAD_DIM, HEAD_DIM)),
        ("ln_g", (2, D_MODEL)), ("ln_b", (2, D_MODEL))]


def _shard_shape(shape, axis):
    return tuple(d // N_DEV if a == axis else d for a, d in enumerate(shape))


def _size(shape):
    return math.prod(shape)


BIG_ROWS = sum(_size(s) for _, s, _ in BIG) // N_DEV // LANES
SMALL_ROWS = SUBLANES


def _pack_rows(flat_parts, rows):
    flat = jnp.concatenate([p.reshape(-1) for p in flat_parts])
    return jnp.pad(flat, (0, rows * LANES - flat.shape[0])).reshape(rows, LANES)


def _to_chunks(full, axis):
    shape = full.shape
    split = shape[:axis] + (N_DEV, shape[axis] // N_DEV) + shape[axis + 1:]
    return jnp.moveaxis(full.reshape(split), axis, 0).reshape(N_DEV, -1)


def _from_chunks(chunks, shape, axis):
    sh = _shard_shape(shape, axis)
    t = chunks.reshape((N_DEV,) + sh)
    t = jnp.moveaxis(t, 0, axis)
    return t.reshape(shape)


def _split_flat(flat2d, table):
    out, off = [], 0
    for size in table:
        out.append(flat2d[:, off:off + size])
        off += size
    return out


def _win0_to_padded(w):
    z = lambda n: jnp.zeros((w.shape[0], n), w.dtype)
    return jnp.concatenate([w[:, 0:640], z(KR_LANE), w[:, 640:672],
                            z(LANES - KR_LANE - QK_ROPE), w[:, 672:1952]], axis=1)


def _win0_from_padded(wp):
    k0 = ZA_KR + KR_LANE
    return jnp.concatenate([wp[:, 0:640], wp[:, k0:k0 + QK_ROPE], wp[:, ZA_W:ZP]], axis=1)


def _pad_heads(w, per_head, lo, hi):
    t = w.reshape(w.shape[0], N_TOK_HEADS, per_head)[:, :, lo:hi]
    t = jnp.pad(t, ((0, 0), (0, 0), (0, HEAD_PAD - (hi - lo))))
    return t.reshape(w.shape[0], QKV_PAD)


def _unpad_heads(wp, width):
    return wp.reshape(wp.shape[0], N_TOK_HEADS, HEAD_PAD)[:, :, :width]


def _block_diag(w):
    eye = jnp.eye(N_TOK_HEADS, dtype=w.dtype)
    return (w[:, :, None, :] * eye[:, None, :, None]).reshape(TOK_WIDTH, TOK_WIDTH)


def _diag_blocks(d):
    t = d.reshape(N_TOK_HEADS, HEAD_DIM, N_TOK_HEADS, HEAD_DIM)
    return jnp.stack([t[g, :, g, :] for g in range(N_TOK_HEADS)])


def _rope_tables(positions):
    half = QK_ROPE // 2
    inv_freq = ROPE_THETA ** (-jnp.arange(half, dtype=F32) / half)
    ang = positions.astype(F32)[:, None] * inv_freq
    cos, sin = jnp.cos(ang), jnp.sin(ang)
    s = positions.shape[0]
    one, zero = jnp.ones((s, QK_NOPE), F32), jnp.zeros((s, half), F32)
    tail = jnp.zeros((s, HEAD_PAD - QK_DIM), F32)
    znope = jnp.zeros((s, QK_NOPE), F32)
    c = jnp.concatenate([one, cos, cos, tail], axis=1)
    sa = jnp.concatenate([znope, -sin, zero, tail], axis=1)
    sb = jnp.concatenate([znope, zero, sin, tail], axis=1)
    return c, sa, sb


def _local_step(x, mem, positions, tgt, wts, ts, tatt, early_exchange):
    bf = lambda t: t.astype(BF16)
    win0 = _win0_to_padded(wts["mla_w_in"])
    wuq = _pad_heads(wts["mla_w_uq"], QK_DIM, 0, QK_DIM)
    wukv = jnp.concatenate([_pad_heads(wts["mla_w_ukv"], 2 * HEAD_DIM, 0, QK_NOPE),
                            _pad_heads(wts["mla_w_ukv"], 2 * HEAD_DIM, QK_NOPE, 2 * HEAD_DIM)],
                           axis=1)
    win1 = wts["lru_w_in"]
    wmkv, wout = wts["w_mem_kv"], wts["w_out"]
    gq = wts["mla_q_norm"].reshape(1, Q_LORA)
    gkv = wts["mla_kv_norm"].reshape(1, KV_LORA)
    ln_g, ln_b = wts["ln_g"], wts["ln_b"]
    wr, wi = bf(_block_diag(wts["lru_w_rgate"])), bf(_block_diag(wts["lru_w_igate"]))
    cw8 = jnp.pad(wts["lru_conv_w"], ((0, SUBLANES - CONV_W), (0, 0)))
    vec8 = jnp.pad(jnp.stack([wts["lru_conv_b"], wts["lru_b_rgate"], wts["lru_b_igate"],
                              wts["lru_lambda"]]), ((0, SUBLANES - 4), (0, 0)))
    tabs = _rope_tables(positions)
    tmem = mem.shape[0]

    za0, zg0, q, k, v = _mla_prep_fwd(x, win0, tabs, gq, gkv, wuq, wukv, ts)
    o, lse = _flash_fwd(q, k, v, tatt, FWD_HEADS)
    mkv0, = _rowmm(mem, wmkv[0], [2 * MEM_WIDTH], "mem_kv0", tmem)
    cat0, y0, pre0, h1 = _mix_out_fwd(o, zg0, mkv0, wout[0], x, ln_g[0:1], ln_b[0:1], None,
                                      0, MIX_WIDTH, True, "mix_out_fwd0", ts)
    del o
    u1, zg1, hs = _lru_fwd(h1, win1, cw8, vec8, wr, wi, ts)
    mkv1, = _rowmm(mem, wmkv[1], [2 * MEM_WIDTH], "mem_kv1", tmem)
    cat1, y1, dpre1, dgb1, loss8 = _mix_out_fwd(hs, zg1, mkv1, wout[1], h1, ln_g[1:2],
                                                ln_b[1:2], tgt, 0, MIX_WIDTH, False,
                                                "mix_out_loss", ts)
    loss = loss8[0, 0]

    dzg1, dhs, dmkv1, dwout1 = _gate_mem_bwd(dpre1, y1, wout[1].T, cat1, zg1, mkv1, None,
                                             0, MIX_WIDTH, "gate_mem_bwd1", ts)
    du, dwr, dwi, dvec = _lru_bwd(u1, dhs, hs, cw8, vec8, wr, wi, wr.T, wi.T, ts)
    dpre0, dwin1, dgb0 = _linear_bwd(h1, [du, dzg1], [0, ZA_W], win1.T, dpre1,
                                     (pre0, ln_g[0:1]), "in_proj_bwd1", ts)
    dwmkv1 = _wgrad_small(mem, dmkv1, "mem_kv_bwd1")
    dzg0, do, dmkv0, dwout0, stats = _gate_mem_bwd(dpre0, y0, wout[0].T, cat0, zg0, mkv0, lse,
                                                   0, MIX_WIDTH, "gate_mem_bwd0", ts)
    dwmkv0 = _wgrad_small(mem, dmkv0, "mem_kv_bwd0")
    early = {
        "lru_w_in": dwin1,
        "lru_small": dvec,
        "lru_w_rgate": _diag_blocks(dwr).reshape(TOK_WIDTH, HEAD_DIM),
        "lru_w_igate": _diag_blocks(dwi).reshape(TOK_WIDTH, HEAD_DIM),
        "w_mem_kv": [dwmkv0, dwmkv1],
        "w_out": [dwout0, dwout1],
    }
    (dq, dk, dv), got_early = _flash_bwd(q, k, v, stats, do, tatt, BWD_HEADS,
                                         early_exchange(early))
    dza, dzk, dwuq_p, dwukv_p, dg = _mla_prep_bwd(za0, dq, dk, dv, tabs, gq, gkv,
                                                  wuq.T, wukv.T, ts)
    gx, dwin0_p = _linear_bwd(x, [dza, dzk, dzg0], [ZA_CQ, ZA_KR, ZA_W], win0.T, dpre0,
                              None, "in_proj_bwd0", ts)

    dwukv = jnp.concatenate([_unpad_heads(dwukv_p[:, :QKV_PAD], HEAD_DIM),
                             _unpad_heads(dwukv_p[:, QKV_PAD:], HEAD_DIM)], axis=2)
    zrow = jnp.zeros((1, D_MODEL), F32)
    gains = jnp.pad(dg[0:1], ((0, 0), (0, D_MODEL - Q_LORA - KV_LORA)))
    small_repl = jnp.concatenate([dgb0[0:2], dgb1[0:2], gains,
                                  loss * jnp.ones((1, D_MODEL), F32), zrow, zrow], axis=0)
    late = {
        "mla_w_in": _win0_from_padded(dwin0_p),
        "mla_w_uq": _unpad_heads(dwuq_p, QK_DIM).reshape(Q_LORA, N_TOK_HEADS * QK_DIM),
        "mla_w_ukv": dwukv.reshape(KV_LORA, N_TOK_HEADS * 2 * HEAD_DIM),
        "small_repl": small_repl,
    }
    return gx, early, got_early, late


WEIGHT_ORDER = ["mla_w_in", "mla_q_norm", "mla_w_uq", "mla_kv_norm", "mla_w_ukv", "lru_w_in",
                "lru_conv_w", "lru_conv_b", "lru_w_rgate", "lru_b_rgate", "lru_w_igate",
                "lru_b_igate", "lru_lambda", "w_mem_kv", "w_out", "ln_g", "ln_b"]


def kernel(x, mem, positions, mla_w_in, mla_q_norm, mla_w_uq, mla_kv_norm, mla_w_ukv, lru_w_in, lru_conv_w, lru_conv_b, lru_w_rgate, lru_b_rgate, lru_w_igate, lru_b_igate, lru_lambda, w_mem_kv, w_out, ln_g, ln_b, loss_target, m_mla_w_in, m_mla_q_norm, m_mla_w_uq, m_mla_kv_norm, m_mla_w_ukv, m_lru_w_in, m_lru_conv_w, m_lru_conv_b, m_lru_w_rgate, m_lru_b_rgate, m_lru_w_igate, m_lru_b_igate, m_lru_lambda, m_w_mem_kv, m_w_out, m_ln_g, m_ln_b, v_mla_w_in, v_mla_q_norm, v_mla_w_uq, v_mla_kv_norm, v_mla_w_ukv, v_lru_w_in, v_lru_conv_w, v_lru_conv_b, v_lru_w_rgate, v_lru_b_rgate, v_lru_w_igate, v_lru_b_igate, v_lru_lambda, v_w_mem_kv, v_w_out, v_ln_g, v_ln_b):
    w_in = dict(mla_w_in=mla_w_in, mla_q_norm=mla_q_norm, mla_w_uq=mla_w_uq,
                mla_kv_norm=mla_kv_norm, mla_w_ukv=mla_w_ukv, lru_w_in=lru_w_in,
                lru_conv_w=lru_conv_w, lru_conv_b=lru_conv_b, lru_w_rgate=lru_w_rgate,
                lru_b_rgate=lru_b_rgate, lru_w_igate=lru_w_igate, lru_b_igate=lru_b_igate,
                lru_lambda=lru_lambda, w_mem_kv=w_mem_kv, w_out=w_out, ln_g=ln_g, ln_b=ln_b)
    m_in = dict(mla_w_in=m_mla_w_in, mla_q_norm=m_mla_q_norm, mla_w_uq=m_mla_w_uq,
                mla_kv_norm=m_mla_kv_norm, mla_w_ukv=m_mla_w_ukv, lru_w_in=m_lru_w_in,
                lru_conv_w=m_lru_conv_w, lru_conv_b=m_lru_conv_b, lru_w_rgate=m_lru_w_rgate,
                lru_b_rgate=m_lru_b_rgate, lru_w_igate=m_lru_w_igate, lru_b_igate=m_lru_b_igate,
                lru_lambda=m_lru_lambda, w_mem_kv=m_w_mem_kv, w_out=m_w_out, ln_g=m_ln_g,
                ln_b=m_ln_b)
    v_in = dict(mla_w_in=v_mla_w_in, mla_q_norm=v_mla_q_norm, mla_w_uq=v_mla_w_uq,
                mla_kv_norm=v_mla_kv_norm, mla_w_ukv=v_mla_w_ukv, lru_w_in=v_lru_w_in,
                lru_conv_w=v_lru_conv_w, lru_conv_b=v_lru_conv_b, lru_w_rgate=v_lru_w_rgate,
                lru_b_rgate=v_lru_b_rgate, lru_w_igate=v_lru_w_igate, lru_b_igate=v_lru_b_igate,
                lru_lambda=v_lru_lambda, w_mem_kv=v_w_mem_kv, w_out=v_w_out, ln_g=v_ln_g,
                ln_b=v_ln_b)
    s = x.shape[1]
    ts = min(ROW_BLOCK, s)
    tatt = min(ATT_BLOCK, s)
    big_sizes = [_size(sh) // N_DEV for _, sh, _ in BIG]
    small_sizes = [_size(sh) // N_DEV for _, sh, _ in SMALL]

    big_local = _pack_rows([w_in[n] for n, _, _ in BIG], BIG_ROWS).astype(BF16)
    small_local = _pack_rows([w_in[n] for n, _, _ in SMALL], SMALL_ROWS)
    big_all, small_all = _allgather([big_local, small_local])
    wts = {}
    for (n, sh, ax), part in zip(BIG, _split_flat(big_all.reshape(N_DEV, -1), big_sizes)):
        wts[n] = _from_chunks(part, sh, ax)
    for (n, sh, ax), part in zip(SMALL, _split_flat(small_all.reshape(N_DEV, -1), small_sizes)):
        wts[n] = _from_chunks(part, sh, ax)
    for n, sh in REPL:
        wts[n] = w_in[n].reshape(sh)

    def early_exchange(g):
        small_chunks = jnp.moveaxis(g["lru_small"].reshape(SUBLANES, N_DEV, -1), 1, 0)
        sends = [(g["lru_w_in"], "cols"),
                 (g["w_mem_kv"][0], "rows"), (g["w_mem_kv"][1], "rows"),
                 (g["w_out"][0], "rows"), (g["w_out"][1], "rows"),
                 (small_chunks, "chunks"), (g["lru_w_rgate"], "all"), (g["lru_w_igate"], "all")]
        return _Exchange([a for a, _ in sends], [k for _, k in sends])

    gx, _, got_early, late = _local_step(x[0], mem[0], positions[0], loss_target[0], wts,
                                         ts, tatt, early_exchange)

    def chunked(name, shape):
        w = shape[1] // N_DEV
        return _to_chunks(late[name], 1).reshape(N_DEV, shape[0], w).astype(BF16)

    got_late = _exchange_grads(
        [chunked("mla_w_in", (D_MODEL, MLA_IN)),
         chunked("mla_w_uq", (Q_LORA, N_TOK_HEADS * QK_DIM)),
         chunked("mla_w_ukv", (KV_LORA, N_TOK_HEADS * 2 * HEAD_DIM)), late["small_repl"]],
        ["chunks", "chunks", "chunks", "all"], "exchange_grads")
    got = list(got_late[:3]) + list(got_early) + [got_late[3]]

    def small_sharded(d):
        return jnp.concatenate([d["lru_conv_w"].reshape(CONV_W, -1), d["lru_conv_b"],
                                d["lru_b_rgate"], d["lru_b_igate"], d["lru_lambda"]], axis=0)

    def small_replicated(d):
        gains = jnp.concatenate([d["mla_q_norm"], d["mla_kv_norm"]], axis=1)
        gains = jnp.pad(gains, ((0, 0), (0, D_MODEL - gains.shape[1])))
        return jnp.concatenate([d["ln_g"][0:1], d["ln_b"][0:1], d["ln_g"][1:2], d["ln_b"][1:2],
                                gains, jnp.zeros((3, D_MODEL), F32)], axis=0)

    def flat2(d, name):
        t = d[name]
        return t.reshape(-1, t.shape[-1])

    def update(parts, view, name):
        return _adamw(parts, view(w_in), view(m_in), view(v_in), "adamw_" + name)

    res = {}
    for idx, name in [(0, "mla_w_in"), (1, "mla_w_uq"), (2, "mla_w_ukv"), (3, "lru_w_in"),
                      (9, "lru_w_rgate"), (10, "lru_w_igate")]:
        res[name] = update([got[idx]], functools.partial(flat2, name=name), name)
    res["w_mem_kv"] = update([got[4], got[5]], functools.partial(flat2, name="w_mem_kv"),
                             "w_mem_kv")
    res["w_out"] = update([got[6], got[7]], functools.partial(flat2, name="w_out"), "w_out")
    res_ss = update([got[8]], small_sharded, "small_sharded")
    res_sr = update([got[11]], small_replicated, "small_replicated")
    loss = res_sr[0][5, 0]

    result = [loss, gx.reshape(x.shape)]
    for kind in range(4):
        ss, sr = res_ss[kind], res_sr[kind]
        out = {n: res[n][kind].reshape(w_in[n].shape) for n in res}
        out["lru_conv_w"] = ss[0:CONV_W].reshape(w_in["lru_conv_w"].shape)
        out["lru_conv_b"], out["lru_b_rgate"] = ss[4:5], ss[5:6]
        out["lru_b_igate"], out["lru_lambda"] = ss[6:7], ss[7:8]
        out["ln_g"] = jnp.concatenate([sr[0:1], sr[2:3]], axis=0)
        out["ln_b"] = jnp.concatenate([sr[1:2], sr[3:4]], axis=0)
        out["mla_q_norm"] = sr[4:5, 0:Q_LORA]
        out["mla_kv_norm"] = sr[4:5, Q_LORA:Q_LORA + KV_LORA]
        result += [out[n] for n in WEIGHT_ORDER]
    return tuple(result)
```

```python
import functools
import math

import jax
import jax.numpy as jnp
from jax import lax
from jax.experimental import pallas as pl
from jax.experimental.pallas import tpu as pltpu

F32 = jnp.float32
BF16 = jnp.bfloat16

D_MODEL = 1024
MEM_LEN = 256
HEAD_DIM = 64
N_TOK_HEADS = 12
N_MEM_HEADS = 4
TOK_WIDTH = 768
MEM_WIDTH = 256
MIX_WIDTH = 1024
Q_LORA = 384
KV_LORA = 256
QK_NOPE = 64
QK_ROPE = 32
QK_DIM = 96
ROPE_THETA = 10000.0
CONV_W = 4
LRU_C = 8.0
ALPHA = (2.0 * 2) ** 0.25
NORM_EPS = 1e-6
MLA_IN = 1952
LRU_IN = 2048
ADAM_LR = 0.001
ADAM_B1 = 0.9
ADAM_B2 = 0.999
ADAM_EPS = 1e-08
ADAM_WD = 0.01
ADAM_STEP = 10

N_DEV = 8
LANES = 128
SUBLANES = 8
HEAD_PAD = 128
QKV_PAD = N_TOK_HEADS * HEAD_PAD
ZP = 2048
ZA_W = TOK_WIDTH
ZG_W = MIX_WIDTH + MEM_WIDTH
ZA_CQ, ZA_CKV, ZA_KR = 0, 384, 640
KR_LANE = 64

ROW_BLOCK = 512
ATT_BLOCK = 512
LOOKAHEAD = 3
FWD_HEADS = 12
BWD_HEADS = 4
VMEM_LIMIT = 56 * 1024 * 1024
NEG_BIG = -1e30
STRIP = 32
SCAN_SEGMENTS = 4
LOG2E = math.log2(math.e)


def _cp(n_axes):
    return pltpu.CompilerParams(dimension_semantics=("arbitrary",) * n_axes,
                                vmem_limit_bytes=VMEM_LIMIT)


def _dot(a, b):
    return jnp.dot(a, b, preferred_element_type=F32)


def _dot_nt(a, b):
    return lax.dot_general(a, b, (((1,), (1,)), ((), ())), preferred_element_type=F32)


def _dot_tn(a, b):
    return lax.dot_general(a, b, (((0,), (0,)), ((), ())), preferred_element_type=F32)


def _sigmoid(t):
    return 1.0 / (1.0 + jnp.exp(-t))


def _lane(shape):
    return lax.broadcasted_iota(jnp.int32, shape, len(shape) - 1)


def _full(shape):
    nd = len(shape)
    return pl.BlockSpec(shape, lambda *_: (0,) * nd)


def _rows(ts, width, col=0):
    return pl.BlockSpec((ts, width), lambda i: (i, col))


def _heads(ts):
    return pl.BlockSpec((N_TOK_HEADS, ts, HEAD_PAD), lambda i: (0, i, 0))


def _rowmm(x, w, widths, name, ts):
    s, k = x.shape
    n = w.shape[1]
    offs = [sum(widths[:a]) for a in range(len(widths))]

    def body(x_ref, w_ref, *o_refs):
        res = _dot(x_ref[...].astype(BF16), w_ref[...])
        for o_ref, off, wd in zip(o_refs, offs, widths):
            o_ref[...] = res[:, off:off + wd]

    return pl.pallas_call(
        body, grid=(s // ts,),
        in_specs=[_rows(ts, k), _full((k, n))],
        out_specs=[_rows(ts, wd) for wd in widths],
        out_shape=[jax.ShapeDtypeStruct((s, wd), F32) for wd in widths],
        name=name, compiler_params=_cp(1))(x, w)


def _rms_parts(t):
    rs = lax.rsqrt(jnp.mean(t * t, axis=-1, keepdims=True) + NORM_EPS)
    return rs


def _rope_terms(c, sn):
    first_half = _lane(sn.shape) < KR_LANE + QK_ROPE // 2
    return c, jnp.where(first_half, -sn, 0.0), jnp.where(first_half, 0.0, sn)


def _rope(t, c, sa, sb):
    return t * c + pltpu.roll(t, LANES - 16, 1) * sa + pltpu.roll(t, 16, 1) * sb


def _rope_t(d, c, sa, sb):
    return d * c + pltpu.roll(d * sa, 16, 1) + pltpu.roll(d * sb, LANES - 16, 1)


def _mla_prep_fwd(x, win, tabs, gq, gkv, wuq, wukv, ts):
    s = x.shape[0]

    def body(x_ref, win_ref, c_ref, sn_ref, gq_ref, gkv_ref, wuq_ref, wukv_ref,
             z_ref, zg_ref, q_ref, k_ref, v_ref):
        zfull = _dot(x_ref[...].astype(BF16), win_ref[...])
        z_ref[...] = zfull[:, 0:ZA_W]
        zg_ref[...] = zfull[:, ZA_W:ZP]
        cq = zfull[:, ZA_CQ:ZA_CQ + Q_LORA]
        ckv = zfull[:, ZA_CKV:ZA_CKV + KV_LORA]
        kr = zfull[:, ZA_KR:ZA_KR + LANES]
        cqn = cq * _rms_parts(cq) * gq_ref[...]
        ckvn = ckv * _rms_parts(ckv) * gkv_ref[...]
        q = _dot(cqn.astype(BF16), wuq_ref[...])
        kv = _dot(ckvn.astype(BF16), wukv_ref[...])
        c, sa, sb = _rope_terms(c_ref[...], sn_ref[...])
        krope = _rope(kr, c, sa, sb)
        pad_lane = _lane((ts, HEAD_PAD)) >= HEAD_DIM
        for h in range(N_TOK_HEADS):
            sl = slice(h * HEAD_PAD, (h + 1) * HEAD_PAD)
            q_ref[h] = _rope(q[:, sl], c, sa, sb).astype(BF16)
            k_ref[h] = (kv[:, sl] + krope).astype(BF16)
            vh = kv[:, QKV_PAD + h * HEAD_PAD:QKV_PAD + (h + 1) * HEAD_PAD]
            v_ref[h] = jnp.where(pad_lane, 1.0, vh).astype(BF16)

    out = jax.ShapeDtypeStruct((N_TOK_HEADS, s, HEAD_PAD), BF16)
    return pl.pallas_call(
        body, grid=(s // ts,),
        in_specs=[_rows(ts, D_MODEL), _full((D_MODEL, ZP)),
                  _rows(ts, LANES), _rows(ts, LANES),
                  _full((1, Q_LORA)), _full((1, KV_LORA)),
                  _full((Q_LORA, QKV_PAD)), _full((KV_LORA, 2 * QKV_PAD))],
        out_specs=[_rows(ts, ZA_W), _rows(ts, ZG_W)] + [_heads(ts)] * 3,
        out_shape=[jax.ShapeDtypeStruct((s, ZA_W), F32), jax.ShapeDtypeStruct((s, ZG_W), F32),
                   out, out, out],
        name="mla_prep_fwd", compiler_params=_cp(1))(x, win, *tabs, gq, gkv, wuq, wukv)


def _mla_prep_bwd(z0, dq, dk, dv, tabs, gq, gkv, wuq_t, wukv_t, ts):
    s = z0.shape[0]

    def body(z_ref, dq_ref, dk_ref, dv_ref, c_ref, sn_ref, gq_ref, gkv_ref,
             wuqt_ref, wukvt_ref, dza_ref, dzk_ref, dwuq_ref, dwukv_ref, dg_ref):
        @pl.when(pl.program_id(0) == 0)
        def _():
            dwuq_ref[...] = jnp.zeros_like(dwuq_ref)
            dwukv_ref[...] = jnp.zeros_like(dwukv_ref)
            dg_ref[...] = jnp.zeros_like(dg_ref)

        cq = z_ref[:, ZA_CQ:ZA_CQ + Q_LORA]
        ckv = z_ref[:, ZA_CKV:ZA_CKV + KV_LORA]
        rq, rkv = _rms_parts(cq), _rms_parts(ckv)
        gq_, gkv_ = gq_ref[...], gkv_ref[...]
        cqn = (cq * rq * gq_).astype(BF16)
        ckvn = (ckv * rkv * gkv_).astype(BF16)
        c, sa, sb = _rope_terms(c_ref[...], sn_ref[...])
        dqp, dksum = [], None
        for h in range(N_TOK_HEADS):
            dqp.append(_rope_t(dq_ref[h], c, sa, sb))
            dksum = dk_ref[h] if dksum is None else dksum + dk_ref[h]
        dqp = jnp.concatenate(dqp, axis=1).astype(BF16)
        lane = _lane(dksum.shape)
        dzk_ref[...] = jnp.where((lane >= KR_LANE) & (lane < KR_LANE + QK_ROPE),
                                 _rope_t(dksum, c, sa, sb), 0.0).astype(BF16)
        dkv = jnp.concatenate([dk_ref[h].astype(BF16) for h in range(N_TOK_HEADS)]
                              + [dv_ref[h] for h in range(N_TOK_HEADS)], axis=1)
        dcqn = _dot(dqp, wuqt_ref[...])
        dckvn = _dot(dkv, wukvt_ref[...])
        dwuq_ref[...] += _dot_tn(cqn, dqp)
        dwukv_ref[...] += _dot_tn(ckvn, dkv)
        dg_ref[0:1, 0:Q_LORA] += jnp.sum(dcqn * cq * rq, axis=0, keepdims=True)
        dg_ref[0:1, Q_LORA:Q_LORA + KV_LORA] += jnp.sum(dckvn * ckv * rkv, axis=0, keepdims=True)
        wq = dcqn * gq_
        wkv = dckvn * gkv_
        dcq = rq * wq - cq * (rq * rq * rq) * jnp.mean(wq * cq, axis=-1, keepdims=True)
        dckv = rkv * wkv - ckv * (rkv * rkv * rkv) * jnp.mean(wkv * ckv, axis=-1, keepdims=True)
        dza_ref[:, 0:Q_LORA] = dcq.astype(BF16)
        dza_ref[:, Q_LORA:Q_LORA + KV_LORA] = dckv.astype(BF16)

    na = Q_LORA + KV_LORA
    return pl.pallas_call(
        body, grid=(s // ts,),
        in_specs=[_rows(ts, ZA_W), _heads(ts), _heads(ts), _heads(ts),
                  _rows(ts, LANES), _rows(ts, LANES),
                  _full((1, Q_LORA)), _full((1, KV_LORA)),
                  _full((QKV_PAD, Q_LORA)), _full((2 * QKV_PAD, KV_LORA))],
        out_specs=[_rows(ts, na), _rows(ts, LANES), _full((Q_LORA, QKV_PAD)),
                   _full((KV_LORA, 2 * QKV_PAD)), _full((SUBLANES, na))],
        out_shape=[jax.ShapeDtypeStruct((s, na), BF16), jax.ShapeDtypeStruct((s, LANES), BF16),
                   jax.ShapeDtypeStruct((Q_LORA, QKV_PAD), F32),
                   jax.ShapeDtypeStruct((KV_LORA, 2 * QKV_PAD), F32),
                   jax.ShapeDtypeStruct((SUBLANES, na), F32)],
        name="mla_prep_bwd", compiler_params=_cp(1))(
            z0, dq, dk, dv, *tabs, gq, gkv, wuq_t, wukv_t)


def _causal_pairs(nb, by_key):
    if by_key:
        pairs = [(i, j) for j in range(nb) for i in range(j, nb)]
    else:
        pairs = [(i, j) for i in range(nb) for j in range(i + 1)]
    return (jnp.array([p[0] for p in pairs], jnp.int32),
            jnp.array([p[1] for p in pairs], jnp.int32))


def _flash_fwd(q, k, v, t, nh):
    s = q.shape[1]
    itab, jtab = _causal_pairs(s // t, False)
    c2 = LOG2E / math.sqrt(QK_DIM)

    def body(it_ref, jt_ref, q_ref, k_ref, v_ref, o_ref, lse_ref, m_scr, acc_scr):
        pair = pl.program_id(1)
        i, j = it_ref[pair], jt_ref[pair]

        @pl.when(j == 0)
        def _():
            m_scr[...] = jnp.full_like(m_scr, NEG_BIG)
            acc_scr[...] = jnp.zeros_like(acc_scr)

        def softmax_strips(masked, hs, sc, row0):
            ps, als = [], []
            for r0 in range(0, sc.shape[0], STRIP):
                rows = slice(row0 + r0, row0 + r0 + STRIP)
                ch = [sc[r0:r0 + STRIP, n * LANES:(n + 1) * LANES] * c2
                      for n in range(sc.shape[1] // LANES)]
                if masked:
                    rr = row0 + r0 + lax.broadcasted_iota(jnp.int32, (STRIP, LANES), 0)
                    cc = lax.broadcasted_iota(jnp.int32, (STRIP, LANES), 1)
                    ch = [jnp.where(cc + n * LANES <= rr, c_, NEG_BIG) for n, c_ in enumerate(ch)]
                mx = ch[0]
                for c_ in ch[1:]:
                    mx = jnp.maximum(mx, c_)
                m_prev = m_scr[hs, rows, :]
                m_next = jnp.maximum(m_prev, jnp.max(mx, axis=-1, keepdims=True))
                ps.append(jnp.concatenate(
                    [jnp.exp2(c_ - m_next).astype(BF16) for c_ in ch], axis=1))
                als.append(jnp.exp2(m_prev - m_next))
                m_scr[hs, rows, :] = m_next
            return jnp.concatenate(ps, axis=0), jnp.concatenate(als, axis=0)

        def run(masked, parts):
            def scores_of(hs):
                return [_dot_nt(q_ref[hs, r0:r0 + nr, :], k_ref[hs, 0:nk, :])
                        for r0, nr, nk in parts]

            ahead = min(LOOKAHEAD, nh)
            scores = [scores_of(hs) for hs in range(ahead)]
            for hs in range(nh):
                if hs + ahead < nh:
                    scores.append(scores_of(hs + ahead))
                for (r0, nr, nk), sc in zip(parts, scores[hs]):
                    p, alpha = softmax_strips(masked, hs, sc, r0)
                    acc_scr[hs, r0:r0 + nr, :] = (alpha * acc_scr[hs, r0:r0 + nr, :]
                                                  + _dot(p, v_ref[hs, 0:nk, :]))

        @pl.when(j < i)
        def _():
            run(False, [(0, t, t)])

        @pl.when(j == i)
        def _():
            run(True, [(0, t, t)])
            for h in range(nh):
                acc = acc_scr[h]
                l = acc[:, HEAD_DIM:HEAD_DIM + 1]
                o_ref[h] = jnp.where(_lane(acc.shape) < HEAD_DIM, acc / l, 0.0)
                lse_ref[h] = m_scr[h] + jnp.log2(l)

    qspec = pl.BlockSpec((nh, t, HEAD_PAD), lambda h, p, it, jt: (h, it[p], 0))
    kspec = pl.BlockSpec((nh, t, HEAD_PAD), lambda h, p, it, jt: (h, jt[p], 0))
    out = jax.ShapeDtypeStruct((N_TOK_HEADS, s, HEAD_PAD), F32)
    return pl.pallas_call(
        body,
        grid_spec=pltpu.PrefetchScalarGridSpec(
            num_scalar_prefetch=2, grid=(N_TOK_HEADS // nh, itab.shape[0]),
            in_specs=[qspec, kspec, kspec], out_specs=[qspec, qspec],
            scratch_shapes=[pltpu.VMEM((nh, t, HEAD_PAD), F32)] * 2),
        out_shape=[out, out],
        name="flash_fwd", compiler_params=_cp(2))(itab, jtab, q, k, v)


def _flash_bwd(q, k, v, stats, do, t, nh, ex):
    s = q.shape[1]
    nb = s // t
    itab, jtab = _causal_pairs(nb, True)
    npairs = itab.shape[0]
    ngroups = N_TOK_HEADS // nh
    scale = 1.0 / math.sqrt(QK_DIM)
    c2 = LOG2E * scale
    nx = ex.n if ex is not None else 0
    ex_arrays, ex_out_shape, ex_scratch = (
        (ex.arrays, ex.out_shape, ex.scratch) if ex is not None else ([], [], []))

    def body(it_ref, jt_ref, q_ref, k_ref, v_ref, st_ref, do_ref, *rest):
        ex_in, rest = rest[:nx], rest[nx:]
        dq_ref, dk_ref, dv_ref = rest[:3]
        ex_out, rest = rest[3:3 + nx], rest[3 + nx:]
        dk_scr, dv_scr = rest[:2]
        ex_sems = rest[2:]
        pair = pl.program_id(1)
        i, j = it_ref[pair], jt_ref[pair]
        rows_i = pl.ds(pl.multiple_of(i * t, t), t)

        if nx:
            @pl.when(jnp.logical_and(pl.program_id(0) == 0, pair == 0))
            def _():
                for cp in ex.copies(ex_in, ex_out, ex_sems):
                    cp.start()

        @pl.when(i == j)
        def _():
            dk_scr[...] = jnp.zeros_like(dk_scr)
            dv_scr[...] = jnp.zeros_like(dv_scr)

        @pl.when(j == 0)
        def _():
            dq_ref[:, rows_i, :] = jnp.zeros((nh, t, HEAD_PAD), F32)

        def prob_strips(masked, h, sct, dpt, k0, q0):
            ps, dss = [], []
            for r0 in range(0, sct.shape[0], STRIP):
                rows = slice(r0, r0 + STRIP)
                if masked:
                    kk = k0 + r0 + lax.broadcasted_iota(jnp.int32, (STRIP, LANES), 0)
                    qq = q0 + lax.broadcasted_iota(jnp.int32, (STRIP, LANES), 1)
                pcs, dcs = [], []
                for n in range(sct.shape[1] // LANES):
                    cols = slice(n * LANES, (n + 1) * LANES)
                    qcols = slice(q0 + n * LANES, q0 + (n + 1) * LANES)
                    x = sct[rows, cols] * c2
                    if masked:
                        x = jnp.where(kk <= qq + n * LANES, x, NEG_BIG)
                    p = jnp.exp2(x - st_ref[h, 0:1, qcols])
                    pcs.append(p.astype(BF16))
                    dcs.append((p * (dpt[rows, cols] - st_ref[h, 1:2, qcols]) * scale).astype(BF16))
                ps.append(jnp.concatenate(pcs, axis=1))
                dss.append(jnp.concatenate(dcs, axis=1))
            return jnp.concatenate(ps, axis=0), jnp.concatenate(dss, axis=0)

        def run(masked, parts):
            def scores_of(h):
                return [(_dot_nt(k_ref[h, k0:k0 + nk, :], q_ref[h, q0:q0 + nq, :]),
                         _dot_nt(v_ref[h, k0:k0 + nk, :], do_ref[h, q0:q0 + nq, :]))
                        for k0, nk, q0, nq in parts]

            ahead = min(LOOKAHEAD, nh)
            scores = [scores_of(h) for h in range(ahead)]
            for h in range(nh):
                if h + ahead < nh:
                    scores.append(scores_of(h + ahead))
                for (k0, nk, q0, nq), (sct, dpt) in zip(parts, scores[h]):
                    pt, dst = prob_strips(masked, h, sct, dpt, k0, q0)
                    dv_scr[h, k0:k0 + nk, :] += _dot(pt, do_ref[h, q0:q0 + nq, :])
                    dk_scr[h, k0:k0 + nk, :] += _dot(dst, q_ref[h, q0:q0 + nq, :])
                    rows = pl.ds(pl.multiple_of(i * t + q0, t // 2), nq)
                    dq_ref[h, rows, :] += _dot_tn(dst, k_ref[h, k0:k0 + nk, :])

        @pl.when(i > j)
        def _():
            run(False, [(0, t, 0, t)])

        @pl.when(i == j)
        def _():
            run(True, [(0, t // 2, 0, t), (t // 2, t // 2, t // 2, t // 2)])

        @pl.when(i == nb - 1)
        def _():
            dk_ref[...] = dk_scr[...]
            dv_ref[...] = dv_scr[...].astype(BF16)

        if nx:
            @pl.when(jnp.logical_and(pl.program_id(0) == ngroups - 1, pair == npairs - 1))
            def _():
                for cp in ex.copies(ex_in, ex_out, ex_sems):
                    cp.wait()

    qspec = pl.BlockSpec((nh, t, HEAD_PAD), lambda h, p, it, jt: (h, it[p], 0))
    kspec = pl.BlockSpec((nh, t, HEAD_PAD), lambda h, p, it, jt: (h, jt[p], 0))
    dqspec = pl.BlockSpec((nh, s, HEAD_PAD), lambda h, p, it, jt: (h, 0, 0))
    stspec = pl.BlockSpec((nh, 2, t), lambda h, p, it, jt: (h, 0, it[p]))
    out = jax.ShapeDtypeStruct((N_TOK_HEADS, s, HEAD_PAD), F32)
    res = pl.pallas_call(
        body,
        grid_spec=pltpu.PrefetchScalarGridSpec(
            num_scalar_prefetch=2, grid=(ngroups, npairs),
            in_specs=[qspec, kspec, kspec, stspec, qspec] + [ANY] * nx,
            out_specs=[dqspec, kspec, kspec] + [ANY] * nx,
            scratch_shapes=[pltpu.VMEM((nh, t, HEAD_PAD), F32)] * 2 + ex_scratch),
        out_shape=[out, out, jax.ShapeDtypeStruct(out.shape, BF16)] + ex_out_shape,
        name="flash_bwd", compiler_params=_cp(2))(itab, jtab, q, k, v, stats, do, *ex_arrays)
    return res[:3], res[3:]


def _mem_probs(qp, kp, hh):
    lane = _lane(qp.shape)
    keep = (lane < HEAD_DIM) if hh == 0 else (lane >= HEAD_DIM)
    qh = jnp.where(keep, qp, 0.0).astype(BF16)
    sc = _dot_nt(qh, kp) * (1.0 / math.sqrt(HEAD_DIM))
    e = jnp.exp(sc - jnp.max(sc, axis=-1, keepdims=True))
    return e / jnp.sum(e, axis=-1, keepdims=True), keep


def _mix_out_fwd(tok, z, memkv, w, h, g, b, tgt, g0, q0, padded, name, ts):
    s = z.shape[0]
    zw = z.shape[1]
    tok_spec = _heads(ts) if padded else _rows(ts, TOK_WIDTH)
    with_loss = tgt is not None

    def body(*refs):
        tok_ref, z_ref, mkv_ref, w_ref, h_ref, g_ref, b_ref = refs[:7]
        if with_loss:
            t_ref, cat_ref, y_ref, dpre_ref, dgb_ref, loss_ref = refs[7:]
        else:
            cat_ref, y_ref, pre_ref, out_ref = refs[7:]
        if padded:
            for p in range(N_TOK_HEADS // 2):
                cat_ref[:, p * LANES:(p + 1) * LANES] = (
                    tok_ref[2 * p] + pltpu.roll(tok_ref[2 * p + 1], HEAD_DIM, 1))
        else:
            cat_ref[:, 0:TOK_WIDTH] = tok_ref[...]
        for pr in range(N_MEM_HEADS // 2):
            sl = slice(pr * LANES, (pr + 1) * LANES)
            qp = z_ref[:, q0 + pr * LANES:q0 + (pr + 1) * LANES]
            kp = mkv_ref[:, sl].astype(BF16)
            vp = mkv_ref[:, MEM_WIDTH + pr * LANES:MEM_WIDTH + (pr + 1) * LANES].astype(BF16)
            outs = []
            for hh in range(2):
                p, _ = _mem_probs(qp, kp, hh)
                outs.append(_dot(p.astype(BF16), vp))
            lane = _lane(outs[0].shape)
            cat_ref[:, TOK_WIDTH + pr * LANES:TOK_WIDTH + (pr + 1) * LANES] = jnp.where(
                lane < HEAD_DIM, outs[0], outs[1])
        gate = z_ref[:, g0:g0 + MIX_WIDTH]
        yb = (cat_ref[...] * (gate * _sigmoid(gate))).astype(BF16)
        y_ref[...] = yb
        pre = ALPHA * h_ref[...] + _dot(yb, w_ref[...])
        xhat, rstd = _ln_stats(pre)
        hout = xhat * g_ref[...] + b_ref[...]
        if with_loss:
            @pl.when(pl.program_id(0) == 0)
            def _():
                loss_ref[...] = jnp.zeros_like(loss_ref)
                dgb_ref[...] = jnp.zeros_like(dgb_ref)
            err = hout - t_ref[...]
            loss_ref[...] += 0.5 * jnp.sum(jnp.mean(err * err, axis=-1, keepdims=True))
            dh = err * (1.0 / D_MODEL)
            dpre_ref[...] = _ln_bwd(dh, xhat, rstd, g_ref[...])
            dgb_ref[0:1, :] += jnp.sum(dh * xhat, axis=0, keepdims=True)
            dgb_ref[1:2, :] += jnp.sum(dh, axis=0, keepdims=True)
        else:
            pre_ref[...] = pre
            out_ref[...] = hout

    act = jax.ShapeDtypeStruct((s, D_MODEL), F32)
    in_specs = [tok_spec, _rows(ts, zw), _full((MEM_LEN, 2 * MEM_WIDTH)),
                _full((MIX_WIDTH, D_MODEL)), _rows(ts, D_MODEL),
                _full((1, D_MODEL)), _full((1, D_MODEL))]
    out_specs = [_rows(ts, MIX_WIDTH)] * 2
    out_shape = [jax.ShapeDtypeStruct((s, MIX_WIDTH), F32),
                 jax.ShapeDtypeStruct((s, MIX_WIDTH), BF16)]
    args = [tok, z, memkv, w, h, g, b]
    if with_loss:
        in_specs.append(_rows(ts, D_MODEL))
        out_specs += [_rows(ts, D_MODEL), _full((SUBLANES, D_MODEL)), _full((SUBLANES, LANES))]
        out_shape += [act, jax.ShapeDtypeStruct((SUBLANES, D_MODEL), F32),
                      jax.ShapeDtypeStruct((SUBLANES, LANES), F32)]
        args.append(tgt)
    else:
        out_specs += [_rows(ts, D_MODEL)] * 2
        out_shape += [act, act]
    return pl.pallas_call(
        body, grid=(s // ts,), in_specs=in_specs, out_specs=out_specs, out_shape=out_shape,
        name=name, compiler_params=_cp(1))(*args)


def _gate_mem_bwd(dpre, y, w_t, cat, z, memkv, lse, g0, q0, name, ts):
    s = z.shape[0]
    zw = z.shape[1]
    padded = lse is not None
    gq_w = MIX_WIDTH + MEM_WIDTH

    def body(*refs):
        if padded:
            (dpre_ref, y_ref, wt_ref, cat_ref, z_ref, mkv_ref, lse_ref,
             dzg_ref, dtok_ref, dmkv_ref, dw_ref, st_ref) = refs
        else:
            (dpre_ref, y_ref, wt_ref, cat_ref, z_ref, mkv_ref,
             dzg_ref, dtok_ref, dmkv_ref, dw_ref) = refs

        @pl.when(pl.program_id(0) == 0)
        def _():
            dmkv_ref[...] = jnp.zeros_like(dmkv_ref)
            dw_ref[...] = jnp.zeros_like(dw_ref)

        dpb = dpre_ref[...].astype(BF16)
        dy_ = _dot(dpb, wt_ref[...])
        dw_ref[...] += _dot_tn(y_ref[...], dpb)
        gate = z_ref[:, g0:g0 + MIX_WIDTH]
        sg = _sigmoid(gate)
        dzg_ref[:, 0:MIX_WIDTH] = (dy_ * cat_ref[...]
                                   * (sg * (1.0 + gate * (1.0 - sg)))).astype(BF16)
        dcat = dy_ * (gate * sg)
        if padded:
            low = _lane((ts, LANES)) < HEAD_DIM
            for p in range(N_TOK_HEADS // 2):
                d = dcat[:, p * LANES:(p + 1) * LANES]
                prod = d * cat_ref[:, p * LANES:(p + 1) * LANES]
                first = jnp.sum(jnp.where(low, prod, 0.0), axis=-1, keepdims=True)
                second = jnp.sum(jnp.where(low, 0.0, prod), axis=-1, keepdims=True)
                dtok_ref[2 * p] = jnp.where(low, d, 0.0).astype(BF16)
                dtok_ref[2 * p + 1] = jnp.where(low, pltpu.roll(d, HEAD_DIM, 1), 0.0).astype(BF16)
                for hh, delta in ((2 * p, first), (2 * p + 1, second)):
                    both = jnp.where(low, lse_ref[hh], delta).T
                    st_ref[hh, 0:1, :] = both[0:1, :]
                    st_ref[hh, 1:2, :] = both[HEAD_DIM:HEAD_DIM + 1, :]
        else:
            dtok_ref[...] = dcat[:, 0:TOK_WIDTH]
        for pr in range(N_MEM_HEADS // 2):
            sl = slice(pr * LANES, (pr + 1) * LANES)
            vsl = slice(MEM_WIDTH + pr * LANES, MEM_WIDTH + (pr + 1) * LANES)
            qp = z_ref[:, q0 + pr * LANES:q0 + (pr + 1) * LANES]
            qpb = qp.astype(BF16)
            kp = mkv_ref[:, sl].astype(BF16)
            vp = mkv_ref[:, vsl].astype(BF16)
            dmo = dcat[:, TOK_WIDTH + pr * LANES:TOK_WIDTH + (pr + 1) * LANES]
            dqp = None
            for hh in range(2):
                p, keep = _mem_probs(qp, kp, hh)
                do_h = jnp.where(keep, dmo, 0.0).astype(BF16)
                dmkv_ref[:, vsl] += _dot_tn(p.astype(BF16), do_h)
                dp = _dot_nt(do_h, vp)
                ds = (p * (dp - jnp.sum(dp * p, axis=-1, keepdims=True))
                      * (1.0 / math.sqrt(HEAD_DIM))).astype(BF16)
                dqh = jnp.where(keep, _dot(ds, kp), 0.0)
                dqp = dqh if dqp is None else dqp + dqh
                dkh = _dot_tn(ds, qpb)
                klane = _lane(dkh.shape)
                kkeep = (klane < HEAD_DIM) if hh == 0 else (klane >= HEAD_DIM)
                dmkv_ref[:, sl] += jnp.where(kkeep, dkh, 0.0)
            dzg_ref[:, MIX_WIDTH + pr * LANES:MIX_WIDTH + (pr + 1) * LANES] = dqp.astype(BF16)

    in_specs = [_rows(ts, D_MODEL), _rows(ts, MIX_WIDTH), _full((D_MODEL, MIX_WIDTH)),
                _rows(ts, MIX_WIDTH), _rows(ts, zw), _full((MEM_LEN, 2 * MEM_WIDTH))]
    out_specs = [_rows(ts, gq_w), _heads(ts) if padded else _rows(ts, TOK_WIDTH),
                 _full((MEM_LEN, 2 * MEM_WIDTH)), _full((MIX_WIDTH, D_MODEL))]
    heads_shape = (N_TOK_HEADS, s, HEAD_PAD)
    out_shape = [jax.ShapeDtypeStruct((s, gq_w), BF16),
                 jax.ShapeDtypeStruct(heads_shape, BF16) if padded
                 else jax.ShapeDtypeStruct((s, TOK_WIDTH), F32),
                 jax.ShapeDtypeStruct((MEM_LEN, 2 * MEM_WIDTH), F32),
                 jax.ShapeDtypeStruct((MIX_WIDTH, D_MODEL), F32)]
    args = [dpre, y, w_t, cat, z, memkv]
    if padded:
        in_specs.append(_heads(ts))
        out_specs.append(pl.BlockSpec((N_TOK_HEADS, 2, ts), lambda i: (0, 0, i)))
        out_shape.append(jax.ShapeDtypeStruct((N_TOK_HEADS, 2, s), F32))
        args.append(lse)
    return pl.pallas_call(
        body, grid=(s // ts,), in_specs=in_specs, out_specs=out_specs, out_shape=out_shape,
        name=name, compiler_params=_cp(1))(*args)


def _ln_stats(pre):
    mu = jnp.mean(pre, axis=-1, keepdims=True)
    d = pre - mu
    rstd = lax.rsqrt(jnp.mean(d * d, axis=-1, keepdims=True) + NORM_EPS)
    return d * rstd, rstd


def _ln_bwd(dh, xhat, rstd, g):
    dxh = dh * g
    return rstd * (dxh - jnp.mean(dxh, axis=-1, keepdims=True)
                   - xhat * jnp.mean(dxh * xhat, axis=-1, keepdims=True))


def _linear_bwd(x, dys, offs, w_t, resid, ln, name, ts):
    s, kdim = x.shape
    n = w_t.shape[0]
    widths = [d.shape[1] for d in dys]
    npieces = len(dys)
    with_ln = ln is not None

    def body(*refs):
        x_ref = refs[0]
        dy_refs = refs[1:1 + npieces]
        if with_ln:
            wt_ref, r_ref, pre_ref, g_ref, dx_ref, dw_ref, dgb_ref = refs[1 + npieces:]
        else:
            wt_ref, r_ref, dx_ref, dw_ref = refs[1 + npieces:]

        @pl.when(pl.program_id(0) == 0)
        def _():
            dw_ref[...] = jnp.zeros_like(dw_ref)
            if with_ln:
                dgb_ref[...] = jnp.zeros_like(dgb_ref)

        xb = x_ref[...].astype(BF16)
        dx = ALPHA * r_ref[...]
        for dy_ref, off, wd in zip(dy_refs, offs, widths):
            dyb = dy_ref[...].astype(BF16)
            dx = dx + _dot(dyb, wt_ref[off:off + wd, :])
            dw_ref[:, off:off + wd] += _dot_tn(xb, dyb)
        if with_ln:
            xhat, rstd = _ln_stats(pre_ref[...])
            dx_ref[...] = _ln_bwd(dx, xhat, rstd, g_ref[...])
            dgb_ref[0:1, :] += jnp.sum(dx * xhat, axis=0, keepdims=True)
            dgb_ref[1:2, :] += jnp.sum(dx, axis=0, keepdims=True)
        else:
            dx_ref[...] = dx

    in_specs = ([_rows(ts, kdim)] + [_rows(ts, wd) for wd in widths]
                + [_full((n, kdim)), _rows(ts, kdim)])
    out_specs = [_rows(ts, kdim), _full((kdim, n))]
    out_shape = [jax.ShapeDtypeStruct((s, kdim), F32), jax.ShapeDtypeStruct((kdim, n), F32)]
    args = [x, *dys, w_t, resid]
    if with_ln:
        in_specs += [_rows(ts, kdim), _full((1, kdim))]
        out_specs.append(_full((SUBLANES, kdim)))
        out_shape.append(jax.ShapeDtypeStruct((SUBLANES, kdim), F32))
        args += list(ln)
    return pl.pallas_call(
        body, grid=(s // ts,), in_specs=in_specs, out_specs=out_specs, out_shape=out_shape,
        name=name, compiler_params=_cp(1))(*args)


def _wgrad_small(x, dy, name):
    def body(x_ref, dy_ref, dw_ref):
        dw_ref[...] = _dot_tn(x_ref[...].astype(BF16), dy_ref[...].astype(BF16))

    return pl.pallas_call(
        body, out_shape=jax.ShapeDtypeStruct((x.shape[1], dy.shape[1]), F32),
        name=name, compiler_params=pltpu.CompilerParams(vmem_limit_bytes=VMEM_LIMIT))(x, dy)


def _shift_down(u, carry8, k):
    if k == 0:
        return u
    rolled = pltpu.roll(u, k, 0)
    row = lax.broadcasted_iota(jnp.int32, carry8.shape, 0)
    top = jnp.where(row < k, pltpu.roll(carry8, k, 0), rolled[0:SUBLANES])
    return jnp.concatenate([top, rolled[SUBLANES:]], axis=0)


def _shift_up(u, carry8, k):
    if k == 0:
        return u
    n = u.shape[0]
    rolled = pltpu.roll(u, n - k, 0)
    row = lax.broadcasted_iota(jnp.int32, carry8.shape, 0)
    bot = jnp.where(row >= SUBLANES - k, pltpu.roll(carry8, SUBLANES - k, 0),
                    rolled[n - SUBLANES:])
    return jnp.concatenate([rolled[:n - SUBLANES], bot], axis=0)


def _neg_expm1(t):
    e = jnp.exp(t)
    em1 = e - 1.0
    safe = jnp.where(e == 1.0, 1.0, jnp.log(e))
    return -jnp.where(e == 1.0, t, jnp.where(em1 == -1.0, -1.0, em1 * t / safe))


def _lru_gates(u, carry8, cw_ref, vec_ref, wr_ref, wi_ref):
    taps = [_shift_down(u, carry8, k) for k in range(CONV_W)]
    xc = vec_ref[0:1, :] + cw_ref[3:4, :] * u
    for k in range(1, CONV_W):
        xc = xc + cw_ref[3 - k:4 - k, :] * taps[k]
    xb = xc.astype(BF16)
    r = _sigmoid(_dot(xb, wr_ref[...]) + vec_ref[1:2, :])
    ig = _sigmoid(_dot(xb, wi_ref[...]) + vec_ref[2:3, :])
    nlam = -vec_ref[3:4, :]
    softplus = jnp.maximum(nlam, 0.0) + jnp.log(1.0 + jnp.exp(-jnp.abs(nlam)))
    cneg = -LRU_C * softplus
    log_a = cneg * r
    a = jnp.exp(log_a)
    sq = jnp.sqrt(_neg_expm1(2.0 * log_a))
    return xc, r, ig, cneg, a, sq, taps


def _chained_scan(a_ref, b_ref, out_ref, cum_scr, x_in):
    rows_total, w = a_ref.shape
    nseg = SCAN_SEGMENTS
    seg = rows_total // nseg

    def step(t, carry):
        xs, ps = carry
        new_x, new_p = [], []
        for sg in range(nseg):
            row = pl.ds(sg * seg + t, 1)
            a = a_ref[row, :]
            x = a * xs[sg] + b_ref[row, :]
            out_ref[row, :] = x
            new_x.append(x)
            if sg > 0:
                p = a * ps[sg - 1]
                cum_scr[row, :] = p
                new_p.append(p)
        return tuple(new_x), tuple(new_p)

    zero, one = jnp.zeros((1, w), F32), jnp.ones((1, w), F32)
    xs, _ = lax.fori_loop(0, seg, step, ((x_in,) + (zero,) * (nseg - 1), (one,) * (nseg - 1)))
    x_prev = xs[0]
    for sg in range(1, nseg):
        rows = slice(sg * seg, (sg + 1) * seg)
        out_ref[rows, :] = out_ref[rows, :] + cum_scr[rows, :] * x_prev
        x_prev = out_ref[(sg + 1) * seg - 1:(sg + 1) * seg, :]
    return x_prev


def _lru_fwd(x, win, cw8, vec8, wr, wi, ts):
    s = x.shape[0]

    def body(x_ref, win_ref, cw_ref, vec_ref, wr_ref, wi_ref, u_ref, zg_ref, hs_ref,
             cu_scr, ch_scr, a_scr, gx_scr, cum_scr):
        @pl.when(pl.program_id(0) == 0)
        def _():
            cu_scr[...] = jnp.zeros_like(cu_scr)
            ch_scr[...] = jnp.zeros_like(ch_scr)

        zfull = _dot(x_ref[...].astype(BF16), win_ref[...])
        u = zfull[:, 0:ZA_W]
        u_ref[...] = u
        zg_ref[...] = zfull[:, ZA_W:ZP]
        xc, _, ig, _, a, sq, _ = _lru_gates(u, cu_scr[...], cw_ref, vec_ref, wr_ref, wi_ref)
        a_scr[...] = a
        gx_scr[...] = sq * (ig * xc)
        ch_scr[0:1, :] = _chained_scan(a_scr, gx_scr, hs_ref, cum_scr, ch_scr[0:1, :])
        cu_scr[...] = u[ts - SUBLANES:, :]

    w = TOK_WIDTH
    return pl.pallas_call(
        body, grid=(s // ts,),
        in_specs=[_rows(ts, D_MODEL), _full((D_MODEL, ZP)),
                  _full((SUBLANES, w)), _full((SUBLANES, w)), _full((w, w)), _full((w, w))],
        out_specs=[_rows(ts, w), _rows(ts, ZG_W), _rows(ts, w)],
        out_shape=[jax.ShapeDtypeStruct((s, w), F32), jax.ShapeDtypeStruct((s, ZG_W), F32),
                   jax.ShapeDtypeStruct((s, w), F32)],
        scratch_shapes=[pltpu.VMEM((SUBLANES, w), F32), pltpu.VMEM((SUBLANES, w), F32)]
                       + [pltpu.VMEM((ts, w), F32)] * 3,
        name="lru_fwd", compiler_params=_cp(1))(x, win, cw8, vec8, wr, wi)


def _lru_bwd(z1, dhs, hs, cw8, vec8, wr, wi, wr_t, wi_t, ts):
    s = z1.shape[0]
    nb = s // ts
    w = TOK_WIDTH
    tiles = ts // SUBLANES

    def body(u_ref, up_ref, dhs_ref, hs_ref, hsp_ref, cw_ref, vec_ref, wr_ref, wi_ref,
             wrt_ref, wit_ref, du_ref, dwr_ref, dwi_ref, dvec_ref,
             cc_scr, cd_scr, a_scr, dh_scr):
        i = pl.program_id(0)

        @pl.when(i == 0)
        def _():
            cc_scr[...] = jnp.zeros_like(cc_scr)
            cd_scr[...] = jnp.zeros_like(cd_scr)
            dwr_ref[...] = jnp.zeros_like(dwr_ref)
            dwi_ref[...] = jnp.zeros_like(dwi_ref)
            dvec_ref[...] = jnp.zeros_like(dvec_ref)

        u = u_ref[...]
        first = i == nb - 1
        carry8 = jnp.where(first, 0.0, up_ref[...])
        xc, r, ig, cneg, a, sq, taps = _lru_gates(u, carry8, cw_ref, vec_ref, wr_ref, wi_ref)
        a_scr[...] = a

        def step(n, c):
            t = ts - 1 - n
            dh = dhs_ref[pl.ds(t, 1), :] + c
            dh_scr[pl.ds(t, 1), :] = dh
            return a_scr[pl.ds(t, 1), :] * dh

        cc_scr[0:1, :] = lax.fori_loop(0, ts, step, cc_scr[0:1, :])
        dh = dh_scr[...]
        hprev = _shift_down(hs_ref[...], jnp.where(first, 0.0, hsp_ref[...]), 1)
        ix = ig * xc
        dix = dh * sq
        dlog_a = dh * hprev * a - (dh * ix) * (a * a) / sq
        dpr = (dlog_a * cneg) * r * (1.0 - r)
        dpi = (dix * xc) * ig * (1.0 - ig)
        dprb, dpib = dpr.astype(BF16), dpi.astype(BF16)
        xb = xc.astype(BF16)
        dwr_ref[...] += _dot_tn(xb, dprb)
        dwi_ref[...] += _dot_tn(xb, dpib)
        dxc = dix * ig + _dot(dprb, wrt_ref[...]) + _dot(dpib, wit_ref[...])
        for k in range(CONV_W):
            dvec_ref[3 - k:4 - k, :] += jnp.sum(dxc * taps[k], axis=0, keepdims=True)
        dvec_ref[4:5, :] += jnp.sum(dxc, axis=0, keepdims=True)
        dvec_ref[5:6, :] += jnp.sum(dpr, axis=0, keepdims=True)
        dvec_ref[6:7, :] += jnp.sum(dpi, axis=0, keepdims=True)
        dvec_ref[7:8, :] += (jnp.sum(dlog_a * r, axis=0, keepdims=True)
                             * (LRU_C * _sigmoid(-vec_ref[3:4, :])))
        nxt = cd_scr[...]
        du = cw_ref[3:4, :] * dxc
        for k in range(1, CONV_W):
            du = du + cw_ref[3 - k:4 - k, :] * _shift_up(dxc, nxt, k)
        du_ref[...] = du.astype(BF16)
        cd_scr[...] = dxc[0:SUBLANES, :]

    rev = lambda i: (nb - 1 - i, 0)
    prev8 = lambda i: (jnp.maximum((nb - 1 - i) * tiles - 1, 0), 0)
    blk = pl.BlockSpec((ts, w), rev)
    before = pl.BlockSpec((SUBLANES, w), prev8)
    scr = pltpu.VMEM((ts, w), F32)
    return pl.pallas_call(
        body, grid=(nb,),
        in_specs=[blk, before, blk, blk, before,
                  _full((SUBLANES, w)), _full((SUBLANES, w)),
                  _full((w, w)), _full((w, w)), _full((w, w)), _full((w, w))],
        out_specs=[blk, _full((w, w)), _full((w, w)), _full((SUBLANES, w))],
        out_shape=[jax.ShapeDtypeStruct((s, w), BF16), jax.ShapeDtypeStruct((w, w), F32),
                   jax.ShapeDtypeStruct((w, w), F32), jax.ShapeDtypeStruct((SUBLANES, w), F32)],
        scratch_shapes=[pltpu.VMEM((SUBLANES, w), F32), pltpu.VMEM((SUBLANES, w), F32),
                        scr, scr],
        name="lru_bwd", compiler_params=_cp(1))(
            z1, z1, dhs, hs, hs, cw8, vec8, wr, wi, wr_t, wi_t)


def _adamw(parts, w, m, v, name):
    n = len(parts)
    rows_per = parts[0].shape[1]

    def body(*refs):
        p_refs = refs[:n]
        w_ref, m_ref, v_ref, g_ref, d_ref, nm_ref, nv_ref = refs[n:]
        for l, p_ref in enumerate(p_refs):
            rows = slice(l * rows_per, (l + 1) * rows_per)
            g = p_ref[0].astype(F32)
            for dev in range(1, N_DEV):
                g = g + p_ref[dev].astype(F32)
            g_ref[rows, :] = g
            nm = ADAM_B1 * m_ref[rows, :] + (1.0 - ADAM_B1) * g
            nv = ADAM_B2 * v_ref[rows, :] + (1.0 - ADAM_B2) * (g * g)
            m_hat = nm / (1.0 - ADAM_B1 ** ADAM_STEP)
            v_hat = nv / (1.0 - ADAM_B2 ** ADAM_STEP)
            d_ref[rows, :] = -ADAM_LR * (m_hat / (jnp.sqrt(v_hat) + ADAM_EPS)
                                         + ADAM_WD * w_ref[rows, :])
            nm_ref[rows, :] = nm
            nv_ref[rows, :] = nv

    out = jax.ShapeDtypeStruct(w.shape, F32)
    return pl.pallas_call(
        body, out_shape=[out] * 4, name=name,
        compiler_params=pltpu.CompilerParams(vmem_limit_bytes=VMEM_LIMIT))(*parts, w, m, v)


def _adam_update(g, w, m, v):
    nm = ADAM_B1 * m + (1.0 - ADAM_B1) * g
    nv = ADAM_B2 * v + (1.0 - ADAM_B2) * (g * g)
    m_hat = nm / (1.0 - ADAM_B1 ** ADAM_STEP)
    v_hat = nv / (1.0 - ADAM_B2 ** ADAM_STEP)
    return -ADAM_LR * (m_hat / (jnp.sqrt(v_hat) + ADAM_EPS) + ADAM_WD * w), nm, nv


def _adamw_small(parts, layout, ws, ms, vs, name):
    n = len(layout)

    def body(*refs):
        p_ref = refs[0]
        w_refs, m_refs, v_refs = refs[1:1 + n], refs[1 + n:1 + 2 * n], refs[1 + 2 * n:1 + 3 * n]
        tile_ref = refs[1 + 3 * n]
        outs = refs[2 + 3 * n:]
        tile = p_ref[0]
        for dev in range(1, N_DEV):
            tile = tile + p_ref[dev]
        tile_ref[...] = tile
        for p, (rows, c0, nc) in enumerate(layout):
            for r, src in enumerate(rows):
                g = tile_ref[src:src + 1, c0:c0 + nc]
                d, nm, nv = _adam_update(g, w_refs[p][r:r + 1, :], m_refs[p][r:r + 1, :],
                                         v_refs[p][r:r + 1, :])
                for kind, val in enumerate((g, d, nm, nv)):
                    outs[4 * p + kind][r:r + 1, :] = val

    out_shape = [jax.ShapeDtypeStruct(parts.shape[1:], F32)]
    for w in ws:
        out_shape += [jax.ShapeDtypeStruct(w.shape, F32)] * 4
    res = pl.pallas_call(
        body, out_shape=out_shape, name=name,
        compiler_params=pltpu.CompilerParams(vmem_limit_bytes=VMEM_LIMIT))(parts, *ws, *ms, *vs)
    return res[0], [res[1 + 4 * p:5 + 4 * p] for p in range(n)]


ANY = pl.BlockSpec(memory_space=pl.ANY)
MESH = pl.DeviceIdType.MESH


def _slot(p):
    return 4 * p[0] + 2 * p[1] + p[2]


def _allgather(xs):
    n = len(xs)

    def body(*refs):
        x_refs, o_refs = refs[:n], refs[n:2 * n]
        send_sems, recv_sems, local_sems = refs[2 * n:]
        x, y, c = lax.axis_index("x"), lax.axis_index("y"), lax.axis_index("c")
        me, sibling = (x, y, c), (x, y, 1 - c)
        chips = [(1 - x, y), (x, 1 - y), (1 - x, 1 - y)]

        def copy(a, k, block, to, from_input=False):
            dst = o_refs[a].at[_slot(block)]
            return pltpu.make_async_remote_copy(
                src_ref=x_refs[a] if from_input else dst, dst_ref=dst,
                send_sem=send_sems.at[a, k], recv_sem=recv_sems.at[a, k],
                device_id=to, device_id_type=MESH)

        mine = [pltpu.make_async_copy(x_refs[a], o_refs[a].at[_slot(me)], local_sems.at[a])
                for a in range(n)]
        for cp in mine:
            cp.start()
        first = []
        for a in range(n):
            first.append(copy(a, 0, me, sibling, True))
            first += [copy(a, 1 + j, me, (*chip, c), True) for j, chip in enumerate(chips)]
        for cp in first:
            cp.start()
        passed = []
        for j, chip in enumerate(chips):
            for a in range(n):
                copy(a, 1 + j, (*chip, c), me).wait_recv()
                cp = copy(a, 4 + j, (*chip, c), sibling)
                cp.start()
                passed.append(cp)
        for a in range(n):
            copy(a, 0, sibling, me).wait_recv()
            for j, chip in enumerate(chips):
                copy(a, 4 + j, (*chip, 1 - c), me).wait_recv()
        for cp in first + passed:
            cp.wait_send()
        for cp in mine:
            cp.wait()

    return pl.pallas_call(
        body,
        out_shape=[jax.ShapeDtypeStruct((N_DEV,) + t.shape, t.dtype) for t in xs],
        in_specs=[ANY] * n, out_specs=[ANY] * n,
        scratch_shapes=[pltpu.SemaphoreType.DMA((n, 7)), pltpu.SemaphoreType.DMA((n, 7)),
                        pltpu.SemaphoreType.DMA((n,))],
        name="allgather_weights")(*xs)


class _Exchange:
    def __init__(self, arrays, kinds):
        self.arrays, self.kinds, self.n = list(arrays), list(kinds), len(arrays)
        self.shapes = [self._part_shape(a, k) for a, k in zip(arrays, kinds)]
        self.out_shape = [jax.ShapeDtypeStruct((N_DEV,) + shp, a.dtype)
                          for shp, a in zip(self.shapes, arrays)]
        self.scratch = [pltpu.SemaphoreType.DMA((self.n, N_DEV - 1)),
                        pltpu.SemaphoreType.DMA((self.n, N_DEV - 1)),
                        pltpu.SemaphoreType.DMA((self.n,))]

    @staticmethod
    def _part_shape(arr, kind):
        if kind == "chunks":
            return arr.shape[1:]
        if kind == "cols":
            return (arr.shape[0], arr.shape[1] // N_DEV)
        if kind == "rows":
            return (arr.shape[0] // N_DEV, arr.shape[1])
        return arr.shape

    def copies(self, in_refs, out_refs, sems):
        send_sems, recv_sems, local_sems = sems
        x, y, c = lax.axis_index("x"), lax.axis_index("y"), lax.axis_index("c")
        me = _slot((x, y, c))

        def part(a, dev):
            ref, kind, shp = in_refs[a], self.kinds[a], self.shapes[a]
            if kind == "chunks":
                return ref.at[dev]
            if kind == "cols":
                return ref.at[:, pl.ds(pl.multiple_of(dev * shp[1], LANES), shp[1])]
            if kind == "rows":
                return ref.at[pl.ds(pl.multiple_of(dev * shp[0], SUBLANES), shp[0]), :]
            return ref

        cps = [pltpu.make_async_copy(part(a, me), out_refs[a].at[me], local_sems.at[a])
               for a in range(self.n)]
        for rel in range(1, N_DEV):
            peer = (x ^ (rel >> 2), y ^ ((rel >> 1) & 1), c ^ (rel & 1))
            for a in range(self.n):
                cps.append(pltpu.make_async_remote_copy(
                    src_ref=part(a, _slot(peer)), dst_ref=out_refs[a].at[me],
                    send_sem=send_sems.at[a, rel - 1], recv_sem=recv_sems.at[a, rel - 1],
                    device_id=peer, device_id_type=MESH))
        return cps


def _exchange_grads(arrays, kinds, name):
    ex = _Exchange(arrays, kinds)
    n = ex.n

    def body(*refs):
        cps = ex.copies(refs[:n], refs[n:2 * n], refs[2 * n:])
        for cp in cps:
            cp.start()
        for cp in cps:
            cp.wait()

    return pl.pallas_call(
        body, out_shape=ex.out_shape, in_specs=[ANY] * n, out_specs=[ANY] * n,
        scratch_shapes=ex.scratch, name=name)(*arrays)


BIG = [("mla_w_in", (D_MODEL, MLA_IN), 1), ("mla_w_uq", (Q_LORA, N_TOK_HEADS * QK_DIM), 1),
       ("mla_w_ukv", (KV_LORA, N_TOK_HEADS * 2 * HEAD_DIM), 1), ("lru_w_in", (D_MODEL, LRU_IN), 1),
       ("w_mem_kv", (2, D_MODEL, 2 * MEM_WIDTH), 1), ("w_out", (2, MIX_WIDTH, D_MODEL), 1)]
SMALL = [("lru_conv_w", (CONV_W, TOK_WIDTH), 1), ("lru_conv_b", (TOK_WIDTH,), 0),
         ("lru_b_rgate", (TOK_WIDTH,), 0), ("lru_b_igate", (TOK_WIDTH,), 0),
         ("lru_lambda", (TOK_WIDTH,), 0)]
REPL = [("mla_q_norm", (Q_LORA,)), ("mla_kv_norm", (KV_LORA,)),
        ("lru_w_rgate", (N_TOK_HEADS, HEAD_DIM, HEAD_DIM)),
        ("lru_w_igate", (N_TOK_HEADS, HEAD_DIM, HEAD_DIM)),
        ("ln_g", (2, D_MODEL)), ("ln_b", (2, D_MODEL))]


def _shard_shape(shape, axis):
    return tuple(d // N_DEV if a == axis else d for a, d in enumerate(shape))


def _size(shape):
    return math.prod(shape)


BIG_ROWS = sum(_size(s) for _, s, _ in BIG) // N_DEV // LANES
SMALL_ROWS = SUBLANES


def _pack_rows(flat_parts, rows):
    flat = jnp.concatenate([p.reshape(-1) for p in flat_parts])
    return jnp.pad(flat, (0, rows * LANES - flat.shape[0])).reshape(rows, LANES)


def _to_chunks(full, axis):
    shape = full.shape
    split = shape[:axis] + (N_DEV, shape[axis] // N_DEV) + shape[axis + 1:]
    return jnp.moveaxis(full.reshape(split), axis, 0).reshape(N_DEV, -1)


def _from_chunks(chunks, shape, axis):
    sh = _shard_shape(shape, axis)
    t = chunks.reshape((N_DEV,) + sh)
    t = jnp.moveaxis(t, 0, axis)
    return t.reshape(shape)


def _split_flat(flat2d, table):
    out, off = [], 0
    for size in table:
        out.append(flat2d[:, off:off + size])
        off += size
    return out


def _win0_to_padded(w):
    z = lambda n: jnp.zeros((w.shape[0], n), w.dtype)
    return jnp.concatenate([w[:, 0:640], z(KR_LANE), w[:, 640:672],
                            z(LANES - KR_LANE - QK_ROPE), w[:, 672:1952]], axis=1)


def _win0_from_padded(wp):
    k0 = ZA_KR + KR_LANE
    return jnp.concatenate([wp[:, 0:640], wp[:, k0:k0 + QK_ROPE], wp[:, ZA_W:ZP]], axis=1)


def _pad_heads(w, per_head, lo, hi):
    t = w.reshape(w.shape[0], N_TOK_HEADS, per_head)[:, :, lo:hi]
    t = jnp.pad(t, ((0, 0), (0, 0), (0, HEAD_PAD - (hi - lo))))
    return t.reshape(w.shape[0], QKV_PAD)


def _unpad_heads(wp, width):
    return wp.reshape(wp.shape[0], N_TOK_HEADS, HEAD_PAD)[:, :, :width]


def _block_diag(w):
    eye = jnp.eye(N_TOK_HEADS, dtype=w.dtype)
    return (w[:, :, None, :] * eye[:, None, :, None]).reshape(TOK_WIDTH, TOK_WIDTH)


def _diag_blocks(d):
    t = d.reshape(N_TOK_HEADS, HEAD_DIM, N_TOK_HEADS, HEAD_DIM)
    return jnp.stack([t[g, :, g, :] for g in range(N_TOK_HEADS)])


def _rope_tables(positions):
    half = QK_ROPE // 2
    inv_freq = ROPE_THETA ** (-jnp.arange(half, dtype=F32) / half)
    ang = positions.astype(F32)[:, None] * inv_freq
    cos, sin = jnp.cos(ang), jnp.sin(ang)
    s = positions.shape[0]
    tail = jnp.zeros((s, HEAD_PAD - QK_DIM), F32)
    c = jnp.concatenate([jnp.ones((s, QK_NOPE), F32), cos, cos, tail], axis=1)
    sn = jnp.concatenate([jnp.zeros((s, QK_NOPE), F32), sin, sin, tail], axis=1)
    return c, sn


def _local_step(x, mem, positions, tgt, wts, ts, tatt, early_exchange):
    bf = lambda t: t.astype(BF16)
    win0 = _win0_to_padded(wts["mla_w_in"])
    wuq = _pad_heads(wts["mla_w_uq"], QK_DIM, 0, QK_DIM)
    wukv = jnp.concatenate([_pad_heads(wts["mla_w_ukv"], 2 * HEAD_DIM, 0, QK_NOPE),
                            _pad_heads(wts["mla_w_ukv"], 2 * HEAD_DIM, QK_NOPE, 2 * HEAD_DIM)],
                           axis=1)
    win1 = wts["lru_w_in"]
    wmkv, wout = wts["w_mem_kv"], wts["w_out"]
    gq = wts["mla_q_norm"].reshape(1, Q_LORA)
    gkv = wts["mla_kv_norm"].reshape(1, KV_LORA)
    ln_g, ln_b = wts["ln_g"], wts["ln_b"]
    wr, wi = bf(_block_diag(wts["lru_w_rgate"])), bf(_block_diag(wts["lru_w_igate"]))
    cw8 = jnp.pad(wts["lru_conv_w"], ((0, SUBLANES - CONV_W), (0, 0)))
    vec8 = jnp.pad(jnp.stack([wts["lru_conv_b"], wts["lru_b_rgate"], wts["lru_b_igate"],
                              wts["lru_lambda"]]), ((0, SUBLANES - 4), (0, 0)))
    tabs = _rope_tables(positions)
    tmem = mem.shape[0]

    za0, zg0, q, k, v = _mla_prep_fwd(x, win0, tabs, gq, gkv, wuq, wukv, ts)
    o, lse = _flash_fwd(q, k, v, tatt, FWD_HEADS)
    mkv0, = _rowmm(mem, wmkv[0], [2 * MEM_WIDTH], "mem_kv0", tmem)
    cat0, y0, pre0, h1 = _mix_out_fwd(o, zg0, mkv0, wout[0], x, ln_g[0:1], ln_b[0:1], None,
                                      0, MIX_WIDTH, True, "mix_out_fwd0", ts)
    del o
    u1, zg1, hs = _lru_fwd(h1, win1, cw8, vec8, wr, wi, ts)
    mkv1, = _rowmm(mem, wmkv[1], [2 * MEM_WIDTH], "mem_kv1", tmem)
    cat1, y1, dpre1, dgb1, loss8 = _mix_out_fwd(hs, zg1, mkv1, wout[1], h1, ln_g[1:2],
                                                ln_b[1:2], tgt, 0, MIX_WIDTH, False,
                                                "mix_out_loss", ts)
    loss = loss8[0, 0]

    dzg1, dhs, dmkv1, dwout1 = _gate_mem_bwd(dpre1, y1, wout[1].T, cat1, zg1, mkv1, None,
                                             0, MIX_WIDTH, "gate_mem_bwd1", ts)
    du, dwr, dwi, dvec = _lru_bwd(u1, dhs, hs, cw8, vec8, wr, wi, wr.T, wi.T, ts)
    dpre0, dwin1, dgb0 = _linear_bwd(h1, [du, dzg1], [0, ZA_W], win1.T, dpre1,
                                     (pre0, ln_g[0:1]), "in_proj_bwd1", ts)
    dwmkv1 = _wgrad_small(mem, dmkv1, "mem_kv_bwd1")
    dzg0, do, dmkv0, dwout0, stats = _gate_mem_bwd(dpre0, y0, wout[0].T, cat0, zg0, mkv0, lse,
                                                   0, MIX_WIDTH, "gate_mem_bwd0", ts)
    dwmkv0 = _wgrad_small(mem, dmkv0, "mem_kv_bwd0")
    early = {
        "lru_w_in": dwin1,
        "lru_small": dvec,
        "lru_w_rgate": _diag_blocks(dwr).reshape(TOK_WIDTH, HEAD_DIM),
        "lru_w_igate": _diag_blocks(dwi).reshape(TOK_WIDTH, HEAD_DIM),
        "w_mem_kv": [dwmkv0, dwmkv1],
        "w_out": [dwout0, dwout1],
    }
    (dq, dk, dv), got_early = _flash_bwd(q, k, v, stats, do, tatt, BWD_HEADS,
                                         early_exchange(early))
    dza, dzk, dwuq_p, dwukv_p, dg = _mla_prep_bwd(za0, dq, dk, dv, tabs, gq, gkv,
                                                  wuq.T, wukv.T, ts)
    gx, dwin0_p = _linear_bwd(x, [dza, dzk, dzg0], [ZA_CQ, ZA_KR, ZA_W], win0.T, dpre0,
                              None, "in_proj_bwd0", ts)

    dwukv = jnp.concatenate([_unpad_heads(dwukv_p[:, :QKV_PAD], HEAD_DIM),
                             _unpad_heads(dwukv_p[:, QKV_PAD:], HEAD_DIM)], axis=2)
    zrow = jnp.zeros((1, D_MODEL), F32)
    gains = jnp.pad(dg[0:1], ((0, 0), (0, D_MODEL - Q_LORA - KV_LORA)))
    small_repl = jnp.concatenate([dgb0[0:2], dgb1[0:2], gains,
                                  loss * jnp.ones((1, D_MODEL), F32), zrow, zrow], axis=0)
    late = {
        "mla_w_in": _win0_from_padded(dwin0_p),
        "mla_w_uq": _unpad_heads(dwuq_p, QK_DIM).reshape(Q_LORA, N_TOK_HEADS * QK_DIM),
        "mla_w_ukv": dwukv.reshape(KV_LORA, N_TOK_HEADS * 2 * HEAD_DIM),
        "small_repl": small_repl,
    }
    return gx, early, got_early, late


WEIGHT_ORDER = ["mla_w_in", "mla_q_norm", "mla_w_uq", "mla_kv_norm", "mla_w_ukv", "lru_w_in",
                "lru_conv_w", "lru_conv_b", "lru_w_rgate", "lru_b_rgate", "lru_w_igate",
                "lru_b_igate", "lru_lambda", "w_mem_kv", "w_out", "ln_g", "ln_b"]


def kernel(x, mem, positions, mla_w_in, mla_q_norm, mla_w_uq, mla_kv_norm, mla_w_ukv, lru_w_in, lru_conv_w, lru_conv_b, lru_w_rgate, lru_b_rgate, lru_w_igate, lru_b_igate, lru_lambda, w_mem_kv, w_out, ln_g, ln_b, loss_target, m_mla_w_in, m_mla_q_norm, m_mla_w_uq, m_mla_kv_norm, m_mla_w_ukv, m_lru_w_in, m_lru_conv_w, m_lru_conv_b, m_lru_w_rgate, m_lru_b_rgate, m_lru_w_igate, m_lru_b_igate, m_lru_lambda, m_w_mem_kv, m_w_out, m_ln_g, m_ln_b, v_mla_w_in, v_mla_q_norm, v_mla_w_uq, v_mla_kv_norm, v_mla_w_ukv, v_lru_w_in, v_lru_conv_w, v_lru_conv_b, v_lru_w_rgate, v_lru_b_rgate, v_lru_w_igate, v_lru_b_igate, v_lru_lambda, v_w_mem_kv, v_w_out, v_ln_g, v_ln_b):
    w_in = dict(mla_w_in=mla_w_in, mla_q_norm=mla_q_norm, mla_w_uq=mla_w_uq,
                mla_kv_norm=mla_kv_norm, mla_w_ukv=mla_w_ukv, lru_w_in=lru_w_in,
                lru_conv_w=lru_conv_w, lru_conv_b=lru_conv_b, lru_w_rgate=lru_w_rgate,
                lru_b_rgate=lru_b_rgate, lru_w_igate=lru_w_igate, lru_b_igate=lru_b_igate,
                lru_lambda=lru_lambda, w_mem_kv=w_mem_kv, w_out=w_out, ln_g=ln_g, ln_b=ln_b)
    m_in = dict(mla_w_in=m_mla_w_in, mla_q_norm=m_mla_q_norm, mla_w_uq=m_mla_w_uq,
                mla_kv_norm=m_mla_kv_norm, mla_w_ukv=m_mla_w_ukv, lru_w_in=m_lru_w_in,
                lru_conv_w=m_lru_conv_w, lru_conv_b=m_lru_conv_b, lru_w_rgate=m_lru_w_rgate,
                lru_b_rgate=m_lru_b_rgate, lru_w_igate=m_lru_w_igate, lru_b_igate=m_lru_b_igate,
                lru_lambda=m_lru_lambda, w_mem_kv=m_w_mem_kv, w_out=m_w_out, ln_g=m_ln_g,
                ln_b=m_ln_b)
    v_in = dict(mla_w_in=v_mla_w_in, mla_q_norm=v_mla_q_norm, mla_w_uq=v_mla_w_uq,
                mla_kv_norm=v_mla_kv_norm, mla_w_ukv=v_mla_w_ukv, lru_w_in=v_lru_w_in,
                lru_conv_w=v_lru_conv_w, lru_conv_b=v_lru_conv_b, lru_w_rgate=v_lru_w_rgate,
                lru_b_rgate=v_lru_b_rgate, lru_w_igate=v_lru_w_igate, lru_b_igate=v_lru_b_igate,
                lru_lambda=v_lru_lambda, w_mem_kv=v_w_mem_kv, w_out=v_w_out, ln_g=v_ln_g,
                ln_b=v_ln_b)
    s = x.shape[1]
    ts = min(ROW_BLOCK, s)
    tatt = min(ATT_BLOCK, s)
    big_sizes = [_size(sh) // N_DEV for _, sh, _ in BIG]
    small_sizes = [_size(sh) // N_DEV for _, sh, _ in SMALL]

    big_local = _pack_rows([w_in[n] for n, _, _ in BIG], BIG_ROWS).astype(BF16)
    small_local = _pack_rows([w_in[n] for n, _, _ in SMALL], SMALL_ROWS)
    big_all, small_all = _allgather([big_local, small_local])
    wts = {}
    for (n, sh, ax), part in zip(BIG, _split_flat(big_all.reshape(N_DEV, -1), big_sizes)):
        wts[n] = _from_chunks(part, sh, ax)
    for (n, sh, ax), part in zip(SMALL, _split_flat(small_all.reshape(N_DEV, -1), small_sizes)):
        wts[n] = _from_chunks(part, sh, ax)
    for n, sh in REPL:
        wts[n] = w_in[n].reshape(sh)

    def early_exchange(g):
        small_chunks = jnp.moveaxis(g["lru_small"].reshape(SUBLANES, N_DEV, -1), 1, 0)
        sends = [(g["lru_w_in"], "cols"),
                 (g["w_mem_kv"][0], "rows"), (g["w_mem_kv"][1], "rows"),
                 (g["w_out"][0], "rows"), (g["w_out"][1], "rows"),
                 (small_chunks, "chunks"), (g["lru_w_rgate"], "all"), (g["lru_w_igate"], "all")]
        return _Exchange([a for a, _ in sends], [k for _, k in sends])

    gx, _, got_early, late = _local_step(x[0], mem[0], positions[0], loss_target[0], wts,
                                         ts, tatt, early_exchange)

    def chunked(name, shape):
        w = shape[1] // N_DEV
        return _to_chunks(late[name], 1).reshape(N_DEV, shape[0], w).astype(BF16)

    got_late = _exchange_grads(
        [chunked("mla_w_in", (D_MODEL, MLA_IN)),
         chunked("mla_w_uq", (Q_LORA, N_TOK_HEADS * QK_DIM)),
         chunked("mla_w_ukv", (KV_LORA, N_TOK_HEADS * 2 * HEAD_DIM)), late["small_repl"]],
        ["chunks", "chunks", "chunks", "all"], "exchange_grads")
    got = list(got_late[:3]) + list(got_early) + [got_late[3]]

    def flat2(d, name):
        t = d[name]
        return t.reshape(-1, t.shape[-1])

    def update(parts, view, name):
        return _adamw(parts, view(w_in), view(m_in), view(v_in), "adamw_" + name)

    res = {}
    for idx, name in [(0, "mla_w_in"), (1, "mla_w_uq"), (2, "mla_w_ukv"), (3, "lru_w_in"),
                      (9, "lru_w_rgate"), (10, "lru_w_igate")]:
        res[name] = update([got[idx]], functools.partial(flat2, name=name), name)
    res["w_mem_kv"] = update([got[4], got[5]], functools.partial(flat2, name="w_mem_kv"),
                             "w_mem_kv")
    res["w_out"] = update([got[6], got[7]], functools.partial(flat2, name="w_out"), "w_out")
    def small(parts, names, layout, name):
        view = lambda d: [flat2(d, n) for n in names]
        tile, outs = _adamw_small(parts, layout, view(w_in), view(m_in), view(v_in), name)
        for n, o in zip(names, outs):
            res[n] = o
        return tile

    taps = list(range(CONV_W))
    small(got[8], ["lru_conv_w", "lru_conv_b", "lru_b_rgate", "lru_b_igate", "lru_lambda"],
          [(taps, 0, TOK_WIDTH // N_DEV)] + [([4 + a], 0, TOK_WIDTH // N_DEV) for a in range(4)],
          "adamw_small_sharded")
    tile = small(got[11], ["ln_g", "ln_b", "mla_q_norm", "mla_kv_norm"],
                 [([0, 2], 0, D_MODEL), ([1, 3], 0, D_MODEL), ([4], 0, Q_LORA),
                  ([4], Q_LORA, KV_LORA)], "adamw_small_replicated")
    loss = tile[5, 0]

    result = [loss, gx.reshape(x.shape)]
    for kind in range(4):
        result += [res[n][kind].reshape(w_in[n].shape) for n in WEIGHT_ORDER]
    return tuple(result)
```

```python
import functools
import math

import jax
import jax.numpy as jnp
from jax import lax
from jax.experimental import pallas as pl
from jax.experimental.pallas import tpu as pltpu

F32 = jnp.float32
BF16 = jnp.bfloat16

D_MODEL = 1024
MEM_LEN = 256
HEAD_DIM = 64
N_TOK_HEADS = 12
N_MEM_HEADS = 4
TOK_WIDTH = 768
MEM_WIDTH = 256
MIX_WIDTH = 1024
Q_LORA = 384
KV_LORA = 256
QK_NOPE = 64
QK_ROPE = 32
QK_DIM = 96
ROPE_THETA = 10000.0
CONV_W = 4
LRU_C = 8.0
ALPHA = (2.0 * 2) ** 0.25
NORM_EPS = 1e-6
MLA_IN = 1952
LRU_IN = 2048
ADAM_LR = 0.001
ADAM_B1 = 0.9
ADAM_B2 = 0.999
ADAM_EPS = 1e-08
ADAM_WD = 0.01
ADAM_STEP = 10

N_DEV = 8
LANES = 128
SUBLANES = 8
HEAD_PAD = 128
QKV_PAD = N_TOK_HEADS * HEAD_PAD
ZP = 2048
ZA_W = TOK_WIDTH
ZG_W = MIX_WIDTH + MEM_WIDTH
ZA_CQ, ZA_CKV, ZA_KR = 0, 384, 640
KR_LANE = 64

ROW_BLOCK = 512
ATT_BLOCK = 512
LOOKAHEAD = 3
FWD_HEADS = 12
BWD_HEADS = 4
VMEM_LIMIT = 56 * 1024 * 1024
NEG_BIG = -1e30
STRIP = 32
SCAN_SEGMENTS = 4
LOG2E = math.log2(math.e)


def _cp(n_axes):
    return pltpu.CompilerParams(dimension_semantics=("arbitrary",) * n_axes,
                                vmem_limit_bytes=VMEM_LIMIT)


def _dot(a, b):
    return jnp.dot(a, b, preferred_element_type=F32)


def _dot_nt(a, b):
    return lax.dot_general(a, b, (((1,), (1,)), ((), ())), preferred_element_type=F32)


def _dot_tn(a, b):
    return lax.dot_general(a, b, (((0,), (0,)), ((), ())), preferred_element_type=F32)


def _sigmoid(t):
    return 1.0 / (1.0 + jnp.exp(-t))


def _lane(shape):
    return lax.broadcasted_iota(jnp.int32, shape, len(shape) - 1)


def _full(shape):
    nd = len(shape)
    return pl.BlockSpec(shape, lambda *_: (0,) * nd)


def _rows(ts, width, col=0):
    return pl.BlockSpec((ts, width), lambda i: (i, col))


def _heads(ts):
    return pl.BlockSpec((N_TOK_HEADS, ts, HEAD_PAD), lambda i: (0, i, 0))


def _rowmm(x, w, widths, name, ts):
    s, k = x.shape
    n = w.shape[1]
    offs = [sum(widths[:a]) for a in range(len(widths))]

    def body(x_ref, w_ref, *o_refs):
        res = _dot(x_ref[...].astype(BF16), w_ref[...])
        for o_ref, off, wd in zip(o_refs, offs, widths):
            o_ref[...] = res[:, off:off + wd]

    return pl.pallas_call(
        body, grid=(s // ts,),
        in_specs=[_rows(ts, k), _full((k, n))],
        out_specs=[_rows(ts, wd) for wd in widths],
        out_shape=[jax.ShapeDtypeStruct((s, wd), F32) for wd in widths],
        name=name, compiler_params=_cp(1))(x, w)


def _rms_parts(t):
    rs = lax.rsqrt(jnp.mean(t * t, axis=-1, keepdims=True) + NORM_EPS)
    return rs


def _rope_terms(c, sn):
    first_half = _lane(sn.shape) < KR_LANE + QK_ROPE // 2
    return c, jnp.where(first_half, -sn, 0.0), jnp.where(first_half, 0.0, sn)


def _rope(t, c, sa, sb):
    return t * c + pltpu.roll(t, LANES - 16, 1) * sa + pltpu.roll(t, 16, 1) * sb


def _rope_t(d, c, sa, sb):
    return d * c + pltpu.roll(d * sa, 16, 1) + pltpu.roll(d * sb, LANES - 16, 1)


def _mla_prep_fwd(x, win, tabs, gq, gkv, wuq, wukv, ts):
    s = x.shape[0]

    def body(x_ref, win_ref, c_ref, sn_ref, gq_ref, gkv_ref, wuq_ref, wukv_ref,
             z_ref, zg_ref, q_ref, k_ref, v_ref):
        zfull = _dot(x_ref[...].astype(BF16), win_ref[...])
        z_ref[...] = zfull[:, 0:ZA_W]
        zg_ref[...] = zfull[:, ZA_W:ZP]
        cq = zfull[:, ZA_CQ:ZA_CQ + Q_LORA]
        ckv = zfull[:, ZA_CKV:ZA_CKV + KV_LORA]
        kr = zfull[:, ZA_KR:ZA_KR + LANES]
        cqn = cq * _rms_parts(cq) * gq_ref[...]
        ckvn = ckv * _rms_parts(ckv) * gkv_ref[...]
        q = _dot(cqn.astype(BF16), wuq_ref[...])
        kv = _dot(ckvn.astype(BF16), wukv_ref[...])
        c, sa, sb = _rope_terms(c_ref[...], sn_ref[...])
        krope = _rope(kr, c, sa, sb)
        pad_lane = _lane((ts, HEAD_PAD)) >= HEAD_DIM
        for h in range(N_TOK_HEADS):
            sl = slice(h * HEAD_PAD, (h + 1) * HEAD_PAD)
            q_ref[h] = _rope(q[:, sl], c, sa, sb).astype(BF16)
            k_ref[h] = (kv[:, sl] + krope).astype(BF16)
            vh = kv[:, QKV_PAD + h * HEAD_PAD:QKV_PAD + (h + 1) * HEAD_PAD]
            v_ref[h] = jnp.where(pad_lane, 1.0, vh).astype(BF16)

    out = jax.ShapeDtypeStruct((N_TOK_HEADS, s, HEAD_PAD), BF16)
    return pl.pallas_call(
        body, grid=(s // ts,),
        in_specs=[_rows(ts, D_MODEL), _full((D_MODEL, ZP)),
                  _rows(ts, LANES), _rows(ts, LANES),
                  _full((1, Q_LORA)), _full((1, KV_LORA)),
                  _full((Q_LORA, QKV_PAD)), _full((KV_LORA, 2 * QKV_PAD))],
        out_specs=[_rows(ts, ZA_W), _rows(ts, ZG_W)] + [_heads(ts)] * 3,
        out_shape=[jax.ShapeDtypeStruct((s, ZA_W), F32), jax.ShapeDtypeStruct((s, ZG_W), F32),
                   out, out, out],
        name="mla_prep_fwd", compiler_params=_cp(1))(x, win, *tabs, gq, gkv, wuq, wukv)


def _mla_prep_bwd(z0, dq, dk, dv, tabs, gq, gkv, wuq_t, wukv_t, ts):
    s = z0.shape[0]

    def body(z_ref, dq_ref, dk_ref, dv_ref, c_ref, sn_ref, gq_ref, gkv_ref,
             wuqt_ref, wukvt_ref, dza_ref, dzk_ref, dwuq_ref, dwukv_ref, dg_ref):
        @pl.when(pl.program_id(0) == 0)
        def _():
            dwuq_ref[...] = jnp.zeros_like(dwuq_ref)
            dwukv_ref[...] = jnp.zeros_like(dwukv_ref)
            dg_ref[...] = jnp.zeros_like(dg_ref)

        cq = z_ref[:, ZA_CQ:ZA_CQ + Q_LORA]
        ckv = z_ref[:, ZA_CKV:ZA_CKV + KV_LORA]
        rq, rkv = _rms_parts(cq), _rms_parts(ckv)
        gq_, gkv_ = gq_ref[...], gkv_ref[...]
        cqn = (cq * rq * gq_).astype(BF16)
        ckvn = (ckv * rkv * gkv_).astype(BF16)
        c, sa, sb = _rope_terms(c_ref[...], sn_ref[...])
        dqp, dksum = [], None
        for h in range(N_TOK_HEADS):
            dqp.append(_rope_t(dq_ref[h], c, sa, sb))
            dksum = dk_ref[h] if dksum is None else dksum + dk_ref[h]
        dqp = jnp.concatenate(dqp, axis=1).astype(BF16)
        lane = _lane(dksum.shape)
        dzk_ref[...] = jnp.where((lane >= KR_LANE) & (lane < KR_LANE + QK_ROPE),
                                 _rope_t(dksum, c, sa, sb), 0.0).astype(BF16)
        dkv = jnp.concatenate([dk_ref[h].astype(BF16) for h in range(N_TOK_HEADS)]
                              + [dv_ref[h] for h in range(N_TOK_HEADS)], axis=1)
        dcqn = _dot(dqp, wuqt_ref[...])
        dckvn = _dot(dkv, wukvt_ref[...])
        dwuq_ref[...] += _dot_tn(cqn, dqp)
        dwukv_ref[...] += _dot_tn(ckvn, dkv)
        dg_ref[0:1, 0:Q_LORA] += jnp.sum(dcqn * cq * rq, axis=0, keepdims=True)
        dg_ref[0:1, Q_LORA:Q_LORA + KV_LORA] += jnp.sum(dckvn * ckv * rkv, axis=0, keepdims=True)
        wq = dcqn * gq_
        wkv = dckvn * gkv_
        dcq = rq * wq - cq * (rq * rq * rq) * jnp.mean(wq * cq, axis=-1, keepdims=True)
        dckv = rkv * wkv - ckv * (rkv * rkv * rkv) * jnp.mean(wkv * ckv, axis=-1, keepdims=True)
        dza_ref[:, 0:Q_LORA] = dcq.astype(BF16)
        dza_ref[:, Q_LORA:Q_LORA + KV_LORA] = dckv.astype(BF16)

    na = Q_LORA + KV_LORA
    return pl.pallas_call(
        body, grid=(s // ts,),
        in_specs=[_rows(ts, ZA_W), _heads(ts), _heads(ts), _heads(ts),
                  _rows(ts, LANES), _rows(ts, LANES),
                  _full((1, Q_LORA)), _full((1, KV_LORA)),
                  _full((QKV_PAD, Q_LORA)), _full((2 * QKV_PAD, KV_LORA))],
        out_specs=[_rows(ts, na), _rows(ts, LANES), _full((Q_LORA, QKV_PAD)),
                   _full((KV_LORA, 2 * QKV_PAD)), _full((SUBLANES, na))],
        out_shape=[jax.ShapeDtypeStruct((s, na), BF16), jax.ShapeDtypeStruct((s, LANES), BF16),
                   jax.ShapeDtypeStruct((Q_LORA, QKV_PAD), F32),
                   jax.ShapeDtypeStruct((KV_LORA, 2 * QKV_PAD), F32),
                   jax.ShapeDtypeStruct((SUBLANES, na), F32)],
        name="mla_prep_bwd", compiler_params=_cp(1))(
            z0, dq, dk, dv, *tabs, gq, gkv, wuq_t, wukv_t)


def _causal_pairs(nb, by_key):
    if by_key:
        pairs = [(i, j) for j in range(nb) for i in range(j, nb)]
    else:
        pairs = [(i, j) for i in range(nb) for j in range(i + 1)]
    return (jnp.array([p[0] for p in pairs], jnp.int32),
            jnp.array([p[1] for p in pairs], jnp.int32))


def _flash_fwd(q, k, v, t, nh):
    s = q.shape[1]
    itab, jtab = _causal_pairs(s // t, False)
    c2 = LOG2E / math.sqrt(QK_DIM)

    def body(it_ref, jt_ref, q_ref, k_ref, v_ref, o_ref, lse_ref, m_scr, acc_scr):
        pair = pl.program_id(1)
        i, j = it_ref[pair], jt_ref[pair]

        @pl.when(j == 0)
        def _():
            m_scr[...] = jnp.full_like(m_scr, NEG_BIG)
            acc_scr[...] = jnp.zeros_like(acc_scr)

        def softmax_strips(masked, hs, sc, row0):
            ps, als = [], []
            for r0 in range(0, sc.shape[0], STRIP):
                rows = slice(row0 + r0, row0 + r0 + STRIP)
                ch = [sc[r0:r0 + STRIP, n * LANES:(n + 1) * LANES] * c2
                      for n in range(sc.shape[1] // LANES)]
                if masked:
                    rr = row0 + r0 + lax.broadcasted_iota(jnp.int32, (STRIP, LANES), 0)
                    cc = lax.broadcasted_iota(jnp.int32, (STRIP, LANES), 1)
                    ch = [jnp.where(cc + n * LANES <= rr, c_, NEG_BIG) for n, c_ in enumerate(ch)]
                mx = ch[0]
                for c_ in ch[1:]:
                    mx = jnp.maximum(mx, c_)
                m_prev = m_scr[hs, rows, :]
                m_next = jnp.maximum(m_prev, jnp.max(mx, axis=-1, keepdims=True))
                ps.append(jnp.concatenate(
                    [jnp.exp2(c_ - m_next).astype(BF16) for c_ in ch], axis=1))
                als.append(jnp.exp2(m_prev - m_next))
                m_scr[hs, rows, :] = m_next
            return jnp.concatenate(ps, axis=0), jnp.concatenate(als, axis=0)

        def run(masked, parts):
            def scores_of(hs):
                return [_dot_nt(q_ref[hs, r0:r0 + nr, :], k_ref[hs, 0:nk, :])
                        for r0, nr, nk in parts]

            ahead = min(LOOKAHEAD, nh)
            scores = [scores_of(hs) for hs in range(ahead)]
            for hs in range(nh):
                if hs + ahead < nh:
                    scores.append(scores_of(hs + ahead))
                for (r0, nr, nk), sc in zip(parts, scores[hs]):
                    p, alpha = softmax_strips(masked, hs, sc, r0)
                    acc_scr[hs, r0:r0 + nr, :] = (alpha * acc_scr[hs, r0:r0 + nr, :]
                                                  + _dot(p, v_ref[hs, 0:nk, :]))

        @pl.when(j < i)
        def _():
            run(False, [(0, t, t)])

        @pl.when(j == i)
        def _():
            run(True, [(0, t, t)])
            for h in range(nh):
                acc = acc_scr[h]
                l = acc[:, HEAD_DIM:HEAD_DIM + 1]
                o_ref[h] = jnp.where(_lane(acc.shape) < HEAD_DIM, acc / l, 0.0)
                lse_ref[h] = m_scr[h] + jnp.log2(l)

    qspec = pl.BlockSpec((nh, t, HEAD_PAD), lambda h, p, it, jt: (h, it[p], 0))
    kspec = pl.BlockSpec((nh, t, HEAD_PAD), lambda h, p, it, jt: (h, jt[p], 0))
    out = jax.ShapeDtypeStruct((N_TOK_HEADS, s, HEAD_PAD), F32)
    return pl.pallas_call(
        body,
        grid_spec=pltpu.PrefetchScalarGridSpec(
            num_scalar_prefetch=2, grid=(N_TOK_HEADS // nh, itab.shape[0]),
            in_specs=[qspec, kspec, kspec], out_specs=[qspec, qspec],
            scratch_shapes=[pltpu.VMEM((nh, t, HEAD_PAD), F32)] * 2),
        out_shape=[out, out],
        name="flash_fwd", compiler_params=_cp(2))(itab, jtab, q, k, v)


def _flash_bwd(q, k, v, stats, do, t, nh, ex):
    s = q.shape[1]
    nb = s // t
    itab, jtab = _causal_pairs(nb, True)
    npairs = itab.shape[0]
    ngroups = N_TOK_HEADS // nh
    scale = 1.0 / math.sqrt(QK_DIM)
    c2 = LOG2E * scale
    nx = ex.n if ex is not None else 0
    ex_arrays, ex_out_shape, ex_scratch = (
        (ex.arrays, ex.out_shape, ex.scratch) if ex is not None else ([], [], []))

    def body(it_ref, jt_ref, q_ref, k_ref, v_ref, st_ref, do_ref, *rest):
        ex_in, rest = rest[:nx], rest[nx:]
        dq_ref, dk_ref, dv_ref = rest[:3]
        ex_out, rest = rest[3:3 + nx], rest[3 + nx:]
        dk_scr, dv_scr = rest[:2]
        ex_sems = rest[2:]
        pair = pl.program_id(1)
        i, j = it_ref[pair], jt_ref[pair]
        rows_i = pl.ds(pl.multiple_of(i * t, t), t)

        if nx:
            @pl.when(jnp.logical_and(pl.program_id(0) == 0, pair == 0))
            def _():
                for cp in ex.copies(ex_in, ex_out, ex_sems):
                    cp.start()

        @pl.when(i == j)
        def _():
            dk_scr[...] = jnp.zeros_like(dk_scr)
            dv_scr[...] = jnp.zeros_like(dv_scr)

        @pl.when(j == 0)
        def _():
            dq_ref[:, rows_i, :] = jnp.zeros((nh, t, HEAD_PAD), F32)

        def prob_strips(masked, h, sct, dpt, k0, q0):
            ps, dss = [], []
            for r0 in range(0, sct.shape[0], STRIP):
                rows = slice(r0, r0 + STRIP)
                if masked:
                    kk = k0 + r0 + lax.broadcasted_iota(jnp.int32, (STRIP, LANES), 0)
                    qq = q0 + lax.broadcasted_iota(jnp.int32, (STRIP, LANES), 1)
                pcs, dcs = [], []
                for n in range(sct.shape[1] // LANES):
                    cols = slice(n * LANES, (n + 1) * LANES)
                    qcols = slice(q0 + n * LANES, q0 + (n + 1) * LANES)
                    x = sct[rows, cols] * c2
                    if masked:
                        x = jnp.where(kk <= qq + n * LANES, x, NEG_BIG)
                    p = jnp.exp2(x - st_ref[h, 0:1, qcols])
                    pcs.append(p.astype(BF16))
                    dcs.append((p * (dpt[rows, cols] - st_ref[h, 1:2, qcols]) * scale).astype(BF16))
                ps.append(jnp.concatenate(pcs, axis=1))
                dss.append(jnp.concatenate(dcs, axis=1))
            return jnp.concatenate(ps, axis=0), jnp.concatenate(dss, axis=0)

        def run(masked, parts):
            def scores_of(h):
                return [(_dot_nt(k_ref[h, k0:k0 + nk, :], q_ref[h, q0:q0 + nq, :]),
                         _dot_nt(v_ref[h, k0:k0 + nk, :], do_ref[h, q0:q0 + nq, :]))
                        for k0, nk, q0, nq in parts]

            ahead = min(LOOKAHEAD, nh)
            scores = [scores_of(h) for h in range(ahead)]
            for h in range(nh):
                if h + ahead < nh:
                    scores.append(scores_of(h + ahead))
                for (k0, nk, q0, nq), (sct, dpt) in zip(parts, scores[h]):
                    pt, dst = prob_strips(masked, h, sct, dpt, k0, q0)
                    dv_scr[h, k0:k0 + nk, :] += _dot(pt, do_ref[h, q0:q0 + nq, :])
                    dk_scr[h, k0:k0 + nk, :] += _dot(dst, q_ref[h, q0:q0 + nq, :])
                    rows = pl.ds(pl.multiple_of(i * t + q0, t // 2), nq)
                    dq_ref[h, rows, :] += _dot_tn(dst, k_ref[h, k0:k0 + nk, :])

        @pl.when(i > j)
        def _():
            run(False, [(0, t, 0, t)])

        @pl.when(i == j)
        def _():
            run(True, [(0, t // 2, 0, t), (t // 2, t // 2, t // 2, t // 2)])

        @pl.when(i == nb - 1)
        def _():
            dk_ref[...] = dk_scr[...]
            dv_ref[...] = dv_scr[...].astype(BF16)

        if nx:
            @pl.when(jnp.logical_and(pl.program_id(0) == ngroups - 1, pair == npairs - 1))
            def _():
                for cp in ex.copies(ex_in, ex_out, ex_sems):
                    cp.wait()

    qspec = pl.BlockSpec((nh, t, HEAD_PAD), lambda h, p, it, jt: (h, it[p], 0))
    kspec = pl.BlockSpec((nh, t, HEAD_PAD), lambda h, p, it, jt: (h, jt[p], 0))
    dqspec = pl.BlockSpec((nh, s, HEAD_PAD), lambda h, p, it, jt: (h, 0, 0))
    stspec = pl.BlockSpec((nh, 2, t), lambda h, p, it, jt: (h, 0, it[p]))
    out = jax.ShapeDtypeStruct((N_TOK_HEADS, s, HEAD_PAD), F32)
    res = pl.pallas_call(
        body,
        grid_spec=pltpu.PrefetchScalarGridSpec(
            num_scalar_prefetch=2, grid=(ngroups, npairs),
            in_specs=[qspec, kspec, kspec, stspec, qspec] + [ANY] * nx,
            out_specs=[dqspec, kspec, kspec] + [ANY] * nx,
            scratch_shapes=[pltpu.VMEM((nh, t, HEAD_PAD), F32)] * 2 + ex_scratch),
        out_shape=[out, out, jax.ShapeDtypeStruct(out.shape, BF16)] + ex_out_shape,
        name="flash_bwd", compiler_params=_cp(2))(itab, jtab, q, k, v, stats, do, *ex_arrays)
    return res[:3], res[3:]


def _mem_probs(qp, kp, hh):
    lane = _lane(qp.shape)
    keep = (lane < HEAD_DIM) if hh == 0 else (lane >= HEAD_DIM)
    qh = jnp.where(keep, qp, 0.0).astype(BF16)
    sc = _dot_nt(qh, kp) * (1.0 / math.sqrt(HEAD_DIM))
    e = jnp.exp(sc - jnp.max(sc, axis=-1, keepdims=True))
    return e / jnp.sum(e, axis=-1, keepdims=True), keep


def _mix_out_fwd(tok, z, memkv, w, h, g, b, tgt, g0, q0, padded, name, ts):
    s = z.shape[0]
    zw = z.shape[1]
    tok_spec = _heads(ts) if padded else _rows(ts, TOK_WIDTH)
    with_loss = tgt is not None

    def body(*refs):
        tok_ref, z_ref, mkv_ref, w_ref, h_ref, g_ref, b_ref = refs[:7]
        if with_loss:
            t_ref, cat_ref, y_ref, dpre_ref, dgb_ref, loss_ref = refs[7:]
        else:
            cat_ref, y_ref, pre_ref, out_ref = refs[7:]
        if padded:
            for p in range(N_TOK_HEADS // 2):
                cat_ref[:, p * LANES:(p + 1) * LANES] = (
                    tok_ref[2 * p] + pltpu.roll(tok_ref[2 * p + 1], HEAD_DIM, 1))
        else:
            cat_ref[:, 0:TOK_WIDTH] = tok_ref[...]
        for pr in range(N_MEM_HEADS // 2):
            sl = slice(pr * LANES, (pr + 1) * LANES)
            qp = z_ref[:, q0 + pr * LANES:q0 + (pr + 1) * LANES]
            kp = mkv_ref[:, sl].astype(BF16)
            vp = mkv_ref[:, MEM_WIDTH + pr * LANES:MEM_WIDTH + (pr + 1) * LANES].astype(BF16)
            outs = []
            for hh in range(2):
                p, _ = _mem_probs(qp, kp, hh)
                outs.append(_dot(p.astype(BF16), vp))
            lane = _lane(outs[0].shape)
            cat_ref[:, TOK_WIDTH + pr * LANES:TOK_WIDTH + (pr + 1) * LANES] = jnp.where(
                lane < HEAD_DIM, outs[0], outs[1])
        gate = z_ref[:, g0:g0 + MIX_WIDTH]
        yb = (cat_ref[...] * (gate * _sigmoid(gate))).astype(BF16)
        y_ref[...] = yb
        pre = ALPHA * h_ref[...] + _dot(yb, w_ref[...])
        xhat, rstd = _ln_stats(pre)
        hout = xhat * g_ref[...] + b_ref[...]
        if with_loss:
            @pl.when(pl.program_id(0) == 0)
            def _():
                loss_ref[...] = jnp.zeros_like(loss_ref)
                dgb_ref[...] = jnp.zeros_like(dgb_ref)
            err = hout - t_ref[...]
            loss_ref[...] += 0.5 * jnp.sum(jnp.mean(err * err, axis=-1, keepdims=True))
            dh = err * (1.0 / D_MODEL)
            dpre_ref[...] = _ln_bwd(dh, xhat, rstd, g_ref[...])
            dgb_ref[0:1, :] += jnp.sum(dh * xhat, axis=0, keepdims=True)
            dgb_ref[1:2, :] += jnp.sum(dh, axis=0, keepdims=True)
        else:
            pre_ref[...] = pre
            out_ref[...] = hout

    act = jax.ShapeDtypeStruct((s, D_MODEL), F32)
    in_specs = [tok_spec, _rows(ts, zw), _full((MEM_LEN, 2 * MEM_WIDTH)),
                _full((MIX_WIDTH, D_MODEL)), _rows(ts, D_MODEL),
                _full((1, D_MODEL)), _full((1, D_MODEL))]
    out_specs = [_rows(ts, MIX_WIDTH)] * 2
    out_shape = [jax.ShapeDtypeStruct((s, MIX_WIDTH), F32),
                 jax.ShapeDtypeStruct((s, MIX_WIDTH), BF16)]
    args = [tok, z, memkv, w, h, g, b]
    if with_loss:
        in_specs.append(_rows(ts, D_MODEL))
        out_specs += [_rows(ts, D_MODEL), _full((SUBLANES, D_MODEL)), _full((SUBLANES, LANES))]
        out_shape += [act, jax.ShapeDtypeStruct((SUBLANES, D_MODEL), F32),
                      jax.ShapeDtypeStruct((SUBLANES, LANES), F32)]
        args.append(tgt)
    else:
        out_specs += [_rows(ts, D_MODEL)] * 2
        out_shape += [act, act]
    return pl.pallas_call(
        body, grid=(s // ts,), in_specs=in_specs, out_specs=out_specs, out_shape=out_shape,
        name=name, compiler_params=_cp(1))(*args)


def _gate_mem_bwd(dpre, y, w_t, cat, z, memkv, lse, g0, q0, name, ts):
    s = z.shape[0]
    zw = z.shape[1]
    padded = lse is not None
    gq_w = MIX_WIDTH + MEM_WIDTH

    def body(*refs):
        if padded:
            (dpre_ref, y_ref, wt_ref, cat_ref, z_ref, mkv_ref, lse_ref,
             dzg_ref, dtok_ref, dmkv_ref, dw_ref, st_ref) = refs
        else:
            (dpre_ref, y_ref, wt_ref, cat_ref, z_ref, mkv_ref,
             dzg_ref, dtok_ref, dmkv_ref, dw_ref) = refs

        @pl.when(pl.program_id(0) == 0)
        def _():
            dmkv_ref[...] = jnp.zeros_like(dmkv_ref)
            dw_ref[...] = jnp.zeros_like(dw_ref)

        dpb = dpre_ref[...].astype(BF16)
        dy_ = _dot(dpb, wt_ref[...])
        dw_ref[...] += _dot_tn(y_ref[...], dpb)
        gate = z_ref[:, g0:g0 + MIX_WIDTH]
        sg = _sigmoid(gate)
        dzg_ref[:, 0:MIX_WIDTH] = (dy_ * cat_ref[...]
                                   * (sg * (1.0 + gate * (1.0 - sg)))).astype(BF16)
        dcat = dy_ * (gate * sg)
        if padded:
            low = _lane((ts, LANES)) < HEAD_DIM
            for p in range(N_TOK_HEADS // 2):
                d = dcat[:, p * LANES:(p + 1) * LANES]
                prod = d * cat_ref[:, p * LANES:(p + 1) * LANES]
                first = jnp.sum(jnp.where(low, prod, 0.0), axis=-1, keepdims=True)
                second = jnp.sum(jnp.where(low, 0.0, prod), axis=-1, keepdims=True)
                dtok_ref[2 * p] = jnp.where(low, d, 0.0).astype(BF16)
                dtok_ref[2 * p + 1] = jnp.where(low, pltpu.roll(d, HEAD_DIM, 1), 0.0).astype(BF16)
                for hh, delta in ((2 * p, first), (2 * p + 1, second)):
                    both = jnp.where(low, lse_ref[hh], delta).T
                    st_ref[hh, 0:1, :] = both[0:1, :]
                    st_ref[hh, 1:2, :] = both[HEAD_DIM:HEAD_DIM + 1, :]
        else:
            dtok_ref[...] = dcat[:, 0:TOK_WIDTH]
        for pr in range(N_MEM_HEADS // 2):
            sl = slice(pr * LANES, (pr + 1) * LANES)
            vsl = slice(MEM_WIDTH + pr * LANES, MEM_WIDTH + (pr + 1) * LANES)
            qp = z_ref[:, q0 + pr * LANES:q0 + (pr + 1) * LANES]
            qpb = qp.astype(BF16)
            kp = mkv_ref[:, sl].astype(BF16)
            vp = mkv_ref[:, vsl].astype(BF16)
            dmo = dcat[:, TOK_WIDTH + pr * LANES:TOK_WIDTH + (pr + 1) * LANES]
            dqp = None
            for hh in range(2):
                p, keep = _mem_probs(qp, kp, hh)
                do_h = jnp.where(keep, dmo, 0.0).astype(BF16)
                dmkv_ref[:, vsl] += _dot_tn(p.astype(BF16), do_h)
                dp = _dot_nt(do_h, vp)
                ds = (p * (dp - jnp.sum(dp * p, axis=-1, keepdims=True))
                      * (1.0 / math.sqrt(HEAD_DIM))).astype(BF16)
                dqh = jnp.where(keep, _dot(ds, kp), 0.0)
                dqp = dqh if dqp is None else dqp + dqh
                dkh = _dot_tn(ds, qpb)
                klane = _lane(dkh.shape)
                kkeep = (klane < HEAD_DIM) if hh == 0 else (klane >= HEAD_DIM)
                dmkv_ref[:, sl] += jnp.where(kkeep, dkh, 0.0)
            dzg_ref[:, MIX_WIDTH + pr * LANES:MIX_WIDTH + (pr + 1) * LANES] = dqp.astype(BF16)

    in_specs = [_rows(ts, D_MODEL), _rows(ts, MIX_WIDTH), _full((D_MODEL, MIX_WIDTH)),
                _rows(ts, MIX_WIDTH), _rows(ts, zw), _full((MEM_LEN, 2 * MEM_WIDTH))]
    out_specs = [_rows(ts, gq_w), _heads(ts) if padded else _rows(ts, TOK_WIDTH),
                 _full((MEM_LEN, 2 * MEM_WIDTH)), _full((MIX_WIDTH, D_MODEL))]
    heads_shape = (N_TOK_HEADS, s, HEAD_PAD)
    out_shape = [jax.ShapeDtypeStruct((s, gq_w), BF16),
                 jax.ShapeDtypeStruct(heads_shape, BF16) if padded
                 else jax.ShapeDtypeStruct((s, TOK_WIDTH), F32),
                 jax.ShapeDtypeStruct((MEM_LEN, 2 * MEM_WIDTH), F32),
                 jax.ShapeDtypeStruct((MIX_WIDTH, D_MODEL), F32)]
    args = [dpre, y, w_t, cat, z, memkv]
    if padded:
        in_specs.append(_heads(ts))
        out_specs.append(pl.BlockSpec((N_TOK_HEADS, 2, ts), lambda i: (0, 0, i)))
        out_shape.append(jax.ShapeDtypeStruct((N_TOK_HEADS, 2, s), F32))
        args.append(lse)
    return pl.pallas_call(
        body, grid=(s // ts,), in_specs=in_specs, out_specs=out_specs, out_shape=out_shape,
        name=name, compiler_params=_cp(1))(*args)


def _ln_stats(pre):
    mu = jnp.mean(pre, axis=-1, keepdims=True)
    d = pre - mu
    rstd = lax.rsqrt(jnp.mean(d * d, axis=-1, keepdims=True) + NORM_EPS)
    return d * rstd, rstd


def _ln_bwd(dh, xhat, rstd, g):
    dxh = dh * g
    return rstd * (dxh - jnp.mean(dxh, axis=-1, keepdims=True)
                   - xhat * jnp.mean(dxh * xhat, axis=-1, keepdims=True))


def _linear_bwd(x, dys, offs, w_t, resid, ln, name, ts):
    s, kdim = x.shape
    n = w_t.shape[0]
    widths = [d.shape[1] for d in dys]
    npieces = len(dys)
    with_ln = ln is not None

    def body(*refs):
        x_ref = refs[0]
        dy_refs = refs[1:1 + npieces]
        if with_ln:
            wt_ref, r_ref, pre_ref, g_ref, dx_ref, dw_ref, dgb_ref = refs[1 + npieces:]
        else:
            wt_ref, r_ref, dx_ref, dw_ref = refs[1 + npieces:]

        @pl.when(pl.program_id(0) == 0)
        def _():
            dw_ref[...] = jnp.zeros_like(dw_ref)
            if with_ln:
                dgb_ref[...] = jnp.zeros_like(dgb_ref)

        xb = x_ref[...].astype(BF16)
        dx = ALPHA * r_ref[...]
        for dy_ref, off, wd in zip(dy_refs, offs, widths):
            dyb = dy_ref[...].astype(BF16)
            dx = dx + _dot(dyb, wt_ref[off:off + wd, :])
            dw_ref[:, off:off + wd] += _dot_tn(xb, dyb)
        if with_ln:
            xhat, rstd = _ln_stats(pre_ref[...])
            dx_ref[...] = _ln_bwd(dx, xhat, rstd, g_ref[...])
            dgb_ref[0:1, :] += jnp.sum(dx * xhat, axis=0, keepdims=True)
            dgb_ref[1:2, :] += jnp.sum(dx, axis=0, keepdims=True)
        else:
            dx_ref[...] = dx

    in_specs = ([_rows(ts, kdim)] + [_rows(ts, wd) for wd in widths]
                + [_full((n, kdim)), _rows(ts, kdim)])
    out_specs = [_rows(ts, kdim), _full((kdim, n))]
    out_shape = [jax.ShapeDtypeStruct((s, kdim), F32), jax.ShapeDtypeStruct((kdim, n), F32)]
    args = [x, *dys, w_t, resid]
    if with_ln:
        in_specs += [_rows(ts, kdim), _full((1, kdim))]
        out_specs.append(_full((SUBLANES, kdim)))
        out_shape.append(jax.ShapeDtypeStruct((SUBLANES, kdim), F32))
        args += list(ln)
    return pl.pallas_call(
        body, grid=(s // ts,), in_specs=in_specs, out_specs=out_specs, out_shape=out_shape,
        name=name, compiler_params=_cp(1))(*args)


def _wgrad_small(x, dy, name):
    def body(x_ref, dy_ref, dw_ref):
        dw_ref[...] = _dot_tn(x_ref[...].astype(BF16), dy_ref[...].astype(BF16))

    return pl.pallas_call(
        body, out_shape=jax.ShapeDtypeStruct((x.shape[1], dy.shape[1]), F32),
        name=name, compiler_params=pltpu.CompilerParams(vmem_limit_bytes=VMEM_LIMIT))(x, dy)


def _shift_down(u, carry8, k):
    if k == 0:
        return u
    rolled = pltpu.roll(u, k, 0)
    row = lax.broadcasted_iota(jnp.int32, carry8.shape, 0)
    top = jnp.where(row < k, pltpu.roll(carry8, k, 0), rolled[0:SUBLANES])
    return jnp.concatenate([top, rolled[SUBLANES:]], axis=0)


def _shift_up(u, carry8, k):
    if k == 0:
        return u
    n = u.shape[0]
    rolled = pltpu.roll(u, n - k, 0)
    row = lax.broadcasted_iota(jnp.int32, carry8.shape, 0)
    bot = jnp.where(row >= SUBLANES - k, pltpu.roll(carry8, SUBLANES - k, 0),
                    rolled[n - SUBLANES:])
    return jnp.concatenate([rolled[:n - SUBLANES], bot], axis=0)


def _neg_expm1(t):
    e = jnp.exp(t)
    em1 = e - 1.0
    safe = jnp.where(e == 1.0, 1.0, jnp.log(e))
    return -jnp.where(e == 1.0, t, jnp.where(em1 == -1.0, -1.0, em1 * t / safe))


def _lru_gates(u, carry8, cw_ref, vec_ref, wr_ref, wi_ref):
    taps = [_shift_down(u, carry8, k) for k in range(CONV_W)]
    xc = vec_ref[0:1, :] + cw_ref[3:4, :] * u
    for k in range(1, CONV_W):
        xc = xc + cw_ref[3 - k:4 - k, :] * taps[k]
    xb = xc.astype(BF16)
    r = _sigmoid(_dot(xb, wr_ref[...]) + vec_ref[1:2, :])
    ig = _sigmoid(_dot(xb, wi_ref[...]) + vec_ref[2:3, :])
    nlam = -vec_ref[3:4, :]
    softplus = jnp.maximum(nlam, 0.0) + jnp.log(1.0 + jnp.exp(-jnp.abs(nlam)))
    cneg = -LRU_C * softplus
    log_a = cneg * r
    a = jnp.exp(log_a)
    sq = jnp.sqrt(_neg_expm1(2.0 * log_a))
    return xc, r, ig, cneg, a, sq, taps


def _chained_scan(a_ref, b_ref, out_ref, cum_scr, x_in):
    rows_total, w = a_ref.shape
    nseg = SCAN_SEGMENTS
    seg = rows_total // nseg

    def step(t, carry):
        xs, ps = carry
        new_x, new_p = [], []
        for sg in range(nseg):
            row = pl.ds(sg * seg + t, 1)
            a = a_ref[row, :]
            x = a * xs[sg] + b_ref[row, :]
            out_ref[row, :] = x
            new_x.append(x)
            if sg > 0:
                p = a * ps[sg - 1]
                cum_scr[row, :] = p
                new_p.append(p)
        return tuple(new_x), tuple(new_p)

    zero, one = jnp.zeros((1, w), F32), jnp.ones((1, w), F32)
    xs, _ = lax.fori_loop(0, seg, step, ((x_in,) + (zero,) * (nseg - 1), (one,) * (nseg - 1)))
    x_prev = xs[0]
    for sg in range(1, nseg):
        rows = slice(sg * seg, (sg + 1) * seg)
        out_ref[rows, :] = out_ref[rows, :] + cum_scr[rows, :] * x_prev
        x_prev = out_ref[(sg + 1) * seg - 1:(sg + 1) * seg, :]
    return x_prev


def _lru_fwd(x, win, cw8, vec8, wr, wi, ts):
    s = x.shape[0]

    def body(x_ref, win_ref, cw_ref, vec_ref, wr_ref, wi_ref, u_ref, zg_ref, hs_ref,
             cu_scr, ch_scr, a_scr, gx_scr, cum_scr):
        @pl.when(pl.program_id(0) == 0)
        def _():
            cu_scr[...] = jnp.zeros_like(cu_scr)
            ch_scr[...] = jnp.zeros_like(ch_scr)

        zfull = _dot(x_ref[...].astype(BF16), win_ref[...])
        u = zfull[:, 0:ZA_W]
        u_ref[...] = u
        zg_ref[...] = zfull[:, ZA_W:ZP]
        xc, _, ig, _, a, sq, _ = _lru_gates(u, cu_scr[...], cw_ref, vec_ref, wr_ref, wi_ref)
        a_scr[...] = a
        gx_scr[...] = sq * (ig * xc)
        ch_scr[0:1, :] = _chained_scan(a_scr, gx_scr, hs_ref, cum_scr, ch_scr[0:1, :])
        cu_scr[...] = u[ts - SUBLANES:, :]

    w = TOK_WIDTH
    return pl.pallas_call(
        body, grid=(s // ts,),
        in_specs=[_rows(ts, D_MODEL), _full((D_MODEL, ZP)),
                  _full((SUBLANES, w)), _full((SUBLANES, w)), _full((w, w)), _full((w, w))],
        out_specs=[_rows(ts, w), _rows(ts, ZG_W), _rows(ts, w)],
        out_shape=[jax.ShapeDtypeStruct((s, w), F32), jax.ShapeDtypeStruct((s, ZG_W), F32),
                   jax.ShapeDtypeStruct((s, w), F32)],
        scratch_shapes=[pltpu.VMEM((SUBLANES, w), F32), pltpu.VMEM((SUBLANES, w), F32)]
                       + [pltpu.VMEM((ts, w), F32)] * 3,
        name="lru_fwd", compiler_params=_cp(1))(x, win, cw8, vec8, wr, wi)


def _lru_bwd(z1, dhs, hs, cw8, vec8, wr, wi, wr_t, wi_t, ts):
    s = z1.shape[0]
    nb = s // ts
    w = TOK_WIDTH
    tiles = ts // SUBLANES

    def body(u_ref, up_ref, dhs_ref, hs_ref, hsp_ref, cw_ref, vec_ref, wr_ref, wi_ref,
             wrt_ref, wit_ref, du_ref, dwr_ref, dwi_ref, dvec_ref,
             cc_scr, cd_scr, a_scr, dh_scr):
        i = pl.program_id(0)

        @pl.when(i == 0)
        def _():
            cc_scr[...] = jnp.zeros_like(cc_scr)
            cd_scr[...] = jnp.zeros_like(cd_scr)
            dwr_ref[...] = jnp.zeros_like(dwr_ref)
            dwi_ref[...] = jnp.zeros_like(dwi_ref)
            dvec_ref[...] = jnp.zeros_like(dvec_ref)

        u = u_ref[...]
        first = i == nb - 1
        carry8 = jnp.where(first, 0.0, up_ref[...])
        xc, r, ig, cneg, a, sq, taps = _lru_gates(u, carry8, cw_ref, vec_ref, wr_ref, wi_ref)
        a_scr[...] = a

        def step(n, c):
            t = ts - 1 - n
            dh = dhs_ref[pl.ds(t, 1), :] + c
            dh_scr[pl.ds(t, 1), :] = dh
            return a_scr[pl.ds(t, 1), :] * dh

        cc_scr[0:1, :] = lax.fori_loop(0, ts, step, cc_scr[0:1, :])
        dh = dh_scr[...]
        hprev = _shift_down(hs_ref[...], jnp.where(first, 0.0, hsp_ref[...]), 1)
        ix = ig * xc
        dix = dh * sq
        dlog_a = dh * hprev * a - (dh * ix) * (a * a) / sq
        dpr = (dlog_a * cneg) * r * (1.0 - r)
        dpi = (dix * xc) * ig * (1.0 - ig)
        dprb, dpib = dpr.astype(BF16), dpi.astype(BF16)
        xb = xc.astype(BF16)
        dwr_ref[...] += _dot_tn(xb, dprb)
        dwi_ref[...] += _dot_tn(xb, dpib)
        dxc = dix * ig + _dot(dprb, wrt_ref[...]) + _dot(dpib, wit_ref[...])
        for k in range(CONV_W):
            dvec_ref[3 - k:4 - k, :] += jnp.sum(dxc * taps[k], axis=0, keepdims=True)
        dvec_ref[4:5, :] += jnp.sum(dxc, axis=0, keepdims=True)
        dvec_ref[5:6, :] += jnp.sum(dpr, axis=0, keepdims=True)
        dvec_ref[6:7, :] += jnp.sum(dpi, axis=0, keepdims=True)
        dvec_ref[7:8, :] += (jnp.sum(dlog_a * r, axis=0, keepdims=True)
                             * (LRU_C * _sigmoid(-vec_ref[3:4, :])))
        nxt = cd_scr[...]
        du = cw_ref[3:4, :] * dxc
        for k in range(1, CONV_W):
            du = du + cw_ref[3 - k:4 - k, :] * _shift_up(dxc, nxt, k)
        du_ref[...] = du.astype(BF16)
        cd_scr[...] = dxc[0:SUBLANES, :]

    rev = lambda i: (nb - 1 - i, 0)
    prev8 = lambda i: (jnp.maximum((nb - 1 - i) * tiles - 1, 0), 0)
    blk = pl.BlockSpec((ts, w), rev)
    before = pl.BlockSpec((SUBLANES, w), prev8)
    scr = pltpu.VMEM((ts, w), F32)
    return pl.pallas_call(
        body, grid=(nb,),
        in_specs=[blk, before, blk, blk, before,
                  _full((SUBLANES, w)), _full((SUBLANES, w)),
                  _full((w, w)), _full((w, w)), _full((w, w)), _full((w, w))],
        out_specs=[blk, _full((w, w)), _full((w, w)), _full((SUBLANES, w))],
        out_shape=[jax.ShapeDtypeStruct((s, w), BF16), jax.ShapeDtypeStruct((w, w), F32),
                   jax.ShapeDtypeStruct((w, w), F32), jax.ShapeDtypeStruct((SUBLANES, w), F32)],
        scratch_shapes=[pltpu.VMEM((SUBLANES, w), F32), pltpu.VMEM((SUBLANES, w), F32),
                        scr, scr],
        name="lru_bwd", compiler_params=_cp(1))(
            z1, z1, dhs, hs, hs, cw8, vec8, wr, wi, wr_t, wi_t)


def _adamw(parts, w, m, v, name):
    n = len(parts)
    rows_per = parts[0].shape[1]

    def body(*refs):
        p_refs = refs[:n]
        w_ref, m_ref, v_ref, g_ref, d_ref, nm_ref, nv_ref = refs[n:]
        for l, p_ref in enumerate(p_refs):
            rows = slice(l * rows_per, (l + 1) * rows_per)
            g = p_ref[0].astype(F32)
            for dev in range(1, N_DEV):
                g = g + p_ref[dev].astype(F32)
            g_ref[rows, :] = g
            nm = ADAM_B1 * m_ref[rows, :] + (1.0 - ADAM_B1) * g
            nv = ADAM_B2 * v_ref[rows, :] + (1.0 - ADAM_B2) * (g * g)
            m_hat = nm / (1.0 - ADAM_B1 ** ADAM_STEP)
            v_hat = nv / (1.0 - ADAM_B2 ** ADAM_STEP)
            d_ref[rows, :] = -ADAM_LR * (m_hat / (jnp.sqrt(v_hat) + ADAM_EPS)
                                         + ADAM_WD * w_ref[rows, :])
            nm_ref[rows, :] = nm
            nv_ref[rows, :] = nv

    out = jax.ShapeDtypeStruct(w.shape, F32)
    return pl.pallas_call(
        body, out_shape=[out] * 4, name=name,
        compiler_params=pltpu.CompilerParams(vmem_limit_bytes=VMEM_LIMIT))(*parts, w, m, v)


def _adam_update(g, w, m, v):
    nm = ADAM_B1 * m + (1.0 - ADAM_B1) * g
    nv = ADAM_B2 * v + (1.0 - ADAM_B2) * (g * g)
    m_hat = nm / (1.0 - ADAM_B1 ** ADAM_STEP)
    v_hat = nv / (1.0 - ADAM_B2 ** ADAM_STEP)
    return -ADAM_LR * (m_hat / (jnp.sqrt(v_hat) + ADAM_EPS) + ADAM_WD * w), nm, nv


def _adamw_small(parts, layout, ws, ms, vs, name):
    n = len(layout)

    def body(*refs):
        p_ref = refs[0]
        w_refs, m_refs, v_refs = refs[1:1 + n], refs[1 + n:1 + 2 * n], refs[1 + 2 * n:1 + 3 * n]
        tile_ref = refs[1 + 3 * n]
        outs = refs[2 + 3 * n:]
        tile = p_ref[0]
        for dev in range(1, N_DEV):
            tile = tile + p_ref[dev]
        tile_ref[...] = tile
        for p, (rows, c0, nc) in enumerate(layout):
            for r, src in enumerate(rows):
                g = tile_ref[src:src + 1, c0:c0 + nc]
                d, nm, nv = _adam_update(g, w_refs[p][r:r + 1, :], m_refs[p][r:r + 1, :],
                                         v_refs[p][r:r + 1, :])
                for kind, val in enumerate((g, d, nm, nv)):
                    outs[4 * p + kind][r:r + 1, :] = val

    out_shape = [jax.ShapeDtypeStruct(parts.shape[1:], F32)]
    for w in ws:
        out_shape += [jax.ShapeDtypeStruct(w.shape, F32)] * 4
    res = pl.pallas_call(
        body, out_shape=out_shape, name=name,
        compiler_params=pltpu.CompilerParams(vmem_limit_bytes=VMEM_LIMIT))(parts, *ws, *ms, *vs)
    return res[0], [res[1 + 4 * p:5 + 4 * p] for p in range(n)]


ANY = pl.BlockSpec(memory_space=pl.ANY)
MESH = pl.DeviceIdType.MESH


def _slot(p):
    return 4 * p[0] + 2 * p[1] + p[2]


def _allgather(xs):
    n = len(xs)

    def body(*refs):
        x_refs, o_refs = refs[:n], refs[n:2 * n]
        send_sems, recv_sems, local_sems = refs[2 * n:]
        x, y, c = lax.axis_index("x"), lax.axis_index("y"), lax.axis_index("c")
        me, sibling = (x, y, c), (x, y, 1 - c)
        chips = [(1 - x, y), (x, 1 - y), (1 - x, 1 - y)]

        def copy(a, k, block, to, from_input=False):
            dst = o_refs[a].at[_slot(block)]
            return pltpu.make_async_remote_copy(
                src_ref=x_refs[a] if from_input else dst, dst_ref=dst,
                send_sem=send_sems.at[a, k], recv_sem=recv_sems.at[a, k],
                device_id=to, device_id_type=MESH)

        mine = [pltpu.make_async_copy(x_refs[a], o_refs[a].at[_slot(me)], local_sems.at[a])
                for a in range(n)]
        for cp in mine:
            cp.start()
        first = []
        for a in range(n):
            first.append(copy(a, 0, me, sibling, True))
            first += [copy(a, 1 + j, me, (*chip, c), True) for j, chip in enumerate(chips)]
        for cp in first:
            cp.start()
        passed = []
        for j, chip in enumerate(chips):
            for a in range(n):
                copy(a, 1 + j, (*chip, c), me).wait_recv()
                cp = copy(a, 4 + j, (*chip, c), sibling)
                cp.start()
                passed.append(cp)
        for a in range(n):
            copy(a, 0, sibling, me).wait_recv()
            for j, chip in enumerate(chips):
                copy(a, 4 + j, (*chip, 1 - c), me).wait_recv()
        for cp in first + passed:
            cp.wait_send()
        for cp in mine:
            cp.wait()

    return pl.pallas_call(
        body,
        out_shape=[jax.ShapeDtypeStruct((N_DEV,) + t.shape, t.dtype) for t in xs],
        in_specs=[ANY] * n, out_specs=[ANY] * n,
        scratch_shapes=[pltpu.SemaphoreType.DMA((n, 7)), pltpu.SemaphoreType.DMA((n, 7)),
                        pltpu.SemaphoreType.DMA((n,))],
        name="allgather_weights")(*xs)


class _Exchange:
    def __init__(self, arrays, kinds):
        self.arrays, self.kinds, self.n = list(arrays), list(kinds), len(arrays)
        self.shapes = [self._part_shape(a, k) for a, k in zip(arrays, kinds)]
        self.out_shape = [jax.ShapeDtypeStruct((N_DEV,) + shp, a.dtype)
                          for shp, a in zip(self.shapes, arrays)]
        self.scratch = [pltpu.SemaphoreType.DMA((self.n, N_DEV - 1)),
                        pltpu.SemaphoreType.DMA((self.n, N_DEV - 1)),
                        pltpu.SemaphoreType.DMA((self.n,))]

    @staticmethod
    def _part_shape(arr, kind):
        if kind == "chunks":
            return arr.shape[1:]
        if kind == "cols":
            return (arr.shape[0], arr.shape[1] // N_DEV)
        if kind == "rows":
            return (arr.shape[0] // N_DEV, arr.shape[1])
        return arr.shape

    def copies(self, in_refs, out_refs, sems):
        send_sems, recv_sems, local_sems = sems
        x, y, c = lax.axis_index("x"), lax.axis_index("y"), lax.axis_index("c")
        me = _slot((x, y, c))

        def part(a, dev):
            ref, kind, shp = in_refs[a], self.kinds[a], self.shapes[a]
            if kind == "chunks":
                return ref.at[dev]
            if kind == "cols":
                return ref.at[:, pl.ds(pl.multiple_of(dev * shp[1], LANES), shp[1])]
            if kind == "rows":
                return ref.at[pl.ds(pl.multiple_of(dev * shp[0], SUBLANES), shp[0]), :]
            return ref

        cps = [pltpu.make_async_copy(part(a, me), out_refs[a].at[me], local_sems.at[a])
               for a in range(self.n)]
        for rel in range(1, N_DEV):
            peer = (x ^ (rel >> 2), y ^ ((rel >> 1) & 1), c ^ (rel & 1))
            for a in range(self.n):
                cps.append(pltpu.make_async_remote_copy(
                    src_ref=part(a, _slot(peer)), dst_ref=out_refs[a].at[me],
                    send_sem=send_sems.at[a, rel - 1], recv_sem=recv_sems.at[a, rel - 1],
                    device_id=peer, device_id_type=MESH))
        return cps


def _exchange_grads(arrays, kinds, name):
    ex = _Exchange(arrays, kinds)
    n = ex.n

    def body(*refs):
        cps = ex.copies(refs[:n], refs[n:2 * n], refs[2 * n:])
        for cp in cps:
            cp.start()
        for cp in cps:
            cp.wait()

    return pl.pallas_call(
        body, out_shape=ex.out_shape, in_specs=[ANY] * n, out_specs=[ANY] * n,
        scratch_shapes=ex.scratch, name=name)(*arrays)


BIG = [("mla_w_in", (D_MODEL, MLA_IN), 1), ("mla_w_uq", (Q_LORA, N_TOK_HEADS * QK_DIM), 1),
       ("mla_w_ukv", (KV_LORA, N_TOK_HEADS * 2 * HEAD_DIM), 1), ("lru_w_in", (D_MODEL, LRU_IN), 1),
       ("w_mem_kv", (2, D_MODEL, 2 * MEM_WIDTH), 1), ("w_out", (2, MIX_WIDTH, D_MODEL), 1)]
SMALL = [("lru_conv_w", (CONV_W, TOK_WIDTH), 1), ("lru_conv_b", (TOK_WIDTH,), 0),
         ("lru_b_rgate", (TOK_WIDTH,), 0), ("lru_b_igate", (TOK_WIDTH,), 0),
         ("lru_lambda", (TOK_WIDTH,), 0)]
REPL = [("mla_q_norm", (Q_LORA,)), ("mla_kv_norm", (KV_LORA,)),
        ("lru_w_rgate", (N_TOK_HEADS, HEAD_DIM, HEAD_DIM)),
        ("lru_w_igate", (N_TOK_HEADS, HEAD_DIM, HEAD_DIM)),
        ("ln_g", (2, D_MODEL)), ("ln_b", (2, D_MODEL))]


def _shard_shape(shape, axis):
    return tuple(d // N_DEV if a == axis else d for a, d in enumerate(shape))


def _size(shape):
    return math.prod(shape)


N_OWN_SHAPE = 3
SMALL_ROWS = SUBLANES


def _pack_rows(flat_parts, rows):
    flat = jnp.concatenate([p.reshape(-1) for p in flat_parts])
    return jnp.pad(flat, (0, rows * LANES - flat.shape[0])).reshape(rows, LANES)


def _to_chunks(full, axis):
    shape = full.shape
    split = shape[:axis] + (N_DEV, shape[axis] // N_DEV) + shape[axis + 1:]
    return jnp.moveaxis(full.reshape(split), axis, 0).reshape(N_DEV, -1)


def _from_chunks(chunks, shape, axis):
    sh = _shard_shape(shape, axis)
    t = chunks.reshape((N_DEV,) + sh)
    t = jnp.moveaxis(t, 0, axis)
    return t.reshape(shape)


def _split_flat(flat2d, table):
    out, off = [], 0
    for size in table:
        out.append(flat2d[:, off:off + size])
        off += size
    return out


def _win0_to_padded(w):
    z = lambda n: jnp.zeros((w.shape[0], n), w.dtype)
    return jnp.concatenate([w[:, 0:640], z(KR_LANE), w[:, 640:672],
                            z(LANES - KR_LANE - QK_ROPE), w[:, 672:1952]], axis=1)


def _win0_from_padded(wp):
    k0 = ZA_KR + KR_LANE
    return jnp.concatenate([wp[:, 0:640], wp[:, k0:k0 + QK_ROPE], wp[:, ZA_W:ZP]], axis=1)


def _pad_heads(w, per_head, lo, hi):
    t = w.reshape(w.shape[0], N_TOK_HEADS, per_head)[:, :, lo:hi]
    t = jnp.pad(t, ((0, 0), (0, 0), (0, HEAD_PAD - (hi - lo))))
    return t.reshape(w.shape[0], QKV_PAD)


def _unpad_heads(wp, width):
    return wp.reshape(wp.shape[0], N_TOK_HEADS, HEAD_PAD)[:, :, :width]


def _block_diag(w):
    eye = jnp.eye(N_TOK_HEADS, dtype=w.dtype)
    return (w[:, :, None, :] * eye[:, None, :, None]).reshape(TOK_WIDTH, TOK_WIDTH)


def _diag_blocks(d):
    t = d.reshape(N_TOK_HEADS, HEAD_DIM, N_TOK_HEADS, HEAD_DIM)
    return jnp.stack([t[g, :, g, :] for g in range(N_TOK_HEADS)])


def _rope_tables(positions):
    half = QK_ROPE // 2
    inv_freq = ROPE_THETA ** (-jnp.arange(half, dtype=F32) / half)
    ang = positions.astype(F32)[:, None] * inv_freq
    cos, sin = jnp.cos(ang), jnp.sin(ang)
    s = positions.shape[0]
    tail = jnp.zeros((s, HEAD_PAD - QK_DIM), F32)
    c = jnp.concatenate([jnp.ones((s, QK_NOPE), F32), cos, cos, tail], axis=1)
    sn = jnp.concatenate([jnp.zeros((s, QK_NOPE), F32), sin, sin, tail], axis=1)
    return c, sn


def _local_step(x, mem, positions, tgt, wts, ts, tatt, early_exchange):
    bf = lambda t: t.astype(BF16)
    win0 = _win0_to_padded(wts["mla_w_in"])
    wuq = _pad_heads(wts["mla_w_uq"], QK_DIM, 0, QK_DIM)
    wukv = jnp.concatenate([_pad_heads(wts["mla_w_ukv"], 2 * HEAD_DIM, 0, QK_NOPE),
                            _pad_heads(wts["mla_w_ukv"], 2 * HEAD_DIM, QK_NOPE, 2 * HEAD_DIM)],
                           axis=1)
    win1 = wts["lru_w_in"]
    wmkv, wout = wts["w_mem_kv"], wts["w_out"]
    gq = wts["mla_q_norm"].reshape(1, Q_LORA)
    gkv = wts["mla_kv_norm"].reshape(1, KV_LORA)
    ln_g, ln_b = wts["ln_g"], wts["ln_b"]
    wr, wi = bf(_block_diag(wts["lru_w_rgate"])), bf(_block_diag(wts["lru_w_igate"]))
    cw8 = jnp.pad(wts["lru_conv_w"], ((0, SUBLANES - CONV_W), (0, 0)))
    vec8 = jnp.pad(jnp.stack([wts["lru_conv_b"], wts["lru_b_rgate"], wts["lru_b_igate"],
                              wts["lru_lambda"]]), ((0, SUBLANES - 4), (0, 0)))
    tabs = _rope_tables(positions)
    tmem = mem.shape[0]

    za0, zg0, q, k, v = _mla_prep_fwd(x, win0, tabs, gq, gkv, wuq, wukv, ts)
    o, lse = _flash_fwd(q, k, v, tatt, FWD_HEADS)
    mkv0, = _rowmm(mem, wmkv[0], [2 * MEM_WIDTH], "mem_kv0", tmem)
    cat0, y0, pre0, h1 = _mix_out_fwd(o, zg0, mkv0, wout[0], x, ln_g[0:1], ln_b[0:1], None,
                                      0, MIX_WIDTH, True, "mix_out_fwd0", ts)
    del o
    u1, zg1, hs = _lru_fwd(h1, win1, cw8, vec8, wr, wi, ts)
    mkv1, = _rowmm(mem, wmkv[1], [2 * MEM_WIDTH], "mem_kv1", tmem)
    cat1, y1, dpre1, dgb1, loss8 = _mix_out_fwd(hs, zg1, mkv1, wout[1], h1, ln_g[1:2],
                                                ln_b[1:2], tgt, 0, MIX_WIDTH, False,
                                                "mix_out_loss", ts)
    loss = loss8[0, 0]

    dzg1, dhs, dmkv1, dwout1 = _gate_mem_bwd(dpre1, y1, wout[1].T, cat1, zg1, mkv1, None,
                                             0, MIX_WIDTH, "gate_mem_bwd1", ts)
    du, dwr, dwi, dvec = _lru_bwd(u1, dhs, hs, cw8, vec8, wr, wi, wr.T, wi.T, ts)
    dpre0, dwin1, dgb0 = _linear_bwd(h1, [du, dzg1], [0, ZA_W], win1.T, dpre1,
                                     (pre0, ln_g[0:1]), "in_proj_bwd1", ts)
    dwmkv1 = _wgrad_small(mem, dmkv1, "mem_kv_bwd1")
    dzg0, do, dmkv0, dwout0, stats = _gate_mem_bwd(dpre0, y0, wout[0].T, cat0, zg0, mkv0, lse,
                                                   0, MIX_WIDTH, "gate_mem_bwd0", ts)
    dwmkv0 = _wgrad_small(mem, dmkv0, "mem_kv_bwd0")
    early = {
        "lru_w_in": dwin1,
        "lru_small": dvec,
        "lru_w_rgate": _diag_blocks(dwr).reshape(TOK_WIDTH, HEAD_DIM),
        "lru_w_igate": _diag_blocks(dwi).reshape(TOK_WIDTH, HEAD_DIM),
        "w_mem_kv": [dwmkv0, dwmkv1],
        "w_out": [dwout0, dwout1],
    }
    (dq, dk, dv), got_early = _flash_bwd(q, k, v, stats, do, tatt, BWD_HEADS,
                                         early_exchange(early))
    dza, dzk, dwuq_p, dwukv_p, dg = _mla_prep_bwd(za0, dq, dk, dv, tabs, gq, gkv,
                                                  wuq.T, wukv.T, ts)
    gx, dwin0_p = _linear_bwd(x, [dza, dzk, dzg0], [ZA_CQ, ZA_KR, ZA_W], win0.T, dpre0,
                              None, "in_proj_bwd0", ts)

    dwukv = jnp.concatenate([_unpad_heads(dwukv_p[:, :QKV_PAD], HEAD_DIM),
                             _unpad_heads(dwukv_p[:, QKV_PAD:], HEAD_DIM)], axis=2)
    zrow = jnp.zeros((1, D_MODEL), F32)
    gains = jnp.pad(dg[0:1], ((0, 0), (0, D_MODEL - Q_LORA - KV_LORA)))
    small_repl = jnp.concatenate([dgb0[0:2], dgb1[0:2], gains,
                                  loss * jnp.ones((1, D_MODEL), F32), zrow, zrow], axis=0)
    late = {
        "mla_w_in": _win0_from_padded(dwin0_p),
        "mla_w_uq": _unpad_heads(dwuq_p, QK_DIM).reshape(Q_LORA, N_TOK_HEADS * QK_DIM),
        "mla_w_ukv": dwukv.reshape(KV_LORA, N_TOK_HEADS * 2 * HEAD_DIM),
        "small_repl": small_repl,
    }
    return gx, early, got_early, late


WEIGHT_ORDER = ["mla_w_in", "mla_q_norm", "mla_w_uq", "mla_kv_norm", "mla_w_ukv", "lru_w_in",
                "lru_conv_w", "lru_conv_b", "lru_w_rgate", "lru_b_rgate", "lru_w_igate",
                "lru_b_igate", "lru_lambda", "w_mem_kv", "w_out", "ln_g", "ln_b"]


def kernel(x, mem, positions, mla_w_in, mla_q_norm, mla_w_uq, mla_kv_norm, mla_w_ukv, lru_w_in, lru_conv_w, lru_conv_b, lru_w_rgate, lru_b_rgate, lru_w_igate, lru_b_igate, lru_lambda, w_mem_kv, w_out, ln_g, ln_b, loss_target, m_mla_w_in, m_mla_q_norm, m_mla_w_uq, m_mla_kv_norm, m_mla_w_ukv, m_lru_w_in, m_lru_conv_w, m_lru_conv_b, m_lru_w_rgate, m_lru_b_rgate, m_lru_w_igate, m_lru_b_igate, m_lru_lambda, m_w_mem_kv, m_w_out, m_ln_g, m_ln_b, v_mla_w_in, v_mla_q_norm, v_mla_w_uq, v_mla_kv_norm, v_mla_w_ukv, v_lru_w_in, v_lru_conv_w, v_lru_conv_b, v_lru_w_rgate, v_lru_b_rgate, v_lru_w_igate, v_lru_b_igate, v_lru_lambda, v_w_mem_kv, v_w_out, v_ln_g, v_ln_b):
    w_in = dict(mla_w_in=mla_w_in, mla_q_norm=mla_q_norm, mla_w_uq=mla_w_uq,
                mla_kv_norm=mla_kv_norm, mla_w_ukv=mla_w_ukv, lru_w_in=lru_w_in,
                lru_conv_w=lru_conv_w, lru_conv_b=lru_conv_b, lru_w_rgate=lru_w_rgate,
                lru_b_rgate=lru_b_rgate, lru_w_igate=lru_w_igate, lru_b_igate=lru_b_igate,
                lru_lambda=lru_lambda, w_mem_kv=w_mem_kv, w_out=w_out, ln_g=ln_g, ln_b=ln_b)
    m_in = dict(mla_w_in=m_mla_w_in, mla_q_norm=m_mla_q_norm, mla_w_uq=m_mla_w_uq,
                mla_kv_norm=m_mla_kv_norm, mla_w_ukv=m_mla_w_ukv, lru_w_in=m_lru_w_in,
                lru_conv_w=m_lru_conv_w, lru_conv_b=m_lru_conv_b, lru_w_rgate=m_lru_w_rgate,
                lru_b_rgate=m_lru_b_rgate, lru_w_igate=m_lru_w_igate, lru_b_igate=m_lru_b_igate,
                lru_lambda=m_lru_lambda, w_mem_kv=m_w_mem_kv, w_out=m_w_out, ln_g=m_ln_g,
                ln_b=m_ln_b)
    v_in = dict(mla_w_in=v_mla_w_in, mla_q_norm=v_mla_q_norm, mla_w_uq=v_mla_w_uq,
                mla_kv_norm=v_mla_kv_norm, mla_w_ukv=v_mla_w_ukv, lru_w_in=v_lru_w_in,
                lru_conv_w=v_lru_conv_w, lru_conv_b=v_lru_conv_b, lru_w_rgate=v_lru_w_rgate,
                lru_b_rgate=v_lru_b_rgate, lru_w_igate=v_lru_w_igate, lru_b_igate=v_lru_b_igate,
                lru_lambda=v_lru_lambda, w_mem_kv=v_w_mem_kv, w_out=v_w_out, ln_g=v_ln_g,
                ln_b=v_ln_b)
    s = x.shape[1]
    ts = min(ROW_BLOCK, s)
    tatt = min(ATT_BLOCK, s)
    big_sizes = [_size(sh) // N_DEV for _, sh, _ in BIG]
    small_sizes = [_size(sh) // N_DEV for _, sh, _ in SMALL]

    own = BIG[:N_OWN_SHAPE]
    packed = BIG[N_OWN_SHAPE:]
    own_local = [w_in[n][0].astype(BF16) for n, _, _ in own]
    packed_rows = sum(big_sizes[N_OWN_SHAPE:]) // LANES
    big_local = _pack_rows([w_in[n] for n, _, _ in packed], packed_rows).astype(BF16)
    small_local = _pack_rows([w_in[n] for n, _, _ in SMALL], SMALL_ROWS)
    *own_all, big_all, small_all = _allgather(own_local + [big_local, small_local])
    wts = {}
    for (n, sh, ax), g in zip(own, own_all):
        wts[n] = jnp.moveaxis(g, 0, 1).reshape(sh)
    for (n, sh, ax), part in zip(packed, _split_flat(big_all.reshape(N_DEV, -1),
                                                     big_sizes[N_OWN_SHAPE:])):
        wts[n] = _from_chunks(part, sh, ax)
    for (n, sh, ax), part in zip(SMALL, _split_flat(small_all.reshape(N_DEV, -1), small_sizes)):
        wts[n] = _from_chunks(part, sh, ax)
    for n, sh in REPL:
        wts[n] = w_in[n].reshape(sh)

    def early_exchange(g):
        small_chunks = jnp.moveaxis(g["lru_small"].reshape(SUBLANES, N_DEV, -1), 1, 0)
        sends = [(g["lru_w_in"], "cols"),
                 (g["w_mem_kv"][0], "rows"), (g["w_mem_kv"][1], "rows"),
                 (g["w_out"][0], "rows"), (g["w_out"][1], "rows"),
                 (small_chunks, "chunks"), (g["lru_w_rgate"], "all"), (g["lru_w_igate"], "all")]
        return _Exchange([a for a, _ in sends], [k for _, k in sends])

    gx, _, got_early, late = _local_step(x[0], mem[0], positions[0], loss_target[0], wts,
                                         ts, tatt, early_exchange)

    def chunked(name, shape):
        w = shape[1] // N_DEV
        return _to_chunks(late[name], 1).reshape(N_DEV, shape[0], w).astype(BF16)

    got_late = _exchange_grads(
        [chunked("mla_w_in", (D_MODEL, MLA_IN)),
         chunked("mla_w_uq", (Q_LORA, N_TOK_HEADS * QK_DIM)),
         chunked("mla_w_ukv", (KV_LORA, N_TOK_HEADS * 2 * HEAD_DIM)), late["small_repl"]],
        ["chunks", "chunks", "chunks", "all"], "exchange_grads")
    got = list(got_late[:3]) + list(got_early) + [got_late[3]]

    def flat2(d, name):
        t = d[name]
        return t.reshape(-1, t.shape[-1])

    def update(parts, view, name):
        return _adamw(parts, view(w_in), view(m_in), view(v_in), "adamw_" + name)

    res = {}
    for idx, name in [(0, "mla_w_in"), (1, "mla_w_uq"), (2, "mla_w_ukv"), (3, "lru_w_in"),
                      (9, "lru_w_rgate"), (10, "lru_w_igate")]:
        res[name] = update([got[idx]], functools.partial(flat2, name=name), name)
    res["w_mem_kv"] = update([got[4], got[5]], functools.partial(flat2, name="w_mem_kv"),
                             "w_mem_kv")
    res["w_out"] = update([got[6], got[7]], functools.partial(flat2, name="w_out"), "w_out")
    def small(parts, names, layout, name):
        view = lambda d: [flat2(d, n) for n in names]
        tile, outs = _adamw_small(parts, layout, view(w_in), view(m_in), view(v_in), name)
        for n, o in zip(names, outs):
            res[n] = o
        return tile

    taps = list(range(CONV_W))
    small(got[8], ["lru_conv_w", "lru_conv_b", "lru_b_rgate", "lru_b_igate", "lru_lambda"],
          [(taps, 0, TOK_WIDTH // N_DEV)] + [([4 + a], 0, TOK_WIDTH // N_DEV) for a in range(4)],
          "adamw_small_sharded")
    tile = small(got[11], ["ln_g", "ln_b", "mla_q_norm", "mla_kv_norm"],
                 [([0, 2], 0, D_MODEL), ([1, 3], 0, D_MODEL), ([4], 0, Q_LORA),
                  ([4], Q_LORA, KV_LORA)], "adamw_small_replicated")
    loss = tile[5, 0]

    result = [loss, gx.reshape(x.shape)]
    for kind in range(4):
        result += [res[n][kind].reshape(w_in[n].shape) for n in WEIGHT_ORDER]
    return tuple(result)
```

```python
import functools
import math

import jax
import jax.numpy as jnp
from jax import lax
from jax.experimental import pallas as pl
from jax.experimental.pallas import tpu as pltpu

F32 = jnp.float32
BF16 = jnp.bfloat16

D_MODEL = 1024
MEM_LEN = 256
HEAD_DIM = 64
N_TOK_HEADS = 12
N_MEM_HEADS = 4
TOK_WIDTH = 768
MEM_WIDTH = 256
MIX_WIDTH = 1024
Q_LORA = 384
KV_LORA = 256
QK_NOPE = 64
QK_ROPE = 32
QK_DIM = 96
ROPE_THETA = 10000.0
CONV_W = 4
LRU_C = 8.0
ALPHA = (2.0 * 2) ** 0.25
NORM_EPS = 1e-6
MLA_IN = 1952
LRU_IN = 2048
ADAM_LR = 0.001
ADAM_B1 = 0.9
ADAM_B2 = 0.999
ADAM_EPS = 1e-08
ADAM_WD = 0.01
ADAM_STEP = 10

N_DEV = 8
LANES = 128
SUBLANES = 8
HEAD_PAD = 128
QKV_PAD = N_TOK_HEADS * HEAD_PAD
ZP = 2048
ZA_W = TOK_WIDTH
ZG_W = MIX_WIDTH + MEM_WIDTH
ZA_CQ, ZA_CKV, ZA_KR = 0, 384, 640
KR_LANE = 64

ROW_BLOCK = 512
ATT_BLOCK = 512
LOOKAHEAD = 3
FWD_HEADS = 12
BWD_HEADS = 4
VMEM_LIMIT = 56 * 1024 * 1024
NEG_BIG = -1e30
STRIP = 32
SCAN_SEGMENTS = 4
LOG2E = math.log2(math.e)


def _cp(n_axes):
    return pltpu.CompilerParams(dimension_semantics=("arbitrary",) * n_axes,
                                vmem_limit_bytes=VMEM_LIMIT)


def _dot(a, b):
    return jnp.dot(a, b, preferred_element_type=F32)


def _dot_nt(a, b):
    return lax.dot_general(a, b, (((1,), (1,)), ((), ())), preferred_element_type=F32)


def _dot_tn(a, b):
    return lax.dot_general(a, b, (((0,), (0,)), ((), ())), preferred_element_type=F32)


def _sigmoid(t):
    return 1.0 / (1.0 + jnp.exp(-t))


def _lane(shape):
    return lax.broadcasted_iota(jnp.int32, shape, len(shape) - 1)


def _full(shape):
    nd = len(shape)
    return pl.BlockSpec(shape, lambda *_: (0,) * nd)


def _rows(ts, width, col=0):
    return pl.BlockSpec((ts, width), lambda i: (i, col))


def _heads(ts):
    return pl.BlockSpec((N_TOK_HEADS, ts, HEAD_PAD), lambda i: (0, i, 0))


def _rowmm(x, w, widths, name, ts):
    s, k = x.shape
    n = w.shape[1]
    offs = [sum(widths[:a]) for a in range(len(widths))]

    def body(x_ref, w_ref, *o_refs):
        res = _dot(x_ref[...].astype(BF16), w_ref[...])
        for o_ref, off, wd in zip(o_refs, offs, widths):
            o_ref[...] = res[:, off:off + wd]

    return pl.pallas_call(
        body, grid=(s // ts,),
        in_specs=[_rows(ts, k), _full((k, n))],
        out_specs=[_rows(ts, wd) for wd in widths],
        out_shape=[jax.ShapeDtypeStruct((s, wd), F32) for wd in widths],
        name=name, compiler_params=_cp(1))(x, w)


def _rms_parts(t):
    rs = lax.rsqrt(jnp.mean(t * t, axis=-1, keepdims=True) + NORM_EPS)
    return rs


def _rope_terms(c, sn):
    first_half = _lane(sn.shape) < KR_LANE + QK_ROPE // 2
    return c, jnp.where(first_half, -sn, 0.0), jnp.where(first_half, 0.0, sn)


def _rope(t, c, sa, sb):
    return t * c + pltpu.roll(t, LANES - 16, 1) * sa + pltpu.roll(t, 16, 1) * sb


def _rope_t(d, c, sa, sb):
    return d * c + pltpu.roll(d * sa, 16, 1) + pltpu.roll(d * sb, LANES - 16, 1)


def _mla_prep_fwd(x, win, tabs, gq, gkv, wuq, wukv, ts):
    s = x.shape[0]

    def body(x_ref, win_ref, c_ref, sn_ref, gq_ref, gkv_ref, wuq_ref, wukv_ref,
             z_ref, zg_ref, q_ref, k_ref, v_ref):
        zfull = _dot(x_ref[...].astype(BF16), win_ref[...])
        z_ref[...] = zfull[:, 0:ZA_W]
        zg_ref[...] = zfull[:, ZA_W:ZP]
        cq = zfull[:, ZA_CQ:ZA_CQ + Q_LORA]
        ckv = zfull[:, ZA_CKV:ZA_CKV + KV_LORA]
        kr = zfull[:, ZA_KR:ZA_KR + LANES]
        cqn = cq * _rms_parts(cq) * gq_ref[...]
        ckvn = ckv * _rms_parts(ckv) * gkv_ref[...]
        q = _dot(cqn.astype(BF16), wuq_ref[...])
        kv = _dot(ckvn.astype(BF16), wukv_ref[...])
        c, sa, sb = _rope_terms(c_ref[...], sn_ref[...])
        krope = _rope(kr, c, sa, sb)
        pad_lane = _lane((ts, HEAD_PAD)) >= HEAD_DIM
        for h in range(N_TOK_HEADS):
            sl = slice(h * HEAD_PAD, (h + 1) * HEAD_PAD)
            q_ref[h] = _rope(q[:, sl], c, sa, sb).astype(BF16)
            k_ref[h] = (kv[:, sl] + krope).astype(BF16)
            vh = kv[:, QKV_PAD + h * HEAD_PAD:QKV_PAD + (h + 1) * HEAD_PAD]
            v_ref[h] = jnp.where(pad_lane, 1.0, vh).astype(BF16)

    out = jax.ShapeDtypeStruct((N_TOK_HEADS, s, HEAD_PAD), BF16)
    return pl.pallas_call(
        body, grid=(s // ts,),
        in_specs=[_rows(ts, D_MODEL), _full((D_MODEL, ZP)),
                  _rows(ts, LANES), _rows(ts, LANES),
                  _full((1, Q_LORA)), _full((1, KV_LORA)),
                  _full((Q_LORA, QKV_PAD)), _full((KV_LORA, 2 * QKV_PAD))],
        out_specs=[_rows(ts, ZA_W), _rows(ts, ZG_W)] + [_heads(ts)] * 3,
        out_shape=[jax.ShapeDtypeStruct((s, ZA_W), F32), jax.ShapeDtypeStruct((s, ZG_W), F32),
                   out, out, out],
        name="mla_prep_fwd", compiler_params=_cp(1))(x, win, *tabs, gq, gkv, wuq, wukv)


def _mla_prep_bwd(z0, dq, dk, dv, tabs, gq, gkv, wuq_t, wukv_t, ts):
    s = z0.shape[0]

    def body(z_ref, dq_ref, dk_ref, dv_ref, c_ref, sn_ref, gq_ref, gkv_ref,
             wuqt_ref, wukvt_ref, dza_ref, dzk_ref, dwuq_ref, dwukv_ref, dg_ref):
        @pl.when(pl.program_id(0) == 0)
        def _():
            dwuq_ref[...] = jnp.zeros_like(dwuq_ref)
            dwukv_ref[...] = jnp.zeros_like(dwukv_ref)
            dg_ref[...] = jnp.zeros_like(dg_ref)

        cq = z_ref[:, ZA_CQ:ZA_CQ + Q_LORA]
        ckv = z_ref[:, ZA_CKV:ZA_CKV + KV_LORA]
        rq, rkv = _rms_parts(cq), _rms_parts(ckv)
        gq_, gkv_ = gq_ref[...], gkv_ref[...]
        cqn = (cq * rq * gq_).astype(BF16)
        ckvn = (ckv * rkv * gkv_).astype(BF16)
        c, sa, sb = _rope_terms(c_ref[...], sn_ref[...])
        dqp, dksum = [], None
        for h in range(N_TOK_HEADS):
            dqp.append(_rope_t(dq_ref[h], c, sa, sb))
            dksum = dk_ref[h] if dksum is None else dksum + dk_ref[h]
        dqp = jnp.concatenate(dqp, axis=1).astype(BF16)
        lane = _lane(dksum.shape)
        dzk_ref[...] = jnp.where((lane >= KR_LANE) & (lane < KR_LANE + QK_ROPE),
                                 _rope_t(dksum, c, sa, sb), 0.0).astype(BF16)
        dkv = jnp.concatenate([dk_ref[h].astype(BF16) for h in range(N_TOK_HEADS)]
                              + [dv_ref[h] for h in range(N_TOK_HEADS)], axis=1)
        dcqn = _dot(dqp, wuqt_ref[...])
        dckvn = _dot(dkv, wukvt_ref[...])
        dwuq_ref[...] += _dot_tn(cqn, dqp)
        dwukv_ref[...] += _dot_tn(ckvn, dkv)
        dg_ref[0:1, 0:Q_LORA] += jnp.sum(dcqn * cq * rq, axis=0, keepdims=True)
        dg_ref[0:1, Q_LORA:Q_LORA + KV_LORA] += jnp.sum(dckvn * ckv * rkv, axis=0, keepdims=True)
        wq = dcqn * gq_
        wkv = dckvn * gkv_
        dcq = rq * wq - cq * (rq * rq * rq) * jnp.mean(wq * cq, axis=-1, keepdims=True)
        dckv = rkv * wkv - ckv * (rkv * rkv * rkv) * jnp.mean(wkv * ckv, axis=-1, keepdims=True)
        dza_ref[:, 0:Q_LORA] = dcq.astype(BF16)
        dza_ref[:, Q_LORA:Q_LORA + KV_LORA] = dckv.astype(BF16)

    na = Q_LORA + KV_LORA
    return pl.pallas_call(
        body, grid=(s // ts,),
        in_specs=[_rows(ts, ZA_W), _heads(ts), _heads(ts), _heads(ts),
                  _rows(ts, LANES), _rows(ts, LANES),
                  _full((1, Q_LORA)), _full((1, KV_LORA)),
                  _full((QKV_PAD, Q_LORA)), _full((2 * QKV_PAD, KV_LORA))],
        out_specs=[_rows(ts, na), _rows(ts, LANES), _full((Q_LORA, QKV_PAD)),
                   _full((KV_LORA, 2 * QKV_PAD)), _full((SUBLANES, na))],
        out_shape=[jax.ShapeDtypeStruct((s, na), BF16), jax.ShapeDtypeStruct((s, LANES), BF16),
                   jax.ShapeDtypeStruct((Q_LORA, QKV_PAD), F32),
                   jax.ShapeDtypeStruct((KV_LORA, 2 * QKV_PAD), F32),
                   jax.ShapeDtypeStruct((SUBLANES, na), F32)],
        name="mla_prep_bwd", compiler_params=_cp(1))(
            z0, dq, dk, dv, *tabs, gq, gkv, wuq_t, wukv_t)


def _causal_pairs(nb, by_key):
    if by_key:
        pairs = [(i, j) for j in range(nb) for i in range(j, nb)]
    else:
        pairs = [(i, j) for i in range(nb) for j in range(i + 1)]
    return (jnp.array([p[0] for p in pairs], jnp.int32),
            jnp.array([p[1] for p in pairs], jnp.int32))


def _flash_fwd(q, k, v, t, nh):
    s = q.shape[1]
    itab, jtab = _causal_pairs(s // t, False)
    c2 = LOG2E / math.sqrt(QK_DIM)

    def body(it_ref, jt_ref, q_ref, k_ref, v_ref, o_ref, lse_ref, m_scr, acc_scr):
        pair = pl.program_id(1)
        i, j = it_ref[pair], jt_ref[pair]

        @pl.when(j == 0)
        def _():
            m_scr[...] = jnp.full_like(m_scr, NEG_BIG)
            acc_scr[...] = jnp.zeros_like(acc_scr)

        def softmax_strips(masked, hs, sc, row0):
            ps, als = [], []
            for r0 in range(0, sc.shape[0], STRIP):
                rows = slice(row0 + r0, row0 + r0 + STRIP)
                ch = [sc[r0:r0 + STRIP, n * LANES:(n + 1) * LANES] * c2
                      for n in range(sc.shape[1] // LANES)]
                if masked:
                    rr = row0 + r0 + lax.broadcasted_iota(jnp.int32, (STRIP, LANES), 0)
                    cc = lax.broadcasted_iota(jnp.int32, (STRIP, LANES), 1)
                    ch = [jnp.where(cc + n * LANES <= rr, c_, NEG_BIG) for n, c_ in enumerate(ch)]
                mx = ch[0]
                for c_ in ch[1:]:
                    mx = jnp.maximum(mx, c_)
                m_prev = m_scr[hs, rows, :]
                m_next = jnp.maximum(m_prev, jnp.max(mx, axis=-1, keepdims=True))
                ps.append(jnp.concatenate(
                    [jnp.exp2(c_ - m_next).astype(BF16) for c_ in ch], axis=1))
                als.append(jnp.exp2(m_prev - m_next))
                m_scr[hs, rows, :] = m_next
            return jnp.concatenate(ps, axis=0), jnp.concatenate(als, axis=0)

        def run(masked, parts):
            def scores_of(hs):
                return [_dot_nt(q_ref[hs, r0:r0 + nr, :], k_ref[hs, 0:nk, :])
                        for r0, nr, nk in parts]

            ahead = min(LOOKAHEAD, nh)
            scores = [scores_of(hs) for hs in range(ahead)]
            for hs in range(nh):
                if hs + ahead < nh:
                    scores.append(scores_of(hs + ahead))
                for (r0, nr, nk), sc in zip(parts, scores[hs]):
                    p, alpha = softmax_strips(masked, hs, sc, r0)
                    acc_scr[hs, r0:r0 + nr, :] = (alpha * acc_scr[hs, r0:r0 + nr, :]
                                                  + _dot(p, v_ref[hs, 0:nk, :]))

        @pl.when(j < i)
        def _():
            run(False, [(0, t, t)])

        @pl.when(j == i)
        def _():
            run(True, [(0, t, t)])
            for h in range(nh):
                acc = acc_scr[h]
                l = acc[:, HEAD_DIM:HEAD_DIM + 1]
                o_ref[h] = jnp.where(_lane(acc.shape) < HEAD_DIM, acc / l, 0.0)
                lse_ref[h] = m_scr[h] + jnp.log2(l)

    qspec = pl.BlockSpec((nh, t, HEAD_PAD), lambda h, p, it, jt: (h, it[p], 0))
    kspec = pl.BlockSpec((nh, t, HEAD_PAD), lambda h, p, it, jt: (h, jt[p], 0))
    out = jax.ShapeDtypeStruct((N_TOK_HEADS, s, HEAD_PAD), F32)
    return pl.pallas_call(
        body,
        grid_spec=pltpu.PrefetchScalarGridSpec(
            num_scalar_prefetch=2, grid=(N_TOK_HEADS // nh, itab.shape[0]),
            in_specs=[qspec, kspec, kspec], out_specs=[qspec, qspec],
            scratch_shapes=[pltpu.VMEM((nh, t, HEAD_PAD), F32)] * 2),
        out_shape=[out, out],
        name="flash_fwd", compiler_params=_cp(2))(itab, jtab, q, k, v)


def _flash_bwd(q, k, v, stats, do, t, nh, ex):
    s = q.shape[1]
    nb = s // t
    itab, jtab = _causal_pairs(nb, True)
    npairs = itab.shape[0]
    ngroups = N_TOK_HEADS // nh
    scale = 1.0 / math.sqrt(QK_DIM)
    c2 = LOG2E * scale
    nx = ex.n if ex is not None else 0
    ex_arrays, ex_out_shape, ex_scratch = (
        (ex.arrays, ex.out_shape, ex.scratch) if ex is not None else ([], [], []))

    def body(it_ref, jt_ref, q_ref, k_ref, v_ref, st_ref, do_ref, *rest):
        ex_in, rest = rest[:nx], rest[nx:]
        dq_ref, dk_ref, dv_ref = rest[:3]
        ex_out, rest = rest[3:3 + nx], rest[3 + nx:]
        dk_scr, dv_scr = rest[:2]
        ex_sems = rest[2:]
        pair = pl.program_id(1)
        i, j = it_ref[pair], jt_ref[pair]
        rows_i = pl.ds(pl.multiple_of(i * t, t), t)

        if nx:
            @pl.when(jnp.logical_and(pl.program_id(0) == 0, pair == 0))
            def _():
                for cp in ex.copies(ex_in, ex_out, ex_sems):
                    cp.start()

        @pl.when(i == j)
        def _():
            dk_scr[...] = jnp.zeros_like(dk_scr)
            dv_scr[...] = jnp.zeros_like(dv_scr)

        @pl.when(j == 0)
        def _():
            dq_ref[:, rows_i, :] = jnp.zeros((nh, t, HEAD_PAD), F32)

        def prob_strips(masked, h, sct, dpt, k0, q0):
            ps, dss = [], []
            for r0 in range(0, sct.shape[0], STRIP):
                rows = slice(r0, r0 + STRIP)
                if masked:
                    kk = k0 + r0 + lax.broadcasted_iota(jnp.int32, (STRIP, LANES), 0)
                    qq = q0 + lax.broadcasted_iota(jnp.int32, (STRIP, LANES), 1)
                pcs, dcs = [], []
                for n in range(sct.shape[1] // LANES):
                    cols = slice(n * LANES, (n + 1) * LANES)
                    qcols = slice(q0 + n * LANES, q0 + (n + 1) * LANES)
                    x = sct[rows, cols] * c2
                    if masked:
                        x = jnp.where(kk <= qq + n * LANES, x, NEG_BIG)
                    p = jnp.exp2(x - st_ref[h, 0:1, qcols])
                    pcs.append(p.astype(BF16))
                    dcs.append((p * (dpt[rows, cols] - st_ref[h, 1:2, qcols]) * scale).astype(BF16))
                ps.append(jnp.concatenate(pcs, axis=1))
                dss.append(jnp.concatenate(dcs, axis=1))
            return jnp.concatenate(ps, axis=0), jnp.concatenate(dss, axis=0)

        def run(masked, parts):
            def scores_of(h):
                return [(_dot_nt(k_ref[h, k0:k0 + nk, :], q_ref[h, q0:q0 + nq, :]),
                         _dot_nt(v_ref[h, k0:k0 + nk, :], do_ref[h, q0:q0 + nq, :]))
                        for k0, nk, q0, nq in parts]

            ahead = min(LOOKAHEAD, nh)
            scores = [scores_of(h) for h in range(ahead)]
            for h in range(nh):
                if h + ahead < nh:
                    scores.append(scores_of(h + ahead))
                for (k0, nk, q0, nq), (sct, dpt) in zip(parts, scores[h]):
                    pt, dst = prob_strips(masked, h, sct, dpt, k0, q0)
                    dv_scr[h, k0:k0 + nk, :] += _dot(pt, do_ref[h, q0:q0 + nq, :])
                    dk_scr[h, k0:k0 + nk, :] += _dot(dst, q_ref[h, q0:q0 + nq, :])
                    rows = pl.ds(pl.multiple_of(i * t + q0, t // 2), nq)
                    dq_ref[h, rows, :] += _dot_tn(dst, k_ref[h, k0:k0 + nk, :])

        @pl.when(i > j)
        def _():
            run(False, [(0, t, 0, t)])

        @pl.when(i == j)
        def _():
            run(True, [(0, t // 2, 0, t), (t // 2, t // 2, t // 2, t // 2)])

        @pl.when(i == nb - 1)
        def _():
            dk_ref[...] = dk_scr[...]
            dv_ref[...] = dv_scr[...].astype(BF16)

        if nx:
            @pl.when(jnp.logical_and(pl.program_id(0) == ngroups - 1, pair == npairs - 1))
            def _():
                for cp in ex.copies(ex_in, ex_out, ex_sems):
                    cp.wait()

    qspec = pl.BlockSpec((nh, t, HEAD_PAD), lambda h, p, it, jt: (h, it[p], 0))
    kspec = pl.BlockSpec((nh, t, HEAD_PAD), lambda h, p, it, jt: (h, jt[p], 0))
    dqspec = pl.BlockSpec((nh, s, HEAD_PAD), lambda h, p, it, jt: (h, 0, 0))
    stspec = pl.BlockSpec((nh, 2, t), lambda h, p, it, jt: (h, 0, it[p]))
    out = jax.ShapeDtypeStruct((N_TOK_HEADS, s, HEAD_PAD), F32)
    res = pl.pallas_call(
        body,
        grid_spec=pltpu.PrefetchScalarGridSpec(
            num_scalar_prefetch=2, grid=(ngroups, npairs),
            in_specs=[qspec, kspec, kspec, stspec, qspec] + [ANY] * nx,
            out_specs=[dqspec, kspec, kspec] + [ANY] * nx,
            scratch_shapes=[pltpu.VMEM((nh, t, HEAD_PAD), F32)] * 2 + ex_scratch),
        out_shape=[out, out, jax.ShapeDtypeStruct(out.shape, BF16)] + ex_out_shape,
        name="flash_bwd", compiler_params=_cp(2))(itab, jtab, q, k, v, stats, do, *ex_arrays)
    return res[:3], res[3:]


def _mem_probs(qp, kp, hh):
    lane = _lane(qp.shape)
    keep = (lane < HEAD_DIM) if hh == 0 else (lane >= HEAD_DIM)
    qh = jnp.where(keep, qp, 0.0).astype(BF16)
    sc = _dot_nt(qh, kp) * (1.0 / math.sqrt(HEAD_DIM))
    e = jnp.exp(sc - jnp.max(sc, axis=-1, keepdims=True))
    return e / jnp.sum(e, axis=-1, keepdims=True), keep


def _mix_out_fwd(tok, z, memkv, w, h, g, b, tgt, g0, q0, padded, name, ts):
    s = z.shape[0]
    zw = z.shape[1]
    tok_spec = _heads(ts) if padded else _rows(ts, TOK_WIDTH)
    with_loss = tgt is not None

    def body(*refs):
        tok_ref, z_ref, mkv_ref, w_ref, h_ref, g_ref, b_ref = refs[:7]
        if with_loss:
            t_ref, cat_ref, y_ref, dpre_ref, dgb_ref, loss_ref = refs[7:]
        else:
            cat_ref, y_ref, pre_ref, out_ref = refs[7:]
        if padded:
            for p in range(N_TOK_HEADS // 2):
                cat_ref[:, p * LANES:(p + 1) * LANES] = (
                    tok_ref[2 * p] + pltpu.roll(tok_ref[2 * p + 1], HEAD_DIM, 1))
        else:
            cat_ref[:, 0:TOK_WIDTH] = tok_ref[...]
        for pr in range(N_MEM_HEADS // 2):
            sl = slice(pr * LANES, (pr + 1) * LANES)
            qp = z_ref[:, q0 + pr * LANES:q0 + (pr + 1) * LANES]
            kp = mkv_ref[:, sl].astype(BF16)
            vp = mkv_ref[:, MEM_WIDTH + pr * LANES:MEM_WIDTH + (pr + 1) * LANES].astype(BF16)
            outs = []
            for hh in range(2):
                p, _ = _mem_probs(qp, kp, hh)
                outs.append(_dot(p.astype(BF16), vp))
            lane = _lane(outs[0].shape)
            cat_ref[:, TOK_WIDTH + pr * LANES:TOK_WIDTH + (pr + 1) * LANES] = jnp.where(
                lane < HEAD_DIM, outs[0], outs[1])
        gate = z_ref[:, g0:g0 + MIX_WIDTH]
        yb = (cat_ref[...] * (gate * _sigmoid(gate))).astype(BF16)
        y_ref[...] = yb
        pre = ALPHA * h_ref[...] + _dot(yb, w_ref[...])
        xhat, rstd = _ln_stats(pre)
        hout = xhat * g_ref[...] + b_ref[...]
        if with_loss:
            @pl.when(pl.program_id(0) == 0)
            def _():
                loss_ref[...] = jnp.zeros_like(loss_ref)
                dgb_ref[...] = jnp.zeros_like(dgb_ref)
            err = hout - t_ref[...]
            loss_ref[...] += 0.5 * jnp.sum(jnp.mean(err * err, axis=-1, keepdims=True))
            dh = err * (1.0 / D_MODEL)
            dpre_ref[...] = _ln_bwd(dh, xhat, rstd, g_ref[...])
            dgb_ref[0:1, :] += jnp.sum(dh * xhat, axis=0, keepdims=True)
            dgb_ref[1:2, :] += jnp.sum(dh, axis=0, keepdims=True)
        else:
            pre_ref[...] = pre
            out_ref[...] = hout

    act = jax.ShapeDtypeStruct((s, D_MODEL), F32)
    in_specs = [tok_spec, _rows(ts, zw), _full((MEM_LEN, 2 * MEM_WIDTH)),
                _full((MIX_WIDTH, D_MODEL)), _rows(ts, D_MODEL),
                _full((1, D_MODEL)), _full((1, D_MODEL))]
    out_specs = [_rows(ts, MIX_WIDTH)] * 2
    out_shape = [jax.ShapeDtypeStruct((s, MIX_WIDTH), F32),
                 jax.ShapeDtypeStruct((s, MIX_WIDTH), BF16)]
    args = [tok, z, memkv, w, h, g, b]
    if with_loss:
        in_specs.append(_rows(ts, D_MODEL))
        out_specs += [_rows(ts, D_MODEL), _full((SUBLANES, D_MODEL)), _full((SUBLANES, LANES))]
        out_shape += [act, jax.ShapeDtypeStruct((SUBLANES, D_MODEL), F32),
                      jax.ShapeDtypeStruct((SUBLANES, LANES), F32)]
        args.append(tgt)
    else:
        out_specs += [_rows(ts, D_MODEL)] * 2
        out_shape += [act, act]
    return pl.pallas_call(
        body, grid=(s // ts,), in_specs=in_specs, out_specs=out_specs, out_shape=out_shape,
        name=name, compiler_params=_cp(1))(*args)


def _gate_mem_bwd(dpre, y, w_t, cat, z, memkv, lse, g0, q0, name, ts):
    s = z.shape[0]
    zw = z.shape[1]
    padded = lse is not None
    gq_w = MIX_WIDTH + MEM_WIDTH

    def body(*refs):
        if padded:
            (dpre_ref, y_ref, wt_ref, cat_ref, z_ref, mkv_ref, lse_ref,
             dzg_ref, dtok_ref, dmkv_ref, dw_ref, st_ref) = refs
        else:
            (dpre_ref, y_ref, wt_ref, cat_ref, z_ref, mkv_ref,
             dzg_ref, dtok_ref, dmkv_ref, dw_ref) = refs

        @pl.when(pl.program_id(0) == 0)
        def _():
            dmkv_ref[...] = jnp.zeros_like(dmkv_ref)
            dw_ref[...] = jnp.zeros_like(dw_ref)

        dpb = dpre_ref[...].astype(BF16)
        dy_ = _dot(dpb, wt_ref[...])
        dw_ref[...] += _dot_tn(y_ref[...], dpb)
        gate = z_ref[:, g0:g0 + MIX_WIDTH]
        sg = _sigmoid(gate)
        dzg_ref[:, 0:MIX_WIDTH] = (dy_ * cat_ref[...]
                                   * (sg * (1.0 + gate * (1.0 - sg)))).astype(BF16)
        dcat = dy_ * (gate * sg)
        if padded:
            low = _lane((ts, LANES)) < HEAD_DIM
            for p in range(N_TOK_HEADS // 2):
                d = dcat[:, p * LANES:(p + 1) * LANES]
                prod = d * cat_ref[:, p * LANES:(p + 1) * LANES]
                first = jnp.sum(jnp.where(low, prod, 0.0), axis=-1, keepdims=True)
                second = jnp.sum(jnp.where(low, 0.0, prod), axis=-1, keepdims=True)
                dtok_ref[2 * p] = jnp.where(low, d, 0.0).astype(BF16)
                dtok_ref[2 * p + 1] = jnp.where(low, pltpu.roll(d, HEAD_DIM, 1), 0.0).astype(BF16)
                for hh, delta in ((2 * p, first), (2 * p + 1, second)):
                    both = jnp.where(low, lse_ref[hh], delta).T
                    st_ref[hh, 0:1, :] = both[0:1, :]
                    st_ref[hh, 1:2, :] = both[HEAD_DIM:HEAD_DIM + 1, :]
        else:
            dtok_ref[...] = dcat[:, 0:TOK_WIDTH]
        for pr in range(N_MEM_HEADS // 2):
            sl = slice(pr * LANES, (pr + 1) * LANES)
            vsl = slice(MEM_WIDTH + pr * LANES, MEM_WIDTH + (pr + 1) * LANES)
            qp = z_ref[:, q0 + pr * LANES:q0 + (pr + 1) * LANES]
            qpb = qp.astype(BF16)
            kp = mkv_ref[:, sl].astype(BF16)
            vp = mkv_ref[:, vsl].astype(BF16)
            dmo = dcat[:, TOK_WIDTH + pr * LANES:TOK_WIDTH + (pr + 1) * LANES]
            dqp = None
            for hh in range(2):
                p, keep = _mem_probs(qp, kp, hh)
                do_h = jnp.where(keep, dmo, 0.0).astype(BF16)
                dmkv_ref[:, vsl] += _dot_tn(p.astype(BF16), do_h)
                dp = _dot_nt(do_h, vp)
                ds = (p * (dp - jnp.sum(dp * p, axis=-1, keepdims=True))
                      * (1.0 / math.sqrt(HEAD_DIM))).astype(BF16)
                dqh = jnp.where(keep, _dot(ds, kp), 0.0)
                dqp = dqh if dqp is None else dqp + dqh
                dkh = _dot_tn(ds, qpb)
                klane = _lane(dkh.shape)
                kkeep = (klane < HEAD_DIM) if hh == 0 else (klane >= HEAD_DIM)
                dmkv_ref[:, sl] += jnp.where(kkeep, dkh, 0.0)
            dzg_ref[:, MIX_WIDTH + pr * LANES:MIX_WIDTH + (pr + 1) * LANES] = dqp.astype(BF16)

    in_specs = [_rows(ts, D_MODEL), _rows(ts, MIX_WIDTH), _full((D_MODEL, MIX_WIDTH)),
                _rows(ts, MIX_WIDTH), _rows(ts, zw), _full((MEM_LEN, 2 * MEM_WIDTH))]
    out_specs = [_rows(ts, gq_w), _heads(ts) if padded else _rows(ts, TOK_WIDTH),
                 _full((MEM_LEN, 2 * MEM_WIDTH)), _full((MIX_WIDTH, D_MODEL))]
    heads_shape = (N_TOK_HEADS, s, HEAD_PAD)
    out_shape = [jax.ShapeDtypeStruct((s, gq_w), BF16),
                 jax.ShapeDtypeStruct(heads_shape, BF16) if padded
                 else jax.ShapeDtypeStruct((s, TOK_WIDTH), F32),
                 jax.ShapeDtypeStruct((MEM_LEN, 2 * MEM_WIDTH), F32),
                 jax.ShapeDtypeStruct((MIX_WIDTH, D_MODEL), F32)]
    args = [dpre, y, w_t, cat, z, memkv]
    if padded:
        in_specs.append(_heads(ts))
        out_specs.append(pl.BlockSpec((N_TOK_HEADS, 2, ts), lambda i: (0, 0, i)))
        out_shape.append(jax.ShapeDtypeStruct((N_TOK_HEADS, 2, s), F32))
        args.append(lse)
    return pl.pallas_call(
        body, grid=(s // ts,), in_specs=in_specs, out_specs=out_specs, out_shape=out_shape,
        name=name, compiler_params=_cp(1))(*args)


def _ln_stats(pre):
    mu = jnp.mean(pre, axis=-1, keepdims=True)
    d = pre - mu
    rstd = lax.rsqrt(jnp.mean(d * d, axis=-1, keepdims=True) + NORM_EPS)
    return d * rstd, rstd


def _ln_bwd(dh, xhat, rstd, g):
    dxh = dh * g
    return rstd * (dxh - jnp.mean(dxh, axis=-1, keepdims=True)
                   - xhat * jnp.mean(dxh * xhat, axis=-1, keepdims=True))


def _linear_bwd(x, dys, offs, w_t, resid, ln, name, ts):
    s, kdim = x.shape
    n = w_t.shape[0]
    widths = [d.shape[1] for d in dys]
    npieces = len(dys)
    with_ln = ln is not None

    def body(*refs):
        x_ref = refs[0]
        dy_refs = refs[1:1 + npieces]
        if with_ln:
            wt_ref, r_ref, pre_ref, g_ref, dx_ref, dw_ref, dgb_ref = refs[1 + npieces:]
        else:
            wt_ref, r_ref, dx_ref, dw_ref = refs[1 + npieces:]

        @pl.when(pl.program_id(0) == 0)
        def _():
            dw_ref[...] = jnp.zeros_like(dw_ref)
            if with_ln:
                dgb_ref[...] = jnp.zeros_like(dgb_ref)

        xb = x_ref[...].astype(BF16)
        dx = ALPHA * r_ref[...]
        for dy_ref, off, wd in zip(dy_refs, offs, widths):
            dyb = dy_ref[...].astype(BF16)
            dx = dx + _dot(dyb, wt_ref[off:off + wd, :])
            dw_ref[:, off:off + wd] += _dot_tn(xb, dyb)
        if with_ln:
            xhat, rstd = _ln_stats(pre_ref[...])
            dx_ref[...] = _ln_bwd(dx, xhat, rstd, g_ref[...])
            dgb_ref[0:1, :] += jnp.sum(dx * xhat, axis=0, keepdims=True)
            dgb_ref[1:2, :] += jnp.sum(dx, axis=0, keepdims=True)
        else:
            dx_ref[...] = dx

    in_specs = ([_rows(ts, kdim)] + [_rows(ts, wd) for wd in widths]
                + [_full((n, kdim)), _rows(ts, kdim)])
    out_specs = [_rows(ts, kdim), _full((kdim, n))]
    out_shape = [jax.ShapeDtypeStruct((s, kdim), F32), jax.ShapeDtypeStruct((kdim, n), F32)]
    args = [x, *dys, w_t, resid]
    if with_ln:
        in_specs += [_rows(ts, kdim), _full((1, kdim))]
        out_specs.append(_full((SUBLANES, kdim)))
        out_shape.append(jax.ShapeDtypeStruct((SUBLANES, kdim), F32))
        args += list(ln)
    return pl.pallas_call(
        body, grid=(s // ts,), in_specs=in_specs, out_specs=out_specs, out_shape=out_shape,
        name=name, compiler_params=_cp(1))(*args)


def _wgrad_small(x, dy, name):
    def body(x_ref, dy_ref, dw_ref):
        dw_ref[...] = _dot_tn(x_ref[...].astype(BF16), dy_ref[...].astype(BF16))

    return pl.pallas_call(
        body, out_shape=jax.ShapeDtypeStruct((x.shape[1], dy.shape[1]), F32),
        name=name, compiler_params=pltpu.CompilerParams(vmem_limit_bytes=VMEM_LIMIT))(x, dy)


def _shift_down(u, carry8, k):
    if k == 0:
        return u
    rolled = pltpu.roll(u, k, 0)
    row = lax.broadcasted_iota(jnp.int32, carry8.shape, 0)
    top = jnp.where(row < k, pltpu.roll(carry8, k, 0), rolled[0:SUBLANES])
    return jnp.concatenate([top, rolled[SUBLANES:]], axis=0)


def _shift_up(u, carry8, k):
    if k == 0:
        return u
    n = u.shape[0]
    rolled = pltpu.roll(u, n - k, 0)
    row = lax.broadcasted_iota(jnp.int32, carry8.shape, 0)
    bot = jnp.where(row >= SUBLANES - k, pltpu.roll(carry8, SUBLANES - k, 0),
                    rolled[n - SUBLANES:])
    return jnp.concatenate([rolled[:n - SUBLANES], bot], axis=0)


def _neg_expm1(t):
    e = jnp.exp(t)
    em1 = e - 1.0
    safe = jnp.where(e == 1.0, 1.0, jnp.log(e))
    return -jnp.where(e == 1.0, t, jnp.where(em1 == -1.0, -1.0, em1 * t / safe))


def _lru_gates(u, carry8, cw_ref, vec_ref, wr_ref, wi_ref):
    taps = [_shift_down(u, carry8, k) for k in range(CONV_W)]
    xc = vec_ref[0:1, :] + cw_ref[3:4, :] * u
    for k in range(1, CONV_W):
        xc = xc + cw_ref[3 - k:4 - k, :] * taps[k]
    xb = xc.astype(BF16)
    r = _sigmoid(_dot(xb, wr_ref[...]) + vec_ref[1:2, :])
    ig = _sigmoid(_dot(xb, wi_ref[...]) + vec_ref[2:3, :])
    nlam = -vec_ref[3:4, :]
    softplus = jnp.maximum(nlam, 0.0) + jnp.log(1.0 + jnp.exp(-jnp.abs(nlam)))
    cneg = -LRU_C * softplus
    log_a = cneg * r
    a = jnp.exp(log_a)
    sq = jnp.sqrt(_neg_expm1(2.0 * log_a))
    return xc, r, ig, cneg, a, sq, taps


def _chained_scan(a_ref, b_ref, out_ref, cum_scr, x_in):
    rows_total, w = a_ref.shape
    nseg = SCAN_SEGMENTS
    seg = rows_total // nseg

    def step(t, carry):
        xs, ps = carry
        new_x, new_p = [], []
        for sg in range(nseg):
            row = pl.ds(sg * seg + t, 1)
            a = a_ref[row, :]
            x = a * xs[sg] + b_ref[row, :]
            out_ref[row, :] = x
            new_x.append(x)
            if sg > 0:
                p = a * ps[sg - 1]
                cum_scr[row, :] = p
                new_p.append(p)
        return tuple(new_x), tuple(new_p)

    zero, one = jnp.zeros((1, w), F32), jnp.ones((1, w), F32)
    xs, _ = lax.fori_loop(0, seg, step, ((x_in,) + (zero,) * (nseg - 1), (one,) * (nseg - 1)))
    x_prev = xs[0]
    for sg in range(1, nseg):
        rows = slice(sg * seg, (sg + 1) * seg)
        out_ref[rows, :] = out_ref[rows, :] + cum_scr[rows, :] * x_prev
        x_prev = out_ref[(sg + 1) * seg - 1:(sg + 1) * seg, :]
    return x_prev


def _lru_fwd(x, win, cw8, vec8, wr, wi, ts):
    s = x.shape[0]

    def body(x_ref, win_ref, cw_ref, vec_ref, wr_ref, wi_ref, u_ref, zg_ref, hs_ref,
             cu_scr, ch_scr, a_scr, gx_scr, cum_scr):
        @pl.when(pl.program_id(0) == 0)
        def _():
            cu_scr[...] = jnp.zeros_like(cu_scr)
            ch_scr[...] = jnp.zeros_like(ch_scr)

        zfull = _dot(x_ref[...].astype(BF16), win_ref[...])
        u = zfull[:, 0:ZA_W]
        u_ref[...] = u
        zg_ref[...] = zfull[:, ZA_W:ZP]
        xc, _, ig, _, a, sq, _ = _lru_gates(u, cu_scr[...], cw_ref, vec_ref, wr_ref, wi_ref)
        a_scr[...] = a
        gx_scr[...] = sq * (ig * xc)
        ch_scr[0:1, :] = _chained_scan(a_scr, gx_scr, hs_ref, cum_scr, ch_scr[0:1, :])
        cu_scr[...] = u[ts - SUBLANES:, :]

    w = TOK_WIDTH
    return pl.pallas_call(
        body, grid=(s // ts,),
        in_specs=[_rows(ts, D_MODEL), _full((D_MODEL, ZP)),
                  _full((SUBLANES, w)), _full((SUBLANES, w)), _full((w, w)), _full((w, w))],
        out_specs=[_rows(ts, w), _rows(ts, ZG_W), _rows(ts, w)],
        out_shape=[jax.ShapeDtypeStruct((s, w), F32), jax.ShapeDtypeStruct((s, ZG_W), F32),
                   jax.ShapeDtypeStruct((s, w), F32)],
        scratch_shapes=[pltpu.VMEM((SUBLANES, w), F32), pltpu.VMEM((SUBLANES, w), F32)]
                       + [pltpu.VMEM((ts, w), F32)] * 3,
        name="lru_fwd", compiler_params=_cp(1))(x, win, cw8, vec8, wr, wi)


def _lru_bwd(z1, dhs, hs, cw8, vec8, wr, wi, wr_t, wi_t, ts):
    s = z1.shape[0]
    nb = s // ts
    w = TOK_WIDTH
    tiles = ts // SUBLANES

    def body(u_ref, up_ref, dhs_ref, hs_ref, hsp_ref, cw_ref, vec_ref, wr_ref, wi_ref,
             wrt_ref, wit_ref, du_ref, dwr_ref, dwi_ref, dvec_ref,
             cc_scr, cd_scr, a_scr, dh_scr):
        i = pl.program_id(0)

        @pl.when(i == 0)
        def _():
            cc_scr[...] = jnp.zeros_like(cc_scr)
            cd_scr[...] = jnp.zeros_like(cd_scr)
            dwr_ref[...] = jnp.zeros_like(dwr_ref)
            dwi_ref[...] = jnp.zeros_like(dwi_ref)
            dvec_ref[...] = jnp.zeros_like(dvec_ref)

        u = u_ref[...]
        first = i == nb - 1
        carry8 = jnp.where(first, 0.0, up_ref[...])
        xc, r, ig, cneg, a, sq, taps = _lru_gates(u, carry8, cw_ref, vec_ref, wr_ref, wi_ref)
        a_scr[...] = a

        def step(n, c):
            t = ts - 1 - n
            dh = dhs_ref[pl.ds(t, 1), :] + c
            dh_scr[pl.ds(t, 1), :] = dh
            return a_scr[pl.ds(t, 1), :] * dh

        cc_scr[0:1, :] = lax.fori_loop(0, ts, step, cc_scr[0:1, :])
        dh = dh_scr[...]
        hprev = _shift_down(hs_ref[...], jnp.where(first, 0.0, hsp_ref[...]), 1)
        ix = ig * xc
        dix = dh * sq
        dlog_a = dh * hprev * a - (dh * ix) * (a * a) / sq
        dpr = (dlog_a * cneg) * r * (1.0 - r)
        dpi = (dix * xc) * ig * (1.0 - ig)
        dprb, dpib = dpr.astype(BF16), dpi.astype(BF16)
        xb = xc.astype(BF16)
        dwr_ref[...] += _dot_tn(xb, dprb)
        dwi_ref[...] += _dot_tn(xb, dpib)
        dxc = dix * ig + _dot(dprb, wrt_ref[...]) + _dot(dpib, wit_ref[...])
        for k in range(CONV_W):
            dvec_ref[3 - k:4 - k, :] += jnp.sum(dxc * taps[k], axis=0, keepdims=True)
        dvec_ref[4:5, :] += jnp.sum(dxc, axis=0, keepdims=True)
        dvec_ref[5:6, :] += jnp.sum(dpr, axis=0, keepdims=True)
        dvec_ref[6:7, :] += jnp.sum(dpi, axis=0, keepdims=True)
        dvec_ref[7:8, :] += (jnp.sum(dlog_a * r, axis=0, keepdims=True)
                             * (LRU_C * _sigmoid(-vec_ref[3:4, :])))
        nxt = cd_scr[...]
        du = cw_ref[3:4, :] * dxc
        for k in range(1, CONV_W):
            du = du + cw_ref[3 - k:4 - k, :] * _shift_up(dxc, nxt, k)
        du_ref[...] = du.astype(BF16)
        cd_scr[...] = dxc[0:SUBLANES, :]

    rev = lambda i: (nb - 1 - i, 0)
    prev8 = lambda i: (jnp.maximum((nb - 1 - i) * tiles - 1, 0), 0)
    blk = pl.BlockSpec((ts, w), rev)
    before = pl.BlockSpec((SUBLANES, w), prev8)
    scr = pltpu.VMEM((ts, w), F32)
    return pl.pallas_call(
        body, grid=(nb,),
        in_specs=[blk, before, blk, blk, before,
                  _full((SUBLANES, w)), _full((SUBLANES, w)),
                  _full((w, w)), _full((w, w)), _full((w, w)), _full((w, w))],
        out_specs=[blk, _full((w, w)), _full((w, w)), _full((SUBLANES, w))],
        out_shape=[jax.ShapeDtypeStruct((s, w), BF16), jax.ShapeDtypeStruct((w, w), F32),
                   jax.ShapeDtypeStruct((w, w), F32), jax.ShapeDtypeStruct((SUBLANES, w), F32)],
        scratch_shapes=[pltpu.VMEM((SUBLANES, w), F32), pltpu.VMEM((SUBLANES, w), F32),
                        scr, scr],
        name="lru_bwd", compiler_params=_cp(1))(
            z1, z1, dhs, hs, hs, cw8, vec8, wr, wi, wr_t, wi_t)


def _adamw(parts, w, m, v, name):
    n = len(parts)
    rows_per = parts[0].shape[1]

    def body(*refs):
        p_refs = refs[:n]
        w_ref, m_ref, v_ref, g_ref, d_ref, nm_ref, nv_ref = refs[n:]
        for l, p_ref in enumerate(p_refs):
            rows = slice(l * rows_per, (l + 1) * rows_per)
            g = p_ref[0].astype(F32)
            for dev in range(1, N_DEV):
                g = g + p_ref[dev].astype(F32)
            g_ref[rows, :] = g
            nm = ADAM_B1 * m_ref[rows, :] + (1.0 - ADAM_B1) * g
            nv = ADAM_B2 * v_ref[rows, :] + (1.0 - ADAM_B2) * (g * g)
            m_hat = nm / (1.0 - ADAM_B1 ** ADAM_STEP)
            v_hat = nv / (1.0 - ADAM_B2 ** ADAM_STEP)
            d_ref[rows, :] = -ADAM_LR * (m_hat / (jnp.sqrt(v_hat) + ADAM_EPS)
                                         + ADAM_WD * w_ref[rows, :])
            nm_ref[rows, :] = nm
            nv_ref[rows, :] = nv

    out = jax.ShapeDtypeStruct(w.shape, F32)
    return pl.pallas_call(
        body, out_shape=[out] * 4, name=name,
        compiler_params=pltpu.CompilerParams(vmem_limit_bytes=VMEM_LIMIT))(*parts, w, m, v)


def _adam_update(g, w, m, v):
    nm = ADAM_B1 * m + (1.0 - ADAM_B1) * g
    nv = ADAM_B2 * v + (1.0 - ADAM_B2) * (g * g)
    m_hat = nm / (1.0 - ADAM_B1 ** ADAM_STEP)
    v_hat = nv / (1.0 - ADAM_B2 ** ADAM_STEP)
    return -ADAM_LR * (m_hat / (jnp.sqrt(v_hat) + ADAM_EPS) + ADAM_WD * w), nm, nv


def _adamw_small(parts, layout, ws, ms, vs, name):
    n = len(layout)

    def body(*refs):
        p_ref = refs[0]
        w_refs, m_refs, v_refs = refs[1:1 + n], refs[1 + n:1 + 2 * n], refs[1 + 2 * n:1 + 3 * n]
        tile_ref = refs[1 + 3 * n]
        outs = refs[2 + 3 * n:]
        tile = p_ref[0]
        for dev in range(1, N_DEV):
            tile = tile + p_ref[dev]
        tile_ref[...] = tile
        for p, (rows, c0, nc) in enumerate(layout):
            for r, src in enumerate(rows):
                g = tile_ref[src:src + 1, c0:c0 + nc]
                d, nm, nv = _adam_update(g, w_refs[p][r:r + 1, :], m_refs[p][r:r + 1, :],
                                         v_refs[p][r:r + 1, :])
                for kind, val in enumerate((g, d, nm, nv)):
                    outs[4 * p + kind][r:r + 1, :] = val

    out_shape = [jax.ShapeDtypeStruct(parts.shape[1:], F32)]
    for w in ws:
        out_shape += [jax.ShapeDtypeStruct(w.shape, F32)] * 4
    res = pl.pallas_call(
        body, out_shape=out_shape, name=name,
        compiler_params=pltpu.CompilerParams(vmem_limit_bytes=VMEM_LIMIT))(parts, *ws, *ms, *vs)
    return res[0], [res[1 + 4 * p:5 + 4 * p] for p in range(n)]


ANY = pl.BlockSpec(memory_space=pl.ANY)
MESH = pl.DeviceIdType.MESH


def _slot(p):
    return 4 * p[0] + 2 * p[1] + p[2]


def _allgather(xs):
    n = len(xs)

    def body(*refs):
        x_refs, o_refs = refs[:n], refs[n:2 * n]
        send_sems, recv_sems, local_sems = refs[2 * n:]
        x, y, c = lax.axis_index("x"), lax.axis_index("y"), lax.axis_index("c")
        me, sibling = (x, y, c), (x, y, 1 - c)
        chips = [(1 - x, y), (x, 1 - y), (1 - x, 1 - y)]

        def copy(a, k, block, to, from_input=False):
            dst = o_refs[a].at[_slot(block)]
            return pltpu.make_async_remote_copy(
                src_ref=x_refs[a] if from_input else dst, dst_ref=dst,
                send_sem=send_sems.at[a, k], recv_sem=recv_sems.at[a, k],
                device_id=to, device_id_type=MESH)

        mine = [pltpu.make_async_copy(x_refs[a], o_refs[a].at[_slot(me)], local_sems.at[a])
                for a in range(n)]
        for cp in mine:
            cp.start()
        first = []
        for a in range(n):
            first.append(copy(a, 0, me, sibling, True))
            first += [copy(a, 1 + j, me, (*chip, c), True) for j, chip in enumerate(chips)]
        for cp in first:
            cp.start()
        passed = []
        for j, chip in enumerate(chips):
            for a in range(n):
                copy(a, 1 + j, (*chip, c), me).wait_recv()
                cp = copy(a, 4 + j, (*chip, c), sibling)
                cp.start()
                passed.append(cp)
        for a in range(n):
            copy(a, 0, sibling, me).wait_recv()
            for j, chip in enumerate(chips):
                copy(a, 4 + j, (*chip, 1 - c), me).wait_recv()
        for cp in first + passed:
            cp.wait_send()
        for cp in mine:
            cp.wait()

    return pl.pallas_call(
        body,
        out_shape=[jax.ShapeDtypeStruct((N_DEV,) + t.shape, t.dtype) for t in xs],
        in_specs=[ANY] * n, out_specs=[ANY] * n,
        scratch_shapes=[pltpu.SemaphoreType.DMA((n, 7)), pltpu.SemaphoreType.DMA((n, 7)),
                        pltpu.SemaphoreType.DMA((n,))],
        name="allgather_weights")(*xs)


class _Exchange:
    def __init__(self, arrays, kinds):
        self.arrays, self.kinds, self.n = list(arrays), list(kinds), len(arrays)
        self.shapes = [self._part_shape(a, k) for a, k in zip(arrays, kinds)]
        self.out_shape = [jax.ShapeDtypeStruct((N_DEV,) + shp, a.dtype)
                          for shp, a in zip(self.shapes, arrays)]
        self.scratch = [pltpu.SemaphoreType.DMA((self.n, N_DEV - 1)),
                        pltpu.SemaphoreType.DMA((self.n, N_DEV - 1)),
                        pltpu.SemaphoreType.DMA((self.n,))]

    @staticmethod
    def _part_shape(arr, kind):
        if kind == "chunks":
            return arr.shape[1:]
        if kind == "cols":
            return (arr.shape[0], arr.shape[1] // N_DEV)
        if kind == "rows":
            return (arr.shape[0] // N_DEV, arr.shape[1])
        return arr.shape

    def copies(self, in_refs, out_refs, sems):
        send_sems, recv_sems, local_sems = sems
        x, y, c = lax.axis_index("x"), lax.axis_index("y"), lax.axis_index("c")
        me = _slot((x, y, c))

        def part(a, dev):
            ref, kind, shp = in_refs[a], self.kinds[a], self.shapes[a]
            if kind == "chunks":
                return ref.at[dev]
            if kind == "cols":
                return ref.at[:, pl.ds(pl.multiple_of(dev * shp[1], LANES), shp[1])]
            if kind == "rows":
                return ref.at[pl.ds(pl.multiple_of(dev * shp[0], SUBLANES), shp[0]), :]
            return ref

        cps = [pltpu.make_async_copy(part(a, me), out_refs[a].at[me], local_sems.at[a])
               for a in range(self.n)]
        for rel in range(1, N_DEV):
            peer = (x ^ (rel >> 2), y ^ ((rel >> 1) & 1), c ^ (rel & 1))
            for a in range(self.n):
                cps.append(pltpu.make_async_remote_copy(
                    src_ref=part(a, _slot(peer)), dst_ref=out_refs[a].at[me],
                    send_sem=send_sems.at[a, rel - 1], recv_sem=recv_sems.at[a, rel - 1],
                    device_id=peer, device_id_type=MESH))
        return cps


def _exchange_grads(arrays, kinds, name):
    ex = _Exchange(arrays, kinds)
    n = ex.n

    def body(*refs):
        cps = ex.copies(refs[:n], refs[n:2 * n], refs[2 * n:])
        for cp in cps:
            cp.start()
        for cp in cps:
            cp.wait()

    return pl.pallas_call(
        body, out_shape=ex.out_shape, in_specs=[ANY] * n, out_specs=[ANY] * n,
        scratch_shapes=ex.scratch, name=name)(*arrays)


BIG = [("mla_w_in", (D_MODEL, MLA_IN), 1), ("mla_w_uq", (Q_LORA, N_TOK_HEADS * QK_DIM), 1),
       ("mla_w_ukv", (KV_LORA, N_TOK_HEADS * 2 * HEAD_DIM), 1), ("lru_w_in", (D_MODEL, LRU_IN), 1),
       ("w_mem_kv", (2, D_MODEL, 2 * MEM_WIDTH), 1), ("w_out", (2, MIX_WIDTH, D_MODEL), 1)]
SMALL = [("lru_conv_w", (CONV_W, TOK_WIDTH), 1), ("lru_conv_b", (TOK_WIDTH,), 0),
         ("lru_b_rgate", (TOK_WIDTH,), 0), ("lru_b_igate", (TOK_WIDTH,), 0),
         ("lru_lambda", (TOK_WIDTH,), 0)]
REPL = [("mla_q_norm", (Q_LORA,)), ("mla_kv_norm", (KV_LORA,)),
        ("lru_w_rgate", (N_TOK_HEADS, HEAD_DIM, HEAD_DIM)),
        ("lru_w_igate", (N_TOK_HEADS, HEAD_DIM, HEAD_DIM)),
        ("ln_g", (2, D_MODEL)), ("ln_b", (2, D_MODEL))]


def _shard_shape(shape, axis):
    return tuple(d // N_DEV if a == axis else d for a, d in enumerate(shape))


def _size(shape):
    return math.prod(shape)


N_OWN_SHAPE = 4
SMALL_ROWS = SUBLANES


def _pack_rows(flat_parts, rows):
    flat = jnp.concatenate([p.reshape(-1) for p in flat_parts])
    return jnp.pad(flat, (0, rows * LANES - flat.shape[0])).reshape(rows, LANES)


def _to_chunks(full, axis):
    shape = full.shape
    split = shape[:axis] + (N_DEV, shape[axis] // N_DEV) + shape[axis + 1:]
    return jnp.moveaxis(full.reshape(split), axis, 0).reshape(N_DEV, -1)


def _from_chunks(chunks, shape, axis):
    sh = _shard_shape(shape, axis)
    t = chunks.reshape((N_DEV,) + sh)
    t = jnp.moveaxis(t, 0, axis)
    return t.reshape(shape)


def _split_flat(flat2d, table):
    out, off = [], 0
    for size in table:
        out.append(flat2d[:, off:off + size])
        off += size
    return out


def _win0_to_padded(w):
    z = lambda n: jnp.zeros((w.shape[0], n), w.dtype)
    return jnp.concatenate([w[:, 0:640], z(KR_LANE), w[:, 640:672],
                            z(LANES - KR_LANE - QK_ROPE), w[:, 672:1952]], axis=1)


def _win0_from_padded(wp):
    k0 = ZA_KR + KR_LANE
    return jnp.concatenate([wp[:, 0:640], wp[:, k0:k0 + QK_ROPE], wp[:, ZA_W:ZP]], axis=1)


def _pad_heads(w, per_head, lo, hi):
    t = w.reshape(w.shape[0], N_TOK_HEADS, per_head)[:, :, lo:hi]
    t = jnp.pad(t, ((0, 0), (0, 0), (0, HEAD_PAD - (hi - lo))))
    return t.reshape(w.shape[0], QKV_PAD)


def _unpad_heads(wp, width):
    return wp.reshape(wp.shape[0], N_TOK_HEADS, HEAD_PAD)[:, :, :width]


def _block_diag(w):
    eye = jnp.eye(N_TOK_HEADS, dtype=w.dtype)
    return (w[:, :, None, :] * eye[:, None, :, None]).reshape(TOK_WIDTH, TOK_WIDTH)


def _diag_blocks(d):
    t = d.reshape(N_TOK_HEADS, HEAD_DIM, N_TOK_HEADS, HEAD_DIM)
    return jnp.stack([t[g, :, g, :] for g in range(N_TOK_HEADS)])


def _rope_tables(positions):
    half = QK_ROPE // 2
    inv_freq = ROPE_THETA ** (-jnp.arange(half, dtype=F32) / half)
    ang = positions.astype(F32)[:, None] * inv_freq
    cos, sin = jnp.cos(ang), jnp.sin(ang)
    s = positions.shape[0]
    tail = jnp.zeros((s, HEAD_PAD - QK_DIM), F32)
    c = jnp.concatenate([jnp.ones((s, QK_NOPE), F32), cos, cos, tail], axis=1)
    sn = jnp.concatenate([jnp.zeros((s, QK_NOPE), F32), sin, sin, tail], axis=1)
    return c, sn


def _local_step(x, mem, positions, tgt, wts, ts, tatt, early_exchange):
    bf = lambda t: t.astype(BF16)
    win0 = _win0_to_padded(wts["mla_w_in"])
    wuq = _pad_heads(wts["mla_w_uq"], QK_DIM, 0, QK_DIM)
    wukv = jnp.concatenate([_pad_heads(wts["mla_w_ukv"], 2 * HEAD_DIM, 0, QK_NOPE),
                            _pad_heads(wts["mla_w_ukv"], 2 * HEAD_DIM, QK_NOPE, 2 * HEAD_DIM)],
                           axis=1)
    win1 = wts["lru_w_in"]
    wmkv, wout = wts["w_mem_kv"], wts["w_out"]
    gq = wts["mla_q_norm"].reshape(1, Q_LORA)
    gkv = wts["mla_kv_norm"].reshape(1, KV_LORA)
    ln_g, ln_b = wts["ln_g"], wts["ln_b"]
    wr, wi = bf(_block_diag(wts["lru_w_rgate"])), bf(_block_diag(wts["lru_w_igate"]))
    cw8 = jnp.pad(wts["lru_conv_w"], ((0, SUBLANES - CONV_W), (0, 0)))
    vec8 = jnp.pad(jnp.stack([wts["lru_conv_b"], wts["lru_b_rgate"], wts["lru_b_igate"],
                              wts["lru_lambda"]]), ((0, SUBLANES - 4), (0, 0)))
    tabs = _rope_tables(positions)
    tmem = mem.shape[0]

    za0, zg0, q, k, v = _mla_prep_fwd(x, win0, tabs, gq, gkv, wuq, wukv, ts)
    o, lse = _flash_fwd(q, k, v, tatt, FWD_HEADS)
    mkv0, = _rowmm(mem, wmkv[0], [2 * MEM_WIDTH], "mem_kv0", tmem)
    cat0, y0, pre0, h1 = _mix_out_fwd(o, zg0, mkv0, wout[0], x, ln_g[0:1], ln_b[0:1], None,
                                      0, MIX_WIDTH, True, "mix_out_fwd0", ts)
    del o
    u1, zg1, hs = _lru_fwd(h1, win1, cw8, vec8, wr, wi, ts)
    mkv1, = _rowmm(mem, wmkv[1], [2 * MEM_WIDTH], "mem_kv1", tmem)
    cat1, y1, dpre1, dgb1, loss8 = _mix_out_fwd(hs, zg1, mkv1, wout[1], h1, ln_g[1:2],
                                                ln_b[1:2], tgt, 0, MIX_WIDTH, False,
                                                "mix_out_loss", ts)
    loss = loss8[0, 0]

    dzg1, dhs, dmkv1, dwout1 = _gate_mem_bwd(dpre1, y1, wout[1].T, cat1, zg1, mkv1, None,
                                             0, MIX_WIDTH, "gate_mem_bwd1", ts)
    du, dwr, dwi, dvec = _lru_bwd(u1, dhs, hs, cw8, vec8, wr, wi, wr.T, wi.T, ts)
    dpre0, dwin1, dgb0 = _linear_bwd(h1, [du, dzg1], [0, ZA_W], win1.T, dpre1,
                                     (pre0, ln_g[0:1]), "in_proj_bwd1", ts)
    dwmkv1 = _wgrad_small(mem, dmkv1, "mem_kv_bwd1")
    dzg0, do, dmkv0, dwout0, stats = _gate_mem_bwd(dpre0, y0, wout[0].T, cat0, zg0, mkv0, lse,
                                                   0, MIX_WIDTH, "gate_mem_bwd0", ts)
    dwmkv0 = _wgrad_small(mem, dmkv0, "mem_kv_bwd0")
    early = {
        "lru_w_in": dwin1,
        "lru_small": dvec,
        "lru_w_rgate": _diag_blocks(dwr).reshape(TOK_WIDTH, HEAD_DIM),
        "lru_w_igate": _diag_blocks(dwi).reshape(TOK_WIDTH, HEAD_DIM),
        "w_mem_kv": [dwmkv0, dwmkv1],
        "w_out": [dwout0, dwout1],
    }
    (dq, dk, dv), got_early = _flash_bwd(q, k, v, stats, do, tatt, BWD_HEADS,
                                         early_exchange(early))
    dza, dzk, dwuq_p, dwukv_p, dg = _mla_prep_bwd(za0, dq, dk, dv, tabs, gq, gkv,
                                                  wuq.T, wukv.T, ts)
    gx, dwin0_p = _linear_bwd(x, [dza, dzk, dzg0], [ZA_CQ, ZA_KR, ZA_W], win0.T, dpre0,
                              None, "in_proj_bwd0", ts)

    dwukv = jnp.concatenate([_unpad_heads(dwukv_p[:, :QKV_PAD], HEAD_DIM),
                             _unpad_heads(dwukv_p[:, QKV_PAD:], HEAD_DIM)], axis=2)
    zrow = jnp.zeros((1, D_MODEL), F32)
    gains = jnp.pad(dg[0:1], ((0, 0), (0, D_MODEL - Q_LORA - KV_LORA)))
    small_repl = jnp.concatenate([dgb0[0:2], dgb1[0:2], gains,
                                  loss * jnp.ones((1, D_MODEL), F32), zrow, zrow], axis=0)
    late = {
        "mla_w_in": _win0_from_padded(dwin0_p),
        "mla_w_uq": _unpad_heads(dwuq_p, QK_DIM).reshape(Q_LORA, N_TOK_HEADS * QK_DIM),
        "mla_w_ukv": dwukv.reshape(KV_LORA, N_TOK_HEADS * 2 * HEAD_DIM),
        "small_repl": small_repl,
    }
    return gx, early, got_early, late


WEIGHT_ORDER = ["mla_w_in", "mla_q_norm", "mla_w_uq", "mla_kv_norm", "mla_w_ukv", "lru_w_in",
                "lru_conv_w", "lru_conv_b", "lru_w_rgate", "lru_b_rgate", "lru_w_igate",
                "lru_b_igate", "lru_lambda", "w_mem_kv", "w_out", "ln_g", "ln_b"]


def kernel(x, mem, positions, mla_w_in, mla_q_norm, mla_w_uq, mla_kv_norm, mla_w_ukv, lru_w_in, lru_conv_w, lru_conv_b, lru_w_rgate, lru_b_rgate, lru_w_igate, lru_b_igate, lru_lambda, w_mem_kv, w_out, ln_g, ln_b, loss_target, m_mla_w_in, m_mla_q_norm, m_mla_w_uq, m_mla_kv_norm, m_mla_w_ukv, m_lru_w_in, m_lru_conv_w, m_lru_conv_b, m_lru_w_rgate, m_lru_b_rgate, m_lru_w_igate, m_lru_b_igate, m_lru_lambda, m_w_mem_kv, m_w_out, m_ln_g, m_ln_b, v_mla_w_in, v_mla_q_norm, v_mla_w_uq, v_mla_kv_norm, v_mla_w_ukv, v_lru_w_in, v_lru_conv_w, v_lru_conv_b, v_lru_w_rgate, v_lru_b_rgate, v_lru_w_igate, v_lru_b_igate, v_lru_lambda, v_w_mem_kv, v_w_out, v_ln_g, v_ln_b):
    w_in = dict(mla_w_in=mla_w_in, mla_q_norm=mla_q_norm, mla_w_uq=mla_w_uq,
                mla_kv_norm=mla_kv_norm, mla_w_ukv=mla_w_ukv, lru_w_in=lru_w_in,
                lru_conv_w=lru_conv_w, lru_conv_b=lru_conv_b, lru_w_rgate=lru_w_rgate,
                lru_b_rgate=lru_b_rgate, lru_w_igate=lru_w_igate, lru_b_igate=lru_b_igate,
                lru_lambda=lru_lambda, w_mem_kv=w_mem_kv, w_out=w_out, ln_g=ln_g, ln_b=ln_b)
    m_in = dict(mla_w_in=m_mla_w_in, mla_q_norm=m_mla_q_norm, mla_w_uq=m_mla_w_uq,
                mla_kv_norm=m_mla_kv_norm, mla_w_ukv=m_mla_w_ukv, lru_w_in=m_lru_w_in,
                lru_conv_w=m_lru_conv_w, lru_conv_b=m_lru_conv_b, lru_w_rgate=m_lru_w_rgate,
                lru_b_rgate=m_lru_b_rgate, lru_w_igate=m_lru_w_igate, lru_b_igate=m_lru_b_igate,
                lru_lambda=m_lru_lambda, w_mem_kv=m_w_mem_kv, w_out=m_w_out, ln_g=m_ln_g,
                ln_b=m_ln_b)
    v_in = dict(mla_w_in=v_mla_w_in, mla_q_norm=v_mla_q_norm, mla_w_uq=v_mla_w_uq,
                mla_kv_norm=v_mla_kv_norm, mla_w_ukv=v_mla_w_ukv, lru_w_in=v_lru_w_in,
                lru_conv_w=v_lru_conv_w, lru_conv_b=v_lru_conv_b, lru_w_rgate=v_lru_w_rgate,
                lru_b_rgate=v_lru_b_rgate, lru_w_igate=v_lru_w_igate, lru_b_igate=v_lru_b_igate,
                lru_lambda=v_lru_lambda, w_mem_kv=v_w_mem_kv, w_out=v_w_out, ln_g=v_ln_g,
                ln_b=v_ln_b)
    s = x.shape[1]
    ts = min(ROW_BLOCK, s)
    tatt = min(ATT_BLOCK, s)
    big_sizes = [_size(sh) // N_DEV for _, sh, _ in BIG]
    small_sizes = [_size(sh) // N_DEV for _, sh, _ in SMALL]

    own = BIG[:N_OWN_SHAPE]
    packed = BIG[N_OWN_SHAPE:]
    own_local = [w_in[n][0].astype(BF16) for n, _, _ in own]
    packed_rows = sum(big_sizes[N_OWN_SHAPE:]) // LANES
    big_local = _pack_rows([w_in[n] for n, _, _ in packed], packed_rows).astype(BF16)
    small_local = _pack_rows([w_in[n] for n, _, _ in SMALL], SMALL_ROWS)
    *own_all, big_all, small_all = _allgather(own_local + [big_local, small_local])
    wts = {}
    for (n, sh, ax), g in zip(own, own_all):
        wts[n] = jnp.moveaxis(g, 0, 1).reshape(sh)
    for (n, sh, ax), part in zip(packed, _split_flat(big_all.reshape(N_DEV, -1),
                                                     big_sizes[N_OWN_SHAPE:])):
        wts[n] = _from_chunks(part, sh, ax)
    for (n, sh, ax), part in zip(SMALL, _split_flat(small_all.reshape(N_DEV, -1), small_sizes)):
        wts[n] = _from_chunks(part, sh, ax)
    for n, sh in REPL:
        wts[n] = w_in[n].reshape(sh)

    def early_exchange(g):
        small_chunks = jnp.moveaxis(g["lru_small"].reshape(SUBLANES, N_DEV, -1), 1, 0)
        sends = [(g["lru_w_in"], "cols"),
                 (g["w_mem_kv"][0], "rows"), (g["w_mem_kv"][1], "rows"),
                 (g["w_out"][0], "rows"), (g["w_out"][1], "rows"),
                 (small_chunks, "chunks"), (g["lru_w_rgate"], "all"), (g["lru_w_igate"], "all")]
        return _Exchange([a for a, _ in sends], [k for _, k in sends])

    gx, _, got_early, late = _local_step(x[0], mem[0], positions[0], loss_target[0], wts,
                                         ts, tatt, early_exchange)

    def chunked(name, shape):
        w = shape[1] // N_DEV
        return _to_chunks(late[name], 1).reshape(N_DEV, shape[0], w).astype(BF16)

    got_late = _exchange_grads(
        [chunked("mla_w_in", (D_MODEL, MLA_IN)),
         chunked("mla_w_uq", (Q_LORA, N_TOK_HEADS * QK_DIM)),
         chunked("mla_w_ukv", (KV_LORA, N_TOK_HEADS * 2 * HEAD_DIM)), late["small_repl"]],
        ["chunks", "chunks", "chunks", "all"], "exchange_grads")
    got = list(got_late[:3]) + list(got_early) + [got_late[3]]

    def flat2(d, name):
        t = d[name]
        return t.reshape(-1, t.shape[-1])

    def update(parts, view, name):
        return _adamw(parts, view(w_in), view(m_in), view(v_in), "adamw_" + name)

    res = {}
    for idx, name in [(0, "mla_w_in"), (1, "mla_w_uq"), (2, "mla_w_ukv"), (3, "lru_w_in"),
                      (9, "lru_w_rgate"), (10, "lru_w_igate")]:
        res[name] = update([got[idx]], functools.partial(flat2, name=name), name)
    res["w_mem_kv"] = update([got[4], got[5]], functools.partial(flat2, name="w_mem_kv"),
                             "w_mem_kv")
    res["w_out"] = update([got[6], got[7]], functools.partial(flat2, name="w_out"), "w_out")
    def small(parts, names, layout, name):
        view = lambda d: [flat2(d, n) for n in names]
        tile, outs = _adamw_small(parts, layout, view(w_in), view(m_in), view(v_in), name)
        for n, o in zip(names, outs):
            res[n] = o
        return tile

    taps = list(range(CONV_W))
    small(got[8], ["lru_conv_w", "lru_conv_b", "lru_b_rgate", "lru_b_igate", "lru_lambda"],
          [(taps, 0, TOK_WIDTH // N_DEV)] + [([4 + a], 0, TOK_WIDTH // N_DEV) for a in range(4)],
          "adamw_small_sharded")
    tile = small(got[11], ["ln_g", "ln_b", "mla_q_norm", "mla_kv_norm"],
                 [([0, 2], 0, D_MODEL), ([1, 3], 0, D_MODEL), ([4], 0, Q_LORA),
                  ([4], Q_LORA, KV_LORA)], "adamw_small_replicated")
    loss = tile[5, 0]

    result = [loss, gx.reshape(x.shape)]
    for kind in range(4):
        result += [res[n][kind].reshape(w_in[n].shape) for n in WEIGHT_ORDER]
    return tuple(result)
```

```python
import functools
import math

import jax
import jax.numpy as jnp
from jax import lax
from jax.experimental import pallas as pl
from jax.experimental.pallas import tpu as pltpu

F32 = jnp.float32
BF16 = jnp.bfloat16

D_MODEL = 1024
MEM_LEN = 256
HEAD_DIM = 64
N_TOK_HEADS = 12
N_MEM_HEADS = 4
TOK_WIDTH = 768
MEM_WIDTH = 256
MIX_WIDTH = 1024
Q_LORA = 384
KV_LORA = 256
QK_NOPE = 64
QK_ROPE = 32
QK_DIM = 96
ROPE_THETA = 10000.0
CONV_W = 4
LRU_C = 8.0
ALPHA = (2.0 * 2) ** 0.25
NORM_EPS = 1e-6
MLA_IN = 1952
LRU_IN = 2048
ADAM_LR = 0.001
ADAM_B1 = 0.9
ADAM_B2 = 0.999
ADAM_EPS = 1e-08
ADAM_WD = 0.01
ADAM_STEP = 10

N_DEV = 8
LANES = 128
SUBLANES = 8
HEAD_PAD = 128
QKV_PAD = N_TOK_HEADS * HEAD_PAD
ZP = 2048
ZA_W = TOK_WIDTH
ZG_W = MIX_WIDTH + MEM_WIDTH
ZA_CQ, ZA_CKV, ZA_KR = 0, 384, 640
KR_LANE = 64

ROW_BLOCK = 512
ATT_BLOCK = 512
LOOKAHEAD = 3
FWD_HEADS = 12
BWD_HEADS = 4
VMEM_LIMIT = 56 * 1024 * 1024
NEG_BIG = -1e30
STRIP = 32
SCAN_SEGMENTS = 4
LOG2E = math.log2(math.e)


def _cp(n_axes):
    return pltpu.CompilerParams(dimension_semantics=("arbitrary",) * n_axes,
                                vmem_limit_bytes=VMEM_LIMIT)


def _dot(a, b):
    return jnp.dot(a, b, preferred_element_type=F32)


def _dot_nt(a, b):
    return lax.dot_general(a, b, (((1,), (1,)), ((), ())), preferred_element_type=F32)


def _dot_tn(a, b):
    return lax.dot_general(a, b, (((0,), (0,)), ((), ())), preferred_element_type=F32)


def _sigmoid(t):
    return 1.0 / (1.0 + jnp.exp(-t))


def _lane(shape):
    return lax.broadcasted_iota(jnp.int32, shape, len(shape) - 1)


def _full(shape):
    nd = len(shape)
    return pl.BlockSpec(shape, lambda *_: (0,) * nd)


def _rows(ts, width, col=0):
    return pl.BlockSpec((ts, width), lambda i: (i, col))


def _heads(ts):
    return pl.BlockSpec((N_TOK_HEADS, ts, HEAD_PAD), lambda i: (0, i, 0))


def _rowmm(x, w, widths, name, ts):
    s, k = x.shape
    n = w.shape[1]
    offs = [sum(widths[:a]) for a in range(len(widths))]

    def body(x_ref, w_ref, *o_refs):
        res = _dot(x_ref[...].astype(BF16), w_ref[...])
        for o_ref, off, wd in zip(o_refs, offs, widths):
            o_ref[...] = res[:, off:off + wd]

    return pl.pallas_call(
        body, grid=(s // ts,),
        in_specs=[_rows(ts, k), _full((k, n))],
        out_specs=[_rows(ts, wd) for wd in widths],
        out_shape=[jax.ShapeDtypeStruct((s, wd), F32) for wd in widths],
        name=name, compiler_params=_cp(1))(x, w)


def _rms_parts(t):
    rs = lax.rsqrt(jnp.mean(t * t, axis=-1, keepdims=True) + NORM_EPS)
    return rs


def _rope_terms(c, sn):
    first_half = _lane(sn.shape) < KR_LANE + QK_ROPE // 2
    return c, jnp.where(first_half, -sn, 0.0), jnp.where(first_half, 0.0, sn)


def _rope(t, c, sa, sb):
    return t * c + pltpu.roll(t, LANES - 16, 1) * sa + pltpu.roll(t, 16, 1) * sb


def _rope_t(d, c, sa, sb):
    return d * c + pltpu.roll(d * sa, 16, 1) + pltpu.roll(d * sb, LANES - 16, 1)


def _mla_prep_fwd(x, win, tabs, gq, gkv, wuq, wukv, ts):
    s = x.shape[0]

    def body(x_ref, win_ref, c_ref, sn_ref, gq_ref, gkv_ref, wuq_ref, wukv_ref,
             z_ref, zg_ref, q_ref, k_ref, v_ref):
        zfull = _dot(x_ref[...].astype(BF16), win_ref[...])
        z_ref[...] = zfull[:, 0:ZA_W]
        zg_ref[...] = zfull[:, ZA_W:ZP]
        cq = zfull[:, ZA_CQ:ZA_CQ + Q_LORA]
        ckv = zfull[:, ZA_CKV:ZA_CKV + KV_LORA]
        kr = zfull[:, ZA_KR:ZA_KR + LANES]
        cqn = cq * _rms_parts(cq) * gq_ref[...]
        ckvn = ckv * _rms_parts(ckv) * gkv_ref[...]
        q = _dot(cqn.astype(BF16), wuq_ref[...])
        kv = _dot(ckvn.astype(BF16), wukv_ref[...])
        c, sa, sb = _rope_terms(c_ref[...], sn_ref[...])
        krope = _rope(kr, c, sa, sb)
        pad_lane = _lane((ts, HEAD_PAD)) >= HEAD_DIM
        for h in range(N_TOK_HEADS):
            sl = slice(h * HEAD_PAD, (h + 1) * HEAD_PAD)
            q_ref[h] = _rope(q[:, sl], c, sa, sb).astype(BF16)
            k_ref[h] = (kv[:, sl] + krope).astype(BF16)
            vh = kv[:, QKV_PAD + h * HEAD_PAD:QKV_PAD + (h + 1) * HEAD_PAD]
            v_ref[h] = jnp.where(pad_lane, 1.0, vh).astype(BF16)

    out = jax.ShapeDtypeStruct((N_TOK_HEADS, s, HEAD_PAD), BF16)
    return pl.pallas_call(
        body, grid=(s // ts,),
        in_specs=[_rows(ts, D_MODEL), _full((D_MODEL, ZP)),
                  _rows(ts, LANES), _rows(ts, LANES),
                  _full((1, Q_LORA)), _full((1, KV_LORA)),
                  _full((Q_LORA, QKV_PAD)), _full((KV_LORA, 2 * QKV_PAD))],
        out_specs=[_rows(ts, ZA_W), _rows(ts, ZG_W)] + [_heads(ts)] * 3,
        out_shape=[jax.ShapeDtypeStruct((s, ZA_W), F32), jax.ShapeDtypeStruct((s, ZG_W), F32),
                   out, out, out],
        name="mla_prep_fwd", compiler_params=_cp(1))(x, win, *tabs, gq, gkv, wuq, wukv)


def _mla_prep_bwd(z0, dq, dk, dv, tabs, gq, gkv, wuq_t, wukv_t, ts):
    s = z0.shape[0]

    def body(z_ref, dq_ref, dk_ref, dv_ref, c_ref, sn_ref, gq_ref, gkv_ref,
             wuqt_ref, wukvt_ref, dza_ref, dzk_ref, dwuq_ref, dwukv_ref, dg_ref):
        @pl.when(pl.program_id(0) == 0)
        def _():
            dwuq_ref[...] = jnp.zeros_like(dwuq_ref)
            dwukv_ref[...] = jnp.zeros_like(dwukv_ref)
            dg_ref[...] = jnp.zeros_like(dg_ref)

        cq = z_ref[:, ZA_CQ:ZA_CQ + Q_LORA]
        ckv = z_ref[:, ZA_CKV:ZA_CKV + KV_LORA]
        rq, rkv = _rms_parts(cq), _rms_parts(ckv)
        gq_, gkv_ = gq_ref[...], gkv_ref[...]
        cqn = (cq * rq * gq_).astype(BF16)
        ckvn = (ckv * rkv * gkv_).astype(BF16)
        c, sa, sb = _rope_terms(c_ref[...], sn_ref[...])
        dqp, dksum = [], None
        for h in range(N_TOK_HEADS):
            dqp.append(_rope_t(dq_ref[h], c, sa, sb))
            dksum = dk_ref[h] if dksum is None else dksum + dk_ref[h]
        dqp = jnp.concatenate(dqp, axis=1).astype(BF16)
        lane = _lane(dksum.shape)
        dzk_ref[...] = jnp.where((lane >= KR_LANE) & (lane < KR_LANE + QK_ROPE),
                                 _rope_t(dksum, c, sa, sb), 0.0).astype(BF16)
        dkv = jnp.concatenate([dk_ref[h].astype(BF16) for h in range(N_TOK_HEADS)]
                              + [dv_ref[h] for h in range(N_TOK_HEADS)], axis=1)
        dcqn = _dot(dqp, wuqt_ref[...])
        dckvn = _dot(dkv, wukvt_ref[...])
        dwuq_ref[...] += _dot_tn(cqn, dqp)
        dwukv_ref[...] += _dot_tn(ckvn, dkv)
        dg_ref[0:1, 0:Q_LORA] += jnp.sum(dcqn * cq * rq, axis=0, keepdims=True)
        dg_ref[0:1, Q_LORA:Q_LORA + KV_LORA] += jnp.sum(dckvn * ckv * rkv, axis=0, keepdims=True)
        wq = dcqn * gq_
        wkv = dckvn * gkv_
        dcq = rq * wq - cq * (rq * rq * rq) * jnp.mean(wq * cq, axis=-1, keepdims=True)
        dckv = rkv * wkv - ckv * (rkv * rkv * rkv) * jnp.mean(wkv * ckv, axis=-1, keepdims=True)
        dza_ref[:, 0:Q_LORA] = dcq.astype(BF16)
        dza_ref[:, Q_LORA:Q_LORA + KV_LORA] = dckv.astype(BF16)

    na = Q_LORA + KV_LORA
    return pl.pallas_call(
        body, grid=(s // ts,),
        in_specs=[_rows(ts, ZA_W), _heads(ts), _heads(ts), _heads(ts),
                  _rows(ts, LANES), _rows(ts, LANES),
                  _full((1, Q_LORA)), _full((1, KV_LORA)),
                  _full((QKV_PAD, Q_LORA)), _full((2 * QKV_PAD, KV_LORA))],
        out_specs=[_rows(ts, na), _rows(ts, LANES), _full((Q_LORA, QKV_PAD)),
                   _full((KV_LORA, 2 * QKV_PAD)), _full((SUBLANES, na))],
        out_shape=[jax.ShapeDtypeStruct((s, na), BF16), jax.ShapeDtypeStruct((s, LANES), BF16),
                   jax.ShapeDtypeStruct((Q_LORA, QKV_PAD), F32),
                   jax.ShapeDtypeStruct((KV_LORA, 2 * QKV_PAD), F32),
                   jax.ShapeDtypeStruct((SUBLANES, na), F32)],
        name="mla_prep_bwd", compiler_params=_cp(1))(
            z0, dq, dk, dv, *tabs, gq, gkv, wuq_t, wukv_t)


def _causal_pairs(nb, by_key):
    if by_key:
        pairs = [(i, j) for j in range(nb) for i in range(j, nb)]
    else:
        pairs = [(i, j) for i in range(nb) for j in range(i + 1)]
    return (jnp.array([p[0] for p in pairs], jnp.int32),
            jnp.array([p[1] for p in pairs], jnp.int32))


def _flash_fwd(q, k, v, t, nh):
    s = q.shape[1]
    itab, jtab = _causal_pairs(s // t, False)
    c2 = LOG2E / math.sqrt(QK_DIM)

    def body(it_ref, jt_ref, q_ref, k_ref, v_ref, o_ref, lse_ref, m_scr, acc_scr):
        pair = pl.program_id(1)
        i, j = it_ref[pair], jt_ref[pair]

        @pl.when(j == 0)
        def _():
            m_scr[...] = jnp.full_like(m_scr, NEG_BIG)
            acc_scr[...] = jnp.zeros_like(acc_scr)

        def softmax_strips(masked, hs, sc, row0):
            ps, als = [], []
            for r0 in range(0, sc.shape[0], STRIP):
                rows = slice(row0 + r0, row0 + r0 + STRIP)
                ch = [sc[r0:r0 + STRIP, n * LANES:(n + 1) * LANES] * c2
                      for n in range(sc.shape[1] // LANES)]
                if masked:
                    rr = row0 + r0 + lax.broadcasted_iota(jnp.int32, (STRIP, LANES), 0)
                    cc = lax.broadcasted_iota(jnp.int32, (STRIP, LANES), 1)
                    ch = [jnp.where(cc + n * LANES <= rr, c_, NEG_BIG) for n, c_ in enumerate(ch)]
                mx = ch[0]
                for c_ in ch[1:]:
                    mx = jnp.maximum(mx, c_)
                m_prev = m_scr[hs, rows, :]
                m_next = jnp.maximum(m_prev, jnp.max(mx, axis=-1, keepdims=True))
                ps.append(jnp.concatenate(
                    [jnp.exp2(c_ - m_next).astype(BF16) for c_ in ch], axis=1))
                als.append(jnp.exp2(m_prev - m_next))
                m_scr[hs, rows, :] = m_next
            return jnp.concatenate(ps, axis=0), jnp.concatenate(als, axis=0)

        def run(masked, parts):
            def scores_of(hs):
                return [_dot_nt(q_ref[hs, r0:r0 + nr, :], k_ref[hs, 0:nk, :])
                        for r0, nr, nk in parts]

            ahead = min(LOOKAHEAD, nh)
            scores = [scores_of(hs) for hs in range(ahead)]
            for hs in range(nh):
                if hs + ahead < nh:
                    scores.append(scores_of(hs + ahead))
                for (r0, nr, nk), sc in zip(parts, scores[hs]):
                    p, alpha = softmax_strips(masked, hs, sc, r0)
                    acc_scr[hs, r0:r0 + nr, :] = (alpha * acc_scr[hs, r0:r0 + nr, :]
                                                  + _dot(p, v_ref[hs, 0:nk, :]))

        @pl.when(j < i)
        def _():
            run(False, [(0, t, t)])

        @pl.when(j == i)
        def _():
            run(True, [(0, t, t)])
            for h in range(nh):
                acc = acc_scr[h]
                l = acc[:, HEAD_DIM:HEAD_DIM + 1]
                o_ref[h] = jnp.where(_lane(acc.shape) < HEAD_DIM, acc / l, 0.0)
                lse_ref[h] = m_scr[h] + jnp.log2(l)

    qspec = pl.BlockSpec((nh, t, HEAD_PAD), lambda h, p, it, jt: (h, it[p], 0))
    kspec = pl.BlockSpec((nh, t, HEAD_PAD), lambda h, p, it, jt: (h, jt[p], 0))
    out = jax.ShapeDtypeStruct((N_TOK_HEADS, s, HEAD_PAD), F32)
    return pl.pallas_call(
        body,
        grid_spec=pltpu.PrefetchScalarGridSpec(
            num_scalar_prefetch=2, grid=(N_TOK_HEADS // nh, itab.shape[0]),
            in_specs=[qspec, kspec, kspec], out_specs=[qspec, qspec],
            scratch_shapes=[pltpu.VMEM((nh, t, HEAD_PAD), F32)] * 2),
        out_shape=[out, out],
        name="flash_fwd", compiler_params=_cp(2))(itab, jtab, q, k, v)


def _flash_bwd(q, k, v, stats, do, t, nh, ex):
    s = q.shape[1]
    nb = s // t
    itab, jtab = _causal_pairs(nb, True)
    npairs = itab.shape[0]
    ngroups = N_TOK_HEADS // nh
    scale = 1.0 / math.sqrt(QK_DIM)
    c2 = LOG2E * scale
    nx = ex.n if ex is not None else 0
    ex_arrays, ex_out_shape, ex_scratch = (
        (ex.arrays, ex.out_shape, ex.scratch) if ex is not None else ([], [], []))

    def body(it_ref, jt_ref, q_ref, k_ref, v_ref, st_ref, do_ref, *rest):
        ex_in, rest = rest[:nx], rest[nx:]
        dq_ref, dk_ref, dv_ref = rest[:3]
        ex_out, rest = rest[3:3 + nx], rest[3 + nx:]
        dk_scr, dv_scr = rest[:2]
        ex_sems = rest[2:]
        pair = pl.program_id(1)
        i, j = it_ref[pair], jt_ref[pair]
        rows_i = pl.ds(pl.multiple_of(i * t, t), t)

        if nx:
            @pl.when(jnp.logical_and(pl.program_id(0) == 0, pair == 0))
            def _():
                for cp in ex.copies(ex_in, ex_out, ex_sems):
                    cp.start()

        @pl.when(i == j)
        def _():
            dk_scr[...] = jnp.zeros_like(dk_scr)
            dv_scr[...] = jnp.zeros_like(dv_scr)

        @pl.when(j == 0)
        def _():
            dq_ref[:, rows_i, :] = jnp.zeros((nh, t, HEAD_PAD), F32)

        def prob_strips(masked, h, sct, dpt, k0, q0):
            ps, dss = [], []
            for r0 in range(0, sct.shape[0], STRIP):
                rows = slice(r0, r0 + STRIP)
                if masked:
                    kk = k0 + r0 + lax.broadcasted_iota(jnp.int32, (STRIP, LANES), 0)
                    qq = q0 + lax.broadcasted_iota(jnp.int32, (STRIP, LANES), 1)
                pcs, dcs = [], []
                for n in range(sct.shape[1] // LANES):
                    cols = slice(n * LANES, (n + 1) * LANES)
                    qcols = slice(q0 + n * LANES, q0 + (n + 1) * LANES)
                    x = sct[rows, cols] * c2
                    if masked:
                        x = jnp.where(kk <= qq + n * LANES, x, NEG_BIG)
                    p = jnp.exp2(x - st_ref[h, 0:1, qcols])
                    pcs.append(p.astype(BF16))
                    dcs.append((p * (dpt[rows, cols] - st_ref[h, 1:2, qcols]) * scale).astype(BF16))
                ps.append(jnp.concatenate(pcs, axis=1))
                dss.append(jnp.concatenate(dcs, axis=1))
            return jnp.concatenate(ps, axis=0), jnp.concatenate(dss, axis=0)

        def run(masked, parts):
            def scores_of(h):
                return [(_dot_nt(k_ref[h, k0:k0 + nk, :], q_ref[h, q0:q0 + nq, :]),
                         _dot_nt(v_ref[h, k0:k0 + nk, :], do_ref[h, q0:q0 + nq, :]))
                        for k0, nk, q0, nq in parts]

            ahead = min(LOOKAHEAD, nh)
            scores = [scores_of(h) for h in range(ahead)]
            for h in range(nh):
                if h + ahead < nh:
                    scores.append(scores_of(h + ahead))
                for (k0, nk, q0, nq), (sct, dpt) in zip(parts, scores[h]):
                    pt, dst = prob_strips(masked, h, sct, dpt, k0, q0)
                    dv_scr[h, k0:k0 + nk, :] += _dot(pt, do_ref[h, q0:q0 + nq, :])
                    dk_scr[h, k0:k0 + nk, :] += _dot(dst, q_ref[h, q0:q0 + nq, :])
                    rows = pl.ds(pl.multiple_of(i * t + q0, t // 2), nq)
                    dq_ref[h, rows, :] += _dot_tn(dst, k_ref[h, k0:k0 + nk, :])

        @pl.when(i > j)
        def _():
            run(False, [(0, t, 0, t)])

        @pl.when(i == j)
        def _():
            run(True, [(0, t // 2, 0, t), (t // 2, t // 2, t // 2, t // 2)])

        @pl.when(i == nb - 1)
        def _():
            dk_ref[...] = dk_scr[...]
            dv_ref[...] = dv_scr[...].astype(BF16)

        if nx:
            @pl.when(jnp.logical_and(pl.program_id(0) == ngroups - 1, pair == npairs - 1))
            def _():
                for cp in ex.copies(ex_in, ex_out, ex_sems):
                    cp.wait()

    qspec = pl.BlockSpec((nh, t, HEAD_PAD), lambda h, p, it, jt: (h, it[p], 0))
    kspec = pl.BlockSpec((nh, t, HEAD_PAD), lambda h, p, it, jt: (h, jt[p], 0))
    dqspec = pl.BlockSpec((nh, s, HEAD_PAD), lambda h, p, it, jt: (h, 0, 0))
    stspec = pl.BlockSpec((nh, 2, t), lambda h, p, it, jt: (h, 0, it[p]))
    out = jax.ShapeDtypeStruct((N_TOK_HEADS, s, HEAD_PAD), F32)
    res = pl.pallas_call(
        body,
        grid_spec=pltpu.PrefetchScalarGridSpec(
            num_scalar_prefetch=2, grid=(ngroups, npairs),
            in_specs=[qspec, kspec, kspec, stspec, qspec] + [ANY] * nx,
            out_specs=[dqspec, kspec, kspec] + [ANY] * nx,
            scratch_shapes=[pltpu.VMEM((nh, t, HEAD_PAD), F32)] * 2 + ex_scratch),
        out_shape=[out, out, jax.ShapeDtypeStruct(out.shape, BF16)] + ex_out_shape,
        name="flash_bwd", compiler_params=_cp(2))(itab, jtab, q, k, v, stats, do, *ex_arrays)
    return res[:3], res[3:]


def _mem_probs(qp, kp, hh):
    lane = _lane(qp.shape)
    keep = (lane < HEAD_DIM) if hh == 0 else (lane >= HEAD_DIM)
    qh = jnp.where(keep, qp, 0.0).astype(BF16)
    sc = _dot_nt(qh, kp) * (1.0 / math.sqrt(HEAD_DIM))
    e = jnp.exp(sc - jnp.max(sc, axis=-1, keepdims=True))
    return e / jnp.sum(e, axis=-1, keepdims=True), keep


def _mix_out_fwd(tok, z, memkv, w, h, g, b, tgt, g0, q0, padded, name, ts):
    s = z.shape[0]
    zw = z.shape[1]
    tok_spec = _heads(ts) if padded else _rows(ts, TOK_WIDTH)
    with_loss = tgt is not None

    def body(*refs):
        tok_ref, z_ref, mkv_ref, w_ref, h_ref, g_ref, b_ref = refs[:7]
        if with_loss:
            t_ref, cat_ref, y_ref, dpre_ref, dgb_ref, loss_ref = refs[7:]
        else:
            cat_ref, y_ref, pre_ref, out_ref = refs[7:]
        if padded:
            for p in range(N_TOK_HEADS // 2):
                cat_ref[:, p * LANES:(p + 1) * LANES] = (
                    tok_ref[2 * p] + pltpu.roll(tok_ref[2 * p + 1], HEAD_DIM, 1))
        else:
            cat_ref[:, 0:TOK_WIDTH] = tok_ref[...]
        for pr in range(N_MEM_HEADS // 2):
            sl = slice(pr * LANES, (pr + 1) * LANES)
            qp = z_ref[:, q0 + pr * LANES:q0 + (pr + 1) * LANES]
            kp = mkv_ref[:, sl].astype(BF16)
            vp = mkv_ref[:, MEM_WIDTH + pr * LANES:MEM_WIDTH + (pr + 1) * LANES].astype(BF16)
            outs = []
            for hh in range(2):
                p, _ = _mem_probs(qp, kp, hh)
                outs.append(_dot(p.astype(BF16), vp))
            lane = _lane(outs[0].shape)
            cat_ref[:, TOK_WIDTH + pr * LANES:TOK_WIDTH + (pr + 1) * LANES] = jnp.where(
                lane < HEAD_DIM, outs[0], outs[1])
        gate = z_ref[:, g0:g0 + MIX_WIDTH]
        yb = (cat_ref[...] * (gate * _sigmoid(gate))).astype(BF16)
        y_ref[...] = yb
        pre = ALPHA * h_ref[...] + _dot(yb, w_ref[...])
        xhat, rstd = _ln_stats(pre)
        hout = xhat * g_ref[...] + b_ref[...]
        if with_loss:
            @pl.when(pl.program_id(0) == 0)
            def _():
                loss_ref[...] = jnp.zeros_like(loss_ref)
                dgb_ref[...] = jnp.zeros_like(dgb_ref)
            err = hout - t_ref[...]
            loss_ref[...] += 0.5 * jnp.sum(jnp.mean(err * err, axis=-1, keepdims=True))
            dh = err * (1.0 / D_MODEL)
            dpre_ref[...] = _ln_bwd(dh, xhat, rstd, g_ref[...])
            dgb_ref[0:1, :] += jnp.sum(dh * xhat, axis=0, keepdims=True)
            dgb_ref[1:2, :] += jnp.sum(dh, axis=0, keepdims=True)
        else:
            pre_ref[...] = pre
            out_ref[...] = hout

    act = jax.ShapeDtypeStruct((s, D_MODEL), F32)
    in_specs = [tok_spec, _rows(ts, zw), _full((MEM_LEN, 2 * MEM_WIDTH)),
                _full((MIX_WIDTH, D_MODEL)), _rows(ts, D_MODEL),
                _full((1, D_MODEL)), _full((1, D_MODEL))]
    out_specs = [_rows(ts, MIX_WIDTH)] * 2
    out_shape = [jax.ShapeDtypeStruct((s, MIX_WIDTH), F32),
                 jax.ShapeDtypeStruct((s, MIX_WIDTH), BF16)]
    args = [tok, z, memkv, w, h, g, b]
    if with_loss:
        in_specs.append(_rows(ts, D_MODEL))
        out_specs += [_rows(ts, D_MODEL), _full((SUBLANES, D_MODEL)), _full((SUBLANES, LANES))]
        out_shape += [act, jax.ShapeDtypeStruct((SUBLANES, D_MODEL), F32),
                      jax.ShapeDtypeStruct((SUBLANES, LANES), F32)]
        args.append(tgt)
    else:
        out_specs += [_rows(ts, D_MODEL)] * 2
        out_shape += [act, act]
    return pl.pallas_call(
        body, grid=(s // ts,), in_specs=in_specs, out_specs=out_specs, out_shape=out_shape,
        name=name, compiler_params=_cp(1))(*args)


def _gate_mem_bwd(dpre, y, w_t, cat, z, memkv, lse, g0, q0, name, ts):
    s = z.shape[0]
    zw = z.shape[1]
    padded = lse is not None
    gq_w = MIX_WIDTH + MEM_WIDTH

    def body(*refs):
        if padded:
            (dpre_ref, y_ref, wt_ref, cat_ref, z_ref, mkv_ref, lse_ref,
             dzg_ref, dtok_ref, dmkv_ref, dw_ref, st_ref) = refs
        else:
            (dpre_ref, y_ref, wt_ref, cat_ref, z_ref, mkv_ref,
             dzg_ref, dtok_ref, dmkv_ref, dw_ref) = refs

        @pl.when(pl.program_id(0) == 0)
        def _():
            dmkv_ref[...] = jnp.zeros_like(dmkv_ref)
            dw_ref[...] = jnp.zeros_like(dw_ref)

        dpb = dpre_ref[...].astype(BF16)
        dy_ = _dot(dpb, wt_ref[...])
        dw_ref[...] += _dot_tn(y_ref[...], dpb)
        gate = z_ref[:, g0:g0 + MIX_WIDTH]
        sg = _sigmoid(gate)
        dzg_ref[:, 0:MIX_WIDTH] = (dy_ * cat_ref[...]
                                   * (sg * (1.0 + gate * (1.0 - sg)))).astype(BF16)
        dcat = dy_ * (gate * sg)
        if padded:
            low = _lane((ts, LANES)) < HEAD_DIM
            for p in range(N_TOK_HEADS // 2):
                d = dcat[:, p * LANES:(p + 1) * LANES]
                prod = d * cat_ref[:, p * LANES:(p + 1) * LANES]
                first = jnp.sum(jnp.where(low, prod, 0.0), axis=-1, keepdims=True)
                second = jnp.sum(jnp.where(low, 0.0, prod), axis=-1, keepdims=True)
                dtok_ref[2 * p] = jnp.where(low, d, 0.0).astype(BF16)
                dtok_ref[2 * p + 1] = jnp.where(low, pltpu.roll(d, HEAD_DIM, 1), 0.0).astype(BF16)
                for hh, delta in ((2 * p, first), (2 * p + 1, second)):
                    both = jnp.where(low, lse_ref[hh], delta).T
                    st_ref[hh, 0:1, :] = both[0:1, :]
                    st_ref[hh, 1:2, :] = both[HEAD_DIM:HEAD_DIM + 1, :]
        else:
            dtok_ref[...] = dcat[:, 0:TOK_WIDTH]
        for pr in range(N_MEM_HEADS // 2):
            sl = slice(pr * LANES, (pr + 1) * LANES)
            vsl = slice(MEM_WIDTH + pr * LANES, MEM_WIDTH + (pr + 1) * LANES)
            qp = z_ref[:, q0 + pr * LANES:q0 + (pr + 1) * LANES]
            qpb = qp.astype(BF16)
            kp = mkv_ref[:, sl].astype(BF16)
            vp = mkv_ref[:, vsl].astype(BF16)
            dmo = dcat[:, TOK_WIDTH + pr * LANES:TOK_WIDTH + (pr + 1) * LANES]
            dqp = None
            for hh in range(2):
                p, keep = _mem_probs(qp, kp, hh)
                do_h = jnp.where(keep, dmo, 0.0).astype(BF16)
                dmkv_ref[:, vsl] += _dot_tn(p.astype(BF16), do_h)
                dp = _dot_nt(do_h, vp)
                ds = (p * (dp - jnp.sum(dp * p, axis=-1, keepdims=True))
                      * (1.0 / math.sqrt(HEAD_DIM))).astype(BF16)
                dqh = jnp.where(keep, _dot(ds, kp), 0.0)
                dqp = dqh if dqp is None else dqp + dqh
                dkh = _dot_tn(ds, qpb)
                klane = _lane(dkh.shape)
                kkeep = (klane < HEAD_DIM) if hh == 0 else (klane >= HEAD_DIM)
                dmkv_ref[:, sl] += jnp.where(kkeep, dkh, 0.0)
            dzg_ref[:, MIX_WIDTH + pr * LANES:MIX_WIDTH + (pr + 1) * LANES] = dqp.astype(BF16)

    in_specs = [_rows(ts, D_MODEL), _rows(ts, MIX_WIDTH), _full((D_MODEL, MIX_WIDTH)),
                _rows(ts, MIX_WIDTH), _rows(ts, zw), _full((MEM_LEN, 2 * MEM_WIDTH))]
    out_specs = [_rows(ts, gq_w), _heads(ts) if padded else _rows(ts, TOK_WIDTH),
                 _full((MEM_LEN, 2 * MEM_WIDTH)), _full((MIX_WIDTH, D_MODEL))]
    heads_shape = (N_TOK_HEADS, s, HEAD_PAD)
    out_shape = [jax.ShapeDtypeStruct((s, gq_w), BF16),
                 jax.ShapeDtypeStruct(heads_shape, BF16) if padded
                 else jax.ShapeDtypeStruct((s, TOK_WIDTH), F32),
                 jax.ShapeDtypeStruct((MEM_LEN, 2 * MEM_WIDTH), F32),
                 jax.ShapeDtypeStruct((MIX_WIDTH, D_MODEL), F32)]
    args = [dpre, y, w_t, cat, z, memkv]
    if padded:
        in_specs.append(_heads(ts))
        out_specs.append(pl.BlockSpec((N_TOK_HEADS, 2, ts), lambda i: (0, 0, i)))
        out_shape.append(jax.ShapeDtypeStruct((N_TOK_HEADS, 2, s), F32))
        args.append(lse)
    return pl.pallas_call(
        body, grid=(s // ts,), in_specs=in_specs, out_specs=out_specs, out_shape=out_shape,
        name=name, compiler_params=_cp(1))(*args)


def _ln_stats(pre):
    mu = jnp.mean(pre, axis=-1, keepdims=True)
    d = pre - mu
    rstd = lax.rsqrt(jnp.mean(d * d, axis=-1, keepdims=True) + NORM_EPS)
    return d * rstd, rstd


def _ln_bwd(dh, xhat, rstd, g):
    dxh = dh * g
    return rstd * (dxh - jnp.mean(dxh, axis=-1, keepdims=True)
                   - xhat * jnp.mean(dxh * xhat, axis=-1, keepdims=True))


def _linear_bwd(x, dys, offs, w_t, resid, ln, name, ts):
    s, kdim = x.shape
    n = w_t.shape[0]
    widths = [d.shape[1] for d in dys]
    npieces = len(dys)
    with_ln = ln is not None

    def body(*refs):
        x_ref = refs[0]
        dy_refs = refs[1:1 + npieces]
        if with_ln:
            wt_ref, r_ref, pre_ref, g_ref, dx_ref, dw_ref, dgb_ref = refs[1 + npieces:]
        else:
            wt_ref, r_ref, dx_ref, dw_ref = refs[1 + npieces:]

        @pl.when(pl.program_id(0) == 0)
        def _():
            dw_ref[...] = jnp.zeros_like(dw_ref)
            if with_ln:
                dgb_ref[...] = jnp.zeros_like(dgb_ref)

        xb = x_ref[...].astype(BF16)
        dx = ALPHA * r_ref[...]
        for dy_ref, off, wd in zip(dy_refs, offs, widths):
            dyb = dy_ref[...].astype(BF16)
            dx = dx + _dot(dyb, wt_ref[off:off + wd, :])
            dw_ref[:, off:off + wd] += _dot_tn(xb, dyb)
        if with_ln:
            xhat, rstd = _ln_stats(pre_ref[...])
            dx_ref[...] = _ln_bwd(dx, xhat, rstd, g_ref[...])
            dgb_ref[0:1, :] += jnp.sum(dx * xhat, axis=0, keepdims=True)
            dgb_ref[1:2, :] += jnp.sum(dx, axis=0, keepdims=True)
        else:
            dx_ref[...] = dx

    in_specs = ([_rows(ts, kdim)] + [_rows(ts, wd) for wd in widths]
                + [_full((n, kdim)), _rows(ts, kdim)])
    out_specs = [_rows(ts, kdim), _full((kdim, n))]
    out_shape = [jax.ShapeDtypeStruct((s, kdim), F32), jax.ShapeDtypeStruct((kdim, n), F32)]
    args = [x, *dys, w_t, resid]
    if with_ln:
        in_specs += [_rows(ts, kdim), _full((1, kdim))]
        out_specs.append(_full((SUBLANES, kdim)))
        out_shape.append(jax.ShapeDtypeStruct((SUBLANES, kdim), F32))
        args += list(ln)
    return pl.pallas_call(
        body, grid=(s // ts,), in_specs=in_specs, out_specs=out_specs, out_shape=out_shape,
        name=name, compiler_params=_cp(1))(*args)


def _wgrad_small(x, dy, name):
    def body(x_ref, dy_ref, dw_ref):
        dw_ref[...] = _dot_tn(x_ref[...].astype(BF16), dy_ref[...].astype(BF16))

    return pl.pallas_call(
        body, out_shape=jax.ShapeDtypeStruct((x.shape[1], dy.shape[1]), F32),
        name=name, compiler_params=pltpu.CompilerParams(vmem_limit_bytes=VMEM_LIMIT))(x, dy)


def _shift_down(u, carry8, k):
    if k == 0:
        return u
    rolled = pltpu.roll(u, k, 0)
    row = lax.broadcasted_iota(jnp.int32, carry8.shape, 0)
    top = jnp.where(row < k, pltpu.roll(carry8, k, 0), rolled[0:SUBLANES])
    return jnp.concatenate([top, rolled[SUBLANES:]], axis=0)


def _shift_up(u, carry8, k):
    if k == 0:
        return u
    n = u.shape[0]
    rolled = pltpu.roll(u, n - k, 0)
    row = lax.broadcasted_iota(jnp.int32, carry8.shape, 0)
    bot = jnp.where(row >= SUBLANES - k, pltpu.roll(carry8, SUBLANES - k, 0),
                    rolled[n - SUBLANES:])
    return jnp.concatenate([rolled[:n - SUBLANES], bot], axis=0)


def _neg_expm1(t):
    e = jnp.exp(t)
    em1 = e - 1.0
    safe = jnp.where(e == 1.0, 1.0, jnp.log(e))
    return -jnp.where(e == 1.0, t, jnp.where(em1 == -1.0, -1.0, em1 * t / safe))


def _lru_gates(u, carry8, cw_ref, vec_ref, wr_ref, wi_ref):
    taps = [_shift_down(u, carry8, k) for k in range(CONV_W)]
    xc = vec_ref[0:1, :] + cw_ref[3:4, :] * u
    for k in range(1, CONV_W):
        xc = xc + cw_ref[3 - k:4 - k, :] * taps[k]
    xb = xc.astype(BF16)
    r = _sigmoid(_dot(xb, wr_ref[...]) + vec_ref[1:2, :])
    ig = _sigmoid(_dot(xb, wi_ref[...]) + vec_ref[2:3, :])
    nlam = -vec_ref[3:4, :]
    e = jnp.exp(-jnp.abs(nlam))
    e1 = 1.0 + e
    log1p = jnp.where(e1 == 1.0, e, jnp.log(e1) * e / jnp.where(e1 == 1.0, 1.0, e1 - 1.0))
    softplus = jnp.maximum(nlam, 0.0) + log1p
    cneg = -LRU_C * softplus
    log_a = cneg * r
    a = jnp.exp(log_a)
    sq = jnp.sqrt(_neg_expm1(2.0 * log_a))
    return xc, r, ig, cneg, a, sq, taps


def _chained_scan(a_ref, b_ref, out_ref, cum_scr, x_in):
    rows_total, w = a_ref.shape
    nseg = SCAN_SEGMENTS
    seg = rows_total // nseg

    def step(t, carry):
        xs, ps = carry
        new_x, new_p = [], []
        for sg in range(nseg):
            row = pl.ds(sg * seg + t, 1)
            a = a_ref[row, :]
            x = a * xs[sg] + b_ref[row, :]
            out_ref[row, :] = x
            new_x.append(x)
            if sg > 0:
                p = a * ps[sg - 1]
                cum_scr[row, :] = p
                new_p.append(p)
        return tuple(new_x), tuple(new_p)

    zero, one = jnp.zeros((1, w), F32), jnp.ones((1, w), F32)
    xs, _ = lax.fori_loop(0, seg, step, ((x_in,) + (zero,) * (nseg - 1), (one,) * (nseg - 1)))
    x_prev = xs[0]
    for sg in range(1, nseg):
        rows = slice(sg * seg, (sg + 1) * seg)
        out_ref[rows, :] = out_ref[rows, :] + cum_scr[rows, :] * x_prev
        x_prev = out_ref[(sg + 1) * seg - 1:(sg + 1) * seg, :]
    return x_prev


def _lru_fwd(x, win, cw8, vec8, wr, wi, ts):
    s = x.shape[0]

    def body(x_ref, win_ref, cw_ref, vec_ref, wr_ref, wi_ref, u_ref, zg_ref, hs_ref,
             cu_scr, ch_scr, a_scr, gx_scr, cum_scr):
        @pl.when(pl.program_id(0) == 0)
        def _():
            cu_scr[...] = jnp.zeros_like(cu_scr)
            ch_scr[...] = jnp.zeros_like(ch_scr)

        zfull = _dot(x_ref[...].astype(BF16), win_ref[...])
        u = zfull[:, 0:ZA_W]
        u_ref[...] = u
        zg_ref[...] = zfull[:, ZA_W:ZP]
        xc, _, ig, _, a, sq, _ = _lru_gates(u, cu_scr[...], cw_ref, vec_ref, wr_ref, wi_ref)
        a_scr[...] = a
        gx_scr[...] = sq * (ig * xc)
        ch_scr[0:1, :] = _chained_scan(a_scr, gx_scr, hs_ref, cum_scr, ch_scr[0:1, :])
        cu_scr[...] = u[ts - SUBLANES:, :]

    w = TOK_WIDTH
    return pl.pallas_call(
        body, grid=(s // ts,),
        in_specs=[_rows(ts, D_MODEL), _full((D_MODEL, ZP)),
                  _full((SUBLANES, w)), _full((SUBLANES, w)), _full((w, w)), _full((w, w))],
        out_specs=[_rows(ts, w), _rows(ts, ZG_W), _rows(ts, w)],
        out_shape=[jax.ShapeDtypeStruct((s, w), F32), jax.ShapeDtypeStruct((s, ZG_W), F32),
                   jax.ShapeDtypeStruct((s, w), F32)],
        scratch_shapes=[pltpu.VMEM((SUBLANES, w), F32), pltpu.VMEM((SUBLANES, w), F32)]
                       + [pltpu.VMEM((ts, w), F32)] * 3,
        name="lru_fwd", compiler_params=_cp(1))(x, win, cw8, vec8, wr, wi)


def _lru_bwd(z1, dhs, hs, cw8, vec8, wr, wi, wr_t, wi_t, ts):
    s = z1.shape[0]
    nb = s // ts
    w = TOK_WIDTH
    tiles = ts // SUBLANES

    def body(u_ref, up_ref, dhs_ref, hs_ref, hsp_ref, cw_ref, vec_ref, wr_ref, wi_ref,
             wrt_ref, wit_ref, du_ref, dwr_ref, dwi_ref, dvec_ref,
             cc_scr, cd_scr, a_scr, dh_scr):
        i = pl.program_id(0)

        @pl.when(i == 0)
        def _():
            cc_scr[...] = jnp.zeros_like(cc_scr)
            cd_scr[...] = jnp.zeros_like(cd_scr)
            dwr_ref[...] = jnp.zeros_like(dwr_ref)
            dwi_ref[...] = jnp.zeros_like(dwi_ref)
            dvec_ref[...] = jnp.zeros_like(dvec_ref)

        u = u_ref[...]
        first = i == nb - 1
        carry8 = jnp.where(first, 0.0, up_ref[...])
        xc, r, ig, cneg, a, sq, taps = _lru_gates(u, carry8, cw_ref, vec_ref, wr_ref, wi_ref)
        a_scr[...] = a

        def step(n, c):
            t = ts - 1 - n
            dh = dhs_ref[pl.ds(t, 1), :] + c
            dh_scr[pl.ds(t, 1), :] = dh
            return a_scr[pl.ds(t, 1), :] * dh

        cc_scr[0:1, :] = lax.fori_loop(0, ts, step, cc_scr[0:1, :])
        dh = dh_scr[...]
        hprev = _shift_down(hs_ref[...], jnp.where(first, 0.0, hsp_ref[...]), 1)
        ix = ig * xc
        dix = dh * sq
        dlog_a = dh * hprev * a - (dh * ix) * (a * a) / sq
        dpr = (dlog_a * cneg) * r * (1.0 - r)
        dpi = (dix * xc) * ig * (1.0 - ig)
        dprb, dpib = dpr.astype(BF16), dpi.astype(BF16)
        xb = xc.astype(BF16)
        dwr_ref[...] += _dot_tn(xb, dprb)
        dwi_ref[...] += _dot_tn(xb, dpib)
        dxc = dix * ig + _dot(dprb, wrt_ref[...]) + _dot(dpib, wit_ref[...])
        for k in range(CONV_W):
            dvec_ref[3 - k:4 - k, :] += jnp.sum(dxc * taps[k], axis=0, keepdims=True)
        dvec_ref[4:5, :] += jnp.sum(dxc, axis=0, keepdims=True)
        dvec_ref[5:6, :] += jnp.sum(dpr, axis=0, keepdims=True)
        dvec_ref[6:7, :] += jnp.sum(dpi, axis=0, keepdims=True)
        dvec_ref[7:8, :] += (jnp.sum(dlog_a * r, axis=0, keepdims=True)
                             * (LRU_C * _sigmoid(-vec_ref[3:4, :])))
        nxt = cd_scr[...]
        du = cw_ref[3:4, :] * dxc
        for k in range(1, CONV_W):
            du = du + cw_ref[3 - k:4 - k, :] * _shift_up(dxc, nxt, k)
        du_ref[...] = du.astype(BF16)
        cd_scr[...] = dxc[0:SUBLANES, :]

    rev = lambda i: (nb - 1 - i, 0)
    prev8 = lambda i: (jnp.maximum((nb - 1 - i) * tiles - 1, 0), 0)
    blk = pl.BlockSpec((ts, w), rev)
    before = pl.BlockSpec((SUBLANES, w), prev8)
    scr = pltpu.VMEM((ts, w), F32)
    return pl.pallas_call(
        body, grid=(nb,),
        in_specs=[blk, before, blk, blk, before,
                  _full((SUBLANES, w)), _full((SUBLANES, w)),
                  _full((w, w)), _full((w, w)), _full((w, w)), _full((w, w))],
        out_specs=[blk, _full((w, w)), _full((w, w)), _full((SUBLANES, w))],
        out_shape=[jax.ShapeDtypeStruct((s, w), BF16), jax.ShapeDtypeStruct((w, w), F32),
                   jax.ShapeDtypeStruct((w, w), F32), jax.ShapeDtypeStruct((SUBLANES, w), F32)],
        scratch_shapes=[pltpu.VMEM((SUBLANES, w), F32), pltpu.VMEM((SUBLANES, w), F32),
                        scr, scr],
        name="lru_bwd", compiler_params=_cp(1))(
            z1, z1, dhs, hs, hs, cw8, vec8, wr, wi, wr_t, wi_t)


def _adamw(parts, w, m, v, name):
    n = len(parts)
    rows_per = parts[0].shape[1]

    def body(*refs):
        p_refs = refs[:n]
        w_ref, m_ref, v_ref, g_ref, d_ref, nm_ref, nv_ref = refs[n:]
        for l, p_ref in enumerate(p_refs):
            rows = slice(l * rows_per, (l + 1) * rows_per)
            g = p_ref[0].astype(F32)
            for dev in range(1, N_DEV):
                g = g + p_ref[dev].astype(F32)
            g_ref[rows, :] = g
            nm = ADAM_B1 * m_ref[rows, :] + (1.0 - ADAM_B1) * g
            nv = ADAM_B2 * v_ref[rows, :] + (1.0 - ADAM_B2) * (g * g)
            m_hat = nm / (1.0 - ADAM_B1 ** ADAM_STEP)
            v_hat = nv / (1.0 - ADAM_B2 ** ADAM_STEP)
            d_ref[rows, :] = -ADAM_LR * (m_hat / (jnp.sqrt(v_hat) + ADAM_EPS)
                                         + ADAM_WD * w_ref[rows, :])
            nm_ref[rows, :] = nm
            nv_ref[rows, :] = nv

    out = jax.ShapeDtypeStruct(w.shape, F32)
    return pl.pallas_call(
        body, out_shape=[out] * 4, name=name,
        compiler_params=pltpu.CompilerParams(vmem_limit_bytes=VMEM_LIMIT))(*parts, w, m, v)


def _adam_update(g, w, m, v):
    nm = ADAM_B1 * m + (1.0 - ADAM_B1) * g
    nv = ADAM_B2 * v + (1.0 - ADAM_B2) * (g * g)
    m_hat = nm / (1.0 - ADAM_B1 ** ADAM_STEP)
    v_hat = nv / (1.0 - ADAM_B2 ** ADAM_STEP)
    return -ADAM_LR * (m_hat / (jnp.sqrt(v_hat) + ADAM_EPS) + ADAM_WD * w), nm, nv


def _adamw_small(parts, layout, ws, ms, vs, name):
    n = len(layout)

    def body(*refs):
        p_ref = refs[0]
        w_refs, m_refs, v_refs = refs[1:1 + n], refs[1 + n:1 + 2 * n], refs[1 + 2 * n:1 + 3 * n]
        tile_ref = refs[1 + 3 * n]
        outs = refs[2 + 3 * n:]
        tile = p_ref[0]
        for dev in range(1, N_DEV):
            tile = tile + p_ref[dev]
        tile_ref[...] = tile
        for p, (rows, c0, nc) in enumerate(layout):
            for r, src in enumerate(rows):
                g = tile_ref[src:src + 1, c0:c0 + nc]
                d, nm, nv = _adam_update(g, w_refs[p][r:r + 1, :], m_refs[p][r:r + 1, :],
                                         v_refs[p][r:r + 1, :])
                for kind, val in enumerate((g, d, nm, nv)):
                    outs[4 * p + kind][r:r + 1, :] = val

    out_shape = [jax.ShapeDtypeStruct(parts.shape[1:], F32)]
    for w in ws:
        out_shape += [jax.ShapeDtypeStruct(w.shape, F32)] * 4
    res = pl.pallas_call(
        body, out_shape=out_shape, name=name,
        compiler_params=pltpu.CompilerParams(vmem_limit_bytes=VMEM_LIMIT))(parts, *ws, *ms, *vs)
    return res[0], [res[1 + 4 * p:5 + 4 * p] for p in range(n)]


ANY = pl.BlockSpec(memory_space=pl.ANY)
MESH = pl.DeviceIdType.MESH


def _slot(p):
    return 4 * p[0] + 2 * p[1] + p[2]


def _allgather(xs):
    n = len(xs)

    def body(*refs):
        x_refs, o_refs = refs[:n], refs[n:2 * n]
        send_sems, recv_sems, local_sems = refs[2 * n:]
        x, y, c = lax.axis_index("x"), lax.axis_index("y"), lax.axis_index("c")
        me, sibling = (x, y, c), (x, y, 1 - c)
        chips = [(1 - x, y), (x, 1 - y), (1 - x, 1 - y)]

        def copy(a, k, block, to, from_input=False):
            dst = o_refs[a].at[_slot(block)]
            return pltpu.make_async_remote_copy(
                src_ref=x_refs[a] if from_input else dst, dst_ref=dst,
                send_sem=send_sems.at[a, k], recv_sem=recv_sems.at[a, k],
                device_id=to, device_id_type=MESH)

        mine = [pltpu.make_async_copy(x_refs[a], o_refs[a].at[_slot(me)], local_sems.at[a])
                for a in range(n)]
        for cp in mine:
            cp.start()
        first = []
        for a in range(n):
            first.append(copy(a, 0, me, sibling, True))
            first += [copy(a, 1 + j, me, (*chip, c), True) for j, chip in enumerate(chips)]
        for cp in first:
            cp.start()
        passed = []
        for j, chip in enumerate(chips):
            for a in range(n):
                copy(a, 1 + j, (*chip, c), me).wait_recv()
                cp = copy(a, 4 + j, (*chip, c), sibling)
                cp.start()
                passed.append(cp)
        for a in range(n):
            copy(a, 0, sibling, me).wait_recv()
            for j, chip in enumerate(chips):
                copy(a, 4 + j, (*chip, 1 - c), me).wait_recv()
        for cp in first + passed:
            cp.wait_send()
        for cp in mine:
            cp.wait()

    return pl.pallas_call(
        body,
        out_shape=[jax.ShapeDtypeStruct((N_DEV,) + t.shape, t.dtype) for t in xs],
        in_specs=[ANY] * n, out_specs=[ANY] * n,
        scratch_shapes=[pltpu.SemaphoreType.DMA((n, 7)), pltpu.SemaphoreType.DMA((n, 7)),
                        pltpu.SemaphoreType.DMA((n,))],
        name="allgather_weights")(*xs)


class _Exchange:
    def __init__(self, arrays, kinds):
        self.arrays, self.kinds, self.n = list(arrays), list(kinds), len(arrays)
        self.shapes = [self._part_shape(a, k) for a, k in zip(arrays, kinds)]
        self.out_shape = [jax.ShapeDtypeStruct((N_DEV,) + shp, a.dtype)
                          for shp, a in zip(self.shapes, arrays)]
        self.scratch = [pltpu.SemaphoreType.DMA((self.n, N_DEV - 1)),
                        pltpu.SemaphoreType.DMA((self.n, N_DEV - 1)),
                        pltpu.SemaphoreType.DMA((self.n,))]

    @staticmethod
    def _part_shape(arr, kind):
        if kind == "chunks":
            return arr.shape[1:]
        if kind == "cols":
            return (arr.shape[0], arr.shape[1] // N_DEV)
        if kind == "rows":
            return (arr.shape[0] // N_DEV, arr.shape[1])
        return arr.shape

    def copies(self, in_refs, out_refs, sems):
        send_sems, recv_sems, local_sems = sems
        x, y, c = lax.axis_index("x"), lax.axis_index("y"), lax.axis_index("c")
        me = _slot((x, y, c))

        def part(a, dev):
            ref, kind, shp = in_refs[a], self.kinds[a], self.shapes[a]
            if kind == "chunks":
                return ref.at[dev]
            if kind == "cols":
                return ref.at[:, pl.ds(pl.multiple_of(dev * shp[1], LANES), shp[1])]
            if kind == "rows":
                return ref.at[pl.ds(pl.multiple_of(dev * shp[0], SUBLANES), shp[0]), :]
            return ref

        cps = [pltpu.make_async_copy(part(a, me), out_refs[a].at[me], local_sems.at[a])
               for a in range(self.n)]
        for rel in range(1, N_DEV):
            peer = (x ^ (rel >> 2), y ^ ((rel >> 1) & 1), c ^ (rel & 1))
            for a in range(self.n):
                cps.append(pltpu.make_async_remote_copy(
                    src_ref=part(a, _slot(peer)), dst_ref=out_refs[a].at[me],
                    send_sem=send_sems.at[a, rel - 1], recv_sem=recv_sems.at[a, rel - 1],
                    device_id=peer, device_id_type=MESH))
        return cps


def _exchange_grads(arrays, kinds, name):
    ex = _Exchange(arrays, kinds)
    n = ex.n

    def body(*refs):
        cps = ex.copies(refs[:n], refs[n:2 * n], refs[2 * n:])
        for cp in cps:
            cp.start()
        for cp in cps:
            cp.wait()

    return pl.pallas_call(
        body, out_shape=ex.out_shape, in_specs=[ANY] * n, out_specs=[ANY] * n,
        scratch_shapes=ex.scratch, name=name)(*arrays)


BIG = [("mla_w_in", (D_MODEL, MLA_IN), 1), ("mla_w_uq", (Q_LORA, N_TOK_HEADS * QK_DIM), 1),
       ("mla_w_ukv", (KV_LORA, N_TOK_HEADS * 2 * HEAD_DIM), 1), ("lru_w_in", (D_MODEL, LRU_IN), 1),
       ("w_mem_kv", (2, D_MODEL, 2 * MEM_WIDTH), 1), ("w_out", (2, MIX_WIDTH, D_MODEL), 1)]
SMALL = [("lru_conv_w", (CONV_W, TOK_WIDTH), 1), ("lru_conv_b", (TOK_WIDTH,), 0),
         ("lru_b_rgate", (TOK_WIDTH,), 0), ("lru_b_igate", (TOK_WIDTH,), 0),
         ("lru_lambda", (TOK_WIDTH,), 0)]
REPL = [("mla_q_norm", (Q_LORA,)), ("mla_kv_norm", (KV_LORA,)),
        ("lru_w_rgate", (N_TOK_HEADS, HEAD_DIM, HEAD_DIM)),
        ("lru_w_igate", (N_TOK_HEADS, HEAD_DIM, HEAD_DIM)),
        ("ln_g", (2, D_MODEL)), ("ln_b", (2, D_MODEL))]


def _shard_shape(shape, axis):
    return tuple(d // N_DEV if a == axis else d for a, d in enumerate(shape))


def _size(shape):
    return math.prod(shape)


N_OWN_SHAPE = 4
SMALL_ROWS = SUBLANES


def _pack_rows(flat_parts, rows):
    flat = jnp.concatenate([p.reshape(-1) for p in flat_parts])
    return jnp.pad(flat, (0, rows * LANES - flat.shape[0])).reshape(rows, LANES)


def _to_chunks(full, axis):
    shape = full.shape
    split = shape[:axis] + (N_DEV, shape[axis] // N_DEV) + shape[axis + 1:]
    return jnp.moveaxis(full.reshape(split), axis, 0).reshape(N_DEV, -1)


def _from_chunks(chunks, shape, axis):
    sh = _shard_shape(shape, axis)
    t = chunks.reshape((N_DEV,) + sh)
    t = jnp.moveaxis(t, 0, axis)
    return t.reshape(shape)


def _split_flat(flat2d, table):
    out, off = [], 0
    for size in table:
        out.append(flat2d[:, off:off + size])
        off += size
    return out


def _win0_to_padded(w):
    z = lambda n: jnp.zeros((w.shape[0], n), w.dtype)
    return jnp.concatenate([w[:, 0:640], z(KR_LANE), w[:, 640:672],
                            z(LANES - KR_LANE - QK_ROPE), w[:, 672:1952]], axis=1)


def _win0_from_padded(wp):
    k0 = ZA_KR + KR_LANE
    return jnp.concatenate([wp[:, 0:640], wp[:, k0:k0 + QK_ROPE], wp[:, ZA_W:ZP]], axis=1)


def _pad_heads(w, per_head, lo, hi):
    t = w.reshape(w.shape[0], N_TOK_HEADS, per_head)[:, :, lo:hi]
    t = jnp.pad(t, ((0, 0), (0, 0), (0, HEAD_PAD - (hi - lo))))
    return t.reshape(w.shape[0], QKV_PAD)


def _unpad_heads(wp, width):
    return wp.reshape(wp.shape[0], N_TOK_HEADS, HEAD_PAD)[:, :, :width]


def _block_diag(w):
    eye = jnp.eye(N_TOK_HEADS, dtype=w.dtype)
    return (w[:, :, None, :] * eye[:, None, :, None]).reshape(TOK_WIDTH, TOK_WIDTH)


def _diag_blocks(d):
    t = d.reshape(N_TOK_HEADS, HEAD_DIM, N_TOK_HEADS, HEAD_DIM)
    return jnp.stack([t[g, :, g, :] for g in range(N_TOK_HEADS)])


def _rope_tables(positions):
    half = QK_ROPE // 2
    inv_freq = ROPE_THETA ** (-jnp.arange(half, dtype=F32) / half)
    ang = positions.astype(F32)[:, None] * inv_freq
    cos, sin = jnp.cos(ang), jnp.sin(ang)
    s = positions.shape[0]
    tail = jnp.zeros((s, HEAD_PAD - QK_DIM), F32)
    c = jnp.concatenate([jnp.ones((s, QK_NOPE), F32), cos, cos, tail], axis=1)
    sn = jnp.concatenate([jnp.zeros((s, QK_NOPE), F32), sin, sin, tail], axis=1)
    return c, sn


def _local_step(x, mem, positions, tgt, wts, ts, tatt, early_exchange):
    bf = lambda t: t.astype(BF16)
    win0 = _win0_to_padded(wts["mla_w_in"])
    wuq = _pad_heads(wts["mla_w_uq"], QK_DIM, 0, QK_DIM)
    wukv = jnp.concatenate([_pad_heads(wts["mla_w_ukv"], 2 * HEAD_DIM, 0, QK_NOPE),
                            _pad_heads(wts["mla_w_ukv"], 2 * HEAD_DIM, QK_NOPE, 2 * HEAD_DIM)],
                           axis=1)
    win1 = wts["lru_w_in"]
    wmkv, wout = wts["w_mem_kv"], wts["w_out"]
    gq = wts["mla_q_norm"].reshape(1, Q_LORA)
    gkv = wts["mla_kv_norm"].reshape(1, KV_LORA)
    ln_g, ln_b = wts["ln_g"], wts["ln_b"]
    wr, wi = bf(_block_diag(wts["lru_w_rgate"])), bf(_block_diag(wts["lru_w_igate"]))
    cw8 = jnp.pad(wts["lru_conv_w"], ((0, SUBLANES - CONV_W), (0, 0)))
    vec8 = jnp.pad(jnp.stack([wts["lru_conv_b"], wts["lru_b_rgate"], wts["lru_b_igate"],
                              wts["lru_lambda"]]), ((0, SUBLANES - 4), (0, 0)))
    tabs = _rope_tables(positions)
    tmem = mem.shape[0]

    za0, zg0, q, k, v = _mla_prep_fwd(x, win0, tabs, gq, gkv, wuq, wukv, ts)
    o, lse = _flash_fwd(q, k, v, tatt, FWD_HEADS)
    mkv0, = _rowmm(mem, wmkv[0], [2 * MEM_WIDTH], "mem_kv0", tmem)
    cat0, y0, pre0, h1 = _mix_out_fwd(o, zg0, mkv0, wout[0], x, ln_g[0:1], ln_b[0:1], None,
                                      0, MIX_WIDTH, True, "mix_out_fwd0", ts)
    del o
    u1, zg1, hs = _lru_fwd(h1, win1, cw8, vec8, wr, wi, ts)
    mkv1, = _rowmm(mem, wmkv[1], [2 * MEM_WIDTH], "mem_kv1", tmem)
    cat1, y1, dpre1, dgb1, loss8 = _mix_out_fwd(hs, zg1, mkv1, wout[1], h1, ln_g[1:2],
                                                ln_b[1:2], tgt, 0, MIX_WIDTH, False,
                                                "mix_out_loss", ts)
    loss = loss8[0, 0]

    dzg1, dhs, dmkv1, dwout1 = _gate_mem_bwd(dpre1, y1, wout[1].T, cat1, zg1, mkv1, None,
                                             0, MIX_WIDTH, "gate_mem_bwd1", ts)
    du, dwr, dwi, dvec = _lru_bwd(u1, dhs, hs, cw8, vec8, wr, wi, wr.T, wi.T, ts)
    dpre0, dwin1, dgb0 = _linear_bwd(h1, [du, dzg1], [0, ZA_W], win1.T, dpre1,
                                     (pre0, ln_g[0:1]), "in_proj_bwd1", ts)
    dwmkv1 = _wgrad_small(mem, dmkv1, "mem_kv_bwd1")
    dzg0, do, dmkv0, dwout0, stats = _gate_mem_bwd(dpre0, y0, wout[0].T, cat0, zg0, mkv0, lse,
                                                   0, MIX_WIDTH, "gate_mem_bwd0", ts)
    dwmkv0 = _wgrad_small(mem, dmkv0, "mem_kv_bwd0")
    early = {
        "lru_w_in": dwin1,
        "lru_small": dvec,
        "lru_w_rgate": _diag_blocks(dwr).reshape(TOK_WIDTH, HEAD_DIM),
        "lru_w_igate": _diag_blocks(dwi).reshape(TOK_WIDTH, HEAD_DIM),
        "w_mem_kv": [dwmkv0, dwmkv1],
        "w_out": [dwout0, dwout1],
    }
    (dq, dk, dv), got_early = _flash_bwd(q, k, v, stats, do, tatt, BWD_HEADS,
                                         early_exchange(early))
    dza, dzk, dwuq_p, dwukv_p, dg = _mla_prep_bwd(za0, dq, dk, dv, tabs, gq, gkv,
                                                  wuq.T, wukv.T, ts)
    gx, dwin0_p = _linear_bwd(x, [dza, dzk, dzg0], [ZA_CQ, ZA_KR, ZA_W], win0.T, dpre0,
                              None, "in_proj_bwd0", ts)

    dwukv = jnp.concatenate([_unpad_heads(dwukv_p[:, :QKV_PAD], HEAD_DIM),
                             _unpad_heads(dwukv_p[:, QKV_PAD:], HEAD_DIM)], axis=2)
    zrow = jnp.zeros((1, D_MODEL), F32)
    gains = jnp.pad(dg[0:1], ((0, 0), (0, D_MODEL - Q_LORA - KV_LORA)))
    small_repl = jnp.concatenate([dgb0[0:2], dgb1[0:2], gains,
                                  loss * jnp.ones((1, D_MODEL), F32), zrow, zrow], axis=0)
    late = {
        "mla_w_in": _win0_from_padded(dwin0_p),
        "mla_w_uq": _unpad_heads(dwuq_p, QK_DIM).reshape(Q_LORA, N_TOK_HEADS * QK_DIM),
        "mla_w_ukv": dwukv.reshape(KV_LORA, N_TOK_HEADS * 2 * HEAD_DIM),
        "small_repl": small_repl,
    }
    return gx, early, got_early, late


WEIGHT_ORDER = ["mla_w_in", "mla_q_norm", "mla_w_uq", "mla_kv_norm", "mla_w_ukv", "lru_w_in",
                "lru_conv_w", "lru_conv_b", "lru_w_rgate", "lru_b_rgate", "lru_w_igate",
                "lru_b_igate", "lru_lambda", "w_mem_kv", "w_out", "ln_g", "ln_b"]


def kernel(x, mem, positions, mla_w_in, mla_q_norm, mla_w_uq, mla_kv_norm, mla_w_ukv, lru_w_in, lru_conv_w, lru_conv_b, lru_w_rgate, lru_b_rgate, lru_w_igate, lru_b_igate, lru_lambda, w_mem_kv, w_out, ln_g, ln_b, loss_target, m_mla_w_in, m_mla_q_norm, m_mla_w_uq, m_mla_kv_norm, m_mla_w_ukv, m_lru_w_in, m_lru_conv_w, m_lru_conv_b, m_lru_w_rgate, m_lru_b_rgate, m_lru_w_igate, m_lru_b_igate, m_lru_lambda, m_w_mem_kv, m_w_out, m_ln_g, m_ln_b, v_mla_w_in, v_mla_q_norm, v_mla_w_uq, v_mla_kv_norm, v_mla_w_ukv, v_lru_w_in, v_lru_conv_w, v_lru_conv_b, v_lru_w_rgate, v_lru_b_rgate, v_lru_w_igate, v_lru_b_igate, v_lru_lambda, v_w_mem_kv, v_w_out, v_ln_g, v_ln_b):
    w_in = dict(mla_w_in=mla_w_in, mla_q_norm=mla_q_norm, mla_w_uq=mla_w_uq,
                mla_kv_norm=mla_kv_norm, mla_w_ukv=mla_w_ukv, lru_w_in=lru_w_in,
                lru_conv_w=lru_conv_w, lru_conv_b=lru_conv_b, lru_w_rgate=lru_w_rgate,
                lru_b_rgate=lru_b_rgate, lru_w_igate=lru_w_igate, lru_b_igate=lru_b_igate,
                lru_lambda=lru_lambda, w_mem_kv=w_mem_kv, w_out=w_out, ln_g=ln_g, ln_b=ln_b)
    m_in = dict(mla_w_in=m_mla_w_in, mla_q_norm=m_mla_q_norm, mla_w_uq=m_mla_w_uq,
                mla_kv_norm=m_mla_kv_norm, mla_w_ukv=m_mla_w_ukv, lru_w_in=m_lru_w_in,
                lru_conv_w=m_lru_conv_w, lru_conv_b=m_lru_conv_b, lru_w_rgate=m_lru_w_rgate,
                lru_b_rgate=m_lru_b_rgate, lru_w_igate=m_lru_w_igate, lru_b_igate=m_lru_b_igate,
                lru_lambda=m_lru_lambda, w_mem_kv=m_w_mem_kv, w_out=m_w_out, ln_g=m_ln_g,
                ln_b=m_ln_b)
    v_in = dict(mla_w_in=v_mla_w_in, mla_q_norm=v_mla_q_norm, mla_w_uq=v_mla_w_uq,
                mla_kv_norm=v_mla_kv_norm, mla_w_ukv=v_mla_w_ukv, lru_w_in=v_lru_w_in,
                lru_conv_w=v_lru_conv_w, lru_conv_b=v_lru_conv_b, lru_w_rgate=v_lru_w_rgate,
                lru_b_rgate=v_lru_b_rgate, lru_w_igate=v_lru_w_igate, lru_b_igate=v_lru_b_igate,
                lru_lambda=v_lru_lambda, w_mem_kv=v_w_mem_kv, w_out=v_w_out, ln_g=v_ln_g,
                ln_b=v_ln_b)
    s = x.shape[1]
    ts = min(ROW_BLOCK, s)
    tatt = min(ATT_BLOCK, s)
    big_sizes = [_size(sh) // N_DEV for _, sh, _ in BIG]
    small_sizes = [_size(sh) // N_DEV for _, sh, _ in SMALL]

    own = BIG[:N_OWN_SHAPE]
    packed = BIG[N_OWN_SHAPE:]
    own_local = [w_in[n][0].astype(BF16) for n, _, _ in own]
    packed_rows = sum(big_sizes[N_OWN_SHAPE:]) // LANES
    big_local = _pack_rows([w_in[n] for n, _, _ in packed], packed_rows).astype(BF16)
    small_local = _pack_rows([w_in[n] for n, _, _ in SMALL], SMALL_ROWS)
    *own_all, big_all, small_all = _allgather(own_local + [big_local, small_local])
    wts = {}
    for (n, sh, ax), g in zip(own, own_all):
        wts[n] = jnp.moveaxis(g, 0, 1).reshape(sh)
    for (n, sh, ax), part in zip(packed, _split_flat(big_all.reshape(N_DEV, -1),
                                                     big_sizes[N_OWN_SHAPE:])):
        wts[n] = _from_chunks(part, sh, ax)
    for (n, sh, ax), part in zip(SMALL, _split_flat(small_all.reshape(N_DEV, -1), small_sizes)):
        wts[n] = _from_chunks(part, sh, ax)
    for n, sh in REPL:
        wts[n] = w_in[n].reshape(sh)

    def early_exchange(g):
        small_chunks = jnp.moveaxis(g["lru_small"].reshape(SUBLANES, N_DEV, -1), 1, 0)
        sends = [(g["lru_w_in"], "cols"),
                 (g["w_mem_kv"][0], "rows"), (g["w_mem_kv"][1], "rows"),
                 (g["w_out"][0], "rows"), (g["w_out"][1], "rows"),
                 (small_chunks, "chunks"), (g["lru_w_rgate"], "all"), (g["lru_w_igate"], "all")]
        return _Exchange([a for a, _ in sends], [k for _, k in sends])

    gx, _, got_early, late = _local_step(x[0], mem[0], positions[0], loss_target[0], wts,
                                         ts, tatt, early_exchange)

    def chunked(name, shape):
        w = shape[1] // N_DEV
        return _to_chunks(late[name], 1).reshape(N_DEV, shape[0], w).astype(BF16)

    got_late = _exchange_grads(
        [chunked("mla_w_in", (D_MODEL, MLA_IN)),
         chunked("mla_w_uq", (Q_LORA, N_TOK_HEADS * QK_DIM)),
         chunked("mla_w_ukv", (KV_LORA, N_TOK_HEADS * 2 * HEAD_DIM)), late["small_repl"]],
        ["chunks", "chunks", "chunks", "all"], "exchange_grads")
    got = list(got_late[:3]) + list(got_early) + [got_late[3]]

    def flat2(d, name):
        t = d[name]
        return t.reshape(-1, t.shape[-1])

    def update(parts, view, name):
        return _adamw(parts, view(w_in), view(m_in), view(v_in), "adamw_" + name)

    res = {}
    for idx, name in [(0, "mla_w_in"), (1, "mla_w_uq"), (2, "mla_w_ukv"), (3, "lru_w_in"),
                      (9, "lru_w_rgate"), (10, "lru_w_igate")]:
        res[name] = update([got[idx]], functools.partial(flat2, name=name), name)
    res["w_mem_kv"] = update([got[4], got[5]], functools.partial(flat2, name="w_mem_kv"),
                             "w_mem_kv")
    res["w_out"] = update([got[6], got[7]], functools.partial(flat2, name="w_out"), "w_out")
    def small(parts, names, layout, name):
        view = lambda d: [flat2(d, n) for n in names]
        tile, outs = _adamw_small(parts, layout, view(w_in), view(m_in), view(v_in), name)
        for n, o in zip(names, outs):
            res[n] = o
        return tile

    taps = list(range(CONV_W))
    small(got[8], ["lru_conv_w", "lru_conv_b", "lru_b_rgate", "lru_b_igate", "lru_lambda"],
          [(taps, 0, TOK_WIDTH // N_DEV)] + [([4 + a], 0, TOK_WIDTH // N_DEV) for a in range(4)],
          "adamw_small_sharded")
    tile = small(got[11], ["ln_g", "ln_b", "mla_q_norm", "mla_kv_norm"],
                 [([0, 2], 0, D_MODEL), ([1, 3], 0, D_MODEL), ([4], 0, Q_LORA),
                  ([4], Q_LORA, KV_LORA)], "adamw_small_replicated")
    loss = tile[5, 0]

    result = [loss, gx.reshape(x.shape)]
    for kind in range(4):
        result += [res[n][kind].reshape(w_in[n].shape) for n in WEIGHT_ORDER]
    return tuple(result)
```
